```python
import jax
import jax.numpy as jnp
from jax import lax
import numpy as np

D_MODEL = 1024
BATCH = 8
SEQ = 4096
DEPTH = 2

GRID_W = 64
CTX_LEN = 256
N_MOD = 9
D_FF = 2816
RMS_EPS = 1e-6
A_HEADS = 8
A_KV_HEADS = 2
A_HEAD_DIM = 64
WINDOW = 128
BLOCK = WINDOW
ROPE_BASE = 10000.0
B_HEADS = 4
B_DK = 64
B_DV = 128
B_GATE_RANK = 16
B_GATE_NORM = 16.0
B_CHUNK = 64
POOL_WINDOWS = (2, 4, 8, 16)
POOL_GROUP = D_MODEL // len(POOL_WINDOWS)
A_Q = A_HEADS * A_HEAD_DIM
A_KV = A_KV_HEADS * A_HEAD_DIM
B_QK = B_HEADS * B_DK
B_V = B_HEADS * B_DV
PROJ_SIZES = (A_Q, A_KV, A_KV, B_QK, B_QK, B_V, B_V, 2 * B_GATE_RANK)
PROJ_DIM = A_Q + 2 * A_KV + 2 * B_QK + 2 * B_V + 2 * B_GATE_RANK
MIX_OUT = A_Q + B_V

kernel_name = "hybrid_swa_gla_pool_prefix_dit"


def rmsnorm(x, g):
    x32 = x.astype(jnp.float32)
    y = x32 * lax.rsqrt(jnp.mean(x32 * x32, axis=-1, keepdims=True) + RMS_EPS)
    return y.astype(x.dtype) * g


def adaln(cond, w, b):
    mm = jax.nn.silu(cond) @ w + b
    mm = mm.reshape(mm.shape[:-1] + (N_MOD, mm.shape[-1] // N_MOD))
    return [mm[..., i, None, :] for i in range(N_MOD)]


def modulate(z, g, shift, scale):
    return rmsnorm(z, g) * (1.0 + scale) + shift


def swiglu(h, wi, wo):
    a, u = jnp.split(h @ wi, 2, axis=-1)
    return (jax.nn.silu(a) * u) @ wo


def _rotate(x, pos):
    n = x.shape[-1] // 2
    freqs = ROPE_BASE ** (-jnp.arange(n, dtype=jnp.float32) / n)
    ang = pos[:, None] * freqs
    cos = jnp.cos(ang)[:, None, :].astype(x.dtype)
    sin = jnp.sin(ang)[:, None, :].astype(x.dtype)
    x1, x2 = x[..., :n], x[..., n:]
    return jnp.concatenate([x1 * cos - x2 * sin, x2 * cos + x1 * sin], axis=-1)


def axial_rope(x, rows, cols):
    half = x.shape[-1] // 2
    return jnp.concatenate([_rotate(x[..., :half], rows), _rotate(x[..., half:], cols)], axis=-1)


def window_attention(q, k, v, kc, vc, sink):
    B, T, Hq, d = q.shape
    G = k.shape[2]
    R = Hq // G
    nb = T // BLOCK
    L = kc.shape[1]
    scale = d ** -0.5
    qb = q.reshape(B, nb, BLOCK, G, R, d)

    def band(a):
        ap = jnp.pad(a, ((0, 0), (BLOCK, BLOCK), (0, 0), (0, 0))).reshape(B, nb + 2, BLOCK, G, d)
        return jnp.concatenate([ap[:, :nb], ap[:, 1:nb + 1], ap[:, 2:]], axis=2)

    kb, vb = band(k), band(v)
    s_band = jnp.einsum('bnigrd,bnjgd->bgrnij', qb, kb).astype(jnp.float32) * scale
    s_ctx = jnp.einsum('bnigrd,blgd->bgrnil', qb, kc).astype(jnp.float32) * scale
    qpos = jnp.arange(nb)[:, None, None] * BLOCK + jnp.arange(BLOCK)[None, :, None]
    kpos = jnp.arange(nb)[:, None, None] * BLOCK - BLOCK + jnp.arange(3 * BLOCK)[None, None, :]
    valid = (kpos >= 0) & (kpos < T) & (jnp.abs(kpos - qpos) <= WINDOW)
    s_band = jnp.where(valid, s_band, -jnp.inf)
    sink_l = jnp.broadcast_to(sink.astype(jnp.float32).reshape(1, G, R, 1, 1, 1), s_band.shape[:-1] + (1,))
    p = jax.nn.softmax(jnp.concatenate([s_band, s_ctx, sink_l], axis=-1), axis=-1)
    nk = 3 * BLOCK
    o = (jnp.einsum('bgrnij,bnjgd->bnigrd', p[..., :nk].astype(vb.dtype), vb)
         + jnp.einsum('bgrnil,blgd->bnigrd', p[..., nk:nk + L].astype(vc.dtype), vc))
    return o.reshape(B, T, Hq * d)


def context_attention(qc, kc, vc, sink):
    B, L, Hq, d = qc.shape
    G = kc.shape[2]
    R = Hq // G
    s = jnp.einsum('blgrd,bmgd->bgrlm', qc.reshape(B, L, G, R, d), kc).astype(jnp.float32) * d ** -0.5
    sink_l = jnp.broadcast_to(sink.astype(jnp.float32).reshape(1, G, R, 1, 1), s.shape[:-1] + (1,))
    p = jax.nn.softmax(jnp.concatenate([s, sink_l], axis=-1), axis=-1)
    o = jnp.einsum('bgrlm,bmgd->blgrd', p[..., :L].astype(vc.dtype), vc)
    return o.reshape(B, L, Hq * d)


def gla_chunked(q, k, v, log_a, s0):
    B, T, H, dk = q.shape
    C = B_CHUNK
    n = T // C
    f32 = jnp.float32

    def chunks(a):
        return a.astype(f32).reshape(B, n, C, H, a.shape[-1])

    qc_ = chunks(q) * dk ** -0.5
    kc_ = chunks(k)
    vc_ = chunks(v)
    g = jnp.cumsum(chunks(log_a), axis=2)
    g_last = g[:, :, -1:]
    q_t = qc_ * jnp.exp(g)
    k_t = kc_ * jnp.exp(-g)
    k_end = kc_ * jnp.exp(g_last - g)
    lower = jnp.tril(jnp.ones((C, C), dtype=bool))
    att = jnp.where(lower, jnp.einsum('bnihd,bnjhd->bnhij', q_t, k_t), 0.0)
    o = jnp.einsum('bnhij,bnjhv->bnihv', att, vc_)
    d_state = jnp.einsum('bnjhd,bnjhv->bnhdv', k_end, vc_)
    decay = jnp.exp(g_last[:, :, 0])

    def step(S, inp):
        dec, ds = inp
        return dec[..., None] * S + ds, S

    s_final, s_prev = lax.scan(step, s0, (jnp.moveaxis(decay, 1, 0), jnp.moveaxis(d_state, 1, 0)))
    s_prev = jnp.moveaxis(s_prev, 0, 1)
    o = o + jnp.einsum('bnihd,bnhdv->bnihv', q_t, s_prev)
    return o.reshape(B, T, H, v.shape[-1]).astype(q.dtype), s_final


def bidir_gla(q, k, v, la_f, la_b, qc, kc, vc, lac_f, lac_b):
    flip = lambda a: a[:, ::-1]
    B = q.shape[0]
    zeros = jnp.zeros((B, B_HEADS, B_DK, B_DV), jnp.float32)
    oc_f, sc_f = gla_chunked(qc, kc, vc, lac_f, zeros)
    oc_b, sc_b = gla_chunked(flip(qc), flip(kc), flip(vc), flip(lac_b), zeros)
    o_f, _ = gla_chunked(q, k, v, la_f, sc_f)
    o_b, _ = gla_chunked(flip(q), flip(k), flip(v), flip(la_b), sc_b)
    return o_f + flip(o_b), oc_f + flip(oc_b)


def gla_output(o, r, gla_g):
    B, T = o.shape[:2]
    return rmsnorm(o, gla_g).reshape(B, T, B_V) * jax.nn.silu(r)


def mixer_ab(h, hc, rows, cols, need_ctx_out, w_in, w_a2_f, b_a_f, w_a2_b, b_a_b, sink, gla_g, w_out):
    split_points = [int(s) for s in np.cumsum(PROJ_SIZES)[:-1]]

    def project(z):
        Bz, Tz = z.shape[:2]
        qa, ka, va, qb, kb, vb, rb, zg = jnp.split(z @ w_in, split_points, axis=-1)
        la_f = jax.nn.log_sigmoid((zg[..., :B_GATE_RANK] @ w_a2_f + b_a_f).astype(jnp.float32)) / B_GATE_NORM
        la_b = jax.nn.log_sigmoid((zg[..., B_GATE_RANK:] @ w_a2_b + b_a_b).astype(jnp.float32)) / B_GATE_NORM
        return (qa.reshape(Bz, Tz, A_HEADS, A_HEAD_DIM),
                ka.reshape(Bz, Tz, A_KV_HEADS, A_HEAD_DIM),
                va.reshape(Bz, Tz, A_KV_HEADS, A_HEAD_DIM),
                qb.reshape(Bz, Tz, B_HEADS, B_DK),
                kb.reshape(Bz, Tz, B_HEADS, B_DK),
                vb.reshape(Bz, Tz, B_HEADS, B_DV),
                rb,
                la_f.reshape(Bz, Tz, B_HEADS, B_DK),
                la_b.reshape(Bz, Tz, B_HEADS, B_DK))

    qa, ka, va, qb, kb, vb, rb, la_f, la_b = project(h)
    cqa, cka, cva, cqb, ckb, cvb, crb, cla_f, cla_b = project(hc)
    o_a = window_attention(axial_rope(qa, rows, cols), axial_rope(ka, rows, cols), va, cka, cva, sink)
    o_b, oc_b = bidir_gla(qb, kb, vb, la_f, la_b, cqb, ckb, cvb, cla_f, cla_b)
    y = jnp.concatenate([o_a, gla_output(o_b, rb, gla_g)], axis=-1) @ w_out
    yc = None
    if need_ctx_out:
        oc_a = context_attention(cqa, cka, cva, sink)
        yc = jnp.concatenate([oc_a, gla_output(oc_b, crb, gla_g)], axis=-1) @ w_out
    return y, yc


def pool_mixer(h, w_pool, pool_scale):
    B, T, D = h.shape
    ng = len(POOL_WINDOWS)
    hg = h.astype(jnp.float32).reshape(B, T, ng, POOL_GROUP)
    prefix = jnp.pad(jnp.cumsum(hg, axis=1), ((0, 0), (1, 0), (0, 0), (0, 0)))
    t = jnp.arange(T)
    means = []
    for gi, w in enumerate(POOL_WINDOWS):
        lo = jnp.maximum(t - w // 2, 0)
        hi = jnp.minimum(t + (w - w // 2), T)
        total = prefix[:, hi, gi] - prefix[:, lo, gi]
        means.append(total / (hi - lo).astype(jnp.float32)[:, None])
    pooled = (jnp.stack(means, axis=2) - hg).astype(h.dtype)
    y = jnp.einsum('btgc,gce->btge', pooled, w_pool)
    return y.reshape(B, T, D) * pool_scale


def _fwd_setup_inputs(seed: int = 0) -> dict:
    key = jax.random.key(seed)
    ks = jax.random.split(key, 24)
    D = D_MODEL
    ne = (DEPTH + 1) // 2
    no = DEPTH // 2

    def nrm(k, shape, scale=1.0):
        return jax.random.normal(k, shape, jnp.float32) * scale

    return {
        "x": nrm(ks[0], (BATCH, SEQ, D)),
        "c": nrm(ks[1], (BATCH, D)),
        "ctx": nrm(ks[2], (BATCH, CTX_LEN, D)),
        "c_ctx": nrm(ks[3], (D,)),
        "w_mod": nrm(ks[4], (DEPTH, D, N_MOD * D), 0.5 * D ** -0.5),
        "b_mod": nrm(ks[5], (DEPTH, N_MOD * D), 0.01),
        "norm_g": 1.0 + nrm(ks[6], (DEPTH, 3, D), 0.05),
        "ffn1_wi": nrm(ks[7], (DEPTH, D, 2 * D_FF), D ** -0.5),
        "ffn1_wo": nrm(ks[8], (DEPTH, D_FF, D), D_FF ** -0.5),
        "ffn2_wi": nrm(ks[9], (DEPTH, D, 2 * D_FF), D ** -0.5),
        "ffn2_wo": nrm(ks[10], (DEPTH, D_FF, D), D_FF ** -0.5),
        "w_in": nrm(ks[11], (ne, D, PROJ_DIM), D ** -0.5),
        "w_a2_f": nrm(ks[12], (ne, B_GATE_RANK, B_QK), B_GATE_RANK ** -0.5),
        "b_a_f": nrm(ks[13], (ne, B_QK), 0.1),
        "w_a2_b": nrm(ks[14], (ne, B_GATE_RANK, B_QK), B_GATE_RANK ** -0.5),
        "b_a_b": nrm(ks[15], (ne, B_QK), 0.1),
        "sink": nrm(ks[16], (ne, A_HEADS), 1.0),
        "gla_g": 1.0 + nrm(ks[17], (ne, B_DV), 0.05),
        "w_out": nrm(ks[18], (ne, MIX_OUT, D), MIX_OUT ** -0.5),
        "w_pool": nrm(ks[19], (no, len(POOL_WINDOWS), POOL_GROUP, POOL_GROUP), POOL_GROUP ** -0.5),
        "pool_scale": 1.0 + nrm(ks[20], (no, D), 0.1),
        "final_g": 1.0 + nrm(ks[21], (D,), 0.05),
    }


def _fwd_reference(x, c, ctx, c_ctx, w_mod, b_mod, norm_g, ffn1_wi, ffn1_wo, ffn2_wi, ffn2_wo,
              w_in, w_a2_f, b_a_f, w_a2_b, b_a_b, sink, gla_g, w_out, w_pool, pool_scale, final_g):
    T = x.shape[1]
    ROWS = T // GRID_W
    rows = jnp.repeat(jnp.arange(ROWS, dtype=jnp.float32), GRID_W)
    cols = jnp.tile(jnp.arange(GRID_W, dtype=jnp.float32), ROWS)

    for l in range(DEPTH):
        even = l % 2 == 0
        ctx_out = any(j % 2 == 0 for j in range(l + 1, DEPTH))
        ctx_in = even or ctx_out
        m = adaln(c, w_mod[l], b_mod[l])
        mc = adaln(c_ctx, w_mod[l], b_mod[l]) if ctx_in else None

        x = x + 0.5 * m[2] * swiglu(modulate(x, norm_g[l, 0], m[0], m[1]), ffn1_wi[l], ffn1_wo[l])
        if ctx_in:
            ctx = ctx + 0.5 * mc[2] * swiglu(modulate(ctx, norm_g[l, 0], mc[0], mc[1]), ffn1_wi[l], ffn1_wo[l])

        h = modulate(x, norm_g[l, 1], m[3], m[4])
        if even:
            e = l // 2
            hc = modulate(ctx, norm_g[l, 1], mc[3], mc[4])
            y, yc = mixer_ab(h, hc, rows, cols, ctx_out, w_in[e], w_a2_f[e], b_a_f[e], w_a2_b[e], b_a_b[e],
                             sink[e], gla_g[e], w_out[e])
        else:
            o = l // 2
            y = pool_mixer(h, w_pool[o], pool_scale[o])
            yc = pool_mixer(modulate(ctx, norm_g[l, 1], mc[3], mc[4]), w_pool[o], pool_scale[o]) if ctx_out else None
        x = x + m[5] * y
        if ctx_out:
            ctx = ctx + mc[5] * yc

        x = x + 0.5 * m[8] * swiglu(modulate(x, norm_g[l, 2], m[6], m[7]), ffn2_wi[l], ffn2_wo[l])
        if ctx_out:
            ctx = ctx + 0.5 * mc[8] * swiglu(modulate(ctx, norm_g[l, 2], mc[6], mc[7]), ffn2_wi[l], ffn2_wo[l])

    return rmsnorm(x, final_g)


import jax as _jax
import jax.numpy as _jnp

TWIN_FORMAT = 'train_step'
FWD_PARAMS = ['x', 'c', 'ctx', 'c_ctx', 'w_mod', 'b_mod', 'norm_g', 'ffn1_wi', 'ffn1_wo', 'ffn2_wi', 'ffn2_wo', 'w_in', 'w_a2_f', 'b_a_f', 'w_a2_b', 'b_a_b', 'sink', 'gla_g', 'w_out', 'w_pool', 'pool_scale', 'final_g']
TWIN_WEIGHTS = ['c_ctx', 'w_mod', 'b_mod', 'norm_g', 'ffn1_wi', 'ffn1_wo', 'ffn2_wi', 'ffn2_wo', 'w_in', 'w_a2_f', 'b_a_f', 'w_a2_b', 'b_a_b', 'sink', 'gla_g', 'w_out', 'w_pool', 'pool_scale', 'final_g']
TWIN_DIFF_INPUT = 'x'
TWIN_INPUTS = ['x', 'c', 'ctx', 'c_ctx', 'w_mod', 'b_mod', 'norm_g', 'ffn1_wi', 'ffn1_wo', 'ffn2_wi', 'ffn2_wo', 'w_in', 'w_a2_f', 'b_a_f', 'w_a2_b', 'b_a_b', 'sink', 'gla_g', 'w_out', 'w_pool', 'pool_scale', 'final_g', 'loss_target', 'm_c_ctx', 'm_w_mod', 'm_b_mod', 'm_norm_g', 'm_ffn1_wi', 'm_ffn1_wo', 'm_ffn2_wi', 'm_ffn2_wo', 'm_w_in', 'm_w_a2_f', 'm_b_a_f', 'm_w_a2_b', 'm_b_a_b', 'm_sink', 'm_gla_g', 'm_w_out', 'm_w_pool', 'm_pool_scale', 'm_final_g', 'v_c_ctx', 'v_w_mod', 'v_b_mod', 'v_norm_g', 'v_ffn1_wi', 'v_ffn1_wo', 'v_ffn2_wi', 'v_ffn2_wo', 'v_w_in', 'v_w_a2_f', 'v_b_a_f', 'v_w_a2_b', 'v_b_a_b', 'v_sink', 'v_gla_g', 'v_w_out', 'v_w_pool', 'v_pool_scale', 'v_final_g']
TWIN_OUTPUTS = ['loss', 'grad_x', 'grad_c_ctx', 'grad_w_mod', 'grad_b_mod', 'grad_norm_g', 'grad_ffn1_wi', 'grad_ffn1_wo', 'grad_ffn2_wi', 'grad_ffn2_wo', 'grad_w_in', 'grad_w_a2_f', 'grad_b_a_f', 'grad_w_a2_b', 'grad_b_a_b', 'grad_sink', 'grad_gla_g', 'grad_w_out', 'grad_w_pool', 'grad_pool_scale', 'grad_final_g', 'delta_c_ctx', 'delta_w_mod', 'delta_b_mod', 'delta_norm_g', 'delta_ffn1_wi', 'delta_ffn1_wo', 'delta_ffn2_wi', 'delta_ffn2_wo', 'delta_w_in', 'delta_w_a2_f', 'delta_b_a_f', 'delta_w_a2_b', 'delta_b_a_b', 'delta_sink', 'delta_gla_g', 'delta_w_out', 'delta_w_pool', 'delta_pool_scale', 'delta_final_g', 'new_m_c_ctx', 'new_m_w_mod', 'new_m_b_mod', 'new_m_norm_g', 'new_m_ffn1_wi', 'new_m_ffn1_wo', 'new_m_ffn2_wi', 'new_m_ffn2_wo', 'new_m_w_in', 'new_m_w_a2_f', 'new_m_b_a_f', 'new_m_w_a2_b', 'new_m_b_a_b', 'new_m_sink', 'new_m_gla_g', 'new_m_w_out', 'new_m_w_pool', 'new_m_pool_scale', 'new_m_final_g', 'new_v_c_ctx', 'new_v_w_mod', 'new_v_b_mod', 'new_v_norm_g', 'new_v_ffn1_wi', 'new_v_ffn1_wo', 'new_v_ffn2_wi', 'new_v_ffn2_wo', 'new_v_w_in', 'new_v_w_a2_f', 'new_v_b_a_f', 'new_v_w_a2_b', 'new_v_b_a_b', 'new_v_sink', 'new_v_gla_g', 'new_v_w_out', 'new_v_w_pool', 'new_v_pool_scale', 'new_v_final_g']
TWIN_LEAF_KINDS = {'loss': 'loss', 'grad_x': 'grad_x', 'grad_c_ctx': 'grad_w', 'grad_w_mod': 'grad_w', 'grad_b_mod': 'grad_w', 'grad_norm_g': 'grad_w', 'grad_ffn1_wi': 'grad_w', 'grad_ffn1_wo': 'grad_w', 'grad_ffn2_wi': 'grad_w', 'grad_ffn2_wo': 'grad_w', 'grad_w_in': 'grad_w', 'grad_w_a2_f': 'grad_w', 'grad_b_a_f': 'grad_w', 'grad_w_a2_b': 'grad_w', 'grad_b_a_b': 'grad_w', 'grad_sink': 'grad_w', 'grad_gla_g': 'grad_w', 'grad_w_out': 'grad_w', 'grad_w_pool': 'grad_w', 'grad_pool_scale': 'grad_w', 'grad_final_g': 'grad_w', 'delta_c_ctx': 'delta_w', 'delta_w_mod': 'delta_w', 'delta_b_mod': 'delta_w', 'delta_norm_g': 'delta_w', 'delta_ffn1_wi': 'delta_w', 'delta_ffn1_wo': 'delta_w', 'delta_ffn2_wi': 'delta_w', 'delta_ffn2_wo': 'delta_w', 'delta_w_in': 'delta_w', 'delta_w_a2_f': 'delta_w', 'delta_b_a_f': 'delta_w', 'delta_w_a2_b': 'delta_w', 'delta_b_a_b': 'delta_w', 'delta_sink': 'delta_w', 'delta_gla_g': 'delta_w', 'delta_w_out': 'delta_w', 'delta_w_pool': 'delta_w', 'delta_pool_scale': 'delta_w', 'delta_final_g': 'delta_w', 'new_m_c_ctx': 'new_m', 'new_m_w_mod': 'new_m', 'new_m_b_mod': 'new_m', 'new_m_norm_g': 'new_m', 'new_m_ffn1_wi': 'new_m', 'new_m_ffn1_wo': 'new_m', 'new_m_ffn2_wi': 'new_m', 'new_m_ffn2_wo': 'new_m', 'new_m_w_in': 'new_m', 'new_m_w_a2_f': 'new_m', 'new_m_b_a_f': 'new_m', 'new_m_w_a2_b': 'new_m', 'new_m_b_a_b': 'new_m', 'new_m_sink': 'new_m', 'new_m_gla_g': 'new_m', 'new_m_w_out': 'new_m', 'new_m_w_pool': 'new_m', 'new_m_pool_scale': 'new_m', 'new_m_final_g': 'new_m', 'new_v_c_ctx': 'new_v', 'new_v_w_mod': 'new_v', 'new_v_b_mod': 'new_v', 'new_v_norm_g': 'new_v', 'new_v_ffn1_wi': 'new_v', 'new_v_ffn1_wo': 'new_v', 'new_v_ffn2_wi': 'new_v', 'new_v_ffn2_wo': 'new_v', 'new_v_w_in': 'new_v', 'new_v_w_a2_f': 'new_v', 'new_v_b_a_f': 'new_v', 'new_v_w_a2_b': 'new_v', 'new_v_b_a_b': 'new_v', 'new_v_sink': 'new_v', 'new_v_gla_g': 'new_v', 'new_v_w_out': 'new_v', 'new_v_w_pool': 'new_v', 'new_v_pool_scale': 'new_v', 'new_v_final_g': 'new_v'}


def _forward(args):
    return _fwd_reference(*[args[k] for k in FWD_PARAMS])


def _output_shape():
    def fwd():
        inp = _fwd_setup_inputs(0)
        return _fwd_reference(*[inp[k] for k in FWD_PARAMS])
    out = _jax.eval_shape(fwd)
    return out.shape, out.dtype

N_MICROBATCH = 1
ADAM_LR = 0.001
ADAM_B1 = 0.9
ADAM_B2 = 0.999
ADAM_EPS = 1e-08
ADAM_WD = 0.01
ADAM_STEP = 10
PER_EXAMPLE_BATCH_AXIS = {'x': 0, 'c': 0, 'ctx': 0, 'loss_target': 0}
SHARED_INPUTS = []
_WEIGHT_DTYPES = {'c_ctx': _jnp.float32, 'w_mod': _jnp.float32, 'b_mod': _jnp.float32, 'norm_g': _jnp.float32, 'ffn1_wi': _jnp.float32, 'ffn1_wo': _jnp.float32, 'ffn2_wi': _jnp.float32, 'ffn2_wo': _jnp.float32, 'w_in': _jnp.float32, 'w_a2_f': _jnp.float32, 'b_a_f': _jnp.float32, 'w_a2_b': _jnp.float32, 'b_a_b': _jnp.float32, 'sink': _jnp.float32, 'gla_g': _jnp.float32, 'w_out': _jnp.float32, 'w_pool': _jnp.float32, 'pool_scale': _jnp.float32, 'final_g': _jnp.float32}
MOMENT_SCALE = {'c_ctx': 7.749034e-03, 'w_mod': 4.863641e-02, 'b_mod': 9.975535e-02, 'norm_g': 3.637726e-02, 'ffn1_wi': 1.200454e-02, 'ffn1_wo': 1.961623e-02, 'ffn2_wi': 1.135358e-02, 'ffn2_wo': 1.849467e-02, 'w_in': 3.540389e-02, 'w_a2_f': 5.439169e-03, 'b_a_f': 1.650284e-02, 'w_a2_b': 5.974633e-03, 'b_a_b': 1.670049e-02, 'sink': 2.054434e-04, 'gla_g': 7.277915e-02, 'w_out': 2.681082e-02, 'w_pool': 5.070413e-02, 'pool_scale': 2.436110e-01, 'final_g': 3.214608e+01}


def _to_microbatches(a, axis):
    t = _jnp.moveaxis(a, axis, 0)
    t = t.reshape((N_MICROBATCH, t.shape[0] // N_MICROBATCH) + t.shape[1:])
    return _jnp.moveaxis(t, 1, axis + 1)


def setup_inputs(seed: int = 0) -> dict:
    inp = _fwd_setup_inputs(seed)
    key = _jax.random.fold_in(_jax.random.key(seed), 7919)
    shape, _ = _output_shape()
    out = dict(inp)
    out["loss_target"] = _jax.random.normal(_jax.random.fold_in(key, 0), shape, _jnp.float32)
    for i, name in enumerate(TWIN_WEIGHTS):
        w = inp[name].astype(_jnp.float32)
        if MOMENT_SCALE is None:
            s = _jnp.sqrt(_jnp.mean(_jnp.square(w)) + 1e-30)
        else:
            s = MOMENT_SCALE[name]
        km, kv = _jax.random.split(_jax.random.fold_in(key, i + 1))
        out[name] = w
        out["m_" + name] = s * _jax.random.normal(km, w.shape, _jnp.float32)
        out["v_" + name] = (s * s) * _jax.random.uniform(kv, w.shape, _jnp.float32, 0.5, 1.5)
    if N_MICROBATCH > 1:
        for name, axis in PER_EXAMPLE_BATCH_AXIS.items():
            out[name] = _to_microbatches(out[name], axis)
    return {'x': out['x'], 'c': out['c'], 'ctx': out['ctx'], 'c_ctx': out['c_ctx'], 'w_mod': out['w_mod'], 'b_mod': out['b_mod'], 'norm_g': out['norm_g'], 'ffn1_wi': out['ffn1_wi'], 'ffn1_wo': out['ffn1_wo'], 'ffn2_wi': out['ffn2_wi'], 'ffn2_wo': out['ffn2_wo'], 'w_in': out['w_in'], 'w_a2_f': out['w_a2_f'], 'b_a_f': out['b_a_f'], 'w_a2_b': out['w_a2_b'], 'b_a_b': out['b_a_b'], 'sink': out['sink'], 'gla_g': out['gla_g'], 'w_out': out['w_out'], 'w_pool': out['w_pool'], 'pool_scale': out['pool_scale'], 'final_g': out['final_g'], 'loss_target': out['loss_target'], 'm_c_ctx': out['m_c_ctx'], 'm_w_mod': out['m_w_mod'], 'm_b_mod': out['m_b_mod'], 'm_norm_g': out['m_norm_g'], 'm_ffn1_wi': out['m_ffn1_wi'], 'm_ffn1_wo': out['m_ffn1_wo'], 'm_ffn2_wi': out['m_ffn2_wi'], 'm_ffn2_wo': out['m_ffn2_wo'], 'm_w_in': out['m_w_in'], 'm_w_a2_f': out['m_w_a2_f'], 'm_b_a_f': out['m_b_a_f'], 'm_w_a2_b': out['m_w_a2_b'], 'm_b_a_b': out['m_b_a_b'], 'm_sink': out['m_sink'], 'm_gla_g': out['m_gla_g'], 'm_w_out': out['m_w_out'], 'm_w_pool': out['m_w_pool'], 'm_pool_scale': out['m_pool_scale'], 'm_final_g': out['m_final_g'], 'v_c_ctx': out['v_c_ctx'], 'v_w_mod': out['v_w_mod'], 'v_b_mod': out['v_b_mod'], 'v_norm_g': out['v_norm_g'], 'v_ffn1_wi': out['v_ffn1_wi'], 'v_ffn1_wo': out['v_ffn1_wo'], 'v_ffn2_wi': out['v_ffn2_wi'], 'v_ffn2_wo': out['v_ffn2_wo'], 'v_w_in': out['v_w_in'], 'v_w_a2_f': out['v_w_a2_f'], 'v_b_a_f': out['v_b_a_f'], 'v_w_a2_b': out['v_w_a2_b'], 'v_b_a_b': out['v_b_a_b'], 'v_sink': out['v_sink'], 'v_gla_g': out['v_gla_g'], 'v_w_out': out['v_w_out'], 'v_w_pool': out['v_w_pool'], 'v_pool_scale': out['v_pool_scale'], 'v_final_g': out['v_final_g']}


def _loss(weights, diff, rest, loss_target):
    with _jax.named_scope("forward"):
        args = {**rest, TWIN_DIFF_INPUT: diff, **{k: w.astype(_WEIGHT_DTYPES[k]) for k, w in weights.items()}}
        y = _forward(args)
    with _jax.named_scope("loss_head"):
        err = _jnp.square(y.astype(_jnp.float32) - loss_target)
        return 0.5 * _jnp.sum(_jnp.mean(err, axis=-1)) if err.ndim else 0.5 * err


def _adamw(w, g, m, v):
    m = ADAM_B1 * m + (1.0 - ADAM_B1) * g
    v = ADAM_B2 * v + (1.0 - ADAM_B2) * _jnp.square(g)
    m_hat = m / (1.0 - ADAM_B1 ** ADAM_STEP)
    v_hat = v / (1.0 - ADAM_B2 ** ADAM_STEP)
    delta = -ADAM_LR * (m_hat / (_jnp.sqrt(v_hat) + ADAM_EPS) + ADAM_WD * w)
    return delta, m, v


def reference(x, c, ctx, c_ctx, w_mod, b_mod, norm_g, ffn1_wi, ffn1_wo, ffn2_wi, ffn2_wo, w_in, w_a2_f, b_a_f, w_a2_b, b_a_b, sink, gla_g, w_out, w_pool, pool_scale, final_g, loss_target, m_c_ctx, m_w_mod, m_b_mod, m_norm_g, m_ffn1_wi, m_ffn1_wo, m_ffn2_wi, m_ffn2_wo, m_w_in, m_w_a2_f, m_b_a_f, m_w_a2_b, m_b_a_b, m_sink, m_gla_g, m_w_out, m_w_pool, m_pool_scale, m_final_g, v_c_ctx, v_w_mod, v_b_mod, v_norm_g, v_ffn1_wi, v_ffn1_wo, v_ffn2_wi, v_ffn2_wo, v_w_in, v_w_a2_f, v_b_a_f, v_w_a2_b, v_b_a_b, v_sink, v_gla_g, v_w_out, v_w_pool, v_pool_scale, v_final_g):
    given = dict(x=x, c=c, ctx=ctx, c_ctx=c_ctx, w_mod=w_mod, b_mod=b_mod, norm_g=norm_g, ffn1_wi=ffn1_wi, ffn1_wo=ffn1_wo, ffn2_wi=ffn2_wi, ffn2_wo=ffn2_wo, w_in=w_in, w_a2_f=w_a2_f, b_a_f=b_a_f, w_a2_b=w_a2_b, b_a_b=b_a_b, sink=sink, gla_g=gla_g, w_out=w_out, w_pool=w_pool, pool_scale=pool_scale, final_g=final_g, loss_target=loss_target, m_c_ctx=m_c_ctx, m_w_mod=m_w_mod, m_b_mod=m_b_mod, m_norm_g=m_norm_g, m_ffn1_wi=m_ffn1_wi, m_ffn1_wo=m_ffn1_wo, m_ffn2_wi=m_ffn2_wi, m_ffn2_wo=m_ffn2_wo, m_w_in=m_w_in, m_w_a2_f=m_w_a2_f, m_b_a_f=m_b_a_f, m_w_a2_b=m_w_a2_b, m_b_a_b=m_b_a_b, m_sink=m_sink, m_gla_g=m_gla_g, m_w_out=m_w_out, m_w_pool=m_w_pool, m_pool_scale=m_pool_scale, m_final_g=m_final_g, v_c_ctx=v_c_ctx, v_w_mod=v_w_mod, v_b_mod=v_b_mod, v_norm_g=v_norm_g, v_ffn1_wi=v_ffn1_wi, v_ffn1_wo=v_ffn1_wo, v_ffn2_wi=v_ffn2_wi, v_ffn2_wo=v_ffn2_wo, v_w_in=v_w_in, v_w_a2_f=v_w_a2_f, v_b_a_f=v_b_a_f, v_w_a2_b=v_w_a2_b, v_b_a_b=v_b_a_b, v_sink=v_sink, v_gla_g=v_gla_g, v_w_out=v_w_out, v_w_pool=v_w_pool, v_pool_scale=v_pool_scale, v_final_g=v_final_g)
    weights = {n: given[n] for n in TWIN_WEIGHTS}
    shared = {n: given[n] for n in SHARED_INPUTS}
    per_example = {n: given[n] for n in ['x', 'c', 'ctx']}
    grad_fn = _jax.value_and_grad(_loss, argnums=(0, 1))

    def one_microbatch(ex, loss_target):
        ex = dict(ex)
        diff = ex.pop(TWIN_DIFF_INPUT)
        return grad_fn(weights, diff, {**shared, **ex}, loss_target)

    if N_MICROBATCH == 1:
        loss, (grad_w, grad_x) = one_microbatch(per_example, given["loss_target"])
    else:
        def body(carry, xs):
            loss_sum, grad_sum = carry
            l_k, (gw_k, gx_k) = one_microbatch(xs[0], xs[1])
            with _jax.named_scope("update"):
                return (loss_sum + l_k, _jax.tree.map(_jnp.add, grad_sum, gw_k)), gx_k

        init = (_jnp.zeros((), _jnp.float32), _jax.tree.map(_jnp.zeros_like, weights))
        (loss, grad_w), grad_x = _jax.lax.scan(body, init, (per_example, given["loss_target"]))
    with _jax.named_scope("update"):
        delta_w, new_m, new_v = {}, {}, {}
        for n in TWIN_WEIGHTS:
            delta_w[n], new_m[n], new_v[n] = _adamw(weights[n], grad_w[n], given["m_" + n], given["v_" + n])
    return (loss, grad_x, *[grad_w[n] for n in TWIN_WEIGHTS], *[delta_w[n] for n in TWIN_WEIGHTS],
            *[new_m[n] for n in TWIN_WEIGHTS], *[new_v[n] for n in TWIN_WEIGHTS])
```

```python
import functools

import numpy as np
import jax
import jax.numpy as jnp
from jax import lax
from jax.experimental import pallas as pl
from jax.experimental.pallas import tpu as pltpu

F32 = jnp.float32
BF16 = jnp.bfloat16
HI = lax.Precision.HIGHEST
MESH = pl.DeviceIdType.MESH

N_DEV = 8
RMS_EPS = 1e-6
N_MOD = 9
GRID_W = 64
A_HEADS, A_KV_HEADS, A_HEAD_DIM = 8, 2, 64
A_REP = A_HEADS // A_KV_HEADS
WINDOW = 128
ROPE_BASE = 10000.0
B_HEADS, B_DK, B_DV = 4, 64, 128
B_GATE_RANK = 16
B_GATE_NORM = 16.0
B_CHUNK = 64
POOL_WINDOWS = (2, 4, 8, 16)
POOL_PAD = 8
A_Q = A_HEADS * A_HEAD_DIM
A_KV = A_KV_HEADS * A_HEAD_DIM
B_QK = B_HEADS * B_DK
B_V = B_HEADS * B_DV
PROJ_SIZES = (A_Q, A_KV, A_KV, B_QK, B_QK, B_V, B_V, 2 * B_GATE_RANK)
PROJ_DIM = sum(PROJ_SIZES)
ADAM_LR, ADAM_B1, ADAM_B2, ADAM_EPS, ADAM_WD, ADAM_STEP = 0.001, 0.9, 0.999, 1e-08, 0.01, 10

VMEM_LIMIT = 56 * 1024 * 1024

NN = ((1,), (0,))
NT = ((1,), (1,))
TN = ((0,), (0,))


def _dot(a, b, dims=NN, prec=None):
    return lax.dot_general(a, b, (dims, ((), ())), precision=prec, preferred_element_type=F32)


def _bdot(a, b, dims=NN):
    return _dot(a.astype(BF16), b.astype(BF16), dims)


def _params(sem=None, **kw):
    return pltpu.CompilerParams(dimension_semantics=sem, vmem_limit_bytes=VMEM_LIMIT, **kw)


def _silu(a):
    return a * jax.nn.sigmoid(a)


def _pick(n, prefs):
    for p in prefs:
        if n % p == 0:
            return p
    return n


def _full(shape):
    nd = len(shape)
    return pl.BlockSpec(shape, lambda *_: (0,) * nd)


def _peers():
    x, y, c = lax.axis_index("x"), lax.axis_index("y"), lax.axis_index("c")
    return x, y, c


def _dev_index():
    x, y, c = _peers()
    return 4 * x + 2 * y + c


def _all_gather(shards, name):
    n = len(shards)
    shapes = [s.shape for s in shards]

    def body(*refs):
        ins, outs = refs[:n], refs[n:2 * n]
        send_sems, recv_sems, local_sems = refs[2 * n:]
        x, y, c = _peers()
        me, sibling = (x, y, c), (x, y, 1 - c)
        chips = [(1 - x, y), (x, 1 - y), (1 - x, 1 - y)]

        def rows(a, dev):
            r = shapes[a][1]
            k = 4 * dev[0] + 2 * dev[1] + dev[2]
            return outs[a].at[:, pl.ds(k * r, r), :]

        def copy(a, k, block, to, src=None):
            return pltpu.make_async_remote_copy(
                src_ref=rows(a, block) if src is None else src, dst_ref=rows(a, block),
                send_sem=send_sems.at[a, k], recv_sem=recv_sems.at[a, k], device_id=to, device_id_type=MESH)

        mine = [pltpu.make_async_copy(ins[a], rows(a, me), local_sems.at[a]) for a in range(n)]
        for cp in mine:
            cp.start()
        first = []
        for a in range(n):
            first.append(copy(a, 0, me, sibling, src=ins[a]))
            first += [copy(a, 1 + j, me, (*chip, c), src=ins[a]) for j, chip in enumerate(chips)]
        for cp in first:
            cp.start()
        passed = []
        for j, chip in enumerate(chips):
            for a in range(n):
                copy(a, 1 + j, (*chip, c), me).wait_recv()
                fwd = copy(a, 4 + j, (*chip, c), sibling)
                fwd.start()
                passed.append(fwd)
        for a in range(n):
            copy(a, 0, sibling, me).wait_recv()
            for j, chip in enumerate(chips):
                copy(a, 4 + j, (*chip, 1 - c), me).wait_recv()
        for cp in first + passed:
            cp.wait_send()
        for cp in mine:
            cp.wait()

    any_spec = pl.BlockSpec(memory_space=pl.ANY)
    return pl.pallas_call(
        body, name=name,
        out_shape=[jax.ShapeDtypeStruct((s[0], N_DEV * s[1], s[2]), a.dtype) for s, a in zip(shapes, shards)],
        in_specs=[any_spec] * n, out_specs=[any_spec] * n,
        scratch_shapes=[pltpu.SemaphoreType.DMA((n, 7)), pltpu.SemaphoreType.DMA((n, 7)), pltpu.SemaphoreType.DMA((n,))],
        compiler_params=pltpu.CompilerParams(has_side_effects=True),
    )(*shards)


def _scatter_to_owners(fulls, name):
    n = len(fulls)
    shapes = [f.shape for f in fulls]

    def body(*refs):
        ins, outs = refs[:n], refs[n:2 * n]
        send_sems, recv_sems, local_sems = refs[2 * n:]
        x, y, c = _peers()
        me = 4 * x + 2 * y + c
        others = [(x, y, 1 - c), (1 - x, y, c), (x, 1 - y, c), (1 - x, 1 - y, c),
                  (1 - x, y, 1 - c), (x, 1 - y, 1 - c), (1 - x, 1 - y, 1 - c)]

        def piece(a, dev_idx):
            r = shapes[a][1] // N_DEV
            return ins[a].at[:, pl.ds(dev_idx * r, r), :]

        def copy(a, k, to):
            to_idx = 4 * to[0] + 2 * to[1] + to[2]
            return pltpu.make_async_remote_copy(
                src_ref=piece(a, to_idx), dst_ref=outs[a].at[me],
                send_sem=send_sems.at[a, k], recv_sem=recv_sems.at[a, k], device_id=to, device_id_type=MESH)

        def landed(a, k, frm):
            frm_idx = 4 * frm[0] + 2 * frm[1] + frm[2]
            return pltpu.make_async_remote_copy(
                src_ref=piece(a, me), dst_ref=outs[a].at[frm_idx],
                send_sem=send_sems.at[a, k], recv_sem=recv_sems.at[a, k], device_id=frm, device_id_type=MESH)

        mine = [pltpu.make_async_copy(piece(a, me), outs[a].at[me], local_sems.at[a]) for a in range(n)]
        for cp in mine:
            cp.start()
        sends = [copy(a, k, to) for a in range(n) for k, to in enumerate(others)]
        for cp in sends:
            cp.start()
        for a in range(n):
            for k, frm in enumerate(others):
                landed(a, k, frm).wait_recv()
        for cp in sends:
            cp.wait_send()
        for cp in mine:
            cp.wait()

    any_spec = pl.BlockSpec(memory_space=pl.ANY)
    return pl.pallas_call(
        body, name=name,
        out_shape=[jax.ShapeDtypeStruct((N_DEV, s[0], s[1] // N_DEV, s[2]), f.dtype) for s, f in zip(shapes, fulls)],
        in_specs=[any_spec] * n, out_specs=[any_spec] * n,
        scratch_shapes=[pltpu.SemaphoreType.DMA((n, 7)), pltpu.SemaphoreType.DMA((n, 7)), pltpu.SemaphoreType.DMA((n,))],
        compiler_params=pltpu.CompilerParams(has_side_effects=True),
    )(*fulls)


def _sum_slots(land, name):
    _, a_, r, c = land.shape
    tr = _pick(r, (352, 256, 128, 64, 32, 16, 8))

    def body(in_ref, out_ref):
        acc = in_ref[0].astype(F32)
        for s in range(1, N_DEV):
            acc = acc + in_ref[s].astype(F32)
        out_ref[...] = acc

    return pl.pallas_call(
        body, name=name, grid=(a_, r // tr),
        in_specs=[pl.BlockSpec((N_DEV, None, tr, c), lambda i, j: (0, i, j, 0))],
        out_specs=pl.BlockSpec((None, tr, c), lambda i, j: (i, j, 0)),
        out_shape=jax.ShapeDtypeStruct((a_, r, c), F32),
        compiler_params=_params(("parallel", "parallel")),
    )(land)


def _gather_small(vec, name):
    p = vec.shape[1]
    pp = -(-p // 1024) * 1024
    blk = jnp.pad(vec, ((0, 0), (0, pp - p))).reshape(8, pp // 8)

    def body(in_ref, out_ref, send_sems, recv_sems):
        x, y, c = _peers()
        me = 4 * x + 2 * y + c
        others = [(x, y, 1 - c), (1 - x, y, c), (x, 1 - y, c), (1 - x, 1 - y, c),
                  (1 - x, y, 1 - c), (x, 1 - y, 1 - c), (1 - x, 1 - y, 1 - c)]

        def rows(idx):
            return out_ref.at[pl.ds(pl.multiple_of(idx * 8, 8), 8), :]

        out_ref[pl.ds(pl.multiple_of(me * 8, 8), 8), :] = in_ref[...]

        def copy(k, dev, slot):
            return pltpu.make_async_remote_copy(
                src_ref=in_ref, dst_ref=rows(slot), send_sem=send_sems.at[k], recv_sem=recv_sems.at[k],
                device_id=dev, device_id_type=MESH)

        sends = [copy(k, dev, me) for k, dev in enumerate(others)]
        for cp in sends:
            cp.start()
        for k, dev in enumerate(others):
            copy(k, dev, 4 * dev[0] + 2 * dev[1] + dev[2]).wait_recv()
        for cp in sends:
            cp.wait_send()

    vm = pl.BlockSpec(memory_space=pltpu.VMEM)
    out = pl.pallas_call(
        body, name=name, out_shape=jax.ShapeDtypeStruct((8 * N_DEV, pp // 8), F32),
        in_specs=[vm], out_specs=vm,
        scratch_shapes=[pltpu.SemaphoreType.DMA((7,)), pltpu.SemaphoreType.DMA((7,))],
        compiler_params=pltpu.CompilerParams(has_side_effects=True, vmem_limit_bytes=VMEM_LIMIT),
    )(blk)
    return out.reshape(N_DEV, pp)[:, :p]


def _sum_rows8(g, name):
    p = g.shape[1]

    def body(in_ref, out_ref):
        acc = in_ref[0:1, :]
        for s in range(1, N_DEV):
            acc = acc + in_ref[s:s + 1, :]
        out_ref[...] = acc

    return pl.pallas_call(body, name=name, out_shape=jax.ShapeDtypeStruct((1, p), F32),
                          compiler_params=_params())(g)


def _sel_row(mods_ref, is_ctx, k):
    return jnp.where(is_ctx, mods_ref[1, k:k + 1, :], mods_ref[0, k:k + 1, :])


def _modulate(z, mods, g, ks, kc, n_x, out_dtype, name):
    m, d = z.shape
    tm = _pick(m, (256, 128, 64, 32, 16, 8))

    def body(z_ref, mods_ref, g_ref, h_ref):
        is_ctx = pl.program_id(0) * tm >= n_x
        zz = z_ref[...]
        r = lax.rsqrt(jnp.mean(zz * zz, axis=-1, keepdims=True) + RMS_EPS)
        shift, scale = _sel_row(mods_ref, is_ctx, ks), _sel_row(mods_ref, is_ctx, kc)
        h_ref[...] = ((zz * r) * g_ref[...] * (1.0 + scale) + shift).astype(out_dtype)

    return pl.pallas_call(
        body, name=name, grid=(m // tm,),
        in_specs=[pl.BlockSpec((tm, d), lambda i: (i, 0)), _full(mods.shape), _full(g.shape)],
        out_specs=pl.BlockSpec((tm, d), lambda i: (i, 0)),
        out_shape=jax.ShapeDtypeStruct((m, d), out_dtype),
        compiler_params=_params(("parallel",)),
    )(z, mods, g)


def _modulate_bwd(z, dh, dres, mods, g, kc, n_x, name):
    m, d = z.shape
    tm = _pick(m, (256, 128, 64, 32, 16, 8))
    first_ctx = n_x // tm

    def body(z_ref, dh_ref, dres_ref, mods_ref, g_ref, dx_ref, acc_ref):
        i = pl.program_id(0)
        is_ctx = i * tm >= n_x

        @pl.when((i == 0) | (i == first_ctx))
        def _():
            acc_ref[...] = jnp.zeros_like(acc_ref)

        zz, dhh = z_ref[...], dh_ref[...]
        r = lax.rsqrt(jnp.mean(zz * zz, axis=-1, keepdims=True) + RMS_EPS)
        nz = zz * r
        gain = g_ref[...] * (1.0 + _sel_row(mods_ref, is_ctx, kc))
        dn = dhh * gain
        dz = r * (dn - nz * jnp.mean(dn * nz, axis=-1, keepdims=True))
        dx_ref[...] = dres_ref[...] + dz
        acc_ref[0:1, :] += jnp.sum(dhh, axis=0, keepdims=True)
        acc_ref[1:2, :] += jnp.sum(dhh * nz, axis=0, keepdims=True)

    row = pl.BlockSpec((tm, d), lambda i: (i, 0))
    return pl.pallas_call(
        body, name=name, grid=(m // tm,),
        in_specs=[row, row, row, _full(mods.shape), _full(g.shape)],
        out_specs=[row, pl.BlockSpec((None, 8, d), lambda i: ((i * tm >= n_x).astype(jnp.int32), 0, 0))],
        out_shape=[jax.ShapeDtypeStruct((m, d), F32), jax.ShapeDtypeStruct((2, 8, d), F32)],
        compiler_params=_params(("arbitrary",)),
    )(z, dh, dres, mods, g)


def _ffn_up(h, wi_t, layer, name):
    m, d = h.shape
    f = wi_t.shape[1] // 2
    tm = _pick(m, (256, 128, 64, 32, 16, 8))
    tn = _pick(f, (1408, 512, 256, 128))
    nj = f // tn

    def body(h_ref, wa_ref, wu_ref, a_ref, u_ref, act_ref):
        hh = h_ref[...]
        a = _dot(hh, wa_ref[...], NT)
        u = _dot(hh, wu_ref[...], NT)
        a_ref[...] = a
        u_ref[...] = u
        act_ref[...] = (_silu(a) * u).astype(BF16)

    out = pl.BlockSpec((tm, tn), lambda j, i: (i, j))
    return pl.pallas_call(
        body, name=name, grid=(nj, m // tm),
        in_specs=[pl.BlockSpec((tm, d), lambda j, i: (i, 0)),
                  pl.BlockSpec((None, tn, d), lambda j, i: (layer, j, 0)),
                  pl.BlockSpec((None, tn, d), lambda j, i: (layer, nj + j, 0))],
        out_specs=[out, out, out],
        out_shape=[jax.ShapeDtypeStruct((m, f), F32), jax.ShapeDtypeStruct((m, f), F32),
                   jax.ShapeDtypeStruct((m, f), BF16)],
        compiler_params=_params(("parallel", "parallel")),
    )(h, wi_t, wi_t)


def _mm_resid(a, b, layer, res, mods, km, coef, n_x, name):
    m, k = a.shape
    n = b.shape[2]
    tm = _pick(m, (256, 128, 64, 32, 16, 8))
    tn = _pick(n, (512, 256, 128))

    def body(a_ref, b_ref, res_ref, mods_ref, out_ref, y_ref):
        is_ctx = pl.program_id(1) * tm >= n_x
        y = _dot(a_ref[...], b_ref[...])
        y_ref[...] = y
        out_ref[...] = res_ref[...] + coef * _sel_row(mods_ref, is_ctx, km) * y

    tile = pl.BlockSpec((tm, tn), lambda j, i: (i, j))
    return pl.pallas_call(
        body, name=name, grid=(n // tn, m // tm),
        in_specs=[pl.BlockSpec((tm, k), lambda j, i: (i, 0)),
                  pl.BlockSpec((None, k, tn), lambda j, i: (layer, 0, j)),
                  tile, pl.BlockSpec((2, 16, tn), lambda j, i: (0, 0, j))],
        out_specs=[tile, tile],
        out_shape=[jax.ShapeDtypeStruct((m, n), F32), jax.ShapeDtypeStruct((m, n), F32)],
        compiler_params=_params(("parallel", "parallel")),
    )(a, b, res, mods)


def _resid_bwd(dx, y, mods, km, coef, n_x, name):
    m, d = dx.shape
    tm = _pick(m, (256, 128, 64, 32, 16, 8))
    first_ctx = n_x // tm

    def body(dx_ref, y_ref, mods_ref, dy_ref, acc_ref):
        i = pl.program_id(0)
        is_ctx = i * tm >= n_x

        @pl.when((i == 0) | (i == first_ctx))
        def _():
            acc_ref[...] = jnp.zeros_like(acc_ref)

        dxx = dx_ref[...]
        dy_ref[...] = (coef * _sel_row(mods_ref, is_ctx, km) * dxx).astype(BF16)
        acc_ref[0:1, :] += jnp.sum(coef * y_ref[...] * dxx, axis=0, keepdims=True)

    row = pl.BlockSpec((tm, d), lambda i: (i, 0))
    return pl.pallas_call(
        body, name=name, grid=(m // tm,),
        in_specs=[row, row, _full(mods.shape)],
        out_specs=[row, pl.BlockSpec((None, 8, d), lambda i: ((i * tm >= n_x).astype(jnp.int32), 0, 0))],
        out_shape=[jax.ShapeDtypeStruct((m, d), BF16), jax.ShapeDtypeStruct((2, 8, d), F32)],
        compiler_params=_params(("arbitrary",)),
    )(dx, y, mods)


def _ffn_down_bwd(dy, wo, layer, a, u, name):
    m, d = dy.shape
    f = wo.shape[1]
    tm = _pick(m, (256, 128, 64, 32, 16, 8))
    tn = _pick(f, (1408, 512, 256, 128))

    def body(dy_ref, wo_ref, a_ref, u_ref, da_ref, du_ref):
        dact = _dot(dy_ref[...], wo_ref[...], NT)
        aa, uu = a_ref[...], u_ref[...]
        sg = jax.nn.sigmoid(aa)
        da_ref[...] = (dact * uu * (sg * (1.0 + aa * (1.0 - sg)))).astype(BF16)
        du_ref[...] = (dact * (aa * sg)).astype(BF16)

    tile = pl.BlockSpec((tm, tn), lambda j, i: (i, j))
    return pl.pallas_call(
        body, name=name, grid=(f // tn, m // tm),
        in_specs=[pl.BlockSpec((tm, d), lambda j, i: (i, 0)),
                  pl.BlockSpec((None, tn, d), lambda j, i: (layer, j, 0)), tile, tile],
        out_specs=[tile, tile],
        out_shape=[jax.ShapeDtypeStruct((m, f), BF16), jax.ShapeDtypeStruct((m, f), BF16)],
        compiler_params=_params(("parallel", "parallel")),
    )(dy, wo, a, u)


def _mm(terms, dims, n, out_dtype, name, tm_pref=(512, 256, 128, 64, 32, 16, 8), tn_pref=(512, 256, 128)):
    m = terms[0][0].shape[0]
    tm = _pick(m, tm_pref)
    tn = _pick(n, tn_pref)
    nt = len(terms)

    def body(*refs):
        out_ref = refs[2 * nt]
        acc = None
        for t in range(nt):
            part = _dot(refs[2 * t][...].astype(BF16), refs[2 * t + 1][...].astype(BF16), dims)
            acc = part if acc is None else acc + part
        out_ref[...] = acc.astype(out_dtype)

    in_specs, args = [], []
    for a, b, layer, rb in terms:
        k = a.shape[1]
        in_specs.append(pl.BlockSpec((tm, k), lambda j, i: (i, 0)))
        if dims == NN:
            in_specs.append(pl.BlockSpec((None, k, tn), lambda j, i, layer=layer, rb=rb: (layer, rb, j)))
        else:
            nb = n // tn
            in_specs.append(pl.BlockSpec((None, tn, k), lambda j, i, layer=layer, rb=rb, nb=nb: (layer, rb * nb + j, 0)))
        args += [a, b]
    return pl.pallas_call(
        body, name=name, grid=(n // tn, m // tm), in_specs=in_specs,
        out_specs=pl.BlockSpec((tm, tn), lambda j, i: (i, j)),
        out_shape=jax.ShapeDtypeStruct((m, n), out_dtype),
        compiler_params=_params(("parallel", "parallel")),
    )(*args)


def _mm_tn(a, b, out_dtype, name, rows=None):
    t = rows if rows is not None else a.shape[0]
    m, n = a.shape[1], b.shape[1]
    tm = _pick(m, (512, 256, 128))
    tn = _pick(n, (1024, 512, 256, 128))
    tk = _pick(t, (512, 256, 128, 64, 32, 16, 8))

    def body(a_ref, b_ref, out_ref, acc_ref):
        kk = pl.program_id(2)

        @pl.when(kk == 0)
        def _():
            acc_ref[...] = jnp.zeros_like(acc_ref)

        acc_ref[...] += _dot(a_ref[...].astype(BF16), b_ref[...].astype(BF16), TN)

        @pl.when(kk == pl.num_programs(2) - 1)
        def _():
            out_ref[...] = acc_ref[...].astype(out_dtype)

    return pl.pallas_call(
        body, name=name, grid=(m // tm, n // tn, t // tk),
        in_specs=[pl.BlockSpec((tk, tm), lambda i, j, k: (k, i)), pl.BlockSpec((tk, tn), lambda i, j, k: (k, j))],
        out_specs=pl.BlockSpec((tm, tn), lambda i, j, k: (i, j)),
        out_shape=jax.ShapeDtypeStruct((m, n), out_dtype),
        scratch_shapes=[pltpu.VMEM((tm, tn), F32)],
        compiler_params=_params(("parallel", "parallel", "arbitrary")),
    )(a, b)


def _final_loss(x, g, target, name):
    t, d = x.shape
    tm = _pick(t, (256, 128, 64, 32, 16, 8))

    def body(x_ref, g_ref, t_ref, dx_ref, loss_ref, dg_ref):
        @pl.when(pl.program_id(0) == 0)
        def _():
            loss_ref[...] = jnp.zeros_like(loss_ref)
            dg_ref[...] = jnp.zeros_like(dg_ref)

        xx, gg = x_ref[...], g_ref[...]
        r = lax.rsqrt(jnp.mean(xx * xx, axis=-1, keepdims=True) + RMS_EPS)
        nz = xx * r
        err = nz * gg - t_ref[...]
        loss_ref[...] += jnp.sum(err * err, axis=0, keepdims=True) * (0.5 / d)
        dout = err * (1.0 / d)
        dg_ref[...] += jnp.sum(dout * nz, axis=0, keepdims=True)
        dn = dout * gg
        dx_ref[...] = r * (dn - nz * jnp.mean(dn * nz, axis=-1, keepdims=True))

    row = pl.BlockSpec((tm, d), lambda i: (i, 0))
    vec = pl.BlockSpec((1, d), lambda i: (0, 0))
    return pl.pallas_call(
        body, name=name, grid=(t // tm,), in_specs=[row, vec, row], out_specs=[row, vec, vec],
        out_shape=[jax.ShapeDtypeStruct((t, d), F32), jax.ShapeDtypeStruct((1, d), F32),
                   jax.ShapeDtypeStruct((1, d), F32)],
        compiler_params=_params(("arbitrary",)),
    )(x, g, target)


def _adaln_fwd(craw, w_mod, b_cols, name):
    lyr, d, nc = w_mod.shape

    def body(c_ref, w_ref, b_ref, out_ref):
        out_ref[...] = _bdot(_silu(c_ref[...]), w_ref[...]) + b_ref[...]

    return pl.pallas_call(
        body, name=name, grid=(lyr,),
        in_specs=[_full(craw.shape), pl.BlockSpec((None, d, nc), lambda l: (l, 0, 0)),
                  pl.BlockSpec((None, 1, nc), lambda l: (l, 0, 0))],
        out_specs=pl.BlockSpec((None, 16, nc), lambda l: (l, 0, 0)),
        out_shape=jax.ShapeDtypeStruct((lyr, 16, nc), F32),
        compiler_params=_params(("parallel",)),
    )(craw, w_mod, b_cols)


def _adaln_bwd(craw, cs_t, dmm_cols, w_mod, name):
    lyr, d, nc = w_mod.shape

    def body(c_ref, cst_ref, dmm_ref, w_ref, gw_ref, dc_ref):
        dmm = dmm_ref[...]
        gw_ref[...] = _bdot(cst_ref[...], dmm)
        cc = c_ref[...]
        sg = jax.nn.sigmoid(cc)
        dc_ref[...] = _bdot(dmm, w_ref[...], NT) * (sg * (1.0 + cc * (1.0 - sg)))

    wspec = pl.BlockSpec((None, d, nc), lambda l: (l, 0, 0))
    return pl.pallas_call(
        body, name=name, grid=(lyr,),
        in_specs=[_full(craw.shape), _full(cs_t.shape), pl.BlockSpec((None, 16, nc), lambda l: (l, 0, 0)), wspec],
        out_specs=[wspec, pl.BlockSpec((None, 16, d), lambda l: (l, 0, 0))],
        out_shape=[jax.ShapeDtypeStruct((lyr, d, nc), F32), jax.ShapeDtypeStruct((lyr, 16, d), F32)],
        compiler_params=_params(("parallel",)),
    )(craw, cs_t, dmm_cols, w_mod)


def _rope_tables(t):
    rows = jnp.repeat(jnp.arange(t // GRID_W, dtype=F32), GRID_W)
    cols = jnp.tile(jnp.arange(GRID_W, dtype=F32), t // GRID_W)
    n = A_HEAD_DIM // 4
    freqs = ROPE_BASE ** (-jnp.arange(n, dtype=F32) / n)
    cr, sr = jnp.cos(rows[:, None] * freqs), jnp.sin(rows[:, None] * freqs)
    cc, sc = jnp.cos(cols[:, None] * freqs), jnp.sin(cols[:, None] * freqs)
    cos = jnp.concatenate([cr, cr, cc, cc], axis=-1)
    sin = jnp.concatenate([-sr, sr, -sc, sc], axis=-1)
    j = np.arange(A_HEAD_DIM)
    partner = np.where((j % 32) < 16, j + 16, j - 16)
    perm = np.zeros((A_HEAD_DIM, A_HEAD_DIM), np.float32)
    perm[partner, j] = 1.0
    return cos, sin, jnp.asarray(perm)


def _rope(xh, cos, sin, perm, adjoint, name):
    h, t, dh = xh.shape
    tb = _pick(t, (512, 256, 128))

    def body(x_ref, c_ref, s_ref, p_ref, o_ref):
        xx = x_ref[...]
        if adjoint:
            o_ref[...] = xx * c_ref[...] + _dot(xx * s_ref[...], p_ref[...], NN, HI)
        else:
            o_ref[...] = xx * c_ref[...] + _dot(xx, p_ref[...], NN, HI) * s_ref[...]

    blk = pl.BlockSpec((None, tb, dh), lambda i, j: (i, j, 0))
    tab = pl.BlockSpec((tb, dh), lambda i, j: (j, 0))
    return pl.pallas_call(
        body, name=name, grid=(h, t // tb), in_specs=[blk, tab, tab, _full(perm.shape)], out_specs=blk,
        out_shape=jax.ShapeDtypeStruct((h, t, dh), F32),
        compiler_params=_params(("parallel", "parallel")),
    )(xh, cos, sin, perm)


def _attn_probs(q, kb, kc, sink, n, t):
    scale = A_HEAD_DIM ** -0.5
    s1 = _bdot(q, kb, NT) * scale
    s2 = _bdot(q, kc, NT) * scale
    qpos = n * WINDOW + lax.broadcasted_iota(jnp.int32, s1.shape, 0)
    kpos = (n - 1) * WINDOW + lax.broadcasted_iota(jnp.int32, s1.shape, 1)
    valid = (kpos >= 0) & (kpos < t) & (jnp.abs(kpos - qpos) <= WINDOW)
    s1 = jnp.where(valid, s1, -jnp.inf)
    mx = jnp.maximum(jnp.maximum(jnp.max(s1, axis=-1, keepdims=True), jnp.max(s2, axis=-1, keepdims=True)), sink)
    p1, p2, ps = jnp.exp(s1 - mx), jnp.exp(s2 - mx), jnp.exp(sink - mx)
    inv = 1.0 / (jnp.sum(p1, axis=-1, keepdims=True) + jnp.sum(p2, axis=-1, keepdims=True) + ps)
    return p1 * inv, p2 * inv, ps * inv


def _attn_fwd(q, kp, vp, kc, vc, sink, name):
    hq, t, dh = q.shape
    nb = t // WINDOW
    lc = kc.shape[1]

    def body(q_ref, k_ref, v_ref, kc_ref, vc_ref, sink_ref, o_ref):
        n = pl.program_id(1)
        start = pl.multiple_of(n * WINDOW, WINDOW)
        kb, vb = k_ref[pl.ds(start, 3 * WINDOW), :], v_ref[pl.ds(start, 3 * WINDOW), :]
        p1, p2, _ = _attn_probs(q_ref[...], kb, kc_ref[...], sink_ref[...], n, t)
        o_ref[...] = _bdot(p1, vb) + _bdot(p2, vc_ref[...])

    qblk = pl.BlockSpec((None, WINDOW, dh), lambda h, n: (h, n, 0))
    kfull = pl.BlockSpec((None, t + 2 * WINDOW, dh), lambda h, n: (h // A_REP, 0, 0))
    cfull = pl.BlockSpec((None, lc, dh), lambda h, n: (h // A_REP, 0, 0))
    return pl.pallas_call(
        body, name=name, grid=(hq, nb),
        in_specs=[qblk, kfull, kfull, cfull, cfull, pl.BlockSpec((None, 1, 1), lambda h, n: (h, 0, 0))],
        out_specs=qblk, out_shape=jax.ShapeDtypeStruct((hq, t, dh), F32),
        compiler_params=_params(("parallel", "parallel")),
    )(q, kp, vp, kc, vc, sink)


def _attn_bwd(q, kp, vp, kc, vc, sink, o, do, name):
    hq, t, dh = q.shape
    nb = t // WINDOW
    lc = kc.shape[1]
    scale = A_HEAD_DIM ** -0.5

    def body(q_ref, k_ref, v_ref, kc_ref, vc_ref, sink_ref, o_ref, do_ref,
             dq_ref, dk_ref, dv_ref, dkc_ref, dvc_ref, dsink_ref):
        h, n = pl.program_id(0), pl.program_id(1)

        @pl.when((h % A_REP == 0) & (n == 0))
        def _():
            dk_ref[...] = jnp.zeros_like(dk_ref)
            dv_ref[...] = jnp.zeros_like(dv_ref)
            dkc_ref[...] = jnp.zeros_like(dkc_ref)
            dvc_ref[...] = jnp.zeros_like(dvc_ref)

        @pl.when(n == 0)
        def _():
            dsink_ref[...] = jnp.zeros_like(dsink_ref)

        start = pl.multiple_of(n * WINDOW, WINDOW)
        band = pl.ds(start, 3 * WINDOW)
        qq, kb, vb, kcc, vcc = q_ref[...], k_ref[band, :], v_ref[band, :], kc_ref[...], vc_ref[...]
        p1, p2, ps = _attn_probs(qq, kb, kcc, sink_ref[...], n, t)
        dout = do_ref[...]
        delta = jnp.sum(dout * o_ref[...], axis=-1, keepdims=True)
        ds1 = p1 * (_bdot(dout, vb, NT) - delta)
        ds2 = p2 * (_bdot(dout, vcc, NT) - delta)
        dq_ref[...] = (_bdot(ds1, kb) + _bdot(ds2, kcc)) * scale
        dk_ref[band, :] += _bdot(ds1.T, qq) * scale
        dv_ref[band, :] += _bdot(p1.T, dout)
        dkc_ref[...] += _bdot(ds2.T, qq) * scale
        dvc_ref[...] += _bdot(p2.T, dout)
        dsink_ref[...] += jnp.sum(-ps * delta, axis=0, keepdims=True)

    qblk = pl.BlockSpec((None, WINDOW, dh), lambda h, n: (h, n, 0))
    kfull = pl.BlockSpec((None, t + 2 * WINDOW, dh), lambda h, n: (h // A_REP, 0, 0))
    cfull = pl.BlockSpec((None, lc, dh), lambda h, n: (h // A_REP, 0, 0))
    return pl.pallas_call(
        body, name=name, grid=(hq, nb),
        in_specs=[qblk, kfull, kfull, cfull, cfull, pl.BlockSpec((None, 1, 1), lambda h, n: (h, 0, 0)), qblk, qblk],
        out_specs=[qblk, kfull, kfull, cfull, cfull, pl.BlockSpec((None, 8, 128), lambda h, n: (h, 0, 0))],
        out_shape=[jax.ShapeDtypeStruct(q.shape, F32), jax.ShapeDtypeStruct(kp.shape, F32),
                   jax.ShapeDtypeStruct(kp.shape, F32), jax.ShapeDtypeStruct(kc.shape, F32),
                   jax.ShapeDtypeStruct(kc.shape, F32), jax.ShapeDtypeStruct((hq, 8, 128), F32)],
        compiler_params=_params(("arbitrary", "arbitrary")),
    )(q, kp, vp, kc, vc, sink, o, do)


def _gate_fwd(zg, w2, b2, name):
    m = zg.shape[0]
    n = w2.shape[1]
    tm = _pick(m, (512, 256, 128, 64, 32, 16, 8))

    def body(z_ref, w_ref, b_ref, o_ref):
        o_ref[...] = jax.nn.log_sigmoid(_bdot(z_ref[...], w_ref[...]) + b_ref[...]) / B_GATE_NORM

    return pl.pallas_call(
        body, name=name, grid=(m // tm,),
        in_specs=[pl.BlockSpec((tm, zg.shape[1]), lambda i: (i, 0)), _full(w2.shape), _full(b2.shape)],
        out_specs=pl.BlockSpec((tm, n), lambda i: (i, 0)), out_shape=jax.ShapeDtypeStruct((m, n), F32),
        compiler_params=_params(("parallel",)),
    )(zg, w2, b2)


def _gate_bwd(zg, w2, b2, dla, name):
    m, rk = zg.shape
    n = w2.shape[1]
    tm = _pick(m, (512, 256, 128, 64, 32, 16, 8))

    def body(z_ref, w_ref, b_ref, d_ref, dz_ref, dw_ref, db_ref):
        @pl.when(pl.program_id(0) == 0)
        def _():
            dw_ref[...] = jnp.zeros_like(dw_ref)
            db_ref[...] = jnp.zeros_like(db_ref)

        zz, ww = z_ref[...], w_ref[...]
        pre = _bdot(zz, ww) + b_ref[...]
        dpre = d_ref[...] * (1.0 / B_GATE_NORM) * jax.nn.sigmoid(-pre)
        dz_ref[...] = _bdot(dpre, ww, NT)
        dw_ref[...] += _bdot(zz.T, dpre)
        db_ref[...] += jnp.sum(dpre, axis=0, keepdims=True)

    return pl.pallas_call(
        body, name=name, grid=(m // tm,),
        in_specs=[pl.BlockSpec((tm, rk), lambda i: (i, 0)), _full(w2.shape), _full(b2.shape),
                  pl.BlockSpec((tm, n), lambda i: (i, 0))],
        out_specs=[pl.BlockSpec((tm, rk), lambda i: (i, 0)), _full(w2.shape), _full(b2.shape)],
        out_shape=[jax.ShapeDtypeStruct((m, rk), F32), jax.ShapeDtypeStruct(w2.shape, F32),
                   jax.ShapeDtypeStruct(b2.shape, F32)],
        compiler_params=_params(("arbitrary",)),
    )(zg, w2, b2, dla)


def _chunk_order(step, n_x_chunks, n_chunks, reverse):
    n_c = n_chunks - n_x_chunks
    if reverse:
        return jnp.where(step < n_c, n_chunks - 1 - step, n_chunks - 1 - step)
    return jnp.where(step < n_c, n_x_chunks + step, step - n_c)


def _tri(reverse, transpose=False):
    i = lax.broadcasted_iota(jnp.int32, (B_CHUNK, B_CHUNK), 0)
    j = lax.broadcasted_iota(jnp.int32, (B_CHUNK, B_CHUNK), 1)
    if transpose:
        i, j = j, i
    return (j >= i) if reverse else (j <= i)


def _gla_chunk(q, k, la, reverse):
    g = _dot(_tri(reverse).astype(F32), la, NN, HI)
    last = 0 if reverse else B_CHUNK - 1
    gl = g[last:last + 1, :]
    eg, eng, egl = jnp.exp(g), jnp.exp(-g), jnp.exp(gl - g)
    decay_col = jnp.exp(jnp.sum(la.T, axis=1, keepdims=True))
    return q * (B_DK ** -0.5) * eg, k * eng, k * egl, eg, eng, egl, decay_col


def _gla_fwd(q, k, v, la, n_x, reverse, name):
    hh, tc, dk = q.shape
    dv = v.shape[2]
    nc, nxc = tc // B_CHUNK, n_x // B_CHUNK
    order = functools.partial(_chunk_order, n_x_chunks=nxc, n_chunks=nc, reverse=reverse)

    def body(q_ref, k_ref, v_ref, la_ref, o_ref, s_save_ref, s_ref):
        @pl.when(pl.program_id(0) == 0)
        def _():
            s_ref[...] = jnp.zeros_like(s_ref)

        mask = _tri(reverse)
        for h in range(hh):
            qt, kt, ke, _, _, _, decay_col = _gla_chunk(q_ref[h], k_ref[h], la_ref[h], reverse)
            vv, s_prev = v_ref[h], s_ref[h]
            att = jnp.where(mask, _bdot(qt, kt, NT), 0.0)
            o_ref[h] = _bdot(att, vv) + _bdot(qt, s_prev)
            s_save_ref[h] = s_prev
            s_ref[h] = decay_col * s_prev + _bdot(ke.T, vv)

    blk = lambda d: pl.BlockSpec((hh, B_CHUNK, d), lambda s: (0, order(s), 0))
    return pl.pallas_call(
        body, name=name, grid=(nc,),
        in_specs=[blk(dk), blk(dk), blk(dv), blk(dk)],
        out_specs=[blk(dv), pl.BlockSpec((None, hh, dk, dv), lambda s: (order(s), 0, 0, 0))],
        out_shape=[jax.ShapeDtypeStruct((hh, tc, dv), F32), jax.ShapeDtypeStruct((nc, hh, dk, dv), F32)],
        scratch_shapes=[pltpu.VMEM((hh, dk, dv), F32)],
        compiler_params=_params(("arbitrary",)),
    )(q, k, v, la)


def _gla_bwd(q, k, v, la, s_saved, do, n_x, reverse, name):
    hh, tc, dk = q.shape
    dv = v.shape[2]
    nc, nxc = tc // B_CHUNK, n_x // B_CHUNK
    order = lambda s: _chunk_order(nc - 1 - s, nxc, nc, reverse)
    last = 0 if reverse else B_CHUNK - 1

    def body(q_ref, k_ref, v_ref, la_ref, s_save_ref, do_ref, dq_ref, dk_ref, dv_ref, dla_ref, ds_ref):
        @pl.when(pl.program_id(0) == 0)
        def _():
            ds_ref[...] = jnp.zeros_like(ds_ref)

        mask = _tri(reverse)
        tri_t = _tri(reverse, transpose=True).astype(F32)
        is_last = lax.broadcasted_iota(jnp.int32, (B_CHUNK, dk), 0) == last
        for h in range(hh):
            qt, kt, ke, eg, eng, egl, decay_col = _gla_chunk(q_ref[h], k_ref[h], la_ref[h], reverse)
            vv, s_prev, dout, ds_new = v_ref[h], s_save_ref[h], do_ref[h], ds_ref[h]
            att = jnp.where(mask, _bdot(qt, kt, NT), 0.0)
            datt = jnp.where(mask, _bdot(dout, vv, NT), 0.0)
            dv_ref[h] = _bdot(att.T, dout) + _bdot(ke, ds_new)
            dqt = _bdot(datt, kt) + _bdot(dout, s_prev, NT)
            dkt = _bdot(datt.T, qt)
            dke = _bdot(vv, ds_new, NT)
            ddecay_row = jnp.sum((ds_new * s_prev).T, axis=0, keepdims=True)
            decay_row = jnp.exp(jnp.sum(la_ref[h], axis=0, keepdims=True))
            ds_ref[h] = decay_col * ds_new + _bdot(qt.T, dout)
            dq_ref[h] = dqt * (B_DK ** -0.5) * eg
            dk_ref[h] = dkt * eng + dke * egl
            dgl = jnp.sum(dke * ke, axis=0, keepdims=True) + ddecay_row * decay_row
            dg = dqt * qt - dkt * kt - dke * ke + jnp.where(is_last, dgl, 0.0)
            dla_ref[h] = _dot(tri_t, dg, NN, HI)

    blk = lambda d: pl.BlockSpec((hh, B_CHUNK, d), lambda s: (0, order(s), 0))
    return pl.pallas_call(
        body, name=name, grid=(nc,),
        in_specs=[blk(dk), blk(dk), blk(dv), blk(dk),
                  pl.BlockSpec((None, hh, dk, dv), lambda s: (order(s), 0, 0, 0)), blk(dv)],
        out_specs=[blk(dk), blk(dk), blk(dv), blk(dk)],
        out_shape=[jax.ShapeDtypeStruct((hh, tc, dk), F32), jax.ShapeDtypeStruct((hh, tc, dk), F32),
                   jax.ShapeDtypeStruct((hh, tc, dv), F32), jax.ShapeDtypeStruct((hh, tc, dk), F32)],
        scratch_shapes=[pltpu.VMEM((hh, dk, dv), F32)],
        compiler_params=_params(("arbitrary",)),
    )(q, k, v, la, s_saved, do)


def _gla_out_fwd(o_f, o_b, r, g, name):
    hh, t, dv = o_f.shape
    tb = _pick(t, (256, 128, 64))

    def body(of_ref, ob_ref, r_ref, g_ref, out_ref):
        for h in range(hh):
            o = of_ref[h] + ob_ref[h]
            rs = lax.rsqrt(jnp.mean(o * o, axis=-1, keepdims=True) + RMS_EPS)
            out_ref[:, h * dv:(h + 1) * dv] = (o * rs) * g_ref[...] * _silu(r_ref[:, h * dv:(h + 1) * dv])

    oblk = pl.BlockSpec((hh, tb, dv), lambda i: (0, i, 0))
    rblk = pl.BlockSpec((tb, hh * dv), lambda i: (i, 0))
    return pl.pallas_call(
        body, name=name, grid=(t // tb,), in_specs=[oblk, oblk, rblk, _full(g.shape)], out_specs=rblk,
        out_shape=jax.ShapeDtypeStruct((t, hh * dv), F32), compiler_params=_params(("parallel",)),
    )(o_f, o_b, r, g)


def _gla_out_bwd(o_f, o_b, r, g, dout, name):
    hh, t, dv = o_f.shape
    tb = _pick(t, (256, 128, 64))

    def body(of_ref, ob_ref, r_ref, g_ref, d_ref, do_ref, dr_ref, dg_ref):
        @pl.when(pl.program_id(0) == 0)
        def _():
            dg_ref[...] = jnp.zeros_like(dg_ref)

        gg = g_ref[...]
        for h in range(hh):
            cols = slice(h * dv, (h + 1) * dv)
            o = of_ref[h] + ob_ref[h]
            rs = lax.rsqrt(jnp.mean(o * o, axis=-1, keepdims=True) + RMS_EPS)
            nz = o * rs
            rr, dd = r_ref[:, cols], d_ref[:, cols]
            sg = jax.nn.sigmoid(rr)
            dr_ref[:, cols] = dd * nz * gg * (sg * (1.0 + rr * (1.0 - sg)))
            dy = dd * (rr * sg)
            dg_ref[...] += jnp.sum(dy * nz, axis=0, keepdims=True)
            dn = dy * gg
            do_ref[h] = rs * (dn - nz * jnp.mean(dn * nz, axis=-1, keepdims=True))

    oblk = pl.BlockSpec((hh, tb, dv), lambda i: (0, i, 0))
    rblk = pl.BlockSpec((tb, hh * dv), lambda i: (i, 0))
    return pl.pallas_call(
        body, name=name, grid=(t // tb,), in_specs=[oblk, oblk, rblk, _full(g.shape), rblk],
        out_specs=[oblk, rblk, _full(g.shape)],
        out_shape=[jax.ShapeDtypeStruct(o_f.shape, F32), jax.ShapeDtypeStruct(r.shape, F32),
                   jax.ShapeDtypeStruct(g.shape, F32)],
        compiler_params=_params(("arbitrary",)),
    )(o_f, o_b, r, g, dout)


def _pool_band(half, tb, adjoint):
    r = lax.broadcasted_iota(jnp.int32, (tb, tb + 2 * POOL_PAD), 0) + POOL_PAD
    j = lax.broadcasted_iota(jnp.int32, (tb, tb + 2 * POOL_PAD), 1)
    if adjoint:
        return ((j > r - half) & (j <= r + half)).astype(F32)
    return ((j >= r - half) & (j < r + half)).astype(F32)


def _pool_count(pos, half, t):
    return (jnp.minimum(pos + half, t) - jnp.maximum(pos - half, 0)).astype(F32)


def _pool_fwd(hp, w_pool, pool_scale, res, mods, km, name):
    t, d = res.shape
    ng, gw = w_pool.shape[0], w_pool.shape[1]
    tb = _pick(t, (256, 128, 64))

    def body(hp_ref, w_ref, ps_ref, res_ref, mods_ref, out_ref, pooled_ref, ypre_ref):
        gi, i = pl.program_id(0), pl.program_id(1)
        half = jnp.left_shift(1, gi)
        win = hp_ref[pl.ds(pl.multiple_of(i * tb, tb), tb + 2 * POOL_PAD), :]
        total = _dot(_pool_band(half, tb, False), win, NN, HI)
        pos = i * tb + lax.broadcasted_iota(jnp.int32, (tb, 1), 0)
        pooled = total / _pool_count(pos, half, t) - win[POOL_PAD:POOL_PAD + tb, :]
        ypre = _bdot(pooled, w_ref[...])
        pooled_ref[...] = pooled.astype(BF16)
        ypre_ref[...] = ypre
        out_ref[...] = res_ref[...] + mods_ref[0, km:km + 1, :] * (ypre * ps_ref[...])

    tile = pl.BlockSpec((tb, gw), lambda gi, i: (i, gi))
    return pl.pallas_call(
        body, name=name, grid=(ng, t // tb),
        in_specs=[pl.BlockSpec((t + 2 * POOL_PAD, gw), lambda gi, i: (0, gi)),
                  pl.BlockSpec((None, gw, gw), lambda gi, i: (gi, 0, 0)),
                  pl.BlockSpec((1, gw), lambda gi, i: (0, gi)), tile,
                  pl.BlockSpec((2, 16, gw), lambda gi, i: (0, 0, gi))],
        out_specs=[tile, tile, tile],
        out_shape=[jax.ShapeDtypeStruct((t, d), F32), jax.ShapeDtypeStruct((t, d), BF16),
                   jax.ShapeDtypeStruct((t, d), F32)],
        compiler_params=_params(("parallel", "parallel")),
    )(hp, w_pool, pool_scale, res, mods)


def _pool_bwd(dxp, w_pool, pool_scale, pooled, ypre, mods, km, name):
    t, d = pooled.shape
    ng, gw = w_pool.shape[0], w_pool.shape[1]
    tb = _pick(t, (256, 128, 64))

    def body(dxp_ref, w_ref, ps_ref, pooled_ref, ypre_ref, mods_ref, dh_ref, dw_ref, acc_ref):
        gi, i = pl.program_id(0), pl.program_id(1)

        @pl.when(i == 0)
        def _():
            dw_ref[...] = jnp.zeros_like(dw_ref)
            acc_ref[...] = jnp.zeros_like(acc_ref)

        half = jnp.left_shift(1, gi)
        mod, ps = mods_ref[0, km:km + 1, :], ps_ref[...]
        dwin = dxp_ref[pl.ds(pl.multiple_of(i * tb, tb), tb + 2 * POOL_PAD), :]
        dpooled = _bdot(dwin * (mod * ps), w_ref[...], NT)
        pos = i * tb - POOL_PAD + lax.broadcasted_iota(jnp.int32, (tb + 2 * POOL_PAD, 1), 0)
        spread = _dot(_pool_band(half, tb, True), dpooled / jnp.maximum(_pool_count(pos, half, t), 1.0), NN, HI)
        dh_ref[...] = spread - dpooled[POOL_PAD:POOL_PAD + tb, :]
        dxc, yp = dwin[POOL_PAD:POOL_PAD + tb, :], ypre_ref[...]
        dw_ref[...] += _bdot(pooled_ref[...].astype(F32).T, dxc * (mod * ps))
        acc_ref[0:1, :] += jnp.sum(dxc * yp * mod, axis=0, keepdims=True)
        acc_ref[1:2, :] += jnp.sum(dxc * yp * ps, axis=0, keepdims=True)

    tile = pl.BlockSpec((tb, gw), lambda gi, i: (i, gi))
    wblk = pl.BlockSpec((None, gw, gw), lambda gi, i: (gi, 0, 0))
    return pl.pallas_call(
        body, name=name, grid=(ng, t // tb),
        in_specs=[pl.BlockSpec((t + 2 * POOL_PAD, gw), lambda gi, i: (0, gi)), wblk,
                  pl.BlockSpec((1, gw), lambda gi, i: (0, gi)), tile, tile,
                  pl.BlockSpec((2, 16, gw), lambda gi, i: (0, 0, gi))],
        out_specs=[tile, wblk, pl.BlockSpec((8, gw), lambda gi, i: (0, gi))],
        out_shape=[jax.ShapeDtypeStruct((t, d), F32), jax.ShapeDtypeStruct(w_pool.shape, F32),
                   jax.ShapeDtypeStruct((8, d), F32)],
        compiler_params=_params(("arbitrary", "arbitrary")),
    )(dxp, w_pool, pool_scale, pooled, ypre, mods)


def _adamw(w, g, m, v, name):
    r, c = w.shape
    tr = _pick(r, (512, 352, 256, 128, 64, 32, 16, 8))
    c1 = 1.0 / (1.0 - ADAM_B1 ** ADAM_STEP)
    c2 = 1.0 / (1.0 - ADAM_B2 ** ADAM_STEP)

    def body(w_ref, g_ref, m_ref, v_ref, d_ref, nm_ref, nv_ref):
        gg = g_ref[...]
        nm = ADAM_B1 * m_ref[...] + (1.0 - ADAM_B1) * gg
        nv = ADAM_B2 * v_ref[...] + (1.0 - ADAM_B2) * (gg * gg)
        nm_ref[...] = nm
        nv_ref[...] = nv
        d_ref[...] = -ADAM_LR * ((nm * c1) / (jnp.sqrt(nv * c2) + ADAM_EPS) + ADAM_WD * w_ref[...])

    blk = pl.BlockSpec((tr, c), lambda i: (i, 0))
    shp = jax.ShapeDtypeStruct((r, c), F32)
    return pl.pallas_call(
        body, name=name, grid=(r // tr,), in_specs=[blk] * 4, out_specs=[blk] * 3, out_shape=[shp] * 3,
        compiler_params=_params(("parallel",)),
    )(w, g, m, v)


def _heads(z, n_heads):
    m = z.shape[0]
    return z.reshape(m, n_heads, -1).transpose(1, 0, 2)


def _unheads(zh):
    return zh.transpose(1, 0, 2).reshape(zh.shape[1], -1)


def _pad_rows(a, n):
    return jnp.pad(a, ((0, 0), (n, n), (0, 0))) if a.ndim == 3 else jnp.pad(a, ((n, n), (0, 0)))


def _local_step(x, ctx, target, mods, wts):
    t, d = x.shape
    l_ctx = ctx.shape[0]
    tc = t + l_ctx
    wi1, wo1, wi2, wo2 = wts["wi1_t"], wts["wo1"], wts["wi2_t"], wts["wo2"]
    norm_g = wts["norm_g"]
    ng = lambda l, k: norm_g[l, k][None, :]
    grads = {}
    dmods = [[[None] * N_MOD for _ in range(2)] for _ in range(2)]
    dnorm = [[None] * 3 for _ in range(2)]

    def ffn_fwd(z, l, kbase, wi, wo, g, n_x, tag):
        h = _modulate(z, mods[l], g, kbase, kbase + 1, n_x, BF16, f"mod_{tag}")
        a, u, act = _ffn_up(h, wi, l, f"ffn_up_{tag}")
        z_new, y = _mm_resid(act, wo, l, z, mods[l], kbase + 2, 0.5, n_x, f"ffn_down_{tag}")
        return z_new, (z, h, a, u, act, y)

    def ffn_bwd(dz_new, saved, l, kbase, wi, wo, g, n_x, tag):
        z, h, a, u, act, y = saved
        m = z.shape[0]
        dy, acc_gate = _resid_bwd(dz_new, y, mods[l], kbase + 2, 0.5, n_x, f"resid_bwd_{tag}")
        da, du = _ffn_down_bwd(dy, wo, l, a, u, f"ffn_down_bwd_{tag}")
        f = da.shape[1]
        dwo = _mm_tn(act, dy, BF16, f"dwo_{tag}")
        dh = _mm([(da, wi, l, 0), (du, wi, l, 1)], NN, d, F32, f"dh_{tag}",
                 tm_pref=(256, 128, 64, 32, 16, 8))
        dwi_t = jnp.concatenate([_mm_tn(da, h, BF16, f"dwi_a_{tag}"), _mm_tn(du, h, BF16, f"dwi_u_{tag}")], axis=0)
        dz, acc_mod = _modulate_bwd(z, dh, dz_new, mods[l], g, kbase + 1, n_x, f"mod_bwd_{tag}")
        del m, f
        return dz, dwi_t, dwo, acc_mod, acc_gate

    def record(l, kbase, k_norm, g, acc_mod, acc_gate, streams):
        total = None
        for s in range(streams):
            dmods[l][s][kbase] = acc_mod[s, 0]
            dmods[l][s][kbase + 1] = acc_mod[s, 1] * g[0]
            if acc_gate is not None:
                dmods[l][s][kbase + 2] = acc_gate[s, 0]
            part = acc_mod[s, 1] * (1.0 + mods[l][s, kbase + 1])
            total = part if total is None else total + part
        dnorm[l][k_norm] = total

    xc0 = jnp.concatenate([x, ctx], axis=0)
    xc1, sv_f1 = ffn_fwd(xc0, 0, 0, wi1, wo1, ng(0, 0), t, "l0f1")
    hc = _modulate(xc1, mods[0], ng(0, 1), 3, 4, t, BF16, "mod_l0mix")
    w_in_t = wts["w_in_t"]
    n_proj = w_in_t.shape[1]
    zall = _mm([(hc, w_in_t, 0, 0)], NT, n_proj, F32, "proj", tm_pref=(256, 128, 64, 32, 16, 8),
               tn_pref=(n_proj,))
    offs = np.cumsum((0,) + PROJ_SIZES)
    part = lambda i, rows=slice(None): zall[rows, offs[i]:offs[i + 1]]
    lat, con = slice(0, t), slice(t, tc)
    cos, sin, perm = _rope_tables(t)
    qa = _rope(_heads(part(0, lat), A_HEADS), cos, sin, perm, False, "rope_q")
    ka = _rope(_heads(part(1, lat), A_KV_HEADS), cos, sin, perm, False, "rope_k")
    va = _heads(part(2, lat), A_KV_HEADS)
    kca, vca = _heads(part(1, con), A_KV_HEADS), _heads(part(2, con), A_KV_HEADS)
    kap, vap = _pad_rows(ka, WINDOW), _pad_rows(va, WINDOW)
    sink = wts["sink"].reshape(A_HEADS, 1, 1)
    o_a = _attn_fwd(qa, kap, vap, kca, vca, sink, "attn_fwd")

    qb, kb, vb = _heads(part(3), B_HEADS), _heads(part(4), B_HEADS), _heads(part(5), B_HEADS)
    rb = part(6, lat)
    zg = part(7)
    zg_f, zg_b = zg[:, :B_GATE_RANK], zg[:, B_GATE_RANK:]
    w2f, w2b, b2f, b2b = wts["w_a2_f"], wts["w_a2_b"], wts["b_a_f"], wts["b_a_b"]
    la_f = _heads(_gate_fwd(zg_f, w2f, b2f, "gate_f"), B_HEADS)
    la_b = _heads(_gate_fwd(zg_b, w2b, b2b, "gate_b"), B_HEADS)
    o_f, s_f = _gla_fwd(qb, kb, vb, la_f, t, False, "gla_fwd_f")
    o_b, s_b = _gla_fwd(qb, kb, vb, la_b, t, True, "gla_fwd_b")
    gla_g = wts["gla_g"]
    go = _gla_out_fwd(o_f[:, :t], o_b[:, :t], rb, gla_g, "gla_out")
    cat = jnp.concatenate([_unheads(o_a), go], axis=-1).astype(BF16)
    x1 = xc1[:t]
    x2, y_mix0 = _mm_resid(cat, wts["w_out"], 0, x1, mods[0], 5, 1.0, t, "w_out")
    x3, sv_f2 = ffn_fwd(x2, 0, 6, wi2, wo2, ng(0, 2), t, "l0f2")

    x4, sv_g1 = ffn_fwd(x3, 1, 0, wi1, wo1, ng(1, 0), t, "l1f1")
    hp = _modulate(x4, mods[1], ng(1, 1), 3, 4, t, F32, "mod_l1mix")
    w_pool, pool_scale = wts["w_pool"], wts["pool_scale"]
    x5, pooled, ypre = _pool_fwd(_pad_rows(hp, POOL_PAD), w_pool, pool_scale, x4, mods[1], 5, "pool_fwd")
    x6, sv_g2 = ffn_fwd(x5, 1, 6, wi2, wo2, ng(1, 2), t, "l1f2")

    dx6, loss_vec, dfinal_g = _final_loss(x6, wts["final_g"], target, "final_loss")
    grads["final_g"] = dfinal_g[0]

    dx5, dwi2_1, dwo2_1, acc_mod, acc_gate = ffn_bwd(dx6, sv_g2, 1, 6, wi2, wo2, ng(1, 2), t, "l1f2")
    record(1, 6, 2, ng(1, 2), acc_mod, acc_gate, 1)
    dhp, dw_pool, acc_pool = _pool_bwd(_pad_rows(dx5, POOL_PAD), w_pool, pool_scale, pooled, ypre, mods[1], 5,
                                       "pool_bwd")
    grads["w_pool"] = dw_pool
    grads["pool_scale"] = acc_pool[0]
    dmods[1][0][5] = acc_pool[1]
    dx4, acc_mod = _modulate_bwd(x4, dhp, dx5, mods[1], ng(1, 1), 4, t, "mod_bwd_l1mix")
    record(1, 3, 1, ng(1, 1), acc_mod, None, 1)
    dx3, dwi1_1, dwo1_1, acc_mod, acc_gate = ffn_bwd(dx4, sv_g1, 1, 0, wi1, wo1, ng(1, 0), t, "l1f1")
    record(1, 0, 0, ng(1, 0), acc_mod, acc_gate, 1)

    dx2, dwi2_0, dwo2_0, acc_mod, acc_gate = ffn_bwd(dx3, sv_f2, 0, 6, wi2, wo2, ng(0, 2), t, "l0f2")
    record(0, 6, 2, ng(0, 2), acc_mod, acc_gate, 1)
    dymix, acc_gate = _resid_bwd(dx2, y_mix0, mods[0], 5, 1.0, t, "resid_bwd_mix")
    dmods[0][0][5] = acc_gate[0, 0]
    grads["w_out"] = _mm_tn(cat, dymix, BF16, "dw_out")
    dcat = _mm([(dymix, wts["w_out"], 0, 0)], NT, cat.shape[1], F32, "dcat")
    do_a = _heads(dcat[:, :A_Q], A_HEADS)
    do_gla, drb, dgla_g = _gla_out_bwd(o_f[:, :t], o_b[:, :t], rb, gla_g, dcat[:, A_Q:], "gla_out_bwd")
    grads["gla_g"] = dgla_g[0]
    do_full = jnp.pad(do_gla, ((0, 0), (0, l_ctx), (0, 0)))
    dq_f, dk_f, dv_f, dla_f = _gla_bwd(qb, kb, vb, la_f, s_f, do_full, t, False, "gla_bwd_f")
    dq_b, dk_b, dv_b, dla_b = _gla_bwd(qb, kb, vb, la_b, s_b, do_full, t, True, "gla_bwd_b")
    dzg_f, dw2f, db2f = _gate_bwd(zg_f, w2f, b2f, _unheads(dla_f), "gate_bwd_f")
    dzg_b, dw2b, db2b = _gate_bwd(zg_b, w2b, b2b, _unheads(dla_b), "gate_bwd_b")
    grads.update(w_a2_f=dw2f, w_a2_b=dw2b, b_a_f=db2f[0], b_a_b=db2b[0])
    dqa_r, dkap, dvap, dkca, dvca, dsink = _attn_bwd(qa, kap, vap, kca, vca, sink, o_a, do_a, "attn_bwd")
    grads["sink"] = dsink[:, 0, 0]
    dqa = _rope(dqa_r, cos, sin, perm, True, "rope_bwd_q")
    dka = _rope(dkap[:, WINDOW:WINDOW + t], cos, sin, perm, True, "rope_bwd_k")
    dva = dvap[:, WINDOW:WINDOW + t]
    zrow = lambda a, n: jnp.pad(a, ((0, n), (0, 0)))
    dz_parts = [
        zrow(_unheads(dqa), l_ctx),
        jnp.concatenate([_unheads(dka), _unheads(dkca)], axis=0),
        jnp.concatenate([_unheads(dva), _unheads(dvca)], axis=0),
        _unheads(dq_f + dq_b), _unheads(dk_f + dk_b), _unheads(dv_f + dv_b),
        zrow(drb, l_ctx),
        jnp.concatenate([dzg_f, dzg_b], axis=-1),
        jnp.zeros((tc, n_proj - PROJ_DIM), F32),
    ]
    dzall = jnp.concatenate(dz_parts, axis=-1).astype(BF16)
    grads["w_in_t"] = _mm_tn(dzall, hc, BF16, "dw_in")
    dhc = _mm([(dzall, w_in_t, 0, 0)], NN, d, F32, "dhc", tm_pref=(256, 128, 64, 32, 16, 8))
    dxc1_res = jnp.concatenate([dx2, jnp.zeros((l_ctx, d), F32)], axis=0)
    dxc1, acc_mod = _modulate_bwd(xc1, dhc, dxc1_res, mods[0], ng(0, 1), 4, t, "mod_bwd_l0mix")
    record(0, 3, 1, ng(0, 1), acc_mod, None, 2)
    dxc0, dwi1_0, dwo1_0, acc_mod, acc_gate = ffn_bwd(dxc1, sv_f1, 0, 0, wi1, wo1, ng(0, 0), t, "l0f1")
    record(0, 0, 0, ng(0, 0), acc_mod, acc_gate, 2)

    grads["wi1_t"] = jnp.stack([dwi1_0, dwi1_1])
    grads["wi2_t"] = jnp.stack([dwi2_0, dwi2_1])
    grads["wo1"] = jnp.stack([dwo1_0, dwo1_1])
    grads["wo2"] = jnp.stack([dwo2_0, dwo2_1])
    grads["norm_g"] = jnp.stack([jnp.stack(dnorm[0]), jnp.stack(dnorm[1])])
    zero = jnp.zeros((d,), F32)
    dmods_arr = jnp.stack([jnp.stack([jnp.stack([v if v is not None else zero for v in dmods[l][s]])
                                      for s in range(2)]) for l in range(2)])
    return loss_vec, dxc0[:t], grads, dmods_arr


def _pack(parts):
    flat = jnp.concatenate([p.reshape(-1).astype(F32) for p in parts])
    pad = (-flat.shape[0]) % 128
    return jnp.pad(flat, (0, pad))[None, :]


def _unpack(rows, shapes):
    out, off = [], 0
    for s in shapes:
        n = int(np.prod(s))
        out.append(rows[:, off:off + n].reshape((rows.shape[0],) + tuple(s)))
        off += n
    return out


def _cols_to_full(g):
    g = jnp.moveaxis(g, 0, -2)
    return g.reshape(g.shape[:-2] + (-1,))


def kernel(x, c, ctx, c_ctx, w_mod, b_mod, norm_g, ffn1_wi, ffn1_wo, ffn2_wi, ffn2_wo, w_in, w_a2_f, b_a_f, w_a2_b, b_a_b, sink, gla_g, w_out, w_pool, pool_scale, final_g, loss_target, m_c_ctx, m_w_mod, m_b_mod, m_norm_g, m_ffn1_wi, m_ffn1_wo, m_ffn2_wi, m_ffn2_wo, m_w_in, m_w_a2_f, m_b_a_f, m_w_a2_b, m_b_a_b, m_sink, m_gla_g, m_w_out, m_w_pool, m_pool_scale, m_final_g, v_c_ctx, v_w_mod, v_b_mod, v_norm_g, v_ffn1_wi, v_ffn1_wo, v_ffn2_wi, v_ffn2_wo, v_w_in, v_w_a2_f, v_b_a_f, v_w_a2_b, v_b_a_b, v_sink, v_gla_g, v_w_out, v_w_pool, v_pool_scale, v_final_g):
    t, d = x.shape[1], x.shape[2]
    me = _dev_index()
    nc = w_mod.shape[2]
    ncol_in = w_in.shape[2]
    ncol_pad = -(-ncol_in // 16) * 16

    small_shapes = [(d,), norm_g.shape, pool_scale.shape, w_a2_f.shape, w_a2_b.shape, w_pool.shape]
    g1 = _gather_small(_pack([c, norm_g, pool_scale, w_a2_f, w_a2_b, w_pool]), "gather_params")
    c_all, norm_g_all, pool_scale_all, w2f_all, w2b_all, w_pool_all = _unpack(g1, small_shapes)
    wts = {
        "norm_g": _cols_to_full(norm_g_all),
        "pool_scale": _cols_to_full(pool_scale_all),
        "w_a2_f": _cols_to_full(w2f_all)[0],
        "w_a2_b": _cols_to_full(w2b_all)[0],
        "w_pool": jnp.moveaxis(w_pool_all[:, 0], 0, 1).reshape(w_pool.shape[1], -1, w_pool.shape[3]),
        "b_a_f": b_a_f, "b_a_b": b_a_b, "sink": sink[0], "gla_g": gla_g, "final_g": final_g[None, :],
    }

    craw = jnp.concatenate([c_all, c_ctx[None, :], jnp.zeros((16 - N_DEV - 1, d), F32)], axis=0)
    b_cols = lax.dynamic_slice_in_dim(b_mod, me * nc, nc, axis=1)[:, None, :]
    mm_cols = _adaln_fwd(craw, w_mod, b_cols, "adaln_fwd")
    g2 = _gather_small(mm_cols.reshape(1, -1), "gather_mods").reshape(N_DEV, 2, 16, nc)
    mm_full = jnp.moveaxis(g2, 0, 2).reshape(2, 16, N_MOD, d)
    mods = jnp.stack([lax.dynamic_index_in_dim(mm_full, me, axis=1, keepdims=False), mm_full[:, N_DEV]], axis=1)
    mods = jnp.pad(mods, ((0, 0), (0, 0), (0, 16 - N_MOD), (0, 0)))

    tr = lambda w: jnp.swapaxes(w, 1, 2).astype(BF16)
    w_in_sh = jnp.pad(tr(w_in), ((0, 0), (0, ncol_pad - ncol_in), (0, 0)))
    wi1_t, wi2_t, wo1, wo2, w_in_g, w_out_g = _all_gather(
        [tr(ffn1_wi), tr(ffn2_wi), ffn1_wo.astype(BF16), ffn2_wo.astype(BF16), w_in_sh, w_out.astype(BF16)],
        "gather_weights")
    n_proj = -(-(N_DEV * ncol_in) // 128) * 128
    w_in_t = w_in_g.reshape(1, N_DEV, ncol_pad, d)[:, :, :ncol_in].reshape(1, N_DEV * ncol_in, d)
    w_in_t = jnp.pad(w_in_t, ((0, 0), (0, n_proj - N_DEV * ncol_in), (0, 0)))
    wts.update(wi1_t=wi1_t, wi2_t=wi2_t, wo1=wo1, wo2=wo2, w_in_t=w_in_t, w_out=w_out_g)

    loss_vec, grad_x, grads, dmods = _local_step(x[0], ctx[0], loss_target[0], mods, wts)
    loss = lax.psum(jnp.sum(loss_vec), ("x", "y", "c"))

    dw_in_full = grads["w_in_t"][:N_DEV * ncol_in].reshape(N_DEV, ncol_in, d)
    dw_in_full = jnp.pad(dw_in_full, ((0, 0), (0, ncol_pad - ncol_in), (0, 0))).reshape(1, N_DEV * ncol_pad, d)
    lands = _scatter_to_owners(
        [grads["wi1_t"], grads["wi2_t"], grads["wo1"], grads["wo2"], dw_in_full, grads["w_out"][None],
         grads["w_pool"].astype(BF16)], "scatter_grads")
    names = ["wi1", "wi2", "wo1", "wo2", "w_in", "w_out", "w_pool"]
    sums = {nm: _sum_slots(ld, f"sum_{nm}") for nm, ld in zip(names, lands)}
    back = lambda g: jnp.swapaxes(g, 1, 2)
    g_big = {
        "ffn1_wi": back(sums["wi1"]), "ffn2_wi": back(sums["wi2"]), "ffn1_wo": sums["wo1"], "ffn2_wo": sums["wo2"],
        "w_in": back(sums["w_in"][:, :ncol_in]), "w_out": sums["w_out"], "w_pool": sums["w_pool"][None],
    }

    small_g = [dmods[:, :, :N_MOD].reshape(2, 2, N_MOD * d), grads["norm_g"], grads["pool_scale"], grads["final_g"],
               grads["b_a_f"], grads["b_a_b"], grads["sink"], grads["gla_g"], grads["w_a2_f"], grads["w_a2_b"]]
    small_g_shapes = [a.shape for a in small_g]
    g3 = _gather_small(_pack(small_g), "gather_small_grads")
    total = _sum_rows8(g3, "sum_small_grads")
    dmm_all = _unpack(g3, small_g_shapes[:1])[0]
    (dmm_sum, dnorm_g, dpool_scale, dfinal_g, db_a_f, db_a_b, dsink, dgla_g, dw_a2_f, dw_a2_b) = [
        a[0] for a in _unpack(total, small_g_shapes)]
    dmm_rows = jnp.concatenate([dmm_all[:, :, 0].transpose(1, 0, 2), dmm_sum[:, 1][:, None, :],
                                jnp.zeros((2, 16 - N_DEV - 1, N_MOD * d), F32)], axis=1)
    grad_b_mod = dmm_sum[:, 0] + dmm_sum[:, 1]
    dmm_cols = lax.dynamic_slice_in_dim(dmm_rows, me * nc, nc, axis=2)
    cs_t = jnp.transpose(_silu(craw)).astype(BF16)
    grad_w_mod, dcraw = _adaln_bwd(craw, cs_t, dmm_cols, w_mod, "adaln_bwd")
    g4 = _gather_small((dcraw[0, N_DEV] + dcraw[1, N_DEV])[None, :], "gather_c_ctx_grad")
    grad_c_ctx = _sum_rows8(g4, "sum_c_ctx_grad")[0]

    col = lambda v, n: lax.dynamic_slice_in_dim(v, me * n, n, axis=v.ndim - 1)
    g_small = {
        "c_ctx": grad_c_ctx, "b_mod": grad_b_mod, "norm_g": col(dnorm_g, norm_g.shape[2]),
        "w_a2_f": col(dw_a2_f, w_a2_f.shape[2])[None], "b_a_f": db_a_f[None], "w_a2_b": col(dw_a2_b, w_a2_b.shape[2])[None],
        "b_a_b": db_a_b[None], "sink": dsink[None], "gla_g": dgla_g[None], "pool_scale": col(dpool_scale, pool_scale.shape[1])[None],
        "final_g": dfinal_g,
    }
    g_all = {**g_big, **g_small, "w_mod": grad_w_mod}

    order = ["c_ctx", "w_mod", "b_mod", "norm_g", "ffn1_wi", "ffn1_wo", "ffn2_wi", "ffn2_wo", "w_in", "w_a2_f", "b_a_f",
             "w_a2_b", "b_a_b", "sink", "gla_g", "w_out", "w_pool", "pool_scale", "final_g"]
    ws = dict(c_ctx=c_ctx, w_mod=w_mod, b_mod=b_mod, norm_g=norm_g, ffn1_wi=ffn1_wi, ffn1_wo=ffn1_wo, ffn2_wi=ffn2_wi,
              ffn2_wo=ffn2_wo, w_in=w_in, w_a2_f=w_a2_f, b_a_f=b_a_f, w_a2_b=w_a2_b, b_a_b=b_a_b, sink=sink, gla_g=gla_g,
              w_out=w_out, w_pool=w_pool, pool_scale=pool_scale, final_g=final_g)
    ms = dict(c_ctx=m_c_ctx, w_mod=m_w_mod, b_mod=m_b_mod, norm_g=m_norm_g, ffn1_wi=m_ffn1_wi, ffn1_wo=m_ffn1_wo,
              ffn2_wi=m_ffn2_wi, ffn2_wo=m_ffn2_wo, w_in=m_w_in, w_a2_f=m_w_a2_f, b_a_f=m_b_a_f, w_a2_b=m_w_a2_b,
              b_a_b=m_b_a_b, sink=m_sink, gla_g=m_gla_g, w_out=m_w_out, w_pool=m_w_pool, pool_scale=m_pool_scale,
              final_g=m_final_g)
    vs = dict(c_ctx=v_c_ctx, w_mod=v_w_mod, b_mod=v_b_mod, norm_g=v_norm_g, ffn1_wi=v_ffn1_wi, ffn1_wo=v_ffn1_wo,
              ffn2_wi=v_ffn2_wi, ffn2_wo=v_ffn2_wo, w_in=v_w_in, w_a2_f=v_w_a2_f, b_a_f=v_b_a_f, w_a2_b=v_w_a2_b,
              b_a_b=v_b_a_b, sink=v_sink, gla_g=v_gla_g, w_out=v_w_out, w_pool=v_w_pool, pool_scale=v_pool_scale,
              final_g=v_final_g)
    g_all = {nm: g_all[nm].reshape(ws[nm].shape) for nm in order}
    big = ["w_mod", "ffn1_wi", "ffn1_wo", "ffn2_wi", "ffn2_wo", "w_out"]
    delta, new_m, new_v = {}, {}, {}
    for nm in big:
        shp = ws[nm].shape
        two_d = lambda a: a.reshape(-1, shp[-1])
        dl, nm_, nv_ = _adamw(two_d(ws[nm]), two_d(g_all[nm]), two_d(ms[nm]), two_d(vs[nm]), f"adamw_{nm}")
        delta[nm], new_m[nm], new_v[nm] = dl.reshape(shp), nm_.reshape(shp), nv_.reshape(shp)
    rest = [nm for nm in order if nm not in big]
    rest_shapes = [ws[nm].shape for nm in rest]
    packed = [_pack([d_[nm] for nm in rest]).reshape(-1, 128) for d_ in (ws, g_all, ms, vs)]
    pad_rows = (-packed[0].shape[0]) % 512
    packed = [jnp.pad(p, ((0, pad_rows), (0, 0))) for p in packed]
    outs = _adamw(*packed, "adamw_small")
    for dst, arr in zip((delta, new_m, new_v), outs):
        for nm, val in zip(rest, _unpack(arr.reshape(1, -1), rest_shapes)):
            dst[nm] = val[0]

    return (loss, grad_x[None], *[g_all[nm] for nm in order], *[delta[nm] for nm in order],
            *[new_m[nm] for nm in order], *[new_v[nm] for nm in order])
```

```python
import functools

import numpy as np
import jax
import jax.numpy as jnp
from jax import lax
from jax.experimental import pallas as pl
from jax.experimental.pallas import tpu as pltpu

F32 = jnp.float32
BF16 = jnp.bfloat16
HI = lax.Precision.HIGHEST
MESH = pl.DeviceIdType.MESH

N_DEV = 8
RMS_EPS = 1e-6
N_MOD = 9
GRID_W = 64
A_HEADS, A_KV_HEADS, A_HEAD_DIM = 8, 2, 64
A_REP = A_HEADS // A_KV_HEADS
WINDOW = 128
ROPE_BASE = 10000.0
B_HEADS, B_DK, B_DV = 4, 64, 128
B_GATE_RANK = 16
B_GATE_NORM = 16.0
B_CHUNK = 64
POOL_WINDOWS = (2, 4, 8, 16)
POOL_PAD = 8
A_Q = A_HEADS * A_HEAD_DIM
A_KV = A_KV_HEADS * A_HEAD_DIM
B_QK = B_HEADS * B_DK
B_V = B_HEADS * B_DV
PROJ_SIZES = (A_Q, A_KV, A_KV, B_QK, B_QK, B_V, B_V, 2 * B_GATE_RANK)
PROJ_DIM = sum(PROJ_SIZES)
ADAM_LR, ADAM_B1, ADAM_B2, ADAM_EPS, ADAM_WD, ADAM_STEP = 0.001, 0.9, 0.999, 1e-08, 0.01, 10

VMEM_LIMIT = 56 * 1024 * 1024

NN = ((1,), (0,))
NT = ((1,), (1,))
TN = ((0,), (0,))


def _dot(a, b, dims=NN, prec=None):
    return lax.dot_general(a, b, (dims, ((), ())), precision=prec, preferred_element_type=F32)


def _bdot(a, b, dims=NN):
    return _dot(a.astype(BF16), b.astype(BF16), dims)


def _params(sem=None, **kw):
    return pltpu.CompilerParams(dimension_semantics=sem, vmem_limit_bytes=VMEM_LIMIT, **kw)


def _silu(a):
    return a * jax.nn.sigmoid(a)


def _pick(n, prefs):
    for p in prefs:
        if n % p == 0:
            return p
    return n


def _full(shape):
    nd = len(shape)
    return pl.BlockSpec(shape, lambda *_: (0,) * nd)


def _peers():
    x, y, c = lax.axis_index("x"), lax.axis_index("y"), lax.axis_index("c")
    return x, y, c


def _dev_index():
    x, y, c = _peers()
    return 4 * x + 2 * y + c


def _others(x, y, c):
    return [(x, y, 1 - c), (1 - x, y, c), (x, 1 - y, c), (1 - x, 1 - y, c),
            (1 - x, y, 1 - c), (x, 1 - y, 1 - c), (1 - x, 1 - y, 1 - c)]


def _index_of(dev):
    return 4 * dev[0] + 2 * dev[1] + dev[2]


def _exchange_refs(gather, shapes, srcs, lands, a, me, to):
    if gather:
        r = shapes[a][1]
        return srcs[a], lands[a].at[:, pl.ds(_index_of(me) * r, r), :]
    r = shapes[a][1] // N_DEV
    return srcs[a].at[:, pl.ds(_index_of(to) * r, r), :], lands[a].at[_index_of(me)]


HBM_SPEC = pl.BlockSpec(memory_space=pltpu.HBM)
SEM_SPEC = pl.BlockSpec(memory_space=pltpu.SEMAPHORE)
EFFECT = pltpu.SideEffectType.DATAFLOW_SIDE_EFFECTING


def _exchange_start(srcs, lands, gather, collective_id, dep, name):
    n = len(srcs)
    shapes = [s.shape for s in srcs]
    deps = [] if dep is None else [dep]

    def body(*refs):
        src_refs, land_refs = refs[:n], refs[n:2 * n]
        send_sems, recv_sems = refs[2 * n + len(deps)], refs[2 * n + len(deps) + 1]
        token = refs[-1]
        x, y, c = _peers()
        others = _others(x, y, c)
        barrier = pltpu.get_barrier_semaphore()
        for peer in others:
            pl.semaphore_signal(barrier, inc=1, device_id=peer, device_id_type=MESH)
        pl.semaphore_wait(barrier, len(others))
        for a in range(n):
            for k, to in enumerate(others):
                src, dst = _exchange_refs(gather, shapes, src_refs, land_refs, a, (x, y, c), to)
                pltpu.make_async_remote_copy(src_ref=src, dst_ref=dst, send_sem=send_sems.at[7 * a + k],
                                             recv_sem=recv_sems.at[7 * a + k], device_id=to, device_id_type=MESH).start()
        token[...] = jnp.zeros_like(token)

    outs = pl.pallas_call(
        body, name=name,
        out_shape=(pltpu.SemaphoreType.DMA((7 * n,)), pltpu.SemaphoreType.DMA((7 * n,)),
                   *[pltpu.HBM(s.shape, s.dtype) for s in srcs], *[pltpu.HBM(l.shape, l.dtype) for l in lands],
                   jax.ShapeDtypeStruct((8, 128), F32)),
        in_specs=[HBM_SPEC] * (2 * n) + [pl.BlockSpec(memory_space=pl.ANY)] * len(deps),
        out_specs=(SEM_SPEC, SEM_SPEC, *[HBM_SPEC] * (2 * n), pl.BlockSpec(memory_space=pltpu.VMEM)),
        input_output_aliases={i: 2 + i for i in range(2 * n)},
        compiler_params=pltpu.CompilerParams(has_side_effects=EFFECT, collective_id=collective_id),
    )(*[pltpu.with_memory_space_constraint(s, pltpu.HBM) for s in srcs],
      *[pltpu.with_memory_space_constraint(l, pltpu.HBM) for l in lands], *deps)
    return outs[0], outs[1], list(outs[2:2 + n]), list(outs[2 + n:2 + 2 * n]), outs[-1]


def _exchange_wait(started, gather, after, name):
    send_sems, recv_sems, srcs, lands, _ = started
    n = len(srcs)
    shapes = [s.shape for s in srcs]

    def body(*refs):
        src_refs, land_refs = refs[:n], refs[n:2 * n]
        send_sems, recv_sems = refs[2 * n], refs[2 * n + 1]
        x, y, c = _peers()
        for a in range(n):
            for k, peer in enumerate(_others(x, y, c)):
                src, _ = _exchange_refs(gather, shapes, src_refs, land_refs, a, (x, y, c), peer)
                _, dst = _exchange_refs(gather, shapes, src_refs, land_refs, a, peer, (x, y, c))
                copy = pltpu.make_async_remote_copy(src_ref=src, dst_ref=dst, send_sem=send_sems.at[7 * a + k],
                                                    recv_sem=recv_sems.at[7 * a + k], device_id=peer, device_id_type=MESH)
                copy.wait_send()
                copy.wait_recv()

    outs = pl.pallas_call(
        body, name=name,
        out_shape=(*[pltpu.HBM(s.shape, s.dtype) for s in srcs], *[pltpu.HBM(l.shape, l.dtype) for l in lands]),
        in_specs=[HBM_SPEC] * (2 * n) + [SEM_SPEC, SEM_SPEC, pl.BlockSpec(memory_space=pl.ANY)],
        out_specs=tuple([HBM_SPEC] * (2 * n)),
        input_output_aliases={i: i for i in range(2 * n)},
        compiler_params=pltpu.CompilerParams(has_side_effects=EFFECT),
    )(*srcs, *lands, send_sems, recv_sems, after)
    return list(outs[:n]), list(outs[n:])


def _sum_slots(land, whole, me, name):
    _, a_, r, c = land.shape
    tr = _pick(r, (352, 256, 128, 64, 32, 16, 8))
    nr = r // tr

    def body(me_ref, land_ref, own_ref, out_ref):
        acc = None
        for s in range(N_DEV):
            part = jnp.where(me_ref[0] == s, own_ref[...], land_ref[s]).astype(F32)
            acc = part if acc is None else acc + part
        out_ref[...] = acc

    return pl.pallas_call(
        body, name=name,
        grid_spec=pltpu.PrefetchScalarGridSpec(
            num_scalar_prefetch=1, grid=(a_, nr),
            in_specs=[pl.BlockSpec((N_DEV, None, tr, c), lambda i, j, me_ref: (0, i, j, 0)),
                      pl.BlockSpec((None, tr, c), lambda i, j, me_ref: (i, me_ref[0] * nr + j, 0))],
            out_specs=pl.BlockSpec((None, tr, c), lambda i, j, me_ref: (i, j, 0))),
        out_shape=jax.ShapeDtypeStruct((a_, r, c), F32),
        compiler_params=_params(("parallel", "parallel")),
    )(me.reshape(1).astype(jnp.int32), land, whole)


def _gather_small(vec, name):
    p = vec.shape[1]
    pp = -(-p // 1024) * 1024
    blk = jnp.pad(vec, ((0, 0), (0, pp - p))).reshape(8, pp // 8)

    def body(in_ref, out_ref, send_sems, recv_sems):
        x, y, c = _peers()
        me = 4 * x + 2 * y + c
        others = [(x, y, 1 - c), (1 - x, y, c), (x, 1 - y, c), (1 - x, 1 - y, c),
                  (1 - x, y, 1 - c), (x, 1 - y, 1 - c), (1 - x, 1 - y, 1 - c)]

        def rows(idx):
            return out_ref.at[pl.ds(pl.multiple_of(idx * 8, 8), 8), :]

        out_ref[pl.ds(pl.multiple_of(me * 8, 8), 8), :] = in_ref[...]

        def copy(k, dev, slot):
            return pltpu.make_async_remote_copy(
                src_ref=in_ref, dst_ref=rows(slot), send_sem=send_sems.at[k], recv_sem=recv_sems.at[k],
                device_id=dev, device_id_type=MESH)

        sends = [copy(k, dev, me) for k, dev in enumerate(others)]
        for cp in sends:
            cp.start()
        for k, dev in enumerate(others):
            copy(k, dev, 4 * dev[0] + 2 * dev[1] + dev[2]).wait_recv()
        for cp in sends:
            cp.wait_send()

    vm = pl.BlockSpec(memory_space=pltpu.VMEM)
    out = pl.pallas_call(
        body, name=name, out_shape=jax.ShapeDtypeStruct((8 * N_DEV, pp // 8), F32),
        in_specs=[vm], out_specs=vm,
        scratch_shapes=[pltpu.SemaphoreType.DMA((7,)), pltpu.SemaphoreType.DMA((7,))],
        compiler_params=pltpu.CompilerParams(has_side_effects=True, vmem_limit_bytes=VMEM_LIMIT),
    )(blk)
    return out.reshape(N_DEV, pp)[:, :p]


def _sum_rows8(g, name):
    p = g.shape[1]

    def body(in_ref, out_ref):
        acc = in_ref[0:1, :]
        for s in range(1, N_DEV):
            acc = acc + in_ref[s:s + 1, :]
        out_ref[...] = acc

    return pl.pallas_call(body, name=name, out_shape=jax.ShapeDtypeStruct((1, p), F32),
                          compiler_params=_params())(g)


def _sel_row(mods_ref, is_ctx, k):
    return jnp.where(is_ctx, mods_ref[1, k:k + 1, :], mods_ref[0, k:k + 1, :])


def _modulate(z, mods, g, ks, kc, n_x, out_dtype, name):
    m, d = z.shape
    tm = _pick(m, (256, 128, 64, 32, 16, 8))

    def body(z_ref, mods_ref, g_ref, h_ref):
        is_ctx = pl.program_id(0) * tm >= n_x
        zz = z_ref[...]
        r = lax.rsqrt(jnp.mean(zz * zz, axis=-1, keepdims=True) + RMS_EPS)
        shift, scale = _sel_row(mods_ref, is_ctx, ks), _sel_row(mods_ref, is_ctx, kc)
        h_ref[...] = ((zz * r) * g_ref[...] * (1.0 + scale) + shift).astype(out_dtype)

    return pl.pallas_call(
        body, name=name, grid=(m // tm,),
        in_specs=[pl.BlockSpec((tm, d), lambda i: (i, 0)), _full(mods.shape), _full(g.shape)],
        out_specs=pl.BlockSpec((tm, d), lambda i: (i, 0)),
        out_shape=jax.ShapeDtypeStruct((m, d), out_dtype),
        compiler_params=_params(("parallel",)),
    )(z, mods, g)


def _modulate_bwd(z, dh, dres, mods, g, kc, n_x, name):
    m, d = z.shape
    tm = _pick(m, (256, 128, 64, 32, 16, 8))
    first_ctx = n_x // tm

    def body(z_ref, dh_ref, dres_ref, mods_ref, g_ref, dx_ref, acc_ref):
        i = pl.program_id(0)
        is_ctx = i * tm >= n_x

        @pl.when((i == 0) | (i == first_ctx))
        def _():
            acc_ref[...] = jnp.zeros_like(acc_ref)

        zz, dhh = z_ref[...], dh_ref[...]
        r = lax.rsqrt(jnp.mean(zz * zz, axis=-1, keepdims=True) + RMS_EPS)
        nz = zz * r
        gain = g_ref[...] * (1.0 + _sel_row(mods_ref, is_ctx, kc))
        dn = dhh * gain
        dz = r * (dn - nz * jnp.mean(dn * nz, axis=-1, keepdims=True))
        dx_ref[...] = dres_ref[...] + dz
        acc_ref[0:1, :] += jnp.sum(dhh, axis=0, keepdims=True)
        acc_ref[1:2, :] += jnp.sum(dhh * nz, axis=0, keepdims=True)

    row = pl.BlockSpec((tm, d), lambda i: (i, 0))
    return pl.pallas_call(
        body, name=name, grid=(m // tm,),
        in_specs=[row, row, row, _full(mods.shape), _full(g.shape)],
        out_specs=[row, pl.BlockSpec((None, 8, d), lambda i: ((i * tm >= n_x).astype(jnp.int32), 0, 0))],
        out_shape=[jax.ShapeDtypeStruct((m, d), F32), jax.ShapeDtypeStruct((2, 8, d), F32)],
        compiler_params=_params(("arbitrary",)),
    )(z, dh, dres, mods, g)


def _ffn_up(h, wi_t, layer, name):
    m, d = h.shape
    f = wi_t.shape[1] // 2
    tm = _pick(m, (256, 128, 64, 32, 16, 8))

    def body(h_ref, w_ref, au_ref, act_ref):
        hh = h_ref[...]
        a = _dot(hh, w_ref[0:f, :], NT)
        u = _dot(hh, w_ref[f:2 * f, :], NT)
        au_ref[:, 0:f] = a.astype(BF16)
        au_ref[:, f:2 * f] = u.astype(BF16)
        act_ref[...] = (_silu(a) * u).astype(BF16)

    return pl.pallas_call(
        body, name=name, grid=(m // tm,),
        in_specs=[pl.BlockSpec((tm, d), lambda i: (i, 0)),
                  pl.BlockSpec((None, 2 * f, d), lambda i: (layer, 0, 0))],
        out_specs=[pl.BlockSpec((tm, 2 * f), lambda i: (i, 0)), pl.BlockSpec((tm, f), lambda i: (i, 0))],
        out_shape=[jax.ShapeDtypeStruct((m, 2 * f), BF16), jax.ShapeDtypeStruct((m, f), BF16)],
        compiler_params=_params(("parallel",)),
    )(h, wi_t)


def _mm_resid(a, b, layer, res, mods, km, coef, n_x, name):
    m, k = a.shape
    n = b.shape[2]
    tm = _pick(m, (512, 256, 128, 64, 32, 16, 8))
    tn = _pick(n, (512, 256, 128))

    def body(a_ref, b_ref, res_ref, mods_ref, out_ref, y_ref):
        is_ctx = pl.program_id(1) * tm >= n_x
        y = _dot(a_ref[...], b_ref[...])
        y_ref[...] = y
        out_ref[...] = res_ref[...] + coef * _sel_row(mods_ref, is_ctx, km) * y

    tile = pl.BlockSpec((tm, tn), lambda j, i: (i, j))
    return pl.pallas_call(
        body, name=name, grid=(n // tn, m // tm),
        in_specs=[pl.BlockSpec((tm, k), lambda j, i: (i, 0)),
                  pl.BlockSpec((None, k, tn), lambda j, i: (layer, 0, j)),
                  tile, pl.BlockSpec((2, 16, tn), lambda j, i: (0, 0, j))],
        out_specs=[tile, tile],
        out_shape=[jax.ShapeDtypeStruct((m, n), F32), jax.ShapeDtypeStruct((m, n), F32)],
        compiler_params=_params(("parallel", "parallel")),
    )(a, b, res, mods)


def _resid_bwd(dx, y, mods, km, coef, n_x, name, dep=None):
    m, d = dx.shape
    tm = _pick(m, (256, 128, 64, 32, 16, 8))
    first_ctx = n_x // tm
    deps = [] if dep is None else [dep]

    def body(dx_ref, y_ref, mods_ref, *rest):
        dy_ref, acc_ref = rest[-2:]
        i = pl.program_id(0)
        is_ctx = i * tm >= n_x

        @pl.when((i == 0) | (i == first_ctx))
        def _():
            acc_ref[...] = jnp.zeros_like(acc_ref)

        dxx = dx_ref[...]
        dy_ref[...] = (coef * _sel_row(mods_ref, is_ctx, km) * dxx).astype(BF16)
        acc_ref[0:1, :] += jnp.sum(coef * y_ref[...] * dxx, axis=0, keepdims=True)

    row = pl.BlockSpec((tm, d), lambda i: (i, 0))
    return pl.pallas_call(
        body, name=name, grid=(m // tm,),
        in_specs=[row, row, _full(mods.shape)] + [pl.BlockSpec(memory_space=pl.ANY)] * len(deps),
        out_specs=[row, pl.BlockSpec((None, 8, d), lambda i: ((i * tm >= n_x).astype(jnp.int32), 0, 0))],
        out_shape=[jax.ShapeDtypeStruct((m, d), BF16), jax.ShapeDtypeStruct((2, 8, d), F32)],
        compiler_params=_params(("arbitrary",)),
    )(dx, y, mods, *deps)


def _ffn_down_bwd(dy, wo, layer, au, name):
    m, d = dy.shape
    f = wo.shape[1]
    tm = _pick(m, (256, 128, 64, 32, 16, 8))

    def body(dy_ref, wo_ref, au_ref, dau_ref):
        dact = _dot(dy_ref[...], wo_ref[...], NT)
        aa, uu = au_ref[:, 0:f].astype(F32), au_ref[:, f:2 * f].astype(F32)
        sg = jax.nn.sigmoid(aa)
        dau_ref[:, 0:f] = (dact * uu * (sg * (1.0 + aa * (1.0 - sg)))).astype(BF16)
        dau_ref[:, f:2 * f] = (dact * (aa * sg)).astype(BF16)

    wide = pl.BlockSpec((tm, 2 * f), lambda i: (i, 0))
    return pl.pallas_call(
        body, name=name, grid=(m // tm,),
        in_specs=[pl.BlockSpec((tm, d), lambda i: (i, 0)), pl.BlockSpec((None, f, d), lambda i: (layer, 0, 0)), wide],
        out_specs=wide, out_shape=jax.ShapeDtypeStruct((m, 2 * f), BF16),
        compiler_params=_params(("parallel",)),
    )(dy, wo, au)


def _mm(terms, dims, n, out_dtype, name, tm_pref=(512, 256, 128, 64, 32, 16, 8), tn_pref=(512, 256, 128)):
    m = terms[0][0].shape[0]
    tm = _pick(m, tm_pref)
    tn = _pick(n, tn_pref)
    nt = len(terms)

    def body(*refs):
        out_ref = refs[2 * nt]
        acc = None
        for t in range(nt):
            part = _dot(refs[2 * t][...].astype(BF16), refs[2 * t + 1][...].astype(BF16), dims)
            acc = part if acc is None else acc + part
        out_ref[...] = acc.astype(out_dtype)

    in_specs, args = [], []
    for a, b, layer, rb in terms:
        k = a.shape[1]
        in_specs.append(pl.BlockSpec((tm, k), lambda j, i: (i, 0)))
        if dims == NN:
            in_specs.append(pl.BlockSpec((None, k, tn), lambda j, i, layer=layer, rb=rb: (layer, rb, j)))
        else:
            nb = n // tn
            in_specs.append(pl.BlockSpec((None, tn, k), lambda j, i, layer=layer, rb=rb, nb=nb: (layer, rb * nb + j, 0)))
        args += [a, b]
    return pl.pallas_call(
        body, name=name, grid=(n // tn, m // tm), in_specs=in_specs,
        out_specs=pl.BlockSpec((tm, tn), lambda j, i: (i, j)),
        out_shape=jax.ShapeDtypeStruct((m, n), out_dtype),
        compiler_params=_params(("parallel", "parallel")),
    )(*args)


def _mm_tn(a, b, out_dtype, name, rows=None):
    t = rows if rows is not None else a.shape[0]
    m, n = a.shape[1], b.shape[1]
    tm = _pick(m, (1408, 2432, 1024, 512, 256, 128))
    tn = _pick(n, (1024, 512, 256, 128))
    tk = _pick(t, (512, 256, 128, 64, 32, 16, 8))

    def body(a_ref, b_ref, out_ref, acc_ref):
        kk = pl.program_id(2)

        @pl.when(kk == 0)
        def _():
            acc_ref[...] = jnp.zeros_like(acc_ref)

        acc_ref[...] += _dot(a_ref[...].astype(BF16), b_ref[...].astype(BF16), TN)

        @pl.when(kk == pl.num_programs(2) - 1)
        def _():
            out_ref[...] = acc_ref[...].astype(out_dtype)

    return pl.pallas_call(
        body, name=name, grid=(m // tm, n // tn, t // tk),
        in_specs=[pl.BlockSpec((tk, tm), lambda i, j, k: (k, i)), pl.BlockSpec((tk, tn), lambda i, j, k: (k, j))],
        out_specs=pl.BlockSpec((tm, tn), lambda i, j, k: (i, j)),
        out_shape=jax.ShapeDtypeStruct((m, n), out_dtype),
        scratch_shapes=[pltpu.VMEM((tm, tn), F32)],
        compiler_params=_params(("parallel", "parallel", "arbitrary")),
    )(a, b)


def _final_loss(x, g, target, name):
    t, d = x.shape
    tm = _pick(t, (256, 128, 64, 32, 16, 8))

    def body(x_ref, g_ref, t_ref, dx_ref, loss_ref, dg_ref):
        @pl.when(pl.program_id(0) == 0)
        def _():
            loss_ref[...] = jnp.zeros_like(loss_ref)
            dg_ref[...] = jnp.zeros_like(dg_ref)

        xx, gg = x_ref[...], g_ref[...]
        r = lax.rsqrt(jnp.mean(xx * xx, axis=-1, keepdims=True) + RMS_EPS)
        nz = xx * r
        err = nz * gg - t_ref[...]
        loss_ref[...] += jnp.sum(err * err, axis=0, keepdims=True) * (0.5 / d)
        dout = err * (1.0 / d)
        dg_ref[...] += jnp.sum(dout * nz, axis=0, keepdims=True)
        dn = dout * gg
        dx_ref[...] = r * (dn - nz * jnp.mean(dn * nz, axis=-1, keepdims=True))

    row = pl.BlockSpec((tm, d), lambda i: (i, 0))
    vec = pl.BlockSpec((1, d), lambda i: (0, 0))
    return pl.pallas_call(
        body, name=name, grid=(t // tm,), in_specs=[row, vec, row], out_specs=[row, vec, vec],
        out_shape=[jax.ShapeDtypeStruct((t, d), F32), jax.ShapeDtypeStruct((1, d), F32),
                   jax.ShapeDtypeStruct((1, d), F32)],
        compiler_params=_params(("arbitrary",)),
    )(x, g, target)


def _adaln_fwd(craw, w_mod, b_cols, name):
    lyr, d, nc = w_mod.shape

    def body(c_ref, w_ref, b_ref, out_ref):
        out_ref[...] = _bdot(_silu(c_ref[...]), w_ref[...]) + b_ref[...]

    return pl.pallas_call(
        body, name=name, grid=(lyr,),
        in_specs=[_full(craw.shape), pl.BlockSpec((None, d, nc), lambda l: (l, 0, 0)),
                  pl.BlockSpec((None, 1, nc), lambda l: (l, 0, 0))],
        out_specs=pl.BlockSpec((None, 16, nc), lambda l: (l, 0, 0)),
        out_shape=jax.ShapeDtypeStruct((lyr, 16, nc), F32),
        compiler_params=_params(("parallel",)),
    )(craw, w_mod, b_cols)


def _adaln_bwd(craw, cs_t, dmm_cols, w_mod, name):
    lyr, d, nc = w_mod.shape

    def body(c_ref, cst_ref, dmm_ref, w_ref, gw_ref, dc_ref):
        dmm = dmm_ref[...]
        gw_ref[...] = _bdot(cst_ref[...], dmm)
        cc = c_ref[...]
        sg = jax.nn.sigmoid(cc)
        dc_ref[...] = _bdot(dmm, w_ref[...], NT) * (sg * (1.0 + cc * (1.0 - sg)))

    wspec = pl.BlockSpec((None, d, nc), lambda l: (l, 0, 0))
    return pl.pallas_call(
        body, name=name, grid=(lyr,),
        in_specs=[_full(craw.shape), _full(cs_t.shape), pl.BlockSpec((None, 16, nc), lambda l: (l, 0, 0)), wspec],
        out_specs=[wspec, pl.BlockSpec((None, 16, d), lambda l: (l, 0, 0))],
        out_shape=[jax.ShapeDtypeStruct((lyr, d, nc), F32), jax.ShapeDtypeStruct((lyr, 16, d), F32)],
        compiler_params=_params(("parallel",)),
    )(craw, cs_t, dmm_cols, w_mod)


def _rope_tables(t):
    rows = jnp.repeat(jnp.arange(t // GRID_W, dtype=F32), GRID_W)
    cols = jnp.tile(jnp.arange(GRID_W, dtype=F32), t // GRID_W)
    n = A_HEAD_DIM // 4
    freqs = ROPE_BASE ** (-jnp.arange(n, dtype=F32) / n)
    cr, sr = jnp.cos(rows[:, None] * freqs), jnp.sin(rows[:, None] * freqs)
    cc, sc = jnp.cos(cols[:, None] * freqs), jnp.sin(cols[:, None] * freqs)
    cos = jnp.concatenate([cr, cr, cc, cc] * 2, axis=-1)
    sin = jnp.concatenate([-sr, sr, -sc, sc] * 2, axis=-1)
    return cos, sin


def _rope(xt, cos, sin, adjoint, name):
    t, w = xt.shape
    tb = _pick(t, (512, 256, 128))
    rep = w // cos.shape[1]

    def body(x_ref, c_ref, s_ref, o_ref):
        xx = x_ref[...]
        cc = jnp.concatenate([c_ref[...]] * rep, axis=1) if rep > 1 else c_ref[...]
        ss = jnp.concatenate([s_ref[...]] * rep, axis=1) if rep > 1 else s_ref[...]
        low = (lax.broadcasted_iota(jnp.int32, xx.shape, 1) % 32) < 16

        def partner(v):
            return jnp.where(low, pltpu.roll(v, w - 16, 1), pltpu.roll(v, 16, 1))

        if adjoint:
            o_ref[...] = xx * cc + partner(xx * ss)
        else:
            o_ref[...] = xx * cc + partner(xx) * ss

    blk = pl.BlockSpec((tb, w), lambda i: (i, 0))
    tab = pl.BlockSpec((tb, cos.shape[1]), lambda i: (i, 0))
    return pl.pallas_call(
        body, name=name, grid=(t // tb,), in_specs=[blk, tab, tab], out_specs=blk,
        out_shape=jax.ShapeDtypeStruct((t, w), F32), compiler_params=_params(("parallel",)),
    )(xt, cos, sin)


def _attn_probs(q, kb, kc, sink, n, t):
    scale = A_HEAD_DIM ** -0.5
    s1 = _bdot(q, kb, NT) * scale
    s2 = _bdot(q, kc, NT) * scale
    qpos = n * WINDOW + lax.broadcasted_iota(jnp.int32, s1.shape, 0) % WINDOW
    kpos = (n - 1) * WINDOW + lax.broadcasted_iota(jnp.int32, s1.shape, 1)
    valid = (kpos >= 0) & (kpos < t) & (jnp.abs(kpos - qpos) <= WINDOW)
    s1 = jnp.where(valid, s1, -jnp.inf)
    mx = jnp.maximum(jnp.maximum(jnp.max(s1, axis=-1, keepdims=True), jnp.max(s2, axis=-1, keepdims=True)), sink)
    p1, p2, ps = jnp.exp(s1 - mx), jnp.exp(s2 - mx), jnp.exp(sink - mx)
    inv = 1.0 / (jnp.sum(p1, axis=-1, keepdims=True) + jnp.sum(p2, axis=-1, keepdims=True) + ps)
    return p1 * inv, p2 * inv, ps * inv


def _sink_rows(sink_ref):
    return jnp.concatenate([jnp.broadcast_to(sink_ref[r], (WINDOW, 1)) for r in range(A_REP)], axis=0)


def _attn_fwd(q, kp, vp, kc, vc, sink, name):
    hq, t, dh = q.shape
    nb = t // WINDOW
    lc = kc.shape[1]
    rows = A_REP * WINDOW

    def body(q_ref, k_ref, v_ref, kc_ref, vc_ref, sink_ref, o_ref):
        n = pl.program_id(1)
        start = pl.multiple_of(n * WINDOW, WINDOW)
        kb, vb = k_ref[pl.ds(start, 3 * WINDOW), :], v_ref[pl.ds(start, 3 * WINDOW), :]
        p1, p2, _ = _attn_probs(q_ref[...].reshape(rows, dh), kb, kc_ref[...], _sink_rows(sink_ref), n, t)
        o_ref[...] = (_bdot(p1, vb) + _bdot(p2, vc_ref[...])).reshape(A_REP, WINDOW, dh)

    qblk = pl.BlockSpec((A_REP, WINDOW, dh), lambda g, n: (g, n, 0))
    kfull = pl.BlockSpec((None, t + 2 * WINDOW, dh), lambda g, n: (g, 0, 0))
    cfull = pl.BlockSpec((None, lc, dh), lambda g, n: (g, 0, 0))
    return pl.pallas_call(
        body, name=name, grid=(hq // A_REP, nb),
        in_specs=[qblk, kfull, kfull, cfull, cfull, pl.BlockSpec((A_REP, 1, 1), lambda g, n: (g, 0, 0))],
        out_specs=qblk, out_shape=jax.ShapeDtypeStruct((hq, t, dh), F32),
        compiler_params=_params(("parallel", "parallel")),
    )(q, kp, vp, kc, vc, sink)


def _attn_bwd(q, kp, vp, kc, vc, sink, o, do, name):
    hq, t, dh = q.shape
    nb = t // WINDOW
    lc = kc.shape[1]
    scale = A_HEAD_DIM ** -0.5
    rows = A_REP * WINDOW

    def body(q_ref, k_ref, v_ref, kc_ref, vc_ref, sink_ref, o_ref, do_ref,
             dq_ref, dk_ref, dv_ref, dkc_ref, dvc_ref, dsink_ref):
        n = pl.program_id(1)

        @pl.when(n == 0)
        def _():
            dk_ref[...] = jnp.zeros_like(dk_ref)
            dv_ref[...] = jnp.zeros_like(dv_ref)
            dkc_ref[...] = jnp.zeros_like(dkc_ref)
            dvc_ref[...] = jnp.zeros_like(dvc_ref)
            dsink_ref[...] = jnp.zeros_like(dsink_ref)

        start = pl.multiple_of(n * WINDOW, WINDOW)
        band = pl.ds(start, 3 * WINDOW)
        qq, kb, vb, kcc, vcc = q_ref[...].reshape(rows, dh), k_ref[band, :], v_ref[band, :], kc_ref[...], vc_ref[...]
        p1, p2, ps = _attn_probs(qq, kb, kcc, _sink_rows(sink_ref), n, t)
        dout = do_ref[...].reshape(rows, dh)
        delta = jnp.sum(dout * o_ref[...].reshape(rows, dh), axis=-1, keepdims=True)
        ds1 = p1 * (_bdot(dout, vb, NT) - delta)
        ds2 = p2 * (_bdot(dout, vcc, NT) - delta)
        dq_ref[...] = ((_bdot(ds1, kb) + _bdot(ds2, kcc)) * scale).reshape(A_REP, WINDOW, dh)
        dk_ref[band, :] += _bdot(ds1.T, qq) * scale
        dv_ref[band, :] += _bdot(p1.T, dout)
        dkc_ref[...] += _bdot(ds2.T, qq) * scale
        dvc_ref[...] += _bdot(p2.T, dout)
        dsink_ref[...] += jnp.sum((-ps * delta).reshape(A_REP, WINDOW, 1), axis=1, keepdims=True)

    qblk = pl.BlockSpec((A_REP, WINDOW, dh), lambda g, n: (g, n, 0))
    kfull = pl.BlockSpec((None, t + 2 * WINDOW, dh), lambda g, n: (g, 0, 0))
    cfull = pl.BlockSpec((None, lc, dh), lambda g, n: (g, 0, 0))
    return pl.pallas_call(
        body, name=name, grid=(hq // A_REP, nb),
        in_specs=[qblk, kfull, kfull, cfull, cfull, pl.BlockSpec((A_REP, 1, 1), lambda g, n: (g, 0, 0)), qblk, qblk],
        out_specs=[qblk, kfull, kfull, cfull, cfull, pl.BlockSpec((A_REP, 8, 128), lambda g, n: (g, 0, 0))],
        out_shape=[jax.ShapeDtypeStruct(q.shape, F32), jax.ShapeDtypeStruct(kp.shape, F32),
                   jax.ShapeDtypeStruct(kp.shape, F32), jax.ShapeDtypeStruct(kc.shape, F32),
                   jax.ShapeDtypeStruct(kc.shape, F32), jax.ShapeDtypeStruct((hq, 8, 128), F32)],
        compiler_params=_params(("parallel", "arbitrary")),
    )(q, kp, vp, kc, vc, sink, o, do)


def _gate_fwd(zg, w2, b2, name):
    m = zg.shape[0]
    n = w2.shape[1]
    tm = _pick(m, (512, 256, 128, 64, 32, 16, 8))

    def body(z_ref, w_ref, b_ref, o_ref):
        o_ref[...] = jax.nn.log_sigmoid(_bdot(z_ref[...], w_ref[...]) + b_ref[...]) / B_GATE_NORM

    return pl.pallas_call(
        body, name=name, grid=(m // tm,),
        in_specs=[pl.BlockSpec((tm, zg.shape[1]), lambda i: (i, 0)), _full(w2.shape), _full(b2.shape)],
        out_specs=pl.BlockSpec((tm, n), lambda i: (i, 0)), out_shape=jax.ShapeDtypeStruct((m, n), F32),
        compiler_params=_params(("parallel",)),
    )(zg, w2, b2)


def _gate_bwd(zg, w2, b2, dla, name):
    m, rk = zg.shape
    n = w2.shape[1]
    tm = _pick(m, (512, 256, 128, 64, 32, 16, 8))

    def body(z_ref, w_ref, b_ref, d_ref, dz_ref, dw_ref, db_ref):
        @pl.when(pl.program_id(0) == 0)
        def _():
            dw_ref[...] = jnp.zeros_like(dw_ref)
            db_ref[...] = jnp.zeros_like(db_ref)

        zz, ww = z_ref[...], w_ref[...]
        pre = _bdot(zz, ww) + b_ref[...]
        dpre = d_ref[...] * (1.0 / B_GATE_NORM) * jax.nn.sigmoid(-pre)
        dz_ref[...] = _bdot(dpre, ww, NT)
        dw_ref[...] += _bdot(zz.T, dpre)
        db_ref[...] += jnp.sum(dpre, axis=0, keepdims=True)

    return pl.pallas_call(
        body, name=name, grid=(m // tm,),
        in_specs=[pl.BlockSpec((tm, rk), lambda i: (i, 0)), _full(w2.shape), _full(b2.shape),
                  pl.BlockSpec((tm, n), lambda i: (i, 0))],
        out_specs=[pl.BlockSpec((tm, rk), lambda i: (i, 0)), _full(w2.shape), _full(b2.shape)],
        out_shape=[jax.ShapeDtypeStruct((m, rk), F32), jax.ShapeDtypeStruct(w2.shape, F32),
                   jax.ShapeDtypeStruct(b2.shape, F32)],
        compiler_params=_params(("arbitrary",)),
    )(zg, w2, b2, dla)


def _chunk_order(step, n_x_chunks, n_chunks, reverse):
    n_c = n_chunks - n_x_chunks
    if reverse:
        return jnp.where(step < n_c, n_chunks - 1 - step, n_chunks - 1 - step)
    return jnp.where(step < n_c, n_x_chunks + step, step - n_c)


def _tri(reverse, transpose=False):
    i = lax.broadcasted_iota(jnp.int32, (B_CHUNK, B_CHUNK), 0)
    j = lax.broadcasted_iota(jnp.int32, (B_CHUNK, B_CHUNK), 1)
    if transpose:
        i, j = j, i
    return (j >= i) if reverse else (j <= i)


def _gla_chunk(q, k, la, reverse):
    g = _dot(_tri(reverse).astype(F32), la, NN, HI)
    last = 0 if reverse else B_CHUNK - 1
    gl = g[last:last + 1, :]
    eg, eng, egl = jnp.exp(g), jnp.exp(-g), jnp.exp(gl - g)
    decay_col = jnp.exp(jnp.sum(la.T, axis=1, keepdims=True))
    return q * (B_DK ** -0.5) * eg, k * eng, k * egl, eg, eng, egl, decay_col


def _gla_fwd(q, k, v, la, n_x, reverse, name):
    hh, tc, dk = q.shape
    dv = v.shape[2]
    nc, nxc = tc // B_CHUNK, n_x // B_CHUNK
    order = functools.partial(_chunk_order, n_x_chunks=nxc, n_chunks=nc, reverse=reverse)

    def body(q_ref, k_ref, v_ref, la_ref, o_ref, s_save_ref, s_ref):
        @pl.when(pl.program_id(0) == 0)
        def _():
            s_ref[...] = jnp.zeros_like(s_ref)

        mask = _tri(reverse)
        for h in range(hh):
            qt, kt, ke, _, _, _, decay_col = _gla_chunk(q_ref[h], k_ref[h], la_ref[h], reverse)
            vv, s_prev = v_ref[h], s_ref[h]
            att = jnp.where(mask, _bdot(qt, kt, NT), 0.0)
            o_ref[h] = _bdot(att, vv) + _bdot(qt, s_prev)
            s_save_ref[h] = s_prev
            s_ref[h] = decay_col * s_prev + _bdot(ke.T, vv)

    blk = lambda d: pl.BlockSpec((hh, B_CHUNK, d), lambda s: (0, order(s), 0))
    return pl.pallas_call(
        body, name=name, grid=(nc,),
        in_specs=[blk(dk), blk(dk), blk(dv), blk(dk)],
        out_specs=[blk(dv), pl.BlockSpec((None, hh, dk, dv), lambda s: (order(s), 0, 0, 0))],
        out_shape=[jax.ShapeDtypeStruct((hh, tc, dv), F32), jax.ShapeDtypeStruct((nc, hh, dk, dv), F32)],
        scratch_shapes=[pltpu.VMEM((hh, dk, dv), F32)],
        compiler_params=_params(("arbitrary",)),
    )(q, k, v, la)


def _gla_bwd(q, k, v, la, s_saved, do, n_x, reverse, name):
    hh, tc, dk = q.shape
    dv = v.shape[2]
    nc, nxc = tc // B_CHUNK, n_x // B_CHUNK
    order = lambda s: _chunk_order(nc - 1 - s, nxc, nc, reverse)
    last = 0 if reverse else B_CHUNK - 1

    def body(q_ref, k_ref, v_ref, la_ref, s_save_ref, do_ref, dq_ref, dk_ref, dv_ref, dla_ref, ds_ref):
        @pl.when(pl.program_id(0) == 0)
        def _():
            ds_ref[...] = jnp.zeros_like(ds_ref)

        mask = _tri(reverse)
        tri_t = _tri(reverse, transpose=True).astype(F32)
        is_last = lax.broadcasted_iota(jnp.int32, (B_CHUNK, dk), 0) == last
        for h in range(hh):
            qt, kt, ke, eg, eng, egl, decay_col = _gla_chunk(q_ref[h], k_ref[h], la_ref[h], reverse)
            vv, s_prev, dout, ds_new = v_ref[h], s_save_ref[h], do_ref[h], ds_ref[h]
            att = jnp.where(mask, _bdot(qt, kt, NT), 0.0)
            datt = jnp.where(mask, _bdot(dout, vv, NT), 0.0)
            dv_ref[h] = _bdot(att.T, dout) + _bdot(ke, ds_new)
            dqt = _bdot(datt, kt) + _bdot(dout, s_prev, NT)
            dkt = _bdot(datt.T, qt)
            dke = _bdot(vv, ds_new, NT)
            ddecay_row = jnp.sum((ds_new * s_prev).T, axis=0, keepdims=True)
            decay_row = jnp.exp(jnp.sum(la_ref[h], axis=0, keepdims=True))
            ds_ref[h] = decay_col * ds_new + _bdot(qt.T, dout)
            dq_ref[h] = dqt * (B_DK ** -0.5) * eg
            dk_ref[h] = dkt * eng + dke * egl
            dgl = jnp.sum(dke * ke, axis=0, keepdims=True) + ddecay_row * decay_row
            dg = dqt * qt - dkt * kt - dke * ke + jnp.where(is_last, dgl, 0.0)
            dla_ref[h] = _dot(tri_t, dg, NN, HI)

    blk = lambda d: pl.BlockSpec((hh, B_CHUNK, d), lambda s: (0, order(s), 0))
    return pl.pallas_call(
        body, name=name, grid=(nc,),
        in_specs=[blk(dk), blk(dk), blk(dv), blk(dk),
                  pl.BlockSpec((None, hh, dk, dv), lambda s: (order(s), 0, 0, 0)), blk(dv)],
        out_specs=[blk(dk), blk(dk), blk(dv), blk(dk)],
        out_shape=[jax.ShapeDtypeStruct((hh, tc, dk), F32), jax.ShapeDtypeStruct((hh, tc, dk), F32),
                   jax.ShapeDtypeStruct((hh, tc, dv), F32), jax.ShapeDtypeStruct((hh, tc, dk), F32)],
        scratch_shapes=[pltpu.VMEM((hh, dk, dv), F32)],
        compiler_params=_params(("arbitrary",)),
    )(q, k, v, la, s_saved, do)


def _gla_out_fwd(o_f, o_b, r, g, name):
    hh, t, dv = o_f.shape
    tb = _pick(t, (256, 128, 64))

    def body(of_ref, ob_ref, r_ref, g_ref, out_ref):
        for h in range(hh):
            o = of_ref[h] + ob_ref[h]
            rs = lax.rsqrt(jnp.mean(o * o, axis=-1, keepdims=True) + RMS_EPS)
            out_ref[:, h * dv:(h + 1) * dv] = (o * rs) * g_ref[...] * _silu(r_ref[:, h * dv:(h + 1) * dv])

    oblk = pl.BlockSpec((hh, tb, dv), lambda i: (0, i, 0))
    rblk = pl.BlockSpec((tb, hh * dv), lambda i: (i, 0))
    return pl.pallas_call(
        body, name=name, grid=(t // tb,), in_specs=[oblk, oblk, rblk, _full(g.shape)], out_specs=rblk,
        out_shape=jax.ShapeDtypeStruct((t, hh * dv), F32), compiler_params=_params(("parallel",)),
    )(o_f, o_b, r, g)


def _gla_out_bwd(o_f, o_b, r, g, dout, name):
    hh, t, dv = o_f.shape
    tb = _pick(t, (256, 128, 64))

    def body(of_ref, ob_ref, r_ref, g_ref, d_ref, do_ref, dr_ref, dg_ref):
        @pl.when(pl.program_id(0) == 0)
        def _():
            dg_ref[...] = jnp.zeros_like(dg_ref)

        gg = g_ref[...]
        for h in range(hh):
            cols = slice(h * dv, (h + 1) * dv)
            o = of_ref[h] + ob_ref[h]
            rs = lax.rsqrt(jnp.mean(o * o, axis=-1, keepdims=True) + RMS_EPS)
            nz = o * rs
            rr, dd = r_ref[:, cols], d_ref[:, cols]
            sg = jax.nn.sigmoid(rr)
            dr_ref[:, cols] = dd * nz * gg * (sg * (1.0 + rr * (1.0 - sg)))
            dy = dd * (rr * sg)
            dg_ref[...] += jnp.sum(dy * nz, axis=0, keepdims=True)
            dn = dy * gg
            do_ref[h] = rs * (dn - nz * jnp.mean(dn * nz, axis=-1, keepdims=True))

    oblk = pl.BlockSpec((hh, tb, dv), lambda i: (0, i, 0))
    rblk = pl.BlockSpec((tb, hh * dv), lambda i: (i, 0))
    return pl.pallas_call(
        body, name=name, grid=(t // tb,), in_specs=[oblk, oblk, rblk, _full(g.shape), rblk],
        out_specs=[oblk, rblk, _full(g.shape)],
        out_shape=[jax.ShapeDtypeStruct(o_f.shape, F32), jax.ShapeDtypeStruct(r.shape, F32),
                   jax.ShapeDtypeStruct(g.shape, F32)],
        compiler_params=_params(("arbitrary",)),
    )(o_f, o_b, r, g, dout)


def _pool_band(half, tb, adjoint):
    r = lax.broadcasted_iota(jnp.int32, (tb, tb + 2 * POOL_PAD), 0) + POOL_PAD
    j = lax.broadcasted_iota(jnp.int32, (tb, tb + 2 * POOL_PAD), 1)
    if adjoint:
        return ((j > r - half) & (j <= r + half)).astype(F32)
    return ((j >= r - half) & (j < r + half)).astype(F32)


def _pool_count(pos, half, t):
    return (jnp.minimum(pos + half, t) - jnp.maximum(pos - half, 0)).astype(F32)


def _pool_fwd(hp, w_pool, pool_scale, res, mods, km, name):
    t, d = res.shape
    ng, gw = w_pool.shape[0], w_pool.shape[1]
    tb = _pick(t, (256, 128, 64))

    def body(hp_ref, w_ref, ps_ref, res_ref, mods_ref, out_ref, pooled_ref, ypre_ref):
        gi, i = pl.program_id(0), pl.program_id(1)
        half = jnp.left_shift(1, gi)
        win = hp_ref[pl.ds(pl.multiple_of(i * tb, tb), tb + 2 * POOL_PAD), :]
        total = _dot(_pool_band(half, tb, False), win, NN, HI)
        pos = i * tb + lax.broadcasted_iota(jnp.int32, (tb, 1), 0)
        pooled = total / _pool_count(pos, half, t) - win[POOL_PAD:POOL_PAD + tb, :]
        ypre = _bdot(pooled, w_ref[...])
        pooled_ref[...] = pooled.astype(BF16)
        ypre_ref[...] = ypre
        out_ref[...] = res_ref[...] + mods_ref[0, km:km + 1, :] * (ypre * ps_ref[...])

    tile = pl.BlockSpec((tb, gw), lambda gi, i: (i, gi))
    return pl.pallas_call(
        body, name=name, grid=(ng, t // tb),
        in_specs=[pl.BlockSpec((t + 2 * POOL_PAD, gw), lambda gi, i: (0, gi)),
                  pl.BlockSpec((None, gw, gw), lambda gi, i: (gi, 0, 0)),
                  pl.BlockSpec((1, gw), lambda gi, i: (0, gi)), tile,
                  pl.BlockSpec((2, 16, gw), lambda gi, i: (0, 0, gi))],
        out_specs=[tile, tile, tile],
        out_shape=[jax.ShapeDtypeStruct((t, d), F32), jax.ShapeDtypeStruct((t, d), BF16),
                   jax.ShapeDtypeStruct((t, d), F32)],
        compiler_params=_params(("parallel", "parallel")),
    )(hp, w_pool, pool_scale, res, mods)


def _pool_bwd(dxp, w_pool, pool_scale, pooled, ypre, mods, km, name):
    t, d = pooled.shape
    ng, gw = w_pool.shape[0], w_pool.shape[1]
    tb = _pick(t, (256, 128, 64))

    def body(dxp_ref, w_ref, ps_ref, pooled_ref, ypre_ref, mods_ref, dh_ref, dw_ref, acc_ref):
        gi, i = pl.program_id(0), pl.program_id(1)

        @pl.when(i == 0)
        def _():
            dw_ref[...] = jnp.zeros_like(dw_ref)
            acc_ref[...] = jnp.zeros_like(acc_ref)

        half = jnp.left_shift(1, gi)
        mod, ps = mods_ref[0, km:km + 1, :], ps_ref[...]
        dwin = dxp_ref[pl.ds(pl.multiple_of(i * tb, tb), tb + 2 * POOL_PAD), :]
        dpooled = _bdot(dwin * (mod * ps), w_ref[...], NT)
        pos = i * tb - POOL_PAD + lax.broadcasted_iota(jnp.int32, (tb + 2 * POOL_PAD, 1), 0)
        spread = _dot(_pool_band(half, tb, True), dpooled / jnp.maximum(_pool_count(pos, half, t), 1.0), NN, HI)
        dh_ref[...] = spread - dpooled[POOL_PAD:POOL_PAD + tb, :]
        dxc, yp = dwin[POOL_PAD:POOL_PAD + tb, :], ypre_ref[...]
        dw_ref[...] += _bdot(pooled_ref[...].astype(F32).T, dxc * (mod * ps))
        acc_ref[0:1, :] += jnp.sum(dxc * yp * mod, axis=0, keepdims=True)
        acc_ref[1:2, :] += jnp.sum(dxc * yp * ps, axis=0, keepdims=True)

    tile = pl.BlockSpec((tb, gw), lambda gi, i: (i, gi))
    wblk = pl.BlockSpec((None, gw, gw), lambda gi, i: (gi, 0, 0))
    return pl.pallas_call(
        body, name=name, grid=(ng, t // tb),
        in_specs=[pl.BlockSpec((t + 2 * POOL_PAD, gw), lambda gi, i: (0, gi)), wblk,
                  pl.BlockSpec((1, gw), lambda gi, i: (0, gi)), tile, tile,
                  pl.BlockSpec((2, 16, gw), lambda gi, i: (0, 0, gi))],
        out_specs=[tile, wblk, pl.BlockSpec((8, gw), lambda gi, i: (0, gi))],
        out_shape=[jax.ShapeDtypeStruct((t, d), F32), jax.ShapeDtypeStruct(w_pool.shape, F32),
                   jax.ShapeDtypeStruct((8, d), F32)],
        compiler_params=_params(("arbitrary", "arbitrary")),
    )(dxp, w_pool, pool_scale, pooled, ypre, mods)


def _adamw(w, g, m, v, name):
    r, c = w.shape
    tr = _pick(r, (512, 352, 256, 128, 64, 32, 16, 8))
    c1 = 1.0 / (1.0 - ADAM_B1 ** ADAM_STEP)
    c2 = 1.0 / (1.0 - ADAM_B2 ** ADAM_STEP)

    def body(w_ref, g_ref, m_ref, v_ref, d_ref, nm_ref, nv_ref):
        gg = g_ref[...]
        nm = ADAM_B1 * m_ref[...] + (1.0 - ADAM_B1) * gg
        nv = ADAM_B2 * v_ref[...] + (1.0 - ADAM_B2) * (gg * gg)
        nm_ref[...] = nm
        nv_ref[...] = nv
        d_ref[...] = -ADAM_LR * ((nm * c1) / (jnp.sqrt(nv * c2) + ADAM_EPS) + ADAM_WD * w_ref[...])

    blk = pl.BlockSpec((tr, c), lambda i: (i, 0))
    shp = jax.ShapeDtypeStruct((r, c), F32)
    return pl.pallas_call(
        body, name=name, grid=(r // tr,), in_specs=[blk] * 4, out_specs=[blk] * 3, out_shape=[shp] * 3,
        compiler_params=_params(("parallel",)),
    )(w, g, m, v)


def _heads(z, n_heads):
    m = z.shape[0]
    return z.reshape(m, n_heads, -1).transpose(1, 0, 2)


def _unheads(zh):
    return zh.transpose(1, 0, 2).reshape(zh.shape[1], -1)


def _pad_rows(a, n):
    return jnp.pad(a, ((0, 0), (n, n), (0, 0))) if a.ndim == 3 else jnp.pad(a, ((n, n), (0, 0)))


def _local_step(x, ctx, target, mods, wts, fetch, emit):
    t, d = x.shape
    l_ctx = ctx.shape[0]
    tc = t + l_ctx
    norm_g = wts["norm_g"]
    ng = lambda l, k: norm_g[l, k][None, :]
    grads = {}
    dmods = [[[None] * N_MOD for _ in range(2)] for _ in range(2)]
    dnorm = [[None] * 3 for _ in range(2)]

    def ffn_fwd(z, l, kbase, wi, wo, g, n_x, tag):
        h = _modulate(z, mods[l], g, kbase, kbase + 1, n_x, BF16, f"mod_{tag}")
        au, act = _ffn_up(h, wi, 0, f"ffn_up_{tag}")
        z_new, y = _mm_resid(act, wo, 0, z, mods[l], kbase + 2, 0.5, n_x, f"ffn_down_{tag}")
        return z_new, (z, h, au, act, y)

    def ffn_bwd(dz_new, saved, l, kbase, wi, wo, g, n_x, tag, dep):
        z, h, au, act, y = saved
        dy, acc_gate = _resid_bwd(dz_new, y, mods[l], kbase + 2, 0.5, n_x, f"resid_bwd_{tag}", dep=dep)
        dau = _ffn_down_bwd(dy, wo, 0, au, f"ffn_down_bwd_{tag}")
        dwo = _mm_tn(act, dy, BF16, f"dwo_{tag}")
        dh = _mm([(dau, wi, 0, 0)], NN, d, F32, f"dh_{tag}", tm_pref=(256, 128, 64, 32, 16, 8))
        dwi_t = _mm_tn(dau, h, BF16, f"dwi_{tag}")
        dz, acc_mod = _modulate_bwd(z, dh, dz_new, mods[l], g, kbase + 1, n_x, f"mod_bwd_{tag}")
        return dz, dwi_t, dwo, acc_mod, acc_gate

    def record(l, kbase, k_norm, g, acc_mod, acc_gate, streams):
        total = None
        for s in range(streams):
            dmods[l][s][kbase] = acc_mod[s, 0]
            dmods[l][s][kbase + 1] = acc_mod[s, 1] * g[0]
            if acc_gate is not None:
                dmods[l][s][kbase + 2] = acc_gate[s, 0]
            part = acc_mod[s, 1] * (1.0 + mods[l][s, kbase + 1])
            total = part if total is None else total + part
        dnorm[l][k_norm] = total

    xc0 = jnp.concatenate([x, ctx], axis=0)
    big = fetch(0, None)
    wi1_0, wo1_0 = big["wi1_0"], big["wo1_0"]
    xc1, sv_f1 = ffn_fwd(xc0, 0, 0, wi1_0, wo1_0, ng(0, 0), t, "l0f1")
    hc = _modulate(xc1, mods[0], ng(0, 1), 3, 4, t, BF16, "mod_l0mix")
    big = fetch(1, xc1)
    w_in_t, w_out, wi2_0, wo2_0 = big["w_in_t"], big["w_out"], big["wi2_0"], big["wo2_0"]
    n_proj = w_in_t.shape[1]
    zall = _mm([(hc, w_in_t, 0, 0)], NT, n_proj, F32, "proj", tm_pref=(256, 128, 64, 32, 16, 8),
               tn_pref=(n_proj,))
    offs = np.cumsum((0,) + PROJ_SIZES)
    part = lambda i, rows=slice(None): zall[rows, offs[i]:offs[i + 1]]
    lat, con = slice(0, t), slice(t, tc)
    cos, sin = _rope_tables(t)
    qa = _heads(_rope(part(0, lat), cos, sin, False, "rope_q"), A_HEADS)
    ka = _heads(_rope(part(1, lat), cos, sin, False, "rope_k"), A_KV_HEADS)
    va = _heads(part(2, lat), A_KV_HEADS)
    kca, vca = _heads(part(1, con), A_KV_HEADS), _heads(part(2, con), A_KV_HEADS)
    kap, vap = _pad_rows(ka, WINDOW), _pad_rows(va, WINDOW)
    sink = wts["sink"].reshape(A_HEADS, 1, 1)
    o_a = _attn_fwd(qa, kap, vap, kca, vca, sink, "attn_fwd")

    qb, kb, vb = _heads(part(3), B_HEADS), _heads(part(4), B_HEADS), _heads(part(5), B_HEADS)
    rb = part(6, lat)
    zg = part(7)
    zg_f, zg_b = zg[:, :B_GATE_RANK], zg[:, B_GATE_RANK:]
    w2f, w2b, b2f, b2b = wts["w_a2_f"], wts["w_a2_b"], wts["b_a_f"], wts["b_a_b"]
    la_f = _heads(_gate_fwd(zg_f, w2f, b2f, "gate_f"), B_HEADS)
    la_b = _heads(_gate_fwd(zg_b, w2b, b2b, "gate_b"), B_HEADS)
    o_f, s_f = _gla_fwd(qb, kb, vb, la_f, t, False, "gla_fwd_f")
    o_b, s_b = _gla_fwd(qb, kb, vb, la_b, t, True, "gla_fwd_b")
    gla_g = wts["gla_g"]
    go = _gla_out_fwd(o_f[:, :t], o_b[:, :t], rb, gla_g, "gla_out")
    cat = jnp.concatenate([_unheads(o_a), go], axis=-1).astype(BF16)
    x1 = xc1[:t]
    x2, y_mix0 = _mm_resid(cat, w_out, 0, x1, mods[0], 5, 1.0, t, "w_out")
    x3, sv_f2 = ffn_fwd(x2, 0, 6, wi2_0, wo2_0, ng(0, 2), t, "l0f2")

    big = fetch(2, x3)
    wi1_1, wo1_1, wi2_1, wo2_1 = big["wi1_1"], big["wo1_1"], big["wi2_1"], big["wo2_1"]
    x4, sv_g1 = ffn_fwd(x3, 1, 0, wi1_1, wo1_1, ng(1, 0), t, "l1f1")
    hp = _modulate(x4, mods[1], ng(1, 1), 3, 4, t, F32, "mod_l1mix")
    w_pool, pool_scale = wts["w_pool"], wts["pool_scale"]
    x5, pooled, ypre = _pool_fwd(_pad_rows(hp, POOL_PAD), w_pool, pool_scale, x4, mods[1], 5, "pool_fwd")
    x6, sv_g2 = ffn_fwd(x5, 1, 6, wi2_1, wo2_1, ng(1, 2), t, "l1f2")

    dx6, loss_vec, dfinal_g = _final_loss(x6, wts["final_g"], target, "final_loss")
    grads["final_g"] = dfinal_g[0]

    dx5, dwi2_1, dwo2_1, acc_mod, acc_gate = ffn_bwd(dx6, sv_g2, 1, 6, wi2_1, wo2_1, ng(1, 2), t, "l1f2", None)
    token = emit(0, [dwi2_1, dwo2_1])
    record(1, 6, 2, ng(1, 2), acc_mod, acc_gate, 1)
    dhp, dw_pool, acc_pool = _pool_bwd(_pad_rows(dx5, POOL_PAD), w_pool, pool_scale, pooled, ypre, mods[1], 5,
                                       "pool_bwd")
    grads["pool_scale"] = acc_pool[0]
    dmods[1][0][5] = acc_pool[1]
    dx4, acc_mod = _modulate_bwd(x4, dhp, dx5, mods[1], ng(1, 1), 4, t, "mod_bwd_l1mix")
    record(1, 3, 1, ng(1, 1), acc_mod, None, 1)
    dx3, dwi1_1, dwo1_1, acc_mod, acc_gate = ffn_bwd(dx4, sv_g1, 1, 0, wi1_1, wo1_1, ng(1, 0), t, "l1f1", token)
    token = emit(1, [dwi1_1, dwo1_1])
    record(1, 0, 0, ng(1, 0), acc_mod, acc_gate, 1)

    dx2, dwi2_0, dwo2_0, acc_mod, acc_gate = ffn_bwd(dx3, sv_f2, 0, 6, wi2_0, wo2_0, ng(0, 2), t, "l0f2", token)
    token = emit(2, [dwi2_0, dwo2_0])
    record(0, 6, 2, ng(0, 2), acc_mod, acc_gate, 1)
    dymix, acc_gate = _resid_bwd(dx2, y_mix0, mods[0], 5, 1.0, t, "resid_bwd_mix", dep=token)
    dmods[0][0][5] = acc_gate[0, 0]
    dw_out = _mm_tn(cat, dymix, BF16, "dw_out")
    dcat = _mm([(dymix, w_out, 0, 0)], NT, cat.shape[1], F32, "dcat")
    do_a = _heads(dcat[:, :A_Q], A_HEADS)
    do_gla, drb, dgla_g = _gla_out_bwd(o_f[:, :t], o_b[:, :t], rb, gla_g, dcat[:, A_Q:], "gla_out_bwd")
    grads["gla_g"] = dgla_g[0]
    do_full = jnp.pad(do_gla, ((0, 0), (0, l_ctx), (0, 0)))
    dq_f, dk_f, dv_f, dla_f = _gla_bwd(qb, kb, vb, la_f, s_f, do_full, t, False, "gla_bwd_f")
    dq_b, dk_b, dv_b, dla_b = _gla_bwd(qb, kb, vb, la_b, s_b, do_full, t, True, "gla_bwd_b")
    dzg_f, dw2f, db2f = _gate_bwd(zg_f, w2f, b2f, _unheads(dla_f), "gate_bwd_f")
    dzg_b, dw2b, db2b = _gate_bwd(zg_b, w2b, b2b, _unheads(dla_b), "gate_bwd_b")
    grads.update(w_a2_f=dw2f, w_a2_b=dw2b, b_a_f=db2f[0], b_a_b=db2b[0])
    dqa_r, dkap, dvap, dkca, dvca, dsink = _attn_bwd(qa, kap, vap, kca, vca, sink, o_a, do_a, "attn_bwd")
    grads["sink"] = dsink[:, 0, 0]
    dqa = _rope(_unheads(dqa_r), cos, sin, True, "rope_bwd_q")
    dka = _rope(_unheads(dkap[:, WINDOW:WINDOW + t]), cos, sin, True, "rope_bwd_k")
    dva = dvap[:, WINDOW:WINDOW + t]
    zrow = lambda a, n: jnp.pad(a, ((0, n), (0, 0)))
    dz_parts = [
        zrow(dqa, l_ctx),
        jnp.concatenate([dka, _unheads(dkca)], axis=0),
        jnp.concatenate([_unheads(dva), _unheads(dvca)], axis=0),
        _unheads(dq_f + dq_b), _unheads(dk_f + dk_b), _unheads(dv_f + dv_b),
        zrow(drb, l_ctx),
        jnp.concatenate([dzg_f, dzg_b], axis=-1),
        jnp.zeros((tc, n_proj - PROJ_DIM), F32),
    ]
    dzall = jnp.concatenate(dz_parts, axis=-1).astype(BF16)
    dw_in_t = _mm_tn(dzall, hc, BF16, "dw_in")
    token = emit(3, [dw_in_t, dw_out, dw_pool])
    dhc = _mm([(dzall, w_in_t, 0, 0)], NN, d, F32, "dhc", tm_pref=(256, 128, 64, 32, 16, 8))
    dxc1_res = jnp.concatenate([dx2, jnp.zeros((l_ctx, d), F32)], axis=0)
    dxc1, acc_mod = _modulate_bwd(xc1, dhc, dxc1_res, mods[0], ng(0, 1), 4, t, "mod_bwd_l0mix")
    record(0, 3, 1, ng(0, 1), acc_mod, None, 2)
    dxc0, dwi1_0, dwo1_0, acc_mod, acc_gate = ffn_bwd(dxc1, sv_f1, 0, 0, wi1_0, wo1_0, ng(0, 0), t, "l0f1", token)
    emit(4, [dwi1_0, dwo1_0])
    record(0, 0, 0, ng(0, 0), acc_mod, acc_gate, 2)

    grads["norm_g"] = jnp.stack([jnp.stack(dnorm[0]), jnp.stack(dnorm[1])])
    zero = jnp.zeros((d,), F32)
    dmods_arr = jnp.stack([jnp.stack([jnp.stack([v if v is not None else zero for v in dmods[l][s]])
                                      for s in range(2)]) for l in range(2)])
    return loss_vec, dxc0[:t], grads, dmods_arr


def _pack(parts):
    flat = jnp.concatenate([p.reshape(-1).astype(F32) for p in parts])
    pad = (-flat.shape[0]) % 128
    return jnp.pad(flat, (0, pad))[None, :]


def _unpack(rows, shapes):
    out, off = [], 0
    for s in shapes:
        n = int(np.prod(s))
        out.append(rows[:, off:off + n].reshape((rows.shape[0],) + tuple(s)))
        off += n
    return out


def _cols_to_full(g):
    g = jnp.moveaxis(g, 0, -2)
    return g.reshape(g.shape[:-2] + (-1,))


def kernel(x, c, ctx, c_ctx, w_mod, b_mod, norm_g, ffn1_wi, ffn1_wo, ffn2_wi, ffn2_wo, w_in, w_a2_f, b_a_f, w_a2_b, b_a_b, sink, gla_g, w_out, w_pool, pool_scale, final_g, loss_target, m_c_ctx, m_w_mod, m_b_mod, m_norm_g, m_ffn1_wi, m_ffn1_wo, m_ffn2_wi, m_ffn2_wo, m_w_in, m_w_a2_f, m_b_a_f, m_w_a2_b, m_b_a_b, m_sink, m_gla_g, m_w_out, m_w_pool, m_pool_scale, m_final_g, v_c_ctx, v_w_mod, v_b_mod, v_norm_g, v_ffn1_wi, v_ffn1_wo, v_ffn2_wi, v_ffn2_wo, v_w_in, v_w_a2_f, v_b_a_f, v_w_a2_b, v_b_a_b, v_sink, v_gla_g, v_w_out, v_w_pool, v_pool_scale, v_final_g):
    t, d = x.shape[1], x.shape[2]
    me = _dev_index()
    nc = w_mod.shape[2]
    ncol_in = w_in.shape[2]
    ncol_pad = -(-ncol_in // 16) * 16

    small_shapes = [(d,), norm_g.shape, pool_scale.shape, w_a2_f.shape, w_a2_b.shape, w_pool.shape]
    g1 = _gather_small(_pack([c, norm_g, pool_scale, w_a2_f, w_a2_b, w_pool]), "gather_params")
    c_all, norm_g_all, pool_scale_all, w2f_all, w2b_all, w_pool_all = _unpack(g1, small_shapes)
    wts = {
        "norm_g": _cols_to_full(norm_g_all),
        "pool_scale": _cols_to_full(pool_scale_all),
        "w_a2_f": _cols_to_full(w2f_all)[0],
        "w_a2_b": _cols_to_full(w2b_all)[0],
        "w_pool": jnp.moveaxis(w_pool_all[:, 0], 0, 1).reshape(w_pool.shape[1], -1, w_pool.shape[3]),
        "b_a_f": b_a_f, "b_a_b": b_a_b, "sink": sink[0], "gla_g": gla_g, "final_g": final_g[None, :],
    }

    craw = jnp.concatenate([c_all, c_ctx[None, :], jnp.zeros((16 - N_DEV - 1, d), F32)], axis=0)
    b_cols = lax.dynamic_slice_in_dim(b_mod, me * nc, nc, axis=1)[:, None, :]
    mm_cols = _adaln_fwd(craw, w_mod, b_cols, "adaln_fwd")
    g2 = _gather_small(mm_cols.reshape(1, -1), "gather_mods").reshape(N_DEV, 2, 16, nc)
    mm_full = jnp.moveaxis(g2, 0, 2).reshape(2, 16, N_MOD, d)
    mods = jnp.stack([lax.dynamic_index_in_dim(mm_full, me, axis=1, keepdims=False), mm_full[:, N_DEV]], axis=1)
    mods = jnp.pad(mods, ((0, 0), (0, 0), (0, 16 - N_MOD), (0, 0)))

    tr = lambda w: jnp.swapaxes(w, 1, 2).astype(BF16)
    wi1_sh, wi2_sh, wo1_sh, wo2_sh = tr(ffn1_wi), tr(ffn2_wi), ffn1_wo.astype(BF16), ffn2_wo.astype(BF16)
    w_in_sh = jnp.pad(tr(w_in), ((0, 0), (0, ncol_pad - ncol_in), (0, 0)))
    groups = [
        {"wi1_0": wi1_sh[0:1], "wo1_0": wo1_sh[0:1]},
        {"w_in": w_in_sh, "w_out": w_out.astype(BF16), "wi2_0": wi2_sh[0:1], "wo2_0": wo2_sh[0:1]},
        {"wi1_1": wi1_sh[1:2], "wo1_1": wo1_sh[1:2], "wi2_1": wi2_sh[1:2], "wo2_1": wo2_sh[1:2]},
    ]

    def land_of(shard):
        a_, r, c_ = shard.shape
        return lax.dynamic_update_slice(lax.empty((a_, N_DEV * r, c_), shard.dtype), shard, (0, me * r, 0))

    gathers, token = [], None
    for gi, grp in enumerate(groups):
        shards = list(grp.values())
        gathers.append(_exchange_start(shards, [land_of(s) for s in shards], True, 1 + gi, token, f"gather_start_{gi}"))
        token = gathers[-1][4]
    n_proj = -(-(N_DEV * ncol_in) // 128) * 128

    def fetch(gi, after):
        _, lands = _exchange_wait(gathers[gi], True, token if after is None else after, f"gather_wait_{gi}")
        out = dict(zip(groups[gi].keys(), lands))
        if "w_in" in out:
            w_in_t = out.pop("w_in").reshape(1, N_DEV, ncol_pad, d)[:, :, :ncol_in].reshape(1, N_DEV * ncol_in, d)
            out["w_in_t"] = jnp.pad(w_in_t, ((0, 0), (0, n_proj - N_DEV * ncol_in), (0, 0)))
        return out

    scatters = []

    def emit(stage, arrays):
        if stage == 3:
            dw_in_t, dw_out, dw_pool = arrays
            dw_in_full = dw_in_t[:N_DEV * ncol_in].reshape(N_DEV, ncol_in, d)
            dw_in_full = jnp.pad(dw_in_full, ((0, 0), (0, ncol_pad - ncol_in), (0, 0)))
            srcs = [dw_in_full.reshape(1, N_DEV * ncol_pad, d), dw_out[None], dw_pool.astype(BF16)]
        else:
            srcs = [a[None] for a in arrays]
        lands = [lax.empty((N_DEV, s.shape[0], s.shape[1] // N_DEV, s.shape[2]), s.dtype) for s in srcs]
        scatters.append(_exchange_start(srcs, lands, False, 4 + stage, None, f"scatter_start_{stage}"))
        return scatters[-1][4]

    loss_vec, grad_x, grads, dmods = _local_step(x[0], ctx[0], loss_target[0], mods, wts, fetch, emit)
    loss = lax.psum(jnp.sum(loss_vec), ("x", "y", "c"))

    def reduce_stage(stage, after):
        wholes, lands = _exchange_wait(scatters[stage], False, after, f"scatter_wait_{stage}")
        return [_sum_slots(ld, wh, me, f"sum_grad_{stage}_{i}") for i, (ld, wh) in enumerate(zip(lands, wholes))]

    (dwi2_1, dwo2_1), (dwi1_1, dwo1_1), (dwi2_0, dwo2_0), (dw_in_s, dw_out_s, dw_pool_s) = [
        reduce_stage(stage, grad_x) for stage in range(4)]
    back = lambda g: jnp.swapaxes(g, 1, 2)
    g_big = {
        "ffn2_wi": back(jnp.concatenate([dwi2_0, dwi2_1], axis=0)), "ffn2_wo": jnp.concatenate([dwo2_0, dwo2_1], axis=0),
        "w_in": back(dw_in_s[:, :ncol_in]), "w_out": dw_out_s, "w_pool": dw_pool_s[None],
    }

    small_g = [dmods[:, :, :N_MOD].reshape(2, 2, N_MOD * d), grads["norm_g"], grads["pool_scale"], grads["final_g"],
               grads["b_a_f"], grads["b_a_b"], grads["sink"], grads["gla_g"], grads["w_a2_f"], grads["w_a2_b"]]
    small_g_shapes = [a.shape for a in small_g]
    g3 = _gather_small(_pack(small_g), "gather_small_grads")
    total = _sum_rows8(g3, "sum_small_grads")
    dmm_all = _unpack(g3, small_g_shapes[:1])[0]
    (dmm_sum, dnorm_g, dpool_scale, dfinal_g, db_a_f, db_a_b, dsink, dgla_g, dw_a2_f, dw_a2_b) = [
        a[0] for a in _unpack(total, small_g_shapes)]
    dmm_rows = jnp.concatenate([dmm_all[:, :, 0].transpose(1, 0, 2), dmm_sum[:, 1][:, None, :],
                                jnp.zeros((2, 16 - N_DEV - 1, N_MOD * d), F32)], axis=1)
    grad_b_mod = dmm_sum[:, 0] + dmm_sum[:, 1]
    dmm_cols = lax.dynamic_slice_in_dim(dmm_rows, me * nc, nc, axis=2)
    cs_t = jnp.transpose(_silu(craw)).astype(BF16)
    grad_w_mod, dcraw = _adaln_bwd(craw, cs_t, dmm_cols, w_mod, "adaln_bwd")
    g4 = _gather_small((dcraw[0, N_DEV] + dcraw[1, N_DEV])[None, :], "gather_c_ctx_grad")
    grad_c_ctx = _sum_rows8(g4, "sum_c_ctx_grad")[0]

    col = lambda v, n: lax.dynamic_slice_in_dim(v, me * n, n, axis=v.ndim - 1)
    g_small = {
        "c_ctx": grad_c_ctx, "b_mod": grad_b_mod, "norm_g": col(dnorm_g, norm_g.shape[2]),
        "w_a2_f": col(dw_a2_f, w_a2_f.shape[2])[None], "b_a_f": db_a_f[None], "w_a2_b": col(dw_a2_b, w_a2_b.shape[2])[None],
        "b_a_b": db_a_b[None], "sink": dsink[None], "gla_g": dgla_g[None], "pool_scale": col(dpool_scale, pool_scale.shape[1])[None],
        "final_g": dfinal_g,
    }
    g_all = {**g_big, **g_small, "w_mod": grad_w_mod}

    order = ["c_ctx", "w_mod", "b_mod", "norm_g", "ffn1_wi", "ffn1_wo", "ffn2_wi", "ffn2_wo", "w_in", "w_a2_f", "b_a_f",
             "w_a2_b", "b_a_b", "sink", "gla_g", "w_out", "w_pool", "pool_scale", "final_g"]
    ws = dict(c_ctx=c_ctx, w_mod=w_mod, b_mod=b_mod, norm_g=norm_g, ffn1_wi=ffn1_wi, ffn1_wo=ffn1_wo, ffn2_wi=ffn2_wi,
              ffn2_wo=ffn2_wo, w_in=w_in, w_a2_f=w_a2_f, b_a_f=b_a_f, w_a2_b=w_a2_b, b_a_b=b_a_b, sink=sink, gla_g=gla_g,
              w_out=w_out, w_pool=w_pool, pool_scale=pool_scale, final_g=final_g)
    ms = dict(c_ctx=m_c_ctx, w_mod=m_w_mod, b_mod=m_b_mod, norm_g=m_norm_g, ffn1_wi=m_ffn1_wi, ffn1_wo=m_ffn1_wo,
              ffn2_wi=m_ffn2_wi, ffn2_wo=m_ffn2_wo, w_in=m_w_in, w_a2_f=m_w_a2_f, b_a_f=m_b_a_f, w_a2_b=m_w_a2_b,
              b_a_b=m_b_a_b, sink=m_sink, gla_g=m_gla_g, w_out=m_w_out, w_pool=m_w_pool, pool_scale=m_pool_scale,
              final_g=m_final_g)
    vs = dict(c_ctx=v_c_ctx, w_mod=v_w_mod, b_mod=v_b_mod, norm_g=v_norm_g, ffn1_wi=v_ffn1_wi, ffn1_wo=v_ffn1_wo,
              ffn2_wi=v_ffn2_wi, ffn2_wo=v_ffn2_wo, w_in=v_w_in, w_a2_f=v_w_a2_f, b_a_f=v_b_a_f, w_a2_b=v_w_a2_b,
              b_a_b=v_b_a_b, sink=v_sink, gla_g=v_gla_g, w_out=v_w_out, w_pool=v_w_pool, pool_scale=v_pool_scale,
              final_g=v_final_g)
    big = ["w_mod", "ffn2_wi", "ffn2_wo", "w_out", "ffn1_wi", "ffn1_wo"]
    delta, new_m, new_v = {}, {}, {}

    def adamw_big(nm):
        shp = ws[nm].shape
        two_d = lambda a: a.reshape(-1, shp[-1])
        dl, nm_, nv_ = _adamw(two_d(ws[nm]), two_d(g_all[nm]), two_d(ms[nm]), two_d(vs[nm]), f"adamw_{nm}")
        delta[nm], new_m[nm], new_v[nm] = dl.reshape(shp), nm_.reshape(shp), nv_.reshape(shp)

    for nm in big[:4]:
        adamw_big(nm)
    rest = [nm for nm in order if nm not in big]
    rest_shapes = [ws[nm].shape for nm in rest]
    packed = [_pack([d_[nm].reshape(ws[nm].shape) for nm in rest]).reshape(-1, 128) for d_ in (ws, g_all, ms, vs)]
    pad_rows = (-packed[0].shape[0]) % 512
    packed = [jnp.pad(p, ((0, pad_rows), (0, 0))) for p in packed]
    outs = _adamw(*packed, "adamw_small")
    for dst, arr in zip((delta, new_m, new_v), outs):
        for nm, val in zip(rest, _unpack(arr.reshape(1, -1), rest_shapes)):
            dst[nm] = val[0]

    dwi1_0, dwo1_0 = reduce_stage(4, outs[0])
    g_all["ffn1_wi"] = back(jnp.concatenate([dwi1_0, dwi1_1], axis=0))
    g_all["ffn1_wo"] = jnp.concatenate([dwo1_0, dwo1_1], axis=0)
    for nm in big[4:]:
        adamw_big(nm)
    g_all = {nm: g_all[nm].reshape(ws[nm].shape) for nm in order}

    return (loss, grad_x[None], *[g_all[nm] for nm in order], *[delta[nm] for nm in order],
            *[new_m[nm] for nm in order], *[new_v[nm] for nm in order])
```

```python
import functools

import numpy as np
import jax
import jax.numpy as jnp
from jax import lax
from jax.experimental import pallas as pl
from jax.experimental.pallas import tpu as pltpu

F32 = jnp.float32
BF16 = jnp.bfloat16
HI = lax.Precision.HIGHEST
MESH = pl.DeviceIdType.MESH

N_DEV = 8
RMS_EPS = 1e-6
N_MOD = 9
GRID_W = 64
A_HEADS, A_KV_HEADS, A_HEAD_DIM = 8, 2, 64
A_REP = A_HEADS // A_KV_HEADS
WINDOW = 128
ROPE_BASE = 10000.0
B_HEADS, B_DK, B_DV = 4, 64, 128
B_GATE_RANK = 16
B_GATE_NORM = 16.0
B_CHUNK = 64
POOL_WINDOWS = (2, 4, 8, 16)
POOL_PAD = 8
A_Q = A_HEADS * A_HEAD_DIM
A_KV = A_KV_HEADS * A_HEAD_DIM
B_QK = B_HEADS * B_DK
B_V = B_HEADS * B_DV
PROJ_SIZES = (A_Q, A_KV, A_KV, B_QK, B_QK, B_V, B_V, 2 * B_GATE_RANK)
PROJ_DIM = sum(PROJ_SIZES)
ADAM_LR, ADAM_B1, ADAM_B2, ADAM_EPS, ADAM_WD, ADAM_STEP = 0.001, 0.9, 0.999, 1e-08, 0.01, 10

VMEM_LIMIT = 56 * 1024 * 1024

NN = ((1,), (0,))
NT = ((1,), (1,))
TN = ((0,), (0,))


def _dot(a, b, dims=NN, prec=None):
    return lax.dot_general(a, b, (dims, ((), ())), precision=prec, preferred_element_type=F32)


def _bdot(a, b, dims=NN):
    return _dot(a.astype(BF16), b.astype(BF16), dims)


def _params(sem=None, **kw):
    return pltpu.CompilerParams(dimension_semantics=sem, vmem_limit_bytes=VMEM_LIMIT, **kw)


def _silu(a):
    return a * jax.nn.sigmoid(a)


def _pick(n, prefs):
    for p in prefs:
        if n % p == 0:
            return p
    return n


def _full(shape):
    nd = len(shape)
    return pl.BlockSpec(shape, lambda *_: (0,) * nd)


def _peers():
    x, y, c = lax.axis_index("x"), lax.axis_index("y"), lax.axis_index("c")
    return x, y, c


def _dev_index():
    x, y, c = _peers()
    return 4 * x + 2 * y + c


def _others(x, y, c):
    return [(x, y, 1 - c), (1 - x, y, c), (x, 1 - y, c), (1 - x, 1 - y, c),
            (1 - x, y, 1 - c), (x, 1 - y, 1 - c), (1 - x, 1 - y, 1 - c)]


def _index_of(dev):
    return 4 * dev[0] + 2 * dev[1] + dev[2]


def _exchange_refs(gather, shapes, srcs, lands, a, me, to):
    if gather:
        r = shapes[a][1]
        return srcs[a], lands[a].at[:, pl.ds(_index_of(me) * r, r), :]
    r = shapes[a][1] // N_DEV
    return srcs[a].at[:, pl.ds(_index_of(to) * r, r), :], lands[a].at[_index_of(me)]


HBM_SPEC = pl.BlockSpec(memory_space=pltpu.HBM)
SEM_SPEC = pl.BlockSpec(memory_space=pltpu.SEMAPHORE)
EFFECT = pltpu.SideEffectType.DATAFLOW_SIDE_EFFECTING


def _exchange_start(srcs, lands, gather, collective_id, dep, name):
    n = len(srcs)
    shapes = [s.shape for s in srcs]
    deps = [] if dep is None else [dep]

    def body(*refs):
        src_refs, land_refs = refs[:n], refs[n:2 * n]
        send_sems, recv_sems = refs[2 * n + len(deps)], refs[2 * n + len(deps) + 1]
        token = refs[-1]
        x, y, c = _peers()
        others = _others(x, y, c)
        barrier = pltpu.get_barrier_semaphore()
        for peer in others:
            pl.semaphore_signal(barrier, inc=1, device_id=peer, device_id_type=MESH)
        pl.semaphore_wait(barrier, len(others))
        for a in range(n):
            for k, to in enumerate(others):
                src, dst = _exchange_refs(gather, shapes, src_refs, land_refs, a, (x, y, c), to)
                pltpu.make_async_remote_copy(src_ref=src, dst_ref=dst, send_sem=send_sems.at[7 * a + k],
                                             recv_sem=recv_sems.at[7 * a + k], device_id=to, device_id_type=MESH).start()
        token[...] = jnp.zeros_like(token)

    outs = pl.pallas_call(
        body, name=name,
        out_shape=(pltpu.SemaphoreType.DMA((7 * n,)), pltpu.SemaphoreType.DMA((7 * n,)),
                   *[pltpu.HBM(s.shape, s.dtype) for s in srcs], *[pltpu.HBM(l.shape, l.dtype) for l in lands],
                   jax.ShapeDtypeStruct((8, 128), F32)),
        in_specs=[HBM_SPEC] * (2 * n) + [pl.BlockSpec(memory_space=pl.ANY)] * len(deps),
        out_specs=(SEM_SPEC, SEM_SPEC, *[HBM_SPEC] * (2 * n), pl.BlockSpec(memory_space=pltpu.VMEM)),
        input_output_aliases={i: 2 + i for i in range(2 * n)},
        compiler_params=pltpu.CompilerParams(has_side_effects=EFFECT, collective_id=collective_id),
    )(*[pltpu.with_memory_space_constraint(s, pltpu.HBM) for s in srcs],
      *[pltpu.with_memory_space_constraint(l, pltpu.HBM) for l in lands], *deps)
    return outs[0], outs[1], list(outs[2:2 + n]), list(outs[2 + n:2 + 2 * n]), outs[-1]


def _exchange_wait(started, gather, after, name):
    send_sems, recv_sems, srcs, lands, _ = started
    n = len(srcs)
    shapes = [s.shape for s in srcs]

    def body(*refs):
        src_refs, land_refs = refs[:n], refs[n:2 * n]
        send_sems, recv_sems = refs[2 * n], refs[2 * n + 1]
        x, y, c = _peers()
        for a in range(n):
            for k, peer in enumerate(_others(x, y, c)):
                src, _ = _exchange_refs(gather, shapes, src_refs, land_refs, a, (x, y, c), peer)
                _, dst = _exchange_refs(gather, shapes, src_refs, land_refs, a, peer, (x, y, c))
                copy = pltpu.make_async_remote_copy(src_ref=src, dst_ref=dst, send_sem=send_sems.at[7 * a + k],
                                                    recv_sem=recv_sems.at[7 * a + k], device_id=peer, device_id_type=MESH)
                copy.wait_send()
                copy.wait_recv()

    outs = pl.pallas_call(
        body, name=name,
        out_shape=(*[pltpu.HBM(s.shape, s.dtype) for s in srcs], *[pltpu.HBM(l.shape, l.dtype) for l in lands]),
        in_specs=[HBM_SPEC] * (2 * n) + [SEM_SPEC, SEM_SPEC, pl.BlockSpec(memory_space=pl.ANY)],
        out_specs=tuple([HBM_SPEC] * (2 * n)),
        input_output_aliases={i: i for i in range(2 * n)},
        compiler_params=pltpu.CompilerParams(has_side_effects=EFFECT),
    )(*srcs, *lands, send_sems, recv_sems, after)
    return list(outs[:n]), list(outs[n:])


def _sum_slots(land, whole, me, name):
    _, a_, r, c = land.shape
    tr = _pick(r, (352, 256, 128, 64, 32, 16, 8))
    nr = r // tr

    def body(me_ref, land_ref, own_ref, out_ref):
        acc = None
        for s in range(N_DEV):
            part = jnp.where(me_ref[0] == s, own_ref[...], land_ref[s]).astype(F32)
            acc = part if acc is None else acc + part
        out_ref[...] = acc

    return pl.pallas_call(
        body, name=name,
        grid_spec=pltpu.PrefetchScalarGridSpec(
            num_scalar_prefetch=1, grid=(a_, nr),
            in_specs=[pl.BlockSpec((N_DEV, None, tr, c), lambda i, j, me_ref: (0, i, j, 0)),
                      pl.BlockSpec((None, tr, c), lambda i, j, me_ref: (i, me_ref[0] * nr + j, 0))],
            out_specs=pl.BlockSpec((None, tr, c), lambda i, j, me_ref: (i, j, 0))),
        out_shape=jax.ShapeDtypeStruct((a_, r, c), F32),
        compiler_params=_params(("parallel", "parallel")),
    )(me.reshape(1).astype(jnp.int32), land, whole)


def _gather_small(vec, name):
    p = vec.shape[1]
    pp = -(-p // 1024) * 1024
    blk = jnp.pad(vec, ((0, 0), (0, pp - p))).reshape(8, pp // 8)

    def body(in_ref, out_ref, send_sems, recv_sems):
        x, y, c = _peers()
        me = 4 * x + 2 * y + c
        others = [(x, y, 1 - c), (1 - x, y, c), (x, 1 - y, c), (1 - x, 1 - y, c),
                  (1 - x, y, 1 - c), (x, 1 - y, 1 - c), (1 - x, 1 - y, 1 - c)]

        def rows(idx):
            return out_ref.at[pl.ds(pl.multiple_of(idx * 8, 8), 8), :]

        out_ref[pl.ds(pl.multiple_of(me * 8, 8), 8), :] = in_ref[...]

        def copy(k, dev, slot):
            return pltpu.make_async_remote_copy(
                src_ref=in_ref, dst_ref=rows(slot), send_sem=send_sems.at[k], recv_sem=recv_sems.at[k],
                device_id=dev, device_id_type=MESH)

        sends = [copy(k, dev, me) for k, dev in enumerate(others)]
        for cp in sends:
            cp.start()
        for k, dev in enumerate(others):
            copy(k, dev, 4 * dev[0] + 2 * dev[1] + dev[2]).wait_recv()
        for cp in sends:
            cp.wait_send()

    vm = pl.BlockSpec(memory_space=pltpu.VMEM)
    out = pl.pallas_call(
        body, name=name, out_shape=jax.ShapeDtypeStruct((8 * N_DEV, pp // 8), F32),
        in_specs=[vm], out_specs=vm,
        scratch_shapes=[pltpu.SemaphoreType.DMA((7,)), pltpu.SemaphoreType.DMA((7,))],
        compiler_params=pltpu.CompilerParams(has_side_effects=True, vmem_limit_bytes=VMEM_LIMIT),
    )(blk)
    return out.reshape(N_DEV, pp)[:, :p]


def _sum_rows8(g, name):
    p = g.shape[1]

    def body(in_ref, out_ref):
        acc = in_ref[0:1, :]
        for s in range(1, N_DEV):
            acc = acc + in_ref[s:s + 1, :]
        out_ref[...] = acc

    return pl.pallas_call(body, name=name, out_shape=jax.ShapeDtypeStruct((1, p), F32),
                          compiler_params=_params())(g)


def _sel_row(mods_ref, is_ctx, k):
    return jnp.where(is_ctx, mods_ref[1, k:k + 1, :], mods_ref[0, k:k + 1, :])


def _modulate(z, mods, g, ks, kc, n_x, out_dtype, name):
    m, d = z.shape
    tm = _pick(m, (256, 128, 64, 32, 16, 8))

    def body(z_ref, mods_ref, g_ref, h_ref):
        is_ctx = pl.program_id(0) * tm >= n_x
        zz = z_ref[...]
        r = lax.rsqrt(jnp.mean(zz * zz, axis=-1, keepdims=True) + RMS_EPS)
        shift, scale = _sel_row(mods_ref, is_ctx, ks), _sel_row(mods_ref, is_ctx, kc)
        h_ref[...] = ((zz * r) * g_ref[...] * (1.0 + scale) + shift).astype(out_dtype)

    return pl.pallas_call(
        body, name=name, grid=(m // tm,),
        in_specs=[pl.BlockSpec((tm, d), lambda i: (i, 0)), _full(mods.shape), _full(g.shape)],
        out_specs=pl.BlockSpec((tm, d), lambda i: (i, 0)),
        out_shape=jax.ShapeDtypeStruct((m, d), out_dtype),
        compiler_params=_params(("parallel",)),
    )(z, mods, g)


def _modulate_bwd(z, dh, dres, mods, g, kc, n_x, name):
    m, d = z.shape
    tm = _pick(m, (256, 128, 64, 32, 16, 8))
    first_ctx = n_x // tm

    def body(z_ref, dh_ref, dres_ref, mods_ref, g_ref, dx_ref, acc_ref):
        i = pl.program_id(0)
        is_ctx = i * tm >= n_x

        @pl.when((i == 0) | (i == first_ctx))
        def _():
            acc_ref[...] = jnp.zeros_like(acc_ref)

        zz, dhh = z_ref[...], dh_ref[...]
        r = lax.rsqrt(jnp.mean(zz * zz, axis=-1, keepdims=True) + RMS_EPS)
        nz = zz * r
        gain = g_ref[...] * (1.0 + _sel_row(mods_ref, is_ctx, kc))
        dn = dhh * gain
        dz = r * (dn - nz * jnp.mean(dn * nz, axis=-1, keepdims=True))
        dx_ref[...] = dres_ref[...] + dz
        acc_ref[0:1, :] += jnp.sum(dhh, axis=0, keepdims=True)
        acc_ref[1:2, :] += jnp.sum(dhh * nz, axis=0, keepdims=True)

    row = pl.BlockSpec((tm, d), lambda i: (i, 0))
    return pl.pallas_call(
        body, name=name, grid=(m // tm,),
        in_specs=[row, row, row, _full(mods.shape), _full(g.shape)],
        out_specs=[row, pl.BlockSpec((None, 8, d), lambda i: ((i * tm >= n_x).astype(jnp.int32), 0, 0))],
        out_shape=[jax.ShapeDtypeStruct((m, d), F32), jax.ShapeDtypeStruct((2, 8, d), F32)],
        compiler_params=_params(("arbitrary",)),
    )(z, dh, dres, mods, g)


def _ffn_up(h, wi_t, layer, name):
    m, d = h.shape
    f = wi_t.shape[1] // 2
    tm = _pick(m, (256, 128, 64, 32, 16, 8))

    def body(h_ref, w_ref, au_ref, act_ref):
        hh = h_ref[...]
        a = _dot(hh, w_ref[0:f, :], NT)
        u = _dot(hh, w_ref[f:2 * f, :], NT)
        au_ref[:, 0:f] = a.astype(BF16)
        au_ref[:, f:2 * f] = u.astype(BF16)
        act_ref[...] = (_silu(a) * u).astype(BF16)

    return pl.pallas_call(
        body, name=name, grid=(m // tm,),
        in_specs=[pl.BlockSpec((tm, d), lambda i: (i, 0)),
                  pl.BlockSpec((None, 2 * f, d), lambda i: (layer, 0, 0))],
        out_specs=[pl.BlockSpec((tm, 2 * f), lambda i: (i, 0)), pl.BlockSpec((tm, f), lambda i: (i, 0))],
        out_shape=[jax.ShapeDtypeStruct((m, 2 * f), BF16), jax.ShapeDtypeStruct((m, f), BF16)],
        compiler_params=_params(("parallel",)),
    )(h, wi_t)


def _mm_resid(a, b, layer, res, mods, km, coef, n_x, name):
    m, k = a.shape
    n = b.shape[2]
    tm = _pick(m, (512, 256, 128, 64, 32, 16, 8))
    tn = _pick(n, (512, 256, 128))

    def body(a_ref, b_ref, res_ref, mods_ref, out_ref, y_ref):
        is_ctx = pl.program_id(1) * tm >= n_x
        y = _dot(a_ref[...], b_ref[...])
        y_ref[...] = y
        out_ref[...] = res_ref[...] + coef * _sel_row(mods_ref, is_ctx, km) * y

    tile = pl.BlockSpec((tm, tn), lambda j, i: (i, j))
    return pl.pallas_call(
        body, name=name, grid=(n // tn, m // tm),
        in_specs=[pl.BlockSpec((tm, k), lambda j, i: (i, 0)),
                  pl.BlockSpec((None, k, tn), lambda j, i: (layer, 0, j)),
                  tile, pl.BlockSpec((2, 16, tn), lambda j, i: (0, 0, j))],
        out_specs=[tile, tile],
        out_shape=[jax.ShapeDtypeStruct((m, n), F32), jax.ShapeDtypeStruct((m, n), F32)],
        compiler_params=_params(("parallel", "parallel")),
    )(a, b, res, mods)


def _resid_bwd(dx, y, mods, km, coef, n_x, name, dep=None):
    m, d = dx.shape
    tm = _pick(m, (256, 128, 64, 32, 16, 8))
    first_ctx = n_x // tm
    deps = [] if dep is None else [dep]

    def body(dx_ref, y_ref, mods_ref, *rest):
        dy_ref, acc_ref = rest[-2:]
        i = pl.program_id(0)
        is_ctx = i * tm >= n_x

        @pl.when((i == 0) | (i == first_ctx))
        def _():
            acc_ref[...] = jnp.zeros_like(acc_ref)

        dxx = dx_ref[...]
        dy_ref[...] = (coef * _sel_row(mods_ref, is_ctx, km) * dxx).astype(BF16)
        acc_ref[0:1, :] += jnp.sum(coef * y_ref[...] * dxx, axis=0, keepdims=True)

    row = pl.BlockSpec((tm, d), lambda i: (i, 0))
    return pl.pallas_call(
        body, name=name, grid=(m // tm,),
        in_specs=[row, row, _full(mods.shape)] + [pl.BlockSpec(memory_space=pl.ANY)] * len(deps),
        out_specs=[row, pl.BlockSpec((None, 8, d), lambda i: ((i * tm >= n_x).astype(jnp.int32), 0, 0))],
        out_shape=[jax.ShapeDtypeStruct((m, d), BF16), jax.ShapeDtypeStruct((2, 8, d), F32)],
        compiler_params=_params(("arbitrary",)),
    )(dx, y, mods, *deps)


def _ffn_down_bwd(dy, wo, layer, au, name):
    m, d = dy.shape
    f = wo.shape[1]
    tm = _pick(m, (256, 128, 64, 32, 16, 8))

    def body(dy_ref, wo_ref, au_ref, dau_ref):
        dact = _dot(dy_ref[...], wo_ref[...], NT)
        aa, uu = au_ref[:, 0:f].astype(F32), au_ref[:, f:2 * f].astype(F32)
        sg = jax.nn.sigmoid(aa)
        dau_ref[:, 0:f] = (dact * uu * (sg * (1.0 + aa * (1.0 - sg)))).astype(BF16)
        dau_ref[:, f:2 * f] = (dact * (aa * sg)).astype(BF16)

    wide = pl.BlockSpec((tm, 2 * f), lambda i: (i, 0))
    return pl.pallas_call(
        body, name=name, grid=(m // tm,),
        in_specs=[pl.BlockSpec((tm, d), lambda i: (i, 0)), pl.BlockSpec((None, f, d), lambda i: (layer, 0, 0)), wide],
        out_specs=wide, out_shape=jax.ShapeDtypeStruct((m, 2 * f), BF16),
        compiler_params=_params(("parallel",)),
    )(dy, wo, au)


def _mm(terms, dims, n, out_dtype, name, tm_pref=(512, 256, 128, 64, 32, 16, 8), tn_pref=(512, 256, 128), dep=None):
    m = terms[0][0].shape[0]
    tm = _pick(m, tm_pref)
    tn = _pick(n, tn_pref)
    nt = len(terms)
    deps = [] if dep is None else [dep]

    def body(*refs):
        out_ref = refs[-1]
        acc = None
        for t in range(nt):
            part = _dot(refs[2 * t][...].astype(BF16), refs[2 * t + 1][...].astype(BF16), dims)
            acc = part if acc is None else acc + part
        out_ref[...] = acc.astype(out_dtype)

    in_specs, args = [], []
    for a, b, layer, rb in terms:
        k = a.shape[1]
        in_specs.append(pl.BlockSpec((tm, k), lambda j, i: (i, 0)))
        if dims == NN:
            in_specs.append(pl.BlockSpec((None, k, tn), lambda j, i, layer=layer, rb=rb: (layer, rb, j)))
        else:
            nb = n // tn
            in_specs.append(pl.BlockSpec((None, tn, k), lambda j, i, layer=layer, rb=rb, nb=nb: (layer, rb * nb + j, 0)))
        args += [a, b]
    return pl.pallas_call(
        body, name=name, grid=(n // tn, m // tm), in_specs=in_specs + [pl.BlockSpec(memory_space=pl.ANY)] * len(deps),
        out_specs=pl.BlockSpec((tm, tn), lambda j, i: (i, j)),
        out_shape=jax.ShapeDtypeStruct((m, n), out_dtype),
        compiler_params=_params(("parallel", "parallel")),
    )(*args, *deps)


def _mm_tn(a, b, out_dtype, name):
    t = a.shape[0]
    m, n = a.shape[1], b.shape[1]
    tm = _pick(m, (1408, 2432, 1024, 512, 256, 128))
    tn = _pick(n, (1024, 512, 256, 128))
    tk = _pick(t, (512, 256, 128, 64, 32, 16, 8))

    def body(a_ref, b_ref, out_ref, acc_ref):
        kk = pl.program_id(2)

        @pl.when(kk == 0)
        def _():
            acc_ref[...] = jnp.zeros_like(acc_ref)

        acc_ref[...] += _dot(a_ref[...].astype(BF16), b_ref[...].astype(BF16), TN)

        @pl.when(kk == pl.num_programs(2) - 1)
        def _():
            out_ref[...] = acc_ref[...].astype(out_dtype)

    return pl.pallas_call(
        body, name=name, grid=(m // tm, n // tn, t // tk),
        in_specs=[pl.BlockSpec((tk, tm), lambda i, j, k: (k, i)), pl.BlockSpec((tk, tn), lambda i, j, k: (k, j))],
        out_specs=pl.BlockSpec((tm, tn), lambda i, j, k: (i, j)),
        out_shape=jax.ShapeDtypeStruct((m, n), out_dtype),
        scratch_shapes=[pltpu.VMEM((tm, tn), F32)],
        compiler_params=_params(("parallel", "parallel", "arbitrary")),
    )(a, b)


def _final_loss(x, g, target, name):
    t, d = x.shape
    tm = _pick(t, (256, 128, 64, 32, 16, 8))

    def body(x_ref, g_ref, t_ref, dx_ref, loss_ref, dg_ref):
        @pl.when(pl.program_id(0) == 0)
        def _():
            loss_ref[...] = jnp.zeros_like(loss_ref)
            dg_ref[...] = jnp.zeros_like(dg_ref)

        xx, gg = x_ref[...], g_ref[...]
        r = lax.rsqrt(jnp.mean(xx * xx, axis=-1, keepdims=True) + RMS_EPS)
        nz = xx * r
        err = nz * gg - t_ref[...]
        loss_ref[...] += jnp.sum(err * err, axis=0, keepdims=True) * (0.5 / d)
        dout = err * (1.0 / d)
        dg_ref[...] += jnp.sum(dout * nz, axis=0, keepdims=True)
        dn = dout * gg
        dx_ref[...] = r * (dn - nz * jnp.mean(dn * nz, axis=-1, keepdims=True))

    row = pl.BlockSpec((tm, d), lambda i: (i, 0))
    vec = pl.BlockSpec((1, d), lambda i: (0, 0))
    return pl.pallas_call(
        body, name=name, grid=(t // tm,), in_specs=[row, vec, row], out_specs=[row, vec, vec],
        out_shape=[jax.ShapeDtypeStruct((t, d), F32), jax.ShapeDtypeStruct((1, d), F32),
                   jax.ShapeDtypeStruct((1, d), F32)],
        compiler_params=_params(("arbitrary",)),
    )(x, g, target)


def _adaln_fwd(craw, w_mod, b_cols, name):
    lyr, d, nc = w_mod.shape

    def body(c_ref, w_ref, b_ref, out_ref):
        out_ref[...] = _bdot(_silu(c_ref[...]), w_ref[...]) + b_ref[...]

    return pl.pallas_call(
        body, name=name, grid=(lyr,),
        in_specs=[_full(craw.shape), pl.BlockSpec((None, d, nc), lambda l: (l, 0, 0)),
                  pl.BlockSpec((None, 1, nc), lambda l: (l, 0, 0))],
        out_specs=pl.BlockSpec((None, 16, nc), lambda l: (l, 0, 0)),
        out_shape=jax.ShapeDtypeStruct((lyr, 16, nc), F32),
        compiler_params=_params(("parallel",)),
    )(craw, w_mod, b_cols)


def _adaln_bwd(craw, cs_t, dmm_cols, w_mod, name):
    lyr, d, nc = w_mod.shape

    def body(c_ref, cst_ref, dmm_ref, w_ref, gw_ref, dc_ref):
        dmm = dmm_ref[...]
        gw_ref[...] = _bdot(cst_ref[...], dmm)
        cc = c_ref[...]
        sg = jax.nn.sigmoid(cc)
        dc_ref[...] = _bdot(dmm, w_ref[...], NT) * (sg * (1.0 + cc * (1.0 - sg)))

    wspec = pl.BlockSpec((None, d, nc), lambda l: (l, 0, 0))
    return pl.pallas_call(
        body, name=name, grid=(lyr,),
        in_specs=[_full(craw.shape), _full(cs_t.shape), pl.BlockSpec((None, 16, nc), lambda l: (l, 0, 0)), wspec],
        out_specs=[wspec, pl.BlockSpec((None, 16, d), lambda l: (l, 0, 0))],
        out_shape=[jax.ShapeDtypeStruct((lyr, d, nc), F32), jax.ShapeDtypeStruct((lyr, 16, d), F32)],
        compiler_params=_params(("parallel",)),
    )(craw, cs_t, dmm_cols, w_mod)


def _rope_tables(t):
    rows = np.repeat(np.arange(t // GRID_W, dtype=np.float32), GRID_W)
    cols = np.tile(np.arange(GRID_W, dtype=np.float32), t // GRID_W)
    n = A_HEAD_DIM // 4
    freqs = (ROPE_BASE ** (-np.arange(n, dtype=np.float32) / n)).astype(np.float32)
    ang_r, ang_c = (rows[:, None] * freqs).astype(np.float32), (cols[:, None] * freqs).astype(np.float32)
    cr, sr, cc, sc = np.cos(ang_r), np.sin(ang_r), np.cos(ang_c), np.sin(ang_c)
    cos = np.concatenate([cr, cr, cc, cc] * 2, axis=-1).astype(np.float32)
    sin = np.concatenate([-sr, sr, -sc, sc] * 2, axis=-1).astype(np.float32)
    return jnp.asarray(cos), jnp.asarray(sin)


def _rope(xt, cos, sin, adjoint, name):
    t, w = xt.shape
    tb = _pick(t, (512, 256, 128))
    rep = w // cos.shape[1]

    def body(x_ref, c_ref, s_ref, o_ref):
        xx = x_ref[...]
        cc = jnp.concatenate([c_ref[...]] * rep, axis=1) if rep > 1 else c_ref[...]
        ss = jnp.concatenate([s_ref[...]] * rep, axis=1) if rep > 1 else s_ref[...]
        low = (lax.broadcasted_iota(jnp.int32, xx.shape, 1) % 32) < 16

        def partner(v):
            return jnp.where(low, pltpu.roll(v, w - 16, 1), pltpu.roll(v, 16, 1))

        if adjoint:
            o_ref[...] = xx * cc + partner(xx * ss)
        else:
            o_ref[...] = xx * cc + partner(xx) * ss

    blk = pl.BlockSpec((tb, w), lambda i: (i, 0))
    tab = pl.BlockSpec((tb, cos.shape[1]), lambda i: (i, 0))
    return pl.pallas_call(
        body, name=name, grid=(t // tb,), in_specs=[blk, tab, tab], out_specs=blk,
        out_shape=jax.ShapeDtypeStruct((t, w), F32), compiler_params=_params(("parallel",)),
    )(xt, cos, sin)


def _attn_probs(q, kb, kc, sink, n, t):
    scale = A_HEAD_DIM ** -0.5
    s1 = _bdot(q, kb, NT) * scale
    s2 = _bdot(q, kc, NT) * scale
    qpos = n * WINDOW + lax.broadcasted_iota(jnp.int32, s1.shape, 0) % WINDOW
    kpos = (n - 1) * WINDOW + lax.broadcasted_iota(jnp.int32, s1.shape, 1)
    valid = (kpos >= 0) & (kpos < t) & (jnp.abs(kpos - qpos) <= WINDOW)
    s1 = jnp.where(valid, s1, -jnp.inf)
    mx = jnp.maximum(jnp.maximum(jnp.max(s1, axis=-1, keepdims=True), jnp.max(s2, axis=-1, keepdims=True)), sink)
    p1, p2, ps = jnp.exp(s1 - mx), jnp.exp(s2 - mx), jnp.exp(sink - mx)
    inv = 1.0 / (jnp.sum(p1, axis=-1, keepdims=True) + jnp.sum(p2, axis=-1, keepdims=True) + ps)
    return p1 * inv, p2 * inv, ps * inv


def _sink_rows(sink_ref):
    return jnp.concatenate([jnp.broadcast_to(sink_ref[r], (WINDOW, 1)) for r in range(A_REP)], axis=0)


def _attn_fwd(q, kp, vp, kc, vc, sink, name):
    hq, t, dh = q.shape
    nb = t // WINDOW
    lc = kc.shape[1]
    rows = A_REP * WINDOW

    def body(q_ref, k_ref, v_ref, kc_ref, vc_ref, sink_ref, o_ref):
        n = pl.program_id(1)
        start = pl.multiple_of(n * WINDOW, WINDOW)
        kb, vb = k_ref[pl.ds(start, 3 * WINDOW), :], v_ref[pl.ds(start, 3 * WINDOW), :]
        p1, p2, _ = _attn_probs(q_ref[...].reshape(rows, dh), kb, kc_ref[...], _sink_rows(sink_ref), n, t)
        o_ref[...] = (_bdot(p1, vb) + _bdot(p2, vc_ref[...])).reshape(A_REP, WINDOW, dh)

    qblk = pl.BlockSpec((A_REP, WINDOW, dh), lambda g, n: (g, n, 0))
    kfull = pl.BlockSpec((None, t + 2 * WINDOW, dh), lambda g, n: (g, 0, 0))
    cfull = pl.BlockSpec((None, lc, dh), lambda g, n: (g, 0, 0))
    return pl.pallas_call(
        body, name=name, grid=(hq // A_REP, nb),
        in_specs=[qblk, kfull, kfull, cfull, cfull, pl.BlockSpec((A_REP, 1, 1), lambda g, n: (g, 0, 0))],
        out_specs=qblk, out_shape=jax.ShapeDtypeStruct((hq, t, dh), F32),
        compiler_params=_params(("parallel", "parallel")),
    )(q, kp, vp, kc, vc, sink)


def _attn_bwd(q, kp, vp, kc, vc, sink, o, do, name):
    hq, t, dh = q.shape
    nb = t // WINDOW
    lc = kc.shape[1]
    scale = A_HEAD_DIM ** -0.5
    rows = A_REP * WINDOW

    def body(q_ref, k_ref, v_ref, kc_ref, vc_ref, sink_ref, o_ref, do_ref,
             dq_ref, dk_ref, dv_ref, dkc_ref, dvc_ref, dsink_ref):
        n = pl.program_id(1)

        @pl.when(n == 0)
        def _():
            dk_ref[...] = jnp.zeros_like(dk_ref)
            dv_ref[...] = jnp.zeros_like(dv_ref)
            dkc_ref[...] = jnp.zeros_like(dkc_ref)
            dvc_ref[...] = jnp.zeros_like(dvc_ref)
            dsink_ref[...] = jnp.zeros_like(dsink_ref)

        start = pl.multiple_of(n * WINDOW, WINDOW)
        band = pl.ds(start, 3 * WINDOW)
        qq, kb, vb, kcc, vcc = q_ref[...].reshape(rows, dh), k_ref[band, :], v_ref[band, :], kc_ref[...], vc_ref[...]
        p1, p2, ps = _attn_probs(qq, kb, kcc, _sink_rows(sink_ref), n, t)
        dout = do_ref[...].reshape(rows, dh)
        delta = jnp.sum(dout * o_ref[...].reshape(rows, dh), axis=-1, keepdims=True)
        ds1 = p1 * (_bdot(dout, vb, NT) - delta)
        ds2 = p2 * (_bdot(dout, vcc, NT) - delta)
        dq_ref[...] = ((_bdot(ds1, kb) + _bdot(ds2, kcc)) * scale).reshape(A_REP, WINDOW, dh)
        dk_ref[band, :] += _bdot(ds1.T, qq) * scale
        dv_ref[band, :] += _bdot(p1.T, dout)
        dkc_ref[...] += _bdot(ds2.T, qq) * scale
        dvc_ref[...] += _bdot(p2.T, dout)
        dsink_ref[...] += jnp.sum((-ps * delta).reshape(A_REP, WINDOW, 1), axis=1, keepdims=True)

    qblk = pl.BlockSpec((A_REP, WINDOW, dh), lambda g, n: (g, n, 0))
    kfull = pl.BlockSpec((None, t + 2 * WINDOW, dh), lambda g, n: (g, 0, 0))
    cfull = pl.BlockSpec((None, lc, dh), lambda g, n: (g, 0, 0))
    return pl.pallas_call(
        body, name=name, grid=(hq // A_REP, nb),
        in_specs=[qblk, kfull, kfull, cfull, cfull, pl.BlockSpec((A_REP, 1, 1), lambda g, n: (g, 0, 0)), qblk, qblk],
        out_specs=[qblk, kfull, kfull, cfull, cfull, pl.BlockSpec((A_REP, 8, 128), lambda g, n: (g, 0, 0))],
        out_shape=[jax.ShapeDtypeStruct(q.shape, F32), jax.ShapeDtypeStruct(kp.shape, F32),
                   jax.ShapeDtypeStruct(kp.shape, F32), jax.ShapeDtypeStruct(kc.shape, F32),
                   jax.ShapeDtypeStruct(kc.shape, F32), jax.ShapeDtypeStruct((hq, 8, 128), F32)],
        compiler_params=_params(("parallel", "arbitrary")),
    )(q, kp, vp, kc, vc, sink, o, do)


def _gate_fwd(zg, w2, b2, name):
    m = zg.shape[0]
    n = w2.shape[1]
    tm = _pick(m, (512, 256, 128, 64, 32, 16, 8))

    def body(z_ref, w_ref, b_ref, o_ref):
        o_ref[...] = jax.nn.log_sigmoid(_bdot(z_ref[...], w_ref[...]) + b_ref[...]) / B_GATE_NORM

    return pl.pallas_call(
        body, name=name, grid=(m // tm,),
        in_specs=[pl.BlockSpec((tm, zg.shape[1]), lambda i: (i, 0)), _full(w2.shape), _full(b2.shape)],
        out_specs=pl.BlockSpec((tm, n), lambda i: (i, 0)), out_shape=jax.ShapeDtypeStruct((m, n), F32),
        compiler_params=_params(("parallel",)),
    )(zg, w2, b2)


def _gate_bwd(zg, w2, b2, dla, name):
    m, rk = zg.shape
    n = w2.shape[1]
    tm = _pick(m, (512, 256, 128, 64, 32, 16, 8))

    def body(z_ref, w_ref, b_ref, d_ref, dz_ref, dw_ref, db_ref):
        @pl.when(pl.program_id(0) == 0)
        def _():
            dw_ref[...] = jnp.zeros_like(dw_ref)
            db_ref[...] = jnp.zeros_like(db_ref)

        zz, ww = z_ref[...], w_ref[...]
        pre = _bdot(zz, ww) + b_ref[...]
        dpre = d_ref[...] * (1.0 / B_GATE_NORM) * jax.nn.sigmoid(-pre)
        dz_ref[...] = _bdot(dpre, ww, NT)
        dw_ref[...] += _bdot(zz.T, dpre)
        db_ref[...] += jnp.sum(dpre, axis=0, keepdims=True)

    return pl.pallas_call(
        body, name=name, grid=(m // tm,),
        in_specs=[pl.BlockSpec((tm, rk), lambda i: (i, 0)), _full(w2.shape), _full(b2.shape),
                  pl.BlockSpec((tm, n), lambda i: (i, 0))],
        out_specs=[pl.BlockSpec((tm, rk), lambda i: (i, 0)), _full(w2.shape), _full(b2.shape)],
        out_shape=[jax.ShapeDtypeStruct((m, rk), F32), jax.ShapeDtypeStruct(w2.shape, F32),
                   jax.ShapeDtypeStruct(b2.shape, F32)],
        compiler_params=_params(("arbitrary",)),
    )(zg, w2, b2, dla)


def _chunk_order(step, n_x_chunks, n_chunks, reverse):
    n_c = n_chunks - n_x_chunks
    if reverse:
        return jnp.where(step < n_c, n_chunks - 1 - step, n_chunks - 1 - step)
    return jnp.where(step < n_c, n_x_chunks + step, step - n_c)


def _tri(reverse, transpose=False):
    i = lax.broadcasted_iota(jnp.int32, (B_CHUNK, B_CHUNK), 0)
    j = lax.broadcasted_iota(jnp.int32, (B_CHUNK, B_CHUNK), 1)
    if transpose:
        i, j = j, i
    return (j >= i) if reverse else (j <= i)


def _gla_chunk(q, k, la, reverse):
    g = _dot(_tri(reverse).astype(F32), la, NN, HI)
    last = 0 if reverse else B_CHUNK - 1
    gl = g[last:last + 1, :]
    eg, eng, egl = jnp.exp(g), jnp.exp(-g), jnp.exp(gl - g)
    decay_col = jnp.exp(jnp.sum(la.T, axis=1, keepdims=True))
    return q * (B_DK ** -0.5) * eg, k * eng, k * egl, eg, eng, egl, decay_col


def _gla_fwd(q, k, v, la_f, la_b, n_x, name):
    hh, tc, dk = q.shape
    dv = v.shape[2]
    nc, nxc = tc // B_CHUNK, n_x // B_CHUNK
    orders = [functools.partial(_chunk_order, n_x_chunks=nxc, n_chunks=nc, reverse=rev) for rev in (False, True)]

    def body(*refs):
        ins, outs, s_refs = refs[:8], refs[8:12], refs[12:]

        @pl.when(pl.program_id(0) == 0)
        def _():
            for s_ref in s_refs:
                s_ref[...] = jnp.zeros_like(s_ref)

        for h in range(hh):
            for di, reverse in enumerate((False, True)):
                q_ref, k_ref, v_ref, la_ref = ins[4 * di:4 * di + 4]
                o_ref, s_save_ref = outs[2 * di:2 * di + 2]
                s_ref = s_refs[di]
                qt, kt, ke, _, _, _, decay_col = _gla_chunk(q_ref[h], k_ref[h], la_ref[h], reverse)
                vv, s_prev = v_ref[h], s_ref[h]
                att = jnp.where(_tri(reverse), _bdot(qt, kt, NT), 0.0)
                o_ref[h] = _bdot(att, vv) + _bdot(qt, s_prev)
                s_save_ref[h] = s_prev
                s_ref[h] = decay_col * s_prev + _bdot(ke.T, vv)

    def blk(d, order):
        return pl.BlockSpec((hh, B_CHUNK, d), lambda s: (0, order(s), 0))

    def sblk(order):
        return pl.BlockSpec((None, hh, dk, dv), lambda s: (order(s), 0, 0, 0))

    in_specs, out_specs = [], []
    for order in orders:
        in_specs += [blk(dk, order), blk(dk, order), blk(dv, order), blk(dk, order)]
        out_specs += [blk(dv, order), sblk(order)]
    o_shape, s_shape = jax.ShapeDtypeStruct((hh, tc, dv), F32), jax.ShapeDtypeStruct((nc, hh, dk, dv), F32)
    return pl.pallas_call(
        body, name=name, grid=(nc,), in_specs=in_specs, out_specs=out_specs,
        out_shape=[o_shape, s_shape, o_shape, s_shape],
        scratch_shapes=[pltpu.VMEM((hh, dk, dv), F32)] * 2,
        compiler_params=_params(("arbitrary",)),
    )(q, k, v, la_f, q, k, v, la_b)


def _gla_bwd(q, k, v, la_f, la_b, s_f, s_b, do, n_x, name):
    hh, tc, dk = q.shape
    dv = v.shape[2]
    nc, nxc = tc // B_CHUNK, n_x // B_CHUNK
    orders = [functools.partial(lambda s, rev: _chunk_order(nc - 1 - s, nxc, nc, rev), rev=rev) for rev in (False, True)]

    def body(*refs):
        ins, outs, ds_refs = refs[:12], refs[12:20], refs[20:]

        @pl.when(pl.program_id(0) == 0)
        def _():
            for ds_ref in ds_refs:
                ds_ref[...] = jnp.zeros_like(ds_ref)

        for h in range(hh):
            for di, reverse in enumerate((False, True)):
                q_ref, k_ref, v_ref, la_ref, s_save_ref, do_ref = ins[6 * di:6 * di + 6]
                dq_ref, dk_ref, dv_ref, dla_ref = outs[4 * di:4 * di + 4]
                ds_ref = ds_refs[di]
                mask = _tri(reverse)
                last = 0 if reverse else B_CHUNK - 1
                is_last = lax.broadcasted_iota(jnp.int32, (B_CHUNK, dk), 0) == last
                qt, kt, ke, eg, eng, egl, decay_col = _gla_chunk(q_ref[h], k_ref[h], la_ref[h], reverse)
                vv, s_prev, dout, ds_new = v_ref[h], s_save_ref[h], do_ref[h], ds_ref[h]
                att = jnp.where(mask, _bdot(qt, kt, NT), 0.0)
                datt = jnp.where(mask, _bdot(dout, vv, NT), 0.0)
                dv_ref[h] = _bdot(att.T, dout) + _bdot(ke, ds_new)
                dqt = _bdot(datt, kt) + _bdot(dout, s_prev, NT)
                dkt = _bdot(datt.T, qt)
                dke = _bdot(vv, ds_new, NT)
                ddecay_row = jnp.sum((ds_new * s_prev).T, axis=0, keepdims=True)
                decay_row = jnp.exp(jnp.sum(la_ref[h], axis=0, keepdims=True))
                ds_ref[h] = decay_col * ds_new + _bdot(qt.T, dout)
                dq_ref[h] = dqt * (B_DK ** -0.5) * eg
                dk_ref[h] = dkt * eng + dke * egl
                dgl = jnp.sum(dke * ke, axis=0, keepdims=True) + ddecay_row * decay_row
                dg = dqt * qt - dkt * kt - dke * ke + jnp.where(is_last, dgl, 0.0)
                dla_ref[h] = _dot(_tri(reverse, transpose=True).astype(F32), dg, NN, HI)

    def blk(d, order):
        return pl.BlockSpec((hh, B_CHUNK, d), lambda s: (0, order(s), 0))

    in_specs, out_specs = [], []
    for order in orders:
        in_specs += [blk(dk, order), blk(dk, order), blk(dv, order), blk(dk, order),
                     pl.BlockSpec((None, hh, dk, dv), lambda s, order=order: (order(s), 0, 0, 0)), blk(dv, order)]
        out_specs += [blk(dk, order), blk(dk, order), blk(dv, order), blk(dk, order)]
    k_shape, v_shape = jax.ShapeDtypeStruct((hh, tc, dk), F32), jax.ShapeDtypeStruct((hh, tc, dv), F32)
    return pl.pallas_call(
        body, name=name, grid=(nc,), in_specs=in_specs, out_specs=out_specs,
        out_shape=[k_shape, k_shape, v_shape, k_shape] * 2,
        scratch_shapes=[pltpu.VMEM((hh, dk, dv), F32)] * 2,
        compiler_params=_params(("arbitrary",)),
    )(q, k, v, la_f, s_f, do, q, k, v, la_b, s_b, do)


def _gla_out_fwd(o_f, o_b, r, g, name):
    hh, t, dv = o_f.shape
    tb = _pick(t, (256, 128, 64))

    def body(of_ref, ob_ref, r_ref, g_ref, out_ref):
        for h in range(hh):
            o = of_ref[h] + ob_ref[h]
            rs = lax.rsqrt(jnp.mean(o * o, axis=-1, keepdims=True) + RMS_EPS)
            out_ref[:, h * dv:(h + 1) * dv] = (o * rs) * g_ref[...] * _silu(r_ref[:, h * dv:(h + 1) * dv])

    oblk = pl.BlockSpec((hh, tb, dv), lambda i: (0, i, 0))
    rblk = pl.BlockSpec((tb, hh * dv), lambda i: (i, 0))
    return pl.pallas_call(
        body, name=name, grid=(t // tb,), in_specs=[oblk, oblk, rblk, _full(g.shape)], out_specs=rblk,
        out_shape=jax.ShapeDtypeStruct((t, hh * dv), F32), compiler_params=_params(("parallel",)),
    )(o_f, o_b, r, g)


def _gla_out_bwd(o_f, o_b, r, g, dout, name):
    hh, t, dv = o_f.shape
    tb = _pick(t, (256, 128, 64))

    def body(of_ref, ob_ref, r_ref, g_ref, d_ref, do_ref, dr_ref, dg_ref):
        @pl.when(pl.program_id(0) == 0)
        def _():
            dg_ref[...] = jnp.zeros_like(dg_ref)

        gg = g_ref[...]
        for h in range(hh):
            cols = slice(h * dv, (h + 1) * dv)
            o = of_ref[h] + ob_ref[h]
            rs = lax.rsqrt(jnp.mean(o * o, axis=-1, keepdims=True) + RMS_EPS)
            nz = o * rs
            rr, dd = r_ref[:, cols], d_ref[:, cols]
            sg = jax.nn.sigmoid(rr)
            dr_ref[:, cols] = dd * nz * gg * (sg * (1.0 + rr * (1.0 - sg)))
            dy = dd * (rr * sg)
            dg_ref[...] += jnp.sum(dy * nz, axis=0, keepdims=True)
            dn = dy * gg
            do_ref[h] = rs * (dn - nz * jnp.mean(dn * nz, axis=-1, keepdims=True))

    oblk = pl.BlockSpec((hh, tb, dv), lambda i: (0, i, 0))
    rblk = pl.BlockSpec((tb, hh * dv), lambda i: (i, 0))
    return pl.pallas_call(
        body, name=name, grid=(t // tb,), in_specs=[oblk, oblk, rblk, _full(g.shape), rblk],
        out_specs=[oblk, rblk, _full(g.shape)],
        out_shape=[jax.ShapeDtypeStruct(o_f.shape, F32), jax.ShapeDtypeStruct(r.shape, F32),
                   jax.ShapeDtypeStruct(g.shape, F32)],
        compiler_params=_params(("arbitrary",)),
    )(o_f, o_b, r, g, dout)


def _pool_band(half, tb, adjoint):
    r = lax.broadcasted_iota(jnp.int32, (tb, tb + 2 * POOL_PAD), 0) + POOL_PAD
    j = lax.broadcasted_iota(jnp.int32, (tb, tb + 2 * POOL_PAD), 1)
    if adjoint:
        return ((j > r - half) & (j <= r + half)).astype(F32)
    return ((j >= r - half) & (j < r + half)).astype(F32)


def _pool_count(pos, half, t):
    return (jnp.minimum(pos + half, t) - jnp.maximum(pos - half, 0)).astype(F32)


def _pool_fwd(hp, w_pool, pool_scale, res, mods, km, name):
    t, d = res.shape
    ng, gw = w_pool.shape[0], w_pool.shape[1]
    tb = _pick(t, (256, 128, 64))

    def body(hp_ref, w_ref, ps_ref, res_ref, mods_ref, out_ref, pooled_ref, ypre_ref):
        gi, i = pl.program_id(0), pl.program_id(1)
        half = jnp.left_shift(1, gi)
        win = hp_ref[pl.ds(pl.multiple_of(i * tb, tb), tb + 2 * POOL_PAD), :]
        total = _dot(_pool_band(half, tb, False), win, NN, HI)
        pos = i * tb + lax.broadcasted_iota(jnp.int32, (tb, 1), 0)
        pooled = total / _pool_count(pos, half, t) - win[POOL_PAD:POOL_PAD + tb, :]
        ypre = _bdot(pooled, w_ref[...])
        pooled_ref[...] = pooled.astype(BF16)
        ypre_ref[...] = ypre
        out_ref[...] = res_ref[...] + mods_ref[0, km:km + 1, :] * (ypre * ps_ref[...])

    tile = pl.BlockSpec((tb, gw), lambda gi, i: (i, gi))
    return pl.pallas_call(
        body, name=name, grid=(ng, t // tb),
        in_specs=[pl.BlockSpec((t + 2 * POOL_PAD, gw), lambda gi, i: (0, gi)),
                  pl.BlockSpec((None, gw, gw), lambda gi, i: (gi, 0, 0)),
                  pl.BlockSpec((1, gw), lambda gi, i: (0, gi)), tile,
                  pl.BlockSpec((2, 16, gw), lambda gi, i: (0, 0, gi))],
        out_specs=[tile, tile, tile],
        out_shape=[jax.ShapeDtypeStruct((t, d), F32), jax.ShapeDtypeStruct((t, d), BF16),
                   jax.ShapeDtypeStruct((t, d), F32)],
        compiler_params=_params(("parallel", "parallel")),
    )(hp, w_pool, pool_scale, res, mods)


def _pool_bwd(dxp, w_pool, pool_scale, pooled, ypre, mods, km, name):
    t, d = pooled.shape
    ng, gw = w_pool.shape[0], w_pool.shape[1]
    tb = _pick(t, (256, 128, 64))

    def body(dxp_ref, w_ref, ps_ref, pooled_ref, ypre_ref, mods_ref, dh_ref, dw_ref, acc_ref):
        gi, i = pl.program_id(0), pl.program_id(1)

        @pl.when(i == 0)
        def _():
            dw_ref[...] = jnp.zeros_like(dw_ref)
            acc_ref[...] = jnp.zeros_like(acc_ref)

        half = jnp.left_shift(1, gi)
        mod, ps = mods_ref[0, km:km + 1, :], ps_ref[...]
        dwin = dxp_ref[pl.ds(pl.multiple_of(i * tb, tb), tb + 2 * POOL_PAD), :]
        dpooled = _bdot(dwin * (mod * ps), w_ref[...], NT)
        pos = i * tb - POOL_PAD + lax.broadcasted_iota(jnp.int32, (tb + 2 * POOL_PAD, 1), 0)
        spread = _dot(_pool_band(half, tb, True), dpooled / jnp.maximum(_pool_count(pos, half, t), 1.0), NN, HI)
        dh_ref[...] = spread - dpooled[POOL_PAD:POOL_PAD + tb, :]
        dxc, yp = dwin[POOL_PAD:POOL_PAD + tb, :], ypre_ref[...]
        dw_ref[...] += _bdot(pooled_ref[...].astype(F32).T, dxc * (mod * ps))
        acc_ref[0:1, :] += jnp.sum(dxc * yp * mod, axis=0, keepdims=True)
        acc_ref[1:2, :] += jnp.sum(dxc * yp * ps, axis=0, keepdims=True)

    tile = pl.BlockSpec((tb, gw), lambda gi, i: (i, gi))
    wblk = pl.BlockSpec((None, gw, gw), lambda gi, i: (gi, 0, 0))
    return pl.pallas_call(
        body, name=name, grid=(ng, t // tb),
        in_specs=[pl.BlockSpec((t + 2 * POOL_PAD, gw), lambda gi, i: (0, gi)), wblk,
                  pl.BlockSpec((1, gw), lambda gi, i: (0, gi)), tile, tile,
                  pl.BlockSpec((2, 16, gw), lambda gi, i: (0, 0, gi))],
        out_specs=[tile, wblk, pl.BlockSpec((8, gw), lambda gi, i: (0, gi))],
        out_shape=[jax.ShapeDtypeStruct((t, d), F32), jax.ShapeDtypeStruct(w_pool.shape, F32),
                   jax.ShapeDtypeStruct((8, d), F32)],
        compiler_params=_params(("arbitrary", "arbitrary")),
    )(dxp, w_pool, pool_scale, pooled, ypre, mods)


def _adamw(w, g, m, v, name):
    r, c = w.shape
    tr = _pick(r, (512, 352, 256, 128, 64, 32, 16, 8))
    c1 = 1.0 / (1.0 - ADAM_B1 ** ADAM_STEP)
    c2 = 1.0 / (1.0 - ADAM_B2 ** ADAM_STEP)

    def body(w_ref, g_ref, m_ref, v_ref, d_ref, nm_ref, nv_ref):
        gg = g_ref[...]
        nm = ADAM_B1 * m_ref[...] + (1.0 - ADAM_B1) * gg
        nv = ADAM_B2 * v_ref[...] + (1.0 - ADAM_B2) * (gg * gg)
        nm_ref[...] = nm
        nv_ref[...] = nv
        d_ref[...] = -ADAM_LR * ((nm * c1) / (jnp.sqrt(nv * c2) + ADAM_EPS) + ADAM_WD * w_ref[...])

    blk = pl.BlockSpec((tr, c), lambda i: (i, 0))
    shp = jax.ShapeDtypeStruct((r, c), F32)
    return pl.pallas_call(
        body, name=name, grid=(r // tr,), in_specs=[blk] * 4, out_specs=[blk] * 3, out_shape=[shp] * 3,
        compiler_params=_params(("parallel",)),
    )(w, g, m, v)


def _heads(z, n_heads):
    m = z.shape[0]
    return z.reshape(m, n_heads, -1).transpose(1, 0, 2)


def _unheads(zh):
    return zh.transpose(1, 0, 2).reshape(zh.shape[1], -1)


def _pad_rows(a, n):
    return jnp.pad(a, ((0, 0), (n, n), (0, 0))) if a.ndim == 3 else jnp.pad(a, ((n, n), (0, 0)))


def _local_step(x, ctx, target, mods, wts, fetch, emit):
    t, d = x.shape
    l_ctx = ctx.shape[0]
    tc = t + l_ctx
    norm_g = wts["norm_g"]
    ng = lambda l, k: norm_g[l, k][None, :]
    grads = {}
    dmods = [[[None] * N_MOD for _ in range(2)] for _ in range(2)]
    dnorm = [[None] * 3 for _ in range(2)]

    def ffn_fwd(z, l, kbase, wi, wo, g, n_x, tag):
        h = _modulate(z, mods[l], g, kbase, kbase + 1, n_x, BF16, f"mod_{tag}")
        au, act = _ffn_up(h, wi, 0, f"ffn_up_{tag}")
        wo = wo(act) if callable(wo) else wo
        z_new, y = _mm_resid(act, wo, 0, z, mods[l], kbase + 2, 0.5, n_x, f"ffn_down_{tag}")
        return z_new, (z, h, au, act, y, wi, wo)

    def ffn_bwd(dz_new, saved, l, kbase, g, n_x, tag, stage):
        z, h, au, act, y, wi, wo = saved
        dy, acc_gate = _resid_bwd(dz_new, y, mods[l], kbase + 2, 0.5, n_x, f"resid_bwd_{tag}")
        dau = _ffn_down_bwd(dy, wo, 0, au, f"ffn_down_bwd_{tag}")
        dwo = _mm_tn(act, dy, BF16, f"dwo_{tag}")
        dwi_t = _mm_tn(dau, h, BF16, f"dwi_{tag}")
        token = emit(stage, [dwi_t, dwo])
        dh = _mm([(dau, wi, 0, 0)], NN, d, F32, f"dh_{tag}", tm_pref=(256, 128, 64, 32, 16, 8), dep=token)
        dz, acc_mod = _modulate_bwd(z, dh, dz_new, mods[l], g, kbase + 1, n_x, f"mod_bwd_{tag}")
        return dz, acc_mod, acc_gate

    def record(l, kbase, k_norm, g, acc_mod, acc_gate, streams):
        total = None
        for s in range(streams):
            dmods[l][s][kbase] = acc_mod[s, 0]
            dmods[l][s][kbase + 1] = acc_mod[s, 1] * g[0]
            if acc_gate is not None:
                dmods[l][s][kbase + 2] = acc_gate[s, 0]
            part = acc_mod[s, 1] * (1.0 + mods[l][s, kbase + 1])
            total = part if total is None else total + part
        dnorm[l][k_norm] = total

    xc0 = jnp.concatenate([x, ctx], axis=0)
    wi1_0 = fetch(0, None)["wi1_0"]
    xc1, sv_f1 = ffn_fwd(xc0, 0, 0, wi1_0, lambda act: fetch(1, act)["wo1_0"], ng(0, 0), t, "l0f1")
    hc = _modulate(xc1, mods[0], ng(0, 1), 3, 4, t, BF16, "mod_l0mix")
    w_in_t = fetch(2, hc)["w_in_t"]
    n_proj = w_in_t.shape[1]
    zall = _mm([(hc, w_in_t, 0, 0)], NT, n_proj, F32, "proj", tm_pref=(256, 128, 64, 32, 16, 8),
               tn_pref=(n_proj,))
    offs = np.cumsum((0,) + PROJ_SIZES)
    part = lambda i, rows=slice(None): zall[rows, offs[i]:offs[i + 1]]
    lat, con = slice(0, t), slice(t, tc)
    cos, sin = _rope_tables(t)
    qa = _heads(_rope(part(0, lat), cos, sin, False, "rope_q"), A_HEADS)
    ka = _heads(_rope(part(1, lat), cos, sin, False, "rope_k"), A_KV_HEADS)
    va = _heads(part(2, lat), A_KV_HEADS)
    kca, vca = _heads(part(1, con), A_KV_HEADS), _heads(part(2, con), A_KV_HEADS)
    kap, vap = _pad_rows(ka, WINDOW), _pad_rows(va, WINDOW)
    sink = wts["sink"].reshape(A_HEADS, 1, 1)
    o_a = _attn_fwd(qa, kap, vap, kca, vca, sink, "attn_fwd")

    qb, kb, vb = _heads(part(3), B_HEADS), _heads(part(4), B_HEADS), _heads(part(5), B_HEADS)
    rb = part(6, lat)
    zg = part(7)
    zg_f, zg_b = zg[:, :B_GATE_RANK], zg[:, B_GATE_RANK:]
    w2f, w2b, b2f, b2b = wts["w_a2_f"], wts["w_a2_b"], wts["b_a_f"], wts["b_a_b"]
    la_f = _heads(_gate_fwd(zg_f, w2f, b2f, "gate_f"), B_HEADS)
    la_b = _heads(_gate_fwd(zg_b, w2b, b2b, "gate_b"), B_HEADS)
    o_f, s_f, o_b, s_b = _gla_fwd(qb, kb, vb, la_f, la_b, t, "gla_fwd")
    gla_g = wts["gla_g"]
    go = _gla_out_fwd(o_f[:, :t], o_b[:, :t], rb, gla_g, "gla_out")
    cat = jnp.concatenate([_unheads(o_a), go], axis=-1).astype(BF16)
    x1 = xc1[:t]
    big = fetch(3, cat)
    w_out, wi2_0, wo2_0 = big["w_out"], big["wi2_0"], big["wo2_0"]
    x2, y_mix0 = _mm_resid(cat, w_out, 0, x1, mods[0], 5, 1.0, t, "w_out")
    x3, sv_f2 = ffn_fwd(x2, 0, 6, wi2_0, wo2_0, ng(0, 2), t, "l0f2")

    big = fetch(4, x3)
    wi1_1, wo1_1, wi2_1, wo2_1 = big["wi1_1"], big["wo1_1"], big["wi2_1"], big["wo2_1"]
    x4, sv_g1 = ffn_fwd(x3, 1, 0, wi1_1, wo1_1, ng(1, 0), t, "l1f1")
    hp = _modulate(x4, mods[1], ng(1, 1), 3, 4, t, F32, "mod_l1mix")
    w_pool, pool_scale = wts["w_pool"], wts["pool_scale"]
    x5, pooled, ypre = _pool_fwd(_pad_rows(hp, POOL_PAD), w_pool, pool_scale, x4, mods[1], 5, "pool_fwd")
    x6, sv_g2 = ffn_fwd(x5, 1, 6, wi2_1, wo2_1, ng(1, 2), t, "l1f2")

    dx6, loss_vec, dfinal_g = _final_loss(x6, wts["final_g"], target, "final_loss")
    grads["final_g"] = dfinal_g[0]

    dx5, acc_mod, acc_gate = ffn_bwd(dx6, sv_g2, 1, 6, ng(1, 2), t, "l1f2", 0)
    record(1, 6, 2, ng(1, 2), acc_mod, acc_gate, 1)
    dhp, dw_pool, acc_pool = _pool_bwd(_pad_rows(dx5, POOL_PAD), w_pool, pool_scale, pooled, ypre, mods[1], 5,
                                       "pool_bwd")
    grads["pool_scale"] = acc_pool[0]
    dmods[1][0][5] = acc_pool[1]
    dx4, acc_mod = _modulate_bwd(x4, dhp, dx5, mods[1], ng(1, 1), 4, t, "mod_bwd_l1mix")
    record(1, 3, 1, ng(1, 1), acc_mod, None, 1)
    dx3, acc_mod, acc_gate = ffn_bwd(dx4, sv_g1, 1, 0, ng(1, 0), t, "l1f1", 1)
    record(1, 0, 0, ng(1, 0), acc_mod, acc_gate, 1)

    dx2, acc_mod, acc_gate = ffn_bwd(dx3, sv_f2, 0, 6, ng(0, 2), t, "l0f2", 2)
    record(0, 6, 2, ng(0, 2), acc_mod, acc_gate, 1)
    dymix, acc_gate = _resid_bwd(dx2, y_mix0, mods[0], 5, 1.0, t, "resid_bwd_mix")
    dmods[0][0][5] = acc_gate[0, 0]
    dw_out = _mm_tn(cat, dymix, BF16, "dw_out")
    dcat = _mm([(dymix, w_out, 0, 0)], NT, cat.shape[1], F32, "dcat")
    do_a = _heads(dcat[:, :A_Q], A_HEADS)
    do_gla, drb, dgla_g = _gla_out_bwd(o_f[:, :t], o_b[:, :t], rb, gla_g, dcat[:, A_Q:], "gla_out_bwd")
    grads["gla_g"] = dgla_g[0]
    do_full = jnp.pad(do_gla, ((0, 0), (0, l_ctx), (0, 0)))
    dq_f, dk_f, dv_f, dla_f, dq_b, dk_b, dv_b, dla_b = _gla_bwd(qb, kb, vb, la_f, la_b, s_f, s_b, do_full, t, "gla_bwd")
    dzg_f, dw2f, db2f = _gate_bwd(zg_f, w2f, b2f, _unheads(dla_f), "gate_bwd_f")
    dzg_b, dw2b, db2b = _gate_bwd(zg_b, w2b, b2b, _unheads(dla_b), "gate_bwd_b")
    grads.update(w_a2_f=dw2f, w_a2_b=dw2b, b_a_f=db2f[0], b_a_b=db2b[0])
    dqa_r, dkap, dvap, dkca, dvca, dsink = _attn_bwd(qa, kap, vap, kca, vca, sink, o_a, do_a, "attn_bwd")
    grads["sink"] = dsink[:, 0, 0]
    dqa = _rope(_unheads(dqa_r), cos, sin, True, "rope_bwd_q")
    dka = _rope(_unheads(dkap[:, WINDOW:WINDOW + t]), cos, sin, True, "rope_bwd_k")
    dva = dvap[:, WINDOW:WINDOW + t]
    zrow = lambda a, n: jnp.pad(a, ((0, n), (0, 0)))
    dz_parts = [
        zrow(dqa, l_ctx),
        jnp.concatenate([dka, _unheads(dkca)], axis=0),
        jnp.concatenate([_unheads(dva), _unheads(dvca)], axis=0),
        _unheads(dq_f + dq_b), _unheads(dk_f + dk_b), _unheads(dv_f + dv_b),
        zrow(drb, l_ctx),
        jnp.concatenate([dzg_f, dzg_b], axis=-1),
        jnp.zeros((tc, n_proj - PROJ_DIM), F32),
    ]
    dzall = jnp.concatenate(dz_parts, axis=-1).astype(BF16)
    dw_in_t = _mm_tn(dzall, hc, BF16, "dw_in")
    token = emit(3, [dw_in_t, dw_out, dw_pool])
    dhc = _mm([(dzall, w_in_t, 0, 0)], NN, d, F32, "dhc", tm_pref=(256, 128, 64, 32, 16, 8), dep=token)
    dxc1_res = jnp.concatenate([dx2, jnp.zeros((l_ctx, d), F32)], axis=0)
    dxc1, acc_mod = _modulate_bwd(xc1, dhc, dxc1_res, mods[0], ng(0, 1), 4, t, "mod_bwd_l0mix")
    record(0, 3, 1, ng(0, 1), acc_mod, None, 2)
    dxc0, acc_mod, acc_gate = ffn_bwd(dxc1, sv_f1, 0, 0, ng(0, 0), t, "l0f1", 4)
    record(0, 0, 0, ng(0, 0), acc_mod, acc_gate, 2)

    grads["norm_g"] = jnp.stack([jnp.stack(dnorm[0]), jnp.stack(dnorm[1])])
    zero = jnp.zeros((d,), F32)
    dmods_arr = jnp.stack([jnp.stack([jnp.stack([v if v is not None else zero for v in dmods[l][s]])
                                      for s in range(2)]) for l in range(2)])
    return loss_vec, dxc0[:t], grads, dmods_arr


def _pack(parts):
    flat = jnp.concatenate([p.reshape(-1).astype(F32) for p in parts])
    pad = (-flat.shape[0]) % 128
    return jnp.pad(flat, (0, pad))[None, :]


def _unpack(rows, shapes):
    out, off = [], 0
    for s in shapes:
        n = int(np.prod(s))
        out.append(rows[:, off:off + n].reshape((rows.shape[0],) + tuple(s)))
        off += n
    return out


def _cols_to_full(g):
    g = jnp.moveaxis(g, 0, -2)
    return g.reshape(g.shape[:-2] + (-1,))


def kernel(x, c, ctx, c_ctx, w_mod, b_mod, norm_g, ffn1_wi, ffn1_wo, ffn2_wi, ffn2_wo, w_in, w_a2_f, b_a_f, w_a2_b, b_a_b, sink, gla_g, w_out, w_pool, pool_scale, final_g, loss_target, m_c_ctx, m_w_mod, m_b_mod, m_norm_g, m_ffn1_wi, m_ffn1_wo, m_ffn2_wi, m_ffn2_wo, m_w_in, m_w_a2_f, m_b_a_f, m_w_a2_b, m_b_a_b, m_sink, m_gla_g, m_w_out, m_w_pool, m_pool_scale, m_final_g, v_c_ctx, v_w_mod, v_b_mod, v_norm_g, v_ffn1_wi, v_ffn1_wo, v_ffn2_wi, v_ffn2_wo, v_w_in, v_w_a2_f, v_b_a_f, v_w_a2_b, v_b_a_b, v_sink, v_gla_g, v_w_out, v_w_pool, v_pool_scale, v_final_g):
    t, d = x.shape[1], x.shape[2]
    me = _dev_index()
    nc = w_mod.shape[2]
    ncol_in = w_in.shape[2]
    ncol_pad = -(-ncol_in // 16) * 16

    small_shapes = [(d,), norm_g.shape, pool_scale.shape, w_a2_f.shape, w_a2_b.shape, w_pool.shape]
    g1 = _gather_small(_pack([c, norm_g, pool_scale, w_a2_f, w_a2_b, w_pool]), "gather_params")
    c_all, norm_g_all, pool_scale_all, w2f_all, w2b_all, w_pool_all = _unpack(g1, small_shapes)
    wts = {
        "norm_g": _cols_to_full(norm_g_all),
        "pool_scale": _cols_to_full(pool_scale_all),
        "w_a2_f": _cols_to_full(w2f_all)[0],
        "w_a2_b": _cols_to_full(w2b_all)[0],
        "w_pool": jnp.moveaxis(w_pool_all[:, 0], 0, 1).reshape(w_pool.shape[1], -1, w_pool.shape[3]),
        "b_a_f": b_a_f, "b_a_b": b_a_b, "sink": sink[0], "gla_g": gla_g, "final_g": final_g[None, :],
    }

    craw = jnp.concatenate([c_all, c_ctx[None, :], jnp.zeros((16 - N_DEV - 1, d), F32)], axis=0)
    b_cols = lax.dynamic_slice_in_dim(b_mod, me * nc, nc, axis=1)[:, None, :]
    mm_cols = _adaln_fwd(craw, w_mod, b_cols, "adaln_fwd")
    g2 = _gather_small(mm_cols.reshape(1, -1), "gather_mods").reshape(N_DEV, 2, 16, nc)
    mm_full = jnp.moveaxis(g2, 0, 2).reshape(2, 16, N_MOD, d)
    mods = jnp.stack([lax.dynamic_index_in_dim(mm_full, me, axis=1, keepdims=False), mm_full[:, N_DEV]], axis=1)
    mods = jnp.pad(mods, ((0, 0), (0, 0), (0, 16 - N_MOD), (0, 0)))

    tr = lambda w: jnp.swapaxes(w, 1, 2).astype(BF16)
    wi1_sh, wi2_sh, wo1_sh, wo2_sh = tr(ffn1_wi), tr(ffn2_wi), ffn1_wo.astype(BF16), ffn2_wo.astype(BF16)
    w_in_sh = jnp.pad(tr(w_in), ((0, 0), (0, ncol_pad - ncol_in), (0, 0)))
    groups = [
        {"wi1_0": wi1_sh[0:1]},
        {"wo1_0": wo1_sh[0:1]},
        {"w_in": w_in_sh},
        {"w_out": w_out.astype(BF16), "wi2_0": wi2_sh[0:1], "wo2_0": wo2_sh[0:1]},
        {"wi1_1": wi1_sh[1:2], "wo1_1": wo1_sh[1:2], "wi2_1": wi2_sh[1:2], "wo2_1": wo2_sh[1:2]},
    ]

    def land_of(shard):
        a_, r, c_ = shard.shape
        return lax.dynamic_update_slice(lax.empty((a_, N_DEV * r, c_), shard.dtype), shard, (0, me * r, 0))

    gathers, token = [], None
    for gi, grp in enumerate(groups):
        shards = list(grp.values())
        gathers.append(_exchange_start(shards, [land_of(s) for s in shards], True, 1 + gi, token, f"gather_start_{gi}"))
        token = gathers[-1][4]
    n_proj = -(-(N_DEV * ncol_in) // 128) * 128

    def fetch(gi, after):
        _, lands = _exchange_wait(gathers[gi], True, token if after is None else after, f"gather_wait_{gi}")
        out = dict(zip(groups[gi].keys(), lands))
        if "w_in" in out:
            w_in_t = out.pop("w_in").reshape(1, N_DEV, ncol_pad, d)[:, :, :ncol_in].reshape(1, N_DEV * ncol_in, d)
            out["w_in_t"] = jnp.pad(w_in_t, ((0, 0), (0, n_proj - N_DEV * ncol_in), (0, 0)))
        return out

    scatters = []

    def emit(stage, arrays):
        if stage == 3:
            dw_in_t, dw_out, dw_pool = arrays
            dw_in_full = dw_in_t[:N_DEV * ncol_in].reshape(N_DEV, ncol_in, d)
            dw_in_full = jnp.pad(dw_in_full, ((0, 0), (0, ncol_pad - ncol_in), (0, 0)))
            srcs = [dw_in_full.reshape(1, N_DEV * ncol_pad, d), dw_out[None], dw_pool.astype(BF16)]
        else:
            srcs = [a[None] for a in arrays]
        lands = [lax.empty((N_DEV, s.shape[0], s.shape[1] // N_DEV, s.shape[2]), s.dtype) for s in srcs]
        scatters.append(_exchange_start(srcs, lands, False, 1 + len(groups) + stage, None, f"scatter_start_{stage}"))
        return scatters[-1][4]

    loss_vec, grad_x, grads, dmods = _local_step(x[0], ctx[0], loss_target[0], mods, wts, fetch, emit)
    loss = lax.psum(jnp.sum(loss_vec), ("x", "y", "c"))

    def reduce_stage(stage, after):
        wholes, lands = _exchange_wait(scatters[stage], False, after, f"scatter_wait_{stage}")
        return [_sum_slots(ld, wh, me, f"sum_grad_{stage}_{i}") for i, (ld, wh) in enumerate(zip(lands, wholes))]

    (dwi2_1, dwo2_1), (dwi1_1, dwo1_1), (dwi2_0, dwo2_0), (dw_in_s, dw_out_s, dw_pool_s) = [
        reduce_stage(stage, grad_x) for stage in range(4)]
    back = lambda g: jnp.swapaxes(g, 1, 2)
    g_big = {
        "ffn2_wi": back(jnp.concatenate([dwi2_0, dwi2_1], axis=0)), "ffn2_wo": jnp.concatenate([dwo2_0, dwo2_1], axis=0),
        "w_in": back(dw_in_s[:, :ncol_in]), "w_out": dw_out_s, "w_pool": dw_pool_s[None],
    }

    small_g = [dmods[:, :, :N_MOD].reshape(2, 2, N_MOD * d), grads["norm_g"], grads["pool_scale"], grads["final_g"],
               grads["b_a_f"], grads["b_a_b"], grads["sink"], grads["gla_g"], grads["w_a2_f"], grads["w_a2_b"]]
    small_g_shapes = [a.shape for a in small_g]
    g3 = _gather_small(_pack(small_g), "gather_small_grads")
    total = _sum_rows8(g3, "sum_small_grads")
    dmm_all = _unpack(g3, small_g_shapes[:1])[0]
    (dmm_sum, dnorm_g, dpool_scale, dfinal_g, db_a_f, db_a_b, dsink, dgla_g, dw_a2_f, dw_a2_b) = [
        a[0] for a in _unpack(total, small_g_shapes)]
    dmm_rows = jnp.concatenate([dmm_all[:, :, 0].transpose(1, 0, 2), dmm_sum[:, 1][:, None, :],
                                jnp.zeros((2, 16 - N_DEV - 1, N_MOD * d), F32)], axis=1)
    grad_b_mod = dmm_sum[:, 0] + dmm_sum[:, 1]
    dmm_cols = lax.dynamic_slice_in_dim(dmm_rows, me * nc, nc, axis=2)
    cs_t = jnp.transpose(_silu(craw)).astype(BF16)
    grad_w_mod, dcraw = _adaln_bwd(craw, cs_t, dmm_cols, w_mod, "adaln_bwd")
    g4 = _gather_small((dcraw[0, N_DEV] + dcraw[1, N_DEV])[None, :], "gather_c_ctx_grad")
    grad_c_ctx = _sum_rows8(g4, "sum_c_ctx_grad")[0]

    col = lambda v, n: lax.dynamic_slice_in_dim(v, me * n, n, axis=v.ndim - 1)
    g_small = {
        "c_ctx": grad_c_ctx, "b_mod": grad_b_mod, "norm_g": col(dnorm_g, norm_g.shape[2]),
        "w_a2_f": col(dw_a2_f, w_a2_f.shape[2])[None], "b_a_f": db_a_f[None], "w_a2_b": col(dw_a2_b, w_a2_b.shape[2])[None],
        "b_a_b": db_a_b[None], "sink": dsink[None], "gla_g": dgla_g[None], "pool_scale": col(dpool_scale, pool_scale.shape[1])[None],
        "final_g": dfinal_g,
    }
    g_all = {**g_big, **g_small, "w_mod": grad_w_mod}

    order = ["c_ctx", "w_mod", "b_mod", "norm_g", "ffn1_wi", "ffn1_wo", "ffn2_wi", "ffn2_wo", "w_in", "w_a2_f", "b_a_f",
             "w_a2_b", "b_a_b", "sink", "gla_g", "w_out", "w_pool", "pool_scale", "final_g"]
    ws = dict(c_ctx=c_ctx, w_mod=w_mod, b_mod=b_mod, norm_g=norm_g, ffn1_wi=ffn1_wi, ffn1_wo=ffn1_wo, ffn2_wi=ffn2_wi,
              ffn2_wo=ffn2_wo, w_in=w_in, w_a2_f=w_a2_f, b_a_f=b_a_f, w_a2_b=w_a2_b, b_a_b=b_a_b, sink=sink, gla_g=gla_g,
              w_out=w_out, w_pool=w_pool, pool_scale=pool_scale, final_g=final_g)
    ms = dict(c_ctx=m_c_ctx, w_mod=m_w_mod, b_mod=m_b_mod, norm_g=m_norm_g, ffn1_wi=m_ffn1_wi, ffn1_wo=m_ffn1_wo,
              ffn2_wi=m_ffn2_wi, ffn2_wo=m_ffn2_wo, w_in=m_w_in, w_a2_f=m_w_a2_f, b_a_f=m_b_a_f, w_a2_b=m_w_a2_b,
              b_a_b=m_b_a_b, sink=m_sink, gla_g=m_gla_g, w_out=m_w_out, w_pool=m_w_pool, pool_scale=m_pool_scale,
              final_g=m_final_g)
    vs = dict(c_ctx=v_c_ctx, w_mod=v_w_mod, b_mod=v_b_mod, norm_g=v_norm_g, ffn1_wi=v_ffn1_wi, ffn1_wo=v_ffn1_wo,
              ffn2_wi=v_ffn2_wi, ffn2_wo=v_ffn2_wo, w_in=v_w_in, w_a2_f=v_w_a2_f, b_a_f=v_b_a_f, w_a2_b=v_w_a2_b,
              b_a_b=v_b_a_b, sink=v_sink, gla_g=v_gla_g, w_out=v_w_out, w_pool=v_w_pool, pool_scale=v_pool_scale,
              final_g=v_final_g)
    big = ["w_mod", "ffn2_wi", "ffn2_wo", "w_out", "ffn1_wi", "ffn1_wo"]
    delta, new_m, new_v = {}, {}, {}

    def adamw_big(nm):
        shp = ws[nm].shape
        two_d = lambda a: a.reshape(-1, shp[-1])
        dl, nm_, nv_ = _adamw(two_d(ws[nm]), two_d(g_all[nm]), two_d(ms[nm]), two_d(vs[nm]), f"adamw_{nm}")
        delta[nm], new_m[nm], new_v[nm] = dl.reshape(shp), nm_.reshape(shp), nv_.reshape(shp)

    for nm in big[:4]:
        adamw_big(nm)
    rest = [nm for nm in order if nm not in big]
    rest_shapes = [ws[nm].shape for nm in rest]
    packed = [_pack([d_[nm].reshape(ws[nm].shape) for nm in rest]).reshape(-1, 128) for d_ in (ws, g_all, ms, vs)]
    pad_rows = (-packed[0].shape[0]) % 512
    packed = [jnp.pad(p, ((0, pad_rows), (0, 0))) for p in packed]
    outs = _adamw(*packed, "adamw_small")
    for dst, arr in zip((delta, new_m, new_v), outs):
        for nm, val in zip(rest, _unpack(arr.reshape(1, -1), rest_shapes)):
            dst[nm] = val[0]

    dwi1_0, dwo1_0 = reduce_stage(4, outs[0])
    g_all["ffn1_wi"] = back(jnp.concatenate([dwi1_0, dwi1_1], axis=0))
    g_all["ffn1_wo"] = jnp.concatenate([dwo1_0, dwo1_1], axis=0)
    for nm in big[4:]:
        adamw_big(nm)
    g_all = {nm: g_all[nm].reshape(ws[nm].shape) for nm in order}

    return (loss, grad_x[None], *[g_all[nm] for nm in order], *[delta[nm] for nm in order],
            *[new_m[nm] for nm in order], *[new_v[nm] for nm in order])
```

```python
import functools

import numpy as np
import jax
import jax.numpy as jnp
from jax import lax
from jax.experimental import pallas as pl
from jax.experimental.pallas import tpu as pltpu

F32 = jnp.float32
BF16 = jnp.bfloat16
HI = lax.Precision.HIGHEST
MESH = pl.DeviceIdType.MESH

N_DEV = 8
RMS_EPS = 1e-6
N_MOD = 9
GRID_W = 64
A_HEADS, A_KV_HEADS, A_HEAD_DIM = 8, 2, 64
A_REP = A_HEADS // A_KV_HEADS
WINDOW = 128
ROPE_BASE = 10000.0
B_HEADS, B_DK, B_DV = 4, 64, 128
B_GATE_RANK = 16
B_GATE_NORM = 16.0
B_CHUNK = 64
POOL_WINDOWS = (2, 4, 8, 16)
POOL_PAD = 8
A_Q = A_HEADS * A_HEAD_DIM
A_KV = A_KV_HEADS * A_HEAD_DIM
B_QK = B_HEADS * B_DK
B_V = B_HEADS * B_DV
PROJ_SIZES = (A_Q, A_KV, A_KV, B_QK, B_QK, B_V, B_V, 2 * B_GATE_RANK)
PROJ_DIM = sum(PROJ_SIZES)
ADAM_LR, ADAM_B1, ADAM_B2, ADAM_EPS, ADAM_WD, ADAM_STEP = 0.001, 0.9, 0.999, 1e-08, 0.01, 10

VMEM_LIMIT = 56 * 1024 * 1024
ROW_TILES = (512, 544, 256, 128, 64, 32, 16, 8)

NN = ((1,), (0,))
NT = ((1,), (1,))
TN = ((0,), (0,))


def _dot(a, b, dims=NN, prec=None):
    return lax.dot_general(a, b, (dims, ((), ())), precision=prec, preferred_element_type=F32)


def _bdot(a, b, dims=NN):
    return _dot(a.astype(BF16), b.astype(BF16), dims)


def _params(sem=None, **kw):
    return pltpu.CompilerParams(dimension_semantics=sem, vmem_limit_bytes=VMEM_LIMIT, **kw)


def _silu(a):
    return a * jax.nn.sigmoid(a)


def _pick(n, prefs):
    for p in prefs:
        if n % p == 0:
            return p
    return n


def _full(shape):
    nd = len(shape)
    return pl.BlockSpec(shape, lambda *_: (0,) * nd)


def _peers():
    x, y, c = lax.axis_index("x"), lax.axis_index("y"), lax.axis_index("c")
    return x, y, c


def _dev_index():
    x, y, c = _peers()
    return 4 * x + 2 * y + c


def _others(x, y, c):
    return [(x, y, 1 - c), (1 - x, y, c), (x, 1 - y, c), (1 - x, 1 - y, c),
            (1 - x, y, 1 - c), (x, 1 - y, 1 - c), (1 - x, 1 - y, 1 - c)]


def _index_of(dev):
    return 4 * dev[0] + 2 * dev[1] + dev[2]


def _exchange_refs(gather, shapes, srcs, lands, a, me, to):
    if gather:
        r = shapes[a][1]
        return srcs[a], lands[a].at[:, pl.ds(_index_of(me) * r, r), :]
    r = shapes[a][1] // N_DEV
    return srcs[a].at[:, pl.ds(_index_of(to) * r, r), :], lands[a].at[_index_of(me)]


HBM_SPEC = pl.BlockSpec(memory_space=pltpu.HBM)
SEM_SPEC = pl.BlockSpec(memory_space=pltpu.SEMAPHORE)
EFFECT = pltpu.SideEffectType.DATAFLOW_SIDE_EFFECTING


def _exchange_start(srcs, lands, gather, collective_id, dep, name):
    n = len(srcs)
    shapes = [s.shape for s in srcs]
    deps = [] if dep is None else [dep]

    def body(*refs):
        src_refs, land_refs = refs[:n], refs[n:2 * n]
        send_sems, recv_sems = refs[2 * n + len(deps)], refs[2 * n + len(deps) + 1]
        token = refs[-1]
        x, y, c = _peers()
        others = _others(x, y, c)
        barrier = pltpu.get_barrier_semaphore()
        for peer in others:
            pl.semaphore_signal(barrier, inc=1, device_id=peer, device_id_type=MESH)
        pl.semaphore_wait(barrier, len(others))
        for a in range(n):
            for k, to in enumerate(others):
                src, dst = _exchange_refs(gather, shapes, src_refs, land_refs, a, (x, y, c), to)
                pltpu.make_async_remote_copy(src_ref=src, dst_ref=dst, send_sem=send_sems.at[7 * a + k],
                                             recv_sem=recv_sems.at[7 * a + k], device_id=to, device_id_type=MESH).start()
        token[...] = jnp.zeros_like(token)

    outs = pl.pallas_call(
        body, name=name,
        out_shape=(pltpu.SemaphoreType.DMA((7 * n,)), pltpu.SemaphoreType.DMA((7 * n,)),
                   *[pltpu.HBM(s.shape, s.dtype) for s in srcs], *[pltpu.HBM(l.shape, l.dtype) for l in lands],
                   jax.ShapeDtypeStruct((8, 128), F32)),
        in_specs=[HBM_SPEC] * (2 * n) + [pl.BlockSpec(memory_space=pl.ANY)] * len(deps),
        out_specs=(SEM_SPEC, SEM_SPEC, *[HBM_SPEC] * (2 * n), pl.BlockSpec(memory_space=pltpu.VMEM)),
        input_output_aliases={i: 2 + i for i in range(2 * n)},
        compiler_params=pltpu.CompilerParams(has_side_effects=EFFECT, collective_id=collective_id),
    )(*[pltpu.with_memory_space_constraint(s, pltpu.HBM) for s in srcs],
      *[pltpu.with_memory_space_constraint(l, pltpu.HBM) for l in lands], *deps)
    return outs[0], outs[1], list(outs[2:2 + n]), list(outs[2 + n:2 + 2 * n]), outs[-1]


def _exchange_wait(started, gather, after, name):
    send_sems, recv_sems, srcs, lands, _ = started
    n = len(srcs)
    shapes = [s.shape for s in srcs]

    def body(*refs):
        src_refs, land_refs = refs[:n], refs[n:2 * n]
        send_sems, recv_sems = refs[2 * n], refs[2 * n + 1]
        x, y, c = _peers()
        for a in range(n):
            for k, peer in enumerate(_others(x, y, c)):
                src, _ = _exchange_refs(gather, shapes, src_refs, land_refs, a, (x, y, c), peer)
                _, dst = _exchange_refs(gather, shapes, src_refs, land_refs, a, peer, (x, y, c))
                copy = pltpu.make_async_remote_copy(src_ref=src, dst_ref=dst, send_sem=send_sems.at[7 * a + k],
                                                    recv_sem=recv_sems.at[7 * a + k], device_id=peer, device_id_type=MESH)
                copy.wait_send()
                copy.wait_recv()

    outs = pl.pallas_call(
        body, name=name,
        out_shape=(*[pltpu.HBM(s.shape, s.dtype) for s in srcs], *[pltpu.HBM(l.shape, l.dtype) for l in lands]),
        in_specs=[HBM_SPEC] * (2 * n) + [SEM_SPEC, SEM_SPEC, pl.BlockSpec(memory_space=pl.ANY)],
        out_specs=tuple([HBM_SPEC] * (2 * n)),
        input_output_aliases={i: i for i in range(2 * n)},
        compiler_params=pltpu.CompilerParams(has_side_effects=EFFECT),
    )(*srcs, *lands, send_sems, recv_sems, after)
    return list(outs[:n]), list(outs[n:])


def _sum_slots(land, whole, me, name):
    _, a_, r, c = land.shape
    tr = _pick(r, (352, 256, 128, 64, 32, 16, 8))
    nr = r // tr

    def body(me_ref, land_ref, own_ref, out_ref):
        acc = None
        for s in range(N_DEV):
            part = jnp.where(me_ref[0] == s, own_ref[...], land_ref[s]).astype(F32)
            acc = part if acc is None else acc + part
        out_ref[...] = acc

    return pl.pallas_call(
        body, name=name,
        grid_spec=pltpu.PrefetchScalarGridSpec(
            num_scalar_prefetch=1, grid=(a_, nr),
            in_specs=[pl.BlockSpec((N_DEV, None, tr, c), lambda i, j, me_ref: (0, i, j, 0)),
                      pl.BlockSpec((None, tr, c), lambda i, j, me_ref: (i, me_ref[0] * nr + j, 0))],
            out_specs=pl.BlockSpec((None, tr, c), lambda i, j, me_ref: (i, j, 0))),
        out_shape=jax.ShapeDtypeStruct((a_, r, c), F32),
        compiler_params=_params(("parallel", "parallel")),
    )(me.reshape(1).astype(jnp.int32), land, whole)


def _gather_small(vec, name, dep=None):
    p = vec.shape[1]
    pp = -(-p // 1024) * 1024
    blk = jnp.pad(vec, ((0, 0), (0, pp - p))).reshape(8, pp // 8)
    deps = [] if dep is None else [dep]

    def body(in_ref, *rest):
        out_ref, send_sems, recv_sems = rest[-3:]
        x, y, c = _peers()
        me = 4 * x + 2 * y + c
        others = [(x, y, 1 - c), (1 - x, y, c), (x, 1 - y, c), (1 - x, 1 - y, c),
                  (1 - x, y, 1 - c), (x, 1 - y, 1 - c), (1 - x, 1 - y, 1 - c)]

        def rows(idx):
            return out_ref.at[pl.ds(pl.multiple_of(idx * 8, 8), 8), :]

        out_ref[pl.ds(pl.multiple_of(me * 8, 8), 8), :] = in_ref[...]

        def copy(k, dev, slot):
            return pltpu.make_async_remote_copy(
                src_ref=in_ref, dst_ref=rows(slot), send_sem=send_sems.at[k], recv_sem=recv_sems.at[k],
                device_id=dev, device_id_type=MESH)

        sends = [copy(k, dev, me) for k, dev in enumerate(others)]
        for cp in sends:
            cp.start()
        for k, dev in enumerate(others):
            copy(k, dev, 4 * dev[0] + 2 * dev[1] + dev[2]).wait_recv()
        for cp in sends:
            cp.wait_send()

    vm = pl.BlockSpec(memory_space=pltpu.VMEM)
    out = pl.pallas_call(
        body, name=name, out_shape=jax.ShapeDtypeStruct((8 * N_DEV, pp // 8), F32),
        in_specs=[vm] + [pl.BlockSpec(memory_space=pl.ANY)] * len(deps), out_specs=vm,
        scratch_shapes=[pltpu.SemaphoreType.DMA((7,)), pltpu.SemaphoreType.DMA((7,))],
        compiler_params=pltpu.CompilerParams(has_side_effects=True, vmem_limit_bytes=VMEM_LIMIT),
    )(blk, *deps)
    return out.reshape(N_DEV, pp)[:, :p]


def _sum_rows8(g, name):
    p = g.shape[1]

    def body(in_ref, out_ref):
        acc = in_ref[0:1, :]
        for s in range(1, N_DEV):
            acc = acc + in_ref[s:s + 1, :]
        out_ref[...] = acc

    return pl.pallas_call(body, name=name, out_shape=jax.ShapeDtypeStruct((1, p), F32),
                          compiler_params=_params())(g)


def _sel_row(mods_ref, is_ctx, k):
    return jnp.where(is_ctx, mods_ref[1, k:k + 1, :], mods_ref[0, k:k + 1, :])


def _modulate(z, mods, g, ks, kc, n_x, out_dtype, name):
    m, d = z.shape
    tm = _pick(m, (256, 128, 64, 32, 16, 8))

    def body(z_ref, mods_ref, g_ref, h_ref):
        is_ctx = pl.program_id(0) * tm >= n_x
        zz = z_ref[...]
        r = lax.rsqrt(jnp.mean(zz * zz, axis=-1, keepdims=True) + RMS_EPS)
        shift, scale = _sel_row(mods_ref, is_ctx, ks), _sel_row(mods_ref, is_ctx, kc)
        h_ref[...] = ((zz * r) * g_ref[...] * (1.0 + scale) + shift).astype(out_dtype)

    return pl.pallas_call(
        body, name=name, grid=(m // tm,),
        in_specs=[pl.BlockSpec((tm, d), lambda i: (i, 0)), _full(mods.shape), _full(g.shape)],
        out_specs=pl.BlockSpec((tm, d), lambda i: (i, 0)),
        out_shape=jax.ShapeDtypeStruct((m, d), out_dtype),
        compiler_params=_params(("parallel",)),
    )(z, mods, g)


def _modulate_bwd(z, dh, dres, mods, g, kc, n_x, name):
    m, d = z.shape
    tm = _pick(m, (256, 128, 64, 32, 16, 8))
    first_ctx = n_x // tm

    def body(z_ref, dh_ref, dres_ref, mods_ref, g_ref, dx_ref, acc_ref):
        i = pl.program_id(0)
        is_ctx = i * tm >= n_x

        @pl.when((i == 0) | (i == first_ctx))
        def _():
            acc_ref[...] = jnp.zeros_like(acc_ref)

        zz, dhh = z_ref[...], dh_ref[...]
        r = lax.rsqrt(jnp.mean(zz * zz, axis=-1, keepdims=True) + RMS_EPS)
        nz = zz * r
        gain = g_ref[...] * (1.0 + _sel_row(mods_ref, is_ctx, kc))
        dn = dhh * gain
        dz = r * (dn - nz * jnp.mean(dn * nz, axis=-1, keepdims=True))
        dx_ref[...] = dres_ref[...] + dz
        acc_ref[0:1, :] += jnp.sum(dhh, axis=0, keepdims=True)
        acc_ref[1:2, :] += jnp.sum(dhh * nz, axis=0, keepdims=True)

    row = pl.BlockSpec((tm, d), lambda i: (i, 0))
    return pl.pallas_call(
        body, name=name, grid=(m // tm,),
        in_specs=[row, row, row, _full(mods.shape), _full(g.shape)],
        out_specs=[row, pl.BlockSpec((None, 8, d), lambda i: ((i * tm >= n_x).astype(jnp.int32), 0, 0))],
        out_shape=[jax.ShapeDtypeStruct((m, d), F32), jax.ShapeDtypeStruct((2, 8, d), F32)],
        compiler_params=_params(("arbitrary",)),
    )(z, dh, dres, mods, g)


def _ffn_up(h, wi_t, layer, name):
    m, d = h.shape
    f = wi_t.shape[1] // 2
    tm = _pick(m, ROW_TILES)

    def body(h_ref, w_ref, au_ref, act_ref):
        hh = h_ref[...]
        a = _dot(hh, w_ref[0:f, :], NT)
        u = _dot(hh, w_ref[f:2 * f, :], NT)
        au_ref[:, 0:f] = a.astype(BF16)
        au_ref[:, f:2 * f] = u.astype(BF16)
        act_ref[...] = (_silu(a) * u).astype(BF16)

    return pl.pallas_call(
        body, name=name, grid=(m // tm,),
        in_specs=[pl.BlockSpec((tm, d), lambda i: (i, 0)),
                  pl.BlockSpec((None, 2 * f, d), lambda i: (layer, 0, 0))],
        out_specs=[pl.BlockSpec((tm, 2 * f), lambda i: (i, 0)), pl.BlockSpec((tm, f), lambda i: (i, 0))],
        out_shape=[jax.ShapeDtypeStruct((m, 2 * f), BF16), jax.ShapeDtypeStruct((m, f), BF16)],
        compiler_params=_params(("parallel",)),
    )(h, wi_t)


def _mm_resid(a, b, layer, res, mods, km, coef, n_x, name):
    m, k = a.shape
    n = b.shape[2]
    tm = _pick(m, (512, 256, 128, 64, 32, 16, 8))
    tn = _pick(n, (1024, 512, 256, 128))

    def body(a_ref, b_ref, res_ref, mods_ref, out_ref, y_ref):
        is_ctx = pl.program_id(1) * tm >= n_x
        y = _dot(a_ref[...], b_ref[...])
        y_ref[...] = y
        out_ref[...] = res_ref[...] + coef * _sel_row(mods_ref, is_ctx, km) * y

    tile = pl.BlockSpec((tm, tn), lambda j, i: (i, j))
    return pl.pallas_call(
        body, name=name, grid=(n // tn, m // tm),
        in_specs=[pl.BlockSpec((tm, k), lambda j, i: (i, 0)),
                  pl.BlockSpec((None, k, tn), lambda j, i: (layer, 0, j)),
                  tile, pl.BlockSpec((2, 16, tn), lambda j, i: (0, 0, j))],
        out_specs=[tile, tile],
        out_shape=[jax.ShapeDtypeStruct((m, n), F32), jax.ShapeDtypeStruct((m, n), F32)],
        compiler_params=_params(("parallel", "parallel")),
    )(a, b, res, mods)


def _resid_bwd(dx, y, mods, km, coef, n_x, name, dep=None):
    m, d = dx.shape
    tm = _pick(m, (256, 128, 64, 32, 16, 8))
    first_ctx = n_x // tm
    deps = [] if dep is None else [dep]

    def body(dx_ref, y_ref, mods_ref, *rest):
        dy_ref, acc_ref = rest[-2:]
        i = pl.program_id(0)
        is_ctx = i * tm >= n_x

        @pl.when((i == 0) | (i == first_ctx))
        def _():
            acc_ref[...] = jnp.zeros_like(acc_ref)

        dxx = dx_ref[...]
        dy_ref[...] = (coef * _sel_row(mods_ref, is_ctx, km) * dxx).astype(BF16)
        acc_ref[0:1, :] += jnp.sum(coef * y_ref[...] * dxx, axis=0, keepdims=True)

    row = pl.BlockSpec((tm, d), lambda i: (i, 0))
    return pl.pallas_call(
        body, name=name, grid=(m // tm,),
        in_specs=[row, row, _full(mods.shape)] + [pl.BlockSpec(memory_space=pl.ANY)] * len(deps),
        out_specs=[row, pl.BlockSpec((None, 8, d), lambda i: ((i * tm >= n_x).astype(jnp.int32), 0, 0))],
        out_shape=[jax.ShapeDtypeStruct((m, d), BF16), jax.ShapeDtypeStruct((2, 8, d), F32)],
        compiler_params=_params(("arbitrary",)),
    )(dx, y, mods, *deps)


def _ffn_down_bwd(dy, wo, layer, au, name):
    m, d = dy.shape
    f = wo.shape[1]
    tm = _pick(m, ROW_TILES)

    def body(dy_ref, wo_ref, au_ref, dau_ref):
        dact = _dot(dy_ref[...], wo_ref[...], NT)
        aa, uu = au_ref[:, 0:f].astype(F32), au_ref[:, f:2 * f].astype(F32)
        sg = jax.nn.sigmoid(aa)
        dau_ref[:, 0:f] = (dact * uu * (sg * (1.0 + aa * (1.0 - sg)))).astype(BF16)
        dau_ref[:, f:2 * f] = (dact * (aa * sg)).astype(BF16)

    wide = pl.BlockSpec((tm, 2 * f), lambda i: (i, 0))
    return pl.pallas_call(
        body, name=name, grid=(m // tm,),
        in_specs=[pl.BlockSpec((tm, d), lambda i: (i, 0)), pl.BlockSpec((None, f, d), lambda i: (layer, 0, 0)), wide],
        out_specs=wide, out_shape=jax.ShapeDtypeStruct((m, 2 * f), BF16),
        compiler_params=_params(("parallel",)),
    )(dy, wo, au)


def _mm(terms, dims, n, out_dtype, name, tm_pref=(512, 256, 128, 64, 32, 16, 8), tn_pref=(512, 256, 128), dep=None):
    m = terms[0][0].shape[0]
    tm = _pick(m, tm_pref)
    tn = _pick(n, tn_pref)
    nt = len(terms)
    deps = [] if dep is None else [dep]

    def body(*refs):
        out_ref = refs[-1]
        acc = None
        for t in range(nt):
            part = _dot(refs[2 * t][...].astype(BF16), refs[2 * t + 1][...].astype(BF16), dims)
            acc = part if acc is None else acc + part
        out_ref[...] = acc.astype(out_dtype)

    in_specs, args = [], []
    for a, b, layer, rb in terms:
        k = a.shape[1]
        in_specs.append(pl.BlockSpec((tm, k), lambda j, i: (i, 0)))
        if dims == NN:
            in_specs.append(pl.BlockSpec((None, k, tn), lambda j, i, layer=layer, rb=rb: (layer, rb, j)))
        else:
            nb = n // tn
            in_specs.append(pl.BlockSpec((None, tn, k), lambda j, i, layer=layer, rb=rb, nb=nb: (layer, rb * nb + j, 0)))
        args += [a, b]
    return pl.pallas_call(
        body, name=name, grid=(n // tn, m // tm), in_specs=in_specs + [pl.BlockSpec(memory_space=pl.ANY)] * len(deps),
        out_specs=pl.BlockSpec((tm, tn), lambda j, i: (i, j)),
        out_shape=jax.ShapeDtypeStruct((m, n), out_dtype),
        compiler_params=_params(("parallel", "parallel")),
    )(*args, *deps)


def _mm_tn(a, b, out_dtype, name):
    t = a.shape[0]
    m, n = a.shape[1], b.shape[1]
    tm = _pick(m, (1408, 2432, 1024, 512, 256, 128))
    tn = _pick(n, (1024, 512, 256, 128))
    tk = _pick(t, (512, 256, 128, 64, 32, 16, 8))

    def body(a_ref, b_ref, out_ref, acc_ref):
        kk = pl.program_id(2)

        @pl.when(kk == 0)
        def _():
            acc_ref[...] = jnp.zeros_like(acc_ref)

        acc_ref[...] += _dot(a_ref[...].astype(BF16), b_ref[...].astype(BF16), TN)

        @pl.when(kk == pl.num_programs(2) - 1)
        def _():
            out_ref[...] = acc_ref[...].astype(out_dtype)

    return pl.pallas_call(
        body, name=name, grid=(m // tm, n // tn, t // tk),
        in_specs=[pl.BlockSpec((tk, tm), lambda i, j, k: (k, i)), pl.BlockSpec((tk, tn), lambda i, j, k: (k, j))],
        out_specs=pl.BlockSpec((tm, tn), lambda i, j, k: (i, j)),
        out_shape=jax.ShapeDtypeStruct((m, n), out_dtype),
        scratch_shapes=[pltpu.VMEM((tm, tn), F32)],
        compiler_params=_params(("parallel", "parallel", "arbitrary")),
    )(a, b)


def _final_loss(x, g, target, name):
    t, d = x.shape
    tm = _pick(t, (256, 128, 64, 32, 16, 8))

    def body(x_ref, g_ref, t_ref, dx_ref, loss_ref, dg_ref):
        @pl.when(pl.program_id(0) == 0)
        def _():
            loss_ref[...] = jnp.zeros_like(loss_ref)
            dg_ref[...] = jnp.zeros_like(dg_ref)

        xx, gg = x_ref[...], g_ref[...]
        r = lax.rsqrt(jnp.mean(xx * xx, axis=-1, keepdims=True) + RMS_EPS)
        nz = xx * r
        err = nz * gg - t_ref[...]
        loss_ref[...] += jnp.sum(err * err, axis=0, keepdims=True) * (0.5 / d)
        dout = err * (1.0 / d)
        dg_ref[...] += jnp.sum(dout * nz, axis=0, keepdims=True)
        dn = dout * gg
        dx_ref[...] = r * (dn - nz * jnp.mean(dn * nz, axis=-1, keepdims=True))

    row = pl.BlockSpec((tm, d), lambda i: (i, 0))
    vec = pl.BlockSpec((1, d), lambda i: (0, 0))
    return pl.pallas_call(
        body, name=name, grid=(t // tm,), in_specs=[row, vec, row], out_specs=[row, vec, vec],
        out_shape=[jax.ShapeDtypeStruct((t, d), F32), jax.ShapeDtypeStruct((1, d), F32),
                   jax.ShapeDtypeStruct((1, d), F32)],
        compiler_params=_params(("arbitrary",)),
    )(x, g, target)


def _adaln_fwd(craw, w_mod, b_cols, name):
    lyr, d, nc = w_mod.shape

    def body(c_ref, w_ref, b_ref, out_ref):
        out_ref[...] = _bdot(_silu(c_ref[...]), w_ref[...]) + b_ref[...]

    return pl.pallas_call(
        body, name=name, grid=(lyr,),
        in_specs=[_full(craw.shape), pl.BlockSpec((None, d, nc), lambda l: (l, 0, 0)),
                  pl.BlockSpec((None, 1, nc), lambda l: (l, 0, 0))],
        out_specs=pl.BlockSpec((None, 16, nc), lambda l: (l, 0, 0)),
        out_shape=jax.ShapeDtypeStruct((lyr, 16, nc), F32),
        compiler_params=_params(("parallel",)),
    )(craw, w_mod, b_cols)


def _adaln_bwd(craw, cs_t, dmm_cols, w_mod, name):
    lyr, d, nc = w_mod.shape

    def body(c_ref, cst_ref, dmm_ref, w_ref, gw_ref, dc_ref):
        dmm = dmm_ref[...]
        gw_ref[...] = _bdot(cst_ref[...], dmm)
        cc = c_ref[...]
        sg = jax.nn.sigmoid(cc)
        dc_ref[...] = _bdot(dmm, w_ref[...], NT) * (sg * (1.0 + cc * (1.0 - sg)))

    wspec = pl.BlockSpec((None, d, nc), lambda l: (l, 0, 0))
    return pl.pallas_call(
        body, name=name, grid=(lyr,),
        in_specs=[_full(craw.shape), _full(cs_t.shape), pl.BlockSpec((None, 16, nc), lambda l: (l, 0, 0)), wspec],
        out_specs=[wspec, pl.BlockSpec((None, 16, d), lambda l: (l, 0, 0))],
        out_shape=[jax.ShapeDtypeStruct((lyr, d, nc), F32), jax.ShapeDtypeStruct((lyr, 16, d), F32)],
        compiler_params=_params(("parallel",)),
    )(craw, cs_t, dmm_cols, w_mod)


def _rope_tables(t):
    rows = np.repeat(np.arange(t // GRID_W, dtype=np.float32), GRID_W)
    cols = np.tile(np.arange(GRID_W, dtype=np.float32), t // GRID_W)
    n = A_HEAD_DIM // 4
    freqs = (ROPE_BASE ** (-np.arange(n, dtype=np.float32) / n)).astype(np.float32)
    ang_r, ang_c = (rows[:, None] * freqs).astype(np.float32), (cols[:, None] * freqs).astype(np.float32)
    cr, sr, cc, sc = np.cos(ang_r), np.sin(ang_r), np.cos(ang_c), np.sin(ang_c)
    cos = np.concatenate([cr, cr, cc, cc] * 2, axis=-1).astype(np.float32)
    sin = np.concatenate([-sr, sr, -sc, sc] * 2, axis=-1).astype(np.float32)
    return jnp.asarray(cos), jnp.asarray(sin)


def _rope(xt, cos, sin, adjoint, name):
    t, w = xt.shape
    tb = _pick(t, (512, 256, 128))
    rep = w // cos.shape[1]

    def body(x_ref, c_ref, s_ref, o_ref):
        xx = x_ref[...]
        cc = jnp.concatenate([c_ref[...]] * rep, axis=1) if rep > 1 else c_ref[...]
        ss = jnp.concatenate([s_ref[...]] * rep, axis=1) if rep > 1 else s_ref[...]
        low = (lax.broadcasted_iota(jnp.int32, xx.shape, 1) % 32) < 16

        def partner(v):
            return jnp.where(low, pltpu.roll(v, w - 16, 1), pltpu.roll(v, 16, 1))

        if adjoint:
            o_ref[...] = xx * cc + partner(xx * ss)
        else:
            o_ref[...] = xx * cc + partner(xx) * ss

    blk = pl.BlockSpec((tb, w), lambda i: (i, 0))
    tab = pl.BlockSpec((tb, cos.shape[1]), lambda i: (i, 0))
    return pl.pallas_call(
        body, name=name, grid=(t // tb,), in_specs=[blk, tab, tab], out_specs=blk,
        out_shape=jax.ShapeDtypeStruct((t, w), F32), compiler_params=_params(("parallel",)),
    )(xt, cos, sin)


def _attn_probs(q, kb, kc, sink, n, t):
    scale = A_HEAD_DIM ** -0.5
    s1 = _bdot(q, kb, NT) * scale
    s2 = _bdot(q, kc, NT) * scale
    qpos = n * WINDOW + lax.broadcasted_iota(jnp.int32, s1.shape, 0) % WINDOW
    kpos = (n - 1) * WINDOW + lax.broadcasted_iota(jnp.int32, s1.shape, 1)
    valid = (kpos >= 0) & (kpos < t) & (jnp.abs(kpos - qpos) <= WINDOW)
    s1 = jnp.where(valid, s1, -jnp.inf)
    mx = jnp.maximum(jnp.maximum(jnp.max(s1, axis=-1, keepdims=True), jnp.max(s2, axis=-1, keepdims=True)), sink)
    p1, p2, ps = jnp.exp(s1 - mx), jnp.exp(s2 - mx), jnp.exp(sink - mx)
    inv = 1.0 / (jnp.sum(p1, axis=-1, keepdims=True) + jnp.sum(p2, axis=-1, keepdims=True) + ps)
    return p1 * inv, p2 * inv, ps * inv


def _sink_rows(sink_ref):
    return jnp.concatenate([jnp.broadcast_to(sink_ref[r], (WINDOW, 1)) for r in range(A_REP)], axis=0)


def _attn_fwd(q, kp, vp, kc, vc, sink, name):
    hq, t, dh = q.shape
    nb = t // WINDOW
    lc = kc.shape[1]
    rows = A_REP * WINDOW

    def body(q_ref, k_ref, v_ref, kc_ref, vc_ref, sink_ref, o_ref):
        n = pl.program_id(1)
        start = pl.multiple_of(n * WINDOW, WINDOW)
        kb, vb = k_ref[pl.ds(start, 3 * WINDOW), :], v_ref[pl.ds(start, 3 * WINDOW), :]
        p1, p2, _ = _attn_probs(q_ref[...].reshape(rows, dh), kb, kc_ref[...], _sink_rows(sink_ref), n, t)
        o_ref[...] = (_bdot(p1, vb) + _bdot(p2, vc_ref[...])).reshape(A_REP, WINDOW, dh)

    qblk = pl.BlockSpec((A_REP, WINDOW, dh), lambda g, n: (g, n, 0))
    kfull = pl.BlockSpec((None, t + 2 * WINDOW, dh), lambda g, n: (g, 0, 0))
    cfull = pl.BlockSpec((None, lc, dh), lambda g, n: (g, 0, 0))
    return pl.pallas_call(
        body, name=name, grid=(hq // A_REP, nb),
        in_specs=[qblk, kfull, kfull, cfull, cfull, pl.BlockSpec((A_REP, 1, 1), lambda g, n: (g, 0, 0))],
        out_specs=qblk, out_shape=jax.ShapeDtypeStruct((hq, t, dh), F32),
        compiler_params=_params(("parallel", "parallel")),
    )(q, kp, vp, kc, vc, sink)


def _attn_bwd(q, kp, vp, kc, vc, sink, o, do, name):
    hq, t, dh = q.shape
    nb = t // WINDOW
    lc = kc.shape[1]
    scale = A_HEAD_DIM ** -0.5
    rows = A_REP * WINDOW

    def body(q_ref, k_ref, v_ref, kc_ref, vc_ref, sink_ref, o_ref, do_ref,
             dq_ref, dk_ref, dv_ref, dkc_ref, dvc_ref, dsink_ref):
        n = pl.program_id(1)

        @pl.when(n == 0)
        def _():
            dk_ref[...] = jnp.zeros_like(dk_ref)
            dv_ref[...] = jnp.zeros_like(dv_ref)
            dkc_ref[...] = jnp.zeros_like(dkc_ref)
            dvc_ref[...] = jnp.zeros_like(dvc_ref)
            dsink_ref[...] = jnp.zeros_like(dsink_ref)

        start = pl.multiple_of(n * WINDOW, WINDOW)
        band = pl.ds(start, 3 * WINDOW)
        qq, kb, vb, kcc, vcc = q_ref[...].reshape(rows, dh), k_ref[band, :], v_ref[band, :], kc_ref[...], vc_ref[...]
        p1, p2, ps = _attn_probs(qq, kb, kcc, _sink_rows(sink_ref), n, t)
        dout = do_ref[...].reshape(rows, dh)
        delta = jnp.sum(dout * o_ref[...].reshape(rows, dh), axis=-1, keepdims=True)
        ds1 = p1 * (_bdot(dout, vb, NT) - delta)
        ds2 = p2 * (_bdot(dout, vcc, NT) - delta)
        dq_ref[...] = ((_bdot(ds1, kb) + _bdot(ds2, kcc)) * scale).reshape(A_REP, WINDOW, dh)
        dk_ref[band, :] += _bdot(ds1.T, qq) * scale
        dv_ref[band, :] += _bdot(p1.T, dout)
        dkc_ref[...] += _bdot(ds2.T, qq) * scale
        dvc_ref[...] += _bdot(p2.T, dout)
        dsink_ref[...] += jnp.sum((-ps * delta).reshape(A_REP, WINDOW, 1), axis=1, keepdims=True)

    qblk = pl.BlockSpec((A_REP, WINDOW, dh), lambda g, n: (g, n, 0))
    kfull = pl.BlockSpec((None, t + 2 * WINDOW, dh), lambda g, n: (g, 0, 0))
    cfull = pl.BlockSpec((None, lc, dh), lambda g, n: (g, 0, 0))
    return pl.pallas_call(
        body, name=name, grid=(hq // A_REP, nb),
        in_specs=[qblk, kfull, kfull, cfull, cfull, pl.BlockSpec((A_REP, 1, 1), lambda g, n: (g, 0, 0)), qblk, qblk],
        out_specs=[qblk, kfull, kfull, cfull, cfull, pl.BlockSpec((A_REP, 8, 128), lambda g, n: (g, 0, 0))],
        out_shape=[jax.ShapeDtypeStruct(q.shape, F32), jax.ShapeDtypeStruct(kp.shape, F32),
                   jax.ShapeDtypeStruct(kp.shape, F32), jax.ShapeDtypeStruct(kc.shape, F32),
                   jax.ShapeDtypeStruct(kc.shape, F32), jax.ShapeDtypeStruct((hq, 8, 128), F32)],
        compiler_params=_params(("parallel", "arbitrary")),
    )(q, kp, vp, kc, vc, sink, o, do)


def _gate_fwd(zg, w2, b2, name):
    m = zg.shape[0]
    n = w2.shape[1]
    tm = _pick(m, (512, 256, 128, 64, 32, 16, 8))

    def body(z_ref, w_ref, b_ref, o_ref):
        o_ref[...] = jax.nn.log_sigmoid(_bdot(z_ref[...], w_ref[...]) + b_ref[...]) / B_GATE_NORM

    return pl.pallas_call(
        body, name=name, grid=(m // tm,),
        in_specs=[pl.BlockSpec((tm, zg.shape[1]), lambda i: (i, 0)), _full(w2.shape), _full(b2.shape)],
        out_specs=pl.BlockSpec((tm, n), lambda i: (i, 0)), out_shape=jax.ShapeDtypeStruct((m, n), F32),
        compiler_params=_params(("parallel",)),
    )(zg, w2, b2)


def _gate_bwd(zg, w2, b2, dla, name):
    m, rk = zg.shape
    n = w2.shape[1]
    tm = _pick(m, (512, 256, 128, 64, 32, 16, 8))

    def body(z_ref, w_ref, b_ref, d_ref, dz_ref, dw_ref, db_ref):
        @pl.when(pl.program_id(0) == 0)
        def _():
            dw_ref[...] = jnp.zeros_like(dw_ref)
            db_ref[...] = jnp.zeros_like(db_ref)

        zz, ww = z_ref[...], w_ref[...]
        pre = _bdot(zz, ww) + b_ref[...]
        dpre = d_ref[...] * (1.0 / B_GATE_NORM) * jax.nn.sigmoid(-pre)
        dz_ref[...] = _bdot(dpre, ww, NT)
        dw_ref[...] += _bdot(zz.T, dpre)
        db_ref[...] += jnp.sum(dpre, axis=0, keepdims=True)

    return pl.pallas_call(
        body, name=name, grid=(m // tm,),
        in_specs=[pl.BlockSpec((tm, rk), lambda i: (i, 0)), _full(w2.shape), _full(b2.shape),
                  pl.BlockSpec((tm, n), lambda i: (i, 0))],
        out_specs=[pl.BlockSpec((tm, rk), lambda i: (i, 0)), _full(w2.shape), _full(b2.shape)],
        out_shape=[jax.ShapeDtypeStruct((m, rk), F32), jax.ShapeDtypeStruct(w2.shape, F32),
                   jax.ShapeDtypeStruct(b2.shape, F32)],
        compiler_params=_params(("arbitrary",)),
    )(zg, w2, b2, dla)


def _chunk_order(step, n_x_chunks, n_chunks, reverse):
    n_c = n_chunks - n_x_chunks
    if reverse:
        return jnp.where(step < n_c, n_chunks - 1 - step, n_chunks - 1 - step)
    return jnp.where(step < n_c, n_x_chunks + step, step - n_c)


def _tri(reverse, transpose=False):
    i = lax.broadcasted_iota(jnp.int32, (B_CHUNK, B_CHUNK), 0)
    j = lax.broadcasted_iota(jnp.int32, (B_CHUNK, B_CHUNK), 1)
    if transpose:
        i, j = j, i
    return (j >= i) if reverse else (j <= i)


def _gla_chunk(q, k, la, reverse):
    g = _dot(_tri(reverse).astype(F32), la, NN, HI)
    last = 0 if reverse else B_CHUNK - 1
    gl = g[last:last + 1, :]
    eg, eng, egl = jnp.exp(g), jnp.exp(-g), jnp.exp(gl - g)
    decay_col = jnp.exp(jnp.sum(la.T, axis=1, keepdims=True))
    return q * (B_DK ** -0.5) * eg, k * eng, k * egl, eg, eng, egl, decay_col


def _head_of(shape, axis, width):
    return lax.broadcasted_iota(jnp.int32, shape, axis) // width


def _gla_fwd(q, k, v, la_f, la_b, n_x, name):
    tc, wk = q.shape
    wv = v.shape[1]
    hh = B_HEADS
    dk, dv = wk // hh, wv // hh
    nc, nxc = tc // B_CHUNK, n_x // B_CHUNK
    orders = [functools.partial(_chunk_order, n_x_chunks=nxc, n_chunks=nc, reverse=rev) for rev in (False, True)]

    def body(*refs):
        ins, outs, s_refs = refs[:8], refs[8:12], refs[12:]

        @pl.when(pl.program_id(0) == 0)
        def _():
            for s_ref in s_refs:
                s_ref[...] = jnp.zeros_like(s_ref)

        lane_head = _head_of((B_CHUNK, wk), 1, dk)
        row_head = _head_of((wk, dv), 0, dk)
        for di, reverse in enumerate((False, True)):
            q_ref, k_ref, v_ref, la_ref = ins[4 * di:4 * di + 4]
            o_ref, s_save_ref = outs[2 * di:2 * di + 2]
            s_ref = s_refs[di]
            qt, kt, ke, _, _, _, decay_col = _gla_chunk(q_ref[...], k_ref[...], la_ref[...], reverse)
            ke_t = ke.T
            s_prev = s_ref[...]
            update = jnp.zeros_like(s_prev)
            for h in range(hh):
                vv = v_ref[:, h * dv:(h + 1) * dv]
                qm = jnp.where(lane_head == h, qt, 0.0)
                att = jnp.where(_tri(reverse), _bdot(qm, kt, NT), 0.0)
                o_ref[:, h * dv:(h + 1) * dv] = _bdot(att, vv) + _bdot(qm, s_prev)
                update = jnp.where(row_head == h, _bdot(ke_t, vv), update)
            s_save_ref[...] = s_prev
            s_ref[...] = decay_col * s_prev + update

    def blk(w, order):
        return pl.BlockSpec((B_CHUNK, w), lambda s: (order(s), 0))

    def sblk(order):
        return pl.BlockSpec((None, wk, dv), lambda s: (order(s), 0, 0))

    in_specs, out_specs = [], []
    for order in orders:
        in_specs += [blk(wk, order), blk(wk, order), blk(wv, order), blk(wk, order)]
        out_specs += [blk(wv, order), sblk(order)]
    o_shape, s_shape = jax.ShapeDtypeStruct((tc, wv), F32), jax.ShapeDtypeStruct((nc, wk, dv), F32)
    return pl.pallas_call(
        body, name=name, grid=(nc,), in_specs=in_specs, out_specs=out_specs,
        out_shape=[o_shape, s_shape, o_shape, s_shape],
        scratch_shapes=[pltpu.VMEM((wk, dv), F32)] * 2,
        compiler_params=_params(("arbitrary",)),
    )(q, k, v, la_f, q, k, v, la_b)


def _gla_bwd(q, k, v, la_f, la_b, s_f, s_b, do, n_x, name):
    tc, wk = q.shape
    wv = v.shape[1]
    hh = B_HEADS
    dk, dv = wk // hh, wv // hh
    nc, nxc = tc // B_CHUNK, n_x // B_CHUNK
    orders = [functools.partial(lambda s, rev: _chunk_order(nc - 1 - s, nxc, nc, rev), rev=rev) for rev in (False, True)]

    def body(*refs):
        ins, outs, ds_refs = refs[:12], refs[12:20], refs[20:]

        @pl.when(pl.program_id(0) == 0)
        def _():
            for ds_ref in ds_refs:
                ds_ref[...] = jnp.zeros_like(ds_ref)

        lane_head = _head_of((B_CHUNK, wk), 1, dk)
        row_head = _head_of((wk, dv), 0, dk)
        for di, reverse in enumerate((False, True)):
            q_ref, k_ref, v_ref, la_ref, s_save_ref, do_ref = ins[6 * di:6 * di + 6]
            dq_ref, dk_ref, dv_ref, dla_ref = outs[4 * di:4 * di + 4]
            ds_ref = ds_refs[di]
            mask = _tri(reverse)
            last = 0 if reverse else B_CHUNK - 1
            is_last = lax.broadcasted_iota(jnp.int32, (B_CHUNK, wk), 0) == last
            la = la_ref[...]
            qt, kt, ke, eg, eng, egl, decay_col = _gla_chunk(q_ref[...], k_ref[...], la, reverse)
            qt_t = qt.T
            s_prev, ds_new = s_save_ref[...], ds_ref[...]
            dqt, dkt, dke = jnp.zeros_like(qt), jnp.zeros_like(qt), jnp.zeros_like(qt)
            ds_add = jnp.zeros_like(ds_new)
            for h in range(hh):
                cols = slice(h * dv, (h + 1) * dv)
                vv, dout = v_ref[:, cols], do_ref[:, cols]
                mine = lane_head == h
                qm, km = jnp.where(mine, qt, 0.0), jnp.where(mine, ke, 0.0)
                att = jnp.where(mask, _bdot(qm, kt, NT), 0.0)
                datt = jnp.where(mask, _bdot(dout, vv, NT), 0.0)
                dv_ref[:, cols] = _bdot(att.T, dout) + _bdot(km, ds_new)
                dqt = jnp.where(mine, _bdot(datt, kt) + _bdot(dout, s_prev, NT), dqt)
                dkt = jnp.where(mine, _bdot(datt.T, qt), dkt)
                dke = jnp.where(mine, _bdot(vv, ds_new, NT), dke)
                ds_add = jnp.where(row_head == h, _bdot(qt_t, dout), ds_add)
            ddecay_row = jnp.sum((ds_new * s_prev).T, axis=0, keepdims=True)
            decay_row = jnp.exp(jnp.sum(la, axis=0, keepdims=True))
            ds_ref[...] = decay_col * ds_new + ds_add
            dq_ref[...] = dqt * (B_DK ** -0.5) * eg
            dk_ref[...] = dkt * eng + dke * egl
            dgl = jnp.sum(dke * ke, axis=0, keepdims=True) + ddecay_row * decay_row
            dg = dqt * qt - dkt * kt - dke * ke + jnp.where(is_last, dgl, 0.0)
            dla_ref[...] = _dot(_tri(reverse, transpose=True).astype(F32), dg, NN, HI)

    def blk(w, order):
        return pl.BlockSpec((B_CHUNK, w), lambda s: (order(s), 0))

    in_specs, out_specs = [], []
    for order in orders:
        in_specs += [blk(wk, order), blk(wk, order), blk(wv, order), blk(wk, order),
                     pl.BlockSpec((None, wk, dv), lambda s, order=order: (order(s), 0, 0)), blk(wv, order)]
        out_specs += [blk(wk, order), blk(wk, order), blk(wv, order), blk(wk, order)]
    k_shape, v_shape = jax.ShapeDtypeStruct((tc, wk), F32), jax.ShapeDtypeStruct((tc, wv), F32)
    return pl.pallas_call(
        body, name=name, grid=(nc,), in_specs=in_specs, out_specs=out_specs,
        out_shape=[k_shape, k_shape, v_shape, k_shape] * 2,
        scratch_shapes=[pltpu.VMEM((wk, dv), F32)] * 2,
        compiler_params=_params(("arbitrary",)),
    )(q, k, v, la_f, s_f, do, q, k, v, la_b, s_b, do)


def _gla_out_fwd(o_f, o_b, r, g, name):
    t = r.shape[0]
    dv = g.shape[1]
    hh = r.shape[1] // dv
    tb = _pick(t, (256, 128, 64))

    def body(of_ref, ob_ref, r_ref, g_ref, out_ref):
        for h in range(hh):
            cols = slice(h * dv, (h + 1) * dv)
            o = of_ref[:, cols] + ob_ref[:, cols]
            rs = lax.rsqrt(jnp.mean(o * o, axis=-1, keepdims=True) + RMS_EPS)
            out_ref[:, cols] = (o * rs) * g_ref[...] * _silu(r_ref[:, cols])

    rblk = pl.BlockSpec((tb, hh * dv), lambda i: (i, 0))
    return pl.pallas_call(
        body, name=name, grid=(t // tb,), in_specs=[rblk, rblk, rblk, _full(g.shape)], out_specs=rblk,
        out_shape=jax.ShapeDtypeStruct((t, hh * dv), F32), compiler_params=_params(("parallel",)),
    )(o_f, o_b, r, g)


def _gla_out_bwd(o_f, o_b, r, g, dout, name):
    tc = o_f.shape[0]
    t = r.shape[0]
    dv = g.shape[1]
    hh = r.shape[1] // dv
    tb = _pick(int(np.gcd(t, tc)), (256, 128, 64))
    nt = t // tb

    def body(of_ref, ob_ref, r_ref, g_ref, d_ref, do_ref, dr_ref, dg_ref):
        i = pl.program_id(0)

        @pl.when(i == 0)
        def _():
            dg_ref[...] = jnp.zeros_like(dg_ref)

        @pl.when(i >= nt)
        def _():
            do_ref[...] = jnp.zeros_like(do_ref)

        @pl.when(i < nt)
        def _():
            gg = g_ref[...]
            for h in range(hh):
                cols = slice(h * dv, (h + 1) * dv)
                o = of_ref[:, cols] + ob_ref[:, cols]
                rs = lax.rsqrt(jnp.mean(o * o, axis=-1, keepdims=True) + RMS_EPS)
                nz = o * rs
                rr, dd = r_ref[:, cols], d_ref[:, cols]
                sg = jax.nn.sigmoid(rr)
                dr_ref[:, cols] = dd * nz * gg * (sg * (1.0 + rr * (1.0 - sg)))
                dy = dd * (rr * sg)
                dg_ref[...] += jnp.sum(dy * nz, axis=0, keepdims=True)
                dn = dy * gg
                do_ref[:, cols] = rs * (dn - nz * jnp.mean(dn * nz, axis=-1, keepdims=True))

    oblk = pl.BlockSpec((tb, hh * dv), lambda i: (i, 0))
    rblk = pl.BlockSpec((tb, hh * dv), lambda i: (jnp.minimum(i, nt - 1), 0))
    return pl.pallas_call(
        body, name=name, grid=(tc // tb,), in_specs=[oblk, oblk, rblk, _full(g.shape), rblk],
        out_specs=[oblk, rblk, _full(g.shape)],
        out_shape=[jax.ShapeDtypeStruct(o_f.shape, F32), jax.ShapeDtypeStruct(r.shape, F32),
                   jax.ShapeDtypeStruct(g.shape, F32)],
        compiler_params=_params(("arbitrary",)),
    )(o_f, o_b, r, g, dout)


def _pool_band(half, tb, adjoint):
    r = lax.broadcasted_iota(jnp.int32, (tb, tb + 2 * POOL_PAD), 0) + POOL_PAD
    j = lax.broadcasted_iota(jnp.int32, (tb, tb + 2 * POOL_PAD), 1)
    if adjoint:
        return ((j > r - half) & (j <= r + half)).astype(F32)
    return ((j >= r - half) & (j < r + half)).astype(F32)


def _pool_count(pos, half, t):
    return (jnp.minimum(pos + half, t) - jnp.maximum(pos - half, 0)).astype(F32)


def _pool_fwd(hp, w_pool, pool_scale, res, mods, km, name):
    t, d = res.shape
    ng, gw = w_pool.shape[0], w_pool.shape[1]
    tb = _pick(t, (256, 128, 64))

    def body(hp_ref, w_ref, ps_ref, res_ref, mods_ref, out_ref, pooled_ref, ypre_ref):
        gi, i = pl.program_id(0), pl.program_id(1)
        half = jnp.left_shift(1, gi)
        win = hp_ref[pl.ds(pl.multiple_of(i * tb, tb), tb + 2 * POOL_PAD), :]
        total = _dot(_pool_band(half, tb, False), win, NN, HI)
        pos = i * tb + lax.broadcasted_iota(jnp.int32, (tb, 1), 0)
        pooled = total / _pool_count(pos, half, t) - win[POOL_PAD:POOL_PAD + tb, :]
        ypre = _bdot(pooled, w_ref[...])
        pooled_ref[...] = pooled.astype(BF16)
        ypre_ref[...] = ypre
        out_ref[...] = res_ref[...] + mods_ref[0, km:km + 1, :] * (ypre * ps_ref[...])

    tile = pl.BlockSpec((tb, gw), lambda gi, i: (i, gi))
    return pl.pallas_call(
        body, name=name, grid=(ng, t // tb),
        in_specs=[pl.BlockSpec((t + 2 * POOL_PAD, gw), lambda gi, i: (0, gi)),
                  pl.BlockSpec((None, gw, gw), lambda gi, i: (gi, 0, 0)),
                  pl.BlockSpec((1, gw), lambda gi, i: (0, gi)), tile,
                  pl.BlockSpec((2, 16, gw), lambda gi, i: (0, 0, gi))],
        out_specs=[tile, tile, tile],
        out_shape=[jax.ShapeDtypeStruct((t, d), F32), jax.ShapeDtypeStruct((t, d), BF16),
                   jax.ShapeDtypeStruct((t, d), F32)],
        compiler_params=_params(("parallel", "parallel")),
    )(hp, w_pool, pool_scale, res, mods)


def _pool_bwd(dxp, w_pool, pool_scale, pooled, ypre, mods, km, name):
    t, d = pooled.shape
    ng, gw = w_pool.shape[0], w_pool.shape[1]
    tb = _pick(t, (256, 128, 64))

    def body(dxp_ref, w_ref, ps_ref, pooled_ref, ypre_ref, mods_ref, dh_ref, dw_ref, acc_ref):
        gi, i = pl.program_id(0), pl.program_id(1)

        @pl.when(i == 0)
        def _():
            dw_ref[...] = jnp.zeros_like(dw_ref)
            acc_ref[...] = jnp.zeros_like(acc_ref)

        half = jnp.left_shift(1, gi)
        mod, ps = mods_ref[0, km:km + 1, :], ps_ref[...]
        dwin = dxp_ref[pl.ds(pl.multiple_of(i * tb, tb), tb + 2 * POOL_PAD), :]
        dpooled = _bdot(dwin * (mod * ps), w_ref[...], NT)
        pos = i * tb - POOL_PAD + lax.broadcasted_iota(jnp.int32, (tb + 2 * POOL_PAD, 1), 0)
        spread = _dot(_pool_band(half, tb, True), dpooled / jnp.maximum(_pool_count(pos, half, t), 1.0), NN, HI)
        dh_ref[...] = spread - dpooled[POOL_PAD:POOL_PAD + tb, :]
        dxc, yp = dwin[POOL_PAD:POOL_PAD + tb, :], ypre_ref[...]
        dw_ref[...] += _bdot(pooled_ref[...].astype(F32).T, dxc * (mod * ps))
        acc_ref[0:1, :] += jnp.sum(dxc * yp * mod, axis=0, keepdims=True)
        acc_ref[1:2, :] += jnp.sum(dxc * yp * ps, axis=0, keepdims=True)

    tile = pl.BlockSpec((tb, gw), lambda gi, i: (i, gi))
    wblk = pl.BlockSpec((None, gw, gw), lambda gi, i: (gi, 0, 0))
    return pl.pallas_call(
        body, name=name, grid=(ng, t // tb),
        in_specs=[pl.BlockSpec((t + 2 * POOL_PAD, gw), lambda gi, i: (0, gi)), wblk,
                  pl.BlockSpec((1, gw), lambda gi, i: (0, gi)), tile, tile,
                  pl.BlockSpec((2, 16, gw), lambda gi, i: (0, 0, gi))],
        out_specs=[tile, wblk, pl.BlockSpec((8, gw), lambda gi, i: (0, gi))],
        out_shape=[jax.ShapeDtypeStruct((t, d), F32), jax.ShapeDtypeStruct(w_pool.shape, F32),
                   jax.ShapeDtypeStruct((8, d), F32)],
        compiler_params=_params(("arbitrary", "arbitrary")),
    )(dxp, w_pool, pool_scale, pooled, ypre, mods)


def _adamw(w, g, m, v, name):
    r, c = w.shape
    tr = _pick(r, (512, 352, 256, 128, 64, 32, 16, 8))
    c1 = 1.0 / (1.0 - ADAM_B1 ** ADAM_STEP)
    c2 = 1.0 / (1.0 - ADAM_B2 ** ADAM_STEP)

    def body(w_ref, g_ref, m_ref, v_ref, d_ref, nm_ref, nv_ref):
        gg = g_ref[...]
        nm = ADAM_B1 * m_ref[...] + (1.0 - ADAM_B1) * gg
        nv = ADAM_B2 * v_ref[...] + (1.0 - ADAM_B2) * (gg * gg)
        nm_ref[...] = nm
        nv_ref[...] = nv
        d_ref[...] = -ADAM_LR * ((nm * c1) / (jnp.sqrt(nv * c2) + ADAM_EPS) + ADAM_WD * w_ref[...])

    blk = pl.BlockSpec((tr, c), lambda i: (i, 0))
    shp = jax.ShapeDtypeStruct((r, c), F32)
    return pl.pallas_call(
        body, name=name, grid=(r // tr,), in_specs=[blk] * 4, out_specs=[blk] * 3, out_shape=[shp] * 3,
        compiler_params=_params(("parallel",)),
    )(w, g, m, v)


def _heads(z, n_heads):
    m = z.shape[0]
    return z.reshape(m, n_heads, -1).transpose(1, 0, 2)


def _unheads(zh):
    return zh.transpose(1, 0, 2).reshape(zh.shape[1], -1)


def _pad_rows(a, n):
    return jnp.pad(a, ((0, 0), (n, n), (0, 0))) if a.ndim == 3 else jnp.pad(a, ((n, n), (0, 0)))


def _local_step(x, ctx, target, mods, wts, fetch, emit):
    t, d = x.shape
    l_ctx = ctx.shape[0]
    tc = t + l_ctx
    norm_g = wts["norm_g"]
    ng = lambda l, k: norm_g[l, k][None, :]
    grads = {}
    dmods = [[[None] * N_MOD for _ in range(2)] for _ in range(2)]
    dnorm = [[None] * 3 for _ in range(2)]

    def ffn_fwd(z, l, kbase, wi, wo, g, n_x, tag):
        h = _modulate(z, mods[l], g, kbase, kbase + 1, n_x, BF16, f"mod_{tag}")
        au, act = _ffn_up(h, wi, 0, f"ffn_up_{tag}")
        wo = wo(act) if callable(wo) else wo
        z_new, y = _mm_resid(act, wo, 0, z, mods[l], kbase + 2, 0.5, n_x, f"ffn_down_{tag}")
        return z_new, (z, h, au, act, y, wi, wo)

    def ffn_bwd(dz_new, saved, l, kbase, g, n_x, tag, stage):
        z, h, au, act, y, wi, wo = saved
        dy, acc_gate = _resid_bwd(dz_new, y, mods[l], kbase + 2, 0.5, n_x, f"resid_bwd_{tag}")
        dau = _ffn_down_bwd(dy, wo, 0, au, f"ffn_down_bwd_{tag}")
        dwo = _mm_tn(act, dy, BF16, f"dwo_{tag}")
        dwi_t = _mm_tn(dau, h, BF16, f"dwi_{tag}")
        token = emit(stage, [dwi_t, dwo])
        dh = _mm([(dau, wi, 0, 0)], NN, d, F32, f"dh_{tag}", tm_pref=ROW_TILES, dep=token)
        dz, acc_mod = _modulate_bwd(z, dh, dz_new, mods[l], g, kbase + 1, n_x, f"mod_bwd_{tag}")
        return dz, acc_mod, acc_gate

    def record(l, kbase, k_norm, g, acc_mod, acc_gate, streams):
        total = None
        for s in range(streams):
            dmods[l][s][kbase] = acc_mod[s, 0]
            dmods[l][s][kbase + 1] = acc_mod[s, 1] * g[0]
            if acc_gate is not None:
                dmods[l][s][kbase + 2] = acc_gate[s, 0]
            part = acc_mod[s, 1] * (1.0 + mods[l][s, kbase + 1])
            total = part if total is None else total + part
        dnorm[l][k_norm] = total

    xc0 = jnp.concatenate([x, ctx], axis=0)
    wi1_0 = fetch(0, None)["wi1_0"]
    xc1, sv_f1 = ffn_fwd(xc0, 0, 0, wi1_0, lambda act: fetch(1, act)["wo1_0"], ng(0, 0), t, "l0f1")
    hc = _modulate(xc1, mods[0], ng(0, 1), 3, 4, t, BF16, "mod_l0mix")
    w_in_t = fetch(2, hc)["w_in_t"]
    n_proj = w_in_t.shape[1]
    zall = _mm([(hc, w_in_t, 0, 0)], NT, n_proj, F32, "proj", tm_pref=ROW_TILES,
               tn_pref=(n_proj,))
    offs = np.cumsum((0,) + PROJ_SIZES)
    part = lambda i, rows=slice(None): zall[rows, offs[i]:offs[i + 1]]
    lat, con = slice(0, t), slice(t, tc)
    cos, sin = _rope_tables(t)
    qa = _heads(_rope(part(0, lat), cos, sin, False, "rope_q"), A_HEADS)
    ka = _heads(_rope(part(1, lat), cos, sin, False, "rope_k"), A_KV_HEADS)
    va = _heads(part(2, lat), A_KV_HEADS)
    kca, vca = _heads(part(1, con), A_KV_HEADS), _heads(part(2, con), A_KV_HEADS)
    kap, vap = _pad_rows(ka, WINDOW), _pad_rows(va, WINDOW)
    sink = wts["sink"].reshape(A_HEADS, 1, 1)
    o_a = _attn_fwd(qa, kap, vap, kca, vca, sink, "attn_fwd")

    qb, kb, vb = part(3), part(4), part(5)
    rb = part(6, lat)
    zg = part(7)
    zg_f, zg_b = zg[:, :B_GATE_RANK], zg[:, B_GATE_RANK:]
    w2f, w2b, b2f, b2b = wts["w_a2_f"], wts["w_a2_b"], wts["b_a_f"], wts["b_a_b"]
    la_f = _gate_fwd(zg_f, w2f, b2f, "gate_f")
    la_b = _gate_fwd(zg_b, w2b, b2b, "gate_b")
    o_f, s_f, o_b, s_b = _gla_fwd(qb, kb, vb, la_f, la_b, t, "gla_fwd")
    gla_g = wts["gla_g"]
    go = _gla_out_fwd(o_f, o_b, rb, gla_g, "gla_out")
    cat = jnp.concatenate([_unheads(o_a), go], axis=-1).astype(BF16)
    x1 = xc1[:t]
    big = fetch(3, cat)
    w_out, wi2_0, wo2_0 = big["w_out"], big["wi2_0"], big["wo2_0"]
    x2, y_mix0 = _mm_resid(cat, w_out, 0, x1, mods[0], 5, 1.0, t, "w_out")
    x3, sv_f2 = ffn_fwd(x2, 0, 6, wi2_0, wo2_0, ng(0, 2), t, "l0f2")

    big = fetch(4, x3)
    wi1_1, wo1_1, wi2_1, wo2_1 = big["wi1_1"], big["wo1_1"], big["wi2_1"], big["wo2_1"]
    x4, sv_g1 = ffn_fwd(x3, 1, 0, wi1_1, wo1_1, ng(1, 0), t, "l1f1")
    hp = _modulate(x4, mods[1], ng(1, 1), 3, 4, t, F32, "mod_l1mix")
    w_pool, pool_scale = wts["w_pool"], wts["pool_scale"]
    x5, pooled, ypre = _pool_fwd(_pad_rows(hp, POOL_PAD), w_pool, pool_scale, x4, mods[1], 5, "pool_fwd")
    x6, sv_g2 = ffn_fwd(x5, 1, 6, wi2_1, wo2_1, ng(1, 2), t, "l1f2")

    dx6, loss_vec, dfinal_g = _final_loss(x6, wts["final_g"], target, "final_loss")
    grads["final_g"] = dfinal_g[0]

    dx5, acc_mod, acc_gate = ffn_bwd(dx6, sv_g2, 1, 6, ng(1, 2), t, "l1f2", 0)
    record(1, 6, 2, ng(1, 2), acc_mod, acc_gate, 1)
    dhp, dw_pool, acc_pool = _pool_bwd(_pad_rows(dx5, POOL_PAD), w_pool, pool_scale, pooled, ypre, mods[1], 5,
                                       "pool_bwd")
    grads["pool_scale"] = acc_pool[0]
    dmods[1][0][5] = acc_pool[1]
    dx4, acc_mod = _modulate_bwd(x4, dhp, dx5, mods[1], ng(1, 1), 4, t, "mod_bwd_l1mix")
    record(1, 3, 1, ng(1, 1), acc_mod, None, 1)
    dx3, acc_mod, acc_gate = ffn_bwd(dx4, sv_g1, 1, 0, ng(1, 0), t, "l1f1", 1)
    record(1, 0, 0, ng(1, 0), acc_mod, acc_gate, 1)

    dx2, acc_mod, acc_gate = ffn_bwd(dx3, sv_f2, 0, 6, ng(0, 2), t, "l0f2", 2)
    record(0, 6, 2, ng(0, 2), acc_mod, acc_gate, 1)
    dymix, acc_gate = _resid_bwd(dx2, y_mix0, mods[0], 5, 1.0, t, "resid_bwd_mix")
    dmods[0][0][5] = acc_gate[0, 0]
    dw_out = _mm_tn(cat, dymix, BF16, "dw_out")
    dcat = _mm([(dymix, w_out, 0, 0)], NT, cat.shape[1], F32, "dcat")
    do_a = _heads(dcat[:, :A_Q], A_HEADS)
    do_full, drb, dgla_g = _gla_out_bwd(o_f, o_b, rb, gla_g, dcat[:, A_Q:], "gla_out_bwd")
    grads["gla_g"] = dgla_g[0]
    dq_f, dk_f, dv_f, dla_f, dq_b, dk_b, dv_b, dla_b = _gla_bwd(qb, kb, vb, la_f, la_b, s_f, s_b, do_full, t, "gla_bwd")
    dzg_f, dw2f, db2f = _gate_bwd(zg_f, w2f, b2f, dla_f, "gate_bwd_f")
    dzg_b, dw2b, db2b = _gate_bwd(zg_b, w2b, b2b, dla_b, "gate_bwd_b")
    grads.update(w_a2_f=dw2f, w_a2_b=dw2b, b_a_f=db2f[0], b_a_b=db2b[0])
    dqa_r, dkap, dvap, dkca, dvca, dsink = _attn_bwd(qa, kap, vap, kca, vca, sink, o_a, do_a, "attn_bwd")
    grads["sink"] = dsink[:, 0, 0]
    dqa = _rope(_unheads(dqa_r), cos, sin, True, "rope_bwd_q")
    dka = _rope(_unheads(dkap[:, WINDOW:WINDOW + t]), cos, sin, True, "rope_bwd_k")
    dva = dvap[:, WINDOW:WINDOW + t]
    zrow = lambda a, n: jnp.pad(a, ((0, n), (0, 0)))
    dz_parts = [
        zrow(dqa, l_ctx),
        jnp.concatenate([dka, _unheads(dkca)], axis=0),
        jnp.concatenate([_unheads(dva), _unheads(dvca)], axis=0),
        dq_f + dq_b, dk_f + dk_b, dv_f + dv_b,
        zrow(drb, l_ctx),
        jnp.concatenate([dzg_f, dzg_b], axis=-1),
        jnp.zeros((tc, n_proj - PROJ_DIM), F32),
    ]
    dzall = jnp.concatenate(dz_parts, axis=-1).astype(BF16)
    dw_in_t = _mm_tn(dzall, hc, BF16, "dw_in")
    token = emit(3, [dw_in_t, dw_out, dw_pool])
    dhc = _mm([(dzall, w_in_t, 0, 0)], NN, d, F32, "dhc", tm_pref=ROW_TILES, dep=token)
    dxc1_res = jnp.concatenate([dx2, jnp.zeros((l_ctx, d), F32)], axis=0)
    dxc1, acc_mod = _modulate_bwd(xc1, dhc, dxc1_res, mods[0], ng(0, 1), 4, t, "mod_bwd_l0mix")
    record(0, 3, 1, ng(0, 1), acc_mod, None, 2)
    dxc0, acc_mod, acc_gate = ffn_bwd(dxc1, sv_f1, 0, 0, ng(0, 0), t, "l0f1", 4)
    record(0, 0, 0, ng(0, 0), acc_mod, acc_gate, 2)

    grads["norm_g"] = jnp.stack([jnp.stack(dnorm[0]), jnp.stack(dnorm[1])])
    zero = jnp.zeros((d,), F32)
    dmods_arr = jnp.stack([jnp.stack([jnp.stack([v if v is not None else zero for v in dmods[l][s]])
                                      for s in range(2)]) for l in range(2)])
    return loss_vec, dxc0[:t], grads, dmods_arr


def _pack(parts):
    flat = jnp.concatenate([p.reshape(-1).astype(F32) for p in parts])
    pad = (-flat.shape[0]) % 128
    return jnp.pad(flat, (0, pad))[None, :]


def _unpack(rows, shapes):
    out, off = [], 0
    for s in shapes:
        n = int(np.prod(s))
        out.append(rows[:, off:off + n].reshape((rows.shape[0],) + tuple(s)))
        off += n
    return out


def _cols_to_full(g):
    g = jnp.moveaxis(g, 0, -2)
    return g.reshape(g.shape[:-2] + (-1,))


def kernel(x, c, ctx, c_ctx, w_mod, b_mod, norm_g, ffn1_wi, ffn1_wo, ffn2_wi, ffn2_wo, w_in, w_a2_f, b_a_f, w_a2_b, b_a_b, sink, gla_g, w_out, w_pool, pool_scale, final_g, loss_target, m_c_ctx, m_w_mod, m_b_mod, m_norm_g, m_ffn1_wi, m_ffn1_wo, m_ffn2_wi, m_ffn2_wo, m_w_in, m_w_a2_f, m_b_a_f, m_w_a2_b, m_b_a_b, m_sink, m_gla_g, m_w_out, m_w_pool, m_pool_scale, m_final_g, v_c_ctx, v_w_mod, v_b_mod, v_norm_g, v_ffn1_wi, v_ffn1_wo, v_ffn2_wi, v_ffn2_wo, v_w_in, v_w_a2_f, v_b_a_f, v_w_a2_b, v_b_a_b, v_sink, v_gla_g, v_w_out, v_w_pool, v_pool_scale, v_final_g):
    t, d = x.shape[1], x.shape[2]
    me = _dev_index()
    nc = w_mod.shape[2]
    ncol_in = w_in.shape[2]
    ncol_pad = -(-ncol_in // 16) * 16

    small_shapes = [(d,), norm_g.shape, pool_scale.shape, w_a2_f.shape, w_a2_b.shape, w_pool.shape]
    g1 = _gather_small(_pack([c, norm_g, pool_scale, w_a2_f, w_a2_b, w_pool]), "gather_params")
    c_all, norm_g_all, pool_scale_all, w2f_all, w2b_all, w_pool_all = _unpack(g1, small_shapes)
    wts = {
        "norm_g": _cols_to_full(norm_g_all),
        "pool_scale": _cols_to_full(pool_scale_all),
        "w_a2_f": _cols_to_full(w2f_all)[0],
        "w_a2_b": _cols_to_full(w2b_all)[0],
        "w_pool": jnp.moveaxis(w_pool_all[:, 0], 0, 1).reshape(w_pool.shape[1], -1, w_pool.shape[3]),
        "b_a_f": b_a_f, "b_a_b": b_a_b, "sink": sink[0], "gla_g": gla_g, "final_g": final_g[None, :],
    }

    craw = jnp.concatenate([c_all, c_ctx[None, :], jnp.zeros((16 - N_DEV - 1, d), F32)], axis=0)
    b_cols = lax.dynamic_slice_in_dim(b_mod, me * nc, nc, axis=1)[:, None, :]
    mm_cols = _adaln_fwd(craw, w_mod, b_cols, "adaln_fwd")
    g2 = _gather_small(mm_cols.reshape(1, -1), "gather_mods").reshape(N_DEV, 2, 16, nc)
    mm_full = jnp.moveaxis(g2, 0, 2).reshape(2, 16, N_MOD, d)
    mods = jnp.stack([lax.dynamic_index_in_dim(mm_full, me, axis=1, keepdims=False), mm_full[:, N_DEV]], axis=1)
    mods = jnp.pad(mods, ((0, 0), (0, 0), (0, 16 - N_MOD), (0, 0)))

    tr = lambda w: jnp.swapaxes(w, 1, 2).astype(BF16)
    wi1_sh, wi2_sh, wo1_sh, wo2_sh = tr(ffn1_wi), tr(ffn2_wi), ffn1_wo.astype(BF16), ffn2_wo.astype(BF16)
    w_in_sh = jnp.pad(tr(w_in), ((0, 0), (0, ncol_pad - ncol_in), (0, 0)))
    groups = [
        {"wi1_0": wi1_sh[0:1]},
        {"wo1_0": wo1_sh[0:1]},
        {"w_in": w_in_sh},
        {"w_out": w_out.astype(BF16), "wi2_0": wi2_sh[0:1], "wo2_0": wo2_sh[0:1]},
        {"wi1_1": wi1_sh[1:2], "wo1_1": wo1_sh[1:2], "wi2_1": wi2_sh[1:2], "wo2_1": wo2_sh[1:2]},
    ]

    def land_of(shard):
        a_, r, c_ = shard.shape
        return lax.dynamic_update_slice(lax.empty((a_, N_DEV * r, c_), shard.dtype), shard, (0, me * r, 0))

    gathers, token = [], None
    for gi, grp in enumerate(groups):
        shards = list(grp.values())
        gathers.append(_exchange_start(shards, [land_of(s) for s in shards], True, 1 + gi, token, f"gather_start_{gi}"))
        token = gathers[-1][4]
    n_proj = -(-(N_DEV * ncol_in) // 128) * 128

    def fetch(gi, after):
        _, lands = _exchange_wait(gathers[gi], True, token if after is None else after, f"gather_wait_{gi}")
        out = dict(zip(groups[gi].keys(), lands))
        if "w_in" in out:
            w_in_t = out.pop("w_in").reshape(1, N_DEV, ncol_pad, d)[:, :, :ncol_in].reshape(1, N_DEV * ncol_in, d)
            out["w_in_t"] = jnp.pad(w_in_t, ((0, 0), (0, n_proj - N_DEV * ncol_in), (0, 0)))
        return out

    scatters = []

    def emit(stage, arrays):
        if stage == 3:
            dw_in_t, dw_out, dw_pool = arrays
            dw_in_full = dw_in_t[:N_DEV * ncol_in].reshape(N_DEV, ncol_in, d)
            dw_in_full = jnp.pad(dw_in_full, ((0, 0), (0, ncol_pad - ncol_in), (0, 0)))
            srcs = [dw_in_full.reshape(1, N_DEV * ncol_pad, d), dw_out[None], dw_pool.astype(BF16)]
        else:
            srcs = [a[None] for a in arrays]
        lands = [lax.empty((N_DEV, s.shape[0], s.shape[1] // N_DEV, s.shape[2]), s.dtype) for s in srcs]
        scatters.append(_exchange_start(srcs, lands, False, 1 + len(groups) + stage, None, f"scatter_start_{stage}"))
        return scatters[-1][4]

    loss_vec, grad_x, grads, dmods = _local_step(x[0], ctx[0], loss_target[0], mods, wts, fetch, emit)
    loss = lax.psum(jnp.sum(loss_vec), ("x", "y", "c"))

    def reduce_stage(stage, after):
        wholes, lands = _exchange_wait(scatters[stage], False, after, f"scatter_wait_{stage}")
        return [_sum_slots(ld, wh, me, f"sum_grad_{stage}_{i}") for i, (ld, wh) in enumerate(zip(lands, wholes))]

    (dwi2_1, dwo2_1), (dwi1_1, dwo1_1), (dwi2_0, dwo2_0), (dw_in_s, dw_out_s, dw_pool_s) = [
        reduce_stage(stage, grad_x) for stage in range(4)]
    back = lambda g: jnp.swapaxes(g, 1, 2)
    g_big = {
        "ffn2_wi": back(jnp.concatenate([dwi2_0, dwi2_1], axis=0)), "ffn2_wo": jnp.concatenate([dwo2_0, dwo2_1], axis=0),
        "w_in": back(dw_in_s[:, :ncol_in]), "w_out": dw_out_s, "w_pool": dw_pool_s[None],
    }

    order = ["c_ctx", "w_mod", "b_mod", "norm_g", "ffn1_wi", "ffn1_wo", "ffn2_wi", "ffn2_wo", "w_in", "w_a2_f", "b_a_f",
             "w_a2_b", "b_a_b", "sink", "gla_g", "w_out", "w_pool", "pool_scale", "final_g"]
    ws = dict(c_ctx=c_ctx, w_mod=w_mod, b_mod=b_mod, norm_g=norm_g, ffn1_wi=ffn1_wi, ffn1_wo=ffn1_wo, ffn2_wi=ffn2_wi,
              ffn2_wo=ffn2_wo, w_in=w_in, w_a2_f=w_a2_f, b_a_f=b_a_f, w_a2_b=w_a2_b, b_a_b=b_a_b, sink=sink, gla_g=gla_g,
              w_out=w_out, w_pool=w_pool, pool_scale=pool_scale, final_g=final_g)
    ms = dict(c_ctx=m_c_ctx, w_mod=m_w_mod, b_mod=m_b_mod, norm_g=m_norm_g, ffn1_wi=m_ffn1_wi, ffn1_wo=m_ffn1_wo,
              ffn2_wi=m_ffn2_wi, ffn2_wo=m_ffn2_wo, w_in=m_w_in, w_a2_f=m_w_a2_f, b_a_f=m_b_a_f, w_a2_b=m_w_a2_b,
              b_a_b=m_b_a_b, sink=m_sink, gla_g=m_gla_g, w_out=m_w_out, w_pool=m_w_pool, pool_scale=m_pool_scale,
              final_g=m_final_g)
    vs = dict(c_ctx=v_c_ctx, w_mod=v_w_mod, b_mod=v_b_mod, norm_g=v_norm_g, ffn1_wi=v_ffn1_wi, ffn1_wo=v_ffn1_wo,
              ffn2_wi=v_ffn2_wi, ffn2_wo=v_ffn2_wo, w_in=v_w_in, w_a2_f=v_w_a2_f, b_a_f=v_b_a_f, w_a2_b=v_w_a2_b,
              b_a_b=v_b_a_b, sink=v_sink, gla_g=v_gla_g, w_out=v_w_out, w_pool=v_w_pool, pool_scale=v_pool_scale,
              final_g=v_final_g)
    big = ["ffn2_wi", "ffn2_wo", "w_out", "w_mod", "ffn1_wi", "ffn1_wo"]
    delta, new_m, new_v = {}, {}, {}
    g_all = dict(g_big)

    def adamw_big(nm):
        shp = ws[nm].shape
        two_d = lambda a: a.reshape(-1, shp[-1])
        dl, nm_, nv_ = _adamw(two_d(ws[nm]), two_d(g_all[nm]), two_d(ms[nm]), two_d(vs[nm]), f"adamw_{nm}")
        delta[nm], new_m[nm], new_v[nm] = dl.reshape(shp), nm_.reshape(shp), nv_.reshape(shp)

    for nm in big[:3]:
        adamw_big(nm)

    small_g = [dmods[:, :, :N_MOD].reshape(2, 2, N_MOD * d), grads["norm_g"], grads["pool_scale"], grads["final_g"],
               grads["b_a_f"], grads["b_a_b"], grads["sink"], grads["gla_g"], grads["w_a2_f"], grads["w_a2_b"]]
    small_g_shapes = [a.shape for a in small_g]
    g3 = _gather_small(_pack(small_g), "gather_small_grads", dep=delta["w_out"])
    total = _sum_rows8(g3, "sum_small_grads")
    dmm_all = _unpack(g3, small_g_shapes[:1])[0]
    (dmm_sum, dnorm_g, dpool_scale, dfinal_g, db_a_f, db_a_b, dsink, dgla_g, dw_a2_f, dw_a2_b) = [
        a[0] for a in _unpack(total, small_g_shapes)]
    dmm_rows = jnp.concatenate([dmm_all[:, :, 0].transpose(1, 0, 2), dmm_sum[:, 1][:, None, :],
                                jnp.zeros((2, 16 - N_DEV - 1, N_MOD * d), F32)], axis=1)
    grad_b_mod = dmm_sum[:, 0] + dmm_sum[:, 1]
    dmm_cols = lax.dynamic_slice_in_dim(dmm_rows, me * nc, nc, axis=2)
    cs_t = jnp.transpose(_silu(craw)).astype(BF16)
    grad_w_mod, dcraw = _adaln_bwd(craw, cs_t, dmm_cols, w_mod, "adaln_bwd")
    g4 = _gather_small((dcraw[0, N_DEV] + dcraw[1, N_DEV])[None, :], "gather_c_ctx_grad")
    grad_c_ctx = _sum_rows8(g4, "sum_c_ctx_grad")[0]

    col = lambda v, n: lax.dynamic_slice_in_dim(v, me * n, n, axis=v.ndim - 1)
    g_small = {
        "c_ctx": grad_c_ctx, "b_mod": grad_b_mod, "norm_g": col(dnorm_g, norm_g.shape[2]),
        "w_a2_f": col(dw_a2_f, w_a2_f.shape[2])[None], "b_a_f": db_a_f[None], "w_a2_b": col(dw_a2_b, w_a2_b.shape[2])[None],
        "b_a_b": db_a_b[None], "sink": dsink[None], "gla_g": dgla_g[None], "pool_scale": col(dpool_scale, pool_scale.shape[1])[None],
        "final_g": dfinal_g,
    }
    g_all.update(g_small, w_mod=grad_w_mod)
    adamw_big("w_mod")
    rest = [nm for nm in order if nm not in big]
    rest_shapes = [ws[nm].shape for nm in rest]
    packed = [_pack([d_[nm].reshape(ws[nm].shape) for nm in rest]).reshape(-1, 128) for d_ in (ws, g_all, ms, vs)]
    pad_rows = (-packed[0].shape[0]) % 512
    packed = [jnp.pad(p, ((0, pad_rows), (0, 0))) for p in packed]
    outs = _adamw(*packed, "adamw_small")
    for dst, arr in zip((delta, new_m, new_v), outs):
        for nm, val in zip(rest, _unpack(arr.reshape(1, -1), rest_shapes)):
            dst[nm] = val[0]

    dwi1_0, dwo1_0 = reduce_stage(4, outs[0])
    g_all["ffn1_wi"] = back(jnp.concatenate([dwi1_0, dwi1_1], axis=0))
    g_all["ffn1_wo"] = jnp.concatenate([dwo1_0, dwo1_1], axis=0)
    for nm in big[4:]:
        adamw_big(nm)
    g_all = {nm: g_all[nm].reshape(ws[nm].shape) for nm in order}

    return (loss, grad_x[None], *[g_all[nm] for nm in order], *[delta[nm] for nm in order],
            *[new_m[nm] for nm in order], *[new_v[nm] for nm in order])
```

```python
import functools

import numpy as np
import jax
import jax.numpy as jnp
from jax import lax
from jax.experimental import pallas as pl
from jax.experimental.pallas import tpu as pltpu

F32 = jnp.float32
BF16 = jnp.bfloat16
MESH = pl.DeviceIdType.MESH

N_DEV = 8
RMS_EPS = 1e-6
N_MOD = 9
GRID_W = 64
A_HEADS, A_KV_HEADS, A_HEAD_DIM = 8, 2, 64
A_REP = A_HEADS // A_KV_HEADS
WINDOW = 128
ROPE_BASE = 10000.0
B_HEADS, B_DK, B_DV = 4, 64, 128
B_GATE_RANK = 16
B_GATE_NORM = 16.0
B_CHUNK = 64
POOL_WINDOWS = (2, 4, 8, 16)
POOL_PAD = 8
A_Q = A_HEADS * A_HEAD_DIM
A_KV = A_KV_HEADS * A_HEAD_DIM
B_QK = B_HEADS * B_DK
B_V = B_HEADS * B_DV
PROJ_SIZES = (A_Q, A_KV, A_KV, B_QK, B_QK, B_V, B_V, 2 * B_GATE_RANK)
PROJ_DIM = sum(PROJ_SIZES)
ADAM_LR, ADAM_B1, ADAM_B2, ADAM_EPS, ADAM_WD, ADAM_STEP = 0.001, 0.9, 0.999, 1e-08, 0.01, 10

VMEM_LIMIT = 56 * 1024 * 1024
ROW_TILES = (512, 544, 256, 128, 64, 32, 16, 8)

NN = ((1,), (0,))
NT = ((1,), (1,))
TN = ((0,), (0,))


def _dot(a, b, dims=NN, prec=None):
    return lax.dot_general(a, b, (dims, ((), ())), precision=prec, preferred_element_type=F32)


def _bdot(a, b, dims=NN):
    return _dot(a.astype(BF16), b.astype(BF16), dims)


def _dot_01(sel, x):
    hi = x.astype(BF16)
    rest = x - hi.astype(F32)
    mid = rest.astype(BF16)
    lo = (rest - mid.astype(F32)).astype(BF16)
    sel = sel.astype(BF16)
    return _dot(sel, hi) + _dot(sel, mid) + _dot(sel, lo)


def _params(sem=None, **kw):
    return pltpu.CompilerParams(dimension_semantics=sem, vmem_limit_bytes=VMEM_LIMIT, **kw)


def _silu(a):
    return a * jax.nn.sigmoid(a)


def _pick(n, prefs):
    for p in prefs:
        if n % p == 0:
            return p
    return n


def _full(shape):
    nd = len(shape)
    return pl.BlockSpec(shape, lambda *_: (0,) * nd)


def _peers():
    x, y, c = lax.axis_index("x"), lax.axis_index("y"), lax.axis_index("c")
    return x, y, c


def _dev_index():
    x, y, c = _peers()
    return 4 * x + 2 * y + c


def _others(x, y, c):
    return [(x, y, 1 - c), (1 - x, y, c), (x, 1 - y, c), (1 - x, 1 - y, c),
            (1 - x, y, 1 - c), (x, 1 - y, 1 - c), (1 - x, 1 - y, 1 - c)]


def _index_of(dev):
    return 4 * dev[0] + 2 * dev[1] + dev[2]


def _exchange_refs(gather, shapes, srcs, lands, a, me, to):
    if gather:
        r = shapes[a][1]
        return srcs[a], lands[a].at[:, pl.ds(_index_of(me) * r, r), :]
    r = shapes[a][1] // N_DEV
    return srcs[a].at[:, pl.ds(_index_of(to) * r, r), :], lands[a].at[_index_of(me)]


HBM_SPEC = pl.BlockSpec(memory_space=pltpu.HBM)
SEM_SPEC = pl.BlockSpec(memory_space=pltpu.SEMAPHORE)
EFFECT = pltpu.SideEffectType.DATAFLOW_SIDE_EFFECTING


def _exchange_start(srcs, lands, gather, collective_id, dep, name):
    n = len(srcs)
    shapes = [s.shape for s in srcs]
    deps = [] if dep is None else [dep]

    def body(*refs):
        src_refs, land_refs = refs[:n], refs[n:2 * n]
        send_sems, recv_sems = refs[2 * n + len(deps)], refs[2 * n + len(deps) + 1]
        token = refs[-1]
        x, y, c = _peers()
        others = _others(x, y, c)
        barrier = pltpu.get_barrier_semaphore()
        for peer in others:
            pl.semaphore_signal(barrier, inc=1, device_id=peer, device_id_type=MESH)
        pl.semaphore_wait(barrier, len(others))
        for a in range(n):
            for k, to in enumerate(others):
                src, dst = _exchange_refs(gather, shapes, src_refs, land_refs, a, (x, y, c), to)
                pltpu.make_async_remote_copy(src_ref=src, dst_ref=dst, send_sem=send_sems.at[7 * a + k],
                                             recv_sem=recv_sems.at[7 * a + k], device_id=to, device_id_type=MESH).start()
        token[...] = jnp.zeros_like(token)

    outs = pl.pallas_call(
        body, name=name,
        out_shape=(pltpu.SemaphoreType.DMA((7 * n,)), pltpu.SemaphoreType.DMA((7 * n,)),
                   *[pltpu.HBM(s.shape, s.dtype) for s in srcs], *[pltpu.HBM(l.shape, l.dtype) for l in lands],
                   jax.ShapeDtypeStruct((8, 128), F32)),
        in_specs=[HBM_SPEC] * (2 * n) + [pl.BlockSpec(memory_space=pl.ANY)] * len(deps),
        out_specs=(SEM_SPEC, SEM_SPEC, *[HBM_SPEC] * (2 * n), pl.BlockSpec(memory_space=pltpu.VMEM)),
        input_output_aliases={i: 2 + i for i in range(2 * n)},
        compiler_params=pltpu.CompilerParams(has_side_effects=EFFECT, collective_id=collective_id),
    )(*[pltpu.with_memory_space_constraint(s, pltpu.HBM) for s in srcs],
      *[pltpu.with_memory_space_constraint(l, pltpu.HBM) for l in lands], *deps)
    return outs[0], outs[1], list(outs[2:2 + n]), list(outs[2 + n:2 + 2 * n]), outs[-1]


def _exchange_wait(started, gather, after, name):
    send_sems, recv_sems, srcs, lands, _ = started
    n = len(srcs)
    shapes = [s.shape for s in srcs]

    def body(*refs):
        src_refs, land_refs = refs[:n], refs[n:2 * n]
        send_sems, recv_sems = refs[2 * n], refs[2 * n + 1]
        x, y, c = _peers()
        for a in range(n):
            for k, peer in enumerate(_others(x, y, c)):
                src, _ = _exchange_refs(gather, shapes, src_refs, land_refs, a, (x, y, c), peer)
                _, dst = _exchange_refs(gather, shapes, src_refs, land_refs, a, peer, (x, y, c))
                copy = pltpu.make_async_remote_copy(src_ref=src, dst_ref=dst, send_sem=send_sems.at[7 * a + k],
                                                    recv_sem=recv_sems.at[7 * a + k], device_id=peer, device_id_type=MESH)
                copy.wait_send()
                copy.wait_recv()

    outs = pl.pallas_call(
        body, name=name,
        out_shape=(*[pltpu.HBM(s.shape, s.dtype) for s in srcs], *[pltpu.HBM(l.shape, l.dtype) for l in lands]),
        in_specs=[HBM_SPEC] * (2 * n) + [SEM_SPEC, SEM_SPEC, pl.BlockSpec(memory_space=pl.ANY)],
        out_specs=tuple([HBM_SPEC] * (2 * n)),
        input_output_aliases={i: i for i in range(2 * n)},
        compiler_params=pltpu.CompilerParams(has_side_effects=EFFECT),
    )(*srcs, *lands, send_sems, recv_sems, after)
    return list(outs[:n]), list(outs[n:])


def _place_shards(shards, name):
    n = len(shards)

    def body(*refs):
        ins, outs, sems = refs[:n], refs[n:2 * n], refs[2 * n]
        me = _index_of(_peers())
        copies = []
        for a in range(n):
            r = shards[a].shape[1]
            copies.append(pltpu.make_async_copy(ins[a], outs[a].at[:, pl.ds(me * r, r), :], sems.at[a]))
            copies[-1].start()
        for cp in copies:
            cp.wait()

    any_spec = pl.BlockSpec(memory_space=pl.ANY)
    return pl.pallas_call(
        body, name=name, in_specs=[any_spec] * n, out_specs=[any_spec] * n,
        out_shape=[jax.ShapeDtypeStruct((s.shape[0], N_DEV * s.shape[1], s.shape[2]), s.dtype) for s in shards],
        scratch_shapes=[pltpu.SemaphoreType.DMA((n,))],
    )(*shards)


def _sum_slots(land, whole, me, name):
    _, a_, r, c = land.shape
    tr = _pick(r, (352, 256, 128, 64, 32, 16, 8))
    nr = r // tr

    def body(me_ref, land_ref, own_ref, out_ref):
        acc = None
        for s in range(N_DEV):
            part = jnp.where(me_ref[0] == s, own_ref[...], land_ref[s]).astype(F32)
            acc = part if acc is None else acc + part
        out_ref[...] = acc

    return pl.pallas_call(
        body, name=name,
        grid_spec=pltpu.PrefetchScalarGridSpec(
            num_scalar_prefetch=1, grid=(a_, nr),
            in_specs=[pl.BlockSpec((N_DEV, None, tr, c), lambda i, j, me_ref: (0, i, j, 0)),
                      pl.BlockSpec((None, tr, c), lambda i, j, me_ref: (i, me_ref[0] * nr + j, 0))],
            out_specs=pl.BlockSpec((None, tr, c), lambda i, j, me_ref: (i, j, 0))),
        out_shape=jax.ShapeDtypeStruct((a_, r, c), F32),
        compiler_params=_params(("parallel", "parallel")),
    )(me.reshape(1).astype(jnp.int32), land, whole)


def _gather_small(vec, name, dep=None):
    p = vec.shape[1]
    pp = -(-p // 1024) * 1024
    blk = jnp.pad(vec, ((0, 0), (0, pp - p))).reshape(8, pp // 8)
    deps = [] if dep is None else [dep]

    def body(in_ref, *rest):
        out_ref, send_sems, recv_sems = rest[-3:]
        x, y, c = _peers()
        me = 4 * x + 2 * y + c
        others = [(x, y, 1 - c), (1 - x, y, c), (x, 1 - y, c), (1 - x, 1 - y, c),
                  (1 - x, y, 1 - c), (x, 1 - y, 1 - c), (1 - x, 1 - y, 1 - c)]

        def rows(idx):
            return out_ref.at[pl.ds(pl.multiple_of(idx * 8, 8), 8), :]

        out_ref[pl.ds(pl.multiple_of(me * 8, 8), 8), :] = in_ref[...]

        def copy(k, dev, slot):
            return pltpu.make_async_remote_copy(
                src_ref=in_ref, dst_ref=rows(slot), send_sem=send_sems.at[k], recv_sem=recv_sems.at[k],
                device_id=dev, device_id_type=MESH)

        sends = [copy(k, dev, me) for k, dev in enumerate(others)]
        for cp in sends:
            cp.start()
        for k, dev in enumerate(others):
            copy(k, dev, 4 * dev[0] + 2 * dev[1] + dev[2]).wait_recv()
        for cp in sends:
            cp.wait_send()

    vm = pl.BlockSpec(memory_space=pltpu.VMEM)
    out = pl.pallas_call(
        body, name=name, out_shape=jax.ShapeDtypeStruct((8 * N_DEV, pp // 8), F32),
        in_specs=[vm] + [pl.BlockSpec(memory_space=pl.ANY)] * len(deps), out_specs=vm,
        scratch_shapes=[pltpu.SemaphoreType.DMA((7,)), pltpu.SemaphoreType.DMA((7,))],
        compiler_params=pltpu.CompilerParams(has_side_effects=True, vmem_limit_bytes=VMEM_LIMIT),
    )(blk, *deps)
    return out.reshape(N_DEV, pp)[:, :p]


def _sum_rows8(g, name):
    p = g.shape[1]

    def body(in_ref, out_ref):
        acc = in_ref[0:1, :]
        for s in range(1, N_DEV):
            acc = acc + in_ref[s:s + 1, :]
        out_ref[...] = acc

    return pl.pallas_call(body, name=name, out_shape=jax.ShapeDtypeStruct((1, p), F32),
                          compiler_params=_params())(g)


def _sel_row(mods_ref, is_ctx, k):
    return jnp.where(is_ctx, mods_ref[1, k:k + 1, :], mods_ref[0, k:k + 1, :])


def _modulate(z, mods, g, ks, kc, n_x, out_dtype, name):
    m, d = z.shape
    tm = _pick(m, (256, 128, 64, 32, 16, 8))

    def body(z_ref, mods_ref, g_ref, h_ref):
        is_ctx = pl.program_id(0) * tm >= n_x
        zz = z_ref[...]
        r = lax.rsqrt(jnp.mean(zz * zz, axis=-1, keepdims=True) + RMS_EPS)
        shift, scale = _sel_row(mods_ref, is_ctx, ks), _sel_row(mods_ref, is_ctx, kc)
        h_ref[...] = ((zz * r) * g_ref[...] * (1.0 + scale) + shift).astype(out_dtype)

    return pl.pallas_call(
        body, name=name, grid=(m // tm,),
        in_specs=[pl.BlockSpec((tm, d), lambda i: (i, 0)), _full(mods.shape), _full(g.shape)],
        out_specs=pl.BlockSpec((tm, d), lambda i: (i, 0)),
        out_shape=jax.ShapeDtypeStruct((m, d), out_dtype),
        compiler_params=_params(("parallel",)),
    )(z, mods, g)


def _modulate_bwd(z, dh, dres, mods, g, kc, n_x, name):
    m, d = z.shape
    tm = _pick(m, (256, 128, 64, 32, 16, 8))
    first_ctx = n_x // tm

    def body(z_ref, dh_ref, dres_ref, mods_ref, g_ref, dx_ref, acc_ref):
        i = pl.program_id(0)
        is_ctx = i * tm >= n_x

        @pl.when((i == 0) | (i == first_ctx))
        def _():
            acc_ref[...] = jnp.zeros_like(acc_ref)

        zz, dhh = z_ref[...], dh_ref[...]
        r = lax.rsqrt(jnp.mean(zz * zz, axis=-1, keepdims=True) + RMS_EPS)
        nz = zz * r
        gain = g_ref[...] * (1.0 + _sel_row(mods_ref, is_ctx, kc))
        dn = dhh * gain
        dz = r * (dn - nz * jnp.mean(dn * nz, axis=-1, keepdims=True))
        dx_ref[...] = dres_ref[...] + dz
        acc_ref[0:1, :] += jnp.sum(dhh, axis=0, keepdims=True)
        acc_ref[1:2, :] += jnp.sum(dhh * nz, axis=0, keepdims=True)

    row = pl.BlockSpec((tm, d), lambda i: (i, 0))
    return pl.pallas_call(
        body, name=name, grid=(m // tm,),
        in_specs=[row, row, row, _full(mods.shape), _full(g.shape)],
        out_specs=[row, pl.BlockSpec((None, 8, d), lambda i: ((i * tm >= n_x).astype(jnp.int32), 0, 0))],
        out_shape=[jax.ShapeDtypeStruct((m, d), F32), jax.ShapeDtypeStruct((2, 8, d), F32)],
        compiler_params=_params(("arbitrary",)),
    )(z, dh, dres, mods, g)


def _ffn_up(h, wi_t, layer, name):
    m, d = h.shape
    f = wi_t.shape[1] // 2
    tm = _pick(m, ROW_TILES)

    def body(h_ref, w_ref, au_ref, act_ref):
        hh = h_ref[...]
        a = _dot(hh, w_ref[0:f, :], NT)
        u = _dot(hh, w_ref[f:2 * f, :], NT)
        au_ref[:, 0:f] = a.astype(BF16)
        au_ref[:, f:2 * f] = u.astype(BF16)
        act_ref[...] = (_silu(a) * u).astype(BF16)

    return pl.pallas_call(
        body, name=name, grid=(m // tm,),
        in_specs=[pl.BlockSpec((tm, d), lambda i: (i, 0)),
                  pl.BlockSpec((None, 2 * f, d), lambda i: (layer, 0, 0))],
        out_specs=[pl.BlockSpec((tm, 2 * f), lambda i: (i, 0)), pl.BlockSpec((tm, f), lambda i: (i, 0))],
        out_shape=[jax.ShapeDtypeStruct((m, 2 * f), BF16), jax.ShapeDtypeStruct((m, f), BF16)],
        compiler_params=_params(("parallel",)),
    )(h, wi_t)


def _mm_resid(a, b, layer, res, mods, km, coef, n_x, name, nxt=None):
    m, k = a.shape
    n = b.shape[2]
    tm = _pick(m, (512, 256, 128, 64, 32, 16, 8))
    tn = n if nxt is not None else _pick(n, (1024, 512, 256, 128))
    extra = [] if nxt is None else [nxt[0], nxt[1]]

    def body(a_ref, b_ref, res_ref, mods_ref, *rest):
        is_ctx = pl.program_id(1) * tm >= n_x
        y = _dot(a_ref[...], b_ref[...])
        new = res_ref[...] + coef * _sel_row(mods_ref, is_ctx, km) * y
        if nxt is None:
            out_ref, y_ref = rest
        else:
            nmods_ref, g_ref, out_ref, y_ref, h_ref = rest
            r = lax.rsqrt(jnp.mean(new * new, axis=-1, keepdims=True) + RMS_EPS)
            shift, scale = _sel_row(nmods_ref, is_ctx, nxt[2]), _sel_row(nmods_ref, is_ctx, nxt[3])
            h_ref[...] = ((new * r) * g_ref[...] * (1.0 + scale) + shift).astype(nxt[4])
        y_ref[...] = y
        out_ref[...] = new

    tile = pl.BlockSpec((tm, tn), lambda j, i: (i, j))
    outs = [jax.ShapeDtypeStruct((m, n), F32), jax.ShapeDtypeStruct((m, n), F32)]
    if nxt is not None:
        outs.append(jax.ShapeDtypeStruct((m, n), nxt[4]))
    return pl.pallas_call(
        body, name=name, grid=(n // tn, m // tm),
        in_specs=[pl.BlockSpec((tm, k), lambda j, i: (i, 0)),
                  pl.BlockSpec((None, k, tn), lambda j, i: (layer, 0, j)),
                  tile, pl.BlockSpec((2, 16, tn), lambda j, i: (0, 0, j))] + [_full(e.shape) for e in extra],
        out_specs=[tile] * len(outs), out_shape=outs,
        compiler_params=_params(("parallel", "parallel")),
    )(a, b, res, mods, *extra)


def _resid_bwd(dx, y, mods, km, coef, n_x, name, dep=None):
    m, d = dx.shape
    tm = _pick(m, (256, 128, 64, 32, 16, 8))
    first_ctx = n_x // tm
    deps = [] if dep is None else [dep]

    def body(dx_ref, y_ref, mods_ref, *rest):
        dy_ref, acc_ref = rest[-2:]
        i = pl.program_id(0)
        is_ctx = i * tm >= n_x

        @pl.when((i == 0) | (i == first_ctx))
        def _():
            acc_ref[...] = jnp.zeros_like(acc_ref)

        dxx = dx_ref[...]
        dy_ref[...] = (coef * _sel_row(mods_ref, is_ctx, km) * dxx).astype(BF16)
        acc_ref[0:1, :] += jnp.sum(coef * y_ref[...] * dxx, axis=0, keepdims=True)

    row = pl.BlockSpec((tm, d), lambda i: (i, 0))
    return pl.pallas_call(
        body, name=name, grid=(m // tm,),
        in_specs=[row, row, _full(mods.shape)] + [pl.BlockSpec(memory_space=pl.ANY)] * len(deps),
        out_specs=[row, pl.BlockSpec((None, 8, d), lambda i: ((i * tm >= n_x).astype(jnp.int32), 0, 0))],
        out_shape=[jax.ShapeDtypeStruct((m, d), BF16), jax.ShapeDtypeStruct((2, 8, d), F32)],
        compiler_params=_params(("arbitrary",)),
    )(dx, y, mods, *deps)


def _ffn_down_bwd(dy, wo, layer, au, name):
    m, d = dy.shape
    f = wo.shape[1]
    tm = _pick(m, ROW_TILES)

    def body(dy_ref, wo_ref, au_ref, dau_ref):
        dact = _dot(dy_ref[...], wo_ref[...], NT)
        aa, uu = au_ref[:, 0:f].astype(F32), au_ref[:, f:2 * f].astype(F32)
        sg = jax.nn.sigmoid(aa)
        dau_ref[:, 0:f] = (dact * uu * (sg * (1.0 + aa * (1.0 - sg)))).astype(BF16)
        dau_ref[:, f:2 * f] = (dact * (aa * sg)).astype(BF16)

    wide = pl.BlockSpec((tm, 2 * f), lambda i: (i, 0))
    return pl.pallas_call(
        body, name=name, grid=(m // tm,),
        in_specs=[pl.BlockSpec((tm, d), lambda i: (i, 0)), pl.BlockSpec((None, f, d), lambda i: (layer, 0, 0)), wide],
        out_specs=wide, out_shape=jax.ShapeDtypeStruct((m, 2 * f), BF16),
        compiler_params=_params(("parallel",)),
    )(dy, wo, au)


def _mm(terms, dims, n, out_dtype, name, tm_pref=(512, 256, 128, 64, 32, 16, 8), tn_pref=(512, 256, 128), dep=None):
    m = terms[0][0].shape[0]
    tm = _pick(m, tm_pref)
    tn = _pick(n, tn_pref)
    nt = len(terms)
    deps = [] if dep is None else [dep]

    def body(*refs):
        out_ref = refs[-1]
        acc = None
        for t in range(nt):
            part = _dot(refs[2 * t][...].astype(BF16), refs[2 * t + 1][...].astype(BF16), dims)
            acc = part if acc is None else acc + part
        out_ref[...] = acc.astype(out_dtype)

    in_specs, args = [], []
    for a, b, layer, rb in terms:
        k = a.shape[1]
        in_specs.append(pl.BlockSpec((tm, k), lambda j, i: (i, 0)))
        if dims == NN:
            in_specs.append(pl.BlockSpec((None, k, tn), lambda j, i, layer=layer, rb=rb: (layer, rb, j)))
        else:
            nb = n // tn
            in_specs.append(pl.BlockSpec((None, tn, k), lambda j, i, layer=layer, rb=rb, nb=nb: (layer, rb * nb + j, 0)))
        args += [a, b]
    return pl.pallas_call(
        body, name=name, grid=(n // tn, m // tm), in_specs=in_specs + [pl.BlockSpec(memory_space=pl.ANY)] * len(deps),
        out_specs=pl.BlockSpec((tm, tn), lambda j, i: (i, j)),
        out_shape=jax.ShapeDtypeStruct((m, n), out_dtype),
        compiler_params=_params(("parallel", "parallel")),
    )(*args, *deps)


def _mm_tn(a, b, out_dtype, name, dep=None):
    t = a.shape[0]
    m, n = a.shape[1], b.shape[1]
    tm = _pick(m, (1408, 2432, 1024, 512, 256, 128))
    tn = _pick(n, (1024, 512, 256, 128))
    tk = _pick(t, (512, 256, 128, 64, 32, 16, 8))
    deps = [] if dep is None else [dep]

    def body(a_ref, b_ref, *rest):
        out_ref, acc_ref = rest[-2:]
        kk = pl.program_id(2)

        @pl.when(kk == 0)
        def _():
            acc_ref[...] = jnp.zeros_like(acc_ref)

        acc_ref[...] += _dot(a_ref[...].astype(BF16), b_ref[...].astype(BF16), TN)

        @pl.when(kk == pl.num_programs(2) - 1)
        def _():
            out_ref[...] = acc_ref[...].astype(out_dtype)

    return pl.pallas_call(
        body, name=name, grid=(m // tm, n // tn, t // tk),
        in_specs=[pl.BlockSpec((tk, tm), lambda i, j, k: (k, i)), pl.BlockSpec((tk, tn), lambda i, j, k: (k, j))]
        + [pl.BlockSpec(memory_space=pl.ANY)] * len(deps),
        out_specs=pl.BlockSpec((tm, tn), lambda i, j, k: (i, j)),
        out_shape=jax.ShapeDtypeStruct((m, n), out_dtype),
        scratch_shapes=[pltpu.VMEM((tm, tn), F32)],
        compiler_params=_params(("parallel", "parallel", "arbitrary")),
    )(a, b, *deps)


def _final_loss(x, g, target, name):
    t, d = x.shape
    tm = _pick(t, (256, 128, 64, 32, 16, 8))

    def body(x_ref, g_ref, t_ref, dx_ref, loss_ref, dg_ref):
        @pl.when(pl.program_id(0) == 0)
        def _():
            loss_ref[...] = jnp.zeros_like(loss_ref)
            dg_ref[...] = jnp.zeros_like(dg_ref)

        xx, gg = x_ref[...], g_ref[...]
        r = lax.rsqrt(jnp.mean(xx * xx, axis=-1, keepdims=True) + RMS_EPS)
        nz = xx * r
        err = nz * gg - t_ref[...]
        loss_ref[...] += jnp.sum(err * err, axis=0, keepdims=True) * (0.5 / d)
        dout = err * (1.0 / d)
        dg_ref[...] += jnp.sum(dout * nz, axis=0, keepdims=True)
        dn = dout * gg
        dx_ref[...] = r * (dn - nz * jnp.mean(dn * nz, axis=-1, keepdims=True))

    row = pl.BlockSpec((tm, d), lambda i: (i, 0))
    vec = pl.BlockSpec((1, d), lambda i: (0, 0))
    return pl.pallas_call(
        body, name=name, grid=(t // tm,), in_specs=[row, vec, row], out_specs=[row, vec, vec],
        out_shape=[jax.ShapeDtypeStruct((t, d), F32), jax.ShapeDtypeStruct((1, d), F32),
                   jax.ShapeDtypeStruct((1, d), F32)],
        compiler_params=_params(("arbitrary",)),
    )(x, g, target)


def _adaln_fwd(craw, w_mod, b_cols, name):
    lyr, d, nc = w_mod.shape

    def body(c_ref, w_ref, b_ref, out_ref):
        out_ref[...] = _bdot(_silu(c_ref[...]), w_ref[...]) + b_ref[...]

    return pl.pallas_call(
        body, name=name, grid=(lyr,),
        in_specs=[_full(craw.shape), pl.BlockSpec((None, d, nc), lambda l: (l, 0, 0)),
                  pl.BlockSpec((None, 1, nc), lambda l: (l, 0, 0))],
        out_specs=pl.BlockSpec((None, 16, nc), lambda l: (l, 0, 0)),
        out_shape=jax.ShapeDtypeStruct((lyr, 16, nc), F32),
        compiler_params=_params(("parallel",)),
    )(craw, w_mod, b_cols)


def _adaln_bwd(craw, cs_t, dmm_cols, w_mod, name):
    lyr, d, nc = w_mod.shape

    def body(c_ref, cst_ref, dmm_ref, w_ref, gw_ref, dc_ref):
        dmm = dmm_ref[...]
        gw_ref[...] = _bdot(cst_ref[...], dmm)
        cc = c_ref[...]
        sg = jax.nn.sigmoid(cc)
        dc_ref[...] = _bdot(dmm, w_ref[...], NT) * (sg * (1.0 + cc * (1.0 - sg)))

    wspec = pl.BlockSpec((None, d, nc), lambda l: (l, 0, 0))
    return pl.pallas_call(
        body, name=name, grid=(lyr,),
        in_specs=[_full(craw.shape), _full(cs_t.shape), pl.BlockSpec((None, 16, nc), lambda l: (l, 0, 0)), wspec],
        out_specs=[wspec, pl.BlockSpec((None, 16, d), lambda l: (l, 0, 0))],
        out_shape=[jax.ShapeDtypeStruct((lyr, d, nc), F32), jax.ShapeDtypeStruct((lyr, 16, d), F32)],
        compiler_params=_params(("parallel",)),
    )(craw, cs_t, dmm_cols, w_mod)


def _rope_tables(t):
    rows = np.repeat(np.arange(t // GRID_W, dtype=np.float32), GRID_W)
    cols = np.tile(np.arange(GRID_W, dtype=np.float32), t // GRID_W)
    n = A_HEAD_DIM // 4
    freqs = (ROPE_BASE ** (-np.arange(n, dtype=np.float32) / n)).astype(np.float32)
    ang_r, ang_c = (rows[:, None] * freqs).astype(np.float32), (cols[:, None] * freqs).astype(np.float32)
    cr, sr, cc, sc = np.cos(ang_r), np.sin(ang_r), np.cos(ang_c), np.sin(ang_c)
    cos = np.concatenate([cr, cr, cc, cc] * 2, axis=-1).astype(np.float32)
    sin = np.concatenate([-sr, sr, -sc, sc] * 2, axis=-1).astype(np.float32)
    return jnp.asarray(cos), jnp.asarray(sin)


def _rope(xt, cos, sin, adjoint, name):
    t, w = xt.shape
    tb = _pick(t, (512, 256, 128))
    rep = w // cos.shape[1]

    def body(x_ref, c_ref, s_ref, o_ref):
        xx = x_ref[...]
        cc = jnp.concatenate([c_ref[...]] * rep, axis=1) if rep > 1 else c_ref[...]
        ss = jnp.concatenate([s_ref[...]] * rep, axis=1) if rep > 1 else s_ref[...]
        low = (lax.broadcasted_iota(jnp.int32, xx.shape, 1) % 32) < 16

        def partner(v):
            return jnp.where(low, pltpu.roll(v, w - 16, 1), pltpu.roll(v, 16, 1))

        if adjoint:
            o_ref[...] = xx * cc + partner(xx * ss)
        else:
            o_ref[...] = xx * cc + partner(xx) * ss

    blk = pl.BlockSpec((tb, w), lambda i: (i, 0))
    tab = pl.BlockSpec((tb, cos.shape[1]), lambda i: (i, 0))
    return pl.pallas_call(
        body, name=name, grid=(t // tb,), in_specs=[blk, tab, tab], out_specs=blk,
        out_shape=jax.ShapeDtypeStruct((t, w), F32), compiler_params=_params(("parallel",)),
    )(xt, cos, sin)


def _attn_probs(q, kb, kc, sink, n, t):
    scale = A_HEAD_DIM ** -0.5
    s1 = _bdot(q, kb, NT) * scale
    s2 = _bdot(q, kc, NT) * scale
    qpos = n * WINDOW + lax.broadcasted_iota(jnp.int32, s1.shape, 0) % WINDOW
    kpos = (n - 1) * WINDOW + lax.broadcasted_iota(jnp.int32, s1.shape, 1)
    valid = (kpos >= 0) & (kpos < t) & (jnp.abs(kpos - qpos) <= WINDOW)
    s1 = jnp.where(valid, s1, -jnp.inf)
    mx = jnp.maximum(jnp.maximum(jnp.max(s1, axis=-1, keepdims=True), jnp.max(s2, axis=-1, keepdims=True)), sink)
    p1, p2, ps = jnp.exp(s1 - mx), jnp.exp(s2 - mx), jnp.exp(sink - mx)
    inv = 1.0 / (jnp.sum(p1, axis=-1, keepdims=True) + jnp.sum(p2, axis=-1, keepdims=True) + ps)
    return p1 * inv, p2 * inv, ps * inv


def _sink_rows(sink_ref):
    return jnp.concatenate([jnp.broadcast_to(sink_ref[r], (WINDOW, 1)) for r in range(A_REP)], axis=0)


def _attn_fwd(q, kp, vp, kc, vc, sink, name):
    hq, t, dh = q.shape
    nb = t // WINDOW
    lc = kc.shape[1]
    rows = A_REP * WINDOW

    def body(q_ref, k_ref, v_ref, kc_ref, vc_ref, sink_ref, o_ref):
        n = pl.program_id(1)
        start = pl.multiple_of(n * WINDOW, WINDOW)
        kb, vb = k_ref[pl.ds(start, 3 * WINDOW), :], v_ref[pl.ds(start, 3 * WINDOW), :]
        p1, p2, _ = _attn_probs(q_ref[...].reshape(rows, dh), kb, kc_ref[...], _sink_rows(sink_ref), n, t)
        o_ref[...] = (_bdot(p1, vb) + _bdot(p2, vc_ref[...])).reshape(A_REP, WINDOW, dh)

    qblk = pl.BlockSpec((A_REP, WINDOW, dh), lambda g, n: (g, n, 0))
    kfull = pl.BlockSpec((None, t + 2 * WINDOW, dh), lambda g, n: (g, 0, 0))
    cfull = pl.BlockSpec((None, lc, dh), lambda g, n: (g, 0, 0))
    return pl.pallas_call(
        body, name=name, grid=(hq // A_REP, nb),
        in_specs=[qblk, kfull, kfull, cfull, cfull, pl.BlockSpec((A_REP, 1, 1), lambda g, n: (g, 0, 0))],
        out_specs=qblk, out_shape=jax.ShapeDtypeStruct((hq, t, dh), F32),
        compiler_params=_params(("parallel", "parallel")),
    )(q, kp, vp, kc, vc, sink)


def _attn_bwd(q, kp, vp, kc, vc, sink, o, do, name):
    hq, t, dh = q.shape
    nb = t // WINDOW
    lc = kc.shape[1]
    scale = A_HEAD_DIM ** -0.5
    rows = A_REP * WINDOW

    def body(q_ref, k_ref, v_ref, kc_ref, vc_ref, sink_ref, o_ref, do_ref,
             dq_ref, dk_ref, dv_ref, dkc_ref, dvc_ref, dsink_ref):
        n = pl.program_id(1)

        @pl.when(n == 0)
        def _():
            dk_ref[...] = jnp.zeros_like(dk_ref)
            dv_ref[...] = jnp.zeros_like(dv_ref)
            dkc_ref[...] = jnp.zeros_like(dkc_ref)
            dvc_ref[...] = jnp.zeros_like(dvc_ref)
            dsink_ref[...] = jnp.zeros_like(dsink_ref)

        start = pl.multiple_of(n * WINDOW, WINDOW)
        band = pl.ds(start, 3 * WINDOW)
        qq, kb, vb, kcc, vcc = q_ref[...].reshape(rows, dh), k_ref[band, :], v_ref[band, :], kc_ref[...], vc_ref[...]
        p1, p2, ps = _attn_probs(qq, kb, kcc, _sink_rows(sink_ref), n, t)
        dout = do_ref[...].reshape(rows, dh)
        delta = jnp.sum(dout * o_ref[...].reshape(rows, dh), axis=-1, keepdims=True)
        ds1 = p1 * (_bdot(dout, vb, NT) - delta)
        ds2 = p2 * (_bdot(dout, vcc, NT) - delta)
        dq_ref[...] = ((_bdot(ds1, kb) + _bdot(ds2, kcc)) * scale).reshape(A_REP, WINDOW, dh)
        dk_ref[band, :] += _bdot(ds1.T, qq) * scale
        dv_ref[band, :] += _bdot(p1.T, dout)
        dkc_ref[...] += _bdot(ds2.T, qq) * scale
        dvc_ref[...] += _bdot(p2.T, dout)
        dsink_ref[...] += jnp.sum((-ps * delta).reshape(A_REP, WINDOW, 1), axis=1, keepdims=True)

    qblk = pl.BlockSpec((A_REP, WINDOW, dh), lambda g, n: (g, n, 0))
    kfull = pl.BlockSpec((None, t + 2 * WINDOW, dh), lambda g, n: (g, 0, 0))
    cfull = pl.BlockSpec((None, lc, dh), lambda g, n: (g, 0, 0))
    return pl.pallas_call(
        body, name=name, grid=(hq // A_REP, nb),
        in_specs=[qblk, kfull, kfull, cfull, cfull, pl.BlockSpec((A_REP, 1, 1), lambda g, n: (g, 0, 0)), qblk, qblk],
        out_specs=[qblk, kfull, kfull, cfull, cfull, pl.BlockSpec((A_REP, 8, 128), lambda g, n: (g, 0, 0))],
        out_shape=[jax.ShapeDtypeStruct(q.shape, F32), jax.ShapeDtypeStruct(kp.shape, F32),
                   jax.ShapeDtypeStruct(kp.shape, F32), jax.ShapeDtypeStruct(kc.shape, F32),
                   jax.ShapeDtypeStruct(kc.shape, F32), jax.ShapeDtypeStruct((hq, 8, 128), F32)],
        compiler_params=_params(("parallel", "arbitrary")),
    )(q, kp, vp, kc, vc, sink, o, do)


def _gate_fwd(zg, w2, b2, name):
    m = zg.shape[0]
    n = w2.shape[1]
    tm = _pick(m, (512, 256, 128, 64, 32, 16, 8))

    def body(z_ref, w_ref, b_ref, o_ref):
        o_ref[...] = jax.nn.log_sigmoid(_bdot(z_ref[...], w_ref[...]) + b_ref[...]) / B_GATE_NORM

    return pl.pallas_call(
        body, name=name, grid=(m // tm,),
        in_specs=[pl.BlockSpec((tm, zg.shape[1]), lambda i: (i, 0)), _full(w2.shape), _full(b2.shape)],
        out_specs=pl.BlockSpec((tm, n), lambda i: (i, 0)), out_shape=jax.ShapeDtypeStruct((m, n), F32),
        compiler_params=_params(("parallel",)),
    )(zg, w2, b2)


def _gate_bwd(zg, w2, b2, dla, name):
    m, rk = zg.shape
    n = w2.shape[1]
    tm = _pick(m, (512, 256, 128, 64, 32, 16, 8))

    def body(z_ref, w_ref, b_ref, d_ref, dz_ref, dw_ref, db_ref):
        @pl.when(pl.program_id(0) == 0)
        def _():
            dw_ref[...] = jnp.zeros_like(dw_ref)
            db_ref[...] = jnp.zeros_like(db_ref)

        zz, ww = z_ref[...], w_ref[...]
        pre = _bdot(zz, ww) + b_ref[...]
        dpre = d_ref[...] * (1.0 / B_GATE_NORM) * jax.nn.sigmoid(-pre)
        dz_ref[...] = _bdot(dpre, ww, NT)
        dw_ref[...] += _bdot(zz.T, dpre)
        db_ref[...] += jnp.sum(dpre, axis=0, keepdims=True)

    return pl.pallas_call(
        body, name=name, grid=(m // tm,),
        in_specs=[pl.BlockSpec((tm, rk), lambda i: (i, 0)), _full(w2.shape), _full(b2.shape),
                  pl.BlockSpec((tm, n), lambda i: (i, 0))],
        out_specs=[pl.BlockSpec((tm, rk), lambda i: (i, 0)), _full(w2.shape), _full(b2.shape)],
        out_shape=[jax.ShapeDtypeStruct((m, rk), F32), jax.ShapeDtypeStruct(w2.shape, F32),
                   jax.ShapeDtypeStruct(b2.shape, F32)],
        compiler_params=_params(("arbitrary",)),
    )(zg, w2, b2, dla)


def _chunk_order(step, n_x_chunks, n_chunks, reverse):
    n_c = n_chunks - n_x_chunks
    if reverse:
        return jnp.where(step < n_c, n_chunks - 1 - step, n_chunks - 1 - step)
    return jnp.where(step < n_c, n_x_chunks + step, step - n_c)


def _tri(reverse, transpose=False):
    i = lax.broadcasted_iota(jnp.int32, (B_CHUNK, B_CHUNK), 0)
    j = lax.broadcasted_iota(jnp.int32, (B_CHUNK, B_CHUNK), 1)
    if transpose:
        i, j = j, i
    return (j >= i) if reverse else (j <= i)


def _gla_chunk(q, k, la, reverse):
    g = _dot_01(_tri(reverse), la)
    last = 0 if reverse else B_CHUNK - 1
    gl = g[last:last + 1, :]
    eg, eng, egl = jnp.exp(g), jnp.exp(-g), jnp.exp(gl - g)
    decay_col = jnp.exp(jnp.sum(la.T, axis=1, keepdims=True))
    return q * (B_DK ** -0.5) * eg, k * eng, k * egl, eg, eng, egl, decay_col


def _head_of(shape, axis, width):
    return lax.broadcasted_iota(jnp.int32, shape, axis) // width


def _gla_fwd(q, k, v, la_f, la_b, n_x, name):
    tc, wk = q.shape
    wv = v.shape[1]
    hh = B_HEADS
    dk, dv = wk // hh, wv // hh
    nc, nxc = tc // B_CHUNK, n_x // B_CHUNK
    orders = [functools.partial(_chunk_order, n_x_chunks=nxc, n_chunks=nc, reverse=rev) for rev in (False, True)]

    def body(*refs):
        ins, outs, s_refs = refs[:8], refs[8:12], refs[12:]

        @pl.when(pl.program_id(0) == 0)
        def _():
            for s_ref in s_refs:
                s_ref[...] = jnp.zeros_like(s_ref)

        lane_head = _head_of((B_CHUNK, wk), 1, dk)
        row_head = _head_of((wk, dv), 0, dk)
        for di, reverse in enumerate((False, True)):
            q_ref, k_ref, v_ref, la_ref = ins[4 * di:4 * di + 4]
            o_ref, s_save_ref = outs[2 * di:2 * di + 2]
            s_ref = s_refs[di]
            qt, kt, ke, _, _, _, decay_col = _gla_chunk(q_ref[...], k_ref[...], la_ref[...], reverse)
            ke_t = ke.T
            s_prev = s_ref[...]
            update = jnp.zeros_like(s_prev)
            for h in range(hh):
                vv = v_ref[:, h * dv:(h + 1) * dv]
                qm = jnp.where(lane_head == h, qt, 0.0)
                att = jnp.where(_tri(reverse), _bdot(qm, kt, NT), 0.0)
                o_ref[:, h * dv:(h + 1) * dv] = _bdot(att, vv) + _bdot(qm, s_prev)
                update = jnp.where(row_head == h, _bdot(ke_t, vv), update)
            s_save_ref[...] = s_prev
            s_ref[...] = decay_col * s_prev + update

    def blk(w, order):
        return pl.BlockSpec((B_CHUNK, w), lambda s: (order(s), 0))

    def sblk(order):
        return pl.BlockSpec((None, wk, dv), lambda s: (order(s), 0, 0))

    in_specs, out_specs = [], []
    for order in orders:
        in_specs += [blk(wk, order), blk(wk, order), blk(wv, order), blk(wk, order)]
        out_specs += [blk(wv, order), sblk(order)]
    o_shape, s_shape = jax.ShapeDtypeStruct((tc, wv), F32), jax.ShapeDtypeStruct((nc, wk, dv), F32)
    return pl.pallas_call(
        body, name=name, grid=(nc,), in_specs=in_specs, out_specs=out_specs,
        out_shape=[o_shape, s_shape, o_shape, s_shape],
        scratch_shapes=[pltpu.VMEM((wk, dv), F32)] * 2,
        compiler_params=_params(("arbitrary",)),
    )(q, k, v, la_f, q, k, v, la_b)


def _gla_bwd(q, k, v, la_f, la_b, s_f, s_b, do, n_x, name):
    tc, wk = q.shape
    wv = v.shape[1]
    hh = B_HEADS
    dk, dv = wk // hh, wv // hh
    nc, nxc = tc // B_CHUNK, n_x // B_CHUNK
    orders = [functools.partial(lambda s, rev: _chunk_order(nc - 1 - s, nxc, nc, rev), rev=rev) for rev in (False, True)]

    def body(*refs):
        ins, outs, ds_refs = refs[:12], refs[12:20], refs[20:]

        @pl.when(pl.program_id(0) == 0)
        def _():
            for ds_ref in ds_refs:
                ds_ref[...] = jnp.zeros_like(ds_ref)

        lane_head = _head_of((B_CHUNK, wk), 1, dk)
        row_head = _head_of((wk, dv), 0, dk)
        for di, reverse in enumerate((False, True)):
            q_ref, k_ref, v_ref, la_ref, s_save_ref, do_ref = ins[6 * di:6 * di + 6]
            dq_ref, dk_ref, dv_ref, dla_ref = outs[4 * di:4 * di + 4]
            ds_ref = ds_refs[di]
            mask = _tri(reverse)
            last = 0 if reverse else B_CHUNK - 1
            is_last = lax.broadcasted_iota(jnp.int32, (B_CHUNK, wk), 0) == last
            la = la_ref[...]
            qt, kt, ke, eg, eng, egl, decay_col = _gla_chunk(q_ref[...], k_ref[...], la, reverse)
            qt_t = qt.T
            s_prev, ds_new = s_save_ref[...], ds_ref[...]
            dqt, dkt, dke = jnp.zeros_like(qt), jnp.zeros_like(qt), jnp.zeros_like(qt)
            ds_add = jnp.zeros_like(ds_new)
            for h in range(hh):
                cols = slice(h * dv, (h + 1) * dv)
                vv, dout = v_ref[:, cols], do_ref[:, cols]
                mine = lane_head == h
                qm, km = jnp.where(mine, qt, 0.0), jnp.where(mine, ke, 0.0)
                att = jnp.where(mask, _bdot(qm, kt, NT), 0.0)
                datt = jnp.where(mask, _bdot(dout, vv, NT), 0.0)
                dv_ref[:, cols] = _bdot(att.T, dout) + _bdot(km, ds_new)
                dqt = jnp.where(mine, _bdot(datt, kt) + _bdot(dout, s_prev, NT), dqt)
                dkt = jnp.where(mine, _bdot(datt.T, qt), dkt)
                dke = jnp.where(mine, _bdot(vv, ds_new, NT), dke)
                ds_add = jnp.where(row_head == h, _bdot(qt_t, dout), ds_add)
            ddecay_row = jnp.sum((ds_new * s_prev).T, axis=0, keepdims=True)
            decay_row = jnp.exp(jnp.sum(la, axis=0, keepdims=True))
            ds_ref[...] = decay_col * ds_new + ds_add
            dq_ref[...] = dqt * (B_DK ** -0.5) * eg
            dk_ref[...] = dkt * eng + dke * egl
            dgl = jnp.sum(dke * ke, axis=0, keepdims=True) + ddecay_row * decay_row
            dg = dqt * qt - dkt * kt - dke * ke + jnp.where(is_last, dgl, 0.0)
            dla_ref[...] = _dot_01(_tri(reverse, transpose=True), dg)

    def blk(w, order):
        return pl.BlockSpec((B_CHUNK, w), lambda s: (order(s), 0))

    in_specs, out_specs = [], []
    for order in orders:
        in_specs += [blk(wk, order), blk(wk, order), blk(wv, order), blk(wk, order),
                     pl.BlockSpec((None, wk, dv), lambda s, order=order: (order(s), 0, 0)), blk(wv, order)]
        out_specs += [blk(wk, order), blk(wk, order), blk(wv, order), blk(wk, order)]
    k_shape, v_shape = jax.ShapeDtypeStruct((tc, wk), F32), jax.ShapeDtypeStruct((tc, wv), F32)
    return pl.pallas_call(
        body, name=name, grid=(nc,), in_specs=in_specs, out_specs=out_specs,
        out_shape=[k_shape, k_shape, v_shape, k_shape] * 2,
        scratch_shapes=[pltpu.VMEM((wk, dv), F32)] * 2,
        compiler_params=_params(("arbitrary",)),
    )(q, k, v, la_f, s_f, do, q, k, v, la_b, s_b, do)


def _gla_out_fwd(o_f, o_b, r, g, name):
    t = r.shape[0]
    dv = g.shape[1]
    hh = r.shape[1] // dv
    tb = _pick(t, (256, 128, 64))

    def body(of_ref, ob_ref, r_ref, g_ref, out_ref):
        for h in range(hh):
            cols = slice(h * dv, (h + 1) * dv)
            o = of_ref[:, cols] + ob_ref[:, cols]
            rs = lax.rsqrt(jnp.mean(o * o, axis=-1, keepdims=True) + RMS_EPS)
            out_ref[:, cols] = (o * rs) * g_ref[...] * _silu(r_ref[:, cols])

    rblk = pl.BlockSpec((tb, hh * dv), lambda i: (i, 0))
    return pl.pallas_call(
        body, name=name, grid=(t // tb,), in_specs=[rblk, rblk, rblk, _full(g.shape)], out_specs=rblk,
        out_shape=jax.ShapeDtypeStruct((t, hh * dv), F32), compiler_params=_params(("parallel",)),
    )(o_f, o_b, r, g)


def _gla_out_bwd(o_f, o_b, r, g, dout, name):
    tc = o_f.shape[0]
    t = r.shape[0]
    dv = g.shape[1]
    hh = r.shape[1] // dv
    tb = _pick(int(np.gcd(t, tc)), (256, 128, 64))
    nt = t // tb

    def body(of_ref, ob_ref, r_ref, g_ref, d_ref, do_ref, dr_ref, dg_ref):
        i = pl.program_id(0)

        @pl.when(i == 0)
        def _():
            dg_ref[...] = jnp.zeros_like(dg_ref)

        @pl.when(i >= nt)
        def _():
            do_ref[...] = jnp.zeros_like(do_ref)

        @pl.when(i < nt)
        def _():
            gg = g_ref[...]
            for h in range(hh):
                cols = slice(h * dv, (h + 1) * dv)
                o = of_ref[:, cols] + ob_ref[:, cols]
                rs = lax.rsqrt(jnp.mean(o * o, axis=-1, keepdims=True) + RMS_EPS)
                nz = o * rs
                rr, dd = r_ref[:, cols], d_ref[:, cols]
                sg = jax.nn.sigmoid(rr)
                dr_ref[:, cols] = dd * nz * gg * (sg * (1.0 + rr * (1.0 - sg)))
                dy = dd * (rr * sg)
                dg_ref[...] += jnp.sum(dy * nz, axis=0, keepdims=True)
                dn = dy * gg
                do_ref[:, cols] = rs * (dn - nz * jnp.mean(dn * nz, axis=-1, keepdims=True))

    oblk = pl.BlockSpec((tb, hh * dv), lambda i: (i, 0))
    rblk = pl.BlockSpec((tb, hh * dv), lambda i: (jnp.minimum(i, nt - 1), 0))
    return pl.pallas_call(
        body, name=name, grid=(tc // tb,), in_specs=[oblk, oblk, rblk, _full(g.shape), rblk],
        out_specs=[oblk, rblk, _full(g.shape)],
        out_shape=[jax.ShapeDtypeStruct(o_f.shape, F32), jax.ShapeDtypeStruct(r.shape, F32),
                   jax.ShapeDtypeStruct(g.shape, F32)],
        compiler_params=_params(("arbitrary",)),
    )(o_f, o_b, r, g, dout)


def _pool_band(half, tb, adjoint):
    r = lax.broadcasted_iota(jnp.int32, (tb, tb + 2 * POOL_PAD), 0) + POOL_PAD
    j = lax.broadcasted_iota(jnp.int32, (tb, tb + 2 * POOL_PAD), 1)
    if adjoint:
        return ((j > r - half) & (j <= r + half)).astype(F32)
    return ((j >= r - half) & (j < r + half)).astype(F32)


def _pool_count(pos, half, t):
    return (jnp.minimum(pos + half, t) - jnp.maximum(pos - half, 0)).astype(F32)


def _pool_fwd(hp, w_pool, pool_scale, res, mods, km, name):
    t, d = res.shape
    ng, gw = w_pool.shape[0], w_pool.shape[1]
    tb = _pick(t, (256, 128, 64))

    def body(hp_ref, w_ref, ps_ref, res_ref, mods_ref, out_ref, pooled_ref, ypre_ref):
        gi, i = pl.program_id(0), pl.program_id(1)
        half = jnp.left_shift(1, gi)
        win = hp_ref[pl.ds(pl.multiple_of(i * tb, tb), tb + 2 * POOL_PAD), :]
        total = _dot_01(_pool_band(half, tb, False), win)
        pos = i * tb + lax.broadcasted_iota(jnp.int32, (tb, 1), 0)
        pooled = total / _pool_count(pos, half, t) - win[POOL_PAD:POOL_PAD + tb, :]
        ypre = _bdot(pooled, w_ref[...])
        pooled_ref[...] = pooled.astype(BF16)
        ypre_ref[...] = ypre
        out_ref[...] = res_ref[...] + mods_ref[0, km:km + 1, :] * (ypre * ps_ref[...])

    tile = pl.BlockSpec((tb, gw), lambda gi, i: (i, gi))
    return pl.pallas_call(
        body, name=name, grid=(ng, t // tb),
        in_specs=[pl.BlockSpec((t + 2 * POOL_PAD, gw), lambda gi, i: (0, gi)),
                  pl.BlockSpec((None, gw, gw), lambda gi, i: (gi, 0, 0)),
                  pl.BlockSpec((1, gw), lambda gi, i: (0, gi)), tile,
                  pl.BlockSpec((2, 16, gw), lambda gi, i: (0, 0, gi))],
        out_specs=[tile, tile, tile],
        out_shape=[jax.ShapeDtypeStruct((t, d), F32), jax.ShapeDtypeStruct((t, d), BF16),
                   jax.ShapeDtypeStruct((t, d), F32)],
        compiler_params=_params(("parallel", "parallel")),
    )(hp, w_pool, pool_scale, res, mods)


def _pool_bwd(dxp, w_pool, pool_scale, pooled, ypre, mods, km, name):
    t, d = pooled.shape
    ng, gw = w_pool.shape[0], w_pool.shape[1]
    tb = _pick(t, (256, 128, 64))

    def body(dxp_ref, w_ref, ps_ref, pooled_ref, ypre_ref, mods_ref, dh_ref, dw_ref, acc_ref):
        gi, i = pl.program_id(0), pl.program_id(1)

        @pl.when(i == 0)
        def _():
            dw_ref[...] = jnp.zeros_like(dw_ref)
            acc_ref[...] = jnp.zeros_like(acc_ref)

        half = jnp.left_shift(1, gi)
        mod, ps = mods_ref[0, km:km + 1, :], ps_ref[...]
        dwin = dxp_ref[pl.ds(pl.multiple_of(i * tb, tb), tb + 2 * POOL_PAD), :]
        dpooled = _bdot(dwin * (mod * ps), w_ref[...], NT)
        pos = i * tb - POOL_PAD + lax.broadcasted_iota(jnp.int32, (tb + 2 * POOL_PAD, 1), 0)
        spread = _dot_01(_pool_band(half, tb, True), dpooled / jnp.maximum(_pool_count(pos, half, t), 1.0))
        dh_ref[...] = spread - dpooled[POOL_PAD:POOL_PAD + tb, :]
        dxc, yp = dwin[POOL_PAD:POOL_PAD + tb, :], ypre_ref[...]
        dw_ref[...] += _bdot(pooled_ref[...].astype(F32).T, dxc * (mod * ps))
        acc_ref[0:1, :] += jnp.sum(dxc * yp * mod, axis=0, keepdims=True)
        acc_ref[1:2, :] += jnp.sum(dxc * yp * ps, axis=0, keepdims=True)

    tile = pl.BlockSpec((tb, gw), lambda gi, i: (i, gi))
    wblk = pl.BlockSpec((None, gw, gw), lambda gi, i: (gi, 0, 0))
    return pl.pallas_call(
        body, name=name, grid=(ng, t // tb),
        in_specs=[pl.BlockSpec((t + 2 * POOL_PAD, gw), lambda gi, i: (0, gi)), wblk,
                  pl.BlockSpec((1, gw), lambda gi, i: (0, gi)), tile, tile,
                  pl.BlockSpec((2, 16, gw), lambda gi, i: (0, 0, gi))],
        out_specs=[tile, wblk, pl.BlockSpec((8, gw), lambda gi, i: (0, gi))],
        out_shape=[jax.ShapeDtypeStruct((t, d), F32), jax.ShapeDtypeStruct(w_pool.shape, F32),
                   jax.ShapeDtypeStruct((8, d), F32)],
        compiler_params=_params(("arbitrary", "arbitrary")),
    )(dxp, w_pool, pool_scale, pooled, ypre, mods)


def _adamw(w, g, m, v, name):
    r, c = w.shape
    tr = _pick(r, (512, 352, 256, 128, 64, 32, 16, 8))
    c1 = 1.0 / (1.0 - ADAM_B1 ** ADAM_STEP)
    c2 = 1.0 / (1.0 - ADAM_B2 ** ADAM_STEP)

    def body(w_ref, g_ref, m_ref, v_ref, d_ref, nm_ref, nv_ref):
        gg = g_ref[...]
        nm = ADAM_B1 * m_ref[...] + (1.0 - ADAM_B1) * gg
        nv = ADAM_B2 * v_ref[...] + (1.0 - ADAM_B2) * (gg * gg)
        nm_ref[...] = nm
        nv_ref[...] = nv
        d_ref[...] = -ADAM_LR * ((nm * c1) / (jnp.sqrt(nv * c2) + ADAM_EPS) + ADAM_WD * w_ref[...])

    blk = pl.BlockSpec((tr, c), lambda i: (i, 0))
    shp = jax.ShapeDtypeStruct((r, c), F32)
    return pl.pallas_call(
        body, name=name, grid=(r // tr,), in_specs=[blk] * 4, out_specs=[blk] * 3, out_shape=[shp] * 3,
        compiler_params=_params(("parallel",)),
    )(w, g, m, v)


def _heads(z, n_heads):
    m = z.shape[0]
    return z.reshape(m, n_heads, -1).transpose(1, 0, 2)


def _unheads(zh):
    return zh.transpose(1, 0, 2).reshape(zh.shape[1], -1)


def _pad_rows(a, n):
    return jnp.pad(a, ((0, 0), (n, n), (0, 0))) if a.ndim == 3 else jnp.pad(a, ((n, n), (0, 0)))


def _local_step(x, ctx, target, mods, wts, fetch, emit):
    t, d = x.shape
    l_ctx = ctx.shape[0]
    tc = t + l_ctx
    norm_g = wts["norm_g"]
    ng = lambda l, k: norm_g[l, k][None, :]
    grads = {}
    dmods = [[[None] * N_MOD for _ in range(2)] for _ in range(2)]
    dnorm = [[None] * 3 for _ in range(2)]

    def ffn_fwd(z, h, l, kbase, wi, wo, n_x, tag, nxt):
        au, act = _ffn_up(h, wi, 0, f"ffn_up_{tag}")
        wo = wo(act) if callable(wo) else wo
        outs = _mm_resid(act, wo, 0, z, mods[l], kbase + 2, 0.5, n_x, f"ffn_down_{tag}", nxt=nxt)
        return outs[0], (z, h, au, act, outs[1], wi, wo), (outs[2] if nxt is not None else None)

    def ffn_bwd(dz_new, saved, l, kbase, g, n_x, tag, stage, split=False):
        z, h, au, act, y, wi, wo = saved
        dy, acc_gate = _resid_bwd(dz_new, y, mods[l], kbase + 2, 0.5, n_x, f"resid_bwd_{tag}")
        dau = _ffn_down_bwd(dy, wo, 0, au, f"ffn_down_bwd_{tag}")
        dwo = _mm_tn(act, dy, BF16, f"dwo_{tag}")
        if split:
            token = emit(stage, [dwo])
            dwi_t = _mm_tn(dau, h, BF16, f"dwi_{tag}", dep=token)
            token = emit(stage + 1, [dwi_t])
        else:
            dwi_t = _mm_tn(dau, h, BF16, f"dwi_{tag}")
            token = emit(stage, [dwi_t, dwo])
        dh = _mm([(dau, wi, 0, 0)], NN, d, F32, f"dh_{tag}", tm_pref=ROW_TILES, dep=token)
        dz, acc_mod = _modulate_bwd(z, dh, dz_new, mods[l], g, kbase + 1, n_x, f"mod_bwd_{tag}")
        return dz, acc_mod, acc_gate

    def record(l, kbase, k_norm, g, acc_mod, acc_gate, streams):
        total = None
        for s in range(streams):
            dmods[l][s][kbase] = acc_mod[s, 0]
            dmods[l][s][kbase + 1] = acc_mod[s, 1] * g[0]
            if acc_gate is not None:
                dmods[l][s][kbase + 2] = acc_gate[s, 0]
            part = acc_mod[s, 1] * (1.0 + mods[l][s, kbase + 1])
            total = part if total is None else total + part
        dnorm[l][k_norm] = total

    xc0 = jnp.concatenate([x, ctx], axis=0)
    wi1_0 = fetch(0, None)["wi1_0"]
    h0 = _modulate(xc0, mods[0], ng(0, 0), 0, 1, t, BF16, "mod_l0f1")
    xc1, sv_f1, hc = ffn_fwd(xc0, h0, 0, 0, wi1_0, lambda act: fetch(1, act)["wo1_0"], t, "l0f1",
                             (mods[0], ng(0, 1), 3, 4, BF16))
    w_in_t = fetch(2, hc)["w_in_t"]
    n_proj = w_in_t.shape[1]
    zall = _mm([(hc, w_in_t, 0, 0)], NT, n_proj, F32, "proj", tm_pref=ROW_TILES,
               tn_pref=(n_proj,))
    offs = np.cumsum((0,) + PROJ_SIZES)
    part = lambda i, rows=slice(None): zall[rows, offs[i]:offs[i + 1]]
    lat, con = slice(0, t), slice(t, tc)
    cos, sin = _rope_tables(t)
    qa = _heads(_rope(part(0, lat), cos, sin, False, "rope_q"), A_HEADS)
    ka = _heads(_rope(part(1, lat), cos, sin, False, "rope_k"), A_KV_HEADS)
    va = _heads(part(2, lat), A_KV_HEADS)
    kca, vca = _heads(part(1, con), A_KV_HEADS), _heads(part(2, con), A_KV_HEADS)
    kap, vap = _pad_rows(ka, WINDOW), _pad_rows(va, WINDOW)
    sink = wts["sink"].reshape(A_HEADS, 1, 1)
    o_a = _attn_fwd(qa, kap, vap, kca, vca, sink, "attn_fwd")

    qb, kb, vb = part(3), part(4), part(5)
    rb = part(6, lat)
    zg = part(7)
    zg_f, zg_b = zg[:, :B_GATE_RANK], zg[:, B_GATE_RANK:]
    w2f, w2b, b2f, b2b = wts["w_a2_f"], wts["w_a2_b"], wts["b_a_f"], wts["b_a_b"]
    la_f = _gate_fwd(zg_f, w2f, b2f, "gate_f")
    la_b = _gate_fwd(zg_b, w2b, b2b, "gate_b")
    o_f, s_f, o_b, s_b = _gla_fwd(qb, kb, vb, la_f, la_b, t, "gla_fwd")
    gla_g = wts["gla_g"]
    go = _gla_out_fwd(o_f, o_b, rb, gla_g, "gla_out")
    cat = jnp.concatenate([_unheads(o_a), go], axis=-1).astype(BF16)
    x1 = xc1[:t]
    big = fetch(3, cat)
    w_out, wi2_0, wo2_0 = big["w_out"], big["wi2_0"], big["wo2_0"]
    x2, y_mix0, h2 = _mm_resid(cat, w_out, 0, x1, mods[0], 5, 1.0, t, "w_out", nxt=(mods[0], ng(0, 2), 6, 7, BF16))
    x3, sv_f2, h3 = ffn_fwd(x2, h2, 0, 6, wi2_0, wo2_0, t, "l0f2", (mods[1], ng(1, 0), 0, 1, BF16))

    big = fetch(4, x3)
    wi1_1, wo1_1, wi2_1, wo2_1 = big["wi1_1"], big["wo1_1"], big["wi2_1"], big["wo2_1"]
    x4, sv_g1, hp = ffn_fwd(x3, h3, 1, 0, wi1_1, wo1_1, t, "l1f1", (mods[1], ng(1, 1), 3, 4, F32))
    w_pool, pool_scale = wts["w_pool"], wts["pool_scale"]
    x5, pooled, ypre = _pool_fwd(_pad_rows(hp, POOL_PAD), w_pool, pool_scale, x4, mods[1], 5, "pool_fwd")
    h5 = _modulate(x5, mods[1], ng(1, 2), 6, 7, t, BF16, "mod_l1f2")
    x6, sv_g2, _ = ffn_fwd(x5, h5, 1, 6, wi2_1, wo2_1, t, "l1f2", None)

    dx6, loss_vec, dfinal_g = _final_loss(x6, wts["final_g"], target, "final_loss")
    grads["final_g"] = dfinal_g[0]

    dx5, acc_mod, acc_gate = ffn_bwd(dx6, sv_g2, 1, 6, ng(1, 2), t, "l1f2", 0)
    record(1, 6, 2, ng(1, 2), acc_mod, acc_gate, 1)
    dhp, dw_pool, acc_pool = _pool_bwd(_pad_rows(dx5, POOL_PAD), w_pool, pool_scale, pooled, ypre, mods[1], 5,
                                       "pool_bwd")
    grads["pool_scale"] = acc_pool[0]
    dmods[1][0][5] = acc_pool[1]
    dx4, acc_mod = _modulate_bwd(x4, dhp, dx5, mods[1], ng(1, 1), 4, t, "mod_bwd_l1mix")
    record(1, 3, 1, ng(1, 1), acc_mod, None, 1)
    dx3, acc_mod, acc_gate = ffn_bwd(dx4, sv_g1, 1, 0, ng(1, 0), t, "l1f1", 1)
    record(1, 0, 0, ng(1, 0), acc_mod, acc_gate, 1)

    dx2, acc_mod, acc_gate = ffn_bwd(dx3, sv_f2, 0, 6, ng(0, 2), t, "l0f2", 2)
    record(0, 6, 2, ng(0, 2), acc_mod, acc_gate, 1)
    dymix, acc_gate = _resid_bwd(dx2, y_mix0, mods[0], 5, 1.0, t, "resid_bwd_mix")
    dmods[0][0][5] = acc_gate[0, 0]
    dw_out = _mm_tn(cat, dymix, BF16, "dw_out")
    dcat = _mm([(dymix, w_out, 0, 0)], NT, cat.shape[1], F32, "dcat")
    do_a = _heads(dcat[:, :A_Q], A_HEADS)
    do_full, drb, dgla_g = _gla_out_bwd(o_f, o_b, rb, gla_g, dcat[:, A_Q:], "gla_out_bwd")
    grads["gla_g"] = dgla_g[0]
    dq_f, dk_f, dv_f, dla_f, dq_b, dk_b, dv_b, dla_b = _gla_bwd(qb, kb, vb, la_f, la_b, s_f, s_b, do_full, t, "gla_bwd")
    dzg_f, dw2f, db2f = _gate_bwd(zg_f, w2f, b2f, dla_f, "gate_bwd_f")
    dzg_b, dw2b, db2b = _gate_bwd(zg_b, w2b, b2b, dla_b, "gate_bwd_b")
    grads.update(w_a2_f=dw2f, w_a2_b=dw2b, b_a_f=db2f[0], b_a_b=db2b[0])
    dqa_r, dkap, dvap, dkca, dvca, dsink = _attn_bwd(qa, kap, vap, kca, vca, sink, o_a, do_a, "attn_bwd")
    grads["sink"] = dsink[:, 0, 0]
    dqa = _rope(_unheads(dqa_r), cos, sin, True, "rope_bwd_q")
    dka = _rope(_unheads(dkap[:, WINDOW:WINDOW + t]), cos, sin, True, "rope_bwd_k")
    dva = dvap[:, WINDOW:WINDOW + t]
    zrow = lambda a, n: jnp.pad(a, ((0, n), (0, 0)))
    dz_parts = [
        zrow(dqa, l_ctx),
        jnp.concatenate([dka, _unheads(dkca)], axis=0),
        jnp.concatenate([_unheads(dva), _unheads(dvca)], axis=0),
        dq_f + dq_b, dk_f + dk_b, dv_f + dv_b,
        zrow(drb, l_ctx),
        jnp.concatenate([dzg_f, dzg_b], axis=-1),
        jnp.zeros((tc, n_proj - PROJ_DIM), F32),
    ]
    dzall = jnp.concatenate(dz_parts, axis=-1).astype(BF16)
    dw_in_t = _mm_tn(dzall, hc, BF16, "dw_in")
    token = emit(3, [dw_in_t, dw_out, dw_pool])
    dhc = _mm([(dzall, w_in_t, 0, 0)], NN, d, F32, "dhc", tm_pref=ROW_TILES, dep=token)
    dxc1_res = jnp.concatenate([dx2, jnp.zeros((l_ctx, d), F32)], axis=0)
    dxc1, acc_mod = _modulate_bwd(xc1, dhc, dxc1_res, mods[0], ng(0, 1), 4, t, "mod_bwd_l0mix")
    record(0, 3, 1, ng(0, 1), acc_mod, None, 2)
    dxc0, acc_mod, acc_gate = ffn_bwd(dxc1, sv_f1, 0, 0, ng(0, 0), t, "l0f1", 4, split=True)
    record(0, 0, 0, ng(0, 0), acc_mod, acc_gate, 2)

    grads["norm_g"] = jnp.stack([jnp.stack(dnorm[0]), jnp.stack(dnorm[1])])
    zero = jnp.zeros((d,), F32)
    dmods_arr = jnp.stack([jnp.stack([jnp.stack([v if v is not None else zero for v in dmods[l][s]])
                                      for s in range(2)]) for l in range(2)])
    return loss_vec, dxc0[:t], grads, dmods_arr


def _pack(parts):
    flat = jnp.concatenate([p.reshape(-1).astype(F32) for p in parts])
    pad = (-flat.shape[0]) % 128
    return jnp.pad(flat, (0, pad))[None, :]


def _unpack(rows, shapes):
    out, off = [], 0
    for s in shapes:
        n = int(np.prod(s))
        out.append(rows[:, off:off + n].reshape((rows.shape[0],) + tuple(s)))
        off += n
    return out


def _cols_to_full(g):
    g = jnp.moveaxis(g, 0, -2)
    return g.reshape(g.shape[:-2] + (-1,))


def kernel(x, c, ctx, c_ctx, w_mod, b_mod, norm_g, ffn1_wi, ffn1_wo, ffn2_wi, ffn2_wo, w_in, w_a2_f, b_a_f, w_a2_b, b_a_b, sink, gla_g, w_out, w_pool, pool_scale, final_g, loss_target, m_c_ctx, m_w_mod, m_b_mod, m_norm_g, m_ffn1_wi, m_ffn1_wo, m_ffn2_wi, m_ffn2_wo, m_w_in, m_w_a2_f, m_b_a_f, m_w_a2_b, m_b_a_b, m_sink, m_gla_g, m_w_out, m_w_pool, m_pool_scale, m_final_g, v_c_ctx, v_w_mod, v_b_mod, v_norm_g, v_ffn1_wi, v_ffn1_wo, v_ffn2_wi, v_ffn2_wo, v_w_in, v_w_a2_f, v_b_a_f, v_w_a2_b, v_b_a_b, v_sink, v_gla_g, v_w_out, v_w_pool, v_pool_scale, v_final_g):
    t, d = x.shape[1], x.shape[2]
    me = _dev_index()
    nc = w_mod.shape[2]
    ncol_in = w_in.shape[2]
    ncol_pad = -(-ncol_in // 16) * 16

    small_shapes = [(d,), norm_g.shape, pool_scale.shape, w_a2_f.shape, w_a2_b.shape, w_pool.shape]
    g1 = _gather_small(_pack([c, norm_g, pool_scale, w_a2_f, w_a2_b, w_pool]), "gather_params")
    c_all, norm_g_all, pool_scale_all, w2f_all, w2b_all, w_pool_all = _unpack(g1, small_shapes)
    wts = {
        "norm_g": _cols_to_full(norm_g_all),
        "pool_scale": _cols_to_full(pool_scale_all),
        "w_a2_f": _cols_to_full(w2f_all)[0],
        "w_a2_b": _cols_to_full(w2b_all)[0],
        "w_pool": jnp.moveaxis(w_pool_all[:, 0], 0, 1).reshape(w_pool.shape[1], -1, w_pool.shape[3]),
        "b_a_f": b_a_f, "b_a_b": b_a_b, "sink": sink[0], "gla_g": gla_g, "final_g": final_g[None, :],
    }

    craw = jnp.concatenate([c_all, c_ctx[None, :], jnp.zeros((16 - N_DEV - 1, d), F32)], axis=0)
    b_cols = lax.dynamic_slice_in_dim(b_mod, me * nc, nc, axis=1)[:, None, :]
    mm_cols = _adaln_fwd(craw, w_mod, b_cols, "adaln_fwd")
    g2 = _gather_small(mm_cols.reshape(1, -1), "gather_mods").reshape(N_DEV, 2, 16, nc)
    mm_full = jnp.moveaxis(g2, 0, 2).reshape(2, 16, N_MOD, d)
    mods = jnp.stack([lax.dynamic_index_in_dim(mm_full, me, axis=1, keepdims=False), mm_full[:, N_DEV]], axis=1)
    mods = jnp.pad(mods, ((0, 0), (0, 0), (0, 16 - N_MOD), (0, 0)))

    tr = lambda w: jnp.swapaxes(w, 1, 2).astype(BF16)
    wi1_sh, wi2_sh, wo1_sh, wo2_sh = tr(ffn1_wi), tr(ffn2_wi), ffn1_wo.astype(BF16), ffn2_wo.astype(BF16)
    w_in_sh = jnp.pad(tr(w_in), ((0, 0), (0, ncol_pad - ncol_in), (0, 0)))
    groups = [
        {"wi1_0": wi1_sh[0:1]},
        {"wo1_0": wo1_sh[0:1]},
        {"w_in": w_in_sh},
        {"w_out": w_out.astype(BF16), "wi2_0": wi2_sh[0:1], "wo2_0": wo2_sh[0:1]},
        {"wi1_1": wi1_sh[1:2], "wo1_1": wo1_sh[1:2], "wi2_1": wi2_sh[1:2], "wo2_1": wo2_sh[1:2]},
    ]

    gathers, token = [], None
    for gi, grp in enumerate(groups):
        shards = list(grp.values())
        gathers.append(_exchange_start(shards, _place_shards(shards, f"gather_place_{gi}"), True, 1 + gi, token,
                                       f"gather_start_{gi}"))
        token = gathers[-1][4]
    n_proj = -(-(N_DEV * ncol_in) // 128) * 128

    def fetch(gi, after):
        _, lands = _exchange_wait(gathers[gi], True, token if after is None else after, f"gather_wait_{gi}")
        out = dict(zip(groups[gi].keys(), lands))
        if "w_in" in out:
            w_in_t = out.pop("w_in").reshape(1, N_DEV, ncol_pad, d)[:, :, :ncol_in].reshape(1, N_DEV * ncol_in, d)
            out["w_in_t"] = jnp.pad(w_in_t, ((0, 0), (0, n_proj - N_DEV * ncol_in), (0, 0)))
        return out

    scatters = []

    def emit(stage, arrays):
        if stage == 3:
            dw_in_t, dw_out, dw_pool = arrays
            dw_in_full = dw_in_t[:N_DEV * ncol_in].reshape(N_DEV, ncol_in, d)
            dw_in_full = jnp.pad(dw_in_full, ((0, 0), (0, ncol_pad - ncol_in), (0, 0)))
            srcs = [dw_in_full.reshape(1, N_DEV * ncol_pad, d), dw_out[None], dw_pool.astype(BF16)]
        else:
            srcs = [a[None] for a in arrays]
        lands = [lax.empty((N_DEV, s.shape[0], s.shape[1] // N_DEV, s.shape[2]), s.dtype) for s in srcs]
        scatters.append(_exchange_start(srcs, lands, False, 1 + len(groups) + stage, None, f"scatter_start_{stage}"))
        return scatters[-1][4]

    loss_vec, grad_x, grads, dmods = _local_step(x[0], ctx[0], loss_target[0], mods, wts, fetch, emit)
    loss = lax.psum(jnp.sum(loss_vec), ("x", "y", "c"))

    def reduce_stage(stage, after):
        wholes, lands = _exchange_wait(scatters[stage], False, after, f"scatter_wait_{stage}")
        return [_sum_slots(ld, wh, me, f"sum_grad_{stage}_{i}") for i, (ld, wh) in enumerate(zip(lands, wholes))]

    (dwi2_1, dwo2_1), (dwi1_1, dwo1_1), (dwi2_0, dwo2_0), (dw_in_s, dw_out_s, dw_pool_s) = [
        reduce_stage(stage, grad_x) for stage in range(4)]
    back = lambda g: jnp.swapaxes(g, 1, 2)
    g_big = {
        "ffn2_wi": back(jnp.concatenate([dwi2_0, dwi2_1], axis=0)), "ffn2_wo": jnp.concatenate([dwo2_0, dwo2_1], axis=0),
        "w_in": back(dw_in_s[:, :ncol_in]), "w_out": dw_out_s, "w_pool": dw_pool_s[None],
    }

    order = ["c_ctx", "w_mod", "b_mod", "norm_g", "ffn1_wi", "ffn1_wo", "ffn2_wi", "ffn2_wo", "w_in", "w_a2_f", "b_a_f",
             "w_a2_b", "b_a_b", "sink", "gla_g", "w_out", "w_pool", "pool_scale", "final_g"]
    ws = dict(c_ctx=c_ctx, w_mod=w_mod, b_mod=b_mod, norm_g=norm_g, ffn1_wi=ffn1_wi, ffn1_wo=ffn1_wo, ffn2_wi=ffn2_wi,
              ffn2_wo=ffn2_wo, w_in=w_in, w_a2_f=w_a2_f, b_a_f=b_a_f, w_a2_b=w_a2_b, b_a_b=b_a_b, sink=sink, gla_g=gla_g,
              w_out=w_out, w_pool=w_pool, pool_scale=pool_scale, final_g=final_g)
    ms = dict(c_ctx=m_c_ctx, w_mod=m_w_mod, b_mod=m_b_mod, norm_g=m_norm_g, ffn1_wi=m_ffn1_wi, ffn1_wo=m_ffn1_wo,
              ffn2_wi=m_ffn2_wi, ffn2_wo=m_ffn2_wo, w_in=m_w_in, w_a2_f=m_w_a2_f, b_a_f=m_b_a_f, w_a2_b=m_w_a2_b,
              b_a_b=m_b_a_b, sink=m_sink, gla_g=m_gla_g, w_out=m_w_out, w_pool=m_w_pool, pool_scale=m_pool_scale,
              final_g=m_final_g)
    vs = dict(c_ctx=v_c_ctx, w_mod=v_w_mod, b_mod=v_b_mod, norm_g=v_norm_g, ffn1_wi=v_ffn1_wi, ffn1_wo=v_ffn1_wo,
              ffn2_wi=v_ffn2_wi, ffn2_wo=v_ffn2_wo, w_in=v_w_in, w_a2_f=v_w_a2_f, b_a_f=v_b_a_f, w_a2_b=v_w_a2_b,
              b_a_b=v_b_a_b, sink=v_sink, gla_g=v_gla_g, w_out=v_w_out, w_pool=v_w_pool, pool_scale=v_pool_scale,
              final_g=v_final_g)
    big = ["ffn2_wi", "ffn2_wo", "w_out", "w_mod", "ffn1_wi", "ffn1_wo"]
    delta, new_m, new_v = {}, {}, {}
    g_all = dict(g_big)

    def adamw_big(nm):
        shp = ws[nm].shape
        two_d = lambda a: a.reshape(-1, shp[-1])
        dl, nm_, nv_ = _adamw(two_d(ws[nm]), two_d(g_all[nm]), two_d(ms[nm]), two_d(vs[nm]), f"adamw_{nm}")
        delta[nm], new_m[nm], new_v[nm] = dl.reshape(shp), nm_.reshape(shp), nv_.reshape(shp)

    for nm in big[:3]:
        adamw_big(nm)

    small_g = [dmods[:, :, :N_MOD].reshape(2, 2, N_MOD * d), grads["norm_g"], grads["pool_scale"], grads["final_g"],
               grads["b_a_f"], grads["b_a_b"], grads["sink"], grads["gla_g"], grads["w_a2_f"], grads["w_a2_b"]]
    small_g_shapes = [a.shape for a in small_g]
    g3 = _gather_small(_pack(small_g), "gather_small_grads", dep=delta["w_out"])
    total = _sum_rows8(g3, "sum_small_grads")
    dmm_all = _unpack(g3, small_g_shapes[:1])[0]
    (dmm_sum, dnorm_g, dpool_scale, dfinal_g, db_a_f, db_a_b, dsink, dgla_g, dw_a2_f, dw_a2_b) = [
        a[0] for a in _unpack(total, small_g_shapes)]
    dmm_rows = jnp.concatenate([dmm_all[:, :, 0].transpose(1, 0, 2), dmm_sum[:, 1][:, None, :],
                                jnp.zeros((2, 16 - N_DEV - 1, N_MOD * d), F32)], axis=1)
    grad_b_mod = dmm_sum[:, 0] + dmm_sum[:, 1]
    dmm_cols = lax.dynamic_slice_in_dim(dmm_rows, me * nc, nc, axis=2)
    cs_t = jnp.transpose(_silu(craw)).astype(BF16)
    grad_w_mod, dcraw = _adaln_bwd(craw, cs_t, dmm_cols, w_mod, "adaln_bwd")
    g4 = _gather_small((dcraw[0, N_DEV] + dcraw[1, N_DEV])[None, :], "gather_c_ctx_grad")
    grad_c_ctx = _sum_rows8(g4, "sum_c_ctx_grad")[0]

    col = lambda v, n: lax.dynamic_slice_in_dim(v, me * n, n, axis=v.ndim - 1)
    g_small = {
        "c_ctx": grad_c_ctx, "b_mod": grad_b_mod, "norm_g": col(dnorm_g, norm_g.shape[2]),
        "w_a2_f": col(dw_a2_f, w_a2_f.shape[2])[None], "b_a_f": db_a_f[None], "w_a2_b": col(dw_a2_b, w_a2_b.shape[2])[None],
        "b_a_b": db_a_b[None], "sink": dsink[None], "gla_g": dgla_g[None], "pool_scale": col(dpool_scale, pool_scale.shape[1])[None],
        "final_g": dfinal_g,
    }
    g_all.update(g_small, w_mod=grad_w_mod)
    adamw_big("w_mod")
    rest = [nm for nm in order if nm not in big]
    rest_shapes = [ws[nm].shape for nm in rest]
    packed = [_pack([d_[nm].reshape(ws[nm].shape) for nm in rest]).reshape(-1, 128) for d_ in (ws, g_all, ms, vs)]
    pad_rows = (-packed[0].shape[0]) % 512
    packed = [jnp.pad(p, ((0, pad_rows), (0, 0))) for p in packed]
    outs = _adamw(*packed, "adamw_small")
    for dst, arr in zip((delta, new_m, new_v), outs):
        for nm, val in zip(rest, _unpack(arr.reshape(1, -1), rest_shapes)):
            dst[nm] = val[0]

    (dwo1_0,), (dwi1_0,) = reduce_stage(4, outs[0]), reduce_stage(5, outs[0])
    g_all["ffn1_wi"] = back(jnp.concatenate([dwi1_0, dwi1_1], axis=0))
    g_all["ffn1_wo"] = jnp.concatenate([dwo1_0, dwo1_1], axis=0)
    for nm in big[4:]:
        adamw_big(nm)
    g_all = {nm: g_all[nm].reshape(ws[nm].shape) for nm in order}

    return (loss, grad_x[None], *[g_all[nm] for nm in order], *[delta[nm] for nm in order],
            *[new_m[nm] for nm in order], *[new_v[nm] for nm in order])
```

```python
import functools

import numpy as np
import jax
import jax.numpy as jnp
from jax import lax
from jax.experimental import pallas as pl
from jax.experimental.pallas import tpu as pltpu

F32 = jnp.float32
BF16 = jnp.bfloat16
MESH = pl.DeviceIdType.MESH

N_DEV = 8
RMS_EPS = 1e-6
N_MOD = 9
GRID_W = 64
A_HEADS, A_KV_HEADS, A_HEAD_DIM = 8, 2, 64
A_REP = A_HEADS // A_KV_HEADS
WINDOW = 128
ROPE_BASE = 10000.0
B_HEADS, B_DK, B_DV = 4, 64, 128
B_GATE_RANK = 16
B_GATE_NORM = 16.0
B_CHUNK = 64
POOL_WINDOWS = (2, 4, 8, 16)
POOL_PAD = 8
A_Q = A_HEADS * A_HEAD_DIM
A_KV = A_KV_HEADS * A_HEAD_DIM
B_QK = B_HEADS * B_DK
B_V = B_HEADS * B_DV
PROJ_SIZES = (A_Q, A_KV, A_KV, B_QK, B_QK, B_V, B_V, 2 * B_GATE_RANK)
PROJ_DIM = sum(PROJ_SIZES)
ADAM_LR, ADAM_B1, ADAM_B2, ADAM_EPS, ADAM_WD, ADAM_STEP = 0.001, 0.9, 0.999, 1e-08, 0.01, 10

VMEM_LIMIT = 56 * 1024 * 1024
ROW_TILES = (512, 544, 256, 128, 64, 32, 16, 8)

NN = ((1,), (0,))
NT = ((1,), (1,))
TN = ((0,), (0,))


def _dot(a, b, dims=NN, prec=None):
    return lax.dot_general(a, b, (dims, ((), ())), precision=prec, preferred_element_type=F32)


def _bdot(a, b, dims=NN):
    return _dot(a.astype(BF16), b.astype(BF16), dims)


def _dot_01(sel, x):
    hi = x.astype(BF16)
    rest = x - hi.astype(F32)
    mid = rest.astype(BF16)
    lo = (rest - mid.astype(F32)).astype(BF16)
    sel = sel.astype(BF16)
    return _dot(sel, hi) + _dot(sel, mid) + _dot(sel, lo)


def _params(sem=None, **kw):
    return pltpu.CompilerParams(dimension_semantics=sem, vmem_limit_bytes=VMEM_LIMIT, **kw)


def _silu(a):
    return a * jax.nn.sigmoid(a)


def _pick(n, prefs):
    for p in prefs:
        if n % p == 0:
            return p
    return n


def _full(shape):
    nd = len(shape)
    return pl.BlockSpec(shape, lambda *_: (0,) * nd)


def _peers():
    x, y, c = lax.axis_index("x"), lax.axis_index("y"), lax.axis_index("c")
    return x, y, c


def _dev_index():
    x, y, c = _peers()
    return 4 * x + 2 * y + c


def _others(x, y, c):
    return [(x, y, 1 - c), (1 - x, y, c), (x, 1 - y, c), (1 - x, 1 - y, c),
            (1 - x, y, 1 - c), (x, 1 - y, 1 - c), (1 - x, 1 - y, 1 - c)]


def _index_of(dev):
    return 4 * dev[0] + 2 * dev[1] + dev[2]


def _exchange_refs(gather, shapes, srcs, lands, a, me, to):
    if gather:
        r = shapes[a][1]
        return srcs[a], lands[a].at[:, pl.ds(_index_of(me) * r, r), :]
    r = shapes[a][1] // N_DEV
    return srcs[a].at[:, pl.ds(_index_of(to) * r, r), :], lands[a].at[_index_of(me)]


HBM_SPEC = pl.BlockSpec(memory_space=pltpu.HBM)
SEM_SPEC = pl.BlockSpec(memory_space=pltpu.SEMAPHORE)
EFFECT = pltpu.SideEffectType.DATAFLOW_SIDE_EFFECTING


def _exchange_start(srcs, lands, gather, collective_id, dep, name):
    n = len(srcs)
    shapes = [s.shape for s in srcs]
    deps = [] if dep is None else [dep]

    def body(*refs):
        src_refs, land_refs = refs[:n], refs[n:2 * n]
        send_sems, recv_sems = refs[2 * n + len(deps)], refs[2 * n + len(deps) + 1]
        token = refs[-1]
        x, y, c = _peers()
        others = _others(x, y, c)
        barrier = pltpu.get_barrier_semaphore()
        for peer in others:
            pl.semaphore_signal(barrier, inc=1, device_id=peer, device_id_type=MESH)
        pl.semaphore_wait(barrier, len(others))
        for a in range(n):
            for k, to in enumerate(others):
                src, dst = _exchange_refs(gather, shapes, src_refs, land_refs, a, (x, y, c), to)
                pltpu.make_async_remote_copy(src_ref=src, dst_ref=dst, send_sem=send_sems.at[7 * a + k],
                                             recv_sem=recv_sems.at[7 * a + k], device_id=to, device_id_type=MESH).start()
        token[...] = jnp.zeros_like(token)

    outs = pl.pallas_call(
        body, name=name,
        out_shape=(pltpu.SemaphoreType.DMA((7 * n,)), pltpu.SemaphoreType.DMA((7 * n,)),
                   *[pltpu.HBM(s.shape, s.dtype) for s in srcs], *[pltpu.HBM(l.shape, l.dtype) for l in lands],
                   jax.ShapeDtypeStruct((8, 128), F32)),
        in_specs=[HBM_SPEC] * (2 * n) + [pl.BlockSpec(memory_space=pl.ANY)] * len(deps),
        out_specs=(SEM_SPEC, SEM_SPEC, *[HBM_SPEC] * (2 * n), pl.BlockSpec(memory_space=pltpu.VMEM)),
        input_output_aliases={i: 2 + i for i in range(2 * n)},
        compiler_params=pltpu.CompilerParams(has_side_effects=EFFECT, collective_id=collective_id),
    )(*[pltpu.with_memory_space_constraint(s, pltpu.HBM) for s in srcs],
      *[pltpu.with_memory_space_constraint(l, pltpu.HBM) for l in lands], *deps)
    return outs[0], outs[1], list(outs[2:2 + n]), list(outs[2 + n:2 + 2 * n]), outs[-1]


def _exchange_wait(started, gather, after, name):
    send_sems, recv_sems, srcs, lands, _ = started
    n = len(srcs)
    shapes = [s.shape for s in srcs]

    def body(*refs):
        src_refs, land_refs = refs[:n], refs[n:2 * n]
        send_sems, recv_sems = refs[2 * n], refs[2 * n + 1]
        x, y, c = _peers()
        for a in range(n):
            for k, peer in enumerate(_others(x, y, c)):
                src, _ = _exchange_refs(gather, shapes, src_refs, land_refs, a, (x, y, c), peer)
                _, dst = _exchange_refs(gather, shapes, src_refs, land_refs, a, peer, (x, y, c))
                copy = pltpu.make_async_remote_copy(src_ref=src, dst_ref=dst, send_sem=send_sems.at[7 * a + k],
                                                    recv_sem=recv_sems.at[7 * a + k], device_id=peer, device_id_type=MESH)
                copy.wait_send()
                copy.wait_recv()

    outs = pl.pallas_call(
        body, name=name,
        out_shape=(*[pltpu.HBM(s.shape, s.dtype) for s in srcs], *[pltpu.HBM(l.shape, l.dtype) for l in lands]),
        in_specs=[HBM_SPEC] * (2 * n) + [SEM_SPEC, SEM_SPEC, pl.BlockSpec(memory_space=pl.ANY)],
        out_specs=tuple([HBM_SPEC] * (2 * n)),
        input_output_aliases={i: i for i in range(2 * n)},
        compiler_params=pltpu.CompilerParams(has_side_effects=EFFECT),
    )(*srcs, *lands, send_sems, recv_sems, after)
    return list(outs[:n]), list(outs[n:])


def _place_shards(shards, name):
    n = len(shards)

    def body(*refs):
        ins, outs, sems = refs[:n], refs[n:2 * n], refs[2 * n]
        me = _index_of(_peers())
        copies = []
        for a in range(n):
            r = shards[a].shape[1]
            copies.append(pltpu.make_async_copy(ins[a], outs[a].at[:, pl.ds(me * r, r), :], sems.at[a]))
            copies[-1].start()
        for cp in copies:
            cp.wait()

    any_spec = pl.BlockSpec(memory_space=pl.ANY)
    return pl.pallas_call(
        body, name=name, in_specs=[any_spec] * n, out_specs=[any_spec] * n,
        out_shape=[jax.ShapeDtypeStruct((s.shape[0], N_DEV * s.shape[1], s.shape[2]), s.dtype) for s in shards],
        scratch_shapes=[pltpu.SemaphoreType.DMA((n,))],
    )(*shards)


def _sum_slots(land, whole, me, name):
    _, a_, r, c = land.shape
    tr = _pick(r, (352, 256, 128, 64, 32, 16, 8))
    nr = r // tr

    def body(me_ref, land_ref, own_ref, out_ref):
        acc = None
        for s in range(N_DEV):
            part = jnp.where(me_ref[0] == s, own_ref[...], land_ref[s]).astype(F32)
            acc = part if acc is None else acc + part
        out_ref[...] = acc

    return pl.pallas_call(
        body, name=name,
        grid_spec=pltpu.PrefetchScalarGridSpec(
            num_scalar_prefetch=1, grid=(a_, nr),
            in_specs=[pl.BlockSpec((N_DEV, None, tr, c), lambda i, j, me_ref: (0, i, j, 0)),
                      pl.BlockSpec((None, tr, c), lambda i, j, me_ref: (i, me_ref[0] * nr + j, 0))],
            out_specs=pl.BlockSpec((None, tr, c), lambda i, j, me_ref: (i, j, 0))),
        out_shape=jax.ShapeDtypeStruct((a_, r, c), F32),
        compiler_params=_params(("parallel", "parallel")),
    )(me.reshape(1).astype(jnp.int32), land, whole)


def _gather_small(vec, name, dep=None):
    p = vec.shape[1]
    pp = -(-p // 1024) * 1024
    blk = jnp.pad(vec, ((0, 0), (0, pp - p))).reshape(8, pp // 8)
    deps = [] if dep is None else [dep]

    def body(in_ref, *rest):
        out_ref, send_sems, recv_sems = rest[-3:]
        x, y, c = _peers()
        me = 4 * x + 2 * y + c
        others = [(x, y, 1 - c), (1 - x, y, c), (x, 1 - y, c), (1 - x, 1 - y, c),
                  (1 - x, y, 1 - c), (x, 1 - y, 1 - c), (1 - x, 1 - y, 1 - c)]

        def rows(idx):
            return out_ref.at[pl.ds(pl.multiple_of(idx * 8, 8), 8), :]

        out_ref[pl.ds(pl.multiple_of(me * 8, 8), 8), :] = in_ref[...]

        def copy(k, dev, slot):
            return pltpu.make_async_remote_copy(
                src_ref=in_ref, dst_ref=rows(slot), send_sem=send_sems.at[k], recv_sem=recv_sems.at[k],
                device_id=dev, device_id_type=MESH)

        sends = [copy(k, dev, me) for k, dev in enumerate(others)]
        for cp in sends:
            cp.start()
        for k, dev in enumerate(others):
            copy(k, dev, 4 * dev[0] + 2 * dev[1] + dev[2]).wait_recv()
        for cp in sends:
            cp.wait_send()

    vm = pl.BlockSpec(memory_space=pltpu.VMEM)
    out = pl.pallas_call(
        body, name=name, out_shape=jax.ShapeDtypeStruct((8 * N_DEV, pp // 8), F32),
        in_specs=[vm] + [pl.BlockSpec(memory_space=pl.ANY)] * len(deps), out_specs=vm,
        scratch_shapes=[pltpu.SemaphoreType.DMA((7,)), pltpu.SemaphoreType.DMA((7,))],
        compiler_params=pltpu.CompilerParams(has_side_effects=True, vmem_limit_bytes=VMEM_LIMIT),
    )(blk, *deps)
    return out.reshape(N_DEV, pp)[:, :p]


def _sum_rows8(g, name):
    p = g.shape[1]

    def body(in_ref, out_ref):
        acc = in_ref[0:1, :]
        for s in range(1, N_DEV):
            acc = acc + in_ref[s:s + 1, :]
        out_ref[...] = acc

    return pl.pallas_call(body, name=name, out_shape=jax.ShapeDtypeStruct((1, p), F32),
                          compiler_params=_params())(g)


def _sel_row(mods_ref, is_ctx, k):
    return jnp.where(is_ctx, mods_ref[1, k:k + 1, :], mods_ref[0, k:k + 1, :])


def _modulate(z, mods, g, ks, kc, n_x, out_dtype, name):
    m, d = z.shape
    tm = _pick(m, (256, 128, 64, 32, 16, 8))

    def body(z_ref, mods_ref, g_ref, h_ref):
        is_ctx = pl.program_id(0) * tm >= n_x
        zz = z_ref[...]
        r = lax.rsqrt(jnp.mean(zz * zz, axis=-1, keepdims=True) + RMS_EPS)
        shift, scale = _sel_row(mods_ref, is_ctx, ks), _sel_row(mods_ref, is_ctx, kc)
        h_ref[...] = ((zz * r) * g_ref[...] * (1.0 + scale) + shift).astype(out_dtype)

    return pl.pallas_call(
        body, name=name, grid=(m // tm,),
        in_specs=[pl.BlockSpec((tm, d), lambda i: (i, 0)), _full(mods.shape), _full(g.shape)],
        out_specs=pl.BlockSpec((tm, d), lambda i: (i, 0)),
        out_shape=jax.ShapeDtypeStruct((m, d), out_dtype),
        compiler_params=_params(("parallel",)),
    )(z, mods, g)


def _modulate_bwd(z, dh, dres, mods, g, kc, n_x, name):
    m, d = z.shape
    tm = _pick(m, (256, 128, 64, 32, 16, 8))
    first_ctx = n_x // tm
    res_blocks = dres.shape[0] // tm

    def body(z_ref, dh_ref, dres_ref, mods_ref, g_ref, dx_ref, acc_ref):
        i = pl.program_id(0)
        is_ctx = i * tm >= n_x

        @pl.when((i == 0) | (i == first_ctx))
        def _():
            acc_ref[...] = jnp.zeros_like(acc_ref)

        zz, dhh = z_ref[...], dh_ref[...]
        r = lax.rsqrt(jnp.mean(zz * zz, axis=-1, keepdims=True) + RMS_EPS)
        nz = zz * r
        gain = g_ref[...] * (1.0 + _sel_row(mods_ref, is_ctx, kc))
        dn = dhh * gain
        dz = r * (dn - nz * jnp.mean(dn * nz, axis=-1, keepdims=True))
        dx_ref[...] = jnp.where(i < res_blocks, dres_ref[...], 0.0) + dz
        acc_ref[0:1, :] += jnp.sum(dhh, axis=0, keepdims=True)
        acc_ref[1:2, :] += jnp.sum(dhh * nz, axis=0, keepdims=True)

    row = pl.BlockSpec((tm, d), lambda i: (i, 0))
    return pl.pallas_call(
        body, name=name, grid=(m // tm,),
        in_specs=[row, row, pl.BlockSpec((tm, d), lambda i: (jnp.minimum(i, res_blocks - 1), 0)),
                  _full(mods.shape), _full(g.shape)],
        out_specs=[row, pl.BlockSpec((None, 8, d), lambda i: ((i * tm >= n_x).astype(jnp.int32), 0, 0))],
        out_shape=[jax.ShapeDtypeStruct((m, d), F32), jax.ShapeDtypeStruct((2, 8, d), F32)],
        compiler_params=_params(("arbitrary",)),
    )(z, dh, dres, mods, g)


def _ffn_up(h, wi_t, layer, name):
    m, d = h.shape
    f = wi_t.shape[1] // 2
    tm = _pick(m, ROW_TILES)

    def body(h_ref, w_ref, au_ref, act_ref):
        hh = h_ref[...]
        a = _dot(hh, w_ref[0:f, :], NT)
        u = _dot(hh, w_ref[f:2 * f, :], NT)
        au_ref[:, 0:f] = a.astype(BF16)
        au_ref[:, f:2 * f] = u.astype(BF16)
        act_ref[...] = (_silu(a) * u).astype(BF16)

    return pl.pallas_call(
        body, name=name, grid=(m // tm,),
        in_specs=[pl.BlockSpec((tm, d), lambda i: (i, 0)),
                  pl.BlockSpec((None, 2 * f, d), lambda i: (layer, 0, 0))],
        out_specs=[pl.BlockSpec((tm, 2 * f), lambda i: (i, 0)), pl.BlockSpec((tm, f), lambda i: (i, 0))],
        out_shape=[jax.ShapeDtypeStruct((m, 2 * f), BF16), jax.ShapeDtypeStruct((m, f), BF16)],
        compiler_params=_params(("parallel",)),
    )(h, wi_t)


def _mm_resid(a, b, layer, res, mods, km, coef, n_x, name, nxt=None):
    m, k = a.shape
    n = b.shape[2]
    tm = _pick(m, (512, 256, 128, 64, 32, 16, 8))
    tn = n if nxt is not None else _pick(n, (1024, 512, 256, 128))
    extra = [] if nxt is None else [nxt[0], nxt[1]]

    def body(a_ref, b_ref, res_ref, mods_ref, *rest):
        is_ctx = pl.program_id(1) * tm >= n_x
        y = _dot(a_ref[...], b_ref[...])
        new = res_ref[...] + coef * _sel_row(mods_ref, is_ctx, km) * y
        if nxt is None:
            out_ref, y_ref = rest
        else:
            nmods_ref, g_ref, out_ref, y_ref, h_ref = rest
            r = lax.rsqrt(jnp.mean(new * new, axis=-1, keepdims=True) + RMS_EPS)
            shift, scale = _sel_row(nmods_ref, is_ctx, nxt[2]), _sel_row(nmods_ref, is_ctx, nxt[3])
            h_ref[...] = ((new * r) * g_ref[...] * (1.0 + scale) + shift).astype(nxt[4])
        y_ref[...] = y
        out_ref[...] = new

    tile = pl.BlockSpec((tm, tn), lambda j, i: (i, j))
    outs = [jax.ShapeDtypeStruct((m, n), F32), jax.ShapeDtypeStruct((m, n), F32)]
    if nxt is not None:
        outs.append(jax.ShapeDtypeStruct((m, n), nxt[4]))
    return pl.pallas_call(
        body, name=name, grid=(n // tn, m // tm),
        in_specs=[pl.BlockSpec((tm, k), lambda j, i: (i, 0)),
                  pl.BlockSpec((None, k, tn), lambda j, i: (layer, 0, j)),
                  tile, pl.BlockSpec((2, 16, tn), lambda j, i: (0, 0, j))] + [_full(e.shape) for e in extra],
        out_specs=[tile] * len(outs), out_shape=outs,
        compiler_params=_params(("parallel", "parallel")),
    )(a, b, res, mods, *extra)


def _resid_bwd(dx, y, mods, km, coef, n_x, name, dep=None):
    m, d = dx.shape
    tm = _pick(m, (256, 128, 64, 32, 16, 8))
    first_ctx = n_x // tm
    deps = [] if dep is None else [dep]

    def body(dx_ref, y_ref, mods_ref, *rest):
        dy_ref, acc_ref = rest[-2:]
        i = pl.program_id(0)
        is_ctx = i * tm >= n_x

        @pl.when((i == 0) | (i == first_ctx))
        def _():
            acc_ref[...] = jnp.zeros_like(acc_ref)

        dxx = dx_ref[...]
        dy_ref[...] = (coef * _sel_row(mods_ref, is_ctx, km) * dxx).astype(BF16)
        acc_ref[0:1, :] += jnp.sum(coef * y_ref[...] * dxx, axis=0, keepdims=True)

    row = pl.BlockSpec((tm, d), lambda i: (i, 0))
    return pl.pallas_call(
        body, name=name, grid=(m // tm,),
        in_specs=[row, row, _full(mods.shape)] + [pl.BlockSpec(memory_space=pl.ANY)] * len(deps),
        out_specs=[row, pl.BlockSpec((None, 8, d), lambda i: ((i * tm >= n_x).astype(jnp.int32), 0, 0))],
        out_shape=[jax.ShapeDtypeStruct((m, d), BF16), jax.ShapeDtypeStruct((2, 8, d), F32)],
        compiler_params=_params(("arbitrary",)),
    )(dx, y, mods, *deps)


def _ffn_down_bwd(dy, wo, layer, au, name):
    m, d = dy.shape
    f = wo.shape[1]
    tm = _pick(m, ROW_TILES)

    def body(dy_ref, wo_ref, au_ref, dau_ref):
        dact = _dot(dy_ref[...], wo_ref[...], NT)
        aa, uu = au_ref[:, 0:f].astype(F32), au_ref[:, f:2 * f].astype(F32)
        sg = jax.nn.sigmoid(aa)
        dau_ref[:, 0:f] = (dact * uu * (sg * (1.0 + aa * (1.0 - sg)))).astype(BF16)
        dau_ref[:, f:2 * f] = (dact * (aa * sg)).astype(BF16)

    wide = pl.BlockSpec((tm, 2 * f), lambda i: (i, 0))
    return pl.pallas_call(
        body, name=name, grid=(m // tm,),
        in_specs=[pl.BlockSpec((tm, d), lambda i: (i, 0)), pl.BlockSpec((None, f, d), lambda i: (layer, 0, 0)), wide],
        out_specs=wide, out_shape=jax.ShapeDtypeStruct((m, 2 * f), BF16),
        compiler_params=_params(("parallel",)),
    )(dy, wo, au)


def _mm(terms, dims, n, out_dtype, name, tm_pref=(512, 256, 128, 64, 32, 16, 8), tn_pref=(512, 256, 128), dep=None):
    m = terms[0][0].shape[0]
    tm = _pick(m, tm_pref)
    tn = _pick(n, tn_pref)
    nt = len(terms)
    deps = [] if dep is None else [dep]

    def body(*refs):
        out_ref = refs[-1]
        acc = None
        for t in range(nt):
            part = _dot(refs[2 * t][...].astype(BF16), refs[2 * t + 1][...].astype(BF16), dims)
            acc = part if acc is None else acc + part
        out_ref[...] = acc.astype(out_dtype)

    in_specs, args = [], []
    for a, b, layer, rb in terms:
        k = a.shape[1]
        in_specs.append(pl.BlockSpec((tm, k), lambda j, i: (i, 0)))
        if dims == NN:
            in_specs.append(pl.BlockSpec((None, k, tn), lambda j, i, layer=layer, rb=rb: (layer, rb, j)))
        else:
            nb = n // tn
            in_specs.append(pl.BlockSpec((None, tn, k), lambda j, i, layer=layer, rb=rb, nb=nb: (layer, rb * nb + j, 0)))
        args += [a, b]
    return pl.pallas_call(
        body, name=name, grid=(n // tn, m // tm), in_specs=in_specs + [pl.BlockSpec(memory_space=pl.ANY)] * len(deps),
        out_specs=pl.BlockSpec((tm, tn), lambda j, i: (i, j)),
        out_shape=jax.ShapeDtypeStruct((m, n), out_dtype),
        compiler_params=_params(("parallel", "parallel")),
    )(*args, *deps)


def _mm_tn(a, b, out_dtype, name, dep=None):
    t = a.shape[0]
    m, n = a.shape[1], b.shape[1]
    tm = _pick(m, (1408, 2432, 1024, 512, 256, 128))
    tn = _pick(n, (1024, 512, 256, 128))
    tk = _pick(t, (512, 256, 128, 64, 32, 16, 8))
    deps = [] if dep is None else [dep]

    def body(a_ref, b_ref, *rest):
        out_ref, acc_ref = rest[-2:]
        kk = pl.program_id(2)

        @pl.when(kk == 0)
        def _():
            acc_ref[...] = jnp.zeros_like(acc_ref)

        acc_ref[...] += _dot(a_ref[...].astype(BF16), b_ref[...].astype(BF16), TN)

        @pl.when(kk == pl.num_programs(2) - 1)
        def _():
            out_ref[...] = acc_ref[...].astype(out_dtype)

    return pl.pallas_call(
        body, name=name, grid=(m // tm, n // tn, t // tk),
        in_specs=[pl.BlockSpec((tk, tm), lambda i, j, k: (k, i)), pl.BlockSpec((tk, tn), lambda i, j, k: (k, j))]
        + [pl.BlockSpec(memory_space=pl.ANY)] * len(deps),
        out_specs=pl.BlockSpec((tm, tn), lambda i, j, k: (i, j)),
        out_shape=jax.ShapeDtypeStruct((m, n), out_dtype),
        scratch_shapes=[pltpu.VMEM((tm, tn), F32)],
        compiler_params=_params(("parallel", "parallel", "arbitrary")),
    )(a, b, *deps)


def _stack_rows(a, b, name):
    ta, d = a.shape
    tm = _pick(int(np.gcd(ta, b.shape[0])), (256, 128, 64, 32, 16, 8))
    na, nb = ta // tm, b.shape[0] // tm

    def body(a_ref, b_ref, o_ref):
        o_ref[...] = jnp.where(pl.program_id(0) < na, a_ref[...], b_ref[...])

    return pl.pallas_call(
        body, name=name, grid=(na + nb,),
        in_specs=[pl.BlockSpec((tm, d), lambda i: (jnp.minimum(i, na - 1), 0)),
                  pl.BlockSpec((tm, d), lambda i: (jnp.maximum(i - na, 0), 0))],
        out_specs=pl.BlockSpec((tm, d), lambda i: (i, 0)),
        out_shape=jax.ShapeDtypeStruct((ta + b.shape[0], d), a.dtype),
        compiler_params=_params(("parallel",)),
    )(a, b)


def _assemble_dz(lat_parts, ctx_parts, both_parts, width, name):
    t = next(p.shape[0] for p in lat_parts if p is not None)
    l_ctx = next(p.shape[0] for p in ctx_parts if p is not None)
    tm = _pick(int(np.gcd(t, l_ctx)), (256, 128, 64, 32, 16, 8))
    nt, nl = t // tm, l_ctx // tm
    plan, args, in_specs, off = [], [], [], 0
    lat_spec = lambda w: pl.BlockSpec((tm, w), lambda i: (jnp.minimum(i, nt - 1), 0))
    ctx_spec = lambda w: pl.BlockSpec((tm, w), lambda i: (jnp.maximum(i - nt, 0), 0))
    all_spec = lambda w: pl.BlockSpec((tm, w), lambda i: (i, 0))
    for lat, ctx, both in zip(lat_parts, ctx_parts, both_parts):
        if both:
            w = both[0].shape[1]
            plan.append(("both", off, w, len(args), len(both)))
            args += both
            in_specs += [all_spec(w)] * len(both)
        else:
            w = (lat if lat is not None else ctx).shape[1]
            plan.append(("split", off, w, len(args), (lat is not None, ctx is not None)))
            for part, spec in ((lat, lat_spec), (ctx, ctx_spec)):
                if part is not None:
                    args.append(part)
                    in_specs.append(spec(w))
        off += w
    n_in = len(args)

    def body(*refs):
        out_ref = refs[n_in]
        is_ctx = pl.program_id(0) >= nt
        for kind, o, w, first, info in plan:
            if kind == "both":
                val = refs[first][...]
                for k in range(1, info):
                    val = val + refs[first + k][...]
            else:
                has_lat, has_ctx = info
                zero = jnp.zeros((tm, w), F32)
                lat = refs[first][...] if has_lat else zero
                ctx = refs[first + int(has_lat)][...] if has_ctx else zero
                val = jnp.where(is_ctx, ctx, lat)
            out_ref[:, o:o + w] = val.astype(BF16)
        if off < width:
            out_ref[:, off:width] = jnp.zeros((tm, width - off), BF16)

    return pl.pallas_call(
        body, name=name, grid=(nt + nl,), in_specs=in_specs,
        out_specs=pl.BlockSpec((tm, width), lambda i: (i, 0)),
        out_shape=jax.ShapeDtypeStruct((t + l_ctx, width), BF16),
        compiler_params=_params(("parallel",)),
    )(*args)


def _final_loss(x, g, target, name):
    t, d = x.shape
    tm = _pick(t, (256, 128, 64, 32, 16, 8))

    def body(x_ref, g_ref, t_ref, dx_ref, loss_ref, dg_ref):
        @pl.when(pl.program_id(0) == 0)
        def _():
            loss_ref[...] = jnp.zeros_like(loss_ref)
            dg_ref[...] = jnp.zeros_like(dg_ref)

        xx, gg = x_ref[...], g_ref[...]
        r = lax.rsqrt(jnp.mean(xx * xx, axis=-1, keepdims=True) + RMS_EPS)
        nz = xx * r
        err = nz * gg - t_ref[...]
        loss_ref[...] += jnp.sum(err * err, axis=0, keepdims=True) * (0.5 / d)
        dout = err * (1.0 / d)
        dg_ref[...] += jnp.sum(dout * nz, axis=0, keepdims=True)
        dn = dout * gg
        dx_ref[...] = r * (dn - nz * jnp.mean(dn * nz, axis=-1, keepdims=True))

    row = pl.BlockSpec((tm, d), lambda i: (i, 0))
    vec = pl.BlockSpec((1, d), lambda i: (0, 0))
    return pl.pallas_call(
        body, name=name, grid=(t // tm,), in_specs=[row, vec, row], out_specs=[row, vec, vec],
        out_shape=[jax.ShapeDtypeStruct((t, d), F32), jax.ShapeDtypeStruct((1, d), F32),
                   jax.ShapeDtypeStruct((1, d), F32)],
        compiler_params=_params(("arbitrary",)),
    )(x, g, target)


def _adaln_fwd(craw, w_mod, b_cols, name):
    lyr, d, nc = w_mod.shape

    def body(c_ref, w_ref, b_ref, out_ref):
        out_ref[...] = _bdot(_silu(c_ref[...]), w_ref[...]) + b_ref[...]

    return pl.pallas_call(
        body, name=name, grid=(lyr,),
        in_specs=[_full(craw.shape), pl.BlockSpec((None, d, nc), lambda l: (l, 0, 0)),
                  pl.BlockSpec((None, 1, nc), lambda l: (l, 0, 0))],
        out_specs=pl.BlockSpec((None, 16, nc), lambda l: (l, 0, 0)),
        out_shape=jax.ShapeDtypeStruct((lyr, 16, nc), F32),
        compiler_params=_params(("parallel",)),
    )(craw, w_mod, b_cols)


def _adaln_bwd(craw, cs_t, dmm_cols, w_mod, name):
    lyr, d, nc = w_mod.shape

    def body(c_ref, cst_ref, dmm_ref, w_ref, gw_ref, dc_ref):
        dmm = dmm_ref[...]
        gw_ref[...] = _bdot(cst_ref[...], dmm)
        cc = c_ref[...]
        sg = jax.nn.sigmoid(cc)
        dc_ref[...] = _bdot(dmm, w_ref[...], NT) * (sg * (1.0 + cc * (1.0 - sg)))

    wspec = pl.BlockSpec((None, d, nc), lambda l: (l, 0, 0))
    return pl.pallas_call(
        body, name=name, grid=(lyr,),
        in_specs=[_full(craw.shape), _full(cs_t.shape), pl.BlockSpec((None, 16, nc), lambda l: (l, 0, 0)), wspec],
        out_specs=[wspec, pl.BlockSpec((None, 16, d), lambda l: (l, 0, 0))],
        out_shape=[jax.ShapeDtypeStruct((lyr, d, nc), F32), jax.ShapeDtypeStruct((lyr, 16, d), F32)],
        compiler_params=_params(("parallel",)),
    )(craw, cs_t, dmm_cols, w_mod)


def _rope_tables(t):
    rows = np.repeat(np.arange(t // GRID_W, dtype=np.float32), GRID_W)
    cols = np.tile(np.arange(GRID_W, dtype=np.float32), t // GRID_W)
    n = A_HEAD_DIM // 4
    freqs = (ROPE_BASE ** (-np.arange(n, dtype=np.float32) / n)).astype(np.float32)
    ang_r, ang_c = (rows[:, None] * freqs).astype(np.float32), (cols[:, None] * freqs).astype(np.float32)
    cr, sr, cc, sc = np.cos(ang_r), np.sin(ang_r), np.cos(ang_c), np.sin(ang_c)
    cos = np.concatenate([cr, cr, cc, cc] * 2, axis=-1).astype(np.float32)
    sin = np.concatenate([-sr, sr, -sc, sc] * 2, axis=-1).astype(np.float32)
    return jnp.asarray(cos), jnp.asarray(sin)


def _rope(xt, cos, sin, adjoint, name):
    t, w = xt.shape
    tb = _pick(t, (512, 256, 128))
    rep = w // cos.shape[1]

    def body(x_ref, c_ref, s_ref, o_ref):
        xx = x_ref[...]
        cc = jnp.concatenate([c_ref[...]] * rep, axis=1) if rep > 1 else c_ref[...]
        ss = jnp.concatenate([s_ref[...]] * rep, axis=1) if rep > 1 else s_ref[...]
        low = (lax.broadcasted_iota(jnp.int32, xx.shape, 1) % 32) < 16

        def partner(v):
            return jnp.where(low, pltpu.roll(v, w - 16, 1), pltpu.roll(v, 16, 1))

        if adjoint:
            o_ref[...] = xx * cc + partner(xx * ss)
        else:
            o_ref[...] = xx * cc + partner(xx) * ss

    blk = pl.BlockSpec((tb, w), lambda i: (i, 0))
    tab = pl.BlockSpec((tb, cos.shape[1]), lambda i: (i, 0))
    return pl.pallas_call(
        body, name=name, grid=(t // tb,), in_specs=[blk, tab, tab], out_specs=blk,
        out_shape=jax.ShapeDtypeStruct((t, w), F32), compiler_params=_params(("parallel",)),
    )(xt, cos, sin)


def _attn_probs(q, kb, kc, sink, n, t):
    scale = A_HEAD_DIM ** -0.5
    s1 = _bdot(q, kb, NT) * scale
    s2 = _bdot(q, kc, NT) * scale
    qpos = n * WINDOW + lax.broadcasted_iota(jnp.int32, s1.shape, 0) % WINDOW
    kpos = (n - 1) * WINDOW + lax.broadcasted_iota(jnp.int32, s1.shape, 1)
    valid = (kpos >= 0) & (kpos < t) & (jnp.abs(kpos - qpos) <= WINDOW)
    s1 = jnp.where(valid, s1, -jnp.inf)
    mx = jnp.maximum(jnp.maximum(jnp.max(s1, axis=-1, keepdims=True), jnp.max(s2, axis=-1, keepdims=True)), sink)
    p1, p2, ps = jnp.exp(s1 - mx), jnp.exp(s2 - mx), jnp.exp(sink - mx)
    inv = 1.0 / (jnp.sum(p1, axis=-1, keepdims=True) + jnp.sum(p2, axis=-1, keepdims=True) + ps)
    return p1 * inv, p2 * inv, ps * inv


def _sink_rows(sink_ref):
    return jnp.concatenate([jnp.broadcast_to(sink_ref[r], (WINDOW, 1)) for r in range(A_REP)], axis=0)


def _attn_fwd(q, kp, vp, kc, vc, sink, name):
    hq, t, dh = q.shape
    nb = t // WINDOW
    lc = kc.shape[1]
    rows = A_REP * WINDOW

    def body(q_ref, k_ref, v_ref, kc_ref, vc_ref, sink_ref, o_ref):
        n = pl.program_id(1)
        start = pl.multiple_of(n * WINDOW, WINDOW)
        kb, vb = k_ref[pl.ds(start, 3 * WINDOW), :], v_ref[pl.ds(start, 3 * WINDOW), :]
        p1, p2, _ = _attn_probs(q_ref[...].reshape(rows, dh), kb, kc_ref[...], _sink_rows(sink_ref), n, t)
        o_ref[...] = (_bdot(p1, vb) + _bdot(p2, vc_ref[...])).reshape(A_REP, WINDOW, dh)

    qblk = pl.BlockSpec((A_REP, WINDOW, dh), lambda g, n: (g, n, 0))
    kfull = pl.BlockSpec((None, t + 2 * WINDOW, dh), lambda g, n: (g, 0, 0))
    cfull = pl.BlockSpec((None, lc, dh), lambda g, n: (g, 0, 0))
    return pl.pallas_call(
        body, name=name, grid=(hq // A_REP, nb),
        in_specs=[qblk, kfull, kfull, cfull, cfull, pl.BlockSpec((A_REP, 1, 1), lambda g, n: (g, 0, 0))],
        out_specs=qblk, out_shape=jax.ShapeDtypeStruct((hq, t, dh), F32),
        compiler_params=_params(("parallel", "parallel")),
    )(q, kp, vp, kc, vc, sink)


def _attn_bwd(q, kp, vp, kc, vc, sink, o, do, name):
    hq, t, dh = q.shape
    nb = t // WINDOW
    lc = kc.shape[1]
    scale = A_HEAD_DIM ** -0.5
    rows = A_REP * WINDOW

    def body(q_ref, k_ref, v_ref, kc_ref, vc_ref, sink_ref, o_ref, do_ref,
             dq_ref, dk_ref, dv_ref, dkc_ref, dvc_ref, dsink_ref):
        n = pl.program_id(1)

        @pl.when(n == 0)
        def _():
            dk_ref[...] = jnp.zeros_like(dk_ref)
            dv_ref[...] = jnp.zeros_like(dv_ref)
            dkc_ref[...] = jnp.zeros_like(dkc_ref)
            dvc_ref[...] = jnp.zeros_like(dvc_ref)
            dsink_ref[...] = jnp.zeros_like(dsink_ref)

        start = pl.multiple_of(n * WINDOW, WINDOW)
        band = pl.ds(start, 3 * WINDOW)
        qq, kb, vb, kcc, vcc = q_ref[...].reshape(rows, dh), k_ref[band, :], v_ref[band, :], kc_ref[...], vc_ref[...]
        p1, p2, ps = _attn_probs(qq, kb, kcc, _sink_rows(sink_ref), n, t)
        dout = do_ref[...].reshape(rows, dh)
        delta = jnp.sum(dout * o_ref[...].reshape(rows, dh), axis=-1, keepdims=True)
        ds1 = p1 * (_bdot(dout, vb, NT) - delta)
        ds2 = p2 * (_bdot(dout, vcc, NT) - delta)
        dq_ref[...] = ((_bdot(ds1, kb) + _bdot(ds2, kcc)) * scale).reshape(A_REP, WINDOW, dh)
        dk_ref[band, :] += _bdot(ds1.T, qq) * scale
        dv_ref[band, :] += _bdot(p1.T, dout)
        dkc_ref[...] += _bdot(ds2.T, qq) * scale
        dvc_ref[...] += _bdot(p2.T, dout)
        dsink_ref[...] += jnp.sum((-ps * delta).reshape(A_REP, WINDOW, 1), axis=1, keepdims=True)

    qblk = pl.BlockSpec((A_REP, WINDOW, dh), lambda g, n: (g, n, 0))
    kfull = pl.BlockSpec((None, t + 2 * WINDOW, dh), lambda g, n: (g, 0, 0))
    cfull = pl.BlockSpec((None, lc, dh), lambda g, n: (g, 0, 0))
    return pl.pallas_call(
        body, name=name, grid=(hq // A_REP, nb),
        in_specs=[qblk, kfull, kfull, cfull, cfull, pl.BlockSpec((A_REP, 1, 1), lambda g, n: (g, 0, 0)), qblk, qblk],
        out_specs=[qblk, kfull, kfull, cfull, cfull, pl.BlockSpec((A_REP, 8, 128), lambda g, n: (g, 0, 0))],
        out_shape=[jax.ShapeDtypeStruct(q.shape, F32), jax.ShapeDtypeStruct(kp.shape, F32),
                   jax.ShapeDtypeStruct(kp.shape, F32), jax.ShapeDtypeStruct(kc.shape, F32),
                   jax.ShapeDtypeStruct(kc.shape, F32), jax.ShapeDtypeStruct((hq, 8, 128), F32)],
        compiler_params=_params(("parallel", "arbitrary")),
    )(q, kp, vp, kc, vc, sink, o, do)


def _gate_fwd(zg, w2, b2, name):
    m = zg.shape[0]
    n = w2.shape[1]
    tm = _pick(m, (512, 256, 128, 64, 32, 16, 8))

    def body(z_ref, w_ref, b_ref, o_ref):
        o_ref[...] = jax.nn.log_sigmoid(_bdot(z_ref[...], w_ref[...]) + b_ref[...]) / B_GATE_NORM

    return pl.pallas_call(
        body, name=name, grid=(m // tm,),
        in_specs=[pl.BlockSpec((tm, zg.shape[1]), lambda i: (i, 0)), _full(w2.shape), _full(b2.shape)],
        out_specs=pl.BlockSpec((tm, n), lambda i: (i, 0)), out_shape=jax.ShapeDtypeStruct((m, n), F32),
        compiler_params=_params(("parallel",)),
    )(zg, w2, b2)


def _gate_bwd(zg, w2, b2, dla, name):
    m, rk = zg.shape
    n = w2.shape[1]
    tm = _pick(m, (512, 256, 128, 64, 32, 16, 8))

    def body(z_ref, w_ref, b_ref, d_ref, dz_ref, dw_ref, db_ref):
        @pl.when(pl.program_id(0) == 0)
        def _():
            dw_ref[...] = jnp.zeros_like(dw_ref)
            db_ref[...] = jnp.zeros_like(db_ref)

        zz, ww = z_ref[...], w_ref[...]
        pre = _bdot(zz, ww) + b_ref[...]
        dpre = d_ref[...] * (1.0 / B_GATE_NORM) * jax.nn.sigmoid(-pre)
        dz_ref[...] = _bdot(dpre, ww, NT)
        dw_ref[...] += _bdot(zz.T, dpre)
        db_ref[...] += jnp.sum(dpre, axis=0, keepdims=True)

    return pl.pallas_call(
        body, name=name, grid=(m // tm,),
        in_specs=[pl.BlockSpec((tm, rk), lambda i: (i, 0)), _full(w2.shape), _full(b2.shape),
                  pl.BlockSpec((tm, n), lambda i: (i, 0))],
        out_specs=[pl.BlockSpec((tm, rk), lambda i: (i, 0)), _full(w2.shape), _full(b2.shape)],
        out_shape=[jax.ShapeDtypeStruct((m, rk), F32), jax.ShapeDtypeStruct(w2.shape, F32),
                   jax.ShapeDtypeStruct(b2.shape, F32)],
        compiler_params=_params(("arbitrary",)),
    )(zg, w2, b2, dla)


def _chunk_order(step, n_x_chunks, n_chunks, reverse):
    n_c = n_chunks - n_x_chunks
    if reverse:
        return jnp.where(step < n_c, n_chunks - 1 - step, n_chunks - 1 - step)
    return jnp.where(step < n_c, n_x_chunks + step, step - n_c)


def _tri(reverse, transpose=False):
    i = lax.broadcasted_iota(jnp.int32, (B_CHUNK, B_CHUNK), 0)
    j = lax.broadcasted_iota(jnp.int32, (B_CHUNK, B_CHUNK), 1)
    if transpose:
        i, j = j, i
    return (j >= i) if reverse else (j <= i)


def _gla_chunk(q, k, la, reverse):
    g = _dot_01(_tri(reverse), la)
    last = 0 if reverse else B_CHUNK - 1
    gl = g[last:last + 1, :]
    eg, eng, egl = jnp.exp(g), jnp.exp(-g), jnp.exp(gl - g)
    decay_col = jnp.exp(jnp.sum(la.T, axis=1, keepdims=True))
    return q * (B_DK ** -0.5) * eg, k * eng, k * egl, eg, eng, egl, decay_col


def _head_of(shape, axis, width):
    return lax.broadcasted_iota(jnp.int32, shape, axis) // width


def _gla_fwd(q, k, v, la_f, la_b, n_x, name):
    tc, wk = q.shape
    wv = v.shape[1]
    hh = B_HEADS
    dk, dv = wk // hh, wv // hh
    nc, nxc = tc // B_CHUNK, n_x // B_CHUNK
    orders = [functools.partial(_chunk_order, n_x_chunks=nxc, n_chunks=nc, reverse=rev) for rev in (False, True)]

    def body(*refs):
        ins, outs, s_refs = refs[:8], refs[8:12], refs[12:]

        @pl.when(pl.program_id(0) == 0)
        def _():
            for s_ref in s_refs:
                s_ref[...] = jnp.zeros_like(s_ref)

        lane_head = _head_of((B_CHUNK, wk), 1, dk)
        row_head = _head_of((wk, dv), 0, dk)
        for di, reverse in enumerate((False, True)):
            q_ref, k_ref, v_ref, la_ref = ins[4 * di:4 * di + 4]
            o_ref, s_save_ref = outs[2 * di:2 * di + 2]
            s_ref = s_refs[di]
            qt, kt, ke, _, _, _, decay_col = _gla_chunk(q_ref[...], k_ref[...], la_ref[...], reverse)
            ke_t = ke.T
            s_prev = s_ref[...]
            update = jnp.zeros_like(s_prev)
            for h in range(hh):
                vv = v_ref[:, h * dv:(h + 1) * dv]
                qm = jnp.where(lane_head == h, qt, 0.0)
                att = jnp.where(_tri(reverse), _bdot(qm, kt, NT), 0.0)
                o_ref[:, h * dv:(h + 1) * dv] = _bdot(att, vv) + _bdot(qm, s_prev)
                update = jnp.where(row_head == h, _bdot(ke_t, vv), update)
            s_save_ref[...] = s_prev
            s_ref[...] = decay_col * s_prev + update

    def blk(w, order):
        return pl.BlockSpec((B_CHUNK, w), lambda s: (order(s), 0))

    def sblk(order):
        return pl.BlockSpec((None, wk, dv), lambda s: (order(s), 0, 0))

    in_specs, out_specs = [], []
    for order in orders:
        in_specs += [blk(wk, order), blk(wk, order), blk(wv, order), blk(wk, order)]
        out_specs += [blk(wv, order), sblk(order)]
    o_shape, s_shape = jax.ShapeDtypeStruct((tc, wv), F32), jax.ShapeDtypeStruct((nc, wk, dv), F32)
    return pl.pallas_call(
        body, name=name, grid=(nc,), in_specs=in_specs, out_specs=out_specs,
        out_shape=[o_shape, s_shape, o_shape, s_shape],
        scratch_shapes=[pltpu.VMEM((wk, dv), F32)] * 2,
        compiler_params=_params(("arbitrary",)),
    )(q, k, v, la_f, q, k, v, la_b)


def _gla_bwd(q, k, v, la_f, la_b, s_f, s_b, do, n_x, name):
    tc, wk = q.shape
    wv = v.shape[1]
    hh = B_HEADS
    dk, dv = wk // hh, wv // hh
    nc, nxc = tc // B_CHUNK, n_x // B_CHUNK
    orders = [functools.partial(lambda s, rev: _chunk_order(nc - 1 - s, nxc, nc, rev), rev=rev) for rev in (False, True)]

    def body(*refs):
        ins, outs, ds_refs = refs[:12], refs[12:20], refs[20:]

        @pl.when(pl.program_id(0) == 0)
        def _():
            for ds_ref in ds_refs:
                ds_ref[...] = jnp.zeros_like(ds_ref)

        lane_head = _head_of((B_CHUNK, wk), 1, dk)
        row_head = _head_of((wk, dv), 0, dk)
        for di, reverse in enumerate((False, True)):
            q_ref, k_ref, v_ref, la_ref, s_save_ref, do_ref = ins[6 * di:6 * di + 6]
            dq_ref, dk_ref, dv_ref, dla_ref = outs[4 * di:4 * di + 4]
            ds_ref = ds_refs[di]
            mask = _tri(reverse)
            last = 0 if reverse else B_CHUNK - 1
            is_last = lax.broadcasted_iota(jnp.int32, (B_CHUNK, wk), 0) == last
            la = la_ref[...]
            qt, kt, ke, eg, eng, egl, decay_col = _gla_chunk(q_ref[...], k_ref[...], la, reverse)
            qt_t = qt.T
            s_prev, ds_new = s_save_ref[...], ds_ref[...]
            dqt, dkt, dke = jnp.zeros_like(qt), jnp.zeros_like(qt), jnp.zeros_like(qt)
            ds_add = jnp.zeros_like(ds_new)
            for h in range(hh):
                cols = slice(h * dv, (h + 1) * dv)
                vv, dout = v_ref[:, cols], do_ref[:, cols]
                mine = lane_head == h
                qm, km = jnp.where(mine, qt, 0.0), jnp.where(mine, ke, 0.0)
                att = jnp.where(mask, _bdot(qm, kt, NT), 0.0)
                datt = jnp.where(mask, _bdot(dout, vv, NT), 0.0)
                dv_ref[:, cols] = _bdot(att.T, dout) + _bdot(km, ds_new)
                dqt = jnp.where(mine, _bdot(datt, kt) + _bdot(dout, s_prev, NT), dqt)
                dkt = jnp.where(mine, _bdot(datt.T, qt), dkt)
                dke = jnp.where(mine, _bdot(vv, ds_new, NT), dke)
                ds_add = jnp.where(row_head == h, _bdot(qt_t, dout), ds_add)
            ddecay_row = jnp.sum((ds_new * s_prev).T, axis=0, keepdims=True)
            decay_row = jnp.exp(jnp.sum(la, axis=0, keepdims=True))
            ds_ref[...] = decay_col * ds_new + ds_add
            dq_ref[...] = dqt * (B_DK ** -0.5) * eg
            dk_ref[...] = dkt * eng + dke * egl
            dgl = jnp.sum(dke * ke, axis=0, keepdims=True) + ddecay_row * decay_row
            dg = dqt * qt - dkt * kt - dke * ke + jnp.where(is_last, dgl, 0.0)
            dla_ref[...] = _dot_01(_tri(reverse, transpose=True), dg)

    def blk(w, order):
        return pl.BlockSpec((B_CHUNK, w), lambda s: (order(s), 0))

    in_specs, out_specs = [], []
    for order in orders:
        in_specs += [blk(wk, order), blk(wk, order), blk(wv, order), blk(wk, order),
                     pl.BlockSpec((None, wk, dv), lambda s, order=order: (order(s), 0, 0)), blk(wv, order)]
        out_specs += [blk(wk, order), blk(wk, order), blk(wv, order), blk(wk, order)]
    k_shape, v_shape = jax.ShapeDtypeStruct((tc, wk), F32), jax.ShapeDtypeStruct((tc, wv), F32)
    return pl.pallas_call(
        body, name=name, grid=(nc,), in_specs=in_specs, out_specs=out_specs,
        out_shape=[k_shape, k_shape, v_shape, k_shape] * 2,
        scratch_shapes=[pltpu.VMEM((wk, dv), F32)] * 2,
        compiler_params=_params(("arbitrary",)),
    )(q, k, v, la_f, s_f, do, q, k, v, la_b, s_b, do)


def _gla_out_fwd(o_f, o_b, r, g, name):
    t = r.shape[0]
    dv = g.shape[1]
    hh = r.shape[1] // dv
    tb = _pick(t, (256, 128, 64))

    def body(of_ref, ob_ref, r_ref, g_ref, out_ref):
        for h in range(hh):
            cols = slice(h * dv, (h + 1) * dv)
            o = of_ref[:, cols] + ob_ref[:, cols]
            rs = lax.rsqrt(jnp.mean(o * o, axis=-1, keepdims=True) + RMS_EPS)
            out_ref[:, cols] = (o * rs) * g_ref[...] * _silu(r_ref[:, cols])

    rblk = pl.BlockSpec((tb, hh * dv), lambda i: (i, 0))
    return pl.pallas_call(
        body, name=name, grid=(t // tb,), in_specs=[rblk, rblk, rblk, _full(g.shape)], out_specs=rblk,
        out_shape=jax.ShapeDtypeStruct((t, hh * dv), F32), compiler_params=_params(("parallel",)),
    )(o_f, o_b, r, g)


def _gla_out_bwd(o_f, o_b, r, g, dout, name):
    tc = o_f.shape[0]
    t = r.shape[0]
    dv = g.shape[1]
    hh = r.shape[1] // dv
    tb = _pick(int(np.gcd(t, tc)), (256, 128, 64))
    nt = t // tb

    def body(of_ref, ob_ref, r_ref, g_ref, d_ref, do_ref, dr_ref, dg_ref):
        i = pl.program_id(0)

        @pl.when(i == 0)
        def _():
            dg_ref[...] = jnp.zeros_like(dg_ref)

        @pl.when(i >= nt)
        def _():
            do_ref[...] = jnp.zeros_like(do_ref)

        @pl.when(i < nt)
        def _():
            gg = g_ref[...]
            for h in range(hh):
                cols = slice(h * dv, (h + 1) * dv)
                o = of_ref[:, cols] + ob_ref[:, cols]
                rs = lax.rsqrt(jnp.mean(o * o, axis=-1, keepdims=True) + RMS_EPS)
                nz = o * rs
                rr, dd = r_ref[:, cols], d_ref[:, cols]
                sg = jax.nn.sigmoid(rr)
                dr_ref[:, cols] = dd * nz * gg * (sg * (1.0 + rr * (1.0 - sg)))
                dy = dd * (rr * sg)
                dg_ref[...] += jnp.sum(dy * nz, axis=0, keepdims=True)
                dn = dy * gg
                do_ref[:, cols] = rs * (dn - nz * jnp.mean(dn * nz, axis=-1, keepdims=True))

    oblk = pl.BlockSpec((tb, hh * dv), lambda i: (i, 0))
    rblk = pl.BlockSpec((tb, hh * dv), lambda i: (jnp.minimum(i, nt - 1), 0))
    return pl.pallas_call(
        body, name=name, grid=(tc // tb,), in_specs=[oblk, oblk, rblk, _full(g.shape), rblk],
        out_specs=[oblk, rblk, _full(g.shape)],
        out_shape=[jax.ShapeDtypeStruct(o_f.shape, F32), jax.ShapeDtypeStruct(r.shape, F32),
                   jax.ShapeDtypeStruct(g.shape, F32)],
        compiler_params=_params(("arbitrary",)),
    )(o_f, o_b, r, g, dout)


def _pool_window(i, tb, t):
    return pl.multiple_of(jnp.clip(i * tb - POOL_PAD, 0, t - (tb + 2 * POOL_PAD)), 8)


def _pool_band(half, i, tb, start, adjoint):
    pos = i * tb + lax.broadcasted_iota(jnp.int32, (tb, tb + 2 * POOL_PAD), 0)
    tok = start + lax.broadcasted_iota(jnp.int32, (tb, tb + 2 * POOL_PAD), 1)
    if adjoint:
        return (tok > pos - half) & (tok <= pos + half)
    return (tok >= pos - half) & (tok < pos + half)


def _pool_count(pos, half, t):
    return (jnp.minimum(pos + half, t) - jnp.maximum(pos - half, 0)).astype(F32)


def _pool_fwd(h, w_pool, pool_scale, res, mods, km, name):
    t, d = res.shape
    ng, gw = w_pool.shape[0], w_pool.shape[1]
    tb = _pick(t, (256, 128, 64))

    def body(h_ref, w_ref, ps_ref, res_ref, mods_ref, out_ref, pooled_ref, ypre_ref):
        gi, i = pl.program_id(0), pl.program_id(1)
        half = jnp.left_shift(1, gi)
        start = _pool_window(i, tb, t)
        win = h_ref[pl.ds(start, tb + 2 * POOL_PAD), :]
        total = _dot_01(_pool_band(half, i, tb, start, False), win)
        pos = i * tb + lax.broadcasted_iota(jnp.int32, (tb, 1), 0)
        pooled = total / _pool_count(pos, half, t) - h_ref[pl.ds(pl.multiple_of(i * tb, tb), tb), :]
        ypre = _bdot(pooled, w_ref[...])
        pooled_ref[...] = pooled.astype(BF16)
        ypre_ref[...] = ypre
        out_ref[...] = res_ref[...] + mods_ref[0, km:km + 1, :] * (ypre * ps_ref[...])

    tile = pl.BlockSpec((tb, gw), lambda gi, i: (i, gi))
    return pl.pallas_call(
        body, name=name, grid=(ng, t // tb),
        in_specs=[pl.BlockSpec((t, gw), lambda gi, i: (0, gi)),
                  pl.BlockSpec((None, gw, gw), lambda gi, i: (gi, 0, 0)),
                  pl.BlockSpec((1, gw), lambda gi, i: (0, gi)), tile,
                  pl.BlockSpec((2, 16, gw), lambda gi, i: (0, 0, gi))],
        out_specs=[tile, tile, tile],
        out_shape=[jax.ShapeDtypeStruct((t, d), F32), jax.ShapeDtypeStruct((t, d), BF16),
                   jax.ShapeDtypeStruct((t, d), F32)],
        compiler_params=_params(("parallel", "parallel")),
    )(h, w_pool, pool_scale, res, mods)


def _pool_bwd(dxp, w_pool, pool_scale, pooled, ypre, mods, km, name):
    t, d = pooled.shape
    ng, gw = w_pool.shape[0], w_pool.shape[1]
    tb = _pick(t, (256, 128, 64))

    def body(dxp_ref, w_ref, ps_ref, pooled_ref, ypre_ref, mods_ref, dh_ref, dw_ref, acc_ref):
        gi, i = pl.program_id(0), pl.program_id(1)

        @pl.when(i == 0)
        def _():
            dw_ref[...] = jnp.zeros_like(dw_ref)
            acc_ref[...] = jnp.zeros_like(acc_ref)

        half = jnp.left_shift(1, gi)
        mod, ps = mods_ref[0, km:km + 1, :], ps_ref[...]
        start = _pool_window(i, tb, t)
        dwin = dxp_ref[pl.ds(start, tb + 2 * POOL_PAD), :]
        dpooled = _bdot(dwin * (mod * ps), w_ref[...], NT)
        pos = start + lax.broadcasted_iota(jnp.int32, (tb + 2 * POOL_PAD, 1), 0)
        spread = _dot_01(_pool_band(half, i, tb, start, True), dpooled / _pool_count(pos, half, t))
        dxc, yp = dxp_ref[pl.ds(pl.multiple_of(i * tb, tb), tb), :], ypre_ref[...]
        dh_ref[...] = spread - _bdot(dxc * (mod * ps), w_ref[...], NT)
        dw_ref[...] += _bdot(pooled_ref[...].astype(F32).T, dxc * (mod * ps))
        acc_ref[0:1, :] += jnp.sum(dxc * yp * mod, axis=0, keepdims=True)
        acc_ref[1:2, :] += jnp.sum(dxc * yp * ps, axis=0, keepdims=True)

    tile = pl.BlockSpec((tb, gw), lambda gi, i: (i, gi))
    wblk = pl.BlockSpec((None, gw, gw), lambda gi, i: (gi, 0, 0))
    return pl.pallas_call(
        body, name=name, grid=(ng, t // tb),
        in_specs=[pl.BlockSpec((t, gw), lambda gi, i: (0, gi)), wblk,
                  pl.BlockSpec((1, gw), lambda gi, i: (0, gi)), tile, tile,
                  pl.BlockSpec((2, 16, gw), lambda gi, i: (0, 0, gi))],
        out_specs=[tile, wblk, pl.BlockSpec((8, gw), lambda gi, i: (0, gi))],
        out_shape=[jax.ShapeDtypeStruct((t, d), F32), jax.ShapeDtypeStruct(w_pool.shape, F32),
                   jax.ShapeDtypeStruct((8, d), F32)],
        compiler_params=_params(("arbitrary", "arbitrary")),
    )(dxp, w_pool, pool_scale, pooled, ypre, mods)


def _adamw(w, g, m, v, name):
    r, c = w.shape
    tr = _pick(r, (512, 352, 256, 128, 64, 32, 16, 8))
    c1 = 1.0 / (1.0 - ADAM_B1 ** ADAM_STEP)
    c2 = 1.0 / (1.0 - ADAM_B2 ** ADAM_STEP)

    def body(w_ref, g_ref, m_ref, v_ref, d_ref, nm_ref, nv_ref):
        gg = g_ref[...]
        nm = ADAM_B1 * m_ref[...] + (1.0 - ADAM_B1) * gg
        nv = ADAM_B2 * v_ref[...] + (1.0 - ADAM_B2) * (gg * gg)
        nm_ref[...] = nm
        nv_ref[...] = nv
        d_ref[...] = -ADAM_LR * ((nm * c1) / (jnp.sqrt(nv * c2) + ADAM_EPS) + ADAM_WD * w_ref[...])

    blk = pl.BlockSpec((tr, c), lambda i: (i, 0))
    shp = jax.ShapeDtypeStruct((r, c), F32)
    return pl.pallas_call(
        body, name=name, grid=(r // tr,), in_specs=[blk] * 4, out_specs=[blk] * 3, out_shape=[shp] * 3,
        compiler_params=_params(("parallel",)),
    )(w, g, m, v)


def _heads(z, n_heads):
    m = z.shape[0]
    return z.reshape(m, n_heads, -1).transpose(1, 0, 2)


def _unheads(zh):
    return zh.transpose(1, 0, 2).reshape(zh.shape[1], -1)


def _pad_rows(a, n):
    return jnp.pad(a, ((0, 0), (n, n), (0, 0))) if a.ndim == 3 else jnp.pad(a, ((n, n), (0, 0)))


def _local_step(x, ctx, target, mods, wts, fetch, emit):
    t, d = x.shape
    l_ctx = ctx.shape[0]
    tc = t + l_ctx
    norm_g = wts["norm_g"]
    ng = lambda l, k: norm_g[l, k][None, :]
    grads = {}
    dmods = [[[None] * N_MOD for _ in range(2)] for _ in range(2)]
    dnorm = [[None] * 3 for _ in range(2)]

    def ffn_fwd(z, h, l, kbase, wi, wo, n_x, tag, nxt):
        au, act = _ffn_up(h, wi, 0, f"ffn_up_{tag}")
        wo = wo(act) if callable(wo) else wo
        outs = _mm_resid(act, wo, 0, z, mods[l], kbase + 2, 0.5, n_x, f"ffn_down_{tag}", nxt=nxt)
        return outs[0], (z, h, au, act, outs[1], wi, wo), (outs[2] if nxt is not None else None)

    def ffn_bwd(dz_new, saved, l, kbase, g, n_x, tag, stage, split=False):
        z, h, au, act, y, wi, wo = saved
        dy, acc_gate = _resid_bwd(dz_new, y, mods[l], kbase + 2, 0.5, n_x, f"resid_bwd_{tag}")
        dau = _ffn_down_bwd(dy, wo, 0, au, f"ffn_down_bwd_{tag}")
        dwo = _mm_tn(act, dy, BF16, f"dwo_{tag}")
        if split:
            token = emit(stage, [dwo])
            dwi_t = _mm_tn(dau, h, BF16, f"dwi_{tag}", dep=token)
            token = emit(stage + 1, [dwi_t])
        else:
            dwi_t = _mm_tn(dau, h, BF16, f"dwi_{tag}")
            token = emit(stage, [dwi_t, dwo])
        dh = _mm([(dau, wi, 0, 0)], NN, d, F32, f"dh_{tag}", tm_pref=ROW_TILES, dep=token)
        dz, acc_mod = _modulate_bwd(z, dh, dz_new, mods[l], g, kbase + 1, n_x, f"mod_bwd_{tag}")
        return dz, acc_mod, acc_gate

    def record(l, kbase, k_norm, g, acc_mod, acc_gate, streams):
        total = None
        for s in range(streams):
            dmods[l][s][kbase] = acc_mod[s, 0]
            dmods[l][s][kbase + 1] = acc_mod[s, 1] * g[0]
            if acc_gate is not None:
                dmods[l][s][kbase + 2] = acc_gate[s, 0]
            part = acc_mod[s, 1] * (1.0 + mods[l][s, kbase + 1])
            total = part if total is None else total + part
        dnorm[l][k_norm] = total

    xc0 = _stack_rows(x, ctx, "stack_tokens")
    wi1_0 = fetch(0, None)["wi1_0"]
    h0 = _modulate(xc0, mods[0], ng(0, 0), 0, 1, t, BF16, "mod_l0f1")
    xc1, sv_f1, hc = ffn_fwd(xc0, h0, 0, 0, wi1_0, lambda act: fetch(1, act)["wo1_0"], t, "l0f1",
                             (mods[0], ng(0, 1), 3, 4, BF16))
    w_in_t = fetch(2, hc)["w_in_t"]
    n_proj = w_in_t.shape[1]
    zall = _mm([(hc, w_in_t, 0, 0)], NT, n_proj, F32, "proj", tm_pref=ROW_TILES,
               tn_pref=(n_proj,))
    offs = np.cumsum((0,) + PROJ_SIZES)
    part = lambda i, rows=slice(None): zall[rows, offs[i]:offs[i + 1]]
    lat, con = slice(0, t), slice(t, tc)
    cos, sin = _rope_tables(t)
    qa = _heads(_rope(part(0, lat), cos, sin, False, "rope_q"), A_HEADS)
    ka = _heads(_rope(part(1, lat), cos, sin, False, "rope_k"), A_KV_HEADS)
    va = _heads(part(2, lat), A_KV_HEADS)
    kca, vca = _heads(part(1, con), A_KV_HEADS), _heads(part(2, con), A_KV_HEADS)
    kap, vap = _pad_rows(ka, WINDOW), _pad_rows(va, WINDOW)
    sink = wts["sink"].reshape(A_HEADS, 1, 1)
    o_a = _attn_fwd(qa, kap, vap, kca, vca, sink, "attn_fwd")

    qb, kb, vb = part(3), part(4), part(5)
    rb = part(6, lat)
    zg = part(7)
    zg_f, zg_b = zg[:, :B_GATE_RANK], zg[:, B_GATE_RANK:]
    w2f, w2b, b2f, b2b = wts["w_a2_f"], wts["w_a2_b"], wts["b_a_f"], wts["b_a_b"]
    la_f = _gate_fwd(zg_f, w2f, b2f, "gate_f")
    la_b = _gate_fwd(zg_b, w2b, b2b, "gate_b")
    o_f, s_f, o_b, s_b = _gla_fwd(qb, kb, vb, la_f, la_b, t, "gla_fwd")
    gla_g = wts["gla_g"]
    go = _gla_out_fwd(o_f, o_b, rb, gla_g, "gla_out")
    cat = jnp.concatenate([_unheads(o_a), go], axis=-1).astype(BF16)
    big = fetch(3, cat)
    w_out, wi2_0, wo2_0 = big["w_out"], big["wi2_0"], big["wo2_0"]
    x2, y_mix0, h2 = _mm_resid(cat, w_out, 0, xc1, mods[0], 5, 1.0, t, "w_out", nxt=(mods[0], ng(0, 2), 6, 7, BF16))
    x3, sv_f2, h3 = ffn_fwd(x2, h2, 0, 6, wi2_0, wo2_0, t, "l0f2", (mods[1], ng(1, 0), 0, 1, BF16))

    big = fetch(4, x3)
    wi1_1, wo1_1, wi2_1, wo2_1 = big["wi1_1"], big["wo1_1"], big["wi2_1"], big["wo2_1"]
    x4, sv_g1, hp = ffn_fwd(x3, h3, 1, 0, wi1_1, wo1_1, t, "l1f1", (mods[1], ng(1, 1), 3, 4, F32))
    w_pool, pool_scale = wts["w_pool"], wts["pool_scale"]
    x5, pooled, ypre = _pool_fwd(hp, w_pool, pool_scale, x4, mods[1], 5, "pool_fwd")
    h5 = _modulate(x5, mods[1], ng(1, 2), 6, 7, t, BF16, "mod_l1f2")
    x6, sv_g2, _ = ffn_fwd(x5, h5, 1, 6, wi2_1, wo2_1, t, "l1f2", None)

    dx6, loss_vec, dfinal_g = _final_loss(x6, wts["final_g"], target, "final_loss")
    grads["final_g"] = dfinal_g[0]

    dx5, acc_mod, acc_gate = ffn_bwd(dx6, sv_g2, 1, 6, ng(1, 2), t, "l1f2", 0)
    record(1, 6, 2, ng(1, 2), acc_mod, acc_gate, 1)
    dhp, dw_pool, acc_pool = _pool_bwd(dx5, w_pool, pool_scale, pooled, ypre, mods[1], 5, "pool_bwd")
    grads["pool_scale"] = acc_pool[0]
    dmods[1][0][5] = acc_pool[1]
    dx4, acc_mod = _modulate_bwd(x4, dhp, dx5, mods[1], ng(1, 1), 4, t, "mod_bwd_l1mix")
    record(1, 3, 1, ng(1, 1), acc_mod, None, 1)
    dx3, acc_mod, acc_gate = ffn_bwd(dx4, sv_g1, 1, 0, ng(1, 0), t, "l1f1", 1)
    record(1, 0, 0, ng(1, 0), acc_mod, acc_gate, 1)

    dx2, acc_mod, acc_gate = ffn_bwd(dx3, sv_f2, 0, 6, ng(0, 2), t, "l0f2", 2)
    record(0, 6, 2, ng(0, 2), acc_mod, acc_gate, 1)
    dymix, acc_gate = _resid_bwd(dx2, y_mix0, mods[0], 5, 1.0, t, "resid_bwd_mix")
    dmods[0][0][5] = acc_gate[0, 0]
    dw_out = _mm_tn(cat, dymix, BF16, "dw_out")
    dcat = _mm([(dymix, w_out, 0, 0)], NT, cat.shape[1], F32, "dcat")
    do_a = _heads(dcat[:, :A_Q], A_HEADS)
    do_full, drb, dgla_g = _gla_out_bwd(o_f, o_b, rb, gla_g, dcat[:, A_Q:], "gla_out_bwd")
    grads["gla_g"] = dgla_g[0]
    dq_f, dk_f, dv_f, dla_f, dq_b, dk_b, dv_b, dla_b = _gla_bwd(qb, kb, vb, la_f, la_b, s_f, s_b, do_full, t, "gla_bwd")
    dzg_f, dw2f, db2f = _gate_bwd(zg_f, w2f, b2f, dla_f, "gate_bwd_f")
    dzg_b, dw2b, db2b = _gate_bwd(zg_b, w2b, b2b, dla_b, "gate_bwd_b")
    grads.update(w_a2_f=dw2f, w_a2_b=dw2b, b_a_f=db2f[0], b_a_b=db2b[0])
    dqa_r, dkap, dvap, dkca, dvca, dsink = _attn_bwd(qa, kap, vap, kca, vca, sink, o_a, do_a, "attn_bwd")
    grads["sink"] = dsink[:, 0, 0]
    dqa = _rope(_unheads(dqa_r), cos, sin, True, "rope_bwd_q")
    dka = _rope(_unheads(dkap[:, WINDOW:WINDOW + t]), cos, sin, True, "rope_bwd_k")
    dva = dvap[:, WINDOW:WINDOW + t]
    dzg = jnp.concatenate([dzg_f, dzg_b, jnp.zeros((tc, n_proj - PROJ_DIM), F32)], axis=-1)
    dzall = _assemble_dz(
        [dqa, dka, _unheads(dva), None, None, None, drb, None],
        [None, _unheads(dkca), _unheads(dvca), None, None, None, None, None],
        [None, None, None, [dq_f, dq_b], [dk_f, dk_b], [dv_f, dv_b], None, [dzg]], n_proj, "assemble_dz")
    dw_in_t = _mm_tn(dzall, hc, BF16, "dw_in")
    token = emit(3, [dw_in_t, dw_out, dw_pool])
    dhc = _mm([(dzall, w_in_t, 0, 0)], NN, d, F32, "dhc", tm_pref=ROW_TILES, dep=token)
    dxc1, acc_mod = _modulate_bwd(xc1, dhc, dx2, mods[0], ng(0, 1), 4, t, "mod_bwd_l0mix")
    record(0, 3, 1, ng(0, 1), acc_mod, None, 2)
    dxc0, acc_mod, acc_gate = ffn_bwd(dxc1, sv_f1, 0, 0, ng(0, 0), t, "l0f1", 4, split=True)
    record(0, 0, 0, ng(0, 0), acc_mod, acc_gate, 2)

    grads["norm_g"] = jnp.stack([jnp.stack(dnorm[0]), jnp.stack(dnorm[1])])
    zero = jnp.zeros((d,), F32)
    dmods_arr = jnp.stack([jnp.stack([jnp.stack([v if v is not None else zero for v in dmods[l][s]])
                                      for s in range(2)]) for l in range(2)])
    return loss_vec, dxc0[:t], grads, dmods_arr


def _pack(parts):
    flat = jnp.concatenate([p.reshape(-1).astype(F32) for p in parts])
    pad = (-flat.shape[0]) % 128
    return jnp.pad(flat, (0, pad))[None, :]


def _unpack(rows, shapes):
    out, off = [], 0
    for s in shapes:
        n = int(np.prod(s))
        out.append(rows[:, off:off + n].reshape((rows.shape[0],) + tuple(s)))
        off += n
    return out


def _cols_to_full(g):
    g = jnp.moveaxis(g, 0, -2)
    return g.reshape(g.shape[:-2] + (-1,))


def kernel(x, c, ctx, c_ctx, w_mod, b_mod, norm_g, ffn1_wi, ffn1_wo, ffn2_wi, ffn2_wo, w_in, w_a2_f, b_a_f, w_a2_b, b_a_b, sink, gla_g, w_out, w_pool, pool_scale, final_g, loss_target, m_c_ctx, m_w_mod, m_b_mod, m_norm_g, m_ffn1_wi, m_ffn1_wo, m_ffn2_wi, m_ffn2_wo, m_w_in, m_w_a2_f, m_b_a_f, m_w_a2_b, m_b_a_b, m_sink, m_gla_g, m_w_out, m_w_pool, m_pool_scale, m_final_g, v_c_ctx, v_w_mod, v_b_mod, v_norm_g, v_ffn1_wi, v_ffn1_wo, v_ffn2_wi, v_ffn2_wo, v_w_in, v_w_a2_f, v_b_a_f, v_w_a2_b, v_b_a_b, v_sink, v_gla_g, v_w_out, v_w_pool, v_pool_scale, v_final_g):
    t, d = x.shape[1], x.shape[2]
    me = _dev_index()
    nc = w_mod.shape[2]
    ncol_in = w_in.shape[2]
    ncol_pad = -(-ncol_in // 16) * 16

    small_shapes = [(d,), norm_g.shape, pool_scale.shape, w_a2_f.shape, w_a2_b.shape, w_pool.shape]
    g1 = _gather_small(_pack([c, norm_g, pool_scale, w_a2_f, w_a2_b, w_pool]), "gather_params")
    c_all, norm_g_all, pool_scale_all, w2f_all, w2b_all, w_pool_all = _unpack(g1, small_shapes)
    wts = {
        "norm_g": _cols_to_full(norm_g_all),
        "pool_scale": _cols_to_full(pool_scale_all),
        "w_a2_f": _cols_to_full(w2f_all)[0],
        "w_a2_b": _cols_to_full(w2b_all)[0],
        "w_pool": jnp.moveaxis(w_pool_all[:, 0], 0, 1).reshape(w_pool.shape[1], -1, w_pool.shape[3]),
        "b_a_f": b_a_f, "b_a_b": b_a_b, "sink": sink[0], "gla_g": gla_g, "final_g": final_g[None, :],
    }

    craw = jnp.concatenate([c_all, c_ctx[None, :], jnp.zeros((16 - N_DEV - 1, d), F32)], axis=0)
    b_cols = lax.dynamic_slice_in_dim(b_mod, me * nc, nc, axis=1)[:, None, :]
    mm_cols = _adaln_fwd(craw, w_mod, b_cols, "adaln_fwd")
    g2 = _gather_small(mm_cols.reshape(1, -1), "gather_mods").reshape(N_DEV, 2, 16, nc)
    mm_full = jnp.moveaxis(g2, 0, 2).reshape(2, 16, N_MOD, d)
    mods = jnp.stack([lax.dynamic_index_in_dim(mm_full, me, axis=1, keepdims=False), mm_full[:, N_DEV]], axis=1)
    mods = jnp.pad(mods, ((0, 0), (0, 0), (0, 16 - N_MOD), (0, 0)))

    tr = lambda w: jnp.swapaxes(w, 1, 2).astype(BF16)
    wi1_sh, wi2_sh, wo1_sh, wo2_sh = tr(ffn1_wi), tr(ffn2_wi), ffn1_wo.astype(BF16), ffn2_wo.astype(BF16)
    w_in_sh = jnp.pad(tr(w_in), ((0, 0), (0, ncol_pad - ncol_in), (0, 0)))
    groups = [
        {"wi1_0": wi1_sh[0:1]},
        {"wo1_0": wo1_sh[0:1]},
        {"w_in": w_in_sh},
        {"w_out": w_out.astype(BF16), "wi2_0": wi2_sh[0:1], "wo2_0": wo2_sh[0:1]},
        {"wi1_1": wi1_sh[1:2], "wo1_1": wo1_sh[1:2], "wi2_1": wi2_sh[1:2], "wo2_1": wo2_sh[1:2]},
    ]

    placed = iter(_place_shards([s for grp in groups for s in grp.values()], "gather_place"))
    gathers, token = [], None
    for gi, grp in enumerate(groups):
        shards = list(grp.values())
        gathers.append(_exchange_start(shards, [next(placed) for _ in shards], True, 1 + gi, token, f"gather_start_{gi}"))
        token = gathers[-1][4]
    n_proj = -(-(N_DEV * ncol_in) // 128) * 128

    def fetch(gi, after):
        _, lands = _exchange_wait(gathers[gi], True, token if after is None else after, f"gather_wait_{gi}")
        out = dict(zip(groups[gi].keys(), lands))
        if "w_in" in out:
            w_in_t = out.pop("w_in").reshape(1, N_DEV, ncol_pad, d)[:, :, :ncol_in].reshape(1, N_DEV * ncol_in, d)
            out["w_in_t"] = jnp.pad(w_in_t, ((0, 0), (0, n_proj - N_DEV * ncol_in), (0, 0)))
        return out

    scatters = []

    def emit(stage, arrays):
        if stage == 3:
            dw_in_t, dw_out, dw_pool = arrays
            dw_in_full = dw_in_t[:N_DEV * ncol_in].reshape(N_DEV, ncol_in, d)
            dw_in_full = jnp.pad(dw_in_full, ((0, 0), (0, ncol_pad - ncol_in), (0, 0)))
            srcs = [dw_in_full.reshape(1, N_DEV * ncol_pad, d), dw_out[None], dw_pool.astype(BF16)]
        else:
            srcs = [a[None] for a in arrays]
        lands = [lax.empty((N_DEV, s.shape[0], s.shape[1] // N_DEV, s.shape[2]), s.dtype) for s in srcs]
        scatters.append(_exchange_start(srcs, lands, False, 1 + len(groups) + stage, None, f"scatter_start_{stage}"))
        return scatters[-1][4]

    loss_vec, grad_x, grads, dmods = _local_step(x[0], ctx[0], loss_target[0], mods, wts, fetch, emit)
    loss = lax.psum(jnp.sum(loss_vec), ("x", "y", "c"))

    def reduce_stage(stage, after):
        wholes, lands = _exchange_wait(scatters[stage], False, after, f"scatter_wait_{stage}")
        return [_sum_slots(ld, wh, me, f"sum_grad_{stage}_{i}") for i, (ld, wh) in enumerate(zip(lands, wholes))]

    (dwi2_1, dwo2_1), (dwi1_1, dwo1_1), (dwi2_0, dwo2_0), (dw_in_s, dw_out_s, dw_pool_s) = [
        reduce_stage(stage, grad_x) for stage in range(4)]
    back = lambda g: jnp.swapaxes(g, 1, 2)
    g_big = {
        "ffn2_wi": back(jnp.concatenate([dwi2_0, dwi2_1], axis=0)), "ffn2_wo": jnp.concatenate([dwo2_0, dwo2_1], axis=0),
        "w_in": back(dw_in_s[:, :ncol_in]), "w_out": dw_out_s, "w_pool": dw_pool_s[None],
    }

    order = ["c_ctx", "w_mod", "b_mod", "norm_g", "ffn1_wi", "ffn1_wo", "ffn2_wi", "ffn2_wo", "w_in", "w_a2_f", "b_a_f",
             "w_a2_b", "b_a_b", "sink", "gla_g", "w_out", "w_pool", "pool_scale", "final_g"]
    ws = dict(c_ctx=c_ctx, w_mod=w_mod, b_mod=b_mod, norm_g=norm_g, ffn1_wi=ffn1_wi, ffn1_wo=ffn1_wo, ffn2_wi=ffn2_wi,
              ffn2_wo=ffn2_wo, w_in=w_in, w_a2_f=w_a2_f, b_a_f=b_a_f, w_a2_b=w_a2_b, b_a_b=b_a_b, sink=sink, gla_g=gla_g,
              w_out=w_out, w_pool=w_pool, pool_scale=pool_scale, final_g=final_g)
    ms = dict(c_ctx=m_c_ctx, w_mod=m_w_mod, b_mod=m_b_mod, norm_g=m_norm_g, ffn1_wi=m_ffn1_wi, ffn1_wo=m_ffn1_wo,
              ffn2_wi=m_ffn2_wi, ffn2_wo=m_ffn2_wo, w_in=m_w_in, w_a2_f=m_w_a2_f, b_a_f=m_b_a_f, w_a2_b=m_w_a2_b,
              b_a_b=m_b_a_b, sink=m_sink, gla_g=m_gla_g, w_out=m_w_out, w_pool=m_w_pool, pool_scale=m_pool_scale,
              final_g=m_final_g)
    vs = dict(c_ctx=v_c_ctx, w_mod=v_w_mod, b_mod=v_b_mod, norm_g=v_norm_g, ffn1_wi=v_ffn1_wi, ffn1_wo=v_ffn1_wo,
              ffn2_wi=v_ffn2_wi, ffn2_wo=v_ffn2_wo, w_in=v_w_in, w_a2_f=v_w_a2_f, b_a_f=v_b_a_f, w_a2_b=v_w_a2_b,
              b_a_b=v_b_a_b, sink=v_sink, gla_g=v_gla_g, w_out=v_w_out, w_pool=v_w_pool, pool_scale=v_pool_scale,
              final_g=v_final_g)
    big = ["ffn2_wi", "ffn2_wo", "w_out", "w_mod", "ffn1_wi", "ffn1_wo"]
    delta, new_m, new_v = {}, {}, {}
    g_all = dict(g_big)

    def adamw_big(nm):
        shp = ws[nm].shape
        two_d = lambda a: a.reshape(-1, shp[-1])
        dl, nm_, nv_ = _adamw(two_d(ws[nm]), two_d(g_all[nm]), two_d(ms[nm]), two_d(vs[nm]), f"adamw_{nm}")
        delta[nm], new_m[nm], new_v[nm] = dl.reshape(shp), nm_.reshape(shp), nv_.reshape(shp)

    for nm in big[:3]:
        adamw_big(nm)

    small_g = [dmods[:, :, :N_MOD].reshape(2, 2, N_MOD * d), grads["norm_g"], grads["pool_scale"], grads["final_g"],
               grads["b_a_f"], grads["b_a_b"], grads["sink"], grads["gla_g"], grads["w_a2_f"], grads["w_a2_b"]]
    small_g_shapes = [a.shape for a in small_g]
    g3 = _gather_small(_pack(small_g), "gather_small_grads", dep=delta["w_out"])
    total = _sum_rows8(g3, "sum_small_grads")
    dmm_all = _unpack(g3, small_g_shapes[:1])[0]
    (dmm_sum, dnorm_g, dpool_scale, dfinal_g, db_a_f, db_a_b, dsink, dgla_g, dw_a2_f, dw_a2_b) = [
        a[0] for a in _unpack(total, small_g_shapes)]
    dmm_rows = jnp.concatenate([dmm_all[:, :, 0].transpose(1, 0, 2), dmm_sum[:, 1][:, None, :],
                                jnp.zeros((2, 16 - N_DEV - 1, N_MOD * d), F32)], axis=1)
    grad_b_mod = dmm_sum[:, 0] + dmm_sum[:, 1]
    dmm_cols = lax.dynamic_slice_in_dim(dmm_rows, me * nc, nc, axis=2)
    cs_t = jnp.transpose(_silu(craw)).astype(BF16)
    grad_w_mod, dcraw = _adaln_bwd(craw, cs_t, dmm_cols, w_mod, "adaln_bwd")
    g4 = _gather_small((dcraw[0, N_DEV] + dcraw[1, N_DEV])[None, :], "gather_c_ctx_grad")
    grad_c_ctx = _sum_rows8(g4, "sum_c_ctx_grad")[0]

    col = lambda v, n: lax.dynamic_slice_in_dim(v, me * n, n, axis=v.ndim - 1)
    g_small = {
        "c_ctx": grad_c_ctx, "b_mod": grad_b_mod, "norm_g": col(dnorm_g, norm_g.shape[2]),
        "w_a2_f": col(dw_a2_f, w_a2_f.shape[2])[None], "b_a_f": db_a_f[None], "w_a2_b": col(dw_a2_b, w_a2_b.shape[2])[None],
        "b_a_b": db_a_b[None], "sink": dsink[None], "gla_g": dgla_g[None], "pool_scale": col(dpool_scale, pool_scale.shape[1])[None],
        "final_g": dfinal_g,
    }
    g_all.update(g_small, w_mod=grad_w_mod)
    adamw_big("w_mod")
    rest = [nm for nm in order if nm not in big]
    rest_shapes = [ws[nm].shape for nm in rest]
    packed = [_pack([d_[nm].reshape(ws[nm].shape) for nm in rest]).reshape(-1, 128) for d_ in (ws, g_all, ms, vs)]
    pad_rows = (-packed[0].shape[0]) % 512
    packed = [jnp.pad(p, ((0, pad_rows), (0, 0))) for p in packed]
    outs = _adamw(*packed, "adamw_small")
    for dst, arr in zip((delta, new_m, new_v), outs):
        for nm, val in zip(rest, _unpack(arr.reshape(1, -1), rest_shapes)):
            dst[nm] = val[0]

    (dwo1_0,), (dwi1_0,) = reduce_stage(4, outs[0]), reduce_stage(5, outs[0])
    g_all["ffn1_wi"] = back(jnp.concatenate([dwi1_0, dwi1_1], axis=0))
    g_all["ffn1_wo"] = jnp.concatenate([dwo1_0, dwo1_1], axis=0)
    for nm in big[4:]:
        adamw_big(nm)
    g_all = {nm: g_all[nm].reshape(ws[nm].shape) for nm in order}

    return (loss, grad_x[None], *[g_all[nm] for nm in order], *[delta[nm] for nm in order],
            *[new_m[nm] for nm in order], *[new_v[nm] for nm in order])
```

```python
import functools

import numpy as np
import jax
import jax.numpy as jnp
from jax import lax
from jax.experimental import pallas as pl
from jax.experimental.pallas import tpu as pltpu

F32 = jnp.float32
BF16 = jnp.bfloat16
MESH = pl.DeviceIdType.MESH

N_DEV = 8
RMS_EPS = 1e-6
N_MOD = 9
GRID_W = 64
A_HEADS, A_KV_HEADS, A_HEAD_DIM = 8, 2, 64
A_REP = A_HEADS // A_KV_HEADS
WINDOW = 128
ROPE_BASE = 10000.0
B_HEADS, B_DK, B_DV = 4, 64, 128
B_GATE_RANK = 16
B_GATE_NORM = 16.0
B_CHUNK = 64
POOL_WINDOWS = (2, 4, 8, 16)
POOL_PAD = 8
A_Q = A_HEADS * A_HEAD_DIM
A_KV = A_KV_HEADS * A_HEAD_DIM
B_QK = B_HEADS * B_DK
B_V = B_HEADS * B_DV
PROJ_SIZES = (A_Q, A_KV, A_KV, B_QK, B_QK, B_V, B_V, 2 * B_GATE_RANK)
PROJ_DIM = sum(PROJ_SIZES)
ADAM_LR, ADAM_B1, ADAM_B2, ADAM_EPS, ADAM_WD, ADAM_STEP = 0.001, 0.9, 0.999, 1e-08, 0.01, 10

VMEM_LIMIT = 56 * 1024 * 1024
ROW_TILES = (512, 544, 256, 128, 64, 32, 16, 8)

NN = ((1,), (0,))
NT = ((1,), (1,))
TN = ((0,), (0,))


def _dot(a, b, dims=NN, prec=None):
    return lax.dot_general(a, b, (dims, ((), ())), precision=prec, preferred_element_type=F32)


def _bdot(a, b, dims=NN):
    return _dot(a.astype(BF16), b.astype(BF16), dims)


def _dot_01(sel, x):
    hi = x.astype(BF16)
    rest = x - hi.astype(F32)
    mid = rest.astype(BF16)
    lo = (rest - mid.astype(F32)).astype(BF16)
    sel = sel.astype(BF16)
    return _dot(sel, hi) + _dot(sel, mid) + _dot(sel, lo)


def _params(sem=None, **kw):
    return pltpu.CompilerParams(dimension_semantics=sem, vmem_limit_bytes=VMEM_LIMIT, **kw)


def _silu(a):
    return a * jax.nn.sigmoid(a)


def _pick(n, prefs):
    for p in prefs:
        if n % p == 0:
            return p
    return n


def _full(shape):
    nd = len(shape)
    return pl.BlockSpec(shape, lambda *_: (0,) * nd)


def _peers():
    x, y, c = lax.axis_index("x"), lax.axis_index("y"), lax.axis_index("c")
    return x, y, c


def _dev_index():
    x, y, c = _peers()
    return 4 * x + 2 * y + c


def _others(x, y, c):
    return [(x, y, 1 - c), (1 - x, y, c), (x, 1 - y, c), (1 - x, 1 - y, c),
            (1 - x, y, 1 - c), (x, 1 - y, 1 - c), (1 - x, 1 - y, 1 - c)]


def _index_of(dev):
    return 4 * dev[0] + 2 * dev[1] + dev[2]


def _exchange_refs(gather, shapes, srcs, lands, a, me, to):
    if gather:
        r = shapes[a][1]
        return srcs[a], lands[a].at[:, pl.ds(_index_of(me) * r, r), :]
    r = shapes[a][1] // N_DEV
    return srcs[a].at[:, pl.ds(_index_of(to) * r, r), :], lands[a].at[_index_of(me)]


HBM_SPEC = pl.BlockSpec(memory_space=pltpu.HBM)
SEM_SPEC = pl.BlockSpec(memory_space=pltpu.SEMAPHORE)
EFFECT = pltpu.SideEffectType.DATAFLOW_SIDE_EFFECTING


def _exchange_start(srcs, lands, gather, collective_id, dep, name):
    n = len(srcs)
    shapes = [s.shape for s in srcs]
    deps = [] if dep is None else [dep]

    def body(*refs):
        src_refs, land_refs = refs[:n], refs[n:2 * n]
        send_sems, recv_sems = refs[2 * n + len(deps)], refs[2 * n + len(deps) + 1]
        token = refs[-1]
        x, y, c = _peers()
        others = _others(x, y, c)
        barrier = pltpu.get_barrier_semaphore()
        for peer in others:
            pl.semaphore_signal(barrier, inc=1, device_id=peer, device_id_type=MESH)
        pl.semaphore_wait(barrier, len(others))
        for a in range(n):
            for k, to in enumerate(others):
                src, dst = _exchange_refs(gather, shapes, src_refs, land_refs, a, (x, y, c), to)
                pltpu.make_async_remote_copy(src_ref=src, dst_ref=dst, send_sem=send_sems.at[7 * a + k],
                                             recv_sem=recv_sems.at[7 * a + k], device_id=to, device_id_type=MESH).start()
        token[...] = jnp.zeros_like(token)

    outs = pl.pallas_call(
        body, name=name,
        out_shape=(pltpu.SemaphoreType.DMA((7 * n,)), pltpu.SemaphoreType.DMA((7 * n,)),
                   *[pltpu.HBM(s.shape, s.dtype) for s in srcs], *[pltpu.HBM(l.shape, l.dtype) for l in lands],
                   jax.ShapeDtypeStruct((8, 128), F32)),
        in_specs=[HBM_SPEC] * (2 * n) + [pl.BlockSpec(memory_space=pl.ANY)] * len(deps),
        out_specs=(SEM_SPEC, SEM_SPEC, *[HBM_SPEC] * (2 * n), pl.BlockSpec(memory_space=pltpu.VMEM)),
        input_output_aliases={i: 2 + i for i in range(2 * n)},
        compiler_params=pltpu.CompilerParams(has_side_effects=EFFECT, collective_id=collective_id),
    )(*[pltpu.with_memory_space_constraint(s, pltpu.HBM) for s in srcs],
      *[pltpu.with_memory_space_constraint(l, pltpu.HBM) for l in lands], *deps)
    return outs[0], outs[1], list(outs[2:2 + n]), list(outs[2 + n:2 + 2 * n]), outs[-1]


def _exchange_wait(started, gather, after, name):
    send_sems, recv_sems, srcs, lands, _ = started
    n = len(srcs)
    shapes = [s.shape for s in srcs]

    def body(*refs):
        src_refs, land_refs = refs[:n], refs[n:2 * n]
        send_sems, recv_sems = refs[2 * n], refs[2 * n + 1]
        x, y, c = _peers()
        for a in range(n):
            for k, peer in enumerate(_others(x, y, c)):
                src, _ = _exchange_refs(gather, shapes, src_refs, land_refs, a, (x, y, c), peer)
                _, dst = _exchange_refs(gather, shapes, src_refs, land_refs, a, peer, (x, y, c))
                copy = pltpu.make_async_remote_copy(src_ref=src, dst_ref=dst, send_sem=send_sems.at[7 * a + k],
                                                    recv_sem=recv_sems.at[7 * a + k], device_id=peer, device_id_type=MESH)
                copy.wait_send()
                copy.wait_recv()

    outs = pl.pallas_call(
        body, name=name,
        out_shape=(*[pltpu.HBM(s.shape, s.dtype) for s in srcs], *[pltpu.HBM(l.shape, l.dtype) for l in lands]),
        in_specs=[HBM_SPEC] * (2 * n) + [SEM_SPEC, SEM_SPEC, pl.BlockSpec(memory_space=pl.ANY)],
        out_specs=tuple([HBM_SPEC] * (2 * n)),
        input_output_aliases={i: i for i in range(2 * n)},
        compiler_params=pltpu.CompilerParams(has_side_effects=EFFECT),
    )(*srcs, *lands, send_sems, recv_sems, after)
    return list(outs[:n]), list(outs[n:])


def _place_shard(shard, me, name):
    a_, r, c = shard.shape
    tr = _pick(r, (352, 304, 256, 128, 64, 32, 16, 8))
    nr = r // tr

    def body(me_ref, in_ref, out_ref):
        out_ref[...] = in_ref[...]

    return pl.pallas_call(
        body, name=name,
        grid_spec=pltpu.PrefetchScalarGridSpec(
            num_scalar_prefetch=1, grid=(a_, nr),
            in_specs=[pl.BlockSpec((None, tr, c), lambda i, j, me_ref: (i, j, 0))],
            out_specs=pl.BlockSpec((None, tr, c), lambda i, j, me_ref: (i, me_ref[0] * nr + j, 0))),
        out_shape=jax.ShapeDtypeStruct((a_, N_DEV * r, c), shard.dtype),
        compiler_params=_params(("parallel", "parallel")),
    )(me.reshape(1).astype(jnp.int32), shard)


def _sum_slots(land, whole, me, name):
    _, a_, r, c = land.shape
    tr = _pick(r, (352, 256, 128, 64, 32, 16, 8))
    nr = r // tr

    def body(me_ref, land_ref, own_ref, out_ref):
        acc = None
        for s in range(N_DEV):
            part = jnp.where(me_ref[0] == s, own_ref[...], land_ref[s]).astype(F32)
            acc = part if acc is None else acc + part
        out_ref[...] = acc

    return pl.pallas_call(
        body, name=name,
        grid_spec=pltpu.PrefetchScalarGridSpec(
            num_scalar_prefetch=1, grid=(a_, nr),
            in_specs=[pl.BlockSpec((N_DEV, None, tr, c), lambda i, j, me_ref: (0, i, j, 0)),
                      pl.BlockSpec((None, tr, c), lambda i, j, me_ref: (i, me_ref[0] * nr + j, 0))],
            out_specs=pl.BlockSpec((None, tr, c), lambda i, j, me_ref: (i, j, 0))),
        out_shape=jax.ShapeDtypeStruct((a_, r, c), F32),
        compiler_params=_params(("parallel", "parallel")),
    )(me.reshape(1).astype(jnp.int32), land, whole)


def _gather_small(vec, name, dep=None):
    p = vec.shape[1]
    pp = -(-p // 1024) * 1024
    blk = jnp.pad(vec, ((0, 0), (0, pp - p))).reshape(8, pp // 8)
    deps = [] if dep is None else [dep]

    def body(in_ref, *rest):
        out_ref, send_sems, recv_sems = rest[-3:]
        x, y, c = _peers()
        me = 4 * x + 2 * y + c
        others = [(x, y, 1 - c), (1 - x, y, c), (x, 1 - y, c), (1 - x, 1 - y, c),
                  (1 - x, y, 1 - c), (x, 1 - y, 1 - c), (1 - x, 1 - y, 1 - c)]

        def rows(idx):
            return out_ref.at[pl.ds(pl.multiple_of(idx * 8, 8), 8), :]

        out_ref[pl.ds(pl.multiple_of(me * 8, 8), 8), :] = in_ref[...]

        def copy(k, dev, slot):
            return pltpu.make_async_remote_copy(
                src_ref=in_ref, dst_ref=rows(slot), send_sem=send_sems.at[k], recv_sem=recv_sems.at[k],
                device_id=dev, device_id_type=MESH)

        sends = [copy(k, dev, me) for k, dev in enumerate(others)]
        for cp in sends:
            cp.start()
        for k, dev in enumerate(others):
            copy(k, dev, 4 * dev[0] + 2 * dev[1] + dev[2]).wait_recv()
        for cp in sends:
            cp.wait_send()

    vm = pl.BlockSpec(memory_space=pltpu.VMEM)
    out = pl.pallas_call(
        body, name=name, out_shape=jax.ShapeDtypeStruct((8 * N_DEV, pp // 8), F32),
        in_specs=[vm] + [pl.BlockSpec(memory_space=pl.ANY)] * len(deps), out_specs=vm,
        scratch_shapes=[pltpu.SemaphoreType.DMA((7,)), pltpu.SemaphoreType.DMA((7,))],
        compiler_params=pltpu.CompilerParams(has_side_effects=True, vmem_limit_bytes=VMEM_LIMIT),
    )(blk, *deps)
    return out.reshape(N_DEV, pp)[:, :p]


def _sum_rows8(g, name):
    p = g.shape[1]

    def body(in_ref, out_ref):
        acc = in_ref[0:1, :]
        for s in range(1, N_DEV):
            acc = acc + in_ref[s:s + 1, :]
        out_ref[...] = acc

    return pl.pallas_call(body, name=name, out_shape=jax.ShapeDtypeStruct((1, p), F32),
                          compiler_params=_params())(g)


def _sel_row(mods_ref, is_ctx, k):
    return jnp.where(is_ctx, mods_ref[1, k:k + 1, :], mods_ref[0, k:k + 1, :])


def _modulate(z, mods, g, ks, kc, n_x, out_dtype, name):
    m, d = z.shape
    tm = _pick(m, (256, 128, 64, 32, 16, 8))

    def body(z_ref, mods_ref, g_ref, h_ref):
        is_ctx = pl.program_id(0) * tm >= n_x
        zz = z_ref[...]
        r = lax.rsqrt(jnp.mean(zz * zz, axis=-1, keepdims=True) + RMS_EPS)
        shift, scale = _sel_row(mods_ref, is_ctx, ks), _sel_row(mods_ref, is_ctx, kc)
        h_ref[...] = ((zz * r) * g_ref[...] * (1.0 + scale) + shift).astype(out_dtype)

    return pl.pallas_call(
        body, name=name, grid=(m // tm,),
        in_specs=[pl.BlockSpec((tm, d), lambda i: (i, 0)), _full(mods.shape), _full(g.shape)],
        out_specs=pl.BlockSpec((tm, d), lambda i: (i, 0)),
        out_shape=jax.ShapeDtypeStruct((m, d), out_dtype),
        compiler_params=_params(("parallel",)),
    )(z, mods, g)


def _modulate_bwd(z, dh, dres, mods, g, kc, n_x, name, latent_only=False):
    m, d = z.shape
    tm = _pick(m, (256, 128, 64, 32, 16, 8))
    first_ctx = n_x // tm
    res_blocks = dres.shape[0] // tm
    out_blocks = (n_x if latent_only else m) // tm

    def body(z_ref, dh_ref, dres_ref, mods_ref, g_ref, dx_ref, acc_ref):
        i = pl.program_id(0)
        is_ctx = i * tm >= n_x

        @pl.when((i == 0) | (i == first_ctx))
        def _():
            acc_ref[...] = jnp.zeros_like(acc_ref)

        zz, dhh = z_ref[...], dh_ref[...]
        r = lax.rsqrt(jnp.mean(zz * zz, axis=-1, keepdims=True) + RMS_EPS)
        nz = zz * r
        gain = g_ref[...] * (1.0 + _sel_row(mods_ref, is_ctx, kc))
        dn = dhh * gain
        dz = r * (dn - nz * jnp.mean(dn * nz, axis=-1, keepdims=True))
        @pl.when(i < out_blocks)
        def _():
            dx_ref[...] = jnp.where(i < res_blocks, dres_ref[...], 0.0) + dz

        acc_ref[0:1, :] += jnp.sum(dhh, axis=0, keepdims=True)
        acc_ref[1:2, :] += jnp.sum(dhh * nz, axis=0, keepdims=True)

    row = pl.BlockSpec((tm, d), lambda i: (i, 0))
    return pl.pallas_call(
        body, name=name, grid=(m // tm,),
        in_specs=[row, row, pl.BlockSpec((tm, d), lambda i: (jnp.minimum(i, res_blocks - 1), 0)),
                  _full(mods.shape), _full(g.shape)],
        out_specs=[pl.BlockSpec((tm, d), lambda i: (jnp.minimum(i, out_blocks - 1), 0)),
                   pl.BlockSpec((None, 8, d), lambda i: ((i * tm >= n_x).astype(jnp.int32), 0, 0))],
        out_shape=[jax.ShapeDtypeStruct((out_blocks * tm, d), F32), jax.ShapeDtypeStruct((2, 8, d), F32)],
        compiler_params=_params(("arbitrary",)),
    )(z, dh, dres, mods, g)


def _ffn_up(h, wi_t, layer, name):
    m, d = h.shape
    f = wi_t.shape[1] // 2
    tm = _pick(m, ROW_TILES)

    def body(h_ref, w_ref, au_ref, act_ref):
        hh = h_ref[...]
        a = _dot(hh, w_ref[0:f, :], NT)
        u = _dot(hh, w_ref[f:2 * f, :], NT)
        au_ref[:, 0:f] = a.astype(BF16)
        au_ref[:, f:2 * f] = u.astype(BF16)
        act_ref[...] = (_silu(a) * u).astype(BF16)

    return pl.pallas_call(
        body, name=name, grid=(m // tm,),
        in_specs=[pl.BlockSpec((tm, d), lambda i: (i, 0)),
                  pl.BlockSpec((None, 2 * f, d), lambda i: (layer, 0, 0))],
        out_specs=[pl.BlockSpec((tm, 2 * f), lambda i: (i, 0)), pl.BlockSpec((tm, f), lambda i: (i, 0))],
        out_shape=[jax.ShapeDtypeStruct((m, 2 * f), BF16), jax.ShapeDtypeStruct((m, f), BF16)],
        compiler_params=_params(("parallel",)),
    )(h, wi_t)


def _mm_resid(a, b, layer, res, mods, km, coef, n_x, name, nxt=None):
    m, k = a.shape
    n = b.shape[2]
    tm = _pick(m, (512, 256, 128, 64, 32, 16, 8))
    tn = n if nxt is not None else _pick(n, (1024, 512, 256, 128))
    extra = [] if nxt is None else [nxt[0], nxt[1]]

    def body(a_ref, b_ref, res_ref, mods_ref, *rest):
        is_ctx = pl.program_id(1) * tm >= n_x
        y = _dot(a_ref[...], b_ref[...])
        new = res_ref[...] + coef * _sel_row(mods_ref, is_ctx, km) * y
        if nxt is None:
            out_ref, y_ref = rest
        else:
            nmods_ref, g_ref, out_ref, y_ref, h_ref = rest
            r = lax.rsqrt(jnp.mean(new * new, axis=-1, keepdims=True) + RMS_EPS)
            shift, scale = _sel_row(nmods_ref, is_ctx, nxt[2]), _sel_row(nmods_ref, is_ctx, nxt[3])
            h_ref[...] = ((new * r) * g_ref[...] * (1.0 + scale) + shift).astype(nxt[4])
        y_ref[...] = y
        out_ref[...] = new

    tile = pl.BlockSpec((tm, tn), lambda j, i: (i, j))
    outs = [jax.ShapeDtypeStruct((m, n), F32), jax.ShapeDtypeStruct((m, n), F32)]
    if nxt is not None:
        outs.append(jax.ShapeDtypeStruct((m, n), nxt[4]))
    return pl.pallas_call(
        body, name=name, grid=(n // tn, m // tm),
        in_specs=[pl.BlockSpec((tm, k), lambda j, i: (i, 0)),
                  pl.BlockSpec((None, k, tn), lambda j, i: (layer, 0, j)),
                  tile, pl.BlockSpec((2, 16, tn), lambda j, i: (0, 0, j))] + [_full(e.shape) for e in extra],
        out_specs=[tile] * len(outs), out_shape=outs,
        compiler_params=_params(("parallel", "parallel")),
    )(a, b, res, mods, *extra)


def _resid_bwd(dx, y, mods, km, coef, n_x, name, dep=None):
    m, d = dx.shape
    tm = _pick(m, (256, 128, 64, 32, 16, 8))
    first_ctx = n_x // tm
    deps = [] if dep is None else [dep]

    def body(dx_ref, y_ref, mods_ref, *rest):
        dy_ref, acc_ref = rest[-2:]
        i = pl.program_id(0)
        is_ctx = i * tm >= n_x

        @pl.when((i == 0) | (i == first_ctx))
        def _():
            acc_ref[...] = jnp.zeros_like(acc_ref)

        dxx = dx_ref[...]
        dy_ref[...] = (coef * _sel_row(mods_ref, is_ctx, km) * dxx).astype(BF16)
        acc_ref[0:1, :] += jnp.sum(coef * y_ref[...] * dxx, axis=0, keepdims=True)

    row = pl.BlockSpec((tm, d), lambda i: (i, 0))
    return pl.pallas_call(
        body, name=name, grid=(m // tm,),
        in_specs=[row, row, _full(mods.shape)] + [pl.BlockSpec(memory_space=pl.ANY)] * len(deps),
        out_specs=[row, pl.BlockSpec((None, 8, d), lambda i: ((i * tm >= n_x).astype(jnp.int32), 0, 0))],
        out_shape=[jax.ShapeDtypeStruct((m, d), BF16), jax.ShapeDtypeStruct((2, 8, d), F32)],
        compiler_params=_params(("arbitrary",)),
    )(dx, y, mods, *deps)


def _ffn_down_bwd(dy, wo, layer, au, name):
    m, d = dy.shape
    f = wo.shape[1]
    tm = _pick(m, ROW_TILES)

    def body(dy_ref, wo_ref, au_ref, dau_ref):
        dact = _dot(dy_ref[...], wo_ref[...], NT)
        aa, uu = au_ref[:, 0:f].astype(F32), au_ref[:, f:2 * f].astype(F32)
        sg = jax.nn.sigmoid(aa)
        dau_ref[:, 0:f] = (dact * uu * (sg * (1.0 + aa * (1.0 - sg)))).astype(BF16)
        dau_ref[:, f:2 * f] = (dact * (aa * sg)).astype(BF16)

    wide = pl.BlockSpec((tm, 2 * f), lambda i: (i, 0))
    return pl.pallas_call(
        body, name=name, grid=(m // tm,),
        in_specs=[pl.BlockSpec((tm, d), lambda i: (i, 0)), pl.BlockSpec((None, f, d), lambda i: (layer, 0, 0)), wide],
        out_specs=wide, out_shape=jax.ShapeDtypeStruct((m, 2 * f), BF16),
        compiler_params=_params(("parallel",)),
    )(dy, wo, au)


def _mm(terms, dims, n, out_dtype, name, tm_pref=(512, 256, 128, 64, 32, 16, 8), tn_pref=(512, 256, 128), dep=None):
    m = terms[0][0].shape[0]
    tm = _pick(m, tm_pref)
    tn = _pick(n, tn_pref)
    nt = len(terms)
    deps = [] if dep is None else [dep]

    def body(*refs):
        out_ref = refs[-1]
        acc = None
        for t in range(nt):
            part = _dot(refs[2 * t][...].astype(BF16), refs[2 * t + 1][...].astype(BF16), dims)
            acc = part if acc is None else acc + part
        out_ref[...] = acc.astype(out_dtype)

    in_specs, args = [], []
    for a, b, layer, rb in terms:
        k = a.shape[1]
        in_specs.append(pl.BlockSpec((tm, k), lambda j, i: (i, 0)))
        if dims == NN:
            in_specs.append(pl.BlockSpec((None, k, tn), lambda j, i, layer=layer, rb=rb: (layer, rb, j)))
        else:
            nb = n // tn
            in_specs.append(pl.BlockSpec((None, tn, k), lambda j, i, layer=layer, rb=rb, nb=nb: (layer, rb * nb + j, 0)))
        args += [a, b]
    return pl.pallas_call(
        body, name=name, grid=(n // tn, m // tm), in_specs=in_specs + [pl.BlockSpec(memory_space=pl.ANY)] * len(deps),
        out_specs=pl.BlockSpec((tm, tn), lambda j, i: (i, j)),
        out_shape=jax.ShapeDtypeStruct((m, n), out_dtype),
        compiler_params=_params(("parallel", "parallel")),
    )(*args, *deps)


def _mm_tn(a, b, out_dtype, name, dep=None):
    t = a.shape[0]
    m, n = a.shape[1], b.shape[1]
    tm = _pick(m, (1408, 2432, 1024, 512, 256, 128))
    tn = _pick(n, (1024, 512, 256, 128))
    tk = _pick(t, (512, 256, 128, 64, 32, 16, 8))
    deps = [] if dep is None else [dep]

    def body(a_ref, b_ref, *rest):
        out_ref, acc_ref = rest[-2:]
        kk = pl.program_id(2)

        @pl.when(kk == 0)
        def _():
            acc_ref[...] = jnp.zeros_like(acc_ref)

        acc_ref[...] += _dot(a_ref[...].astype(BF16), b_ref[...].astype(BF16), TN)

        @pl.when(kk == pl.num_programs(2) - 1)
        def _():
            out_ref[...] = acc_ref[...].astype(out_dtype)

    return pl.pallas_call(
        body, name=name, grid=(m // tm, n // tn, t // tk),
        in_specs=[pl.BlockSpec((tk, tm), lambda i, j, k: (k, i)), pl.BlockSpec((tk, tn), lambda i, j, k: (k, j))]
        + [pl.BlockSpec(memory_space=pl.ANY)] * len(deps),
        out_specs=pl.BlockSpec((tm, tn), lambda i, j, k: (i, j)),
        out_shape=jax.ShapeDtypeStruct((m, n), out_dtype),
        scratch_shapes=[pltpu.VMEM((tm, tn), F32)],
        compiler_params=_params(("parallel", "parallel", "arbitrary")),
    )(a, b, *deps)


def _stack_rows(a, b, name):
    ta, d = a.shape
    tm = _pick(int(np.gcd(ta, b.shape[0])), (256, 128, 64, 32, 16, 8))
    na, nb = ta // tm, b.shape[0] // tm

    def body(a_ref, b_ref, o_ref):
        o_ref[...] = jnp.where(pl.program_id(0) < na, a_ref[...], b_ref[...])

    return pl.pallas_call(
        body, name=name, grid=(na + nb,),
        in_specs=[pl.BlockSpec((tm, d), lambda i: (jnp.minimum(i, na - 1), 0)),
                  pl.BlockSpec((tm, d), lambda i: (jnp.maximum(i - na, 0), 0))],
        out_specs=pl.BlockSpec((tm, d), lambda i: (i, 0)),
        out_shape=jax.ShapeDtypeStruct((ta + b.shape[0], d), a.dtype),
        compiler_params=_params(("parallel",)),
    )(a, b)


def _assemble_dz(lat_parts, ctx_parts, both_parts, width, name):
    t = next(p.shape[0] for p in lat_parts if p is not None)
    l_ctx = next(p.shape[0] for p in ctx_parts if p is not None)
    tm = _pick(int(np.gcd(t, l_ctx)), (256, 128, 64, 32, 16, 8))
    nt, nl = t // tm, l_ctx // tm
    plan, args, in_specs, off = [], [], [], 0
    lat_spec = lambda w: pl.BlockSpec((tm, w), lambda i: (jnp.minimum(i, nt - 1), 0))
    ctx_spec = lambda w: pl.BlockSpec((tm, w), lambda i: (jnp.maximum(i - nt, 0), 0))
    all_spec = lambda w: pl.BlockSpec((tm, w), lambda i: (i, 0))
    for lat, ctx, both in zip(lat_parts, ctx_parts, both_parts):
        if both:
            w = both[0].shape[1]
            plan.append(("both", off, w, len(args), len(both)))
            args += both
            in_specs += [all_spec(w)] * len(both)
        else:
            w = (lat if lat is not None else ctx).shape[1]
            plan.append(("split", off, w, len(args), (lat is not None, ctx is not None)))
            for part, spec in ((lat, lat_spec), (ctx, ctx_spec)):
                if part is not None:
                    args.append(part)
                    in_specs.append(spec(w))
        off += w
    n_in = len(args)

    def body(*refs):
        out_ref = refs[n_in]
        is_ctx = pl.program_id(0) >= nt
        for kind, o, w, first, info in plan:
            if kind == "both":
                val = refs[first][...]
                for k in range(1, info):
                    val = val + refs[first + k][...]
            else:
                has_lat, has_ctx = info
                zero = jnp.zeros((tm, w), F32)
                lat = refs[first][...] if has_lat else zero
                ctx = refs[first + int(has_lat)][...] if has_ctx else zero
                val = jnp.where(is_ctx, ctx, lat)
            out_ref[:, o:o + w] = val.astype(BF16)
        if off < width:
            out_ref[:, off:width] = jnp.zeros((tm, width - off), BF16)

    return pl.pallas_call(
        body, name=name, grid=(nt + nl,), in_specs=in_specs,
        out_specs=pl.BlockSpec((tm, width), lambda i: (i, 0)),
        out_shape=jax.ShapeDtypeStruct((t + l_ctx, width), BF16),
        compiler_params=_params(("parallel",)),
    )(*args)


def _final_loss(x, g, target, name):
    t, d = x.shape
    tm = _pick(t, (256, 128, 64, 32, 16, 8))

    def body(x_ref, g_ref, t_ref, dx_ref, loss_ref, dg_ref):
        @pl.when(pl.program_id(0) == 0)
        def _():
            loss_ref[...] = jnp.zeros_like(loss_ref)
            dg_ref[...] = jnp.zeros_like(dg_ref)

        xx, gg = x_ref[...], g_ref[...]
        r = lax.rsqrt(jnp.mean(xx * xx, axis=-1, keepdims=True) + RMS_EPS)
        nz = xx * r
        err = nz * gg - t_ref[...]
        loss_ref[...] += jnp.sum(err * err, axis=0, keepdims=True) * (0.5 / d)
        dout = err * (1.0 / d)
        dg_ref[...] += jnp.sum(dout * nz, axis=0, keepdims=True)
        dn = dout * gg
        dx_ref[...] = r * (dn - nz * jnp.mean(dn * nz, axis=-1, keepdims=True))

    row = pl.BlockSpec((tm, d), lambda i: (i, 0))
    vec = pl.BlockSpec((1, d), lambda i: (0, 0))
    return pl.pallas_call(
        body, name=name, grid=(t // tm,), in_specs=[row, vec, row], out_specs=[row, vec, vec],
        out_shape=[jax.ShapeDtypeStruct((t, d), F32), jax.ShapeDtypeStruct((1, d), F32),
                   jax.ShapeDtypeStruct((1, d), F32)],
        compiler_params=_params(("arbitrary",)),
    )(x, g, target)


def _adaln_fwd(craw, w_mod, b_cols, name):
    lyr, d, nc = w_mod.shape

    def body(c_ref, w_ref, b_ref, out_ref):
        out_ref[...] = _bdot(_silu(c_ref[...]), w_ref[...]) + b_ref[...]

    return pl.pallas_call(
        body, name=name, grid=(lyr,),
        in_specs=[_full(craw.shape), pl.BlockSpec((None, d, nc), lambda l: (l, 0, 0)),
                  pl.BlockSpec((None, 1, nc), lambda l: (l, 0, 0))],
        out_specs=pl.BlockSpec((None, 16, nc), lambda l: (l, 0, 0)),
        out_shape=jax.ShapeDtypeStruct((lyr, 16, nc), F32),
        compiler_params=_params(("parallel",)),
    )(craw, w_mod, b_cols)


def _adaln_bwd(craw, cs_t, dmm_cols, w_mod, name):
    lyr, d, nc = w_mod.shape

    def body(c_ref, cst_ref, dmm_ref, w_ref, gw_ref, dc_ref):
        dmm = dmm_ref[...]
        gw_ref[...] = _bdot(cst_ref[...], dmm)
        cc = c_ref[...]
        sg = jax.nn.sigmoid(cc)
        dc_ref[...] = _bdot(dmm, w_ref[...], NT) * (sg * (1.0 + cc * (1.0 - sg)))

    wspec = pl.BlockSpec((None, d, nc), lambda l: (l, 0, 0))
    return pl.pallas_call(
        body, name=name, grid=(lyr,),
        in_specs=[_full(craw.shape), _full(cs_t.shape), pl.BlockSpec((None, 16, nc), lambda l: (l, 0, 0)), wspec],
        out_specs=[wspec, pl.BlockSpec((None, 16, d), lambda l: (l, 0, 0))],
        out_shape=[jax.ShapeDtypeStruct((lyr, d, nc), F32), jax.ShapeDtypeStruct((lyr, 16, d), F32)],
        compiler_params=_params(("parallel",)),
    )(craw, cs_t, dmm_cols, w_mod)


def _rope_tables(t):
    rows = np.repeat(np.arange(t // GRID_W, dtype=np.float32), GRID_W)
    cols = np.tile(np.arange(GRID_W, dtype=np.float32), t // GRID_W)
    n = A_HEAD_DIM // 4
    freqs = (ROPE_BASE ** (-np.arange(n, dtype=np.float32) / n)).astype(np.float32)
    ang_r, ang_c = (rows[:, None] * freqs).astype(np.float32), (cols[:, None] * freqs).astype(np.float32)
    cr, sr, cc, sc = np.cos(ang_r), np.sin(ang_r), np.cos(ang_c), np.sin(ang_c)
    cos = np.concatenate([cr, cr, cc, cc] * 2, axis=-1).astype(np.float32)
    sin = np.concatenate([-sr, sr, -sc, sc] * 2, axis=-1).astype(np.float32)
    return jnp.asarray(cos), jnp.asarray(sin)


def _rope(xt, cos, sin, adjoint, name):
    t, w = xt.shape
    tb = _pick(t, (512, 256, 128))
    rep = w // cos.shape[1]

    def body(x_ref, c_ref, s_ref, o_ref):
        xx = x_ref[...]
        cc = jnp.concatenate([c_ref[...]] * rep, axis=1) if rep > 1 else c_ref[...]
        ss = jnp.concatenate([s_ref[...]] * rep, axis=1) if rep > 1 else s_ref[...]
        low = (lax.broadcasted_iota(jnp.int32, xx.shape, 1) % 32) < 16

        def partner(v):
            return jnp.where(low, pltpu.roll(v, w - 16, 1), pltpu.roll(v, 16, 1))

        if adjoint:
            o_ref[...] = xx * cc + partner(xx * ss)
        else:
            o_ref[...] = xx * cc + partner(xx) * ss

    blk = pl.BlockSpec((tb, w), lambda i: (i, 0))
    tab = pl.BlockSpec((tb, cos.shape[1]), lambda i: (i, 0))
    return pl.pallas_call(
        body, name=name, grid=(t // tb,), in_specs=[blk, tab, tab], out_specs=blk,
        out_shape=jax.ShapeDtypeStruct((t, w), F32), compiler_params=_params(("parallel",)),
    )(xt, cos, sin)


def _attn_probs(q, kb, kc, sink, n, t):
    scale = A_HEAD_DIM ** -0.5
    s1 = _bdot(q, kb, NT) * scale
    s2 = _bdot(q, kc, NT) * scale
    qpos = n * WINDOW + lax.broadcasted_iota(jnp.int32, s1.shape, 0) % WINDOW
    kpos = (n - 1) * WINDOW + lax.broadcasted_iota(jnp.int32, s1.shape, 1)
    valid = (kpos >= 0) & (kpos < t) & (jnp.abs(kpos - qpos) <= WINDOW)
    s1 = jnp.where(valid, s1, -jnp.inf)
    mx = jnp.maximum(jnp.maximum(jnp.max(s1, axis=-1, keepdims=True), jnp.max(s2, axis=-1, keepdims=True)), sink)
    p1, p2, ps = jnp.exp(s1 - mx), jnp.exp(s2 - mx), jnp.exp(sink - mx)
    inv = 1.0 / (jnp.sum(p1, axis=-1, keepdims=True) + jnp.sum(p2, axis=-1, keepdims=True) + ps)
    return p1 * inv, p2 * inv, ps * inv


def _sink_rows(sink_ref):
    return jnp.concatenate([jnp.broadcast_to(sink_ref[r], (WINDOW, 1)) for r in range(A_REP)], axis=0)


def _attn_fwd(q, kp, vp, kc, vc, sink, name):
    hq, t, dh = q.shape
    nb = t // WINDOW
    lc = kc.shape[1]
    rows = A_REP * WINDOW

    def body(q_ref, k_ref, v_ref, kc_ref, vc_ref, sink_ref, o_ref):
        n = pl.program_id(1)
        start = pl.multiple_of(n * WINDOW, WINDOW)
        kb, vb = k_ref[pl.ds(start, 3 * WINDOW), :], v_ref[pl.ds(start, 3 * WINDOW), :]
        p1, p2, _ = _attn_probs(q_ref[...].reshape(rows, dh), kb, kc_ref[...], _sink_rows(sink_ref), n, t)
        o_ref[...] = (_bdot(p1, vb) + _bdot(p2, vc_ref[...])).reshape(A_REP, WINDOW, dh)

    qblk = pl.BlockSpec((A_REP, WINDOW, dh), lambda g, n: (g, n, 0))
    kfull = pl.BlockSpec((None, t + 2 * WINDOW, dh), lambda g, n: (g, 0, 0))
    cfull = pl.BlockSpec((None, lc, dh), lambda g, n: (g, 0, 0))
    return pl.pallas_call(
        body, name=name, grid=(hq // A_REP, nb),
        in_specs=[qblk, kfull, kfull, cfull, cfull, pl.BlockSpec((A_REP, 1, 1), lambda g, n: (g, 0, 0))],
        out_specs=qblk, out_shape=jax.ShapeDtypeStruct((hq, t, dh), F32),
        compiler_params=_params(("parallel", "parallel")),
    )(q, kp, vp, kc, vc, sink)


def _attn_bwd(q, kp, vp, kc, vc, sink, o, do, name):
    hq, t, dh = q.shape
    nb = t // WINDOW
    lc = kc.shape[1]
    scale = A_HEAD_DIM ** -0.5
    rows = A_REP * WINDOW

    def body(q_ref, k_ref, v_ref, kc_ref, vc_ref, sink_ref, o_ref, do_ref,
             dq_ref, dk_ref, dv_ref, dkc_ref, dvc_ref, dsink_ref):
        n = pl.program_id(1)

        @pl.when(n == 0)
        def _():
            dk_ref[...] = jnp.zeros_like(dk_ref)
            dv_ref[...] = jnp.zeros_like(dv_ref)
            dkc_ref[...] = jnp.zeros_like(dkc_ref)
            dvc_ref[...] = jnp.zeros_like(dvc_ref)
            dsink_ref[...] = jnp.zeros_like(dsink_ref)

        start = pl.multiple_of(n * WINDOW, WINDOW)
        band = pl.ds(start, 3 * WINDOW)
        qq, kb, vb, kcc, vcc = q_ref[...].reshape(rows, dh), k_ref[band, :], v_ref[band, :], kc_ref[...], vc_ref[...]
        p1, p2, ps = _attn_probs(qq, kb, kcc, _sink_rows(sink_ref), n, t)
        dout = do_ref[...].reshape(rows, dh)
        delta = jnp.sum(dout * o_ref[...].reshape(rows, dh), axis=-1, keepdims=True)
        ds1 = p1 * (_bdot(dout, vb, NT) - delta)
        ds2 = p2 * (_bdot(dout, vcc, NT) - delta)
        dq_ref[...] = ((_bdot(ds1, kb) + _bdot(ds2, kcc)) * scale).reshape(A_REP, WINDOW, dh)
        dk_ref[band, :] += _bdot(ds1.T, qq) * scale
        dv_ref[band, :] += _bdot(p1.T, dout)
        dkc_ref[...] += _bdot(ds2.T, qq) * scale
        dvc_ref[...] += _bdot(p2.T, dout)
        dsink_ref[...] += jnp.sum((-ps * delta).reshape(A_REP, WINDOW, 1), axis=1, keepdims=True)

    qblk = pl.BlockSpec((A_REP, WINDOW, dh), lambda g, n: (g, n, 0))
    kfull = pl.BlockSpec((None, t + 2 * WINDOW, dh), lambda g, n: (g, 0, 0))
    cfull = pl.BlockSpec((None, lc, dh), lambda g, n: (g, 0, 0))
    return pl.pallas_call(
        body, name=name, grid=(hq // A_REP, nb),
        in_specs=[qblk, kfull, kfull, cfull, cfull, pl.BlockSpec((A_REP, 1, 1), lambda g, n: (g, 0, 0)), qblk, qblk],
        out_specs=[qblk, kfull, kfull, cfull, cfull, pl.BlockSpec((A_REP, 8, 128), lambda g, n: (g, 0, 0))],
        out_shape=[jax.ShapeDtypeStruct(q.shape, F32), jax.ShapeDtypeStruct(kp.shape, F32),
                   jax.ShapeDtypeStruct(kp.shape, F32), jax.ShapeDtypeStruct(kc.shape, F32),
                   jax.ShapeDtypeStruct(kc.shape, F32), jax.ShapeDtypeStruct((hq, 8, 128), F32)],
        compiler_params=_params(("parallel", "arbitrary")),
    )(q, kp, vp, kc, vc, sink, o, do)


def _gate_fwd(zg, w2, b2, name):
    m = zg.shape[0]
    n = w2.shape[1]
    tm = _pick(m, (512, 256, 128, 64, 32, 16, 8))

    def body(z_ref, w_ref, b_ref, o_ref):
        o_ref[...] = jax.nn.log_sigmoid(_bdot(z_ref[...], w_ref[...]) + b_ref[...]) / B_GATE_NORM

    return pl.pallas_call(
        body, name=name, grid=(m // tm,),
        in_specs=[pl.BlockSpec((tm, zg.shape[1]), lambda i: (i, 0)), _full(w2.shape), _full(b2.shape)],
        out_specs=pl.BlockSpec((tm, n), lambda i: (i, 0)), out_shape=jax.ShapeDtypeStruct((m, n), F32),
        compiler_params=_params(("parallel",)),
    )(zg, w2, b2)


def _gate_bwd(zg, w2, b2, dla, name):
    m, rk = zg.shape
    n = w2.shape[1]
    tm = _pick(m, (512, 256, 128, 64, 32, 16, 8))

    def body(z_ref, w_ref, b_ref, d_ref, dz_ref, dw_ref, db_ref):
        @pl.when(pl.program_id(0) == 0)
        def _():
            dw_ref[...] = jnp.zeros_like(dw_ref)
            db_ref[...] = jnp.zeros_like(db_ref)

        zz, ww = z_ref[...], w_ref[...]
        pre = _bdot(zz, ww) + b_ref[...]
        dpre = d_ref[...] * (1.0 / B_GATE_NORM) * jax.nn.sigmoid(-pre)
        dz_ref[...] = _bdot(dpre, ww, NT)
        dw_ref[...] += _bdot(zz.T, dpre)
        db_ref[...] += jnp.sum(dpre, axis=0, keepdims=True)

    return pl.pallas_call(
        body, name=name, grid=(m // tm,),
        in_specs=[pl.BlockSpec((tm, rk), lambda i: (i, 0)), _full(w2.shape), _full(b2.shape),
                  pl.BlockSpec((tm, n), lambda i: (i, 0))],
        out_specs=[pl.BlockSpec((tm, rk), lambda i: (i, 0)), _full(w2.shape), _full(b2.shape)],
        out_shape=[jax.ShapeDtypeStruct((m, rk), F32), jax.ShapeDtypeStruct(w2.shape, F32),
                   jax.ShapeDtypeStruct(b2.shape, F32)],
        compiler_params=_params(("arbitrary",)),
    )(zg, w2, b2, dla)


def _chunk_order(step, n_x_chunks, n_chunks, reverse):
    n_c = n_chunks - n_x_chunks
    if reverse:
        return jnp.where(step < n_c, n_chunks - 1 - step, n_chunks - 1 - step)
    return jnp.where(step < n_c, n_x_chunks + step, step - n_c)


def _tri(reverse, transpose=False):
    i = lax.broadcasted_iota(jnp.int32, (B_CHUNK, B_CHUNK), 0)
    j = lax.broadcasted_iota(jnp.int32, (B_CHUNK, B_CHUNK), 1)
    if transpose:
        i, j = j, i
    return (j >= i) if reverse else (j <= i)


def _gla_chunk(q, k, la, reverse):
    g = _dot_01(_tri(reverse), la)
    last = 0 if reverse else B_CHUNK - 1
    gl = g[last:last + 1, :]
    eg, eng, egl = jnp.exp(g), jnp.exp(-g), jnp.exp(gl - g)
    decay_col = jnp.exp(jnp.sum(la.T, axis=1, keepdims=True))
    return q * (B_DK ** -0.5) * eg, k * eng, k * egl, eg, eng, egl, decay_col


def _head_of(shape, axis, width):
    return lax.broadcasted_iota(jnp.int32, shape, axis) // width


def _gla_fwd(q, k, v, la_f, la_b, n_x, name):
    tc, wk = q.shape
    wv = v.shape[1]
    hh = B_HEADS
    dk, dv = wk // hh, wv // hh
    nc, nxc = tc // B_CHUNK, n_x // B_CHUNK
    orders = [functools.partial(_chunk_order, n_x_chunks=nxc, n_chunks=nc, reverse=rev) for rev in (False, True)]

    def body(*refs):
        ins, outs, s_refs = refs[:8], refs[8:12], refs[12:]

        @pl.when(pl.program_id(0) == 0)
        def _():
            for s_ref in s_refs:
                s_ref[...] = jnp.zeros_like(s_ref)

        lane_head = _head_of((B_CHUNK, wk), 1, dk)
        row_head = _head_of((wk, dv), 0, dk)
        for di, reverse in enumerate((False, True)):
            q_ref, k_ref, v_ref, la_ref = ins[4 * di:4 * di + 4]
            o_ref, s_save_ref = outs[2 * di:2 * di + 2]
            s_ref = s_refs[di]
            qt, kt, ke, _, _, _, decay_col = _gla_chunk(q_ref[...], k_ref[...], la_ref[...], reverse)
            ke_t = ke.T
            s_prev = s_ref[...]
            update = jnp.zeros_like(s_prev)
            for h in range(hh):
                vv = v_ref[:, h * dv:(h + 1) * dv]
                qm = jnp.where(lane_head == h, qt, 0.0)
                att = jnp.where(_tri(reverse), _bdot(qm, kt, NT), 0.0)
                o_ref[:, h * dv:(h + 1) * dv] = _bdot(att, vv) + _bdot(qm, s_prev)
                update = jnp.where(row_head == h, _bdot(ke_t, vv), update)
            s_save_ref[...] = s_prev
            s_ref[...] = decay_col * s_prev + update

    def blk(w, order):
        return pl.BlockSpec((B_CHUNK, w), lambda s: (order(s), 0))

    def sblk(order):
        return pl.BlockSpec((None, wk, dv), lambda s: (order(s), 0, 0))

    in_specs, out_specs = [], []
    for order in orders:
        in_specs += [blk(wk, order), blk(wk, order), blk(wv, order), blk(wk, order)]
        out_specs += [blk(wv, order), sblk(order)]
    o_shape, s_shape = jax.ShapeDtypeStruct((tc, wv), F32), jax.ShapeDtypeStruct((nc, wk, dv), F32)
    return pl.pallas_call(
        body, name=name, grid=(nc,), in_specs=in_specs, out_specs=out_specs,
        out_shape=[o_shape, s_shape, o_shape, s_shape],
        scratch_shapes=[pltpu.VMEM((wk, dv), F32)] * 2,
        compiler_params=_params(("arbitrary",)),
    )(q, k, v, la_f, q, k, v, la_b)


def _gla_bwd(q, k, v, la_f, la_b, s_f, s_b, do, n_x, name):
    tc, wk = q.shape
    wv = v.shape[1]
    hh = B_HEADS
    dk, dv = wk // hh, wv // hh
    nc, nxc = tc // B_CHUNK, n_x // B_CHUNK
    orders = [functools.partial(lambda s, rev: _chunk_order(nc - 1 - s, nxc, nc, rev), rev=rev) for rev in (False, True)]

    def body(*refs):
        ins, outs, ds_refs = refs[:12], refs[12:20], refs[20:]

        @pl.when(pl.program_id(0) == 0)
        def _():
            for ds_ref in ds_refs:
                ds_ref[...] = jnp.zeros_like(ds_ref)

        lane_head = _head_of((B_CHUNK, wk), 1, dk)
        row_head = _head_of((wk, dv), 0, dk)
        for di, reverse in enumerate((False, True)):
            q_ref, k_ref, v_ref, la_ref, s_save_ref, do_ref = ins[6 * di:6 * di + 6]
            dq_ref, dk_ref, dv_ref, dla_ref = outs[4 * di:4 * di + 4]
            ds_ref = ds_refs[di]
            mask = _tri(reverse)
            last = 0 if reverse else B_CHUNK - 1
            is_last = lax.broadcasted_iota(jnp.int32, (B_CHUNK, wk), 0) == last
            la = la_ref[...]
            qt, kt, ke, eg, eng, egl, decay_col = _gla_chunk(q_ref[...], k_ref[...], la, reverse)
            qt_t = qt.T
            s_prev, ds_new = s_save_ref[...], ds_ref[...]
            dqt, dkt, dke = jnp.zeros_like(qt), jnp.zeros_like(qt), jnp.zeros_like(qt)
            ds_add = jnp.zeros_like(ds_new)
            for h in range(hh):
                cols = slice(h * dv, (h + 1) * dv)
                vv, dout = v_ref[:, cols], do_ref[:, cols]
                mine = lane_head == h
                qm, km = jnp.where(mine, qt, 0.0), jnp.where(mine, ke, 0.0)
                att = jnp.where(mask, _bdot(qm, kt, NT), 0.0)
                datt = jnp.where(mask, _bdot(dout, vv, NT), 0.0)
                dv_ref[:, cols] = _bdot(att.T, dout) + _bdot(km, ds_new)
                dqt = jnp.where(mine, _bdot(datt, kt) + _bdot(dout, s_prev, NT), dqt)
                dkt = jnp.where(mine, _bdot(datt.T, qt), dkt)
                dke = jnp.where(mine, _bdot(vv, ds_new, NT), dke)
                ds_add = jnp.where(row_head == h, _bdot(qt_t, dout), ds_add)
            ddecay_row = jnp.sum((ds_new * s_prev).T, axis=0, keepdims=True)
            decay_row = jnp.exp(jnp.sum(la, axis=0, keepdims=True))
            ds_ref[...] = decay_col * ds_new + ds_add
            dq_ref[...] = dqt * (B_DK ** -0.5) * eg
            dk_ref[...] = dkt * eng + dke * egl
            dgl = jnp.sum(dke * ke, axis=0, keepdims=True) + ddecay_row * decay_row
            dg = dqt * qt - dkt * kt - dke * ke + jnp.where(is_last, dgl, 0.0)
            dla_ref[...] = _dot_01(_tri(reverse, transpose=True), dg)

    def blk(w, order):
        return pl.BlockSpec((B_CHUNK, w), lambda s: (order(s), 0))

    in_specs, out_specs = [], []
    for order in orders:
        in_specs += [blk(wk, order), blk(wk, order), blk(wv, order), blk(wk, order),
                     pl.BlockSpec((None, wk, dv), lambda s, order=order: (order(s), 0, 0)), blk(wv, order)]
        out_specs += [blk(wk, order), blk(wk, order), blk(wv, order), blk(wk, order)]
    k_shape, v_shape = jax.ShapeDtypeStruct((tc, wk), F32), jax.ShapeDtypeStruct((tc, wv), F32)
    return pl.pallas_call(
        body, name=name, grid=(nc,), in_specs=in_specs, out_specs=out_specs,
        out_shape=[k_shape, k_shape, v_shape, k_shape] * 2,
        scratch_shapes=[pltpu.VMEM((wk, dv), F32)] * 2,
        compiler_params=_params(("arbitrary",)),
    )(q, k, v, la_f, s_f, do, q, k, v, la_b, s_b, do)


def _gla_out_fwd(o_f, o_b, r, g, name):
    t = r.shape[0]
    dv = g.shape[1]
    hh = r.shape[1] // dv
    tb = _pick(t, (256, 128, 64))

    def body(of_ref, ob_ref, r_ref, g_ref, out_ref):
        for h in range(hh):
            cols = slice(h * dv, (h + 1) * dv)
            o = of_ref[:, cols] + ob_ref[:, cols]
            rs = lax.rsqrt(jnp.mean(o * o, axis=-1, keepdims=True) + RMS_EPS)
            out_ref[:, cols] = (o * rs) * g_ref[...] * _silu(r_ref[:, cols])

    rblk = pl.BlockSpec((tb, hh * dv), lambda i: (i, 0))
    return pl.pallas_call(
        body, name=name, grid=(t // tb,), in_specs=[rblk, rblk, rblk, _full(g.shape)], out_specs=rblk,
        out_shape=jax.ShapeDtypeStruct((t, hh * dv), F32), compiler_params=_params(("parallel",)),
    )(o_f, o_b, r, g)


def _gla_out_bwd(o_f, o_b, r, g, dout, name):
    tc = o_f.shape[0]
    t = r.shape[0]
    dv = g.shape[1]
    hh = r.shape[1] // dv
    tb = _pick(int(np.gcd(t, tc)), (256, 128, 64))
    nt = t // tb

    def body(of_ref, ob_ref, r_ref, g_ref, d_ref, do_ref, dr_ref, dg_ref):
        i = pl.program_id(0)

        @pl.when(i == 0)
        def _():
            dg_ref[...] = jnp.zeros_like(dg_ref)

        @pl.when(i >= nt)
        def _():
            do_ref[...] = jnp.zeros_like(do_ref)

        @pl.when(i < nt)
        def _():
            gg = g_ref[...]
            for h in range(hh):
                cols = slice(h * dv, (h + 1) * dv)
                o = of_ref[:, cols] + ob_ref[:, cols]
                rs = lax.rsqrt(jnp.mean(o * o, axis=-1, keepdims=True) + RMS_EPS)
                nz = o * rs
                rr, dd = r_ref[:, cols], d_ref[:, cols]
                sg = jax.nn.sigmoid(rr)
                dr_ref[:, cols] = dd * nz * gg * (sg * (1.0 + rr * (1.0 - sg)))
                dy = dd * (rr * sg)
                dg_ref[...] += jnp.sum(dy * nz, axis=0, keepdims=True)
                dn = dy * gg
                do_ref[:, cols] = rs * (dn - nz * jnp.mean(dn * nz, axis=-1, keepdims=True))

    oblk = pl.BlockSpec((tb, hh * dv), lambda i: (i, 0))
    rblk = pl.BlockSpec((tb, hh * dv), lambda i: (jnp.minimum(i, nt - 1), 0))
    return pl.pallas_call(
        body, name=name, grid=(tc // tb,), in_specs=[oblk, oblk, rblk, _full(g.shape), rblk],
        out_specs=[oblk, rblk, _full(g.shape)],
        out_shape=[jax.ShapeDtypeStruct(o_f.shape, F32), jax.ShapeDtypeStruct(r.shape, F32),
                   jax.ShapeDtypeStruct(g.shape, F32)],
        compiler_params=_params(("arbitrary",)),
    )(o_f, o_b, r, g, dout)


def _pool_window(i, tb, t):
    return pl.multiple_of(jnp.clip(i * tb - POOL_PAD, 0, t - (tb + 2 * POOL_PAD)), 8)


def _pool_band(half, i, tb, start, adjoint):
    pos = i * tb + lax.broadcasted_iota(jnp.int32, (tb, tb + 2 * POOL_PAD), 0)
    tok = start + lax.broadcasted_iota(jnp.int32, (tb, tb + 2 * POOL_PAD), 1)
    if adjoint:
        return (tok > pos - half) & (tok <= pos + half)
    return (tok >= pos - half) & (tok < pos + half)


def _pool_count(pos, half, t):
    return (jnp.minimum(pos + half, t) - jnp.maximum(pos - half, 0)).astype(F32)


def _pool_fwd(h, w_pool, pool_scale, res, mods, km, name):
    t, d = res.shape
    ng, gw = w_pool.shape[0], w_pool.shape[1]
    tb = _pick(t, (256, 128, 64))

    def body(h_ref, w_ref, ps_ref, res_ref, mods_ref, out_ref, pooled_ref, ypre_ref):
        gi, i = pl.program_id(0), pl.program_id(1)
        half = jnp.left_shift(1, gi)
        start = _pool_window(i, tb, t)
        win = h_ref[pl.ds(start, tb + 2 * POOL_PAD), :]
        total = _dot_01(_pool_band(half, i, tb, start, False), win)
        pos = i * tb + lax.broadcasted_iota(jnp.int32, (tb, 1), 0)
        pooled = total / _pool_count(pos, half, t) - h_ref[pl.ds(pl.multiple_of(i * tb, tb), tb), :]
        ypre = _bdot(pooled, w_ref[...])
        pooled_ref[...] = pooled.astype(BF16)
        ypre_ref[...] = ypre
        out_ref[...] = res_ref[...] + mods_ref[0, km:km + 1, :] * (ypre * ps_ref[...])

    tile = pl.BlockSpec((tb, gw), lambda gi, i: (i, gi))
    return pl.pallas_call(
        body, name=name, grid=(ng, t // tb),
        in_specs=[pl.BlockSpec((t, gw), lambda gi, i: (0, gi)),
                  pl.BlockSpec((None, gw, gw), lambda gi, i: (gi, 0, 0)),
                  pl.BlockSpec((1, gw), lambda gi, i: (0, gi)), tile,
                  pl.BlockSpec((2, 16, gw), lambda gi, i: (0, 0, gi))],
        out_specs=[tile, tile, tile],
        out_shape=[jax.ShapeDtypeStruct((t, d), F32), jax.ShapeDtypeStruct((t, d), BF16),
                   jax.ShapeDtypeStruct((t, d), F32)],
        compiler_params=_params(("parallel", "parallel")),
    )(h, w_pool, pool_scale, res, mods)


def _pool_bwd(dxp, w_pool, pool_scale, pooled, ypre, mods, km, name):
    t, d = pooled.shape
    ng, gw = w_pool.shape[0], w_pool.shape[1]
    tb = _pick(t, (256, 128, 64))

    def body(dxp_ref, w_ref, ps_ref, pooled_ref, ypre_ref, mods_ref, dh_ref, dw_ref, acc_ref):
        gi, i = pl.program_id(0), pl.program_id(1)

        @pl.when(i == 0)
        def _():
            dw_ref[...] = jnp.zeros_like(dw_ref)
            acc_ref[...] = jnp.zeros_like(acc_ref)

        half = jnp.left_shift(1, gi)
        mod, ps = mods_ref[0, km:km + 1, :], ps_ref[...]
        start = _pool_window(i, tb, t)
        dwin = dxp_ref[pl.ds(start, tb + 2 * POOL_PAD), :]
        dpooled = _bdot(dwin * (mod * ps), w_ref[...], NT)
        pos = start + lax.broadcasted_iota(jnp.int32, (tb + 2 * POOL_PAD, 1), 0)
        spread = _dot_01(_pool_band(half, i, tb, start, True), dpooled / _pool_count(pos, half, t))
        dxc, yp = dxp_ref[pl.ds(pl.multiple_of(i * tb, tb), tb), :], ypre_ref[...]
        dh_ref[...] = spread - _bdot(dxc * (mod * ps), w_ref[...], NT)
        dw_ref[...] += _bdot(pooled_ref[...].astype(F32).T, dxc * (mod * ps))
        acc_ref[0:1, :] += jnp.sum(dxc * yp * mod, axis=0, keepdims=True)
        acc_ref[1:2, :] += jnp.sum(dxc * yp * ps, axis=0, keepdims=True)

    tile = pl.BlockSpec((tb, gw), lambda gi, i: (i, gi))
    wblk = pl.BlockSpec((None, gw, gw), lambda gi, i: (gi, 0, 0))
    return pl.pallas_call(
        body, name=name, grid=(ng, t // tb),
        in_specs=[pl.BlockSpec((t, gw), lambda gi, i: (0, gi)), wblk,
                  pl.BlockSpec((1, gw), lambda gi, i: (0, gi)), tile, tile,
                  pl.BlockSpec((2, 16, gw), lambda gi, i: (0, 0, gi))],
        out_specs=[tile, wblk, pl.BlockSpec((8, gw), lambda gi, i: (0, gi))],
        out_shape=[jax.ShapeDtypeStruct((t, d), F32), jax.ShapeDtypeStruct(w_pool.shape, F32),
                   jax.ShapeDtypeStruct((8, d), F32)],
        compiler_params=_params(("arbitrary", "arbitrary")),
    )(dxp, w_pool, pool_scale, pooled, ypre, mods)


def _adamw(w, g, m, v, name):
    r, c = w.shape
    tr = _pick(r, (512, 352, 256, 128, 64, 32, 16, 8))
    c1 = 1.0 / (1.0 - ADAM_B1 ** ADAM_STEP)
    c2 = 1.0 / (1.0 - ADAM_B2 ** ADAM_STEP)

    def body(w_ref, g_ref, m_ref, v_ref, d_ref, nm_ref, nv_ref):
        gg = g_ref[...]
        nm = ADAM_B1 * m_ref[...] + (1.0 - ADAM_B1) * gg
        nv = ADAM_B2 * v_ref[...] + (1.0 - ADAM_B2) * (gg * gg)
        nm_ref[...] = nm
        nv_ref[...] = nv
        d_ref[...] = -ADAM_LR * ((nm * c1) / (jnp.sqrt(nv * c2) + ADAM_EPS) + ADAM_WD * w_ref[...])

    blk = pl.BlockSpec((tr, c), lambda i: (i, 0))
    shp = jax.ShapeDtypeStruct((r, c), F32)
    return pl.pallas_call(
        body, name=name, grid=(r // tr,), in_specs=[blk] * 4, out_specs=[blk] * 3, out_shape=[shp] * 3,
        compiler_params=_params(("parallel",)),
    )(w, g, m, v)


def _heads(z, n_heads):
    m = z.shape[0]
    return z.reshape(m, n_heads, -1).transpose(1, 0, 2)


def _unheads(zh):
    return zh.transpose(1, 0, 2).reshape(zh.shape[1], -1)


def _pad_rows(a, n):
    return jnp.pad(a, ((0, 0), (n, n), (0, 0))) if a.ndim == 3 else jnp.pad(a, ((n, n), (0, 0)))


def _local_step(x, ctx, target, mods, wts, fetch, emit):
    t, d = x.shape
    l_ctx = ctx.shape[0]
    tc = t + l_ctx
    norm_g = wts["norm_g"]
    ng = lambda l, k: norm_g[l, k][None, :]
    grads = {}
    dmods = [[[None] * N_MOD for _ in range(2)] for _ in range(2)]
    dnorm = [[None] * 3 for _ in range(2)]

    def ffn_fwd(z, h, l, kbase, wi, wo, n_x, tag, nxt):
        au, act = _ffn_up(h, wi, 0, f"ffn_up_{tag}")
        wo = wo(act) if callable(wo) else wo
        outs = _mm_resid(act, wo, 0, z, mods[l], kbase + 2, 0.5, n_x, f"ffn_down_{tag}", nxt=nxt)
        return outs[0], (z, h, au, act, outs[1], wi, wo), (outs[2] if nxt is not None else None)

    def ffn_bwd(dz_new, saved, l, kbase, g, n_x, tag, stage, split=False):
        z, h, au, act, y, wi, wo = saved
        dy, acc_gate = _resid_bwd(dz_new, y, mods[l], kbase + 2, 0.5, n_x, f"resid_bwd_{tag}")
        dau = _ffn_down_bwd(dy, wo, 0, au, f"ffn_down_bwd_{tag}")
        dwo = _mm_tn(act, dy, BF16, f"dwo_{tag}")
        if split:
            token = emit(stage, [dwo])
            dwi_t = _mm_tn(dau, h, BF16, f"dwi_{tag}", dep=token)
            token = emit(stage + 1, [dwi_t])
        else:
            dwi_t = _mm_tn(dau, h, BF16, f"dwi_{tag}")
            token = emit(stage, [dwi_t, dwo])
        dh = _mm([(dau, wi, 0, 0)], NN, d, F32, f"dh_{tag}", tm_pref=ROW_TILES, dep=token)
        dz, acc_mod = _modulate_bwd(z, dh, dz_new, mods[l], g, kbase + 1, n_x, f"mod_bwd_{tag}", latent_only=split)
        return dz, acc_mod, acc_gate

    def record(l, kbase, k_norm, g, acc_mod, acc_gate, streams):
        total = None
        for s in range(streams):
            dmods[l][s][kbase] = acc_mod[s, 0]
            dmods[l][s][kbase + 1] = acc_mod[s, 1] * g[0]
            if acc_gate is not None:
                dmods[l][s][kbase + 2] = acc_gate[s, 0]
            part = acc_mod[s, 1] * (1.0 + mods[l][s, kbase + 1])
            total = part if total is None else total + part
        dnorm[l][k_norm] = total

    xc0 = _stack_rows(x, ctx, "stack_tokens")
    wi1_0 = fetch(0, None)["wi1_0"]
    h0 = _modulate(xc0, mods[0], ng(0, 0), 0, 1, t, BF16, "mod_l0f1")
    xc1, sv_f1, hc = ffn_fwd(xc0, h0, 0, 0, wi1_0, lambda act: fetch(1, act)["wo1_0"], t, "l0f1",
                             (mods[0], ng(0, 1), 3, 4, BF16))
    w_in_t = fetch(2, hc)["w_in_t"]
    n_proj = w_in_t.shape[1]
    zall = _mm([(hc, w_in_t, 0, 0)], NT, n_proj, F32, "proj", tm_pref=ROW_TILES,
               tn_pref=(n_proj,))
    offs = np.cumsum((0,) + PROJ_SIZES)
    part = lambda i, rows=slice(None): zall[rows, offs[i]:offs[i + 1]]
    lat, con = slice(0, t), slice(t, tc)
    cos, sin = _rope_tables(t)
    qa = _heads(_rope(part(0, lat), cos, sin, False, "rope_q"), A_HEADS)
    ka = _heads(_rope(part(1, lat), cos, sin, False, "rope_k"), A_KV_HEADS)
    va = _heads(part(2, lat), A_KV_HEADS)
    kca, vca = _heads(part(1, con), A_KV_HEADS), _heads(part(2, con), A_KV_HEADS)
    kap, vap = _pad_rows(ka, WINDOW), _pad_rows(va, WINDOW)
    sink = wts["sink"].reshape(A_HEADS, 1, 1)
    o_a = _attn_fwd(qa, kap, vap, kca, vca, sink, "attn_fwd")

    qb, kb, vb = part(3), part(4), part(5)
    rb = part(6, lat)
    zg = part(7)
    zg_f, zg_b = zg[:, :B_GATE_RANK], zg[:, B_GATE_RANK:]
    w2f, w2b, b2f, b2b = wts["w_a2_f"], wts["w_a2_b"], wts["b_a_f"], wts["b_a_b"]
    la_f = _gate_fwd(zg_f, w2f, b2f, "gate_f")
    la_b = _gate_fwd(zg_b, w2b, b2b, "gate_b")
    o_f, s_f, o_b, s_b = _gla_fwd(qb, kb, vb, la_f, la_b, t, "gla_fwd")
    gla_g = wts["gla_g"]
    go = _gla_out_fwd(o_f, o_b, rb, gla_g, "gla_out")
    cat = jnp.concatenate([_unheads(o_a), go], axis=-1).astype(BF16)
    big = fetch(3, cat)
    w_out, wi2_0, wo2_0 = big["w_out"], big["wi2_0"], big["wo2_0"]
    x2, y_mix0, h2 = _mm_resid(cat, w_out, 0, xc1, mods[0], 5, 1.0, t, "w_out", nxt=(mods[0], ng(0, 2), 6, 7, BF16))
    x3, sv_f2, h3 = ffn_fwd(x2, h2, 0, 6, wi2_0, wo2_0, t, "l0f2", (mods[1], ng(1, 0), 0, 1, BF16))

    big = fetch(4, x3)
    wi1_1, wo1_1, wi2_1, wo2_1 = big["wi1_1"], big["wo1_1"], big["wi2_1"], big["wo2_1"]
    x4, sv_g1, hp = ffn_fwd(x3, h3, 1, 0, wi1_1, wo1_1, t, "l1f1", (mods[1], ng(1, 1), 3, 4, F32))
    w_pool, pool_scale = wts["w_pool"], wts["pool_scale"]
    x5, pooled, ypre = _pool_fwd(hp, w_pool, pool_scale, x4, mods[1], 5, "pool_fwd")
    h5 = _modulate(x5, mods[1], ng(1, 2), 6, 7, t, BF16, "mod_l1f2")
    x6, sv_g2, _ = ffn_fwd(x5, h5, 1, 6, wi2_1, wo2_1, t, "l1f2", None)

    dx6, loss_vec, dfinal_g = _final_loss(x6, wts["final_g"], target, "final_loss")
    grads["final_g"] = dfinal_g[0]

    dx5, acc_mod, acc_gate = ffn_bwd(dx6, sv_g2, 1, 6, ng(1, 2), t, "l1f2", 0)
    record(1, 6, 2, ng(1, 2), acc_mod, acc_gate, 1)
    dhp, dw_pool, acc_pool = _pool_bwd(dx5, w_pool, pool_scale, pooled, ypre, mods[1], 5, "pool_bwd")
    grads["pool_scale"] = acc_pool[0]
    dmods[1][0][5] = acc_pool[1]
    dx4, acc_mod = _modulate_bwd(x4, dhp, dx5, mods[1], ng(1, 1), 4, t, "mod_bwd_l1mix")
    record(1, 3, 1, ng(1, 1), acc_mod, None, 1)
    dx3, acc_mod, acc_gate = ffn_bwd(dx4, sv_g1, 1, 0, ng(1, 0), t, "l1f1", 1)
    record(1, 0, 0, ng(1, 0), acc_mod, acc_gate, 1)

    dx2, acc_mod, acc_gate = ffn_bwd(dx3, sv_f2, 0, 6, ng(0, 2), t, "l0f2", 2)
    record(0, 6, 2, ng(0, 2), acc_mod, acc_gate, 1)
    dymix, acc_gate = _resid_bwd(dx2, y_mix0, mods[0], 5, 1.0, t, "resid_bwd_mix")
    dmods[0][0][5] = acc_gate[0, 0]
    dw_out = _mm_tn(cat, dymix, BF16, "dw_out")
    dcat = _mm([(dymix, w_out, 0, 0)], NT, cat.shape[1], F32, "dcat")
    do_a = _heads(dcat[:, :A_Q], A_HEADS)
    do_full, drb, dgla_g = _gla_out_bwd(o_f, o_b, rb, gla_g, dcat[:, A_Q:], "gla_out_bwd")
    grads["gla_g"] = dgla_g[0]
    dq_f, dk_f, dv_f, dla_f, dq_b, dk_b, dv_b, dla_b = _gla_bwd(qb, kb, vb, la_f, la_b, s_f, s_b, do_full, t, "gla_bwd")
    dzg_f, dw2f, db2f = _gate_bwd(zg_f, w2f, b2f, dla_f, "gate_bwd_f")
    dzg_b, dw2b, db2b = _gate_bwd(zg_b, w2b, b2b, dla_b, "gate_bwd_b")
    grads.update(w_a2_f=dw2f, w_a2_b=dw2b, b_a_f=db2f[0], b_a_b=db2b[0])
    dqa_r, dkap, dvap, dkca, dvca, dsink = _attn_bwd(qa, kap, vap, kca, vca, sink, o_a, do_a, "attn_bwd")
    grads["sink"] = dsink[:, 0, 0]
    dqa = _rope(_unheads(dqa_r), cos, sin, True, "rope_bwd_q")
    dka = _rope(_unheads(dkap[:, WINDOW:WINDOW + t]), cos, sin, True, "rope_bwd_k")
    dva = dvap[:, WINDOW:WINDOW + t]
    dzg = jnp.concatenate([dzg_f, dzg_b, jnp.zeros((tc, n_proj - PROJ_DIM), F32)], axis=-1)
    dzall = _assemble_dz(
        [dqa, dka, _unheads(dva), None, None, None, drb, None],
        [None, _unheads(dkca), _unheads(dvca), None, None, None, None, None],
        [None, None, None, [dq_f, dq_b], [dk_f, dk_b], [dv_f, dv_b], None, [dzg]], n_proj, "assemble_dz")
    dw_in_t = _mm_tn(dzall, hc, BF16, "dw_in")
    token = emit(3, [dw_in_t, dw_out, dw_pool])
    dhc = _mm([(dzall, w_in_t, 0, 0)], NN, d, F32, "dhc", tm_pref=ROW_TILES, dep=token)
    dxc1, acc_mod = _modulate_bwd(xc1, dhc, dx2, mods[0], ng(0, 1), 4, t, "mod_bwd_l0mix")
    record(0, 3, 1, ng(0, 1), acc_mod, None, 2)
    dxc0, acc_mod, acc_gate = ffn_bwd(dxc1, sv_f1, 0, 0, ng(0, 0), t, "l0f1", 4, split=True)
    record(0, 0, 0, ng(0, 0), acc_mod, acc_gate, 2)

    grads["norm_g"] = jnp.stack([jnp.stack(dnorm[0]), jnp.stack(dnorm[1])])
    zero = jnp.zeros((d,), F32)
    dmods_arr = jnp.stack([jnp.stack([jnp.stack([v if v is not None else zero for v in dmods[l][s]])
                                      for s in range(2)]) for l in range(2)])
    return loss_vec, dxc0, grads, dmods_arr


def _pack(parts):
    flat = jnp.concatenate([p.reshape(-1).astype(F32) for p in parts])
    pad = (-flat.shape[0]) % 128
    return jnp.pad(flat, (0, pad))[None, :]


def _unpack(rows, shapes):
    out, off = [], 0
    for s in shapes:
        n = int(np.prod(s))
        out.append(rows[:, off:off + n].reshape((rows.shape[0],) + tuple(s)))
        off += n
    return out


def _cols_to_full(g):
    g = jnp.moveaxis(g, 0, -2)
    return g.reshape(g.shape[:-2] + (-1,))


def kernel(x, c, ctx, c_ctx, w_mod, b_mod, norm_g, ffn1_wi, ffn1_wo, ffn2_wi, ffn2_wo, w_in, w_a2_f, b_a_f, w_a2_b, b_a_b, sink, gla_g, w_out, w_pool, pool_scale, final_g, loss_target, m_c_ctx, m_w_mod, m_b_mod, m_norm_g, m_ffn1_wi, m_ffn1_wo, m_ffn2_wi, m_ffn2_wo, m_w_in, m_w_a2_f, m_b_a_f, m_w_a2_b, m_b_a_b, m_sink, m_gla_g, m_w_out, m_w_pool, m_pool_scale, m_final_g, v_c_ctx, v_w_mod, v_b_mod, v_norm_g, v_ffn1_wi, v_ffn1_wo, v_ffn2_wi, v_ffn2_wo, v_w_in, v_w_a2_f, v_b_a_f, v_w_a2_b, v_b_a_b, v_sink, v_gla_g, v_w_out, v_w_pool, v_pool_scale, v_final_g):
    t, d = x.shape[1], x.shape[2]
    me = _dev_index()
    nc = w_mod.shape[2]
    ncol_in = w_in.shape[2]
    ncol_pad = -(-ncol_in // 16) * 16

    small_shapes = [(d,), norm_g.shape, pool_scale.shape, w_a2_f.shape, w_a2_b.shape, w_pool.shape]
    g1 = _gather_small(_pack([c, norm_g, pool_scale, w_a2_f, w_a2_b, w_pool]), "gather_params")
    c_all, norm_g_all, pool_scale_all, w2f_all, w2b_all, w_pool_all = _unpack(g1, small_shapes)
    wts = {
        "norm_g": _cols_to_full(norm_g_all),
        "pool_scale": _cols_to_full(pool_scale_all),
        "w_a2_f": _cols_to_full(w2f_all)[0],
        "w_a2_b": _cols_to_full(w2b_all)[0],
        "w_pool": jnp.moveaxis(w_pool_all[:, 0], 0, 1).reshape(w_pool.shape[1], -1, w_pool.shape[3]),
        "b_a_f": b_a_f, "b_a_b": b_a_b, "sink": sink[0], "gla_g": gla_g, "final_g": final_g[None, :],
    }

    craw = jnp.concatenate([c_all, c_ctx[None, :], jnp.zeros((16 - N_DEV - 1, d), F32)], axis=0)
    b_cols = lax.dynamic_slice_in_dim(b_mod, me * nc, nc, axis=1)[:, None, :]
    mm_cols = _adaln_fwd(craw, w_mod, b_cols, "adaln_fwd")
    g2 = _gather_small(mm_cols.reshape(1, -1), "gather_mods").reshape(N_DEV, 2, 16, nc)
    mm_full = jnp.moveaxis(g2, 0, 2).reshape(2, 16, N_MOD, d)
    mods = jnp.stack([lax.dynamic_index_in_dim(mm_full, me, axis=1, keepdims=False), mm_full[:, N_DEV]], axis=1)
    mods = jnp.pad(mods, ((0, 0), (0, 0), (0, 16 - N_MOD), (0, 0)))

    tr = lambda w: jnp.swapaxes(w, 1, 2).astype(BF16)
    wi1_sh, wi2_sh, wo1_sh, wo2_sh = tr(ffn1_wi), tr(ffn2_wi), ffn1_wo.astype(BF16), ffn2_wo.astype(BF16)
    w_in_sh = jnp.pad(tr(w_in), ((0, 0), (0, ncol_pad - ncol_in), (0, 0)))
    groups = [
        {"wi1_0": wi1_sh[0:1]},
        {"wo1_0": wo1_sh[0:1]},
        {"w_in": w_in_sh},
        {"w_out": w_out.astype(BF16), "wi2_0": wi2_sh[0:1], "wo2_0": wo2_sh[0:1]},
        {"wi1_1": wi1_sh[1:2], "wo1_1": wo1_sh[1:2], "wi2_1": wi2_sh[1:2], "wo2_1": wo2_sh[1:2]},
    ]

    gathers, token = [], None
    for gi, grp in enumerate(groups):
        lands = [_place_shard(s, me, f"gather_place_{nm}") for nm, s in grp.items()]
        gathers.append(_exchange_start(list(grp.values()), lands, True, 1 + gi, token, f"gather_start_{gi}"))
        token = gathers[-1][4]
    n_proj = -(-(N_DEV * ncol_in) // 128) * 128

    def fetch(gi, after):
        _, lands = _exchange_wait(gathers[gi], True, token if after is None else after, f"gather_wait_{gi}")
        out = dict(zip(groups[gi].keys(), lands))
        if "w_in" in out:
            w_in_t = out.pop("w_in").reshape(1, N_DEV, ncol_pad, d)[:, :, :ncol_in].reshape(1, N_DEV * ncol_in, d)
            out["w_in_t"] = jnp.pad(w_in_t, ((0, 0), (0, n_proj - N_DEV * ncol_in), (0, 0)))
        return out

    scatters = []

    def emit(stage, arrays):
        if stage == 3:
            dw_in_t, dw_out, dw_pool = arrays
            dw_in_full = dw_in_t[:N_DEV * ncol_in].reshape(N_DEV, ncol_in, d)
            dw_in_full = jnp.pad(dw_in_full, ((0, 0), (0, ncol_pad - ncol_in), (0, 0)))
            srcs = [dw_in_full.reshape(1, N_DEV * ncol_pad, d), dw_out[None], dw_pool.astype(BF16)]
        else:
            srcs = [a[None] for a in arrays]
        lands = [lax.empty((N_DEV, s.shape[0], s.shape[1] // N_DEV, s.shape[2]), s.dtype) for s in srcs]
        scatters.append(_exchange_start(srcs, lands, False, 1 + len(groups) + stage, None, f"scatter_start_{stage}"))
        return scatters[-1][4]

    loss_vec, grad_x, grads, dmods = _local_step(x[0], ctx[0], loss_target[0], mods, wts, fetch, emit)
    loss = lax.psum(jnp.sum(loss_vec), ("x", "y", "c"))

    def reduce_stage(stage, after):
        wholes, lands = _exchange_wait(scatters[stage], False, after, f"scatter_wait_{stage}")
        return [_sum_slots(ld, wh, me, f"sum_grad_{stage}_{i}") for i, (ld, wh) in enumerate(zip(lands, wholes))]

    (dwi2_1, dwo2_1), (dwi1_1, dwo1_1), (dwi2_0, dwo2_0), (dw_in_s, dw_out_s, dw_pool_s) = [
        reduce_stage(stage, grad_x) for stage in range(4)]
    back = lambda g: jnp.swapaxes(g, 1, 2)
    g_big = {
        "ffn2_wi": back(jnp.concatenate([dwi2_0, dwi2_1], axis=0)), "ffn2_wo": jnp.concatenate([dwo2_0, dwo2_1], axis=0),
        "w_in": back(dw_in_s[:, :ncol_in]), "w_out": dw_out_s, "w_pool": dw_pool_s[None],
    }

    order = ["c_ctx", "w_mod", "b_mod", "norm_g", "ffn1_wi", "ffn1_wo", "ffn2_wi", "ffn2_wo", "w_in", "w_a2_f", "b_a_f",
             "w_a2_b", "b_a_b", "sink", "gla_g", "w_out", "w_pool", "pool_scale", "final_g"]
    ws = dict(c_ctx=c_ctx, w_mod=w_mod, b_mod=b_mod, norm_g=norm_g, ffn1_wi=ffn1_wi, ffn1_wo=ffn1_wo, ffn2_wi=ffn2_wi,
              ffn2_wo=ffn2_wo, w_in=w_in, w_a2_f=w_a2_f, b_a_f=b_a_f, w_a2_b=w_a2_b, b_a_b=b_a_b, sink=sink, gla_g=gla_g,
              w_out=w_out, w_pool=w_pool, pool_scale=pool_scale, final_g=final_g)
    ms = dict(c_ctx=m_c_ctx, w_mod=m_w_mod, b_mod=m_b_mod, norm_g=m_norm_g, ffn1_wi=m_ffn1_wi, ffn1_wo=m_ffn1_wo,
              ffn2_wi=m_ffn2_wi, ffn2_wo=m_ffn2_wo, w_in=m_w_in, w_a2_f=m_w_a2_f, b_a_f=m_b_a_f, w_a2_b=m_w_a2_b,
              b_a_b=m_b_a_b, sink=m_sink, gla_g=m_gla_g, w_out=m_w_out, w_pool=m_w_pool, pool_scale=m_pool_scale,
              final_g=m_final_g)
    vs = dict(c_ctx=v_c_ctx, w_mod=v_w_mod, b_mod=v_b_mod, norm_g=v_norm_g, ffn1_wi=v_ffn1_wi, ffn1_wo=v_ffn1_wo,
              ffn2_wi=v_ffn2_wi, ffn2_wo=v_ffn2_wo, w_in=v_w_in, w_a2_f=v_w_a2_f, b_a_f=v_b_a_f, w_a2_b=v_w_a2_b,
              b_a_b=v_b_a_b, sink=v_sink, gla_g=v_gla_g, w_out=v_w_out, w_pool=v_w_pool, pool_scale=v_pool_scale,
              final_g=v_final_g)
    early, late = ["ffn2_wi", "ffn2_wo", "w_out", "w_in", "w_pool"], ["ffn1_wi", "ffn1_wo"]
    big = early + ["w_mod"] + late
    delta, new_m, new_v = {}, {}, {}
    g_all = dict(g_big)

    def adamw_big(nm):
        shp = ws[nm].shape
        two_d = lambda a: a.reshape(-1, shp[-1])
        dl, nm_, nv_ = _adamw(two_d(ws[nm]), two_d(g_all[nm]), two_d(ms[nm]), two_d(vs[nm]), f"adamw_{nm}")
        delta[nm], new_m[nm], new_v[nm] = dl.reshape(shp), nm_.reshape(shp), nv_.reshape(shp)

    for nm in early:
        adamw_big(nm)

    small_g = [dmods[:, :, :N_MOD].reshape(2, 2, N_MOD * d), grads["norm_g"], grads["pool_scale"], grads["final_g"],
               grads["b_a_f"], grads["b_a_b"], grads["sink"], grads["gla_g"], grads["w_a2_f"], grads["w_a2_b"]]
    small_g_shapes = [a.shape for a in small_g]
    g3 = _gather_small(_pack(small_g), "gather_small_grads", dep=delta["w_out"])
    total = _sum_rows8(g3, "sum_small_grads")
    dmm_all = _unpack(g3, small_g_shapes[:1])[0]
    (dmm_sum, dnorm_g, dpool_scale, dfinal_g, db_a_f, db_a_b, dsink, dgla_g, dw_a2_f, dw_a2_b) = [
        a[0] for a in _unpack(total, small_g_shapes)]
    dmm_rows = jnp.concatenate([dmm_all[:, :, 0].transpose(1, 0, 2), dmm_sum[:, 1][:, None, :],
                                jnp.zeros((2, 16 - N_DEV - 1, N_MOD * d), F32)], axis=1)
    grad_b_mod = dmm_sum[:, 0] + dmm_sum[:, 1]
    dmm_cols = lax.dynamic_slice_in_dim(dmm_rows, me * nc, nc, axis=2)
    cs_t = jnp.transpose(_silu(craw)).astype(BF16)
    grad_w_mod, dcraw = _adaln_bwd(craw, cs_t, dmm_cols, w_mod, "adaln_bwd")
    g4 = _gather_small((dcraw[0, N_DEV] + dcraw[1, N_DEV])[None, :], "gather_c_ctx_grad")
    grad_c_ctx = _sum_rows8(g4, "sum_c_ctx_grad")[0]

    col = lambda v, n: lax.dynamic_slice_in_dim(v, me * n, n, axis=v.ndim - 1)
    g_small = {
        "c_ctx": grad_c_ctx, "b_mod": grad_b_mod, "norm_g": col(dnorm_g, norm_g.shape[2]),
        "w_a2_f": col(dw_a2_f, w_a2_f.shape[2])[None], "b_a_f": db_a_f[None], "w_a2_b": col(dw_a2_b, w_a2_b.shape[2])[None],
        "b_a_b": db_a_b[None], "sink": dsink[None], "gla_g": dgla_g[None], "pool_scale": col(dpool_scale, pool_scale.shape[1])[None],
        "final_g": dfinal_g,
    }
    g_all.update(g_small, w_mod=grad_w_mod)
    adamw_big("w_mod")
    rest = [nm for nm in order if nm not in big]
    rest_shapes = [ws[nm].shape for nm in rest]
    packed = [_pack([d_[nm].reshape(ws[nm].shape) for nm in rest]).reshape(-1, 128) for d_ in (ws, g_all, ms, vs)]
    pad_rows = (-packed[0].shape[0]) % 512
    packed = [jnp.pad(p, ((0, pad_rows), (0, 0))) for p in packed]
    outs = _adamw(*packed, "adamw_small")
    for dst, arr in zip((delta, new_m, new_v), outs):
        for nm, val in zip(rest, _unpack(arr.reshape(1, -1), rest_shapes)):
            dst[nm] = val[0]

    (dwo1_0,), (dwi1_0,) = reduce_stage(4, outs[0]), reduce_stage(5, outs[0])
    g_all["ffn1_wi"] = back(jnp.concatenate([dwi1_0, dwi1_1], axis=0))
    g_all["ffn1_wo"] = jnp.concatenate([dwo1_0, dwo1_1], axis=0)
    for nm in late:
        adamw_big(nm)
    g_all = {nm: g_all[nm].reshape(ws[nm].shape) for nm in order}

    return (loss, grad_x[None], *[g_all[nm] for nm in order], *[delta[nm] for nm in order],
            *[new_m[nm] for nm in order], *[new_v[nm] for nm in order])
```

```python
import functools

import numpy as np
import jax
import jax.numpy as jnp
from jax import lax
from jax.experimental import pallas as pl
from jax.experimental.pallas import tpu as pltpu

F32 = jnp.float32
BF16 = jnp.bfloat16
MESH = pl.DeviceIdType.MESH

N_DEV = 8
RMS_EPS = 1e-6
N_MOD = 9
GRID_W = 64
A_HEADS, A_KV_HEADS, A_HEAD_DIM = 8, 2, 64
A_REP = A_HEADS // A_KV_HEADS
WINDOW = 128
ROPE_BASE = 10000.0
B_HEADS, B_DK, B_DV = 4, 64, 128
B_GATE_RANK = 16
B_GATE_NORM = 16.0
B_CHUNK = 64
POOL_WINDOWS = (2, 4, 8, 16)
POOL_PAD = 8
A_Q = A_HEADS * A_HEAD_DIM
A_KV = A_KV_HEADS * A_HEAD_DIM
B_QK = B_HEADS * B_DK
B_V = B_HEADS * B_DV
PROJ_SIZES = (A_Q, A_KV, A_KV, B_QK, B_QK, B_V, B_V, 2 * B_GATE_RANK)
PROJ_DIM = sum(PROJ_SIZES)
ADAM_LR, ADAM_B1, ADAM_B2, ADAM_EPS, ADAM_WD, ADAM_STEP = 0.001, 0.9, 0.999, 1e-08, 0.01, 10

VMEM_LIMIT = 56 * 1024 * 1024
ROW_TILES = (512, 544, 256, 128, 64, 32, 16, 8)
TALL_TILES = (1024, 1088) + ROW_TILES

NN = ((1,), (0,))
NT = ((1,), (1,))
TN = ((0,), (0,))


def _dot(a, b, dims=NN, prec=None):
    return lax.dot_general(a, b, (dims, ((), ())), precision=prec, preferred_element_type=F32)


def _bdot(a, b, dims=NN):
    return _dot(a.astype(BF16), b.astype(BF16), dims)


def _dot_01(sel, x):
    hi = x.astype(BF16)
    rest = x - hi.astype(F32)
    mid = rest.astype(BF16)
    lo = (rest - mid.astype(F32)).astype(BF16)
    sel = sel.astype(BF16)
    return _dot(sel, hi) + _dot(sel, mid) + _dot(sel, lo)


def _params(sem=None, **kw):
    return pltpu.CompilerParams(dimension_semantics=sem, vmem_limit_bytes=VMEM_LIMIT, **kw)


def _silu(a):
    return a * jax.nn.sigmoid(a)


def _pick(n, prefs):
    for p in prefs:
        if n % p == 0:
            return p
    return n


def _full(shape):
    nd = len(shape)
    return pl.BlockSpec(shape, lambda *_: (0,) * nd)


def _peers():
    x, y, c = lax.axis_index("x"), lax.axis_index("y"), lax.axis_index("c")
    return x, y, c


def _dev_index():
    x, y, c = _peers()
    return 4 * x + 2 * y + c


def _others(x, y, c):
    return [(x, y, 1 - c), (1 - x, y, c), (x, 1 - y, c), (1 - x, 1 - y, c),
            (1 - x, y, 1 - c), (x, 1 - y, 1 - c), (1 - x, 1 - y, 1 - c)]


def _index_of(dev):
    return 4 * dev[0] + 2 * dev[1] + dev[2]


def _exchange_refs(gather, shapes, srcs, lands, a, me, to):
    if gather:
        r = shapes[a][1]
        return srcs[a], lands[a].at[:, pl.ds(_index_of(me) * r, r), :]
    r = shapes[a][1] // N_DEV
    return srcs[a].at[:, pl.ds(_index_of(to) * r, r), :], lands[a].at[_index_of(me)]


HBM_SPEC = pl.BlockSpec(memory_space=pltpu.HBM)
SEM_SPEC = pl.BlockSpec(memory_space=pltpu.SEMAPHORE)
EFFECT = pltpu.SideEffectType.DATAFLOW_SIDE_EFFECTING


def _exchange_start(srcs, lands, gather, collective_id, dep, name):
    n = len(srcs)
    shapes = [s.shape for s in srcs]
    deps = [] if dep is None else [dep]

    def body(*refs):
        src_refs, land_refs = refs[:n], refs[n:2 * n]
        send_sems, recv_sems = refs[2 * n + len(deps)], refs[2 * n + len(deps) + 1]
        token = refs[-1]
        x, y, c = _peers()
        others = _others(x, y, c)
        barrier = pltpu.get_barrier_semaphore()
        for peer in others:
            pl.semaphore_signal(barrier, inc=1, device_id=peer, device_id_type=MESH)
        pl.semaphore_wait(barrier, len(others))
        for a in range(n):
            for k, to in enumerate(others):
                src, dst = _exchange_refs(gather, shapes, src_refs, land_refs, a, (x, y, c), to)
                pltpu.make_async_remote_copy(src_ref=src, dst_ref=dst, send_sem=send_sems.at[7 * a + k],
                                             recv_sem=recv_sems.at[7 * a + k], device_id=to, device_id_type=MESH).start()
        token[...] = jnp.zeros_like(token)

    outs = pl.pallas_call(
        body, name=name,
        out_shape=(pltpu.SemaphoreType.DMA((7 * n,)), pltpu.SemaphoreType.DMA((7 * n,)),
                   *[pltpu.HBM(s.shape, s.dtype) for s in srcs], *[pltpu.HBM(l.shape, l.dtype) for l in lands],
                   jax.ShapeDtypeStruct((8, 128), F32)),
        in_specs=[HBM_SPEC] * (2 * n) + [pl.BlockSpec(memory_space=pl.ANY)] * len(deps),
        out_specs=(SEM_SPEC, SEM_SPEC, *[HBM_SPEC] * (2 * n), pl.BlockSpec(memory_space=pltpu.VMEM)),
        input_output_aliases={i: 2 + i for i in range(2 * n)},
        compiler_params=pltpu.CompilerParams(has_side_effects=EFFECT, collective_id=collective_id),
    )(*[pltpu.with_memory_space_constraint(s, pltpu.HBM) for s in srcs],
      *[pltpu.with_memory_space_constraint(l, pltpu.HBM) for l in lands], *deps)
    return outs[0], outs[1], list(outs[2:2 + n]), list(outs[2 + n:2 + 2 * n]), outs[-1]


def _exchange_wait(started, gather, after, name):
    send_sems, recv_sems, srcs, lands, _ = started
    n = len(srcs)
    shapes = [s.shape for s in srcs]

    def body(*refs):
        src_refs, land_refs = refs[:n], refs[n:2 * n]
        send_sems, recv_sems = refs[2 * n], refs[2 * n + 1]
        x, y, c = _peers()
        for a in range(n):
            for k, peer in enumerate(_others(x, y, c)):
                src, _ = _exchange_refs(gather, shapes, src_refs, land_refs, a, (x, y, c), peer)
                _, dst = _exchange_refs(gather, shapes, src_refs, land_refs, a, peer, (x, y, c))
                copy = pltpu.make_async_remote_copy(src_ref=src, dst_ref=dst, send_sem=send_sems.at[7 * a + k],
                                                    recv_sem=recv_sems.at[7 * a + k], device_id=peer, device_id_type=MESH)
                copy.wait_send()
                copy.wait_recv()

    outs = pl.pallas_call(
        body, name=name,
        out_shape=(*[pltpu.HBM(s.shape, s.dtype) for s in srcs], *[pltpu.HBM(l.shape, l.dtype) for l in lands]),
        in_specs=[HBM_SPEC] * (2 * n) + [SEM_SPEC, SEM_SPEC, pl.BlockSpec(memory_space=pl.ANY)],
        out_specs=tuple([HBM_SPEC] * (2 * n)),
        input_output_aliases={i: i for i in range(2 * n)},
        compiler_params=pltpu.CompilerParams(has_side_effects=EFFECT),
    )(*srcs, *lands, send_sems, recv_sems, after)
    return list(outs[:n]), list(outs[n:])


def _place_shard(shard, me, name):
    a_, r, c = shard.shape
    tr = _pick(r, (352, 304, 256, 128, 64, 32, 16, 8))
    nr = r // tr

    def body(me_ref, in_ref, out_ref):
        out_ref[...] = in_ref[...]

    return pl.pallas_call(
        body, name=name,
        grid_spec=pltpu.PrefetchScalarGridSpec(
            num_scalar_prefetch=1, grid=(a_, nr),
            in_specs=[pl.BlockSpec((None, tr, c), lambda i, j, me_ref: (i, j, 0))],
            out_specs=pl.BlockSpec((None, tr, c), lambda i, j, me_ref: (i, me_ref[0] * nr + j, 0))),
        out_shape=jax.ShapeDtypeStruct((a_, N_DEV * r, c), shard.dtype),
        compiler_params=_params(("parallel", "parallel")),
    )(me.reshape(1).astype(jnp.int32), shard)


def _sum_slots(land, whole, me, name):
    _, a_, r, c = land.shape
    tr = _pick(r, (352, 256, 128, 64, 32, 16, 8))
    nr = r // tr

    def body(me_ref, land_ref, own_ref, out_ref):
        acc = None
        for s in range(N_DEV):
            part = jnp.where(me_ref[0] == s, own_ref[...], land_ref[s]).astype(F32)
            acc = part if acc is None else acc + part
        out_ref[...] = acc

    return pl.pallas_call(
        body, name=name,
        grid_spec=pltpu.PrefetchScalarGridSpec(
            num_scalar_prefetch=1, grid=(a_, nr),
            in_specs=[pl.BlockSpec((N_DEV, None, tr, c), lambda i, j, me_ref: (0, i, j, 0)),
                      pl.BlockSpec((None, tr, c), lambda i, j, me_ref: (i, me_ref[0] * nr + j, 0))],
            out_specs=pl.BlockSpec((None, tr, c), lambda i, j, me_ref: (i, j, 0))),
        out_shape=jax.ShapeDtypeStruct((a_, r, c), F32),
        compiler_params=_params(("parallel", "parallel")),
    )(me.reshape(1).astype(jnp.int32), land, whole)


def _gather_small(vec, name, dep=None):
    p = vec.shape[1]
    pp = -(-p // 1024) * 1024
    blk = jnp.pad(vec, ((0, 0), (0, pp - p))).reshape(8, pp // 8)
    deps = [] if dep is None else [dep]

    def body(in_ref, *rest):
        out_ref, send_sems, recv_sems = rest[-3:]
        x, y, c = _peers()
        me = 4 * x + 2 * y + c
        others = [(x, y, 1 - c), (1 - x, y, c), (x, 1 - y, c), (1 - x, 1 - y, c),
                  (1 - x, y, 1 - c), (x, 1 - y, 1 - c), (1 - x, 1 - y, 1 - c)]

        def rows(idx):
            return out_ref.at[pl.ds(pl.multiple_of(idx * 8, 8), 8), :]

        out_ref[pl.ds(pl.multiple_of(me * 8, 8), 8), :] = in_ref[...]

        def copy(k, dev, slot):
            return pltpu.make_async_remote_copy(
                src_ref=in_ref, dst_ref=rows(slot), send_sem=send_sems.at[k], recv_sem=recv_sems.at[k],
                device_id=dev, device_id_type=MESH)

        sends = [copy(k, dev, me) for k, dev in enumerate(others)]
        for cp in sends:
            cp.start()
        for k, dev in enumerate(others):
            copy(k, dev, 4 * dev[0] + 2 * dev[1] + dev[2]).wait_recv()
        for cp in sends:
            cp.wait_send()

    vm = pl.BlockSpec(memory_space=pltpu.VMEM)
    out = pl.pallas_call(
        body, name=name, out_shape=jax.ShapeDtypeStruct((8 * N_DEV, pp // 8), F32),
        in_specs=[vm] + [pl.BlockSpec(memory_space=pl.ANY)] * len(deps), out_specs=vm,
        scratch_shapes=[pltpu.SemaphoreType.DMA((7,)), pltpu.SemaphoreType.DMA((7,))],
        compiler_params=pltpu.CompilerParams(has_side_effects=True, vmem_limit_bytes=VMEM_LIMIT),
    )(blk, *deps)
    return out.reshape(N_DEV, pp)[:, :p]


def _sum_rows8(g, name):
    p = g.shape[1]

    def body(in_ref, out_ref):
        acc = in_ref[0:1, :]
        for s in range(1, N_DEV):
            acc = acc + in_ref[s:s + 1, :]
        out_ref[...] = acc

    return pl.pallas_call(body, name=name, out_shape=jax.ShapeDtypeStruct((1, p), F32),
                          compiler_params=_params())(g)


def _sel_row(mods_ref, is_ctx, k):
    return jnp.where(is_ctx, mods_ref[1, k:k + 1, :], mods_ref[0, k:k + 1, :])


def _modulate(z, mods, g, ks, kc, n_x, out_dtype, name):
    m, d = z.shape
    tm = _pick(m, (256, 128, 64, 32, 16, 8))

    def body(z_ref, mods_ref, g_ref, h_ref):
        is_ctx = pl.program_id(0) * tm >= n_x
        zz = z_ref[...]
        r = lax.rsqrt(jnp.mean(zz * zz, axis=-1, keepdims=True) + RMS_EPS)
        shift, scale = _sel_row(mods_ref, is_ctx, ks), _sel_row(mods_ref, is_ctx, kc)
        h_ref[...] = ((zz * r) * g_ref[...] * (1.0 + scale) + shift).astype(out_dtype)

    return pl.pallas_call(
        body, name=name, grid=(m // tm,),
        in_specs=[pl.BlockSpec((tm, d), lambda i: (i, 0)), _full(mods.shape), _full(g.shape)],
        out_specs=pl.BlockSpec((tm, d), lambda i: (i, 0)),
        out_shape=jax.ShapeDtypeStruct((m, d), out_dtype),
        compiler_params=_params(("parallel",)),
    )(z, mods, g)


def _modulate_bwd(z, dh, dres, mods, g, kc, n_x, name, latent_only=False):
    m, d = z.shape
    tm = _pick(m, (256, 128, 64, 32, 16, 8))
    first_ctx = n_x // tm
    res_blocks = dres.shape[0] // tm
    out_blocks = (n_x if latent_only else m) // tm

    def body(z_ref, dh_ref, dres_ref, mods_ref, g_ref, dx_ref, acc_ref):
        i = pl.program_id(0)
        is_ctx = i * tm >= n_x

        @pl.when((i == 0) | (i == first_ctx))
        def _():
            acc_ref[...] = jnp.zeros_like(acc_ref)

        zz, dhh = z_ref[...], dh_ref[...]
        r = lax.rsqrt(jnp.mean(zz * zz, axis=-1, keepdims=True) + RMS_EPS)
        nz = zz * r
        gain = g_ref[...] * (1.0 + _sel_row(mods_ref, is_ctx, kc))
        dn = dhh * gain
        dz = r * (dn - nz * jnp.mean(dn * nz, axis=-1, keepdims=True))
        @pl.when(i < out_blocks)
        def _():
            dx_ref[...] = jnp.where(i < res_blocks, dres_ref[...], 0.0) + dz

        acc_ref[0:1, :] += jnp.sum(dhh, axis=0, keepdims=True)
        acc_ref[1:2, :] += jnp.sum(dhh * nz, axis=0, keepdims=True)

    row = pl.BlockSpec((tm, d), lambda i: (i, 0))
    return pl.pallas_call(
        body, name=name, grid=(m // tm,),
        in_specs=[row, row, pl.BlockSpec((tm, d), lambda i: (jnp.minimum(i, res_blocks - 1), 0)),
                  _full(mods.shape), _full(g.shape)],
        out_specs=[pl.BlockSpec((tm, d), lambda i: (jnp.minimum(i, out_blocks - 1), 0)),
                   pl.BlockSpec((None, 8, d), lambda i: ((i * tm >= n_x).astype(jnp.int32), 0, 0))],
        out_shape=[jax.ShapeDtypeStruct((out_blocks * tm, d), F32), jax.ShapeDtypeStruct((2, 8, d), F32)],
        compiler_params=_params(("arbitrary",)),
    )(z, dh, dres, mods, g)


def _ffn_up(h, wi_t, layer, name):
    m, d = h.shape
    f = wi_t.shape[1] // 2
    tm = _pick(m, ROW_TILES)

    def body(h_ref, w_ref, au_ref, act_ref):
        hh = h_ref[...]
        a = _dot(hh, w_ref[0:f, :], NT)
        u = _dot(hh, w_ref[f:2 * f, :], NT)
        au_ref[:, 0:f] = a.astype(BF16)
        au_ref[:, f:2 * f] = u.astype(BF16)
        act_ref[...] = (_silu(a) * u).astype(BF16)

    return pl.pallas_call(
        body, name=name, grid=(m // tm,),
        in_specs=[pl.BlockSpec((tm, d), lambda i: (i, 0)),
                  pl.BlockSpec((None, 2 * f, d), lambda i: (layer, 0, 0))],
        out_specs=[pl.BlockSpec((tm, 2 * f), lambda i: (i, 0)), pl.BlockSpec((tm, f), lambda i: (i, 0))],
        out_shape=[jax.ShapeDtypeStruct((m, 2 * f), BF16), jax.ShapeDtypeStruct((m, f), BF16)],
        compiler_params=_params(("parallel",)),
    )(h, wi_t)


def _mm_resid(a, b, layer, res, mods, km, coef, n_x, name, nxt=None):
    m, k = a.shape
    n = b.shape[2]
    tm = _pick(m, (512, 256, 128, 64, 32, 16, 8))
    tn = n if nxt is not None else _pick(n, (1024, 512, 256, 128))
    extra = [] if nxt is None else [nxt[0], nxt[1]]

    def body(a_ref, b_ref, res_ref, mods_ref, *rest):
        is_ctx = pl.program_id(1) * tm >= n_x
        y = _dot(a_ref[...], b_ref[...])
        new = res_ref[...] + coef * _sel_row(mods_ref, is_ctx, km) * y
        if nxt is None:
            out_ref, y_ref = rest
        else:
            nmods_ref, g_ref, out_ref, y_ref, h_ref = rest
            r = lax.rsqrt(jnp.mean(new * new, axis=-1, keepdims=True) + RMS_EPS)
            shift, scale = _sel_row(nmods_ref, is_ctx, nxt[2]), _sel_row(nmods_ref, is_ctx, nxt[3])
            h_ref[...] = ((new * r) * g_ref[...] * (1.0 + scale) + shift).astype(nxt[4])
        y_ref[...] = y
        out_ref[...] = new

    tile = pl.BlockSpec((tm, tn), lambda j, i: (i, j))
    outs = [jax.ShapeDtypeStruct((m, n), F32), jax.ShapeDtypeStruct((m, n), F32)]
    if nxt is not None:
        outs.append(jax.ShapeDtypeStruct((m, n), nxt[4]))
    return pl.pallas_call(
        body, name=name, grid=(n // tn, m // tm),
        in_specs=[pl.BlockSpec((tm, k), lambda j, i: (i, 0)),
                  pl.BlockSpec((None, k, tn), lambda j, i: (layer, 0, j)),
                  tile, pl.BlockSpec((2, 16, tn), lambda j, i: (0, 0, j))] + [_full(e.shape) for e in extra],
        out_specs=[tile] * len(outs), out_shape=outs,
        compiler_params=_params(("parallel", "parallel")),
    )(a, b, res, mods, *extra)


def _resid_bwd(dx, y, mods, km, coef, n_x, name, dep=None):
    m, d = dx.shape
    tm = _pick(m, (256, 128, 64, 32, 16, 8))
    first_ctx = n_x // tm
    deps = [] if dep is None else [dep]

    def body(dx_ref, y_ref, mods_ref, *rest):
        dy_ref, acc_ref = rest[-2:]
        i = pl.program_id(0)
        is_ctx = i * tm >= n_x

        @pl.when((i == 0) | (i == first_ctx))
        def _():
            acc_ref[...] = jnp.zeros_like(acc_ref)

        dxx = dx_ref[...]
        dy_ref[...] = (coef * _sel_row(mods_ref, is_ctx, km) * dxx).astype(BF16)
        acc_ref[0:1, :] += jnp.sum(coef * y_ref[...] * dxx, axis=0, keepdims=True)

    row = pl.BlockSpec((tm, d), lambda i: (i, 0))
    return pl.pallas_call(
        body, name=name, grid=(m // tm,),
        in_specs=[row, row, _full(mods.shape)] + [pl.BlockSpec(memory_space=pl.ANY)] * len(deps),
        out_specs=[row, pl.BlockSpec((None, 8, d), lambda i: ((i * tm >= n_x).astype(jnp.int32), 0, 0))],
        out_shape=[jax.ShapeDtypeStruct((m, d), BF16), jax.ShapeDtypeStruct((2, 8, d), F32)],
        compiler_params=_params(("arbitrary",)),
    )(dx, y, mods, *deps)


def _ffn_down_bwd(dy, wo, layer, au, name):
    m, d = dy.shape
    f = wo.shape[1]
    tm = _pick(m, ROW_TILES)

    def body(dy_ref, wo_ref, au_ref, dau_ref):
        dact = _dot(dy_ref[...], wo_ref[...], NT)
        aa, uu = au_ref[:, 0:f].astype(F32), au_ref[:, f:2 * f].astype(F32)
        sg = jax.nn.sigmoid(aa)
        dau_ref[:, 0:f] = (dact * uu * (sg * (1.0 + aa * (1.0 - sg)))).astype(BF16)
        dau_ref[:, f:2 * f] = (dact * (aa * sg)).astype(BF16)

    wide = pl.BlockSpec((tm, 2 * f), lambda i: (i, 0))
    return pl.pallas_call(
        body, name=name, grid=(m // tm,),
        in_specs=[pl.BlockSpec((tm, d), lambda i: (i, 0)), pl.BlockSpec((None, f, d), lambda i: (layer, 0, 0)), wide],
        out_specs=wide, out_shape=jax.ShapeDtypeStruct((m, 2 * f), BF16),
        compiler_params=_params(("parallel",)),
    )(dy, wo, au)


def _mm(terms, dims, n, out_dtype, name, tm_pref=(512, 256, 128, 64, 32, 16, 8), tn_pref=(512, 256, 128), dep=None):
    m = terms[0][0].shape[0]
    tm = _pick(m, tm_pref)
    tn = _pick(n, tn_pref)
    nt = len(terms)
    deps = [] if dep is None else [dep]

    def body(*refs):
        out_ref = refs[-1]
        acc = None
        for t in range(nt):
            part = _dot(refs[2 * t][...].astype(BF16), refs[2 * t + 1][...].astype(BF16), dims)
            acc = part if acc is None else acc + part
        out_ref[...] = acc.astype(out_dtype)

    in_specs, args = [], []
    for a, b, layer, rb in terms:
        k = a.shape[1]
        in_specs.append(pl.BlockSpec((tm, k), lambda j, i: (i, 0)))
        if dims == NN:
            in_specs.append(pl.BlockSpec((None, k, tn), lambda j, i, layer=layer, rb=rb: (layer, rb, j)))
        else:
            nb = n // tn
            in_specs.append(pl.BlockSpec((None, tn, k), lambda j, i, layer=layer, rb=rb, nb=nb: (layer, rb * nb + j, 0)))
        args += [a, b]
    return pl.pallas_call(
        body, name=name, grid=(n // tn, m // tm), in_specs=in_specs + [pl.BlockSpec(memory_space=pl.ANY)] * len(deps),
        out_specs=pl.BlockSpec((tm, tn), lambda j, i: (i, j)),
        out_shape=jax.ShapeDtypeStruct((m, n), out_dtype),
        compiler_params=_params(("parallel", "parallel")),
    )(*args, *deps)


def _mm_tn(a, b, out_dtype, name, dep=None):
    t = a.shape[0]
    m, n = a.shape[1], b.shape[1]
    tm = _pick(m, (1408, 2432, 1024, 512, 256, 128))
    tn = _pick(n, (1024, 512, 256, 128))
    tk = _pick(t, TALL_TILES)
    deps = [] if dep is None else [dep]

    def body(a_ref, b_ref, *rest):
        out_ref, acc_ref = rest[-2:]
        kk = pl.program_id(2)

        @pl.when(kk == 0)
        def _():
            acc_ref[...] = jnp.zeros_like(acc_ref)

        acc_ref[...] += _dot(a_ref[...].astype(BF16), b_ref[...].astype(BF16), TN)

        @pl.when(kk == pl.num_programs(2) - 1)
        def _():
            out_ref[...] = acc_ref[...].astype(out_dtype)

    return pl.pallas_call(
        body, name=name, grid=(m // tm, n // tn, t // tk),
        in_specs=[pl.BlockSpec((tk, tm), lambda i, j, k: (k, i)), pl.BlockSpec((tk, tn), lambda i, j, k: (k, j))]
        + [pl.BlockSpec(memory_space=pl.ANY)] * len(deps),
        out_specs=pl.BlockSpec((tm, tn), lambda i, j, k: (i, j)),
        out_shape=jax.ShapeDtypeStruct((m, n), out_dtype),
        scratch_shapes=[pltpu.VMEM((tm, tn), F32)],
        compiler_params=_params(("parallel", "parallel", "arbitrary")),
    )(a, b, *deps)


def _stack_rows(a, b, name):
    ta, d = a.shape
    tm = _pick(int(np.gcd(ta, b.shape[0])), (256, 128, 64, 32, 16, 8))
    na, nb = ta // tm, b.shape[0] // tm

    def body(a_ref, b_ref, o_ref):
        o_ref[...] = jnp.where(pl.program_id(0) < na, a_ref[...], b_ref[...])

    return pl.pallas_call(
        body, name=name, grid=(na + nb,),
        in_specs=[pl.BlockSpec((tm, d), lambda i: (jnp.minimum(i, na - 1), 0)),
                  pl.BlockSpec((tm, d), lambda i: (jnp.maximum(i - na, 0), 0))],
        out_specs=pl.BlockSpec((tm, d), lambda i: (i, 0)),
        out_shape=jax.ShapeDtypeStruct((ta + b.shape[0], d), a.dtype),
        compiler_params=_params(("parallel",)),
    )(a, b)


def _assemble_dz(lat_parts, ctx_parts, both_parts, width, name):
    t = next(p.shape[0] for p in lat_parts if p is not None)
    l_ctx = next(p.shape[0] for p in ctx_parts if p is not None)
    tm = _pick(int(np.gcd(t, l_ctx)), (256, 128, 64, 32, 16, 8))
    nt, nl = t // tm, l_ctx // tm
    plan, args, in_specs, off = [], [], [], 0
    lat_spec = lambda w: pl.BlockSpec((tm, w), lambda i: (jnp.minimum(i, nt - 1), 0))
    ctx_spec = lambda w: pl.BlockSpec((tm, w), lambda i: (jnp.maximum(i - nt, 0), 0))
    all_spec = lambda w: pl.BlockSpec((tm, w), lambda i: (i, 0))
    for lat, ctx, both in zip(lat_parts, ctx_parts, both_parts):
        if both:
            w = both[0].shape[1]
            plan.append(("both", off, w, len(args), len(both)))
            args += both
            in_specs += [all_spec(w)] * len(both)
        else:
            w = (lat if lat is not None else ctx).shape[1]
            plan.append(("split", off, w, len(args), (lat is not None, ctx is not None)))
            for part, spec in ((lat, lat_spec), (ctx, ctx_spec)):
                if part is not None:
                    args.append(part)
                    in_specs.append(spec(w))
        off += w
    n_in = len(args)

    def body(*refs):
        out_ref = refs[n_in]
        is_ctx = pl.program_id(0) >= nt
        for kind, o, w, first, info in plan:
            if kind == "both":
                val = refs[first][...]
                for k in range(1, info):
                    val = val + refs[first + k][...]
            else:
                has_lat, has_ctx = info
                zero = jnp.zeros((tm, w), F32)
                lat = refs[first][...] if has_lat else zero
                ctx = refs[first + int(has_lat)][...] if has_ctx else zero
                val = jnp.where(is_ctx, ctx, lat)
            out_ref[:, o:o + w] = val.astype(BF16)
        if off < width:
            out_ref[:, off:width] = jnp.zeros((tm, width - off), BF16)

    return pl.pallas_call(
        body, name=name, grid=(nt + nl,), in_specs=in_specs,
        out_specs=pl.BlockSpec((tm, width), lambda i: (i, 0)),
        out_shape=jax.ShapeDtypeStruct((t + l_ctx, width), BF16),
        compiler_params=_params(("parallel",)),
    )(*args)


def _final_loss(x, g, target, name):
    t, d = x.shape
    tm = _pick(t, (256, 128, 64, 32, 16, 8))

    def body(x_ref, g_ref, t_ref, dx_ref, loss_ref, dg_ref):
        @pl.when(pl.program_id(0) == 0)
        def _():
            loss_ref[...] = jnp.zeros_like(loss_ref)
            dg_ref[...] = jnp.zeros_like(dg_ref)

        xx, gg = x_ref[...], g_ref[...]
        r = lax.rsqrt(jnp.mean(xx * xx, axis=-1, keepdims=True) + RMS_EPS)
        nz = xx * r
        err = nz * gg - t_ref[...]
        loss_ref[...] += jnp.sum(err * err, axis=0, keepdims=True) * (0.5 / d)
        dout = err * (1.0 / d)
        dg_ref[...] += jnp.sum(dout * nz, axis=0, keepdims=True)
        dn = dout * gg
        dx_ref[...] = r * (dn - nz * jnp.mean(dn * nz, axis=-1, keepdims=True))

    row = pl.BlockSpec((tm, d), lambda i: (i, 0))
    vec = pl.BlockSpec((1, d), lambda i: (0, 0))
    return pl.pallas_call(
        body, name=name, grid=(t // tm,), in_specs=[row, vec, row], out_specs=[row, vec, vec],
        out_shape=[jax.ShapeDtypeStruct((t, d), F32), jax.ShapeDtypeStruct((1, d), F32),
                   jax.ShapeDtypeStruct((1, d), F32)],
        compiler_params=_params(("arbitrary",)),
    )(x, g, target)


def _adaln_fwd(craw, w_mod, b_cols, name):
    lyr, d, nc = w_mod.shape

    def body(c_ref, w_ref, b_ref, out_ref):
        out_ref[...] = _bdot(_silu(c_ref[...]), w_ref[...]) + b_ref[...]

    return pl.pallas_call(
        body, name=name, grid=(lyr,),
        in_specs=[_full(craw.shape), pl.BlockSpec((None, d, nc), lambda l: (l, 0, 0)),
                  pl.BlockSpec((None, 1, nc), lambda l: (l, 0, 0))],
        out_specs=pl.BlockSpec((None, 16, nc), lambda l: (l, 0, 0)),
        out_shape=jax.ShapeDtypeStruct((lyr, 16, nc), F32),
        compiler_params=_params(("parallel",)),
    )(craw, w_mod, b_cols)


def _adaln_bwd(craw, cs_t, dmm_cols, w_mod, name):
    lyr, d, nc = w_mod.shape

    def body(c_ref, cst_ref, dmm_ref, w_ref, gw_ref, dc_ref):
        dmm = dmm_ref[...]
        gw_ref[...] = _bdot(cst_ref[...], dmm)
        cc = c_ref[...]
        sg = jax.nn.sigmoid(cc)
        dc_ref[...] = _bdot(dmm, w_ref[...], NT) * (sg * (1.0 + cc * (1.0 - sg)))

    wspec = pl.BlockSpec((None, d, nc), lambda l: (l, 0, 0))
    return pl.pallas_call(
        body, name=name, grid=(lyr,),
        in_specs=[_full(craw.shape), _full(cs_t.shape), pl.BlockSpec((None, 16, nc), lambda l: (l, 0, 0)), wspec],
        out_specs=[wspec, pl.BlockSpec((None, 16, d), lambda l: (l, 0, 0))],
        out_shape=[jax.ShapeDtypeStruct((lyr, d, nc), F32), jax.ShapeDtypeStruct((lyr, 16, d), F32)],
        compiler_params=_params(("parallel",)),
    )(craw, cs_t, dmm_cols, w_mod)


def _rope_tables(t):
    rows = np.repeat(np.arange(t // GRID_W, dtype=np.float32), GRID_W)
    cols = np.tile(np.arange(GRID_W, dtype=np.float32), t // GRID_W)
    n = A_HEAD_DIM // 4
    freqs = (ROPE_BASE ** (-np.arange(n, dtype=np.float32) / n)).astype(np.float32)
    ang_r, ang_c = (rows[:, None] * freqs).astype(np.float32), (cols[:, None] * freqs).astype(np.float32)
    cr, sr, cc, sc = np.cos(ang_r), np.sin(ang_r), np.cos(ang_c), np.sin(ang_c)
    cos = np.concatenate([cr, cr, cc, cc] * 2, axis=-1).astype(np.float32)
    sin = np.concatenate([-sr, sr, -sc, sc] * 2, axis=-1).astype(np.float32)
    return jnp.asarray(cos), jnp.asarray(sin)


def _rope(xt, cos, sin, adjoint, name):
    t, w = xt.shape
    tb = _pick(t, (512, 256, 128))
    rep = w // cos.shape[1]

    def body(x_ref, c_ref, s_ref, o_ref):
        xx = x_ref[...]
        cc = jnp.concatenate([c_ref[...]] * rep, axis=1) if rep > 1 else c_ref[...]
        ss = jnp.concatenate([s_ref[...]] * rep, axis=1) if rep > 1 else s_ref[...]
        low = (lax.broadcasted_iota(jnp.int32, xx.shape, 1) % 32) < 16

        def partner(v):
            return jnp.where(low, pltpu.roll(v, w - 16, 1), pltpu.roll(v, 16, 1))

        if adjoint:
            o_ref[...] = xx * cc + partner(xx * ss)
        else:
            o_ref[...] = xx * cc + partner(xx) * ss

    blk = pl.BlockSpec((tb, w), lambda i: (i, 0))
    tab = pl.BlockSpec((tb, cos.shape[1]), lambda i: (i, 0))
    return pl.pallas_call(
        body, name=name, grid=(t // tb,), in_specs=[blk, tab, tab], out_specs=blk,
        out_shape=jax.ShapeDtypeStruct((t, w), F32), compiler_params=_params(("parallel",)),
    )(xt, cos, sin)


def _attn_probs(q, kb, kc, sink, n, t):
    scale = A_HEAD_DIM ** -0.5
    s1 = _bdot(q, kb, NT) * scale
    s2 = _bdot(q, kc, NT) * scale
    qpos = n * WINDOW + lax.broadcasted_iota(jnp.int32, s1.shape, 0) % WINDOW
    kpos = (n - 1) * WINDOW + lax.broadcasted_iota(jnp.int32, s1.shape, 1)
    valid = (kpos >= 0) & (kpos < t) & (jnp.abs(kpos - qpos) <= WINDOW)
    s1 = jnp.where(valid, s1, -jnp.inf)
    mx = jnp.maximum(jnp.maximum(jnp.max(s1, axis=-1, keepdims=True), jnp.max(s2, axis=-1, keepdims=True)), sink)
    p1, p2, ps = jnp.exp(s1 - mx), jnp.exp(s2 - mx), jnp.exp(sink - mx)
    inv = 1.0 / (jnp.sum(p1, axis=-1, keepdims=True) + jnp.sum(p2, axis=-1, keepdims=True) + ps)
    return p1 * inv, p2 * inv, ps * inv


def _sink_rows(sink_ref):
    return jnp.concatenate([jnp.broadcast_to(sink_ref[r], (WINDOW, 1)) for r in range(A_REP)], axis=0)


def _attn_fwd(q, kp, vp, kc, vc, sink, name):
    hq, t, dh = q.shape
    nb = t // WINDOW
    lc = kc.shape[1]
    rows = A_REP * WINDOW

    def body(q_ref, k_ref, v_ref, kc_ref, vc_ref, sink_ref, o_ref):
        n = pl.program_id(1)
        start = pl.multiple_of(n * WINDOW, WINDOW)
        kb, vb = k_ref[pl.ds(start, 3 * WINDOW), :], v_ref[pl.ds(start, 3 * WINDOW), :]
        p1, p2, _ = _attn_probs(q_ref[...].reshape(rows, dh), kb, kc_ref[...], _sink_rows(sink_ref), n, t)
        o_ref[...] = (_bdot(p1, vb) + _bdot(p2, vc_ref[...])).reshape(A_REP, WINDOW, dh)

    qblk = pl.BlockSpec((A_REP, WINDOW, dh), lambda g, n: (g, n, 0))
    kfull = pl.BlockSpec((None, t + 2 * WINDOW, dh), lambda g, n: (g, 0, 0))
    cfull = pl.BlockSpec((None, lc, dh), lambda g, n: (g, 0, 0))
    return pl.pallas_call(
        body, name=name, grid=(hq // A_REP, nb),
        in_specs=[qblk, kfull, kfull, cfull, cfull, pl.BlockSpec((A_REP, 1, 1), lambda g, n: (g, 0, 0))],
        out_specs=qblk, out_shape=jax.ShapeDtypeStruct((hq, t, dh), F32),
        compiler_params=_params(("parallel", "parallel")),
    )(q, kp, vp, kc, vc, sink)


def _attn_bwd(q, kp, vp, kc, vc, sink, o, do, name):
    hq, t, dh = q.shape
    nb = t // WINDOW
    lc = kc.shape[1]
    scale = A_HEAD_DIM ** -0.5
    rows = A_REP * WINDOW

    def body(q_ref, k_ref, v_ref, kc_ref, vc_ref, sink_ref, o_ref, do_ref,
             dq_ref, dk_ref, dv_ref, dkc_ref, dvc_ref, dsink_ref):
        n = pl.program_id(1)

        @pl.when(n == 0)
        def _():
            dk_ref[...] = jnp.zeros_like(dk_ref)
            dv_ref[...] = jnp.zeros_like(dv_ref)
            dkc_ref[...] = jnp.zeros_like(dkc_ref)
            dvc_ref[...] = jnp.zeros_like(dvc_ref)
            dsink_ref[...] = jnp.zeros_like(dsink_ref)

        start = pl.multiple_of(n * WINDOW, WINDOW)
        band = pl.ds(start, 3 * WINDOW)
        qq, kb, vb, kcc, vcc = q_ref[...].reshape(rows, dh), k_ref[band, :], v_ref[band, :], kc_ref[...], vc_ref[...]
        p1, p2, ps = _attn_probs(qq, kb, kcc, _sink_rows(sink_ref), n, t)
        dout = do_ref[...].reshape(rows, dh)
        delta = jnp.sum(dout * o_ref[...].reshape(rows, dh), axis=-1, keepdims=True)
        ds1 = p1 * (_bdot(dout, vb, NT) - delta)
        ds2 = p2 * (_bdot(dout, vcc, NT) - delta)
        dq_ref[...] = ((_bdot(ds1, kb) + _bdot(ds2, kcc)) * scale).reshape(A_REP, WINDOW, dh)
        dk_ref[band, :] += _bdot(ds1.T, qq) * scale
        dv_ref[band, :] += _bdot(p1.T, dout)
        dkc_ref[...] += _bdot(ds2.T, qq) * scale
        dvc_ref[...] += _bdot(p2.T, dout)
        dsink_ref[...] += jnp.sum((-ps * delta).reshape(A_REP, WINDOW, 1), axis=1, keepdims=True)

    qblk = pl.BlockSpec((A_REP, WINDOW, dh), lambda g, n: (g, n, 0))
    kfull = pl.BlockSpec((None, t + 2 * WINDOW, dh), lambda g, n: (g, 0, 0))
    cfull = pl.BlockSpec((None, lc, dh), lambda g, n: (g, 0, 0))
    return pl.pallas_call(
        body, name=name, grid=(hq // A_REP, nb),
        in_specs=[qblk, kfull, kfull, cfull, cfull, pl.BlockSpec((A_REP, 1, 1), lambda g, n: (g, 0, 0)), qblk, qblk],
        out_specs=[qblk, kfull, kfull, cfull, cfull, pl.BlockSpec((A_REP, 8, 128), lambda g, n: (g, 0, 0))],
        out_shape=[jax.ShapeDtypeStruct(q.shape, F32), jax.ShapeDtypeStruct(kp.shape, F32),
                   jax.ShapeDtypeStruct(kp.shape, F32), jax.ShapeDtypeStruct(kc.shape, F32),
                   jax.ShapeDtypeStruct(kc.shape, F32), jax.ShapeDtypeStruct((hq, 8, 128), F32)],
        compiler_params=_params(("parallel", "arbitrary")),
    )(q, kp, vp, kc, vc, sink, o, do)


def _gate_fwd(zg, w2, b2, name):
    m = zg.shape[0]
    n = w2.shape[1]
    tm = _pick(m, (512, 256, 128, 64, 32, 16, 8))

    def body(z_ref, w_ref, b_ref, o_ref):
        o_ref[...] = jax.nn.log_sigmoid(_bdot(z_ref[...], w_ref[...]) + b_ref[...]) / B_GATE_NORM

    return pl.pallas_call(
        body, name=name, grid=(m // tm,),
        in_specs=[pl.BlockSpec((tm, zg.shape[1]), lambda i: (i, 0)), _full(w2.shape), _full(b2.shape)],
        out_specs=pl.BlockSpec((tm, n), lambda i: (i, 0)), out_shape=jax.ShapeDtypeStruct((m, n), F32),
        compiler_params=_params(("parallel",)),
    )(zg, w2, b2)


def _gate_bwd(zg, w2, b2, dla, name):
    m, rk = zg.shape
    n = w2.shape[1]
    tm = _pick(m, (512, 256, 128, 64, 32, 16, 8))

    def body(z_ref, w_ref, b_ref, d_ref, dz_ref, dw_ref, db_ref):
        @pl.when(pl.program_id(0) == 0)
        def _():
            dw_ref[...] = jnp.zeros_like(dw_ref)
            db_ref[...] = jnp.zeros_like(db_ref)

        zz, ww = z_ref[...], w_ref[...]
        pre = _bdot(zz, ww) + b_ref[...]
        dpre = d_ref[...] * (1.0 / B_GATE_NORM) * jax.nn.sigmoid(-pre)
        dz_ref[...] = _bdot(dpre, ww, NT)
        dw_ref[...] += _bdot(zz.T, dpre)
        db_ref[...] += jnp.sum(dpre, axis=0, keepdims=True)

    return pl.pallas_call(
        body, name=name, grid=(m // tm,),
        in_specs=[pl.BlockSpec((tm, rk), lambda i: (i, 0)), _full(w2.shape), _full(b2.shape),
                  pl.BlockSpec((tm, n), lambda i: (i, 0))],
        out_specs=[pl.BlockSpec((tm, rk), lambda i: (i, 0)), _full(w2.shape), _full(b2.shape)],
        out_shape=[jax.ShapeDtypeStruct((m, rk), F32), jax.ShapeDtypeStruct(w2.shape, F32),
                   jax.ShapeDtypeStruct(b2.shape, F32)],
        compiler_params=_params(("arbitrary",)),
    )(zg, w2, b2, dla)


def _chunk_order(step, n_x_chunks, n_chunks, reverse):
    n_c = n_chunks - n_x_chunks
    if reverse:
        return jnp.where(step < n_c, n_chunks - 1 - step, n_chunks - 1 - step)
    return jnp.where(step < n_c, n_x_chunks + step, step - n_c)


def _tri(reverse, transpose=False):
    i = lax.broadcasted_iota(jnp.int32, (B_CHUNK, B_CHUNK), 0)
    j = lax.broadcasted_iota(jnp.int32, (B_CHUNK, B_CHUNK), 1)
    if transpose:
        i, j = j, i
    return (j >= i) if reverse else (j <= i)


def _gla_chunk(q, k, la, reverse):
    g = _dot_01(_tri(reverse), la)
    last = 0 if reverse else B_CHUNK - 1
    gl = g[last:last + 1, :]
    eg, eng, egl = jnp.exp(g), jnp.exp(-g), jnp.exp(gl - g)
    decay_col = jnp.exp(jnp.sum(la.T, axis=1, keepdims=True))
    return q * (B_DK ** -0.5) * eg, k * eng, k * egl, eg, eng, egl, decay_col


def _head_of(shape, axis, width):
    return lax.broadcasted_iota(jnp.int32, shape, axis) // width


def _gla_fwd(q, k, v, la_f, la_b, n_x, name):
    tc, wk = q.shape
    wv = v.shape[1]
    hh = B_HEADS
    dk, dv = wk // hh, wv // hh
    nc, nxc = tc // B_CHUNK, n_x // B_CHUNK
    orders = [functools.partial(_chunk_order, n_x_chunks=nxc, n_chunks=nc, reverse=rev) for rev in (False, True)]

    def body(*refs):
        ins, outs, s_refs = refs[:8], refs[8:12], refs[12:]

        @pl.when(pl.program_id(0) == 0)
        def _():
            for s_ref in s_refs:
                s_ref[...] = jnp.zeros_like(s_ref)

        lane_head = _head_of((B_CHUNK, wk), 1, dk)
        row_head = _head_of((wk, dv), 0, dk)
        for di, reverse in enumerate((False, True)):
            q_ref, k_ref, v_ref, la_ref = ins[4 * di:4 * di + 4]
            o_ref, s_save_ref = outs[2 * di:2 * di + 2]
            s_ref = s_refs[di]
            qt, kt, ke, _, _, _, decay_col = _gla_chunk(q_ref[...], k_ref[...], la_ref[...], reverse)
            ke_t = ke.T
            s_prev = s_ref[...]
            update = jnp.zeros_like(s_prev)
            for h in range(hh):
                vv = v_ref[:, h * dv:(h + 1) * dv]
                qm = jnp.where(lane_head == h, qt, 0.0)
                att = jnp.where(_tri(reverse), _bdot(qm, kt, NT), 0.0)
                o_ref[:, h * dv:(h + 1) * dv] = _bdot(att, vv) + _bdot(qm, s_prev)
                update = jnp.where(row_head == h, _bdot(ke_t, vv), update)
            s_save_ref[...] = s_prev
            s_ref[...] = decay_col * s_prev + update

    def blk(w, order):
        return pl.BlockSpec((B_CHUNK, w), lambda s: (order(s), 0))

    def sblk(order):
        return pl.BlockSpec((None, wk, dv), lambda s: (order(s), 0, 0))

    in_specs, out_specs = [], []
    for order in orders:
        in_specs += [blk(wk, order), blk(wk, order), blk(wv, order), blk(wk, order)]
        out_specs += [blk(wv, order), sblk(order)]
    o_shape, s_shape = jax.ShapeDtypeStruct((tc, wv), F32), jax.ShapeDtypeStruct((nc, wk, dv), F32)
    return pl.pallas_call(
        body, name=name, grid=(nc,), in_specs=in_specs, out_specs=out_specs,
        out_shape=[o_shape, s_shape, o_shape, s_shape],
        scratch_shapes=[pltpu.VMEM((wk, dv), F32)] * 2,
        compiler_params=_params(("arbitrary",)),
    )(q, k, v, la_f, q, k, v, la_b)


def _gla_bwd(q, k, v, la_f, la_b, s_f, s_b, do, n_x, name):
    tc, wk = q.shape
    wv = v.shape[1]
    hh = B_HEADS
    dk, dv = wk // hh, wv // hh
    nc, nxc = tc // B_CHUNK, n_x // B_CHUNK
    orders = [functools.partial(lambda s, rev: _chunk_order(nc - 1 - s, nxc, nc, rev), rev=rev) for rev in (False, True)]

    def body(*refs):
        ins, outs, ds_refs = refs[:12], refs[12:20], refs[20:]

        @pl.when(pl.program_id(0) == 0)
        def _():
            for ds_ref in ds_refs:
                ds_ref[...] = jnp.zeros_like(ds_ref)

        lane_head = _head_of((B_CHUNK, wk), 1, dk)
        row_head = _head_of((wk, dv), 0, dk)
        for di, reverse in enumerate((False, True)):
            q_ref, k_ref, v_ref, la_ref, s_save_ref, do_ref = ins[6 * di:6 * di + 6]
            dq_ref, dk_ref, dv_ref, dla_ref = outs[4 * di:4 * di + 4]
            ds_ref = ds_refs[di]
            mask = _tri(reverse)
            last = 0 if reverse else B_CHUNK - 1
            is_last = lax.broadcasted_iota(jnp.int32, (B_CHUNK, wk), 0) == last
            la = la_ref[...]
            qt, kt, ke, eg, eng, egl, decay_col = _gla_chunk(q_ref[...], k_ref[...], la, reverse)
            qt_t = qt.T
            s_prev, ds_new = s_save_ref[...], ds_ref[...]
            dqt, dkt, dke = jnp.zeros_like(qt), jnp.zeros_like(qt), jnp.zeros_like(qt)
            ds_add = jnp.zeros_like(ds_new)
            for h in range(hh):
                cols = slice(h * dv, (h + 1) * dv)
                vv, dout = v_ref[:, cols], do_ref[:, cols]
                mine = lane_head == h
                qm, km = jnp.where(mine, qt, 0.0), jnp.where(mine, ke, 0.0)
                att = jnp.where(mask, _bdot(qm, kt, NT), 0.0)
                datt = jnp.where(mask, _bdot(dout, vv, NT), 0.0)
                dv_ref[:, cols] = _bdot(att.T, dout) + _bdot(km, ds_new)
                dqt = jnp.where(mine, _bdot(datt, kt) + _bdot(dout, s_prev, NT), dqt)
                dkt = jnp.where(mine, _bdot(datt.T, qt), dkt)
                dke = jnp.where(mine, _bdot(vv, ds_new, NT), dke)
                ds_add = jnp.where(row_head == h, _bdot(qt_t, dout), ds_add)
            ddecay_row = jnp.sum((ds_new * s_prev).T, axis=0, keepdims=True)
            decay_row = jnp.exp(jnp.sum(la, axis=0, keepdims=True))
            ds_ref[...] = decay_col * ds_new + ds_add
            dq_ref[...] = dqt * (B_DK ** -0.5) * eg
            dk_ref[...] = dkt * eng + dke * egl
            dgl = jnp.sum(dke * ke, axis=0, keepdims=True) + ddecay_row * decay_row
            dg = dqt * qt - dkt * kt - dke * ke + jnp.where(is_last, dgl, 0.0)
            dla_ref[...] = _dot_01(_tri(reverse, transpose=True), dg)

    def blk(w, order):
        return pl.BlockSpec((B_CHUNK, w), lambda s: (order(s), 0))

    in_specs, out_specs = [], []
    for order in orders:
        in_specs += [blk(wk, order), blk(wk, order), blk(wv, order), blk(wk, order),
                     pl.BlockSpec((None, wk, dv), lambda s, order=order: (order(s), 0, 0)), blk(wv, order)]
        out_specs += [blk(wk, order), blk(wk, order), blk(wv, order), blk(wk, order)]
    k_shape, v_shape = jax.ShapeDtypeStruct((tc, wk), F32), jax.ShapeDtypeStruct((tc, wv), F32)
    return pl.pallas_call(
        body, name=name, grid=(nc,), in_specs=in_specs, out_specs=out_specs,
        out_shape=[k_shape, k_shape, v_shape, k_shape] * 2,
        scratch_shapes=[pltpu.VMEM((wk, dv), F32)] * 2,
        compiler_params=_params(("arbitrary",)),
    )(q, k, v, la_f, s_f, do, q, k, v, la_b, s_b, do)


def _gla_out_fwd(o_f, o_b, r, g, name):
    t = r.shape[0]
    dv = g.shape[1]
    hh = r.shape[1] // dv
    tb = _pick(t, (256, 128, 64))

    def body(of_ref, ob_ref, r_ref, g_ref, out_ref):
        for h in range(hh):
            cols = slice(h * dv, (h + 1) * dv)
            o = of_ref[:, cols] + ob_ref[:, cols]
            rs = lax.rsqrt(jnp.mean(o * o, axis=-1, keepdims=True) + RMS_EPS)
            out_ref[:, cols] = (o * rs) * g_ref[...] * _silu(r_ref[:, cols])

    rblk = pl.BlockSpec((tb, hh * dv), lambda i: (i, 0))
    return pl.pallas_call(
        body, name=name, grid=(t // tb,), in_specs=[rblk, rblk, rblk, _full(g.shape)], out_specs=rblk,
        out_shape=jax.ShapeDtypeStruct((t, hh * dv), F32), compiler_params=_params(("parallel",)),
    )(o_f, o_b, r, g)


def _gla_out_bwd(o_f, o_b, r, g, dout, name):
    tc = o_f.shape[0]
    t = r.shape[0]
    dv = g.shape[1]
    hh = r.shape[1] // dv
    tb = _pick(int(np.gcd(t, tc)), (256, 128, 64))
    nt = t // tb

    def body(of_ref, ob_ref, r_ref, g_ref, d_ref, do_ref, dr_ref, dg_ref):
        i = pl.program_id(0)

        @pl.when(i == 0)
        def _():
            dg_ref[...] = jnp.zeros_like(dg_ref)

        @pl.when(i >= nt)
        def _():
            do_ref[...] = jnp.zeros_like(do_ref)

        @pl.when(i < nt)
        def _():
            gg = g_ref[...]
            for h in range(hh):
                cols = slice(h * dv, (h + 1) * dv)
                o = of_ref[:, cols] + ob_ref[:, cols]
                rs = lax.rsqrt(jnp.mean(o * o, axis=-1, keepdims=True) + RMS_EPS)
                nz = o * rs
                rr, dd = r_ref[:, cols], d_ref[:, cols]
                sg = jax.nn.sigmoid(rr)
                dr_ref[:, cols] = dd * nz * gg * (sg * (1.0 + rr * (1.0 - sg)))
                dy = dd * (rr * sg)
                dg_ref[...] += jnp.sum(dy * nz, axis=0, keepdims=True)
                dn = dy * gg
                do_ref[:, cols] = rs * (dn - nz * jnp.mean(dn * nz, axis=-1, keepdims=True))

    oblk = pl.BlockSpec((tb, hh * dv), lambda i: (i, 0))
    rblk = pl.BlockSpec((tb, hh * dv), lambda i: (jnp.minimum(i, nt - 1), 0))
    return pl.pallas_call(
        body, name=name, grid=(tc // tb,), in_specs=[oblk, oblk, rblk, _full(g.shape), rblk],
        out_specs=[oblk, rblk, _full(g.shape)],
        out_shape=[jax.ShapeDtypeStruct(o_f.shape, F32), jax.ShapeDtypeStruct(r.shape, F32),
                   jax.ShapeDtypeStruct(g.shape, F32)],
        compiler_params=_params(("arbitrary",)),
    )(o_f, o_b, r, g, dout)


def _pool_window(i, tb, t):
    return pl.multiple_of(jnp.clip(i * tb - POOL_PAD, 0, t - (tb + 2 * POOL_PAD)), 8)


def _pool_band(half, i, tb, start, adjoint):
    pos = i * tb + lax.broadcasted_iota(jnp.int32, (tb, tb + 2 * POOL_PAD), 0)
    tok = start + lax.broadcasted_iota(jnp.int32, (tb, tb + 2 * POOL_PAD), 1)
    if adjoint:
        return (tok > pos - half) & (tok <= pos + half)
    return (tok >= pos - half) & (tok < pos + half)


def _pool_count(pos, half, t):
    return (jnp.minimum(pos + half, t) - jnp.maximum(pos - half, 0)).astype(F32)


def _pool_fwd(h, w_pool, pool_scale, res, mods, km, name):
    t, d = res.shape
    ng, gw = w_pool.shape[0], w_pool.shape[1]
    tb = _pick(t, (256, 128, 64))

    def body(h_ref, w_ref, ps_ref, res_ref, mods_ref, out_ref, pooled_ref, ypre_ref):
        gi, i = pl.program_id(0), pl.program_id(1)
        half = jnp.left_shift(1, gi)
        start = _pool_window(i, tb, t)
        win = h_ref[pl.ds(start, tb + 2 * POOL_PAD), :]
        total = _dot_01(_pool_band(half, i, tb, start, False), win)
        pos = i * tb + lax.broadcasted_iota(jnp.int32, (tb, 1), 0)
        pooled = total / _pool_count(pos, half, t) - h_ref[pl.ds(pl.multiple_of(i * tb, tb), tb), :]
        ypre = _bdot(pooled, w_ref[...])
        pooled_ref[...] = pooled.astype(BF16)
        ypre_ref[...] = ypre
        out_ref[...] = res_ref[...] + mods_ref[0, km:km + 1, :] * (ypre * ps_ref[...])

    tile = pl.BlockSpec((tb, gw), lambda gi, i: (i, gi))
    return pl.pallas_call(
        body, name=name, grid=(ng, t // tb),
        in_specs=[pl.BlockSpec((t, gw), lambda gi, i: (0, gi)),
                  pl.BlockSpec((None, gw, gw), lambda gi, i: (gi, 0, 0)),
                  pl.BlockSpec((1, gw), lambda gi, i: (0, gi)), tile,
                  pl.BlockSpec((2, 16, gw), lambda gi, i: (0, 0, gi))],
        out_specs=[tile, tile, tile],
        out_shape=[jax.ShapeDtypeStruct((t, d), F32), jax.ShapeDtypeStruct((t, d), BF16),
                   jax.ShapeDtypeStruct((t, d), F32)],
        compiler_params=_params(("parallel", "parallel")),
    )(h, w_pool, pool_scale, res, mods)


def _pool_bwd(dxp, w_pool, pool_scale, pooled, ypre, mods, km, name):
    t, d = pooled.shape
    ng, gw = w_pool.shape[0], w_pool.shape[1]
    tb = _pick(t, (256, 128, 64))

    def body(dxp_ref, w_ref, ps_ref, pooled_ref, ypre_ref, mods_ref, dh_ref, dw_ref, acc_ref):
        gi, i = pl.program_id(0), pl.program_id(1)

        @pl.when(i == 0)
        def _():
            dw_ref[...] = jnp.zeros_like(dw_ref)
            acc_ref[...] = jnp.zeros_like(acc_ref)

        half = jnp.left_shift(1, gi)
        mod, ps = mods_ref[0, km:km + 1, :], ps_ref[...]
        start = _pool_window(i, tb, t)
        dwin = dxp_ref[pl.ds(start, tb + 2 * POOL_PAD), :]
        dpooled = _bdot(dwin * (mod * ps), w_ref[...], NT)
        pos = start + lax.broadcasted_iota(jnp.int32, (tb + 2 * POOL_PAD, 1), 0)
        spread = _dot_01(_pool_band(half, i, tb, start, True), dpooled / _pool_count(pos, half, t))
        dxc, yp = dxp_ref[pl.ds(pl.multiple_of(i * tb, tb), tb), :], ypre_ref[...]
        dh_ref[...] = spread - _bdot(dxc * (mod * ps), w_ref[...], NT)
        dw_ref[...] += _bdot(pooled_ref[...].astype(F32).T, dxc * (mod * ps))
        acc_ref[0:1, :] += jnp.sum(dxc * yp * mod, axis=0, keepdims=True)
        acc_ref[1:2, :] += jnp.sum(dxc * yp * ps, axis=0, keepdims=True)

    tile = pl.BlockSpec((tb, gw), lambda gi, i: (i, gi))
    wblk = pl.BlockSpec((None, gw, gw), lambda gi, i: (gi, 0, 0))
    return pl.pallas_call(
        body, name=name, grid=(ng, t // tb),
        in_specs=[pl.BlockSpec((t, gw), lambda gi, i: (0, gi)), wblk,
                  pl.BlockSpec((1, gw), lambda gi, i: (0, gi)), tile, tile,
                  pl.BlockSpec((2, 16, gw), lambda gi, i: (0, 0, gi))],
        out_specs=[tile, wblk, pl.BlockSpec((8, gw), lambda gi, i: (0, gi))],
        out_shape=[jax.ShapeDtypeStruct((t, d), F32), jax.ShapeDtypeStruct(w_pool.shape, F32),
                   jax.ShapeDtypeStruct((8, d), F32)],
        compiler_params=_params(("arbitrary", "arbitrary")),
    )(dxp, w_pool, pool_scale, pooled, ypre, mods)


def _adamw(w, g, m, v, name):
    r, c = w.shape
    tr = _pick(r, (512, 352, 256, 128, 64, 32, 16, 8))
    c1 = 1.0 / (1.0 - ADAM_B1 ** ADAM_STEP)
    c2 = 1.0 / (1.0 - ADAM_B2 ** ADAM_STEP)

    def body(w_ref, g_ref, m_ref, v_ref, d_ref, nm_ref, nv_ref):
        gg = g_ref[...]
        nm = ADAM_B1 * m_ref[...] + (1.0 - ADAM_B1) * gg
        nv = ADAM_B2 * v_ref[...] + (1.0 - ADAM_B2) * (gg * gg)
        nm_ref[...] = nm
        nv_ref[...] = nv
        d_ref[...] = -ADAM_LR * ((nm * c1) / (jnp.sqrt(nv * c2) + ADAM_EPS) + ADAM_WD * w_ref[...])

    blk = pl.BlockSpec((tr, c), lambda i: (i, 0))
    shp = jax.ShapeDtypeStruct((r, c), F32)
    return pl.pallas_call(
        body, name=name, grid=(r // tr,), in_specs=[blk] * 4, out_specs=[blk] * 3, out_shape=[shp] * 3,
        compiler_params=_params(("parallel",)),
    )(w, g, m, v)


def _heads(z, n_heads):
    m = z.shape[0]
    return z.reshape(m, n_heads, -1).transpose(1, 0, 2)


def _unheads(zh):
    return zh.transpose(1, 0, 2).reshape(zh.shape[1], -1)


def _pad_rows(a, n):
    return jnp.pad(a, ((0, 0), (n, n), (0, 0))) if a.ndim == 3 else jnp.pad(a, ((n, n), (0, 0)))


def _local_step(x, ctx, target, mods, wts, fetch, emit):
    t, d = x.shape
    l_ctx = ctx.shape[0]
    tc = t + l_ctx
    norm_g = wts["norm_g"]
    ng = lambda l, k: norm_g[l, k][None, :]
    grads = {}
    dmods = [[[None] * N_MOD for _ in range(2)] for _ in range(2)]
    dnorm = [[None] * 3 for _ in range(2)]

    def ffn_fwd(z, h, l, kbase, wi, wo, n_x, tag, nxt):
        au, act = _ffn_up(h, wi, 0, f"ffn_up_{tag}")
        wo = wo(act) if callable(wo) else wo
        outs = _mm_resid(act, wo, 0, z, mods[l], kbase + 2, 0.5, n_x, f"ffn_down_{tag}", nxt=nxt)
        return outs[0], (z, h, au, act, outs[1], wi, wo), (outs[2] if nxt is not None else None)

    def ffn_bwd(dz_new, saved, l, kbase, g, n_x, tag, stage, split=False):
        z, h, au, act, y, wi, wo = saved
        dy, acc_gate = _resid_bwd(dz_new, y, mods[l], kbase + 2, 0.5, n_x, f"resid_bwd_{tag}")
        dau = _ffn_down_bwd(dy, wo, 0, au, f"ffn_down_bwd_{tag}")
        dwo = _mm_tn(act, dy, BF16, f"dwo_{tag}")
        if split:
            token = emit(stage, [dwo])
            dwi_t = _mm_tn(dau, h, BF16, f"dwi_{tag}", dep=token)
            token = emit(stage + 1, [dwi_t])
        else:
            dwi_t = _mm_tn(dau, h, BF16, f"dwi_{tag}")
            token = emit(stage, [dwi_t, dwo])
        dh = _mm([(dau, wi, 0, 0)], NN, d, F32, f"dh_{tag}", tm_pref=TALL_TILES, dep=token)
        dz, acc_mod = _modulate_bwd(z, dh, dz_new, mods[l], g, kbase + 1, n_x, f"mod_bwd_{tag}", latent_only=split)
        return dz, acc_mod, acc_gate

    def record(l, kbase, k_norm, g, acc_mod, acc_gate, streams):
        total = None
        for s in range(streams):
            dmods[l][s][kbase] = acc_mod[s, 0]
            dmods[l][s][kbase + 1] = acc_mod[s, 1] * g[0]
            if acc_gate is not None:
                dmods[l][s][kbase + 2] = acc_gate[s, 0]
            part = acc_mod[s, 1] * (1.0 + mods[l][s, kbase + 1])
            total = part if total is None else total + part
        dnorm[l][k_norm] = total

    xc0 = _stack_rows(x, ctx, "stack_tokens")
    wi1_0 = fetch(0, None)["wi1_0"]
    h0 = _modulate(xc0, mods[0], ng(0, 0), 0, 1, t, BF16, "mod_l0f1")
    xc1, sv_f1, hc = ffn_fwd(xc0, h0, 0, 0, wi1_0, lambda act: fetch(1, act)["wo1_0"], t, "l0f1",
                             (mods[0], ng(0, 1), 3, 4, BF16))
    w_in_t = fetch(2, hc)["w_in_t"]
    n_proj = w_in_t.shape[1]
    zall = _mm([(hc, w_in_t, 0, 0)], NT, n_proj, F32, "proj", tm_pref=TALL_TILES,
               tn_pref=(n_proj,))
    offs = np.cumsum((0,) + PROJ_SIZES)
    part = lambda i, rows=slice(None): zall[rows, offs[i]:offs[i + 1]]
    lat, con = slice(0, t), slice(t, tc)
    cos, sin = _rope_tables(t)
    qa = _heads(_rope(part(0, lat), cos, sin, False, "rope_q"), A_HEADS)
    ka = _heads(_rope(part(1, lat), cos, sin, False, "rope_k"), A_KV_HEADS)
    va = _heads(part(2, lat), A_KV_HEADS)
    kca, vca = _heads(part(1, con), A_KV_HEADS), _heads(part(2, con), A_KV_HEADS)
    kap, vap = _pad_rows(ka, WINDOW), _pad_rows(va, WINDOW)
    sink = wts["sink"].reshape(A_HEADS, 1, 1)
    o_a = _attn_fwd(qa, kap, vap, kca, vca, sink, "attn_fwd")

    qb, kb, vb = part(3), part(4), part(5)
    rb = part(6, lat)
    zg = part(7)
    zg_f, zg_b = zg[:, :B_GATE_RANK], zg[:, B_GATE_RANK:]
    w2f, w2b, b2f, b2b = wts["w_a2_f"], wts["w_a2_b"], wts["b_a_f"], wts["b_a_b"]
    la_f = _gate_fwd(zg_f, w2f, b2f, "gate_f")
    la_b = _gate_fwd(zg_b, w2b, b2b, "gate_b")
    o_f, s_f, o_b, s_b = _gla_fwd(qb, kb, vb, la_f, la_b, t, "gla_fwd")
    gla_g = wts["gla_g"]
    go = _gla_out_fwd(o_f, o_b, rb, gla_g, "gla_out")
    cat = jnp.concatenate([_unheads(o_a), go], axis=-1).astype(BF16)
    big = fetch(3, cat)
    w_out, wi2_0, wo2_0 = big["w_out"], big["wi2_0"], big["wo2_0"]
    x2, y_mix0, h2 = _mm_resid(cat, w_out, 0, xc1, mods[0], 5, 1.0, t, "w_out", nxt=(mods[0], ng(0, 2), 6, 7, BF16))
    x3, sv_f2, h3 = ffn_fwd(x2, h2, 0, 6, wi2_0, wo2_0, t, "l0f2", (mods[1], ng(1, 0), 0, 1, BF16))

    big = fetch(4, x3)
    wi1_1, wo1_1, wi2_1, wo2_1 = big["wi1_1"], big["wo1_1"], big["wi2_1"], big["wo2_1"]
    x4, sv_g1, hp = ffn_fwd(x3, h3, 1, 0, wi1_1, wo1_1, t, "l1f1", (mods[1], ng(1, 1), 3, 4, F32))
    w_pool, pool_scale = wts["w_pool"], wts["pool_scale"]
    x5, pooled, ypre = _pool_fwd(hp, w_pool, pool_scale, x4, mods[1], 5, "pool_fwd")
    h5 = _modulate(x5, mods[1], ng(1, 2), 6, 7, t, BF16, "mod_l1f2")
    x6, sv_g2, _ = ffn_fwd(x5, h5, 1, 6, wi2_1, wo2_1, t, "l1f2", None)

    dx6, loss_vec, dfinal_g = _final_loss(x6, wts["final_g"], target, "final_loss")
    grads["final_g"] = dfinal_g[0]

    dx5, acc_mod, acc_gate = ffn_bwd(dx6, sv_g2, 1, 6, ng(1, 2), t, "l1f2", 0)
    record(1, 6, 2, ng(1, 2), acc_mod, acc_gate, 1)
    dhp, dw_pool, acc_pool = _pool_bwd(dx5, w_pool, pool_scale, pooled, ypre, mods[1], 5, "pool_bwd")
    grads["pool_scale"] = acc_pool[0]
    dmods[1][0][5] = acc_pool[1]
    dx4, acc_mod = _modulate_bwd(x4, dhp, dx5, mods[1], ng(1, 1), 4, t, "mod_bwd_l1mix")
    record(1, 3, 1, ng(1, 1), acc_mod, None, 1)
    dx3, acc_mod, acc_gate = ffn_bwd(dx4, sv_g1, 1, 0, ng(1, 0), t, "l1f1", 1)
    record(1, 0, 0, ng(1, 0), acc_mod, acc_gate, 1)

    dx2, acc_mod, acc_gate = ffn_bwd(dx3, sv_f2, 0, 6, ng(0, 2), t, "l0f2", 2)
    record(0, 6, 2, ng(0, 2), acc_mod, acc_gate, 1)
    dymix, acc_gate = _resid_bwd(dx2, y_mix0, mods[0], 5, 1.0, t, "resid_bwd_mix")
    dmods[0][0][5] = acc_gate[0, 0]
    dw_out = _mm_tn(cat, dymix, BF16, "dw_out")
    dcat = _mm([(dymix, w_out, 0, 0)], NT, cat.shape[1], F32, "dcat")
    do_a = _heads(dcat[:, :A_Q], A_HEADS)
    do_full, drb, dgla_g = _gla_out_bwd(o_f, o_b, rb, gla_g, dcat[:, A_Q:], "gla_out_bwd")
    grads["gla_g"] = dgla_g[0]
    dq_f, dk_f, dv_f, dla_f, dq_b, dk_b, dv_b, dla_b = _gla_bwd(qb, kb, vb, la_f, la_b, s_f, s_b, do_full, t, "gla_bwd")
    dzg_f, dw2f, db2f = _gate_bwd(zg_f, w2f, b2f, dla_f, "gate_bwd_f")
    dzg_b, dw2b, db2b = _gate_bwd(zg_b, w2b, b2b, dla_b, "gate_bwd_b")
    grads.update(w_a2_f=dw2f, w_a2_b=dw2b, b_a_f=db2f[0], b_a_b=db2b[0])
    dqa_r, dkap, dvap, dkca, dvca, dsink = _attn_bwd(qa, kap, vap, kca, vca, sink, o_a, do_a, "attn_bwd")
    grads["sink"] = dsink[:, 0, 0]
    dqa = _rope(_unheads(dqa_r), cos, sin, True, "rope_bwd_q")
    dka = _rope(_unheads(dkap[:, WINDOW:WINDOW + t]), cos, sin, True, "rope_bwd_k")
    dva = dvap[:, WINDOW:WINDOW + t]
    dzg = jnp.concatenate([dzg_f, dzg_b, jnp.zeros((tc, n_proj - PROJ_DIM), F32)], axis=-1)
    dzall = _assemble_dz(
        [dqa, dka, _unheads(dva), None, None, None, drb, None],
        [None, _unheads(dkca), _unheads(dvca), None, None, None, None, None],
        [None, None, None, [dq_f, dq_b], [dk_f, dk_b], [dv_f, dv_b], None, [dzg]], n_proj, "assemble_dz")
    dw_in_t = _mm_tn(dzall, hc, BF16, "dw_in")
    token = emit(3, [dw_in_t, dw_out, dw_pool])
    dhc = _mm([(dzall, w_in_t, 0, 0)], NN, d, F32, "dhc", tm_pref=TALL_TILES, dep=token)
    dxc1, acc_mod = _modulate_bwd(xc1, dhc, dx2, mods[0], ng(0, 1), 4, t, "mod_bwd_l0mix")
    record(0, 3, 1, ng(0, 1), acc_mod, None, 2)
    dxc0, acc_mod, acc_gate = ffn_bwd(dxc1, sv_f1, 0, 0, ng(0, 0), t, "l0f1", 4, split=True)
    record(0, 0, 0, ng(0, 0), acc_mod, acc_gate, 2)

    grads["norm_g"] = jnp.stack([jnp.stack(dnorm[0]), jnp.stack(dnorm[1])])
    zero = jnp.zeros((d,), F32)
    dmods_arr = jnp.stack([jnp.stack([jnp.stack([v if v is not None else zero for v in dmods[l][s]])
                                      for s in range(2)]) for l in range(2)])
    return loss_vec, dxc0, grads, dmods_arr


def _pack(parts):
    flat = jnp.concatenate([p.reshape(-1).astype(F32) for p in parts])
    pad = (-flat.shape[0]) % 128
    return jnp.pad(flat, (0, pad))[None, :]


def _unpack(rows, shapes):
    out, off = [], 0
    for s in shapes:
        n = int(np.prod(s))
        out.append(rows[:, off:off + n].reshape((rows.shape[0],) + tuple(s)))
        off += n
    return out


def _cols_to_full(g):
    g = jnp.moveaxis(g, 0, -2)
    return g.reshape(g.shape[:-2] + (-1,))


def kernel(x, c, ctx, c_ctx, w_mod, b_mod, norm_g, ffn1_wi, ffn1_wo, ffn2_wi, ffn2_wo, w_in, w_a2_f, b_a_f, w_a2_b, b_a_b, sink, gla_g, w_out, w_pool, pool_scale, final_g, loss_target, m_c_ctx, m_w_mod, m_b_mod, m_norm_g, m_ffn1_wi, m_ffn1_wo, m_ffn2_wi, m_ffn2_wo, m_w_in, m_w_a2_f, m_b_a_f, m_w_a2_b, m_b_a_b, m_sink, m_gla_g, m_w_out, m_w_pool, m_pool_scale, m_final_g, v_c_ctx, v_w_mod, v_b_mod, v_norm_g, v_ffn1_wi, v_ffn1_wo, v_ffn2_wi, v_ffn2_wo, v_w_in, v_w_a2_f, v_b_a_f, v_w_a2_b, v_b_a_b, v_sink, v_gla_g, v_w_out, v_w_pool, v_pool_scale, v_final_g):
    t, d = x.shape[1], x.shape[2]
    me = _dev_index()
    nc = w_mod.shape[2]
    ncol_in = w_in.shape[2]
    ncol_pad = -(-ncol_in // 16) * 16

    small_shapes = [(d,), norm_g.shape, pool_scale.shape, w_a2_f.shape, w_a2_b.shape, w_pool.shape]
    g1 = _gather_small(_pack([c, norm_g, pool_scale, w_a2_f, w_a2_b, w_pool]), "gather_params")
    c_all, norm_g_all, pool_scale_all, w2f_all, w2b_all, w_pool_all = _unpack(g1, small_shapes)
    wts = {
        "norm_g": _cols_to_full(norm_g_all),
        "pool_scale": _cols_to_full(pool_scale_all),
        "w_a2_f": _cols_to_full(w2f_all)[0],
        "w_a2_b": _cols_to_full(w2b_all)[0],
        "w_pool": jnp.moveaxis(w_pool_all[:, 0], 0, 1).reshape(w_pool.shape[1], -1, w_pool.shape[3]),
        "b_a_f": b_a_f, "b_a_b": b_a_b, "sink": sink[0], "gla_g": gla_g, "final_g": final_g[None, :],
    }

    craw = jnp.concatenate([c_all, c_ctx[None, :], jnp.zeros((16 - N_DEV - 1, d), F32)], axis=0)
    b_cols = lax.dynamic_slice_in_dim(b_mod, me * nc, nc, axis=1)[:, None, :]
    mm_cols = _adaln_fwd(craw, w_mod, b_cols, "adaln_fwd")
    g2 = _gather_small(mm_cols.reshape(1, -1), "gather_mods").reshape(N_DEV, 2, 16, nc)
    mm_full = jnp.moveaxis(g2, 0, 2).reshape(2, 16, N_MOD, d)
    mods = jnp.stack([lax.dynamic_index_in_dim(mm_full, me, axis=1, keepdims=False), mm_full[:, N_DEV]], axis=1)
    mods = jnp.pad(mods, ((0, 0), (0, 0), (0, 16 - N_MOD), (0, 0)))

    tr = lambda w: jnp.swapaxes(w, 1, 2).astype(BF16)
    wi1_sh, wi2_sh, wo1_sh, wo2_sh = tr(ffn1_wi), tr(ffn2_wi), ffn1_wo.astype(BF16), ffn2_wo.astype(BF16)
    w_in_sh = jnp.pad(tr(w_in), ((0, 0), (0, ncol_pad - ncol_in), (0, 0)))
    groups = [
        {"wi1_0": wi1_sh[0:1]},
        {"wo1_0": wo1_sh[0:1]},
        {"w_in": w_in_sh},
        {"w_out": w_out.astype(BF16), "wi2_0": wi2_sh[0:1], "wo2_0": wo2_sh[0:1]},
        {"wi1_1": wi1_sh[1:2], "wo1_1": wo1_sh[1:2], "wi2_1": wi2_sh[1:2], "wo2_1": wo2_sh[1:2]},
    ]

    gathers, token = [], mods
    for gi, grp in enumerate(groups):
        lands = [_place_shard(s, me, f"gather_place_{nm}") for nm, s in grp.items()]
        gathers.append(_exchange_start(list(grp.values()), lands, True, 1 + gi, token, f"gather_start_{gi}"))
        token = gathers[-1][4]
    n_proj = -(-(N_DEV * ncol_in) // 128) * 128

    def fetch(gi, after):
        _, lands = _exchange_wait(gathers[gi], True, token if after is None else after, f"gather_wait_{gi}")
        out = dict(zip(groups[gi].keys(), lands))
        if "w_in" in out:
            w_in_t = out.pop("w_in").reshape(1, N_DEV, ncol_pad, d)[:, :, :ncol_in].reshape(1, N_DEV * ncol_in, d)
            out["w_in_t"] = jnp.pad(w_in_t, ((0, 0), (0, n_proj - N_DEV * ncol_in), (0, 0)))
        return out

    scatters = []

    def emit(stage, arrays):
        if stage == 3:
            dw_in_t, dw_out, dw_pool = arrays
            dw_in_full = dw_in_t[:N_DEV * ncol_in].reshape(N_DEV, ncol_in, d)
            dw_in_full = jnp.pad(dw_in_full, ((0, 0), (0, ncol_pad - ncol_in), (0, 0)))
            srcs = [dw_in_full.reshape(1, N_DEV * ncol_pad, d), dw_out[None], dw_pool.astype(BF16)]
        else:
            srcs = [a[None] for a in arrays]
        lands = [lax.empty((N_DEV, s.shape[0], s.shape[1] // N_DEV, s.shape[2]), s.dtype) for s in srcs]
        scatters.append(_exchange_start(srcs, lands, False, 1 + len(groups) + stage, None, f"scatter_start_{stage}"))
        return scatters[-1][4]

    loss_vec, grad_x, grads, dmods = _local_step(x[0], ctx[0], loss_target[0], mods, wts, fetch, emit)
    loss = lax.psum(jnp.sum(loss_vec), ("x", "y", "c"))

    def reduce_stage(stage, after):
        wholes, lands = _exchange_wait(scatters[stage], False, after, f"scatter_wait_{stage}")
        return [_sum_slots(ld, wh, me, f"sum_grad_{stage}_{i}") for i, (ld, wh) in enumerate(zip(lands, wholes))]

    (dwi2_1, dwo2_1), (dwi1_1, dwo1_1), (dwi2_0, dwo2_0), (dw_in_s, dw_out_s, dw_pool_s) = [
        reduce_stage(stage, grad_x) for stage in range(4)]
    back = lambda g: jnp.swapaxes(g, 1, 2)
    g_big = {
        "ffn2_wi": back(jnp.concatenate([dwi2_0, dwi2_1], axis=0)), "ffn2_wo": jnp.concatenate([dwo2_0, dwo2_1], axis=0),
        "w_in": back(dw_in_s[:, :ncol_in]), "w_out": dw_out_s, "w_pool": dw_pool_s[None],
    }

    order = ["c_ctx", "w_mod", "b_mod", "norm_g", "ffn1_wi", "ffn1_wo", "ffn2_wi", "ffn2_wo", "w_in", "w_a2_f", "b_a_f",
             "w_a2_b", "b_a_b", "sink", "gla_g", "w_out", "w_pool", "pool_scale", "final_g"]
    ws = dict(c_ctx=c_ctx, w_mod=w_mod, b_mod=b_mod, norm_g=norm_g, ffn1_wi=ffn1_wi, ffn1_wo=ffn1_wo, ffn2_wi=ffn2_wi,
              ffn2_wo=ffn2_wo, w_in=w_in, w_a2_f=w_a2_f, b_a_f=b_a_f, w_a2_b=w_a2_b, b_a_b=b_a_b, sink=sink, gla_g=gla_g,
              w_out=w_out, w_pool=w_pool, pool_scale=pool_scale, final_g=final_g)
    ms = dict(c_ctx=m_c_ctx, w_mod=m_w_mod, b_mod=m_b_mod, norm_g=m_norm_g, ffn1_wi=m_ffn1_wi, ffn1_wo=m_ffn1_wo,
              ffn2_wi=m_ffn2_wi, ffn2_wo=m_ffn2_wo, w_in=m_w_in, w_a2_f=m_w_a2_f, b_a_f=m_b_a_f, w_a2_b=m_w_a2_b,
              b_a_b=m_b_a_b, sink=m_sink, gla_g=m_gla_g, w_out=m_w_out, w_pool=m_w_pool, pool_scale=m_pool_scale,
              final_g=m_final_g)
    vs = dict(c_ctx=v_c_ctx, w_mod=v_w_mod, b_mod=v_b_mod, norm_g=v_norm_g, ffn1_wi=v_ffn1_wi, ffn1_wo=v_ffn1_wo,
              ffn2_wi=v_ffn2_wi, ffn2_wo=v_ffn2_wo, w_in=v_w_in, w_a2_f=v_w_a2_f, b_a_f=v_b_a_f, w_a2_b=v_w_a2_b,
              b_a_b=v_b_a_b, sink=v_sink, gla_g=v_gla_g, w_out=v_w_out, w_pool=v_w_pool, pool_scale=v_pool_scale,
              final_g=v_final_g)
    early, late = ["ffn2_wi", "ffn2_wo", "w_out", "w_in", "w_pool"], ["ffn1_wi", "ffn1_wo"]
    big = early + ["w_mod"] + late
    delta, new_m, new_v = {}, {}, {}
    g_all = dict(g_big)

    def adamw_big(nm):
        shp = ws[nm].shape
        two_d = lambda a: a.reshape(-1, shp[-1])
        dl, nm_, nv_ = _adamw(two_d(ws[nm]), two_d(g_all[nm]), two_d(ms[nm]), two_d(vs[nm]), f"adamw_{nm}")
        delta[nm], new_m[nm], new_v[nm] = dl.reshape(shp), nm_.reshape(shp), nv_.reshape(shp)

    for nm in early:
        adamw_big(nm)

    small_g = [dmods[:, :, :N_MOD].reshape(2, 2, N_MOD * d), grads["norm_g"], grads["pool_scale"], grads["final_g"],
               grads["b_a_f"], grads["b_a_b"], grads["sink"], grads["gla_g"], grads["w_a2_f"], grads["w_a2_b"]]
    small_g_shapes = [a.shape for a in small_g]
    g3 = _gather_small(_pack(small_g), "gather_small_grads", dep=delta["w_out"])
    total = _sum_rows8(g3, "sum_small_grads")
    dmm_all = _unpack(g3, small_g_shapes[:1])[0]
    (dmm_sum, dnorm_g, dpool_scale, dfinal_g, db_a_f, db_a_b, dsink, dgla_g, dw_a2_f, dw_a2_b) = [
        a[0] for a in _unpack(total, small_g_shapes)]
    dmm_rows = jnp.concatenate([dmm_all[:, :, 0].transpose(1, 0, 2), dmm_sum[:, 1][:, None, :],
                                jnp.zeros((2, 16 - N_DEV - 1, N_MOD * d), F32)], axis=1)
    grad_b_mod = dmm_sum[:, 0] + dmm_sum[:, 1]
    dmm_cols = lax.dynamic_slice_in_dim(dmm_rows, me * nc, nc, axis=2)
    cs_t = jnp.transpose(_silu(craw)).astype(BF16)
    grad_w_mod, dcraw = _adaln_bwd(craw, cs_t, dmm_cols, w_mod, "adaln_bwd")
    g4 = _gather_small((dcraw[0, N_DEV] + dcraw[1, N_DEV])[None, :], "gather_c_ctx_grad")
    grad_c_ctx = _sum_rows8(g4, "sum_c_ctx_grad")[0]

    col = lambda v, n: lax.dynamic_slice_in_dim(v, me * n, n, axis=v.ndim - 1)
    g_small = {
        "c_ctx": grad_c_ctx, "b_mod": grad_b_mod, "norm_g": col(dnorm_g, norm_g.shape[2]),
        "w_a2_f": col(dw_a2_f, w_a2_f.shape[2])[None], "b_a_f": db_a_f[None], "w_a2_b": col(dw_a2_b, w_a2_b.shape[2])[None],
        "b_a_b": db_a_b[None], "sink": dsink[None], "gla_g": dgla_g[None], "pool_scale": col(dpool_scale, pool_scale.shape[1])[None],
        "final_g": dfinal_g,
    }
    g_all.update(g_small, w_mod=grad_w_mod)
    adamw_big("w_mod")
    rest = [nm for nm in order if nm not in big]
    rest_shapes = [ws[nm].shape for nm in rest]
    packed = [_pack([d_[nm].reshape(ws[nm].shape) for nm in rest]).reshape(-1, 128) for d_ in (ws, g_all, ms, vs)]
    pad_rows = (-packed[0].shape[0]) % 512
    packed = [jnp.pad(p, ((0, pad_rows), (0, 0))) for p in packed]
    outs = _adamw(*packed, "adamw_small")
    for dst, arr in zip((delta, new_m, new_v), outs):
        for nm, val in zip(rest, _unpack(arr.reshape(1, -1), rest_shapes)):
            dst[nm] = val[0]

    (dwo1_0,), (dwi1_0,) = reduce_stage(4, outs[0]), reduce_stage(5, outs[0])
    g_all["ffn1_wi"] = back(jnp.concatenate([dwi1_0, dwi1_1], axis=0))
    g_all["ffn1_wo"] = jnp.concatenate([dwo1_0, dwo1_1], axis=0)
    for nm in late:
        adamw_big(nm)
    g_all = {nm: g_all[nm].reshape(ws[nm].shape) for nm in order}

    return (loss, grad_x[None], *[g_all[nm] for nm in order], *[delta[nm] for nm in order],
            *[new_m[nm] for nm in order], *[new_v[nm] for nm in order])
```

```python
import functools

import numpy as np
import jax
import jax.numpy as jnp
from jax import lax
from jax.experimental import pallas as pl
from jax.experimental.pallas import tpu as pltpu

F32 = jnp.float32
BF16 = jnp.bfloat16
MESH = pl.DeviceIdType.MESH

N_DEV = 8
RMS_EPS = 1e-6
N_MOD = 9
GRID_W = 64
A_HEADS, A_KV_HEADS, A_HEAD_DIM = 8, 2, 64
A_REP = A_HEADS // A_KV_HEADS
WINDOW = 128
ROPE_BASE = 10000.0
B_HEADS, B_DK, B_DV = 4, 64, 128
B_GATE_RANK = 16
B_GATE_NORM = 16.0
B_CHUNK = 64
POOL_WINDOWS = (2, 4, 8, 16)
POOL_PAD = 8
A_Q = A_HEADS * A_HEAD_DIM
A_KV = A_KV_HEADS * A_HEAD_DIM
B_QK = B_HEADS * B_DK
B_V = B_HEADS * B_DV
PROJ_SIZES = (A_Q, A_KV, A_KV, B_QK, B_QK, B_V, B_V, 2 * B_GATE_RANK)
PROJ_DIM = sum(PROJ_SIZES)
ADAM_LR, ADAM_B1, ADAM_B2, ADAM_EPS, ADAM_WD, ADAM_STEP = 0.001, 0.9, 0.999, 1e-08, 0.01, 10

VMEM_LIMIT = 56 * 1024 * 1024
ROW_TILES = (512, 544, 256, 128, 64, 32, 16, 8)
TALL_TILES = (1024, 1088) + ROW_TILES

NN = ((1,), (0,))
NT = ((1,), (1,))
TN = ((0,), (0,))


def _dot(a, b, dims=NN, prec=None):
    return lax.dot_general(a, b, (dims, ((), ())), precision=prec, preferred_element_type=F32)


def _bdot(a, b, dims=NN):
    return _dot(a.astype(BF16), b.astype(BF16), dims)


def _dot_01(sel, x):
    hi = x.astype(BF16)
    rest = x - hi.astype(F32)
    mid = rest.astype(BF16)
    lo = (rest - mid.astype(F32)).astype(BF16)
    sel = sel.astype(BF16)
    return _dot(sel, hi) + _dot(sel, mid) + _dot(sel, lo)


def _params(sem=None, **kw):
    return pltpu.CompilerParams(dimension_semantics=sem, vmem_limit_bytes=VMEM_LIMIT, **kw)


def _silu(a):
    return a * jax.nn.sigmoid(a)


def _pick(n, prefs):
    for p in prefs:
        if n % p == 0:
            return p
    return n


def _full(shape):
    nd = len(shape)
    return pl.BlockSpec(shape, lambda *_: (0,) * nd)


def _peers():
    x, y, c = lax.axis_index("x"), lax.axis_index("y"), lax.axis_index("c")
    return x, y, c


def _dev_index():
    x, y, c = _peers()
    return 4 * x + 2 * y + c


def _others(x, y, c):
    return [(x, y, 1 - c), (1 - x, y, c), (x, 1 - y, c), (1 - x, 1 - y, c),
            (1 - x, y, 1 - c), (x, 1 - y, 1 - c), (1 - x, 1 - y, 1 - c)]


def _index_of(dev):
    return 4 * dev[0] + 2 * dev[1] + dev[2]


def _exchange_refs(gather, shapes, srcs, lands, a, me, to):
    if gather:
        r = shapes[a][1]
        return srcs[a], lands[a].at[:, pl.ds(_index_of(me) * r, r), :]
    r = shapes[a][1] // N_DEV
    return srcs[a].at[:, pl.ds(_index_of(to) * r, r), :], lands[a].at[_index_of(me)]


HBM_SPEC = pl.BlockSpec(memory_space=pltpu.HBM)
SEM_SPEC = pl.BlockSpec(memory_space=pltpu.SEMAPHORE)
EFFECT = pltpu.SideEffectType.DATAFLOW_SIDE_EFFECTING


def _exchange_start(srcs, lands, gather, collective_id, dep, name):
    n = len(srcs)
    shapes = [s.shape for s in srcs]
    deps = [] if dep is None else [dep]

    def body(*refs):
        src_refs, land_refs = refs[:n], refs[n:2 * n]
        send_sems, recv_sems = refs[2 * n + len(deps)], refs[2 * n + len(deps) + 1]
        token = refs[-1]
        x, y, c = _peers()
        others = _others(x, y, c)
        barrier = pltpu.get_barrier_semaphore()
        for peer in others:
            pl.semaphore_signal(barrier, inc=1, device_id=peer, device_id_type=MESH)
        pl.semaphore_wait(barrier, len(others))
        for a in range(n):
            for k, to in enumerate(others):
                src, dst = _exchange_refs(gather, shapes, src_refs, land_refs, a, (x, y, c), to)
                pltpu.make_async_remote_copy(src_ref=src, dst_ref=dst, send_sem=send_sems.at[7 * a + k],
                                             recv_sem=recv_sems.at[7 * a + k], device_id=to, device_id_type=MESH).start()
        token[...] = jnp.zeros_like(token)

    outs = pl.pallas_call(
        body, name=name,
        out_shape=(pltpu.SemaphoreType.DMA((7 * n,)), pltpu.SemaphoreType.DMA((7 * n,)),
                   *[pltpu.HBM(s.shape, s.dtype) for s in srcs], *[pltpu.HBM(l.shape, l.dtype) for l in lands],
                   jax.ShapeDtypeStruct((8, 128), F32)),
        in_specs=[HBM_SPEC] * (2 * n) + [pl.BlockSpec(memory_space=pl.ANY)] * len(deps),
        out_specs=(SEM_SPEC, SEM_SPEC, *[HBM_SPEC] * (2 * n), pl.BlockSpec(memory_space=pltpu.VMEM)),
        input_output_aliases={i: 2 + i for i in range(2 * n)},
        compiler_params=pltpu.CompilerParams(has_side_effects=EFFECT, collective_id=collective_id),
    )(*[pltpu.with_memory_space_constraint(s, pltpu.HBM) for s in srcs],
      *[pltpu.with_memory_space_constraint(l, pltpu.HBM) for l in lands], *deps)
    return outs[0], outs[1], list(outs[2:2 + n]), list(outs[2 + n:2 + 2 * n]), outs[-1]


def _exchange_wait(started, gather, after, name):
    send_sems, recv_sems, srcs, lands, _ = started
    n = len(srcs)
    shapes = [s.shape for s in srcs]

    def body(*refs):
        src_refs, land_refs = refs[:n], refs[n:2 * n]
        send_sems, recv_sems = refs[2 * n], refs[2 * n + 1]
        x, y, c = _peers()
        for a in range(n):
            for k, peer in enumerate(_others(x, y, c)):
                src, _ = _exchange_refs(gather, shapes, src_refs, land_refs, a, (x, y, c), peer)
                _, dst = _exchange_refs(gather, shapes, src_refs, land_refs, a, peer, (x, y, c))
                copy = pltpu.make_async_remote_copy(src_ref=src, dst_ref=dst, send_sem=send_sems.at[7 * a + k],
                                                    recv_sem=recv_sems.at[7 * a + k], device_id=peer, device_id_type=MESH)
                copy.wait_send()
                copy.wait_recv()

    outs = pl.pallas_call(
        body, name=name,
        out_shape=(*[pltpu.HBM(s.shape, s.dtype) for s in srcs], *[pltpu.HBM(l.shape, l.dtype) for l in lands]),
        in_specs=[HBM_SPEC] * (2 * n) + [SEM_SPEC, SEM_SPEC, pl.BlockSpec(memory_space=pl.ANY)],
        out_specs=tuple([HBM_SPEC] * (2 * n)),
        input_output_aliases={i: i for i in range(2 * n)},
        compiler_params=pltpu.CompilerParams(has_side_effects=EFFECT),
    )(*srcs, *lands, send_sems, recv_sems, after)
    return list(outs[:n]), list(outs[n:])


def _place_shard(shard, me, name):
    a_, r, c = shard.shape
    tr = _pick(r, (352, 304, 256, 128, 64, 32, 16, 8))
    nr = r // tr

    def body(me_ref, in_ref, out_ref):
        out_ref[...] = in_ref[...]

    return pl.pallas_call(
        body, name=name,
        grid_spec=pltpu.PrefetchScalarGridSpec(
            num_scalar_prefetch=1, grid=(a_, nr),
            in_specs=[pl.BlockSpec((None, tr, c), lambda i, j, me_ref: (i, j, 0))],
            out_specs=pl.BlockSpec((None, tr, c), lambda i, j, me_ref: (i, me_ref[0] * nr + j, 0))),
        out_shape=jax.ShapeDtypeStruct((a_, N_DEV * r, c), shard.dtype),
        compiler_params=_params(("parallel", "parallel")),
    )(me.reshape(1).astype(jnp.int32), shard)


def _sum_slots(land, whole, me, name):
    _, a_, r, c = land.shape
    tr = _pick(r, (352, 256, 128, 64, 32, 16, 8))
    nr = r // tr

    def body(me_ref, land_ref, own_ref, out_ref):
        acc = None
        for s in range(N_DEV):
            part = jnp.where(me_ref[0] == s, own_ref[...], land_ref[s]).astype(F32)
            acc = part if acc is None else acc + part
        out_ref[...] = acc

    return pl.pallas_call(
        body, name=name,
        grid_spec=pltpu.PrefetchScalarGridSpec(
            num_scalar_prefetch=1, grid=(a_, nr),
            in_specs=[pl.BlockSpec((N_DEV, None, tr, c), lambda i, j, me_ref: (0, i, j, 0)),
                      pl.BlockSpec((None, tr, c), lambda i, j, me_ref: (i, me_ref[0] * nr + j, 0))],
            out_specs=pl.BlockSpec((None, tr, c), lambda i, j, me_ref: (i, j, 0))),
        out_shape=jax.ShapeDtypeStruct((a_, r, c), F32),
        compiler_params=_params(("parallel", "parallel")),
    )(me.reshape(1).astype(jnp.int32), land, whole)


def _gather_small(vec, name, dep=None):
    p = vec.shape[1]
    pp = -(-p // 1024) * 1024
    blk = jnp.pad(vec, ((0, 0), (0, pp - p))).reshape(8, pp // 8)
    deps = [] if dep is None else [dep]

    def body(in_ref, *rest):
        out_ref, send_sems, recv_sems = rest[-3:]
        x, y, c = _peers()
        me = 4 * x + 2 * y + c
        others = [(x, y, 1 - c), (1 - x, y, c), (x, 1 - y, c), (1 - x, 1 - y, c),
                  (1 - x, y, 1 - c), (x, 1 - y, 1 - c), (1 - x, 1 - y, 1 - c)]

        def rows(idx):
            return out_ref.at[pl.ds(pl.multiple_of(idx * 8, 8), 8), :]

        out_ref[pl.ds(pl.multiple_of(me * 8, 8), 8), :] = in_ref[...]

        def copy(k, dev, slot):
            return pltpu.make_async_remote_copy(
                src_ref=in_ref, dst_ref=rows(slot), send_sem=send_sems.at[k], recv_sem=recv_sems.at[k],
                device_id=dev, device_id_type=MESH)

        sends = [copy(k, dev, me) for k, dev in enumerate(others)]
        for cp in sends:
            cp.start()
        for k, dev in enumerate(others):
            copy(k, dev, 4 * dev[0] + 2 * dev[1] + dev[2]).wait_recv()
        for cp in sends:
            cp.wait_send()

    vm = pl.BlockSpec(memory_space=pltpu.VMEM)
    out = pl.pallas_call(
        body, name=name, out_shape=jax.ShapeDtypeStruct((8 * N_DEV, pp // 8), F32),
        in_specs=[vm] + [pl.BlockSpec(memory_space=pl.ANY)] * len(deps), out_specs=vm,
        scratch_shapes=[pltpu.SemaphoreType.DMA((7,)), pltpu.SemaphoreType.DMA((7,))],
        compiler_params=pltpu.CompilerParams(has_side_effects=True, vmem_limit_bytes=VMEM_LIMIT),
    )(blk, *deps)
    return out.reshape(N_DEV, pp)[:, :p]


def _sum_rows8(g, name):
    p = g.shape[1]

    def body(in_ref, out_ref):
        acc = in_ref[0:1, :]
        for s in range(1, N_DEV):
            acc = acc + in_ref[s:s + 1, :]
        out_ref[...] = acc

    return pl.pallas_call(body, name=name, out_shape=jax.ShapeDtypeStruct((1, p), F32),
                          compiler_params=_params())(g)


def _sel_row(mods_ref, is_ctx, k):
    return jnp.where(is_ctx, mods_ref[1, k:k + 1, :], mods_ref[0, k:k + 1, :])


def _modulate(z, mods, g, ks, kc, n_x, out_dtype, name):
    m, d = z.shape
    tm = _pick(m, (256, 128, 64, 32, 16, 8))

    def body(z_ref, mods_ref, g_ref, h_ref):
        is_ctx = pl.program_id(0) * tm >= n_x
        zz = z_ref[...]
        r = lax.rsqrt(jnp.mean(zz * zz, axis=-1, keepdims=True) + RMS_EPS)
        shift, scale = _sel_row(mods_ref, is_ctx, ks), _sel_row(mods_ref, is_ctx, kc)
        h_ref[...] = ((zz * r) * g_ref[...] * (1.0 + scale) + shift).astype(out_dtype)

    return pl.pallas_call(
        body, name=name, grid=(m // tm,),
        in_specs=[pl.BlockSpec((tm, d), lambda i: (i, 0)), _full(mods.shape), _full(g.shape)],
        out_specs=pl.BlockSpec((tm, d), lambda i: (i, 0)),
        out_shape=jax.ShapeDtypeStruct((m, d), out_dtype),
        compiler_params=_params(("parallel",)),
    )(z, mods, g)


def _gate_bwd_rows(dx, y, gate, coef):
    return (coef * gate * dx).astype(BF16), jnp.sum(coef * y * dx, axis=0, keepdims=True)


def _modulate_bwd(z, dh, dres, mods, g, kc, n_x, name, latent_only=False, then=None):
    m, d = z.shape
    tm = _pick(m, (256, 128, 64, 32, 16, 8))
    first_ctx = n_x // tm
    res_blocks = dres.shape[0] // tm
    out_blocks = (n_x if latent_only else m) // tm
    extra = [] if then is None else [then[0], then[1]]

    def body(z_ref, dh_ref, dres_ref, mods_ref, g_ref, *rest):
        i = pl.program_id(0)
        is_ctx = i * tm >= n_x
        dx_ref, acc_ref = rest[len(extra)], rest[len(extra) + 1]

        @pl.when((i == 0) | (i == first_ctx))
        def _():
            acc_ref[...] = jnp.zeros_like(acc_ref)
            if then is not None:
                rest[-1][...] = jnp.zeros_like(rest[-1])

        zz, dhh = z_ref[...], dh_ref[...]
        r = lax.rsqrt(jnp.mean(zz * zz, axis=-1, keepdims=True) + RMS_EPS)
        nz = zz * r
        gain = g_ref[...] * (1.0 + _sel_row(mods_ref, is_ctx, kc))
        dn = dhh * gain
        dz = r * (dn - nz * jnp.mean(dn * nz, axis=-1, keepdims=True))
        dx = jnp.where(i < res_blocks, dres_ref[...], 0.0) + dz

        @pl.when(i < out_blocks)
        def _():
            dx_ref[...] = dx

        acc_ref[0:1, :] += jnp.sum(dhh, axis=0, keepdims=True)
        acc_ref[1:2, :] += jnp.sum(dhh * nz, axis=0, keepdims=True)
        if then is not None:
            y_ref, tmods_ref, dy_ref, gate_acc_ref = rest[0], rest[1], rest[-2], rest[-1]
            dy, part = _gate_bwd_rows(dx, y_ref[...], _sel_row(tmods_ref, is_ctx, then[2]), then[3])
            dy_ref[...] = dy
            gate_acc_ref[0:1, :] += part

    row = pl.BlockSpec((tm, d), lambda i: (i, 0))
    acc_spec = pl.BlockSpec((None, 8, d), lambda i: ((i * tm >= n_x).astype(jnp.int32), 0, 0))
    out_specs = [pl.BlockSpec((tm, d), lambda i: (jnp.minimum(i, out_blocks - 1), 0)), acc_spec]
    out_shape = [jax.ShapeDtypeStruct((out_blocks * tm, d), F32), jax.ShapeDtypeStruct((2, 8, d), F32)]
    in_specs = [row, row, pl.BlockSpec((tm, d), lambda i: (jnp.minimum(i, res_blocks - 1), 0)),
                _full(mods.shape), _full(g.shape)]
    if then is not None:
        in_specs += [row, _full(then[1].shape)]
        out_specs += [row, acc_spec]
        out_shape += [jax.ShapeDtypeStruct((m, d), BF16), jax.ShapeDtypeStruct((2, 8, d), F32)]
    return pl.pallas_call(
        body, name=name, grid=(m // tm,), in_specs=in_specs, out_specs=out_specs, out_shape=out_shape,
        compiler_params=_params(("arbitrary",)),
    )(z, dh, dres, mods, g, *extra)


def _ffn_up(h, wi_t, layer, name):
    m, d = h.shape
    f = wi_t.shape[1] // 2
    tm = _pick(m, ROW_TILES)

    def body(h_ref, w_ref, jac_ref, act_ref):
        hh = h_ref[...]
        a = _dot(hh, w_ref[0:f, :], NT)
        u = _dot(hh, w_ref[f:2 * f, :], NT)
        sg = jax.nn.sigmoid(a)
        s = a * sg
        jac_ref[:, 0:f] = (u * (sg * (1.0 + a * (1.0 - sg)))).astype(BF16)
        jac_ref[:, f:2 * f] = s.astype(BF16)
        act_ref[...] = (s * u).astype(BF16)

    return pl.pallas_call(
        body, name=name, grid=(m // tm,),
        in_specs=[pl.BlockSpec((tm, d), lambda i: (i, 0)),
                  pl.BlockSpec((None, 2 * f, d), lambda i: (layer, 0, 0))],
        out_specs=[pl.BlockSpec((tm, 2 * f), lambda i: (i, 0)), pl.BlockSpec((tm, f), lambda i: (i, 0))],
        out_shape=[jax.ShapeDtypeStruct((m, 2 * f), BF16), jax.ShapeDtypeStruct((m, f), BF16)],
        compiler_params=_params(("parallel",)),
    )(h, wi_t)


def _mm_resid(a, b, layer, res, mods, km, coef, n_x, name, nxt=None):
    m, k = a.shape
    n = b.shape[2]
    tm = _pick(m, (512, 256, 128, 64, 32, 16, 8))
    tn = n if nxt is not None else _pick(n, (1024, 512, 256, 128))
    extra = [] if nxt is None else [nxt[0], nxt[1]]

    def body(a_ref, b_ref, res_ref, mods_ref, *rest):
        is_ctx = pl.program_id(1) * tm >= n_x
        y = _dot(a_ref[...], b_ref[...])
        new = res_ref[...] + coef * _sel_row(mods_ref, is_ctx, km) * y
        if nxt is None:
            out_ref, y_ref = rest
        else:
            nmods_ref, g_ref, out_ref, y_ref, h_ref = rest
            r = lax.rsqrt(jnp.mean(new * new, axis=-1, keepdims=True) + RMS_EPS)
            shift, scale = _sel_row(nmods_ref, is_ctx, nxt[2]), _sel_row(nmods_ref, is_ctx, nxt[3])
            h_ref[...] = ((new * r) * g_ref[...] * (1.0 + scale) + shift).astype(nxt[4])
        y_ref[...] = y
        out_ref[...] = new

    tile = pl.BlockSpec((tm, tn), lambda j, i: (i, j))
    outs = [jax.ShapeDtypeStruct((m, n), F32), jax.ShapeDtypeStruct((m, n), F32)]
    if nxt is not None:
        outs.append(jax.ShapeDtypeStruct((m, n), nxt[4]))
    return pl.pallas_call(
        body, name=name, grid=(n // tn, m // tm),
        in_specs=[pl.BlockSpec((tm, k), lambda j, i: (i, 0)),
                  pl.BlockSpec((None, k, tn), lambda j, i: (layer, 0, j)),
                  tile, pl.BlockSpec((2, 16, tn), lambda j, i: (0, 0, j))] + [_full(e.shape) for e in extra],
        out_specs=[tile] * len(outs), out_shape=outs,
        compiler_params=_params(("parallel", "parallel")),
    )(a, b, res, mods, *extra)


def _ffn_down_bwd(dy, wo, layer, au, name):
    m, d = dy.shape
    f = wo.shape[1]
    tm = _pick(m, ROW_TILES)

    def body(dy_ref, wo_ref, au_ref, dau_ref):
        dact = _dot(dy_ref[...], wo_ref[...], NT)
        dau_ref[:, 0:f] = (dact * au_ref[:, 0:f].astype(F32)).astype(BF16)
        dau_ref[:, f:2 * f] = (dact * au_ref[:, f:2 * f].astype(F32)).astype(BF16)

    wide = pl.BlockSpec((tm, 2 * f), lambda i: (i, 0))
    return pl.pallas_call(
        body, name=name, grid=(m // tm,),
        in_specs=[pl.BlockSpec((tm, d), lambda i: (i, 0)), pl.BlockSpec((None, f, d), lambda i: (layer, 0, 0)), wide],
        out_specs=wide, out_shape=jax.ShapeDtypeStruct((m, 2 * f), BF16),
        compiler_params=_params(("parallel",)),
    )(dy, wo, au)


def _mm(terms, dims, n, out_dtype, name, tm_pref=(512, 256, 128, 64, 32, 16, 8), tn_pref=(512, 256, 128), dep=None):
    m = terms[0][0].shape[0]
    tm = _pick(m, tm_pref)
    tn = _pick(n, tn_pref)
    nt = len(terms)
    deps = [] if dep is None else [dep]

    def body(*refs):
        out_ref = refs[-1]
        acc = None
        for t in range(nt):
            part = _dot(refs[2 * t][...].astype(BF16), refs[2 * t + 1][...].astype(BF16), dims)
            acc = part if acc is None else acc + part
        out_ref[...] = acc.astype(out_dtype)

    in_specs, args = [], []
    for a, b, layer, rb in terms:
        k = a.shape[1]
        in_specs.append(pl.BlockSpec((tm, k), lambda j, i: (i, 0)))
        if dims == NN:
            in_specs.append(pl.BlockSpec((None, k, tn), lambda j, i, layer=layer, rb=rb: (layer, rb, j)))
        else:
            nb = n // tn
            in_specs.append(pl.BlockSpec((None, tn, k), lambda j, i, layer=layer, rb=rb, nb=nb: (layer, rb * nb + j, 0)))
        args += [a, b]
    return pl.pallas_call(
        body, name=name, grid=(n // tn, m // tm), in_specs=in_specs + [pl.BlockSpec(memory_space=pl.ANY)] * len(deps),
        out_specs=pl.BlockSpec((tm, tn), lambda j, i: (i, j)),
        out_shape=jax.ShapeDtypeStruct((m, n), out_dtype),
        compiler_params=_params(("parallel", "parallel")),
    )(*args, *deps)


def _mm_tn(a, b, out_dtype, name, dep=None):
    t = a.shape[0]
    m, n = a.shape[1], b.shape[1]
    tm = _pick(m, (1408, 2432, 1024, 512, 256, 128))
    tn = _pick(n, (1024, 512, 256, 128))
    tk = _pick(t, TALL_TILES)
    deps = [] if dep is None else [dep]

    def body(a_ref, b_ref, *rest):
        out_ref, acc_ref = rest[-2:]
        kk = pl.program_id(2)

        @pl.when(kk == 0)
        def _():
            acc_ref[...] = jnp.zeros_like(acc_ref)

        acc_ref[...] += _dot(a_ref[...].astype(BF16), b_ref[...].astype(BF16), TN)

        @pl.when(kk == pl.num_programs(2) - 1)
        def _():
            out_ref[...] = acc_ref[...].astype(out_dtype)

    return pl.pallas_call(
        body, name=name, grid=(m // tm, n // tn, t // tk),
        in_specs=[pl.BlockSpec((tk, tm), lambda i, j, k: (k, i)), pl.BlockSpec((tk, tn), lambda i, j, k: (k, j))]
        + [pl.BlockSpec(memory_space=pl.ANY)] * len(deps),
        out_specs=pl.BlockSpec((tm, tn), lambda i, j, k: (i, j)),
        out_shape=jax.ShapeDtypeStruct((m, n), out_dtype),
        scratch_shapes=[pltpu.VMEM((tm, tn), F32)],
        compiler_params=_params(("parallel", "parallel", "arbitrary")),
    )(a, b, *deps)


def _stack_rows(a, b, name):
    ta, d = a.shape
    tm = _pick(int(np.gcd(ta, b.shape[0])), (256, 128, 64, 32, 16, 8))
    na, nb = ta // tm, b.shape[0] // tm

    def body(a_ref, b_ref, o_ref):
        o_ref[...] = jnp.where(pl.program_id(0) < na, a_ref[...], b_ref[...])

    return pl.pallas_call(
        body, name=name, grid=(na + nb,),
        in_specs=[pl.BlockSpec((tm, d), lambda i: (jnp.minimum(i, na - 1), 0)),
                  pl.BlockSpec((tm, d), lambda i: (jnp.maximum(i - na, 0), 0))],
        out_specs=pl.BlockSpec((tm, d), lambda i: (i, 0)),
        out_shape=jax.ShapeDtypeStruct((ta + b.shape[0], d), a.dtype),
        compiler_params=_params(("parallel",)),
    )(a, b)


def _assemble_dz(lat_parts, ctx_parts, both_parts, width, name):
    t = next(p.shape[0] for p in lat_parts if p is not None)
    l_ctx = next(p.shape[0] for p in ctx_parts if p is not None)
    tm = _pick(int(np.gcd(t, l_ctx)), (256, 128, 64, 32, 16, 8))
    nt, nl = t // tm, l_ctx // tm
    plan, args, in_specs, off = [], [], [], 0
    lat_spec = lambda w: pl.BlockSpec((tm, w), lambda i: (jnp.minimum(i, nt - 1), 0))
    ctx_spec = lambda w: pl.BlockSpec((tm, w), lambda i: (jnp.maximum(i - nt, 0), 0))
    all_spec = lambda w: pl.BlockSpec((tm, w), lambda i: (i, 0))
    for lat, ctx, both in zip(lat_parts, ctx_parts, both_parts):
        if both:
            w = both[0].shape[1]
            plan.append(("both", off, w, len(args), len(both)))
            args += both
            in_specs += [all_spec(w)] * len(both)
        else:
            w = (lat if lat is not None else ctx).shape[1]
            plan.append(("split", off, w, len(args), (lat is not None, ctx is not None)))
            for part, spec in ((lat, lat_spec), (ctx, ctx_spec)):
                if part is not None:
                    args.append(part)
                    in_specs.append(spec(w))
        off += w
    n_in = len(args)

    def body(*refs):
        out_ref = refs[n_in]
        is_ctx = pl.program_id(0) >= nt
        for kind, o, w, first, info in plan:
            if kind == "both":
                val = refs[first][...]
                for k in range(1, info):
                    val = val + refs[first + k][...]
            else:
                has_lat, has_ctx = info
                zero = jnp.zeros((tm, w), F32)
                lat = refs[first][...] if has_lat else zero
                ctx = refs[first + int(has_lat)][...] if has_ctx else zero
                val = jnp.where(is_ctx, ctx, lat)
            out_ref[:, o:o + w] = val.astype(BF16)
        if off < width:
            out_ref[:, off:width] = jnp.zeros((tm, width - off), BF16)

    return pl.pallas_call(
        body, name=name, grid=(nt + nl,), in_specs=in_specs,
        out_specs=pl.BlockSpec((tm, width), lambda i: (i, 0)),
        out_shape=jax.ShapeDtypeStruct((t + l_ctx, width), BF16),
        compiler_params=_params(("parallel",)),
    )(*args)


def _final_loss(x, g, target, then, name):
    t, d = x.shape
    tm = _pick(t, (256, 128, 64, 32, 16, 8))
    y, tmods, km, coef = then

    def body(x_ref, g_ref, t_ref, y_ref, tmods_ref, dx_ref, loss_ref, dg_ref, dy_ref, gate_acc_ref):
        @pl.when(pl.program_id(0) == 0)
        def _():
            loss_ref[...] = jnp.zeros_like(loss_ref)
            dg_ref[...] = jnp.zeros_like(dg_ref)
            gate_acc_ref[...] = jnp.zeros_like(gate_acc_ref)

        xx, gg = x_ref[...], g_ref[...]
        r = lax.rsqrt(jnp.mean(xx * xx, axis=-1, keepdims=True) + RMS_EPS)
        nz = xx * r
        err = nz * gg - t_ref[...]
        loss_ref[...] += jnp.sum(err * err, axis=0, keepdims=True) * (0.5 / d)
        dout = err * (1.0 / d)
        dg_ref[...] += jnp.sum(dout * nz, axis=0, keepdims=True)
        dn = dout * gg
        dx = r * (dn - nz * jnp.mean(dn * nz, axis=-1, keepdims=True))
        dx_ref[...] = dx
        dy, part = _gate_bwd_rows(dx, y_ref[...], tmods_ref[0, km:km + 1, :], coef)
        dy_ref[...] = dy
        gate_acc_ref[0:1, :] += part

    row = pl.BlockSpec((tm, d), lambda i: (i, 0))
    vec = pl.BlockSpec((1, d), lambda i: (0, 0))
    acc = pl.BlockSpec((None, 8, d), lambda i: (0, 0, 0))
    return pl.pallas_call(
        body, name=name, grid=(t // tm,), in_specs=[row, vec, row, row, _full(tmods.shape)],
        out_specs=[row, vec, vec, row, acc],
        out_shape=[jax.ShapeDtypeStruct((t, d), F32), jax.ShapeDtypeStruct((1, d), F32),
                   jax.ShapeDtypeStruct((1, d), F32), jax.ShapeDtypeStruct((t, d), BF16),
                   jax.ShapeDtypeStruct((2, 8, d), F32)],
        compiler_params=_params(("arbitrary",)),
    )(x, g, target, y, tmods)


def _adaln_fwd(craw, w_mod, b_cols, name):
    lyr, d, nc = w_mod.shape

    def body(c_ref, w_ref, b_ref, out_ref):
        out_ref[...] = _bdot(_silu(c_ref[...]), w_ref[...]) + b_ref[...]

    return pl.pallas_call(
        body, name=name, grid=(lyr,),
        in_specs=[_full(craw.shape), pl.BlockSpec((None, d, nc), lambda l: (l, 0, 0)),
                  pl.BlockSpec((None, 1, nc), lambda l: (l, 0, 0))],
        out_specs=pl.BlockSpec((None, 16, nc), lambda l: (l, 0, 0)),
        out_shape=jax.ShapeDtypeStruct((lyr, 16, nc), F32),
        compiler_params=_params(("parallel",)),
    )(craw, w_mod, b_cols)


def _adaln_bwd(craw, cs_t, dmm_cols, w_mod, name):
    lyr, d, nc = w_mod.shape

    def body(c_ref, cst_ref, dmm_ref, w_ref, gw_ref, dc_ref):
        dmm = dmm_ref[...]
        gw_ref[...] = _bdot(cst_ref[...], dmm)
        cc = c_ref[...]
        sg = jax.nn.sigmoid(cc)
        dc_ref[...] = _bdot(dmm, w_ref[...], NT) * (sg * (1.0 + cc * (1.0 - sg)))

    wspec = pl.BlockSpec((None, d, nc), lambda l: (l, 0, 0))
    return pl.pallas_call(
        body, name=name, grid=(lyr,),
        in_specs=[_full(craw.shape), _full(cs_t.shape), pl.BlockSpec((None, 16, nc), lambda l: (l, 0, 0)), wspec],
        out_specs=[wspec, pl.BlockSpec((None, 16, d), lambda l: (l, 0, 0))],
        out_shape=[jax.ShapeDtypeStruct((lyr, d, nc), F32), jax.ShapeDtypeStruct((lyr, 16, d), F32)],
        compiler_params=_params(("parallel",)),
    )(craw, cs_t, dmm_cols, w_mod)


def _rope_tables(t):
    rows = np.repeat(np.arange(t // GRID_W, dtype=np.float32), GRID_W)
    cols = np.tile(np.arange(GRID_W, dtype=np.float32), t // GRID_W)
    n = A_HEAD_DIM // 4
    freqs = (ROPE_BASE ** (-np.arange(n, dtype=np.float32) / n)).astype(np.float32)
    ang_r, ang_c = (rows[:, None] * freqs).astype(np.float32), (cols[:, None] * freqs).astype(np.float32)
    cr, sr, cc, sc = np.cos(ang_r), np.sin(ang_r), np.cos(ang_c), np.sin(ang_c)
    cos = np.concatenate([cr, cr, cc, cc] * 2, axis=-1).astype(np.float32)
    sin = np.concatenate([-sr, sr, -sc, sc] * 2, axis=-1).astype(np.float32)
    return jnp.asarray(cos), jnp.asarray(sin)


def _rope(xt, cos, sin, adjoint, name):
    t, w = xt.shape
    tb = _pick(t, (512, 256, 128))
    rep = w // cos.shape[1]

    def body(x_ref, c_ref, s_ref, o_ref):
        xx = x_ref[...]
        cc = jnp.concatenate([c_ref[...]] * rep, axis=1) if rep > 1 else c_ref[...]
        ss = jnp.concatenate([s_ref[...]] * rep, axis=1) if rep > 1 else s_ref[...]
        low = (lax.broadcasted_iota(jnp.int32, xx.shape, 1) % 32) < 16

        def partner(v):
            return jnp.where(low, pltpu.roll(v, w - 16, 1), pltpu.roll(v, 16, 1))

        if adjoint:
            o_ref[...] = xx * cc + partner(xx * ss)
        else:
            o_ref[...] = xx * cc + partner(xx) * ss

    blk = pl.BlockSpec((tb, w), lambda i: (i, 0))
    tab = pl.BlockSpec((tb, cos.shape[1]), lambda i: (i, 0))
    return pl.pallas_call(
        body, name=name, grid=(t // tb,), in_specs=[blk, tab, tab], out_specs=blk,
        out_shape=jax.ShapeDtypeStruct((t, w), F32), compiler_params=_params(("parallel",)),
    )(xt, cos, sin)


def _attn_probs(q, kb, kc, sink, n, t):
    scale = A_HEAD_DIM ** -0.5
    s1 = _bdot(q, kb, NT) * scale
    s2 = _bdot(q, kc, NT) * scale
    qpos = n * WINDOW + lax.broadcasted_iota(jnp.int32, s1.shape, 0) % WINDOW
    kpos = (n - 1) * WINDOW + lax.broadcasted_iota(jnp.int32, s1.shape, 1)
    valid = (kpos >= 0) & (kpos < t) & (jnp.abs(kpos - qpos) <= WINDOW)
    s1 = jnp.where(valid, s1, -jnp.inf)
    mx = jnp.maximum(jnp.maximum(jnp.max(s1, axis=-1, keepdims=True), jnp.max(s2, axis=-1, keepdims=True)), sink)
    p1, p2, ps = jnp.exp(s1 - mx), jnp.exp(s2 - mx), jnp.exp(sink - mx)
    inv = 1.0 / (jnp.sum(p1, axis=-1, keepdims=True) + jnp.sum(p2, axis=-1, keepdims=True) + ps)
    return p1 * inv, p2 * inv, ps * inv


def _sink_rows(sink_ref):
    return jnp.concatenate([jnp.broadcast_to(sink_ref[r], (WINDOW, 1)) for r in range(A_REP)], axis=0)


def _attn_fwd(q, kp, vp, kc, vc, sink, name):
    hq, t, dh = q.shape
    nb = t // WINDOW
    lc = kc.shape[1]
    rows = A_REP * WINDOW

    def body(q_ref, k_ref, v_ref, kc_ref, vc_ref, sink_ref, o_ref):
        n = pl.program_id(1)
        start = pl.multiple_of(n * WINDOW, WINDOW)
        kb, vb = k_ref[pl.ds(start, 3 * WINDOW), :], v_ref[pl.ds(start, 3 * WINDOW), :]
        p1, p2, _ = _attn_probs(q_ref[...].reshape(rows, dh), kb, kc_ref[...], _sink_rows(sink_ref), n, t)
        o_ref[...] = (_bdot(p1, vb) + _bdot(p2, vc_ref[...])).reshape(A_REP, WINDOW, dh)

    qblk = pl.BlockSpec((A_REP, WINDOW, dh), lambda g, n: (g, n, 0))
    kfull = pl.BlockSpec((None, t + 2 * WINDOW, dh), lambda g, n: (g, 0, 0))
    cfull = pl.BlockSpec((None, lc, dh), lambda g, n: (g, 0, 0))
    return pl.pallas_call(
        body, name=name, grid=(hq // A_REP, nb),
        in_specs=[qblk, kfull, kfull, cfull, cfull, pl.BlockSpec((A_REP, 1, 1), lambda g, n: (g, 0, 0))],
        out_specs=qblk, out_shape=jax.ShapeDtypeStruct((hq, t, dh), F32),
        compiler_params=_params(("parallel", "parallel")),
    )(q, kp, vp, kc, vc, sink)


def _attn_bwd(q, kp, vp, kc, vc, sink, o, do, name):
    hq, t, dh = q.shape
    nb = t // WINDOW
    lc = kc.shape[1]
    scale = A_HEAD_DIM ** -0.5
    rows = A_REP * WINDOW

    def body(q_ref, k_ref, v_ref, kc_ref, vc_ref, sink_ref, o_ref, do_ref,
             dq_ref, dk_ref, dv_ref, dkc_ref, dvc_ref, dsink_ref):
        n = pl.program_id(1)

        @pl.when(n == 0)
        def _():
            dk_ref[...] = jnp.zeros_like(dk_ref)
            dv_ref[...] = jnp.zeros_like(dv_ref)
            dkc_ref[...] = jnp.zeros_like(dkc_ref)
            dvc_ref[...] = jnp.zeros_like(dvc_ref)
            dsink_ref[...] = jnp.zeros_like(dsink_ref)

        start = pl.multiple_of(n * WINDOW, WINDOW)
        band = pl.ds(start, 3 * WINDOW)
        qq, kb, vb, kcc, vcc = q_ref[...].reshape(rows, dh), k_ref[band, :], v_ref[band, :], kc_ref[...], vc_ref[...]
        p1, p2, ps = _attn_probs(qq, kb, kcc, _sink_rows(sink_ref), n, t)
        dout = do_ref[...].reshape(rows, dh)
        delta = jnp.sum(dout * o_ref[...].reshape(rows, dh), axis=-1, keepdims=True)
        ds1 = p1 * (_bdot(dout, vb, NT) - delta)
        ds2 = p2 * (_bdot(dout, vcc, NT) - delta)
        dq_ref[...] = ((_bdot(ds1, kb) + _bdot(ds2, kcc)) * scale).reshape(A_REP, WINDOW, dh)
        dk_ref[band, :] += _bdot(ds1.T, qq) * scale
        dv_ref[band, :] += _bdot(p1.T, dout)
        dkc_ref[...] += _bdot(ds2.T, qq) * scale
        dvc_ref[...] += _bdot(p2.T, dout)
        dsink_ref[...] += jnp.sum((-ps * delta).reshape(A_REP, WINDOW, 1), axis=1, keepdims=True)

    qblk = pl.BlockSpec((A_REP, WINDOW, dh), lambda g, n: (g, n, 0))
    kfull = pl.BlockSpec((None, t + 2 * WINDOW, dh), lambda g, n: (g, 0, 0))
    cfull = pl.BlockSpec((None, lc, dh), lambda g, n: (g, 0, 0))
    return pl.pallas_call(
        body, name=name, grid=(hq // A_REP, nb),
        in_specs=[qblk, kfull, kfull, cfull, cfull, pl.BlockSpec((A_REP, 1, 1), lambda g, n: (g, 0, 0)), qblk, qblk],
        out_specs=[qblk, kfull, kfull, cfull, cfull, pl.BlockSpec((A_REP, 8, 128), lambda g, n: (g, 0, 0))],
        out_shape=[jax.ShapeDtypeStruct(q.shape, F32), jax.ShapeDtypeStruct(kp.shape, F32),
                   jax.ShapeDtypeStruct(kp.shape, F32), jax.ShapeDtypeStruct(kc.shape, F32),
                   jax.ShapeDtypeStruct(kc.shape, F32), jax.ShapeDtypeStruct((hq, 8, 128), F32)],
        compiler_params=_params(("parallel", "arbitrary")),
    )(q, kp, vp, kc, vc, sink, o, do)


def _gate_fwd(zg, w2, b2, name):
    m = zg.shape[0]
    n = w2.shape[1]
    tm = _pick(m, (512, 256, 128, 64, 32, 16, 8))

    def body(z_ref, w_ref, b_ref, o_ref):
        o_ref[...] = jax.nn.log_sigmoid(_bdot(z_ref[...], w_ref[...]) + b_ref[...]) / B_GATE_NORM

    return pl.pallas_call(
        body, name=name, grid=(m // tm,),
        in_specs=[pl.BlockSpec((tm, zg.shape[1]), lambda i: (i, 0)), _full(w2.shape), _full(b2.shape)],
        out_specs=pl.BlockSpec((tm, n), lambda i: (i, 0)), out_shape=jax.ShapeDtypeStruct((m, n), F32),
        compiler_params=_params(("parallel",)),
    )(zg, w2, b2)


def _gate_bwd(zg, w2, b2, dla, name):
    m, rk = zg.shape
    n = w2.shape[1]
    tm = _pick(m, (512, 256, 128, 64, 32, 16, 8))

    def body(z_ref, w_ref, b_ref, d_ref, dz_ref, dw_ref, db_ref):
        @pl.when(pl.program_id(0) == 0)
        def _():
            dw_ref[...] = jnp.zeros_like(dw_ref)
            db_ref[...] = jnp.zeros_like(db_ref)

        zz, ww = z_ref[...], w_ref[...]
        pre = _bdot(zz, ww) + b_ref[...]
        dpre = d_ref[...] * (1.0 / B_GATE_NORM) * jax.nn.sigmoid(-pre)
        dz_ref[...] = _bdot(dpre, ww, NT)
        dw_ref[...] += _bdot(zz.T, dpre)
        db_ref[...] += jnp.sum(dpre, axis=0, keepdims=True)

    return pl.pallas_call(
        body, name=name, grid=(m // tm,),
        in_specs=[pl.BlockSpec((tm, rk), lambda i: (i, 0)), _full(w2.shape), _full(b2.shape),
                  pl.BlockSpec((tm, n), lambda i: (i, 0))],
        out_specs=[pl.BlockSpec((tm, rk), lambda i: (i, 0)), _full(w2.shape), _full(b2.shape)],
        out_shape=[jax.ShapeDtypeStruct((m, rk), F32), jax.ShapeDtypeStruct(w2.shape, F32),
                   jax.ShapeDtypeStruct(b2.shape, F32)],
        compiler_params=_params(("arbitrary",)),
    )(zg, w2, b2, dla)


def _chunk_order(step, n_x_chunks, n_chunks, reverse):
    n_c = n_chunks - n_x_chunks
    if reverse:
        return jnp.where(step < n_c, n_chunks - 1 - step, n_chunks - 1 - step)
    return jnp.where(step < n_c, n_x_chunks + step, step - n_c)


def _tri(reverse, transpose=False):
    i = lax.broadcasted_iota(jnp.int32, (B_CHUNK, B_CHUNK), 0)
    j = lax.broadcasted_iota(jnp.int32, (B_CHUNK, B_CHUNK), 1)
    if transpose:
        i, j = j, i
    return (j >= i) if reverse else (j <= i)


def _gla_chunk(q, k, la, reverse):
    g = _dot_01(_tri(reverse), la)
    last = 0 if reverse else B_CHUNK - 1
    gl = g[last:last + 1, :]
    eg, eng, egl = jnp.exp(g), jnp.exp(-g), jnp.exp(gl - g)
    decay_col = jnp.exp(jnp.sum(la.T, axis=1, keepdims=True))
    return q * (B_DK ** -0.5) * eg, k * eng, k * egl, eg, eng, egl, decay_col


def _head_of(shape, axis, width):
    return lax.broadcasted_iota(jnp.int32, shape, axis) // width


def _gla_fwd(q, k, v, la_f, la_b, n_x, name):
    tc, wk = q.shape
    wv = v.shape[1]
    hh = B_HEADS
    dk, dv = wk // hh, wv // hh
    nc, nxc = tc // B_CHUNK, n_x // B_CHUNK
    orders = [functools.partial(_chunk_order, n_x_chunks=nxc, n_chunks=nc, reverse=rev) for rev in (False, True)]

    def body(*refs):
        ins, outs, s_refs = refs[:8], refs[8:12], refs[12:]

        @pl.when(pl.program_id(0) == 0)
        def _():
            for s_ref in s_refs:
                s_ref[...] = jnp.zeros_like(s_ref)

        lane_head = _head_of((B_CHUNK, wk), 1, dk)
        row_head = _head_of((wk, dv), 0, dk)
        for di, reverse in enumerate((False, True)):
            q_ref, k_ref, v_ref, la_ref = ins[4 * di:4 * di + 4]
            o_ref, s_save_ref = outs[2 * di:2 * di + 2]
            s_ref = s_refs[di]
            qt, kt, ke, _, _, _, decay_col = _gla_chunk(q_ref[...], k_ref[...], la_ref[...], reverse)
            ke_t = ke.T
            s_prev = s_ref[...]
            update = jnp.zeros_like(s_prev)
            for h in range(hh):
                vv = v_ref[:, h * dv:(h + 1) * dv]
                qm = jnp.where(lane_head == h, qt, 0.0)
                att = jnp.where(_tri(reverse), _bdot(qm, kt, NT), 0.0)
                o_ref[:, h * dv:(h + 1) * dv] = _bdot(att, vv) + _bdot(qm, s_prev)
                update = jnp.where(row_head == h, _bdot(ke_t, vv), update)
            s_save_ref[...] = s_prev
            s_ref[...] = decay_col * s_prev + update

    def blk(w, order):
        return pl.BlockSpec((B_CHUNK, w), lambda s: (order(s), 0))

    def sblk(order):
        return pl.BlockSpec((None, wk, dv), lambda s: (order(s), 0, 0))

    in_specs, out_specs = [], []
    for order in orders:
        in_specs += [blk(wk, order), blk(wk, order), blk(wv, order), blk(wk, order)]
        out_specs += [blk(wv, order), sblk(order)]
    o_shape, s_shape = jax.ShapeDtypeStruct((tc, wv), F32), jax.ShapeDtypeStruct((nc, wk, dv), F32)
    return pl.pallas_call(
        body, name=name, grid=(nc,), in_specs=in_specs, out_specs=out_specs,
        out_shape=[o_shape, s_shape, o_shape, s_shape],
        scratch_shapes=[pltpu.VMEM((wk, dv), F32)] * 2,
        compiler_params=_params(("arbitrary",)),
    )(q, k, v, la_f, q, k, v, la_b)


def _gla_bwd(q, k, v, la_f, la_b, s_f, s_b, do, n_x, name):
    tc, wk = q.shape
    wv = v.shape[1]
    hh = B_HEADS
    dk, dv = wk // hh, wv // hh
    nc, nxc = tc // B_CHUNK, n_x // B_CHUNK
    orders = [functools.partial(lambda s, rev: _chunk_order(nc - 1 - s, nxc, nc, rev), rev=rev) for rev in (False, True)]

    def body(*refs):
        ins, outs, ds_refs = refs[:12], refs[12:20], refs[20:]

        @pl.when(pl.program_id(0) == 0)
        def _():
            for ds_ref in ds_refs:
                ds_ref[...] = jnp.zeros_like(ds_ref)

        lane_head = _head_of((B_CHUNK, wk), 1, dk)
        row_head = _head_of((wk, dv), 0, dk)
        for di, reverse in enumerate((False, True)):
            q_ref, k_ref, v_ref, la_ref, s_save_ref, do_ref = ins[6 * di:6 * di + 6]
            dq_ref, dk_ref, dv_ref, dla_ref = outs[4 * di:4 * di + 4]
            ds_ref = ds_refs[di]
            mask = _tri(reverse)
            last = 0 if reverse else B_CHUNK - 1
            is_last = lax.broadcasted_iota(jnp.int32, (B_CHUNK, wk), 0) == last
            la = la_ref[...]
            qt, kt, ke, eg, eng, egl, decay_col = _gla_chunk(q_ref[...], k_ref[...], la, reverse)
            qt_t = qt.T
            s_prev, ds_new = s_save_ref[...], ds_ref[...]
            dqt, dkt, dke = jnp.zeros_like(qt), jnp.zeros_like(qt), jnp.zeros_like(qt)
            ds_add = jnp.zeros_like(ds_new)
            for h in range(hh):
                cols = slice(h * dv, (h + 1) * dv)
                vv, dout = v_ref[:, cols], do_ref[:, cols]
                mine = lane_head == h
                qm, km = jnp.where(mine, qt, 0.0), jnp.where(mine, ke, 0.0)
                att = jnp.where(mask, _bdot(qm, kt, NT), 0.0)
                datt = jnp.where(mask, _bdot(dout, vv, NT), 0.0)
                dv_ref[:, cols] = _bdot(att.T, dout) + _bdot(km, ds_new)
                dqt = jnp.where(mine, _bdot(datt, kt) + _bdot(dout, s_prev, NT), dqt)
                dkt = jnp.where(mine, _bdot(datt.T, qt), dkt)
                dke = jnp.where(mine, _bdot(vv, ds_new, NT), dke)
                ds_add = jnp.where(row_head == h, _bdot(qt_t, dout), ds_add)
            ddecay_row = jnp.sum((ds_new * s_prev).T, axis=0, keepdims=True)
            decay_row = jnp.exp(jnp.sum(la, axis=0, keepdims=True))
            ds_ref[...] = decay_col * ds_new + ds_add
            dq_ref[...] = dqt * (B_DK ** -0.5) * eg
            dk_ref[...] = dkt * eng + dke * egl
            dgl = jnp.sum(dke * ke, axis=0, keepdims=True) + ddecay_row * decay_row
            dg = dqt * qt - dkt * kt - dke * ke + jnp.where(is_last, dgl, 0.0)
            dla_ref[...] = _dot_01(_tri(reverse, transpose=True), dg)

    def blk(w, order):
        return pl.BlockSpec((B_CHUNK, w), lambda s: (order(s), 0))

    in_specs, out_specs = [], []
    for order in orders:
        in_specs += [blk(wk, order), blk(wk, order), blk(wv, order), blk(wk, order),
                     pl.BlockSpec((None, wk, dv), lambda s, order=order: (order(s), 0, 0)), blk(wv, order)]
        out_specs += [blk(wk, order), blk(wk, order), blk(wv, order), blk(wk, order)]
    k_shape, v_shape = jax.ShapeDtypeStruct((tc, wk), F32), jax.ShapeDtypeStruct((tc, wv), F32)
    return pl.pallas_call(
        body, name=name, grid=(nc,), in_specs=in_specs, out_specs=out_specs,
        out_shape=[k_shape, k_shape, v_shape, k_shape] * 2,
        scratch_shapes=[pltpu.VMEM((wk, dv), F32)] * 2,
        compiler_params=_params(("arbitrary",)),
    )(q, k, v, la_f, s_f, do, q, k, v, la_b, s_b, do)


def _gla_out_fwd(o_f, o_b, r, g, name):
    t = r.shape[0]
    dv = g.shape[1]
    hh = r.shape[1] // dv
    tb = _pick(t, (256, 128, 64))

    def body(of_ref, ob_ref, r_ref, g_ref, out_ref):
        for h in range(hh):
            cols = slice(h * dv, (h + 1) * dv)
            o = of_ref[:, cols] + ob_ref[:, cols]
            rs = lax.rsqrt(jnp.mean(o * o, axis=-1, keepdims=True) + RMS_EPS)
            out_ref[:, cols] = (o * rs) * g_ref[...] * _silu(r_ref[:, cols])

    rblk = pl.BlockSpec((tb, hh * dv), lambda i: (i, 0))
    return pl.pallas_call(
        body, name=name, grid=(t // tb,), in_specs=[rblk, rblk, rblk, _full(g.shape)], out_specs=rblk,
        out_shape=jax.ShapeDtypeStruct((t, hh * dv), F32), compiler_params=_params(("parallel",)),
    )(o_f, o_b, r, g)


def _gla_out_bwd(o_f, o_b, r, g, dout, name):
    tc = o_f.shape[0]
    t = r.shape[0]
    dv = g.shape[1]
    hh = r.shape[1] // dv
    tb = _pick(int(np.gcd(t, tc)), (256, 128, 64))
    nt = t // tb

    def body(of_ref, ob_ref, r_ref, g_ref, d_ref, do_ref, dr_ref, dg_ref):
        i = pl.program_id(0)

        @pl.when(i == 0)
        def _():
            dg_ref[...] = jnp.zeros_like(dg_ref)

        @pl.when(i >= nt)
        def _():
            do_ref[...] = jnp.zeros_like(do_ref)

        @pl.when(i < nt)
        def _():
            gg = g_ref[...]
            for h in range(hh):
                cols = slice(h * dv, (h + 1) * dv)
                o = of_ref[:, cols] + ob_ref[:, cols]
                rs = lax.rsqrt(jnp.mean(o * o, axis=-1, keepdims=True) + RMS_EPS)
                nz = o * rs
                rr, dd = r_ref[:, cols], d_ref[:, cols]
                sg = jax.nn.sigmoid(rr)
                dr_ref[:, cols] = dd * nz * gg * (sg * (1.0 + rr * (1.0 - sg)))
                dy = dd * (rr * sg)
                dg_ref[...] += jnp.sum(dy * nz, axis=0, keepdims=True)
                dn = dy * gg
                do_ref[:, cols] = rs * (dn - nz * jnp.mean(dn * nz, axis=-1, keepdims=True))

    oblk = pl.BlockSpec((tb, hh * dv), lambda i: (i, 0))
    rblk = pl.BlockSpec((tb, hh * dv), lambda i: (jnp.minimum(i, nt - 1), 0))
    return pl.pallas_call(
        body, name=name, grid=(tc // tb,), in_specs=[oblk, oblk, rblk, _full(g.shape), rblk],
        out_specs=[oblk, rblk, _full(g.shape)],
        out_shape=[jax.ShapeDtypeStruct(o_f.shape, F32), jax.ShapeDtypeStruct(r.shape, F32),
                   jax.ShapeDtypeStruct(g.shape, F32)],
        compiler_params=_params(("arbitrary",)),
    )(o_f, o_b, r, g, dout)


def _pool_window(i, tb, t):
    return pl.multiple_of(jnp.clip(i * tb - POOL_PAD, 0, t - (tb + 2 * POOL_PAD)), 8)


def _pool_band(half, i, tb, start, adjoint):
    pos = i * tb + lax.broadcasted_iota(jnp.int32, (tb, tb + 2 * POOL_PAD), 0)
    tok = start + lax.broadcasted_iota(jnp.int32, (tb, tb + 2 * POOL_PAD), 1)
    if adjoint:
        return (tok > pos - half) & (tok <= pos + half)
    return (tok >= pos - half) & (tok < pos + half)


def _pool_count(pos, half, t):
    return (jnp.minimum(pos + half, t) - jnp.maximum(pos - half, 0)).astype(F32)


def _pool_fwd(h, w_pool, pool_scale, res, mods, km, name):
    t, d = res.shape
    ng, gw = w_pool.shape[0], w_pool.shape[1]
    tb = _pick(t, (256, 128, 64))

    def body(h_ref, w_ref, ps_ref, res_ref, mods_ref, out_ref, pooled_ref, ypre_ref):
        gi, i = pl.program_id(0), pl.program_id(1)
        half = jnp.left_shift(1, gi)
        start = _pool_window(i, tb, t)
        win = h_ref[pl.ds(start, tb + 2 * POOL_PAD), :]
        total = _dot_01(_pool_band(half, i, tb, start, False), win)
        pos = i * tb + lax.broadcasted_iota(jnp.int32, (tb, 1), 0)
        pooled = total / _pool_count(pos, half, t) - h_ref[pl.ds(pl.multiple_of(i * tb, tb), tb), :]
        ypre = _bdot(pooled, w_ref[...])
        pooled_ref[...] = pooled.astype(BF16)
        ypre_ref[...] = ypre
        out_ref[...] = res_ref[...] + mods_ref[0, km:km + 1, :] * (ypre * ps_ref[...])

    tile = pl.BlockSpec((tb, gw), lambda gi, i: (i, gi))
    return pl.pallas_call(
        body, name=name, grid=(ng, t // tb),
        in_specs=[pl.BlockSpec((t, gw), lambda gi, i: (0, gi)),
                  pl.BlockSpec((None, gw, gw), lambda gi, i: (gi, 0, 0)),
                  pl.BlockSpec((1, gw), lambda gi, i: (0, gi)), tile,
                  pl.BlockSpec((2, 16, gw), lambda gi, i: (0, 0, gi))],
        out_specs=[tile, tile, tile],
        out_shape=[jax.ShapeDtypeStruct((t, d), F32), jax.ShapeDtypeStruct((t, d), BF16),
                   jax.ShapeDtypeStruct((t, d), F32)],
        compiler_params=_params(("parallel", "parallel")),
    )(h, w_pool, pool_scale, res, mods)


def _pool_bwd(dxp, w_pool, pool_scale, pooled, ypre, mods, km, name):
    t, d = pooled.shape
    ng, gw = w_pool.shape[0], w_pool.shape[1]
    tb = _pick(t, (256, 128, 64))

    def body(dxp_ref, w_ref, ps_ref, pooled_ref, ypre_ref, mods_ref, dh_ref, dw_ref, acc_ref):
        gi, i = pl.program_id(0), pl.program_id(1)

        @pl.when(i == 0)
        def _():
            dw_ref[...] = jnp.zeros_like(dw_ref)
            acc_ref[...] = jnp.zeros_like(acc_ref)

        half = jnp.left_shift(1, gi)
        mod, ps = mods_ref[0, km:km + 1, :], ps_ref[...]
        start = _pool_window(i, tb, t)
        dwin = dxp_ref[pl.ds(start, tb + 2 * POOL_PAD), :]
        dpooled = _bdot(dwin * (mod * ps), w_ref[...], NT)
        pos = start + lax.broadcasted_iota(jnp.int32, (tb + 2 * POOL_PAD, 1), 0)
        spread = _dot_01(_pool_band(half, i, tb, start, True), dpooled / _pool_count(pos, half, t))
        dxc, yp = dxp_ref[pl.ds(pl.multiple_of(i * tb, tb), tb), :], ypre_ref[...]
        dh_ref[...] = spread - _bdot(dxc * (mod * ps), w_ref[...], NT)
        dw_ref[...] += _bdot(pooled_ref[...].astype(F32).T, dxc * (mod * ps))
        acc_ref[0:1, :] += jnp.sum(dxc * yp * mod, axis=0, keepdims=True)
        acc_ref[1:2, :] += jnp.sum(dxc * yp * ps, axis=0, keepdims=True)

    tile = pl.BlockSpec((tb, gw), lambda gi, i: (i, gi))
    wblk = pl.BlockSpec((None, gw, gw), lambda gi, i: (gi, 0, 0))
    return pl.pallas_call(
        body, name=name, grid=(ng, t // tb),
        in_specs=[pl.BlockSpec((t, gw), lambda gi, i: (0, gi)), wblk,
                  pl.BlockSpec((1, gw), lambda gi, i: (0, gi)), tile, tile,
                  pl.BlockSpec((2, 16, gw), lambda gi, i: (0, 0, gi))],
        out_specs=[tile, wblk, pl.BlockSpec((8, gw), lambda gi, i: (0, gi))],
        out_shape=[jax.ShapeDtypeStruct((t, d), F32), jax.ShapeDtypeStruct(w_pool.shape, F32),
                   jax.ShapeDtypeStruct((8, d), F32)],
        compiler_params=_params(("arbitrary", "arbitrary")),
    )(dxp, w_pool, pool_scale, pooled, ypre, mods)


def _adamw(w, g, m, v, name):
    r, c = w.shape
    tr = _pick(r, (512, 352, 256, 128, 64, 32, 16, 8))
    c1 = 1.0 / (1.0 - ADAM_B1 ** ADAM_STEP)
    c2 = 1.0 / (1.0 - ADAM_B2 ** ADAM_STEP)

    def body(w_ref, g_ref, m_ref, v_ref, d_ref, nm_ref, nv_ref):
        gg = g_ref[...]
        nm = ADAM_B1 * m_ref[...] + (1.0 - ADAM_B1) * gg
        nv = ADAM_B2 * v_ref[...] + (1.0 - ADAM_B2) * (gg * gg)
        nm_ref[...] = nm
        nv_ref[...] = nv
        d_ref[...] = -ADAM_LR * ((nm * c1) / (jnp.sqrt(nv * c2) + ADAM_EPS) + ADAM_WD * w_ref[...])

    blk = pl.BlockSpec((tr, c), lambda i: (i, 0))
    shp = jax.ShapeDtypeStruct((r, c), F32)
    return pl.pallas_call(
        body, name=name, grid=(r // tr,), in_specs=[blk] * 4, out_specs=[blk] * 3, out_shape=[shp] * 3,
        compiler_params=_params(("parallel",)),
    )(w, g, m, v)


def _heads(z, n_heads):
    m = z.shape[0]
    return z.reshape(m, n_heads, -1).transpose(1, 0, 2)


def _unheads(zh):
    return zh.transpose(1, 0, 2).reshape(zh.shape[1], -1)


def _pad_rows(a, n):
    return jnp.pad(a, ((0, 0), (n, n), (0, 0))) if a.ndim == 3 else jnp.pad(a, ((n, n), (0, 0)))


def _local_step(x, ctx, target, mods, wts, fetch, emit):
    t, d = x.shape
    l_ctx = ctx.shape[0]
    tc = t + l_ctx
    norm_g = wts["norm_g"]
    ng = lambda l, k: norm_g[l, k][None, :]
    grads = {}
    dmods = [[[None] * N_MOD for _ in range(2)] for _ in range(2)]
    dnorm = [[None] * 3 for _ in range(2)]

    def ffn_fwd(z, h, l, kbase, wi, wo, n_x, tag, nxt):
        au, act = _ffn_up(h, wi, 0, f"ffn_up_{tag}")
        wo = wo(act) if callable(wo) else wo
        outs = _mm_resid(act, wo, 0, z, mods[l], kbase + 2, 0.5, n_x, f"ffn_down_{tag}", nxt=nxt)
        return outs[0], (z, h, au, act, outs[1], wi, wo), (outs[2] if nxt is not None else None)

    def ffn_bwd(dz_new, dy, saved, l, kbase, g, n_x, tag, stage, split=False, then=None):
        z, h, au, act, y, wi, wo = saved
        dau = _ffn_down_bwd(dy, wo, 0, au, f"ffn_down_bwd_{tag}")
        dwo = _mm_tn(act, dy, BF16, f"dwo_{tag}")
        if split:
            token = emit(stage, [dwo])
            dwi_t = _mm_tn(dau, h, BF16, f"dwi_{tag}", dep=token)
            token = emit(stage + 1, [dwi_t])
        else:
            dwi_t = _mm_tn(dau, h, BF16, f"dwi_{tag}")
            token = emit(stage, [dwi_t, dwo])
        dh = _mm([(dau, wi, 0, 0)], NN, d, F32, f"dh_{tag}", tm_pref=TALL_TILES, dep=token)
        return _modulate_bwd(z, dh, dz_new, mods[l], g, kbase + 1, n_x, f"mod_bwd_{tag}", latent_only=split, then=then)

    def record(l, kbase, k_norm, g, acc_mod, acc_gate, streams):
        total = None
        for s in range(streams):
            dmods[l][s][kbase] = acc_mod[s, 0]
            dmods[l][s][kbase + 1] = acc_mod[s, 1] * g[0]
            if acc_gate is not None:
                dmods[l][s][kbase + 2] = acc_gate[s, 0]
            part = acc_mod[s, 1] * (1.0 + mods[l][s, kbase + 1])
            total = part if total is None else total + part
        dnorm[l][k_norm] = total

    xc0 = _stack_rows(x, ctx, "stack_tokens")
    wi1_0 = fetch(0, None)["wi1_0"]
    h0 = _modulate(xc0, mods[0], ng(0, 0), 0, 1, t, BF16, "mod_l0f1")
    xc1, sv_f1, hc = ffn_fwd(xc0, h0, 0, 0, wi1_0, lambda act: fetch(1, act)["wo1_0"], t, "l0f1",
                             (mods[0], ng(0, 1), 3, 4, BF16))
    w_in_t = fetch(2, hc)["w_in_t"]
    n_proj = w_in_t.shape[1]
    zall = _mm([(hc, w_in_t, 0, 0)], NT, n_proj, F32, "proj", tm_pref=TALL_TILES,
               tn_pref=(n_proj,))
    offs = np.cumsum((0,) + PROJ_SIZES)
    part = lambda i, rows=slice(None): zall[rows, offs[i]:offs[i + 1]]
    lat, con = slice(0, t), slice(t, tc)
    cos, sin = _rope_tables(t)
    qa = _heads(_rope(part(0, lat), cos, sin, False, "rope_q"), A_HEADS)
    ka = _heads(_rope(part(1, lat), cos, sin, False, "rope_k"), A_KV_HEADS)
    va = _heads(part(2, lat), A_KV_HEADS)
    kca, vca = _heads(part(1, con), A_KV_HEADS), _heads(part(2, con), A_KV_HEADS)
    kap, vap = _pad_rows(ka, WINDOW), _pad_rows(va, WINDOW)
    sink = wts["sink"].reshape(A_HEADS, 1, 1)
    o_a = _attn_fwd(qa, kap, vap, kca, vca, sink, "attn_fwd")

    qb, kb, vb = part(3), part(4), part(5)
    rb = part(6, lat)
    zg = part(7)
    zg_f, zg_b = zg[:, :B_GATE_RANK], zg[:, B_GATE_RANK:]
    w2f, w2b, b2f, b2b = wts["w_a2_f"], wts["w_a2_b"], wts["b_a_f"], wts["b_a_b"]
    la_f = _gate_fwd(zg_f, w2f, b2f, "gate_f")
    la_b = _gate_fwd(zg_b, w2b, b2b, "gate_b")
    o_f, s_f, o_b, s_b = _gla_fwd(qb, kb, vb, la_f, la_b, t, "gla_fwd")
    gla_g = wts["gla_g"]
    go = _gla_out_fwd(o_f, o_b, rb, gla_g, "gla_out")
    cat = jnp.concatenate([_unheads(o_a), go], axis=-1).astype(BF16)
    big = fetch(3, cat)
    w_out, wi2_0, wo2_0 = big["w_out"], big["wi2_0"], big["wo2_0"]
    x2, y_mix0, h2 = _mm_resid(cat, w_out, 0, xc1, mods[0], 5, 1.0, t, "w_out", nxt=(mods[0], ng(0, 2), 6, 7, BF16))
    x3, sv_f2, h3 = ffn_fwd(x2, h2, 0, 6, wi2_0, wo2_0, t, "l0f2", (mods[1], ng(1, 0), 0, 1, BF16))

    big = fetch(4, x3)
    wi1_1, wo1_1, wi2_1, wo2_1 = big["wi1_1"], big["wo1_1"], big["wi2_1"], big["wo2_1"]
    x4, sv_g1, hp = ffn_fwd(x3, h3, 1, 0, wi1_1, wo1_1, t, "l1f1", (mods[1], ng(1, 1), 3, 4, F32))
    w_pool, pool_scale = wts["w_pool"], wts["pool_scale"]
    x5, pooled, ypre = _pool_fwd(hp, w_pool, pool_scale, x4, mods[1], 5, "pool_fwd")
    h5 = _modulate(x5, mods[1], ng(1, 2), 6, 7, t, BF16, "mod_l1f2")
    x6, sv_g2, _ = ffn_fwd(x5, h5, 1, 6, wi2_1, wo2_1, t, "l1f2", None)

    y_of = lambda saved: saved[4]
    dx6, loss_vec, dfinal_g, dy, acc_gate = _final_loss(x6, wts["final_g"], target, (y_of(sv_g2), mods[1], 8, 0.5),
                                                        "final_loss")
    grads["final_g"] = dfinal_g[0]

    dx5, acc_mod = ffn_bwd(dx6, dy, sv_g2, 1, 6, ng(1, 2), t, "l1f2", 0)
    record(1, 6, 2, ng(1, 2), acc_mod, acc_gate, 1)
    dhp, dw_pool, acc_pool = _pool_bwd(dx5, w_pool, pool_scale, pooled, ypre, mods[1], 5, "pool_bwd")
    grads["pool_scale"] = acc_pool[0]
    dmods[1][0][5] = acc_pool[1]
    dx4, acc_mod, dy, acc_gate = _modulate_bwd(x4, dhp, dx5, mods[1], ng(1, 1), 4, t, "mod_bwd_l1mix",
                                               then=(y_of(sv_g1), mods[1], 2, 0.5))
    record(1, 3, 1, ng(1, 1), acc_mod, None, 1)
    dx3, acc_mod, dy, acc_gate_next = ffn_bwd(dx4, dy, sv_g1, 1, 0, ng(1, 0), t, "l1f1", 1,
                                              then=(y_of(sv_f2), mods[0], 8, 0.5))
    record(1, 0, 0, ng(1, 0), acc_mod, acc_gate, 1)

    dx2, acc_mod, dymix, acc_gate_mix = ffn_bwd(dx3, dy, sv_f2, 0, 6, ng(0, 2), t, "l0f2", 2,
                                                then=(y_mix0, mods[0], 5, 1.0))
    record(0, 6, 2, ng(0, 2), acc_mod, acc_gate_next, 1)
    dmods[0][0][5] = acc_gate_mix[0, 0]
    dw_out = _mm_tn(cat, dymix, BF16, "dw_out")
    dcat = _mm([(dymix, w_out, 0, 0)], NT, cat.shape[1], F32, "dcat")
    do_a = _heads(dcat[:, :A_Q], A_HEADS)
    do_full, drb, dgla_g = _gla_out_bwd(o_f, o_b, rb, gla_g, dcat[:, A_Q:], "gla_out_bwd")
    grads["gla_g"] = dgla_g[0]
    dq_f, dk_f, dv_f, dla_f, dq_b, dk_b, dv_b, dla_b = _gla_bwd(qb, kb, vb, la_f, la_b, s_f, s_b, do_full, t, "gla_bwd")
    dzg_f, dw2f, db2f = _gate_bwd(zg_f, w2f, b2f, dla_f, "gate_bwd_f")
    dzg_b, dw2b, db2b = _gate_bwd(zg_b, w2b, b2b, dla_b, "gate_bwd_b")
    grads.update(w_a2_f=dw2f, w_a2_b=dw2b, b_a_f=db2f[0], b_a_b=db2b[0])
    dqa_r, dkap, dvap, dkca, dvca, dsink = _attn_bwd(qa, kap, vap, kca, vca, sink, o_a, do_a, "attn_bwd")
    grads["sink"] = dsink[:, 0, 0]
    dqa = _rope(_unheads(dqa_r), cos, sin, True, "rope_bwd_q")
    dka = _rope(_unheads(dkap[:, WINDOW:WINDOW + t]), cos, sin, True, "rope_bwd_k")
    dva = dvap[:, WINDOW:WINDOW + t]
    dzg = jnp.concatenate([dzg_f, dzg_b, jnp.zeros((tc, n_proj - PROJ_DIM), F32)], axis=-1)
    dzall = _assemble_dz(
        [dqa, dka, _unheads(dva), None, None, None, drb, None],
        [None, _unheads(dkca), _unheads(dvca), None, None, None, None, None],
        [None, None, None, [dq_f, dq_b], [dk_f, dk_b], [dv_f, dv_b], None, [dzg]], n_proj, "assemble_dz")
    dw_in_t = _mm_tn(dzall, hc, BF16, "dw_in")
    token = emit(3, [dw_in_t, dw_out, dw_pool])
    dhc = _mm([(dzall, w_in_t, 0, 0)], NN, d, F32, "dhc", tm_pref=TALL_TILES, dep=token)
    dxc1, acc_mod, dy, acc_gate = _modulate_bwd(xc1, dhc, dx2, mods[0], ng(0, 1), 4, t, "mod_bwd_l0mix",
                                                then=(y_of(sv_f1), mods[0], 2, 0.5))
    record(0, 3, 1, ng(0, 1), acc_mod, None, 2)
    dxc0, acc_mod = ffn_bwd(dxc1, dy, sv_f1, 0, 0, ng(0, 0), t, "l0f1", 4, split=True)
    record(0, 0, 0, ng(0, 0), acc_mod, acc_gate, 2)

    grads["norm_g"] = jnp.stack([jnp.stack(dnorm[0]), jnp.stack(dnorm[1])])
    zero = jnp.zeros((d,), F32)
    dmods_arr = jnp.stack([jnp.stack([jnp.stack([v if v is not None else zero for v in dmods[l][s]])
                                      for s in range(2)]) for l in range(2)])
    return loss_vec, dxc0, grads, dmods_arr


def _pack(parts):
    flat = jnp.concatenate([p.reshape(-1).astype(F32) for p in parts])
    pad = (-flat.shape[0]) % 128
    return jnp.pad(flat, (0, pad))[None, :]


def _unpack(rows, shapes):
    out, off = [], 0
    for s in shapes:
        n = int(np.prod(s))
        out.append(rows[:, off:off + n].reshape((rows.shape[0],) + tuple(s)))
        off += n
    return out


def _cols_to_full(g):
    g = jnp.moveaxis(g, 0, -2)
    return g.reshape(g.shape[:-2] + (-1,))


def kernel(x, c, ctx, c_ctx, w_mod, b_mod, norm_g, ffn1_wi, ffn1_wo, ffn2_wi, ffn2_wo, w_in, w_a2_f, b_a_f, w_a2_b, b_a_b, sink, gla_g, w_out, w_pool, pool_scale, final_g, loss_target, m_c_ctx, m_w_mod, m_b_mod, m_norm_g, m_ffn1_wi, m_ffn1_wo, m_ffn2_wi, m_ffn2_wo, m_w_in, m_w_a2_f, m_b_a_f, m_w_a2_b, m_b_a_b, m_sink, m_gla_g, m_w_out, m_w_pool, m_pool_scale, m_final_g, v_c_ctx, v_w_mod, v_b_mod, v_norm_g, v_ffn1_wi, v_ffn1_wo, v_ffn2_wi, v_ffn2_wo, v_w_in, v_w_a2_f, v_b_a_f, v_w_a2_b, v_b_a_b, v_sink, v_gla_g, v_w_out, v_w_pool, v_pool_scale, v_final_g):
    t, d = x.shape[1], x.shape[2]
    me = _dev_index()
    nc = w_mod.shape[2]
    ncol_in = w_in.shape[2]
    ncol_pad = -(-ncol_in // 16) * 16

    small_shapes = [(d,), norm_g.shape, pool_scale.shape, w_a2_f.shape, w_a2_b.shape, w_pool.shape]
    g1 = _gather_small(_pack([c, norm_g, pool_scale, w_a2_f, w_a2_b, w_pool]), "gather_params")
    c_all, norm_g_all, pool_scale_all, w2f_all, w2b_all, w_pool_all = _unpack(g1, small_shapes)
    wts = {
        "norm_g": _cols_to_full(norm_g_all),
        "pool_scale": _cols_to_full(pool_scale_all),
        "w_a2_f": _cols_to_full(w2f_all)[0],
        "w_a2_b": _cols_to_full(w2b_all)[0],
        "w_pool": jnp.moveaxis(w_pool_all[:, 0], 0, 1).reshape(w_pool.shape[1], -1, w_pool.shape[3]),
        "b_a_f": b_a_f, "b_a_b": b_a_b, "sink": sink[0], "gla_g": gla_g, "final_g": final_g[None, :],
    }

    craw = jnp.concatenate([c_all, c_ctx[None, :], jnp.zeros((16 - N_DEV - 1, d), F32)], axis=0)
    b_cols = lax.dynamic_slice_in_dim(b_mod, me * nc, nc, axis=1)[:, None, :]
    mm_cols = _adaln_fwd(craw, w_mod, b_cols, "adaln_fwd")
    g2 = _gather_small(mm_cols.reshape(1, -1), "gather_mods").reshape(N_DEV, 2, 16, nc)
    mm_full = jnp.moveaxis(g2, 0, 2).reshape(2, 16, N_MOD, d)
    mods = jnp.stack([lax.dynamic_index_in_dim(mm_full, me, axis=1, keepdims=False), mm_full[:, N_DEV]], axis=1)
    mods = jnp.pad(mods, ((0, 0), (0, 0), (0, 16 - N_MOD), (0, 0)))

    tr = lambda w: jnp.swapaxes(w, 1, 2).astype(BF16)
    wi1_sh, wi2_sh, wo1_sh, wo2_sh = tr(ffn1_wi), tr(ffn2_wi), ffn1_wo.astype(BF16), ffn2_wo.astype(BF16)
    w_in_sh = jnp.pad(tr(w_in), ((0, 0), (0, ncol_pad - ncol_in), (0, 0)))
    groups = [
        {"wi1_0": wi1_sh[0:1]},
        {"wo1_0": wo1_sh[0:1]},
        {"w_in": w_in_sh},
        {"w_out": w_out.astype(BF16), "wi2_0": wi2_sh[0:1], "wo2_0": wo2_sh[0:1]},
        {"wi1_1": wi1_sh[1:2], "wo1_1": wo1_sh[1:2], "wi2_1": wi2_sh[1:2], "wo2_1": wo2_sh[1:2]},
    ]

    gathers, token = [], mods
    for gi, grp in enumerate(groups):
        lands = [_place_shard(s, me, f"gather_place_{nm}") for nm, s in grp.items()]
        gathers.append(_exchange_start(list(grp.values()), lands, True, 1 + gi, token, f"gather_start_{gi}"))
        token = gathers[-1][4]
    n_proj = -(-(N_DEV * ncol_in) // 128) * 128

    def fetch(gi, after):
        _, lands = _exchange_wait(gathers[gi], True, token if after is None else after, f"gather_wait_{gi}")
        out = dict(zip(groups[gi].keys(), lands))
        if "w_in" in out:
            w_in_t = out.pop("w_in").reshape(1, N_DEV, ncol_pad, d)[:, :, :ncol_in].reshape(1, N_DEV * ncol_in, d)
            out["w_in_t"] = jnp.pad(w_in_t, ((0, 0), (0, n_proj - N_DEV * ncol_in), (0, 0)))
        return out

    scatters = []

    def emit(stage, arrays):
        if stage == 3:
            dw_in_t, dw_out, dw_pool = arrays
            dw_in_full = dw_in_t[:N_DEV * ncol_in].reshape(N_DEV, ncol_in, d)
            dw_in_full = jnp.pad(dw_in_full, ((0, 0), (0, ncol_pad - ncol_in), (0, 0)))
            srcs = [dw_in_full.reshape(1, N_DEV * ncol_pad, d), dw_out[None], dw_pool.astype(BF16)]
        else:
            srcs = [a[None] for a in arrays]
        lands = [lax.empty((N_DEV, s.shape[0], s.shape[1] // N_DEV, s.shape[2]), s.dtype) for s in srcs]
        scatters.append(_exchange_start(srcs, lands, False, 1 + len(groups) + stage, None, f"scatter_start_{stage}"))
        return scatters[-1][4]

    loss_vec, grad_x, grads, dmods = _local_step(x[0], ctx[0], loss_target[0], mods, wts, fetch, emit)
    loss = lax.psum(jnp.sum(loss_vec), ("x", "y", "c"))

    def reduce_stage(stage, after):
        wholes, lands = _exchange_wait(scatters[stage], False, after, f"scatter_wait_{stage}")
        return [_sum_slots(ld, wh, me, f"sum_grad_{stage}_{i}") for i, (ld, wh) in enumerate(zip(lands, wholes))]

    (dwi2_1, dwo2_1), (dwi1_1, dwo1_1), (dwi2_0, dwo2_0), (dw_in_s, dw_out_s, dw_pool_s) = [
        reduce_stage(stage, grad_x) for stage in range(4)]
    back = lambda g: jnp.swapaxes(g, 1, 2)
    g_big = {
        "ffn2_wi": back(jnp.concatenate([dwi2_0, dwi2_1], axis=0)), "ffn2_wo": jnp.concatenate([dwo2_0, dwo2_1], axis=0),
        "w_in": back(dw_in_s[:, :ncol_in]), "w_out": dw_out_s, "w_pool": dw_pool_s[None],
    }

    order = ["c_ctx", "w_mod", "b_mod", "norm_g", "ffn1_wi", "ffn1_wo", "ffn2_wi", "ffn2_wo", "w_in", "w_a2_f", "b_a_f",
             "w_a2_b", "b_a_b", "sink", "gla_g", "w_out", "w_pool", "pool_scale", "final_g"]
    ws = dict(c_ctx=c_ctx, w_mod=w_mod, b_mod=b_mod, norm_g=norm_g, ffn1_wi=ffn1_wi, ffn1_wo=ffn1_wo, ffn2_wi=ffn2_wi,
              ffn2_wo=ffn2_wo, w_in=w_in, w_a2_f=w_a2_f, b_a_f=b_a_f, w_a2_b=w_a2_b, b_a_b=b_a_b, sink=sink, gla_g=gla_g,
              w_out=w_out, w_pool=w_pool, pool_scale=pool_scale, final_g=final_g)
    ms = dict(c_ctx=m_c_ctx, w_mod=m_w_mod, b_mod=m_b_mod, norm_g=m_norm_g, ffn1_wi=m_ffn1_wi, ffn1_wo=m_ffn1_wo,
              ffn2_wi=m_ffn2_wi, ffn2_wo=m_ffn2_wo, w_in=m_w_in, w_a2_f=m_w_a2_f, b_a_f=m_b_a_f, w_a2_b=m_w_a2_b,
              b_a_b=m_b_a_b, sink=m_sink, gla_g=m_gla_g, w_out=m_w_out, w_pool=m_w_pool, pool_scale=m_pool_scale,
              final_g=m_final_g)
    vs = dict(c_ctx=v_c_ctx, w_mod=v_w_mod, b_mod=v_b_mod, norm_g=v_norm_g, ffn1_wi=v_ffn1_wi, ffn1_wo=v_ffn1_wo,
              ffn2_wi=v_ffn2_wi, ffn2_wo=v_ffn2_wo, w_in=v_w_in, w_a2_f=v_w_a2_f, b_a_f=v_b_a_f, w_a2_b=v_w_a2_b,
              b_a_b=v_b_a_b, sink=v_sink, gla_g=v_gla_g, w_out=v_w_out, w_pool=v_w_pool, pool_scale=v_pool_scale,
              final_g=v_final_g)
    early, late = ["ffn2_wi", "ffn2_wo", "w_out", "w_in", "w_pool"], ["ffn1_wi", "ffn1_wo"]
    big = early + ["w_mod"] + late
    delta, new_m, new_v = {}, {}, {}
    g_all = dict(g_big)

    def adamw_big(nm):
        shp = ws[nm].shape
        two_d = lambda a: a.reshape(-1, shp[-1])
        dl, nm_, nv_ = _adamw(two_d(ws[nm]), two_d(g_all[nm]), two_d(ms[nm]), two_d(vs[nm]), f"adamw_{nm}")
        delta[nm], new_m[nm], new_v[nm] = dl.reshape(shp), nm_.reshape(shp), nv_.reshape(shp)

    for nm in early:
        adamw_big(nm)

    small_g = [dmods[:, :, :N_MOD].reshape(2, 2, N_MOD * d), grads["norm_g"], grads["pool_scale"], grads["final_g"],
               grads["b_a_f"], grads["b_a_b"], grads["sink"], grads["gla_g"], grads["w_a2_f"], grads["w_a2_b"]]
    small_g_shapes = [a.shape for a in small_g]
    g3 = _gather_small(_pack(small_g), "gather_small_grads", dep=delta["w_out"])
    total = _sum_rows8(g3, "sum_small_grads")
    dmm_all = _unpack(g3, small_g_shapes[:1])[0]
    (dmm_sum, dnorm_g, dpool_scale, dfinal_g, db_a_f, db_a_b, dsink, dgla_g, dw_a2_f, dw_a2_b) = [
        a[0] for a in _unpack(total, small_g_shapes)]
    dmm_rows = jnp.concatenate([dmm_all[:, :, 0].transpose(1, 0, 2), dmm_sum[:, 1][:, None, :],
                                jnp.zeros((2, 16 - N_DEV - 1, N_MOD * d), F32)], axis=1)
    grad_b_mod = dmm_sum[:, 0] + dmm_sum[:, 1]
    dmm_cols = lax.dynamic_slice_in_dim(dmm_rows, me * nc, nc, axis=2)
    cs_t = jnp.transpose(_silu(craw)).astype(BF16)
    grad_w_mod, dcraw = _adaln_bwd(craw, cs_t, dmm_cols, w_mod, "adaln_bwd")
    g4 = _gather_small((dcraw[0, N_DEV] + dcraw[1, N_DEV])[None, :], "gather_c_ctx_grad")
    grad_c_ctx = _sum_rows8(g4, "sum_c_ctx_grad")[0]

    col = lambda v, n: lax.dynamic_slice_in_dim(v, me * n, n, axis=v.ndim - 1)
    g_small = {
        "c_ctx": grad_c_ctx, "b_mod": grad_b_mod, "norm_g": col(dnorm_g, norm_g.shape[2]),
        "w_a2_f": col(dw_a2_f, w_a2_f.shape[2])[None], "b_a_f": db_a_f[None], "w_a2_b": col(dw_a2_b, w_a2_b.shape[2])[None],
        "b_a_b": db_a_b[None], "sink": dsink[None], "gla_g": dgla_g[None], "pool_scale": col(dpool_scale, pool_scale.shape[1])[None],
        "final_g": dfinal_g,
    }
    g_all.update(g_small, w_mod=grad_w_mod)
    adamw_big("w_mod")
    rest = [nm for nm in order if nm not in big]
    rest_shapes = [ws[nm].shape for nm in rest]
    packed = [_pack([d_[nm].reshape(ws[nm].shape) for nm in rest]).reshape(-1, 128) for d_ in (ws, g_all, ms, vs)]
    pad_rows = (-packed[0].shape[0]) % 512
    packed = [jnp.pad(p, ((0, pad_rows), (0, 0))) for p in packed]
    outs = _adamw(*packed, "adamw_small")
    for dst, arr in zip((delta, new_m, new_v), outs):
        for nm, val in zip(rest, _unpack(arr.reshape(1, -1), rest_shapes)):
            dst[nm] = val[0]

    (dwo1_0,), (dwi1_0,) = reduce_stage(4, outs[0]), reduce_stage(5, outs[0])
    g_all["ffn1_wi"] = back(jnp.concatenate([dwi1_0, dwi1_1], axis=0))
    g_all["ffn1_wo"] = jnp.concatenate([dwo1_0, dwo1_1], axis=0)
    for nm in late:
        adamw_big(nm)
    g_all = {nm: g_all[nm].reshape(ws[nm].shape) for nm in order}

    return (loss, grad_x[None], *[g_all[nm] for nm in order], *[delta[nm] for nm in order],
            *[new_m[nm] for nm in order], *[new_v[nm] for nm in order])
```

```python
import functools

import numpy as np
import jax
import jax.numpy as jnp
from jax import lax
from jax.experimental import pallas as pl
from jax.experimental.pallas import tpu as pltpu

F32 = jnp.float32
BF16 = jnp.bfloat16
MESH = pl.DeviceIdType.MESH

N_DEV = 8
RMS_EPS = 1e-6
N_MOD = 9
GRID_W = 64
A_HEADS, A_KV_HEADS, A_HEAD_DIM = 8, 2, 64
A_REP = A_HEADS // A_KV_HEADS
WINDOW = 128
ROPE_BASE = 10000.0
B_HEADS, B_DK, B_DV = 4, 64, 128
B_GATE_RANK = 16
B_GATE_NORM = 16.0
B_CHUNK = 64
POOL_WINDOWS = (2, 4, 8, 16)
POOL_PAD = 8
A_Q = A_HEADS * A_HEAD_DIM
A_KV = A_KV_HEADS * A_HEAD_DIM
B_QK = B_HEADS * B_DK
B_V = B_HEADS * B_DV
PROJ_SIZES = (A_Q, A_KV, A_KV, B_QK, B_QK, B_V, B_V, 2 * B_GATE_RANK)
PROJ_DIM = sum(PROJ_SIZES)
ADAM_LR, ADAM_B1, ADAM_B2, ADAM_EPS, ADAM_WD, ADAM_STEP = 0.001, 0.9, 0.999, 1e-08, 0.01, 10

VMEM_LIMIT = 56 * 1024 * 1024
ROW_TILES = (512, 544, 256, 128, 64, 32, 16, 8)
TALL_TILES = (1024, 1088) + ROW_TILES

NN = ((1,), (0,))
NT = ((1,), (1,))
TN = ((0,), (0,))


def _dot(a, b, dims=NN, prec=None):
    return lax.dot_general(a, b, (dims, ((), ())), precision=prec, preferred_element_type=F32)


def _bdot(a, b, dims=NN):
    return _dot(a.astype(BF16), b.astype(BF16), dims)


def _dot_01(sel, x):
    hi = x.astype(BF16)
    rest = x - hi.astype(F32)
    mid = rest.astype(BF16)
    lo = (rest - mid.astype(F32)).astype(BF16)
    sel = sel.astype(BF16)
    return _dot(sel, hi) + _dot(sel, mid) + _dot(sel, lo)


def _params(sem=None, **kw):
    return pltpu.CompilerParams(dimension_semantics=sem, vmem_limit_bytes=VMEM_LIMIT, **kw)


def _silu(a):
    return a * jax.nn.sigmoid(a)


def _pick(n, prefs):
    for p in prefs:
        if n % p == 0:
            return p
    return n


def _full(shape):
    nd = len(shape)
    return pl.BlockSpec(shape, lambda *_: (0,) * nd)


def _peers():
    x, y, c = lax.axis_index("x"), lax.axis_index("y"), lax.axis_index("c")
    return x, y, c


def _dev_index():
    x, y, c = _peers()
    return 4 * x + 2 * y + c


def _others(x, y, c):
    return [(x, y, 1 - c), (1 - x, y, c), (x, 1 - y, c), (1 - x, 1 - y, c),
            (1 - x, y, 1 - c), (x, 1 - y, 1 - c), (1 - x, 1 - y, 1 - c)]


def _index_of(dev):
    return 4 * dev[0] + 2 * dev[1] + dev[2]


def _exchange_refs(gather, shapes, srcs, lands, a, me, to):
    if gather:
        r = shapes[a][1]
        return srcs[a], lands[a].at[:, pl.ds(_index_of(me) * r, r), :]
    r = shapes[a][1] // N_DEV
    return srcs[a].at[:, pl.ds(_index_of(to) * r, r), :], lands[a].at[_index_of(me)]


HBM_SPEC = pl.BlockSpec(memory_space=pltpu.HBM)
SEM_SPEC = pl.BlockSpec(memory_space=pltpu.SEMAPHORE)
EFFECT = pltpu.SideEffectType.DATAFLOW_SIDE_EFFECTING


def _exchange_start(srcs, lands, gather, collective_id, dep, name):
    n = len(srcs)
    shapes = [s.shape for s in srcs]
    deps = [] if dep is None else [dep]

    def body(*refs):
        src_refs, land_refs = refs[:n], refs[n:2 * n]
        send_sems, recv_sems = refs[2 * n + len(deps)], refs[2 * n + len(deps) + 1]
        token = refs[-1]
        x, y, c = _peers()
        others = _others(x, y, c)
        barrier = pltpu.get_barrier_semaphore()
        for peer in others:
            pl.semaphore_signal(barrier, inc=1, device_id=peer, device_id_type=MESH)
        pl.semaphore_wait(barrier, len(others))
        for a in range(n):
            for k, to in enumerate(others):
                src, dst = _exchange_refs(gather, shapes, src_refs, land_refs, a, (x, y, c), to)
                pltpu.make_async_remote_copy(src_ref=src, dst_ref=dst, send_sem=send_sems.at[7 * a + k],
                                             recv_sem=recv_sems.at[7 * a + k], device_id=to, device_id_type=MESH).start()
        token[...] = jnp.zeros_like(token)

    outs = pl.pallas_call(
        body, name=name,
        out_shape=(pltpu.SemaphoreType.DMA((7 * n,)), pltpu.SemaphoreType.DMA((7 * n,)),
                   *[pltpu.HBM(s.shape, s.dtype) for s in srcs], *[pltpu.HBM(l.shape, l.dtype) for l in lands],
                   jax.ShapeDtypeStruct((8, 128), F32)),
        in_specs=[HBM_SPEC] * (2 * n) + [pl.BlockSpec(memory_space=pl.ANY)] * len(deps),
        out_specs=(SEM_SPEC, SEM_SPEC, *[HBM_SPEC] * (2 * n), pl.BlockSpec(memory_space=pltpu.VMEM)),
        input_output_aliases={i: 2 + i for i in range(2 * n)},
        compiler_params=pltpu.CompilerParams(has_side_effects=EFFECT, collective_id=collective_id),
    )(*[pltpu.with_memory_space_constraint(s, pltpu.HBM) for s in srcs],
      *[pltpu.with_memory_space_constraint(l, pltpu.HBM) for l in lands], *deps)
    return outs[0], outs[1], list(outs[2:2 + n]), list(outs[2 + n:2 + 2 * n]), outs[-1]


def _exchange_wait(started, gather, after, name):
    send_sems, recv_sems, srcs, lands, _ = started
    n = len(srcs)
    shapes = [s.shape for s in srcs]

    def body(*refs):
        src_refs, land_refs = refs[:n], refs[n:2 * n]
        send_sems, recv_sems = refs[2 * n], refs[2 * n + 1]
        x, y, c = _peers()
        for a in range(n):
            for k, peer in enumerate(_others(x, y, c)):
                src, _ = _exchange_refs(gather, shapes, src_refs, land_refs, a, (x, y, c), peer)
                _, dst = _exchange_refs(gather, shapes, src_refs, land_refs, a, peer, (x, y, c))
                copy = pltpu.make_async_remote_copy(src_ref=src, dst_ref=dst, send_sem=send_sems.at[7 * a + k],
                                                    recv_sem=recv_sems.at[7 * a + k], device_id=peer, device_id_type=MESH)
                copy.wait_send()
                copy.wait_recv()

    outs = pl.pallas_call(
        body, name=name,
        out_shape=(*[pltpu.HBM(s.shape, s.dtype) for s in srcs], *[pltpu.HBM(l.shape, l.dtype) for l in lands]),
        in_specs=[HBM_SPEC] * (2 * n) + [SEM_SPEC, SEM_SPEC, pl.BlockSpec(memory_space=pl.ANY)],
        out_specs=tuple([HBM_SPEC] * (2 * n)),
        input_output_aliases={i: i for i in range(2 * n)},
        compiler_params=pltpu.CompilerParams(has_side_effects=EFFECT),
    )(*srcs, *lands, send_sems, recv_sems, after)
    return list(outs[:n]), list(outs[n:])


def _place_shard(shard, me, name):
    a_, r, c = shard.shape
    tr = _pick(r, (352, 304, 256, 128, 64, 32, 16, 8))
    nr = r // tr

    def body(me_ref, in_ref, out_ref):
        out_ref[...] = in_ref[...]

    return pl.pallas_call(
        body, name=name,
        grid_spec=pltpu.PrefetchScalarGridSpec(
            num_scalar_prefetch=1, grid=(a_, nr),
            in_specs=[pl.BlockSpec((None, tr, c), lambda i, j, me_ref: (i, j, 0))],
            out_specs=pl.BlockSpec((None, tr, c), lambda i, j, me_ref: (i, me_ref[0] * nr + j, 0))),
        out_shape=jax.ShapeDtypeStruct((a_, N_DEV * r, c), shard.dtype),
        compiler_params=_params(("parallel", "parallel")),
    )(me.reshape(1).astype(jnp.int32), shard)


def _sum_slots(land, whole, me, name):
    _, a_, r, c = land.shape
    tr = _pick(r, (352, 256, 128, 64, 32, 16, 8))
    nr = r // tr

    def body(me_ref, land_ref, own_ref, out_ref):
        acc = None
        for s in range(N_DEV):
            part = jnp.where(me_ref[0] == s, own_ref[...], land_ref[s]).astype(F32)
            acc = part if acc is None else acc + part
        out_ref[...] = acc

    return pl.pallas_call(
        body, name=name,
        grid_spec=pltpu.PrefetchScalarGridSpec(
            num_scalar_prefetch=1, grid=(a_, nr),
            in_specs=[pl.BlockSpec((N_DEV, None, tr, c), lambda i, j, me_ref: (0, i, j, 0)),
                      pl.BlockSpec((None, tr, c), lambda i, j, me_ref: (i, me_ref[0] * nr + j, 0))],
            out_specs=pl.BlockSpec((None, tr, c), lambda i, j, me_ref: (i, j, 0))),
        out_shape=jax.ShapeDtypeStruct((a_, r, c), F32),
        compiler_params=_params(("parallel", "parallel")),
    )(me.reshape(1).astype(jnp.int32), land, whole)


def _gather_small(vec, name, dep=None):
    p = vec.shape[1]
    pp = -(-p // 1024) * 1024
    blk = jnp.pad(vec, ((0, 0), (0, pp - p))).reshape(8, pp // 8)
    deps = [] if dep is None else [dep]

    def body(in_ref, *rest):
        out_ref, send_sems, recv_sems = rest[-3:]
        x, y, c = _peers()
        me = 4 * x + 2 * y + c
        others = [(x, y, 1 - c), (1 - x, y, c), (x, 1 - y, c), (1 - x, 1 - y, c),
                  (1 - x, y, 1 - c), (x, 1 - y, 1 - c), (1 - x, 1 - y, 1 - c)]

        def rows(idx):
            return out_ref.at[pl.ds(pl.multiple_of(idx * 8, 8), 8), :]

        out_ref[pl.ds(pl.multiple_of(me * 8, 8), 8), :] = in_ref[...]

        def copy(k, dev, slot):
            return pltpu.make_async_remote_copy(
                src_ref=in_ref, dst_ref=rows(slot), send_sem=send_sems.at[k], recv_sem=recv_sems.at[k],
                device_id=dev, device_id_type=MESH)

        sends = [copy(k, dev, me) for k, dev in enumerate(others)]
        for cp in sends:
            cp.start()
        for k, dev in enumerate(others):
            copy(k, dev, 4 * dev[0] + 2 * dev[1] + dev[2]).wait_recv()
        for cp in sends:
            cp.wait_send()

    vm = pl.BlockSpec(memory_space=pltpu.VMEM)
    out = pl.pallas_call(
        body, name=name, out_shape=jax.ShapeDtypeStruct((8 * N_DEV, pp // 8), F32),
        in_specs=[vm] + [pl.BlockSpec(memory_space=pl.ANY)] * len(deps), out_specs=vm,
        scratch_shapes=[pltpu.SemaphoreType.DMA((7,)), pltpu.SemaphoreType.DMA((7,))],
        compiler_params=pltpu.CompilerParams(has_side_effects=True, vmem_limit_bytes=VMEM_LIMIT),
    )(blk, *deps)
    return out.reshape(N_DEV, pp)[:, :p]


def _sum_rows8(g, name):
    p = g.shape[1]

    def body(in_ref, out_ref):
        acc = in_ref[0:1, :]
        for s in range(1, N_DEV):
            acc = acc + in_ref[s:s + 1, :]
        out_ref[...] = acc

    return pl.pallas_call(body, name=name, out_shape=jax.ShapeDtypeStruct((1, p), F32),
                          compiler_params=_params())(g)


def _sel_row(mods_ref, is_ctx, k):
    return jnp.where(is_ctx, mods_ref[1, k:k + 1, :], mods_ref[0, k:k + 1, :])


def _modulate(z, mods, g, ks, kc, n_x, out_dtype, name):
    m, d = z.shape
    tm = _pick(m, (256, 128, 64, 32, 16, 8))

    def body(z_ref, mods_ref, g_ref, h_ref):
        is_ctx = pl.program_id(0) * tm >= n_x
        zz = z_ref[...]
        r = lax.rsqrt(jnp.mean(zz * zz, axis=-1, keepdims=True) + RMS_EPS)
        shift, scale = _sel_row(mods_ref, is_ctx, ks), _sel_row(mods_ref, is_ctx, kc)
        h_ref[...] = ((zz * r) * g_ref[...] * (1.0 + scale) + shift).astype(out_dtype)

    return pl.pallas_call(
        body, name=name, grid=(m // tm,),
        in_specs=[pl.BlockSpec((tm, d), lambda i: (i, 0)), _full(mods.shape), _full(g.shape)],
        out_specs=pl.BlockSpec((tm, d), lambda i: (i, 0)),
        out_shape=jax.ShapeDtypeStruct((m, d), out_dtype),
        compiler_params=_params(("parallel",)),
    )(z, mods, g)


def _gate_bwd_rows(dx, y, gate, coef):
    return (coef * gate * dx).astype(BF16), jnp.sum(coef * y * dx, axis=0, keepdims=True)


def _modulate_bwd(z, dh, dres, mods, g, kc, n_x, name, latent_only=False, then=None):
    m, d = z.shape
    tm = _pick(m, (256, 128, 64, 32, 16, 8))
    first_ctx = n_x // tm
    res_blocks = dres.shape[0] // tm
    out_blocks = (n_x if latent_only else m) // tm
    extra = [] if then is None else [then[0], then[1]]

    def body(z_ref, dh_ref, dres_ref, mods_ref, g_ref, *rest):
        i = pl.program_id(0)
        is_ctx = i * tm >= n_x
        dx_ref, acc_ref = rest[len(extra)], rest[len(extra) + 1]

        @pl.when((i == 0) | (i == first_ctx))
        def _():
            acc_ref[...] = jnp.zeros_like(acc_ref)
            if then is not None:
                rest[-1][...] = jnp.zeros_like(rest[-1])

        zz, dhh = z_ref[...], dh_ref[...]
        r = lax.rsqrt(jnp.mean(zz * zz, axis=-1, keepdims=True) + RMS_EPS)
        nz = zz * r
        gain = g_ref[...] * (1.0 + _sel_row(mods_ref, is_ctx, kc))
        dn = dhh * gain
        dz = r * (dn - nz * jnp.mean(dn * nz, axis=-1, keepdims=True))
        dx = jnp.where(i < res_blocks, dres_ref[...], 0.0) + dz

        @pl.when(i < out_blocks)
        def _():
            dx_ref[...] = dx

        acc_ref[0:1, :] += jnp.sum(dhh, axis=0, keepdims=True)
        acc_ref[1:2, :] += jnp.sum(dhh * nz, axis=0, keepdims=True)
        if then is not None:
            y_ref, tmods_ref, dy_ref, gate_acc_ref = rest[0], rest[1], rest[-2], rest[-1]
            dy, part = _gate_bwd_rows(dx, y_ref[...], _sel_row(tmods_ref, is_ctx, then[2]), then[3])
            dy_ref[...] = dy
            gate_acc_ref[0:1, :] += part

    row = pl.BlockSpec((tm, d), lambda i: (i, 0))
    acc_spec = pl.BlockSpec((None, 8, d), lambda i: ((i * tm >= n_x).astype(jnp.int32), 0, 0))
    out_specs = [pl.BlockSpec((tm, d), lambda i: (jnp.minimum(i, out_blocks - 1), 0)), acc_spec]
    out_shape = [jax.ShapeDtypeStruct((out_blocks * tm, d), F32), jax.ShapeDtypeStruct((2, 8, d), F32)]
    in_specs = [row, row, pl.BlockSpec((tm, d), lambda i: (jnp.minimum(i, res_blocks - 1), 0)),
                _full(mods.shape), _full(g.shape)]
    if then is not None:
        in_specs += [row, _full(then[1].shape)]
        out_specs += [row, acc_spec]
        out_shape += [jax.ShapeDtypeStruct((m, d), BF16), jax.ShapeDtypeStruct((2, 8, d), F32)]
    return pl.pallas_call(
        body, name=name, grid=(m // tm,), in_specs=in_specs, out_specs=out_specs, out_shape=out_shape,
        compiler_params=_params(("arbitrary",)),
    )(z, dh, dres, mods, g, *extra)


def _ffn_up(h, wi_t, layer, name):
    m, d = h.shape
    f = wi_t.shape[1] // 2
    tm = _pick(m, ROW_TILES)

    def body(h_ref, w_ref, jac_ref, act_ref):
        hh = h_ref[...]
        a = _dot(hh, w_ref[0:f, :], NT)
        u = _dot(hh, w_ref[f:2 * f, :], NT)
        sg = jax.nn.sigmoid(a)
        s = a * sg
        jac_ref[:, 0:f] = (u * (sg * (1.0 + a * (1.0 - sg)))).astype(BF16)
        jac_ref[:, f:2 * f] = s.astype(BF16)
        act_ref[...] = (s * u).astype(BF16)

    return pl.pallas_call(
        body, name=name, grid=(m // tm,),
        in_specs=[pl.BlockSpec((tm, d), lambda i: (i, 0)),
                  pl.BlockSpec((None, 2 * f, d), lambda i: (layer, 0, 0))],
        out_specs=[pl.BlockSpec((tm, 2 * f), lambda i: (i, 0)), pl.BlockSpec((tm, f), lambda i: (i, 0))],
        out_shape=[jax.ShapeDtypeStruct((m, 2 * f), BF16), jax.ShapeDtypeStruct((m, f), BF16)],
        compiler_params=_params(("parallel",)),
    )(h, wi_t)


def _mm_resid(a, b, layer, res, mods, km, coef, n_x, name, nxt=None):
    m, k = a.shape
    n = b.shape[2]
    tm = _pick(m, (512, 256, 128, 64, 32, 16, 8))
    tn = n if nxt is not None else _pick(n, (1024, 512, 256, 128))
    extra = [] if nxt is None else [nxt[0], nxt[1]]

    def body(a_ref, b_ref, res_ref, mods_ref, *rest):
        is_ctx = pl.program_id(1) * tm >= n_x
        y = _dot(a_ref[...], b_ref[...])
        new = res_ref[...] + coef * _sel_row(mods_ref, is_ctx, km) * y
        if nxt is None:
            out_ref, y_ref = rest
        else:
            nmods_ref, g_ref, out_ref, y_ref, h_ref = rest
            r = lax.rsqrt(jnp.mean(new * new, axis=-1, keepdims=True) + RMS_EPS)
            shift, scale = _sel_row(nmods_ref, is_ctx, nxt[2]), _sel_row(nmods_ref, is_ctx, nxt[3])
            h_ref[...] = ((new * r) * g_ref[...] * (1.0 + scale) + shift).astype(nxt[4])
        y_ref[...] = y
        out_ref[...] = new

    tile = pl.BlockSpec((tm, tn), lambda j, i: (i, j))
    outs = [jax.ShapeDtypeStruct((m, n), F32), jax.ShapeDtypeStruct((m, n), F32)]
    if nxt is not None:
        outs.append(jax.ShapeDtypeStruct((m, n), nxt[4]))
    return pl.pallas_call(
        body, name=name, grid=(n // tn, m // tm),
        in_specs=[pl.BlockSpec((tm, k), lambda j, i: (i, 0)),
                  pl.BlockSpec((None, k, tn), lambda j, i: (layer, 0, j)),
                  tile, pl.BlockSpec((2, 16, tn), lambda j, i: (0, 0, j))] + [_full(e.shape) for e in extra],
        out_specs=[tile] * len(outs), out_shape=outs,
        compiler_params=_params(("parallel", "parallel")),
    )(a, b, res, mods, *extra)


def _ffn_down_bwd(dy, wo, layer, au, name):
    m, d = dy.shape
    f = wo.shape[1]
    tm = _pick(m, ROW_TILES)

    def body(dy_ref, wo_ref, au_ref, dau_ref):
        dact = _dot(dy_ref[...], wo_ref[...], NT)
        dau_ref[:, 0:f] = (dact * au_ref[:, 0:f].astype(F32)).astype(BF16)
        dau_ref[:, f:2 * f] = (dact * au_ref[:, f:2 * f].astype(F32)).astype(BF16)

    wide = pl.BlockSpec((tm, 2 * f), lambda i: (i, 0))
    return pl.pallas_call(
        body, name=name, grid=(m // tm,),
        in_specs=[pl.BlockSpec((tm, d), lambda i: (i, 0)), pl.BlockSpec((None, f, d), lambda i: (layer, 0, 0)), wide],
        out_specs=wide, out_shape=jax.ShapeDtypeStruct((m, 2 * f), BF16),
        compiler_params=_params(("parallel",)),
    )(dy, wo, au)


def _mm(terms, dims, n, out_dtype, name, tm_pref=(512, 256, 128, 64, 32, 16, 8), tn_pref=(512, 256, 128), dep=None):
    m = terms[0][0].shape[0]
    tm = _pick(m, tm_pref)
    tn = _pick(n, tn_pref)
    nt = len(terms)
    deps = [] if dep is None else [dep]

    def body(*refs):
        out_ref = refs[-1]
        acc = None
        for t in range(nt):
            part = _dot(refs[2 * t][...].astype(BF16), refs[2 * t + 1][...].astype(BF16), dims)
            acc = part if acc is None else acc + part
        out_ref[...] = acc.astype(out_dtype)

    in_specs, args = [], []
    for a, b, layer, rb in terms:
        k = a.shape[1]
        in_specs.append(pl.BlockSpec((tm, k), lambda j, i: (i, 0)))
        if dims == NN:
            in_specs.append(pl.BlockSpec((None, k, tn), lambda j, i, layer=layer, rb=rb: (layer, rb, j)))
        else:
            nb = n // tn
            in_specs.append(pl.BlockSpec((None, tn, k), lambda j, i, layer=layer, rb=rb, nb=nb: (layer, rb * nb + j, 0)))
        args += [a, b]
    return pl.pallas_call(
        body, name=name, grid=(n // tn, m // tm), in_specs=in_specs + [pl.BlockSpec(memory_space=pl.ANY)] * len(deps),
        out_specs=pl.BlockSpec((tm, tn), lambda j, i: (i, j)),
        out_shape=jax.ShapeDtypeStruct((m, n), out_dtype),
        compiler_params=_params(("parallel", "parallel")),
    )(*args, *deps)


def _mm_tn(a, b, out_dtype, name, dep=None):
    t = a.shape[0]
    m, n = a.shape[1], b.shape[1]
    tm = _pick(m, (1408, 2432, 1024, 512, 256, 128))
    tn = _pick(n, (1024, 512, 256, 128))
    tk = _pick(t, TALL_TILES)
    deps = [] if dep is None else [dep]

    def body(a_ref, b_ref, *rest):
        out_ref, acc_ref = rest[-2:]
        kk = pl.program_id(2)

        @pl.when(kk == 0)
        def _():
            acc_ref[...] = jnp.zeros_like(acc_ref)

        acc_ref[...] += _dot(a_ref[...].astype(BF16), b_ref[...].astype(BF16), TN)

        @pl.when(kk == pl.num_programs(2) - 1)
        def _():
            out_ref[...] = acc_ref[...].astype(out_dtype)

    return pl.pallas_call(
        body, name=name, grid=(m // tm, n // tn, t // tk),
        in_specs=[pl.BlockSpec((tk, tm), lambda i, j, k: (k, i)), pl.BlockSpec((tk, tn), lambda i, j, k: (k, j))]
        + [pl.BlockSpec(memory_space=pl.ANY)] * len(deps),
        out_specs=pl.BlockSpec((tm, tn), lambda i, j, k: (i, j)),
        out_shape=jax.ShapeDtypeStruct((m, n), out_dtype),
        scratch_shapes=[pltpu.VMEM((tm, tn), F32)],
        compiler_params=_params(("parallel", "parallel", "arbitrary")),
    )(a, b, *deps)


def _stack_rows(a, b, name):
    ta, d = a.shape
    tm = _pick(int(np.gcd(ta, b.shape[0])), (256, 128, 64, 32, 16, 8))
    na, nb = ta // tm, b.shape[0] // tm

    def body(a_ref, b_ref, o_ref):
        o_ref[...] = jnp.where(pl.program_id(0) < na, a_ref[...], b_ref[...])

    return pl.pallas_call(
        body, name=name, grid=(na + nb,),
        in_specs=[pl.BlockSpec((tm, d), lambda i: (jnp.minimum(i, na - 1), 0)),
                  pl.BlockSpec((tm, d), lambda i: (jnp.maximum(i - na, 0), 0))],
        out_specs=pl.BlockSpec((tm, d), lambda i: (i, 0)),
        out_shape=jax.ShapeDtypeStruct((ta + b.shape[0], d), a.dtype),
        compiler_params=_params(("parallel",)),
    )(a, b)


def _assemble_dz(lat_parts, ctx_parts, both_parts, width, name):
    t = next(p.shape[0] for p in lat_parts if p is not None)
    l_ctx = next(p.shape[0] for p in ctx_parts if p is not None)
    tm = _pick(int(np.gcd(t, l_ctx)), (256, 128, 64, 32, 16, 8))
    nt, nl = t // tm, l_ctx // tm
    plan, args, in_specs, off = [], [], [], 0
    lat_spec = lambda w: pl.BlockSpec((tm, w), lambda i: (jnp.minimum(i, nt - 1), 0))
    ctx_spec = lambda w: pl.BlockSpec((tm, w), lambda i: (jnp.maximum(i - nt, 0), 0))
    all_spec = lambda w: pl.BlockSpec((tm, w), lambda i: (i, 0))
    for lat, ctx, both in zip(lat_parts, ctx_parts, both_parts):
        if both:
            w = both[0].shape[1]
            plan.append(("both", off, w, len(args), len(both)))
            args += both
            in_specs += [all_spec(w)] * len(both)
        else:
            w = (lat if lat is not None else ctx).shape[1]
            plan.append(("split", off, w, len(args), (lat is not None, ctx is not None)))
            for part, spec in ((lat, lat_spec), (ctx, ctx_spec)):
                if part is not None:
                    args.append(part)
                    in_specs.append(spec(w))
        off += w
    n_in = len(args)

    def body(*refs):
        out_ref = refs[n_in]
        is_ctx = pl.program_id(0) >= nt
        for kind, o, w, first, info in plan:
            if kind == "both":
                val = refs[first][...]
                for k in range(1, info):
                    val = val + refs[first + k][...]
            else:
                has_lat, has_ctx = info
                zero = jnp.zeros((tm, w), F32)
                lat = refs[first][...] if has_lat else zero
                ctx = refs[first + int(has_lat)][...] if has_ctx else zero
                val = jnp.where(is_ctx, ctx, lat)
            out_ref[:, o:o + w] = val.astype(BF16)
        if off < width:
            out_ref[:, off:width] = jnp.zeros((tm, width - off), BF16)

    return pl.pallas_call(
        body, name=name, grid=(nt + nl,), in_specs=in_specs,
        out_specs=pl.BlockSpec((tm, width), lambda i: (i, 0)),
        out_shape=jax.ShapeDtypeStruct((t + l_ctx, width), BF16),
        compiler_params=_params(("parallel",)),
    )(*args)


def _final_loss(x, g, target, then, name):
    t, d = x.shape
    tm = _pick(t, (256, 128, 64, 32, 16, 8))
    y, tmods, km, coef = then

    def body(x_ref, g_ref, t_ref, y_ref, tmods_ref, dx_ref, loss_ref, dg_ref, dy_ref, gate_acc_ref):
        @pl.when(pl.program_id(0) == 0)
        def _():
            loss_ref[...] = jnp.zeros_like(loss_ref)
            dg_ref[...] = jnp.zeros_like(dg_ref)
            gate_acc_ref[...] = jnp.zeros_like(gate_acc_ref)

        xx, gg = x_ref[...], g_ref[...]
        r = lax.rsqrt(jnp.mean(xx * xx, axis=-1, keepdims=True) + RMS_EPS)
        nz = xx * r
        err = nz * gg - t_ref[...]
        loss_ref[...] += jnp.sum(err * err, axis=0, keepdims=True) * (0.5 / d)
        dout = err * (1.0 / d)
        dg_ref[...] += jnp.sum(dout * nz, axis=0, keepdims=True)
        dn = dout * gg
        dx = r * (dn - nz * jnp.mean(dn * nz, axis=-1, keepdims=True))
        dx_ref[...] = dx
        dy, part = _gate_bwd_rows(dx, y_ref[...], tmods_ref[0, km:km + 1, :], coef)
        dy_ref[...] = dy
        gate_acc_ref[0:1, :] += part

    row = pl.BlockSpec((tm, d), lambda i: (i, 0))
    vec = pl.BlockSpec((1, d), lambda i: (0, 0))
    acc = pl.BlockSpec((None, 8, d), lambda i: (0, 0, 0))
    return pl.pallas_call(
        body, name=name, grid=(t // tm,), in_specs=[row, vec, row, row, _full(tmods.shape)],
        out_specs=[row, vec, vec, row, acc],
        out_shape=[jax.ShapeDtypeStruct((t, d), F32), jax.ShapeDtypeStruct((1, d), F32),
                   jax.ShapeDtypeStruct((1, d), F32), jax.ShapeDtypeStruct((t, d), BF16),
                   jax.ShapeDtypeStruct((2, 8, d), F32)],
        compiler_params=_params(("arbitrary",)),
    )(x, g, target, y, tmods)


def _adaln_fwd(craw, w_mod, b_cols, name):
    lyr, d, nc = w_mod.shape

    def body(c_ref, w_ref, b_ref, out_ref):
        out_ref[...] = _bdot(_silu(c_ref[...]), w_ref[...]) + b_ref[...]

    return pl.pallas_call(
        body, name=name, grid=(lyr,),
        in_specs=[_full(craw.shape), pl.BlockSpec((None, d, nc), lambda l: (l, 0, 0)),
                  pl.BlockSpec((None, 1, nc), lambda l: (l, 0, 0))],
        out_specs=pl.BlockSpec((None, 16, nc), lambda l: (l, 0, 0)),
        out_shape=jax.ShapeDtypeStruct((lyr, 16, nc), F32),
        compiler_params=_params(("parallel",)),
    )(craw, w_mod, b_cols)


def _adaln_bwd(craw, cs_t, dmm_cols, w_mod, name):
    lyr, d, nc = w_mod.shape

    def body(c_ref, cst_ref, dmm_ref, w_ref, gw_ref, dc_ref):
        dmm = dmm_ref[...]
        gw_ref[...] = _bdot(cst_ref[...], dmm)
        cc = c_ref[...]
        sg = jax.nn.sigmoid(cc)
        dc_ref[...] = _bdot(dmm, w_ref[...], NT) * (sg * (1.0 + cc * (1.0 - sg)))

    wspec = pl.BlockSpec((None, d, nc), lambda l: (l, 0, 0))
    return pl.pallas_call(
        body, name=name, grid=(lyr,),
        in_specs=[_full(craw.shape), _full(cs_t.shape), pl.BlockSpec((None, 16, nc), lambda l: (l, 0, 0)), wspec],
        out_specs=[wspec, pl.BlockSpec((None, 16, d), lambda l: (l, 0, 0))],
        out_shape=[jax.ShapeDtypeStruct((lyr, d, nc), F32), jax.ShapeDtypeStruct((lyr, 16, d), F32)],
        compiler_params=_params(("parallel",)),
    )(craw, cs_t, dmm_cols, w_mod)


def _rope_tables(t):
    rows = np.repeat(np.arange(t // GRID_W, dtype=np.float32), GRID_W)
    cols = np.tile(np.arange(GRID_W, dtype=np.float32), t // GRID_W)
    n = A_HEAD_DIM // 4
    freqs = (ROPE_BASE ** (-np.arange(n, dtype=np.float32) / n)).astype(np.float32)
    ang_r, ang_c = (rows[:, None] * freqs).astype(np.float32), (cols[:, None] * freqs).astype(np.float32)
    cr, sr, cc, sc = np.cos(ang_r), np.sin(ang_r), np.cos(ang_c), np.sin(ang_c)
    cos = np.concatenate([cr, cr, cc, cc] * 2, axis=-1).astype(np.float32)
    sin = np.concatenate([-sr, sr, -sc, sc] * 2, axis=-1).astype(np.float32)
    return jnp.asarray(cos), jnp.asarray(sin)


def _rope(xt, cos, sin, adjoint, name):
    t, w = xt.shape
    tb = _pick(t, (512, 256, 128))
    rep = w // cos.shape[1]

    def body(x_ref, c_ref, s_ref, o_ref):
        xx = x_ref[...]
        cc = jnp.concatenate([c_ref[...]] * rep, axis=1) if rep > 1 else c_ref[...]
        ss = jnp.concatenate([s_ref[...]] * rep, axis=1) if rep > 1 else s_ref[...]
        low = (lax.broadcasted_iota(jnp.int32, xx.shape, 1) % 32) < 16

        def partner(v):
            return jnp.where(low, pltpu.roll(v, w - 16, 1), pltpu.roll(v, 16, 1))

        if adjoint:
            o_ref[...] = xx * cc + partner(xx * ss)
        else:
            o_ref[...] = xx * cc + partner(xx) * ss

    blk = pl.BlockSpec((tb, w), lambda i: (i, 0))
    tab = pl.BlockSpec((tb, cos.shape[1]), lambda i: (i, 0))
    return pl.pallas_call(
        body, name=name, grid=(t // tb,), in_specs=[blk, tab, tab], out_specs=blk,
        out_shape=jax.ShapeDtypeStruct((t, w), F32), compiler_params=_params(("parallel",)),
    )(xt, cos, sin)


def _attn_probs(q, kb, kc, sink, n, t):
    scale = A_HEAD_DIM ** -0.5
    s1 = _bdot(q, kb, NT) * scale
    s2 = _bdot(q, kc, NT) * scale
    qpos = n * WINDOW + lax.broadcasted_iota(jnp.int32, s1.shape, 0) % WINDOW
    kpos = (n - 1) * WINDOW + lax.broadcasted_iota(jnp.int32, s1.shape, 1)
    valid = (kpos >= 0) & (kpos < t) & (jnp.abs(kpos - qpos) <= WINDOW)
    s1 = jnp.where(valid, s1, -jnp.inf)
    mx = jnp.maximum(jnp.maximum(jnp.max(s1, axis=-1, keepdims=True), jnp.max(s2, axis=-1, keepdims=True)), sink)
    p1, p2, ps = jnp.exp(s1 - mx), jnp.exp(s2 - mx), jnp.exp(sink - mx)
    inv = 1.0 / (jnp.sum(p1, axis=-1, keepdims=True) + jnp.sum(p2, axis=-1, keepdims=True) + ps)
    return p1 * inv, p2 * inv, ps * inv


def _sink_rows(sink_ref):
    return jnp.concatenate([jnp.broadcast_to(sink_ref[r], (WINDOW, 1)) for r in range(A_REP)], axis=0)


def _attn_fwd(q, kp, vp, kc, vc, sink, name):
    hq, t, dh = q.shape
    nb = t // WINDOW
    lc = kc.shape[1]
    rows = A_REP * WINDOW

    def body(q_ref, k_ref, v_ref, kc_ref, vc_ref, sink_ref, o_ref):
        n = pl.program_id(1)
        start = pl.multiple_of(n * WINDOW, WINDOW)
        kb, vb = k_ref[pl.ds(start, 3 * WINDOW), :], v_ref[pl.ds(start, 3 * WINDOW), :]
        p1, p2, _ = _attn_probs(q_ref[...].reshape(rows, dh), kb, kc_ref[...], _sink_rows(sink_ref), n, t)
        o_ref[...] = (_bdot(p1, vb) + _bdot(p2, vc_ref[...])).reshape(A_REP, WINDOW, dh)

    qblk = pl.BlockSpec((A_REP, WINDOW, dh), lambda g, n: (g, n, 0))
    kfull = pl.BlockSpec((None, t + 2 * WINDOW, dh), lambda g, n: (g, 0, 0))
    cfull = pl.BlockSpec((None, lc, dh), lambda g, n: (g, 0, 0))
    return pl.pallas_call(
        body, name=name, grid=(hq // A_REP, nb),
        in_specs=[qblk, kfull, kfull, cfull, cfull, pl.BlockSpec((A_REP, 1, 1), lambda g, n: (g, 0, 0))],
        out_specs=qblk, out_shape=jax.ShapeDtypeStruct((hq, t, dh), F32),
        compiler_params=_params(("parallel", "parallel")),
    )(q, kp, vp, kc, vc, sink)


def _attn_bwd(q, kp, vp, kc, vc, sink, o, do, name):
    hq, t, dh = q.shape
    nb = t // WINDOW
    lc = kc.shape[1]
    scale = A_HEAD_DIM ** -0.5
    rows = A_REP * WINDOW

    def body(q_ref, k_ref, v_ref, kc_ref, vc_ref, sink_ref, o_ref, do_ref,
             dq_ref, dk_ref, dv_ref, dkc_ref, dvc_ref, dsink_ref):
        n = pl.program_id(1)

        @pl.when(n == 0)
        def _():
            dk_ref[...] = jnp.zeros_like(dk_ref)
            dv_ref[...] = jnp.zeros_like(dv_ref)
            dkc_ref[...] = jnp.zeros_like(dkc_ref)
            dvc_ref[...] = jnp.zeros_like(dvc_ref)
            dsink_ref[...] = jnp.zeros_like(dsink_ref)

        start = pl.multiple_of(n * WINDOW, WINDOW)
        band = pl.ds(start, 3 * WINDOW)
        qq, kb, vb, kcc, vcc = q_ref[...].reshape(rows, dh), k_ref[band, :], v_ref[band, :], kc_ref[...], vc_ref[...]
        p1, p2, ps = _attn_probs(qq, kb, kcc, _sink_rows(sink_ref), n, t)
        dout = do_ref[...].reshape(rows, dh)
        delta = jnp.sum(dout * o_ref[...].reshape(rows, dh), axis=-1, keepdims=True)
        ds1 = p1 * (_bdot(dout, vb, NT) - delta)
        ds2 = p2 * (_bdot(dout, vcc, NT) - delta)
        dq_ref[...] = ((_bdot(ds1, kb) + _bdot(ds2, kcc)) * scale).reshape(A_REP, WINDOW, dh)
        dk_ref[band, :] += _bdot(ds1.T, qq) * scale
        dv_ref[band, :] += _bdot(p1.T, dout)
        dkc_ref[...] += _bdot(ds2.T, qq) * scale
        dvc_ref[...] += _bdot(p2.T, dout)
        dsink_ref[...] += jnp.sum((-ps * delta).reshape(A_REP, WINDOW, 1), axis=1, keepdims=True)

    qblk = pl.BlockSpec((A_REP, WINDOW, dh), lambda g, n: (g, n, 0))
    kfull = pl.BlockSpec((None, t + 2 * WINDOW, dh), lambda g, n: (g, 0, 0))
    cfull = pl.BlockSpec((None, lc, dh), lambda g, n: (g, 0, 0))
    return pl.pallas_call(
        body, name=name, grid=(hq // A_REP, nb),
        in_specs=[qblk, kfull, kfull, cfull, cfull, pl.BlockSpec((A_REP, 1, 1), lambda g, n: (g, 0, 0)), qblk, qblk],
        out_specs=[qblk, kfull, kfull, cfull, cfull, pl.BlockSpec((A_REP, 8, 128), lambda g, n: (g, 0, 0))],
        out_shape=[jax.ShapeDtypeStruct(q.shape, F32), jax.ShapeDtypeStruct(kp.shape, F32),
                   jax.ShapeDtypeStruct(kp.shape, F32), jax.ShapeDtypeStruct(kc.shape, F32),
                   jax.ShapeDtypeStruct(kc.shape, F32), jax.ShapeDtypeStruct((hq, 8, 128), F32)],
        compiler_params=_params(("parallel", "arbitrary")),
    )(q, kp, vp, kc, vc, sink, o, do)


def _gate_fwd(zg, w2, b2, name):
    m = zg.shape[0]
    n = w2.shape[1]
    tm = _pick(m, (512, 256, 128, 64, 32, 16, 8))

    def body(z_ref, w_ref, b_ref, o_ref):
        o_ref[...] = jax.nn.log_sigmoid(_bdot(z_ref[...], w_ref[...]) + b_ref[...]) / B_GATE_NORM

    return pl.pallas_call(
        body, name=name, grid=(m // tm,),
        in_specs=[pl.BlockSpec((tm, zg.shape[1]), lambda i: (i, 0)), _full(w2.shape), _full(b2.shape)],
        out_specs=pl.BlockSpec((tm, n), lambda i: (i, 0)), out_shape=jax.ShapeDtypeStruct((m, n), F32),
        compiler_params=_params(("parallel",)),
    )(zg, w2, b2)


def _gate_bwd(zg, w2, b2, dla, name):
    m, rk = zg.shape
    n = w2.shape[1]
    tm = _pick(m, (512, 256, 128, 64, 32, 16, 8))

    def body(z_ref, w_ref, b_ref, d_ref, dz_ref, dw_ref, db_ref):
        @pl.when(pl.program_id(0) == 0)
        def _():
            dw_ref[...] = jnp.zeros_like(dw_ref)
            db_ref[...] = jnp.zeros_like(db_ref)

        zz, ww = z_ref[...], w_ref[...]
        pre = _bdot(zz, ww) + b_ref[...]
        dpre = d_ref[...] * (1.0 / B_GATE_NORM) * jax.nn.sigmoid(-pre)
        dz_ref[...] = _bdot(dpre, ww, NT)
        dw_ref[...] += _bdot(zz.T, dpre)
        db_ref[...] += jnp.sum(dpre, axis=0, keepdims=True)

    return pl.pallas_call(
        body, name=name, grid=(m // tm,),
        in_specs=[pl.BlockSpec((tm, rk), lambda i: (i, 0)), _full(w2.shape), _full(b2.shape),
                  pl.BlockSpec((tm, n), lambda i: (i, 0))],
        out_specs=[pl.BlockSpec((tm, rk), lambda i: (i, 0)), _full(w2.shape), _full(b2.shape)],
        out_shape=[jax.ShapeDtypeStruct((m, rk), F32), jax.ShapeDtypeStruct(w2.shape, F32),
                   jax.ShapeDtypeStruct(b2.shape, F32)],
        compiler_params=_params(("arbitrary",)),
    )(zg, w2, b2, dla)


def _chunk_order(step, n_x_chunks, n_chunks, reverse):
    n_c = n_chunks - n_x_chunks
    if reverse:
        return jnp.where(step < n_c, n_chunks - 1 - step, n_chunks - 1 - step)
    return jnp.where(step < n_c, n_x_chunks + step, step - n_c)


def _tri(reverse, transpose=False):
    i = lax.broadcasted_iota(jnp.int32, (B_CHUNK, B_CHUNK), 0)
    j = lax.broadcasted_iota(jnp.int32, (B_CHUNK, B_CHUNK), 1)
    if transpose:
        i, j = j, i
    return (j >= i) if reverse else (j <= i)


def _gla_chunk(q, k, la, reverse):
    g = _dot_01(_tri(reverse), la)
    last = 0 if reverse else B_CHUNK - 1
    gl = g[last:last + 1, :]
    eg, eng, egl = jnp.exp(g), jnp.exp(-g), jnp.exp(gl - g)
    decay_col = jnp.exp(jnp.sum(la.T, axis=1, keepdims=True))
    return q * (B_DK ** -0.5) * eg, k * eng, k * egl, eg, eng, egl, decay_col


def _head_of(shape, axis, width):
    return lax.broadcasted_iota(jnp.int32, shape, axis) // width


def _gla_chunks_per_step(n_chunks, n_x_chunks):
    return _pick(int(np.gcd(n_chunks - n_x_chunks, n_x_chunks)), (4, 2, 1))


def _gla_fwd(q, k, v, la_f, la_b, n_x, name):
    tc, wk = q.shape
    wv = v.shape[1]
    hh = B_HEADS
    dk, dv = wk // hh, wv // hh
    nc, nxc = tc // B_CHUNK, n_x // B_CHUNK
    sub = _gla_chunks_per_step(nc, nxc)
    rows_per_step = sub * B_CHUNK
    orders = [functools.partial(_chunk_order, n_x_chunks=nxc // sub, n_chunks=nc // sub, reverse=rev)
              for rev in (False, True)]

    def body(*refs):
        ins, outs, s_refs = refs[:8], refs[8:12], refs[12:]

        @pl.when(pl.program_id(0) == 0)
        def _():
            for s_ref in s_refs:
                s_ref[...] = jnp.zeros_like(s_ref)

        lane_head = _head_of((B_CHUNK, wk), 1, dk)
        row_head = _head_of((wk, dv), 0, dk)
        for di, reverse in enumerate((False, True)):
            q_ref, k_ref, v_ref, la_ref = ins[4 * di:4 * di + 4]
            o_ref, s_save_ref = outs[2 * di:2 * di + 2]
            s_prev = s_refs[di][...]
            for c in (reversed(range(sub)) if reverse else range(sub)):
                rows = slice(c * B_CHUNK, (c + 1) * B_CHUNK)
                qt, kt, ke, _, _, _, decay_col = _gla_chunk(q_ref[rows, :], k_ref[rows, :], la_ref[rows, :], reverse)
                ke_t = ke.T
                update = jnp.zeros_like(s_prev)
                for h in range(hh):
                    vv = v_ref[rows, h * dv:(h + 1) * dv]
                    qm = jnp.where(lane_head == h, qt, 0.0)
                    att = jnp.where(_tri(reverse), _bdot(qm, kt, NT), 0.0)
                    o_ref[rows, h * dv:(h + 1) * dv] = _bdot(att, vv) + _bdot(qm, s_prev)
                    update = jnp.where(row_head == h, _bdot(ke_t, vv), update)
                s_save_ref[c] = s_prev
                s_prev = decay_col * s_prev + update
            s_refs[di][...] = s_prev

    def blk(w, order):
        return pl.BlockSpec((rows_per_step, w), lambda s: (order(s), 0))

    def sblk(order):
        return pl.BlockSpec((sub, wk, dv), lambda s: (order(s), 0, 0))

    in_specs, out_specs = [], []
    for order in orders:
        in_specs += [blk(wk, order), blk(wk, order), blk(wv, order), blk(wk, order)]
        out_specs += [blk(wv, order), sblk(order)]
    o_shape, s_shape = jax.ShapeDtypeStruct((tc, wv), F32), jax.ShapeDtypeStruct((nc, wk, dv), F32)
    return pl.pallas_call(
        body, name=name, grid=(nc // sub,), in_specs=in_specs, out_specs=out_specs,
        out_shape=[o_shape, s_shape, o_shape, s_shape],
        scratch_shapes=[pltpu.VMEM((wk, dv), F32)] * 2,
        compiler_params=_params(("arbitrary",)),
    )(q, k, v, la_f, q, k, v, la_b)


def _gla_bwd(q, k, v, la_f, la_b, s_f, s_b, do, n_x, name):
    tc, wk = q.shape
    wv = v.shape[1]
    hh = B_HEADS
    dk, dv = wk // hh, wv // hh
    nc, nxc = tc // B_CHUNK, n_x // B_CHUNK
    sub = _gla_chunks_per_step(nc, nxc)
    rows_per_step = sub * B_CHUNK
    nb, nxb = nc // sub, nxc // sub
    orders = [functools.partial(lambda s, rev: _chunk_order(nb - 1 - s, nxb, nb, rev), rev=rev) for rev in (False, True)]

    def body(*refs):
        ins, outs, ds_refs = refs[:12], refs[12:20], refs[20:]

        @pl.when(pl.program_id(0) == 0)
        def _():
            for ds_ref in ds_refs:
                ds_ref[...] = jnp.zeros_like(ds_ref)

        lane_head = _head_of((B_CHUNK, wk), 1, dk)
        row_head = _head_of((wk, dv), 0, dk)
        for di, reverse in enumerate((False, True)):
            q_ref, k_ref, v_ref, la_ref, s_save_ref, do_ref = ins[6 * di:6 * di + 6]
            dq_ref, dk_ref, dv_ref, dla_ref = outs[4 * di:4 * di + 4]
            mask = _tri(reverse)
            last = 0 if reverse else B_CHUNK - 1
            is_last = lax.broadcasted_iota(jnp.int32, (B_CHUNK, wk), 0) == last
            ds_new = ds_refs[di][...]
            for c in (range(sub) if reverse else reversed(range(sub))):
                rows = slice(c * B_CHUNK, (c + 1) * B_CHUNK)
                la = la_ref[rows, :]
                qt, kt, ke, eg, eng, egl, decay_col = _gla_chunk(q_ref[rows, :], k_ref[rows, :], la, reverse)
                qt_t = qt.T
                s_prev = s_save_ref[c]
                dqt, dkt, dke = jnp.zeros_like(qt), jnp.zeros_like(qt), jnp.zeros_like(qt)
                ds_add = jnp.zeros_like(ds_new)
                for h in range(hh):
                    cols = slice(h * dv, (h + 1) * dv)
                    vv, dout = v_ref[rows, cols], do_ref[rows, cols]
                    mine = lane_head == h
                    qm, km = jnp.where(mine, qt, 0.0), jnp.where(mine, ke, 0.0)
                    att = jnp.where(mask, _bdot(qm, kt, NT), 0.0)
                    datt = jnp.where(mask, _bdot(dout, vv, NT), 0.0)
                    dv_ref[rows, cols] = _bdot(att.T, dout) + _bdot(km, ds_new)
                    dqt = jnp.where(mine, _bdot(datt, kt) + _bdot(dout, s_prev, NT), dqt)
                    dkt = jnp.where(mine, _bdot(datt.T, qt), dkt)
                    dke = jnp.where(mine, _bdot(vv, ds_new, NT), dke)
                    ds_add = jnp.where(row_head == h, _bdot(qt_t, dout), ds_add)
                ddecay_row = jnp.sum((ds_new * s_prev).T, axis=0, keepdims=True)
                decay_row = jnp.exp(jnp.sum(la, axis=0, keepdims=True))
                dq_ref[rows, :] = dqt * (B_DK ** -0.5) * eg
                dk_ref[rows, :] = dkt * eng + dke * egl
                dgl = jnp.sum(dke * ke, axis=0, keepdims=True) + ddecay_row * decay_row
                dg = dqt * qt - dkt * kt - dke * ke + jnp.where(is_last, dgl, 0.0)
                dla_ref[rows, :] = _dot_01(_tri(reverse, transpose=True), dg)
                ds_new = decay_col * ds_new + ds_add
            ds_refs[di][...] = ds_new

    def blk(w, order):
        return pl.BlockSpec((rows_per_step, w), lambda s: (order(s), 0))

    in_specs, out_specs = [], []
    for order in orders:
        in_specs += [blk(wk, order), blk(wk, order), blk(wv, order), blk(wk, order),
                     pl.BlockSpec((sub, wk, dv), lambda s, order=order: (order(s), 0, 0)), blk(wv, order)]
        out_specs += [blk(wk, order), blk(wk, order), blk(wv, order), blk(wk, order)]
    k_shape, v_shape = jax.ShapeDtypeStruct((tc, wk), F32), jax.ShapeDtypeStruct((tc, wv), F32)
    return pl.pallas_call(
        body, name=name, grid=(nb,), in_specs=in_specs, out_specs=out_specs,
        out_shape=[k_shape, k_shape, v_shape, k_shape] * 2,
        scratch_shapes=[pltpu.VMEM((wk, dv), F32)] * 2,
        compiler_params=_params(("arbitrary",)),
    )(q, k, v, la_f, s_f, do, q, k, v, la_b, s_b, do)


def _gla_out_fwd(o_f, o_b, r, g, name):
    t = r.shape[0]
    dv = g.shape[1]
    hh = r.shape[1] // dv
    tb = _pick(t, (256, 128, 64))

    def body(of_ref, ob_ref, r_ref, g_ref, out_ref):
        for h in range(hh):
            cols = slice(h * dv, (h + 1) * dv)
            o = of_ref[:, cols] + ob_ref[:, cols]
            rs = lax.rsqrt(jnp.mean(o * o, axis=-1, keepdims=True) + RMS_EPS)
            out_ref[:, cols] = (o * rs) * g_ref[...] * _silu(r_ref[:, cols])

    rblk = pl.BlockSpec((tb, hh * dv), lambda i: (i, 0))
    return pl.pallas_call(
        body, name=name, grid=(t // tb,), in_specs=[rblk, rblk, rblk, _full(g.shape)], out_specs=rblk,
        out_shape=jax.ShapeDtypeStruct((t, hh * dv), F32), compiler_params=_params(("parallel",)),
    )(o_f, o_b, r, g)


def _gla_out_bwd(o_f, o_b, r, g, dout, name):
    tc = o_f.shape[0]
    t = r.shape[0]
    dv = g.shape[1]
    hh = r.shape[1] // dv
    tb = _pick(int(np.gcd(t, tc)), (256, 128, 64))
    nt = t // tb

    def body(of_ref, ob_ref, r_ref, g_ref, d_ref, do_ref, dr_ref, dg_ref):
        i = pl.program_id(0)

        @pl.when(i == 0)
        def _():
            dg_ref[...] = jnp.zeros_like(dg_ref)

        @pl.when(i >= nt)
        def _():
            do_ref[...] = jnp.zeros_like(do_ref)

        @pl.when(i < nt)
        def _():
            gg = g_ref[...]
            for h in range(hh):
                cols = slice(h * dv, (h + 1) * dv)
                o = of_ref[:, cols] + ob_ref[:, cols]
                rs = lax.rsqrt(jnp.mean(o * o, axis=-1, keepdims=True) + RMS_EPS)
                nz = o * rs
                rr, dd = r_ref[:, cols], d_ref[:, cols]
                sg = jax.nn.sigmoid(rr)
                dr_ref[:, cols] = dd * nz * gg * (sg * (1.0 + rr * (1.0 - sg)))
                dy = dd * (rr * sg)
                dg_ref[...] += jnp.sum(dy * nz, axis=0, keepdims=True)
                dn = dy * gg
                do_ref[:, cols] = rs * (dn - nz * jnp.mean(dn * nz, axis=-1, keepdims=True))

    oblk = pl.BlockSpec((tb, hh * dv), lambda i: (i, 0))
    rblk = pl.BlockSpec((tb, hh * dv), lambda i: (jnp.minimum(i, nt - 1), 0))
    return pl.pallas_call(
        body, name=name, grid=(tc // tb,), in_specs=[oblk, oblk, rblk, _full(g.shape), rblk],
        out_specs=[oblk, rblk, _full(g.shape)],
        out_shape=[jax.ShapeDtypeStruct(o_f.shape, F32), jax.ShapeDtypeStruct(r.shape, F32),
                   jax.ShapeDtypeStruct(g.shape, F32)],
        compiler_params=_params(("arbitrary",)),
    )(o_f, o_b, r, g, dout)


def _pool_window(i, tb, t):
    return pl.multiple_of(jnp.clip(i * tb - POOL_PAD, 0, t - (tb + 2 * POOL_PAD)), 8)


def _pool_band(half, i, tb, start, adjoint):
    pos = i * tb + lax.broadcasted_iota(jnp.int32, (tb, tb + 2 * POOL_PAD), 0)
    tok = start + lax.broadcasted_iota(jnp.int32, (tb, tb + 2 * POOL_PAD), 1)
    if adjoint:
        return (tok > pos - half) & (tok <= pos + half)
    return (tok >= pos - half) & (tok < pos + half)


def _pool_count(pos, half, t):
    return (jnp.minimum(pos + half, t) - jnp.maximum(pos - half, 0)).astype(F32)


def _pool_fwd(h, w_pool, pool_scale, res, mods, km, name):
    t, d = res.shape
    ng, gw = w_pool.shape[0], w_pool.shape[1]
    tb = _pick(t, (256, 128, 64))

    def body(h_ref, w_ref, ps_ref, res_ref, mods_ref, out_ref, pooled_ref, ypre_ref):
        gi, i = pl.program_id(0), pl.program_id(1)
        half = jnp.left_shift(1, gi)
        start = _pool_window(i, tb, t)
        win = h_ref[pl.ds(start, tb + 2 * POOL_PAD), :]
        total = _dot_01(_pool_band(half, i, tb, start, False), win)
        pos = i * tb + lax.broadcasted_iota(jnp.int32, (tb, 1), 0)
        pooled = total / _pool_count(pos, half, t) - h_ref[pl.ds(pl.multiple_of(i * tb, tb), tb), :]
        ypre = _bdot(pooled, w_ref[...])
        pooled_ref[...] = pooled.astype(BF16)
        ypre_ref[...] = ypre
        out_ref[...] = res_ref[...] + mods_ref[0, km:km + 1, :] * (ypre * ps_ref[...])

    tile = pl.BlockSpec((tb, gw), lambda gi, i: (i, gi))
    return pl.pallas_call(
        body, name=name, grid=(ng, t // tb),
        in_specs=[pl.BlockSpec((t, gw), lambda gi, i: (0, gi)),
                  pl.BlockSpec((None, gw, gw), lambda gi, i: (gi, 0, 0)),
                  pl.BlockSpec((1, gw), lambda gi, i: (0, gi)), tile,
                  pl.BlockSpec((2, 16, gw), lambda gi, i: (0, 0, gi))],
        out_specs=[tile, tile, tile],
        out_shape=[jax.ShapeDtypeStruct((t, d), F32), jax.ShapeDtypeStruct((t, d), BF16),
                   jax.ShapeDtypeStruct((t, d), F32)],
        compiler_params=_params(("parallel", "parallel")),
    )(h, w_pool, pool_scale, res, mods)


def _pool_bwd(dxp, w_pool, pool_scale, pooled, ypre, mods, km, name):
    t, d = pooled.shape
    ng, gw = w_pool.shape[0], w_pool.shape[1]
    tb = _pick(t, (256, 128, 64))

    def body(dxp_ref, w_ref, ps_ref, pooled_ref, ypre_ref, mods_ref, dh_ref, dw_ref, acc_ref):
        gi, i = pl.program_id(0), pl.program_id(1)

        @pl.when(i == 0)
        def _():
            dw_ref[...] = jnp.zeros_like(dw_ref)
            acc_ref[...] = jnp.zeros_like(acc_ref)

        half = jnp.left_shift(1, gi)
        mod, ps = mods_ref[0, km:km + 1, :], ps_ref[...]
        start = _pool_window(i, tb, t)
        dwin = dxp_ref[pl.ds(start, tb + 2 * POOL_PAD), :]
        dpooled = _bdot(dwin * (mod * ps), w_ref[...], NT)
        pos = start + lax.broadcasted_iota(jnp.int32, (tb + 2 * POOL_PAD, 1), 0)
        spread = _dot_01(_pool_band(half, i, tb, start, True), dpooled / _pool_count(pos, half, t))
        dxc, yp = dxp_ref[pl.ds(pl.multiple_of(i * tb, tb), tb), :], ypre_ref[...]
        dh_ref[...] = spread - _bdot(dxc * (mod * ps), w_ref[...], NT)
        dw_ref[...] += _bdot(pooled_ref[...].astype(F32).T, dxc * (mod * ps))
        acc_ref[0:1, :] += jnp.sum(dxc * yp * mod, axis=0, keepdims=True)
        acc_ref[1:2, :] += jnp.sum(dxc * yp * ps, axis=0, keepdims=True)

    tile = pl.BlockSpec((tb, gw), lambda gi, i: (i, gi))
    wblk = pl.BlockSpec((None, gw, gw), lambda gi, i: (gi, 0, 0))
    return pl.pallas_call(
        body, name=name, grid=(ng, t // tb),
        in_specs=[pl.BlockSpec((t, gw), lambda gi, i: (0, gi)), wblk,
                  pl.BlockSpec((1, gw), lambda gi, i: (0, gi)), tile, tile,
                  pl.BlockSpec((2, 16, gw), lambda gi, i: (0, 0, gi))],
        out_specs=[tile, wblk, pl.BlockSpec((8, gw), lambda gi, i: (0, gi))],
        out_shape=[jax.ShapeDtypeStruct((t, d), F32), jax.ShapeDtypeStruct(w_pool.shape, F32),
                   jax.ShapeDtypeStruct((8, d), F32)],
        compiler_params=_params(("arbitrary", "arbitrary")),
    )(dxp, w_pool, pool_scale, pooled, ypre, mods)


def _adamw(w, g, m, v, name):
    r, c = w.shape
    tr = _pick(r, (512, 352, 256, 128, 64, 32, 16, 8))
    c1 = 1.0 / (1.0 - ADAM_B1 ** ADAM_STEP)
    c2 = 1.0 / (1.0 - ADAM_B2 ** ADAM_STEP)

    def body(w_ref, g_ref, m_ref, v_ref, d_ref, nm_ref, nv_ref):
        gg = g_ref[...]
        nm = ADAM_B1 * m_ref[...] + (1.0 - ADAM_B1) * gg
        nv = ADAM_B2 * v_ref[...] + (1.0 - ADAM_B2) * (gg * gg)
        nm_ref[...] = nm
        nv_ref[...] = nv
        d_ref[...] = -ADAM_LR * ((nm * c1) / (jnp.sqrt(nv * c2) + ADAM_EPS) + ADAM_WD * w_ref[...])

    blk = pl.BlockSpec((tr, c), lambda i: (i, 0))
    shp = jax.ShapeDtypeStruct((r, c), F32)
    return pl.pallas_call(
        body, name=name, grid=(r // tr,), in_specs=[blk] * 4, out_specs=[blk] * 3, out_shape=[shp] * 3,
        compiler_params=_params(("parallel",)),
    )(w, g, m, v)


def _heads(z, n_heads):
    m = z.shape[0]
    return z.reshape(m, n_heads, -1).transpose(1, 0, 2)


def _unheads(zh):
    return zh.transpose(1, 0, 2).reshape(zh.shape[1], -1)


def _pad_rows(a, n):
    return jnp.pad(a, ((0, 0), (n, n), (0, 0))) if a.ndim == 3 else jnp.pad(a, ((n, n), (0, 0)))


def _local_step(x, ctx, target, mods, wts, fetch, emit):
    t, d = x.shape
    l_ctx = ctx.shape[0]
    tc = t + l_ctx
    norm_g = wts["norm_g"]
    ng = lambda l, k: norm_g[l, k][None, :]
    grads = {}
    dmods = [[[None] * N_MOD for _ in range(2)] for _ in range(2)]
    dnorm = [[None] * 3 for _ in range(2)]

    def ffn_fwd(z, h, l, kbase, wi, wo, n_x, tag, nxt):
        au, act = _ffn_up(h, wi, 0, f"ffn_up_{tag}")
        wo = wo(act) if callable(wo) else wo
        outs = _mm_resid(act, wo, 0, z, mods[l], kbase + 2, 0.5, n_x, f"ffn_down_{tag}", nxt=nxt)
        return outs[0], (z, h, au, act, outs[1], wi, wo), (outs[2] if nxt is not None else None)

    def ffn_bwd(dz_new, dy, saved, l, kbase, g, n_x, tag, stage, split=False, then=None):
        z, h, au, act, y, wi, wo = saved
        dau = _ffn_down_bwd(dy, wo, 0, au, f"ffn_down_bwd_{tag}")
        dwo = _mm_tn(act, dy, BF16, f"dwo_{tag}")
        if split:
            token = emit(stage, [dwo])
            dwi_t = _mm_tn(dau, h, BF16, f"dwi_{tag}", dep=token)
            token = emit(stage + 1, [dwi_t])
        else:
            dwi_t = _mm_tn(dau, h, BF16, f"dwi_{tag}")
            token = emit(stage, [dwi_t, dwo])
        dh = _mm([(dau, wi, 0, 0)], NN, d, F32, f"dh_{tag}", tm_pref=TALL_TILES, dep=token)
        return _modulate_bwd(z, dh, dz_new, mods[l], g, kbase + 1, n_x, f"mod_bwd_{tag}", latent_only=split, then=then)

    def record(l, kbase, k_norm, g, acc_mod, acc_gate, streams):
        total = None
        for s in range(streams):
            dmods[l][s][kbase] = acc_mod[s, 0]
            dmods[l][s][kbase + 1] = acc_mod[s, 1] * g[0]
            if acc_gate is not None:
                dmods[l][s][kbase + 2] = acc_gate[s, 0]
            part = acc_mod[s, 1] * (1.0 + mods[l][s, kbase + 1])
            total = part if total is None else total + part
        dnorm[l][k_norm] = total

    xc0 = _stack_rows(x, ctx, "stack_tokens")
    wi1_0 = fetch(0, None)["wi1_0"]
    h0 = _modulate(xc0, mods[0], ng(0, 0), 0, 1, t, BF16, "mod_l0f1")
    xc1, sv_f1, hc = ffn_fwd(xc0, h0, 0, 0, wi1_0, lambda act: fetch(1, act)["wo1_0"], t, "l0f1",
                             (mods[0], ng(0, 1), 3, 4, BF16))
    w_in_t = fetch(2, hc)["w_in_t"]
    n_proj = w_in_t.shape[1]
    zall = _mm([(hc, w_in_t, 0, 0)], NT, n_proj, F32, "proj", tm_pref=TALL_TILES,
               tn_pref=(n_proj,))
    offs = np.cumsum((0,) + PROJ_SIZES)
    part = lambda i, rows=slice(None): zall[rows, offs[i]:offs[i + 1]]
    lat, con = slice(0, t), slice(t, tc)
    cos, sin = _rope_tables(t)
    qa = _heads(_rope(part(0, lat), cos, sin, False, "rope_q"), A_HEADS)
    ka = _heads(_rope(part(1, lat), cos, sin, False, "rope_k"), A_KV_HEADS)
    va = _heads(part(2, lat), A_KV_HEADS)
    kca, vca = _heads(part(1, con), A_KV_HEADS), _heads(part(2, con), A_KV_HEADS)
    kap, vap = _pad_rows(ka, WINDOW), _pad_rows(va, WINDOW)
    sink = wts["sink"].reshape(A_HEADS, 1, 1)
    o_a = _attn_fwd(qa, kap, vap, kca, vca, sink, "attn_fwd")

    qb, kb, vb = part(3), part(4), part(5)
    rb = part(6, lat)
    zg = part(7)
    zg_f, zg_b = zg[:, :B_GATE_RANK], zg[:, B_GATE_RANK:]
    w2f, w2b, b2f, b2b = wts["w_a2_f"], wts["w_a2_b"], wts["b_a_f"], wts["b_a_b"]
    la_f = _gate_fwd(zg_f, w2f, b2f, "gate_f")
    la_b = _gate_fwd(zg_b, w2b, b2b, "gate_b")
    o_f, s_f, o_b, s_b = _gla_fwd(qb, kb, vb, la_f, la_b, t, "gla_fwd")
    gla_g = wts["gla_g"]
    go = _gla_out_fwd(o_f, o_b, rb, gla_g, "gla_out")
    cat = jnp.concatenate([_unheads(o_a), go], axis=-1).astype(BF16)
    big = fetch(3, cat)
    w_out, wi2_0, wo2_0 = big["w_out"], big["wi2_0"], big["wo2_0"]
    x2, y_mix0, h2 = _mm_resid(cat, w_out, 0, xc1, mods[0], 5, 1.0, t, "w_out", nxt=(mods[0], ng(0, 2), 6, 7, BF16))
    x3, sv_f2, h3 = ffn_fwd(x2, h2, 0, 6, wi2_0, wo2_0, t, "l0f2", (mods[1], ng(1, 0), 0, 1, BF16))

    big = fetch(4, x3)
    wi1_1, wo1_1, wi2_1, wo2_1 = big["wi1_1"], big["wo1_1"], big["wi2_1"], big["wo2_1"]
    x4, sv_g1, hp = ffn_fwd(x3, h3, 1, 0, wi1_1, wo1_1, t, "l1f1", (mods[1], ng(1, 1), 3, 4, F32))
    w_pool, pool_scale = wts["w_pool"], wts["pool_scale"]
    x5, pooled, ypre = _pool_fwd(hp, w_pool, pool_scale, x4, mods[1], 5, "pool_fwd")
    h5 = _modulate(x5, mods[1], ng(1, 2), 6, 7, t, BF16, "mod_l1f2")
    x6, sv_g2, _ = ffn_fwd(x5, h5, 1, 6, wi2_1, wo2_1, t, "l1f2", None)

    y_of = lambda saved: saved[4]
    dx6, loss_vec, dfinal_g, dy, acc_gate = _final_loss(x6, wts["final_g"], target, (y_of(sv_g2), mods[1], 8, 0.5),
                                                        "final_loss")
    grads["final_g"] = dfinal_g[0]

    dx5, acc_mod = ffn_bwd(dx6, dy, sv_g2, 1, 6, ng(1, 2), t, "l1f2", 0)
    record(1, 6, 2, ng(1, 2), acc_mod, acc_gate, 1)
    dhp, dw_pool, acc_pool = _pool_bwd(dx5, w_pool, pool_scale, pooled, ypre, mods[1], 5, "pool_bwd")
    grads["pool_scale"] = acc_pool[0]
    dmods[1][0][5] = acc_pool[1]
    dx4, acc_mod, dy, acc_gate = _modulate_bwd(x4, dhp, dx5, mods[1], ng(1, 1), 4, t, "mod_bwd_l1mix",
                                               then=(y_of(sv_g1), mods[1], 2, 0.5))
    record(1, 3, 1, ng(1, 1), acc_mod, None, 1)
    dx3, acc_mod, dy, acc_gate_next = ffn_bwd(dx4, dy, sv_g1, 1, 0, ng(1, 0), t, "l1f1", 1,
                                              then=(y_of(sv_f2), mods[0], 8, 0.5))
    record(1, 0, 0, ng(1, 0), acc_mod, acc_gate, 1)

    dx2, acc_mod, dymix, acc_gate_mix = ffn_bwd(dx3, dy, sv_f2, 0, 6, ng(0, 2), t, "l0f2", 2,
                                                then=(y_mix0, mods[0], 5, 1.0))
    record(0, 6, 2, ng(0, 2), acc_mod, acc_gate_next, 1)
    dmods[0][0][5] = acc_gate_mix[0, 0]
    dw_out = _mm_tn(cat, dymix, BF16, "dw_out")
    dcat = _mm([(dymix, w_out, 0, 0)], NT, cat.shape[1], F32, "dcat")
    do_a = _heads(dcat[:, :A_Q], A_HEADS)
    do_full, drb, dgla_g = _gla_out_bwd(o_f, o_b, rb, gla_g, dcat[:, A_Q:], "gla_out_bwd")
    grads["gla_g"] = dgla_g[0]
    dq_f, dk_f, dv_f, dla_f, dq_b, dk_b, dv_b, dla_b = _gla_bwd(qb, kb, vb, la_f, la_b, s_f, s_b, do_full, t, "gla_bwd")
    dzg_f, dw2f, db2f = _gate_bwd(zg_f, w2f, b2f, dla_f, "gate_bwd_f")
    dzg_b, dw2b, db2b = _gate_bwd(zg_b, w2b, b2b, dla_b, "gate_bwd_b")
    grads.update(w_a2_f=dw2f, w_a2_b=dw2b, b_a_f=db2f[0], b_a_b=db2b[0])
    dqa_r, dkap, dvap, dkca, dvca, dsink = _attn_bwd(qa, kap, vap, kca, vca, sink, o_a, do_a, "attn_bwd")
    grads["sink"] = dsink[:, 0, 0]
    dqa = _rope(_unheads(dqa_r), cos, sin, True, "rope_bwd_q")
    dka = _rope(_unheads(dkap[:, WINDOW:WINDOW + t]), cos, sin, True, "rope_bwd_k")
    dva = dvap[:, WINDOW:WINDOW + t]
    dzg = jnp.concatenate([dzg_f, dzg_b, jnp.zeros((tc, n_proj - PROJ_DIM), F32)], axis=-1)
    dzall = _assemble_dz(
        [dqa, dka, _unheads(dva), None, None, None, drb, None],
        [None, _unheads(dkca), _unheads(dvca), None, None, None, None, None],
        [None, None, None, [dq_f, dq_b], [dk_f, dk_b], [dv_f, dv_b], None, [dzg]], n_proj, "assemble_dz")
    dw_in_t = _mm_tn(dzall, hc, BF16, "dw_in")
    token = emit(3, [dw_in_t, dw_out, dw_pool])
    dhc = _mm([(dzall, w_in_t, 0, 0)], NN, d, F32, "dhc", tm_pref=TALL_TILES, dep=token)
    dxc1, acc_mod, dy, acc_gate = _modulate_bwd(xc1, dhc, dx2, mods[0], ng(0, 1), 4, t, "mod_bwd_l0mix",
                                                then=(y_of(sv_f1), mods[0], 2, 0.5))
    record(0, 3, 1, ng(0, 1), acc_mod, None, 2)
    dxc0, acc_mod = ffn_bwd(dxc1, dy, sv_f1, 0, 0, ng(0, 0), t, "l0f1", 4, split=True)
    record(0, 0, 0, ng(0, 0), acc_mod, acc_gate, 2)

    grads["norm_g"] = jnp.stack([jnp.stack(dnorm[0]), jnp.stack(dnorm[1])])
    zero = jnp.zeros((d,), F32)
    dmods_arr = jnp.stack([jnp.stack([jnp.stack([v if v is not None else zero for v in dmods[l][s]])
                                      for s in range(2)]) for l in range(2)])
    return loss_vec, dxc0, grads, dmods_arr


def _pack(parts):
    flat = jnp.concatenate([p.reshape(-1).astype(F32) for p in parts])
    pad = (-flat.shape[0]) % 128
    return jnp.pad(flat, (0, pad))[None, :]


def _unpack(rows, shapes):
    out, off = [], 0
    for s in shapes:
        n = int(np.prod(s))
        out.append(rows[:, off:off + n].reshape((rows.shape[0],) + tuple(s)))
        off += n
    return out


def _cols_to_full(g):
    g = jnp.moveaxis(g, 0, -2)
    return g.reshape(g.shape[:-2] + (-1,))


def kernel(x, c, ctx, c_ctx, w_mod, b_mod, norm_g, ffn1_wi, ffn1_wo, ffn2_wi, ffn2_wo, w_in, w_a2_f, b_a_f, w_a2_b, b_a_b, sink, gla_g, w_out, w_pool, pool_scale, final_g, loss_target, m_c_ctx, m_w_mod, m_b_mod, m_norm_g, m_ffn1_wi, m_ffn1_wo, m_ffn2_wi, m_ffn2_wo, m_w_in, m_w_a2_f, m_b_a_f, m_w_a2_b, m_b_a_b, m_sink, m_gla_g, m_w_out, m_w_pool, m_pool_scale, m_final_g, v_c_ctx, v_w_mod, v_b_mod, v_norm_g, v_ffn1_wi, v_ffn1_wo, v_ffn2_wi, v_ffn2_wo, v_w_in, v_w_a2_f, v_b_a_f, v_w_a2_b, v_b_a_b, v_sink, v_gla_g, v_w_out, v_w_pool, v_pool_scale, v_final_g):
    t, d = x.shape[1], x.shape[2]
    me = _dev_index()
    nc = w_mod.shape[2]
    ncol_in = w_in.shape[2]
    ncol_pad = -(-ncol_in // 16) * 16

    small_shapes = [(d,), norm_g.shape, pool_scale.shape, w_a2_f.shape, w_a2_b.shape, w_pool.shape]
    g1 = _gather_small(_pack([c, norm_g, pool_scale, w_a2_f, w_a2_b, w_pool]), "gather_params")
    c_all, norm_g_all, pool_scale_all, w2f_all, w2b_all, w_pool_all = _unpack(g1, small_shapes)
    wts = {
        "norm_g": _cols_to_full(norm_g_all),
        "pool_scale": _cols_to_full(pool_scale_all),
        "w_a2_f": _cols_to_full(w2f_all)[0],
        "w_a2_b": _cols_to_full(w2b_all)[0],
        "w_pool": jnp.moveaxis(w_pool_all[:, 0], 0, 1).reshape(w_pool.shape[1], -1, w_pool.shape[3]),
        "b_a_f": b_a_f, "b_a_b": b_a_b, "sink": sink[0], "gla_g": gla_g, "final_g": final_g[None, :],
    }

    craw = jnp.concatenate([c_all, c_ctx[None, :], jnp.zeros((16 - N_DEV - 1, d), F32)], axis=0)
    b_cols = lax.dynamic_slice_in_dim(b_mod, me * nc, nc, axis=1)[:, None, :]
    mm_cols = _adaln_fwd(craw, w_mod, b_cols, "adaln_fwd")
    g2 = _gather_small(mm_cols.reshape(1, -1), "gather_mods").reshape(N_DEV, 2, 16, nc)
    mm_full = jnp.moveaxis(g2, 0, 2).reshape(2, 16, N_MOD, d)
    mods = jnp.stack([lax.dynamic_index_in_dim(mm_full, me, axis=1, keepdims=False), mm_full[:, N_DEV]], axis=1)
    mods = jnp.pad(mods, ((0, 0), (0, 0), (0, 16 - N_MOD), (0, 0)))

    tr = lambda w: jnp.swapaxes(w, 1, 2).astype(BF16)
    wi1_sh, wi2_sh, wo1_sh, wo2_sh = tr(ffn1_wi), tr(ffn2_wi), ffn1_wo.astype(BF16), ffn2_wo.astype(BF16)
    w_in_sh = jnp.pad(tr(w_in), ((0, 0), (0, ncol_pad - ncol_in), (0, 0)))
    groups = [
        {"wi1_0": wi1_sh[0:1]},
        {"wo1_0": wo1_sh[0:1]},
        {"w_in": w_in_sh},
        {"w_out": w_out.astype(BF16), "wi2_0": wi2_sh[0:1], "wo2_0": wo2_sh[0:1]},
        {"wi1_1": wi1_sh[1:2], "wo1_1": wo1_sh[1:2], "wi2_1": wi2_sh[1:2], "wo2_1": wo2_sh[1:2]},
    ]

    gathers, token = [], mods
    for gi, grp in enumerate(groups):
        lands = [_place_shard(s, me, f"gather_place_{nm}") for nm, s in grp.items()]
        gathers.append(_exchange_start(list(grp.values()), lands, True, 1 + gi, token, f"gather_start_{gi}"))
        token = gathers[-1][4]
    n_proj = -(-(N_DEV * ncol_in) // 128) * 128

    def fetch(gi, after):
        _, lands = _exchange_wait(gathers[gi], True, token if after is None else after, f"gather_wait_{gi}")
        out = dict(zip(groups[gi].keys(), lands))
        if "w_in" in out:
            w_in_t = out.pop("w_in").reshape(1, N_DEV, ncol_pad, d)[:, :, :ncol_in].reshape(1, N_DEV * ncol_in, d)
            out["w_in_t"] = jnp.pad(w_in_t, ((0, 0), (0, n_proj - N_DEV * ncol_in), (0, 0)))
        return out

    scatters = []

    def emit(stage, arrays):
        if stage == 3:
            dw_in_t, dw_out, dw_pool = arrays
            dw_in_full = dw_in_t[:N_DEV * ncol_in].reshape(N_DEV, ncol_in, d)
            dw_in_full = jnp.pad(dw_in_full, ((0, 0), (0, ncol_pad - ncol_in), (0, 0)))
            srcs = [dw_in_full.reshape(1, N_DEV * ncol_pad, d), dw_out[None], dw_pool.astype(BF16)]
        else:
            srcs = [a[None] for a in arrays]
        lands = [lax.empty((N_DEV, s.shape[0], s.shape[1] // N_DEV, s.shape[2]), s.dtype) for s in srcs]
        scatters.append(_exchange_start(srcs, lands, False, 1 + len(groups) + stage, None, f"scatter_start_{stage}"))
        return scatters[-1][4]

    loss_vec, grad_x, grads, dmods = _local_step(x[0], ctx[0], loss_target[0], mods, wts, fetch, emit)
    loss = lax.psum(jnp.sum(loss_vec), ("x", "y", "c"))

    def reduce_stage(stage, after):
        wholes, lands = _exchange_wait(scatters[stage], False, after, f"scatter_wait_{stage}")
        return [_sum_slots(ld, wh, me, f"sum_grad_{stage}_{i}") for i, (ld, wh) in enumerate(zip(lands, wholes))]

    (dwi2_1, dwo2_1), (dwi1_1, dwo1_1), (dwi2_0, dwo2_0), (dw_in_s, dw_out_s, dw_pool_s) = [
        reduce_stage(stage, grad_x) for stage in range(4)]
    back = lambda g: jnp.swapaxes(g, 1, 2)
    g_big = {
        "ffn2_wi": back(jnp.concatenate([dwi2_0, dwi2_1], axis=0)), "ffn2_wo": jnp.concatenate([dwo2_0, dwo2_1], axis=0),
        "w_in": back(dw_in_s[:, :ncol_in]), "w_out": dw_out_s, "w_pool": dw_pool_s[None],
    }

    order = ["c_ctx", "w_mod", "b_mod", "norm_g", "ffn1_wi", "ffn1_wo", "ffn2_wi", "ffn2_wo", "w_in", "w_a2_f", "b_a_f",
             "w_a2_b", "b_a_b", "sink", "gla_g", "w_out", "w_pool", "pool_scale", "final_g"]
    ws = dict(c_ctx=c_ctx, w_mod=w_mod, b_mod=b_mod, norm_g=norm_g, ffn1_wi=ffn1_wi, ffn1_wo=ffn1_wo, ffn2_wi=ffn2_wi,
              ffn2_wo=ffn2_wo, w_in=w_in, w_a2_f=w_a2_f, b_a_f=b_a_f, w_a2_b=w_a2_b, b_a_b=b_a_b, sink=sink, gla_g=gla_g,
              w_out=w_out, w_pool=w_pool, pool_scale=pool_scale, final_g=final_g)
    ms = dict(c_ctx=m_c_ctx, w_mod=m_w_mod, b_mod=m_b_mod, norm_g=m_norm_g, ffn1_wi=m_ffn1_wi, ffn1_wo=m_ffn1_wo,
              ffn2_wi=m_ffn2_wi, ffn2_wo=m_ffn2_wo, w_in=m_w_in, w_a2_f=m_w_a2_f, b_a_f=m_b_a_f, w_a2_b=m_w_a2_b,
              b_a_b=m_b_a_b, sink=m_sink, gla_g=m_gla_g, w_out=m_w_out, w_pool=m_w_pool, pool_scale=m_pool_scale,
              final_g=m_final_g)
    vs = dict(c_ctx=v_c_ctx, w_mod=v_w_mod, b_mod=v_b_mod, norm_g=v_norm_g, ffn1_wi=v_ffn1_wi, ffn1_wo=v_ffn1_wo,
              ffn2_wi=v_ffn2_wi, ffn2_wo=v_ffn2_wo, w_in=v_w_in, w_a2_f=v_w_a2_f, b_a_f=v_b_a_f, w_a2_b=v_w_a2_b,
              b_a_b=v_b_a_b, sink=v_sink, gla_g=v_gla_g, w_out=v_w_out, w_pool=v_w_pool, pool_scale=v_pool_scale,
              final_g=v_final_g)
    early, late = ["ffn2_wi", "ffn2_wo", "w_out", "w_in", "w_pool"], ["ffn1_wi", "ffn1_wo"]
    big = early + ["w_mod"] + late
    delta, new_m, new_v = {}, {}, {}
    g_all = dict(g_big)

    def adamw_big(nm):
        shp = ws[nm].shape
        two_d = lambda a: a.reshape(-1, shp[-1])
        dl, nm_, nv_ = _adamw(two_d(ws[nm]), two_d(g_all[nm]), two_d(ms[nm]), two_d(vs[nm]), f"adamw_{nm}")
        delta[nm], new_m[nm], new_v[nm] = dl.reshape(shp), nm_.reshape(shp), nv_.reshape(shp)

    for nm in early:
        adamw_big(nm)

    small_g = [dmods[:, :, :N_MOD].reshape(2, 2, N_MOD * d), grads["norm_g"], grads["pool_scale"], grads["final_g"],
               grads["b_a_f"], grads["b_a_b"], grads["sink"], grads["gla_g"], grads["w_a2_f"], grads["w_a2_b"]]
    small_g_shapes = [a.shape for a in small_g]
    g3 = _gather_small(_pack(small_g), "gather_small_grads", dep=delta["w_out"])
    total = _sum_rows8(g3, "sum_small_grads")
    dmm_all = _unpack(g3, small_g_shapes[:1])[0]
    (dmm_sum, dnorm_g, dpool_scale, dfinal_g, db_a_f, db_a_b, dsink, dgla_g, dw_a2_f, dw_a2_b) = [
        a[0] for a in _unpack(total, small_g_shapes)]
    dmm_rows = jnp.concatenate([dmm_all[:, :, 0].transpose(1, 0, 2), dmm_sum[:, 1][:, None, :],
                                jnp.zeros((2, 16 - N_DEV - 1, N_MOD * d), F32)], axis=1)
    grad_b_mod = dmm_sum[:, 0] + dmm_sum[:, 1]
    dmm_cols = lax.dynamic_slice_in_dim(dmm_rows, me * nc, nc, axis=2)
    cs_t = jnp.transpose(_silu(craw)).astype(BF16)
    grad_w_mod, dcraw = _adaln_bwd(craw, cs_t, dmm_cols, w_mod, "adaln_bwd")
    g4 = _gather_small((dcraw[0, N_DEV] + dcraw[1, N_DEV])[None, :], "gather_c_ctx_grad")
    grad_c_ctx = _sum_rows8(g4, "sum_c_ctx_grad")[0]

    col = lambda v, n: lax.dynamic_slice_in_dim(v, me * n, n, axis=v.ndim - 1)
    g_small = {
        "c_ctx": grad_c_ctx, "b_mod": grad_b_mod, "norm_g": col(dnorm_g, norm_g.shape[2]),
        "w_a2_f": col(dw_a2_f, w_a2_f.shape[2])[None], "b_a_f": db_a_f[None], "w_a2_b": col(dw_a2_b, w_a2_b.shape[2])[None],
        "b_a_b": db_a_b[None], "sink": dsink[None], "gla_g": dgla_g[None], "pool_scale": col(dpool_scale, pool_scale.shape[1])[None],
        "final_g": dfinal_g,
    }
    g_all.update(g_small, w_mod=grad_w_mod)
    adamw_big("w_mod")
    rest = [nm for nm in order if nm not in big]
    rest_shapes = [ws[nm].shape for nm in rest]
    packed = [_pack([d_[nm].reshape(ws[nm].shape) for nm in rest]).reshape(-1, 128) for d_ in (ws, g_all, ms, vs)]
    pad_rows = (-packed[0].shape[0]) % 512
    packed = [jnp.pad(p, ((0, pad_rows), (0, 0))) for p in packed]
    outs = _adamw(*packed, "adamw_small")
    for dst, arr in zip((delta, new_m, new_v), outs):
        for nm, val in zip(rest, _unpack(arr.reshape(1, -1), rest_shapes)):
            dst[nm] = val[0]

    (dwo1_0,), (dwi1_0,) = reduce_stage(4, outs[0]), reduce_stage(5, outs[0])
    g_all["ffn1_wi"] = back(jnp.concatenate([dwi1_0, dwi1_1], axis=0))
    g_all["ffn1_wo"] = jnp.concatenate([dwo1_0, dwo1_1], axis=0)
    for nm in late:
        adamw_big(nm)
    g_all = {nm: g_all[nm].reshape(ws[nm].shape) for nm in order}

    return (loss, grad_x[None], *[g_all[nm] for nm in order], *[delta[nm] for nm in order],
            *[new_m[nm] for nm in order], *[new_v[nm] for nm in order])
```

```python
import functools

import numpy as np
import jax
import jax.numpy as jnp
from jax import lax
from jax.experimental import pallas as pl
from jax.experimental.pallas import tpu as pltpu

F32 = jnp.float32
BF16 = jnp.bfloat16
MESH = pl.DeviceIdType.MESH

N_DEV = 8
RMS_EPS = 1e-6
N_MOD = 9
GRID_W = 64
A_HEADS, A_KV_HEADS, A_HEAD_DIM = 8, 2, 64
A_REP = A_HEADS // A_KV_HEADS
WINDOW = 128
ROPE_BASE = 10000.0
B_HEADS, B_DK, B_DV = 4, 64, 128
B_GATE_RANK = 16
B_GATE_NORM = 16.0
B_CHUNK = 64
POOL_WINDOWS = (2, 4, 8, 16)
POOL_PAD = 8
A_Q = A_HEADS * A_HEAD_DIM
A_KV = A_KV_HEADS * A_HEAD_DIM
B_QK = B_HEADS * B_DK
B_V = B_HEADS * B_DV
PROJ_SIZES = (A_Q, A_KV, A_KV, B_QK, B_QK, B_V, B_V, 2 * B_GATE_RANK)
PROJ_DIM = sum(PROJ_SIZES)
ADAM_LR, ADAM_B1, ADAM_B2, ADAM_EPS, ADAM_WD, ADAM_STEP = 0.001, 0.9, 0.999, 1e-08, 0.01, 10

VMEM_LIMIT = 56 * 1024 * 1024
ROW_TILES = (512, 544, 256, 128, 64, 32, 16, 8)
TALL_TILES = (1024, 1088) + ROW_TILES

NN = ((1,), (0,))
NT = ((1,), (1,))
TN = ((0,), (0,))


def _dot(a, b, dims=NN, prec=None):
    return lax.dot_general(a, b, (dims, ((), ())), precision=prec, preferred_element_type=F32)


def _bdot(a, b, dims=NN):
    return _dot(a.astype(BF16), b.astype(BF16), dims)


def _dot_01(sel, x):
    hi = x.astype(BF16)
    rest = x - hi.astype(F32)
    mid = rest.astype(BF16)
    lo = (rest - mid.astype(F32)).astype(BF16)
    sel = sel.astype(BF16)
    return _dot(sel, hi) + _dot(sel, mid) + _dot(sel, lo)


def _params(sem=None, **kw):
    return pltpu.CompilerParams(dimension_semantics=sem, vmem_limit_bytes=VMEM_LIMIT, **kw)


def _silu(a):
    return a * jax.nn.sigmoid(a)


def _pick(n, prefs):
    for p in prefs:
        if n % p == 0:
            return p
    return n


def _full(shape):
    nd = len(shape)
    return pl.BlockSpec(shape, lambda *_: (0,) * nd)


def _peers():
    x, y, c = lax.axis_index("x"), lax.axis_index("y"), lax.axis_index("c")
    return x, y, c


def _dev_index():
    x, y, c = _peers()
    return 4 * x + 2 * y + c


def _others(x, y, c):
    return [(x, y, 1 - c), (1 - x, y, c), (x, 1 - y, c), (1 - x, 1 - y, c),
            (1 - x, y, 1 - c), (x, 1 - y, 1 - c), (1 - x, 1 - y, 1 - c)]


def _index_of(dev):
    return 4 * dev[0] + 2 * dev[1] + dev[2]


def _exchange_refs(gather, shapes, srcs, lands, a, me, to):
    if gather:
        r = shapes[a][1]
        return srcs[a], lands[a].at[:, pl.ds(_index_of(me) * r, r), :]
    r = shapes[a][1] // N_DEV
    return srcs[a].at[:, pl.ds(_index_of(to) * r, r), :], lands[a].at[_index_of(me)]


HBM_SPEC = pl.BlockSpec(memory_space=pltpu.HBM)
SEM_SPEC = pl.BlockSpec(memory_space=pltpu.SEMAPHORE)
EFFECT = pltpu.SideEffectType.DATAFLOW_SIDE_EFFECTING


NEAR_PEERS = 4


def _exchange_start(srcs, lands, gather, collective_id, dep, name, n_peers=N_DEV - 1):
    n = len(srcs)
    shapes = [s.shape for s in srcs]
    deps = [] if dep is None else [dep]

    def body(*refs):
        src_refs, land_refs = refs[:n], refs[n:2 * n]
        send_sems, recv_sems = refs[2 * n + len(deps)], refs[2 * n + len(deps) + 1]
        token = refs[-1]
        x, y, c = _peers()
        others = _others(x, y, c)[:n_peers]
        barrier = pltpu.get_barrier_semaphore()
        for peer in others:
            pl.semaphore_signal(barrier, inc=1, device_id=peer, device_id_type=MESH)
        pl.semaphore_wait(barrier, len(others))
        for a in range(n):
            for k, to in enumerate(others):
                src, dst = _exchange_refs(gather, shapes, src_refs, land_refs, a, (x, y, c), to)
                pltpu.make_async_remote_copy(src_ref=src, dst_ref=dst, send_sem=send_sems.at[7 * a + k],
                                             recv_sem=recv_sems.at[7 * a + k], device_id=to, device_id_type=MESH).start()
        token[...] = jnp.zeros_like(token)

    outs = pl.pallas_call(
        body, name=name,
        out_shape=(pltpu.SemaphoreType.DMA((7 * n,)), pltpu.SemaphoreType.DMA((7 * n,)),
                   *[pltpu.HBM(s.shape, s.dtype) for s in srcs], *[pltpu.HBM(l.shape, l.dtype) for l in lands],
                   jax.ShapeDtypeStruct((8, 128), F32)),
        in_specs=[HBM_SPEC] * (2 * n) + [pl.BlockSpec(memory_space=pl.ANY)] * len(deps),
        out_specs=(SEM_SPEC, SEM_SPEC, *[HBM_SPEC] * (2 * n), pl.BlockSpec(memory_space=pltpu.VMEM)),
        input_output_aliases={i: 2 + i for i in range(2 * n)},
        compiler_params=pltpu.CompilerParams(has_side_effects=EFFECT, collective_id=collective_id),
    )(*[pltpu.with_memory_space_constraint(s, pltpu.HBM) for s in srcs],
      *[pltpu.with_memory_space_constraint(l, pltpu.HBM) for l in lands], *deps)
    return outs[0], outs[1], list(outs[2:2 + n]), list(outs[2 + n:2 + 2 * n]), outs[-1]


def _exchange_wait(started, gather, after, name, n_peers=N_DEV - 1):
    send_sems, recv_sems, srcs, lands, _ = started
    n = len(srcs)
    shapes = [s.shape for s in srcs]

    def body(*refs):
        src_refs, land_refs = refs[:n], refs[n:2 * n]
        send_sems, recv_sems = refs[2 * n], refs[2 * n + 1]
        x, y, c = _peers()
        for a in range(n):
            for k, peer in enumerate(_others(x, y, c)[:n_peers]):
                src, _ = _exchange_refs(gather, shapes, src_refs, land_refs, a, (x, y, c), peer)
                _, dst = _exchange_refs(gather, shapes, src_refs, land_refs, a, peer, (x, y, c))
                copy = pltpu.make_async_remote_copy(src_ref=src, dst_ref=dst, send_sem=send_sems.at[7 * a + k],
                                                    recv_sem=recv_sems.at[7 * a + k], device_id=peer, device_id_type=MESH)
                copy.wait_send()
                copy.wait_recv()

    outs = pl.pallas_call(
        body, name=name,
        out_shape=(*[pltpu.HBM(s.shape, s.dtype) for s in srcs], *[pltpu.HBM(l.shape, l.dtype) for l in lands]),
        in_specs=[HBM_SPEC] * (2 * n) + [SEM_SPEC, SEM_SPEC, pl.BlockSpec(memory_space=pl.ANY)],
        out_specs=tuple([HBM_SPEC] * (2 * n)),
        input_output_aliases={i: i for i in range(2 * n)},
        compiler_params=pltpu.CompilerParams(has_side_effects=EFFECT),
    )(*srcs, *lands, send_sems, recv_sems, after)
    return list(outs[:n]), list(outs[n:])


def _forward_refs(land_refs, rows, a, others, j, received):
    origin = others[j + 3] if received else others[j]
    return land_refs[a].at[:, pl.ds(_index_of(origin) * rows[a], rows[a]), :]


def _forward_start(lands, rows, collective_id, name):
    n = len(lands)

    def body(*refs):
        land_refs, send_sems, recv_sems, token = refs[:n], refs[n], refs[n + 1], refs[-1]
        x, y, c = _peers()
        others = _others(x, y, c)
        barrier = pltpu.get_barrier_semaphore()
        pl.semaphore_signal(barrier, inc=1, device_id=others[0], device_id_type=MESH)
        pl.semaphore_wait(barrier, 1)
        for a in range(n):
            for j in (1, 2, 3):
                blk = _forward_refs(land_refs, rows, a, others, j, False)
                pltpu.make_async_remote_copy(src_ref=blk, dst_ref=blk, send_sem=send_sems.at[3 * a + j - 1],
                                             recv_sem=recv_sems.at[3 * a + j - 1], device_id=others[0],
                                             device_id_type=MESH).start()
        token[...] = jnp.zeros_like(token)

    outs = pl.pallas_call(
        body, name=name,
        out_shape=(pltpu.SemaphoreType.DMA((3 * n,)), pltpu.SemaphoreType.DMA((3 * n,)),
                   *[pltpu.HBM(l.shape, l.dtype) for l in lands], jax.ShapeDtypeStruct((8, 128), F32)),
        in_specs=[HBM_SPEC] * n,
        out_specs=(SEM_SPEC, SEM_SPEC, *[HBM_SPEC] * n, pl.BlockSpec(memory_space=pltpu.VMEM)),
        input_output_aliases={i: 2 + i for i in range(n)},
        compiler_params=pltpu.CompilerParams(has_side_effects=EFFECT, collective_id=collective_id),
    )(*[pltpu.with_memory_space_constraint(l, pltpu.HBM) for l in lands])
    return outs[0], outs[1], list(outs[2:2 + n]), outs[-1]


def _forward_wait(started, rows, after, name):
    send_sems, recv_sems, lands, _ = started
    n = len(lands)

    def body(*refs):
        land_refs, send_sems, recv_sems = refs[:n], refs[n], refs[n + 1]
        x, y, c = _peers()
        others = _others(x, y, c)
        for a in range(n):
            for j in (1, 2, 3):
                copy = pltpu.make_async_remote_copy(
                    src_ref=_forward_refs(land_refs, rows, a, others, j, False),
                    dst_ref=_forward_refs(land_refs, rows, a, others, j, True), send_sem=send_sems.at[3 * a + j - 1],
                    recv_sem=recv_sems.at[3 * a + j - 1], device_id=others[0], device_id_type=MESH)
                copy.wait_send()
                copy.wait_recv()

    outs = pl.pallas_call(
        body, name=name, out_shape=tuple(pltpu.HBM(l.shape, l.dtype) for l in lands),
        in_specs=[HBM_SPEC] * n + [SEM_SPEC, SEM_SPEC, pl.BlockSpec(memory_space=pl.ANY)],
        out_specs=tuple([HBM_SPEC] * n), input_output_aliases={i: i for i in range(n)},
        compiler_params=pltpu.CompilerParams(has_side_effects=EFFECT),
    )(*lands, send_sems, recv_sems, after)
    return list(outs)


def _place_shard(shard, me, name):
    a_, r, c = shard.shape
    tr = _pick(r, (352, 304, 256, 128, 64, 32, 16, 8))
    nr = r // tr

    def body(me_ref, in_ref, out_ref):
        out_ref[...] = in_ref[...]

    return pl.pallas_call(
        body, name=name,
        grid_spec=pltpu.PrefetchScalarGridSpec(
            num_scalar_prefetch=1, grid=(a_, nr),
            in_specs=[pl.BlockSpec((None, tr, c), lambda i, j, me_ref: (i, j, 0))],
            out_specs=pl.BlockSpec((None, tr, c), lambda i, j, me_ref: (i, me_ref[0] * nr + j, 0))),
        out_shape=jax.ShapeDtypeStruct((a_, N_DEV * r, c), shard.dtype),
        compiler_params=_params(("parallel", "parallel")),
    )(me.reshape(1).astype(jnp.int32), shard)


def _sum_slots(land, whole, me, name):
    _, a_, r, c = land.shape
    tr = _pick(r, (352, 256, 128, 64, 32, 16, 8))
    nr = r // tr

    def body(me_ref, land_ref, own_ref, out_ref):
        acc = None
        for s in range(N_DEV):
            part = jnp.where(me_ref[0] == s, own_ref[...], land_ref[s]).astype(F32)
            acc = part if acc is None else acc + part
        out_ref[...] = acc

    return pl.pallas_call(
        body, name=name,
        grid_spec=pltpu.PrefetchScalarGridSpec(
            num_scalar_prefetch=1, grid=(a_, nr),
            in_specs=[pl.BlockSpec((N_DEV, None, tr, c), lambda i, j, me_ref: (0, i, j, 0)),
                      pl.BlockSpec((None, tr, c), lambda i, j, me_ref: (i, me_ref[0] * nr + j, 0))],
            out_specs=pl.BlockSpec((None, tr, c), lambda i, j, me_ref: (i, j, 0))),
        out_shape=jax.ShapeDtypeStruct((a_, r, c), F32),
        compiler_params=_params(("parallel", "parallel")),
    )(me.reshape(1).astype(jnp.int32), land, whole)


def _gather_small(vec, name, dep=None):
    p = vec.shape[1]
    pp = -(-p // 1024) * 1024
    blk = jnp.pad(vec, ((0, 0), (0, pp - p))).reshape(8, pp // 8)
    deps = [] if dep is None else [dep]

    def body(in_ref, *rest):
        out_ref, send_sems, recv_sems = rest[-3:]
        x, y, c = _peers()
        me = 4 * x + 2 * y + c
        others = [(x, y, 1 - c), (1 - x, y, c), (x, 1 - y, c), (1 - x, 1 - y, c),
                  (1 - x, y, 1 - c), (x, 1 - y, 1 - c), (1 - x, 1 - y, 1 - c)]

        def rows(idx):
            return out_ref.at[pl.ds(pl.multiple_of(idx * 8, 8), 8), :]

        out_ref[pl.ds(pl.multiple_of(me * 8, 8), 8), :] = in_ref[...]

        def copy(k, dev, slot):
            return pltpu.make_async_remote_copy(
                src_ref=in_ref, dst_ref=rows(slot), send_sem=send_sems.at[k], recv_sem=recv_sems.at[k],
                device_id=dev, device_id_type=MESH)

        sends = [copy(k, dev, me) for k, dev in enumerate(others)]
        for cp in sends:
            cp.start()
        for k, dev in enumerate(others):
            copy(k, dev, 4 * dev[0] + 2 * dev[1] + dev[2]).wait_recv()
        for cp in sends:
            cp.wait_send()

    vm = pl.BlockSpec(memory_space=pltpu.VMEM)
    out = pl.pallas_call(
        body, name=name, out_shape=jax.ShapeDtypeStruct((8 * N_DEV, pp // 8), F32),
        in_specs=[vm] + [pl.BlockSpec(memory_space=pl.ANY)] * len(deps), out_specs=vm,
        scratch_shapes=[pltpu.SemaphoreType.DMA((7,)), pltpu.SemaphoreType.DMA((7,))],
        compiler_params=pltpu.CompilerParams(has_side_effects=True, vmem_limit_bytes=VMEM_LIMIT),
    )(blk, *deps)
    return out.reshape(N_DEV, pp)[:, :p]


def _sum_rows8(g, name):
    p = g.shape[1]

    def body(in_ref, out_ref):
        acc = in_ref[0:1, :]
        for s in range(1, N_DEV):
            acc = acc + in_ref[s:s + 1, :]
        out_ref[...] = acc

    return pl.pallas_call(body, name=name, out_shape=jax.ShapeDtypeStruct((1, p), F32),
                          compiler_params=_params())(g)


def _sel_row(mods_ref, is_ctx, k):
    return jnp.where(is_ctx, mods_ref[1, k:k + 1, :], mods_ref[0, k:k + 1, :])


def _modulate(z, mods, g, ks, kc, n_x, out_dtype, name):
    m, d = z.shape
    tm = _pick(m, (256, 128, 64, 32, 16, 8))

    def body(z_ref, mods_ref, g_ref, h_ref):
        is_ctx = pl.program_id(0) * tm >= n_x
        zz = z_ref[...]
        r = lax.rsqrt(jnp.mean(zz * zz, axis=-1, keepdims=True) + RMS_EPS)
        shift, scale = _sel_row(mods_ref, is_ctx, ks), _sel_row(mods_ref, is_ctx, kc)
        h_ref[...] = ((zz * r) * g_ref[...] * (1.0 + scale) + shift).astype(out_dtype)

    return pl.pallas_call(
        body, name=name, grid=(m // tm,),
        in_specs=[pl.BlockSpec((tm, d), lambda i: (i, 0)), _full(mods.shape), _full(g.shape)],
        out_specs=pl.BlockSpec((tm, d), lambda i: (i, 0)),
        out_shape=jax.ShapeDtypeStruct((m, d), out_dtype),
        compiler_params=_params(("parallel",)),
    )(z, mods, g)


def _gate_bwd_rows(dx, y, gate, coef):
    return (coef * gate * dx).astype(BF16), jnp.sum(coef * y * dx, axis=0, keepdims=True)


def _modulate_bwd(z, dh, dres, mods, g, kc, n_x, name, latent_only=False, then=None):
    m, d = z.shape
    tm = _pick(m, (256, 128, 64, 32, 16, 8))
    first_ctx = n_x // tm
    res_blocks = dres.shape[0] // tm
    out_blocks = (n_x if latent_only else m) // tm
    extra = [] if then is None else [then[0], then[1]]

    def body(z_ref, dh_ref, dres_ref, mods_ref, g_ref, *rest):
        i = pl.program_id(0)
        is_ctx = i * tm >= n_x
        dx_ref, acc_ref = rest[len(extra)], rest[len(extra) + 1]

        @pl.when((i == 0) | (i == first_ctx))
        def _():
            acc_ref[...] = jnp.zeros_like(acc_ref)
            if then is not None:
                rest[-1][...] = jnp.zeros_like(rest[-1])

        zz, dhh = z_ref[...], dh_ref[...]
        r = lax.rsqrt(jnp.mean(zz * zz, axis=-1, keepdims=True) + RMS_EPS)
        nz = zz * r
        gain = g_ref[...] * (1.0 + _sel_row(mods_ref, is_ctx, kc))
        dn = dhh * gain
        dz = r * (dn - nz * jnp.mean(dn * nz, axis=-1, keepdims=True))
        dx = jnp.where(i < res_blocks, dres_ref[...], 0.0) + dz

        @pl.when(i < out_blocks)
        def _():
            dx_ref[...] = dx

        acc_ref[0:1, :] += jnp.sum(dhh, axis=0, keepdims=True)
        acc_ref[1:2, :] += jnp.sum(dhh * nz, axis=0, keepdims=True)
        if then is not None:
            y_ref, tmods_ref, dy_ref, gate_acc_ref = rest[0], rest[1], rest[-2], rest[-1]
            dy, part = _gate_bwd_rows(dx, y_ref[...], _sel_row(tmods_ref, is_ctx, then[2]), then[3])
            dy_ref[...] = dy
            gate_acc_ref[0:1, :] += part

    row = pl.BlockSpec((tm, d), lambda i: (i, 0))
    acc_spec = pl.BlockSpec((None, 8, d), lambda i: ((i * tm >= n_x).astype(jnp.int32), 0, 0))
    out_specs = [pl.BlockSpec((tm, d), lambda i: (jnp.minimum(i, out_blocks - 1), 0)), acc_spec]
    out_shape = [jax.ShapeDtypeStruct((out_blocks * tm, d), F32), jax.ShapeDtypeStruct((2, 8, d), F32)]
    in_specs = [row, row, pl.BlockSpec((tm, d), lambda i: (jnp.minimum(i, res_blocks - 1), 0)),
                _full(mods.shape), _full(g.shape)]
    if then is not None:
        in_specs += [row, _full(then[1].shape)]
        out_specs += [row, acc_spec]
        out_shape += [jax.ShapeDtypeStruct((m, d), BF16), jax.ShapeDtypeStruct((2, 8, d), F32)]
    return pl.pallas_call(
        body, name=name, grid=(m // tm,), in_specs=in_specs, out_specs=out_specs, out_shape=out_shape,
        compiler_params=_params(("arbitrary",)),
    )(z, dh, dres, mods, g, *extra)


def _ffn_up(h, wi_t, layer, name):
    m, d = h.shape
    f = wi_t.shape[1] // 2
    tm = _pick(m, ROW_TILES)

    def body(h_ref, w_ref, jac_ref, act_ref):
        hh = h_ref[...]
        a = _dot(hh, w_ref[0:f, :], NT)
        u = _dot(hh, w_ref[f:2 * f, :], NT)
        sg = jax.nn.sigmoid(a)
        s = a * sg
        jac_ref[:, 0:f] = (u * (sg * (1.0 + a * (1.0 - sg)))).astype(BF16)
        jac_ref[:, f:2 * f] = s.astype(BF16)
        act_ref[...] = (s * u).astype(BF16)

    return pl.pallas_call(
        body, name=name, grid=(m // tm,),
        in_specs=[pl.BlockSpec((tm, d), lambda i: (i, 0)),
                  pl.BlockSpec((None, 2 * f, d), lambda i: (layer, 0, 0))],
        out_specs=[pl.BlockSpec((tm, 2 * f), lambda i: (i, 0)), pl.BlockSpec((tm, f), lambda i: (i, 0))],
        out_shape=[jax.ShapeDtypeStruct((m, 2 * f), BF16), jax.ShapeDtypeStruct((m, f), BF16)],
        compiler_params=_params(("parallel",)),
    )(h, wi_t)


def _mm_resid(a, b, layer, res, mods, km, coef, n_x, name, nxt=None):
    m, k = a.shape
    n = b.shape[2]
    tm = _pick(m, (512, 256, 128, 64, 32, 16, 8))
    tn = n if nxt is not None else _pick(n, (1024, 512, 256, 128))
    extra = [] if nxt is None else [nxt[0], nxt[1]]

    def body(a_ref, b_ref, res_ref, mods_ref, *rest):
        is_ctx = pl.program_id(1) * tm >= n_x
        y = _dot(a_ref[...], b_ref[...])
        new = res_ref[...] + coef * _sel_row(mods_ref, is_ctx, km) * y
        if nxt is None:
            out_ref, y_ref = rest
        else:
            nmods_ref, g_ref, out_ref, y_ref, h_ref = rest
            r = lax.rsqrt(jnp.mean(new * new, axis=-1, keepdims=True) + RMS_EPS)
            shift, scale = _sel_row(nmods_ref, is_ctx, nxt[2]), _sel_row(nmods_ref, is_ctx, nxt[3])
            h_ref[...] = ((new * r) * g_ref[...] * (1.0 + scale) + shift).astype(nxt[4])
        y_ref[...] = y.astype(BF16)
        out_ref[...] = new

    tile = pl.BlockSpec((tm, tn), lambda j, i: (i, j))
    outs = [jax.ShapeDtypeStruct((m, n), F32), jax.ShapeDtypeStruct((m, n), BF16)]
    if nxt is not None:
        outs.append(jax.ShapeDtypeStruct((m, n), nxt[4]))
    return pl.pallas_call(
        body, name=name, grid=(n // tn, m // tm),
        in_specs=[pl.BlockSpec((tm, k), lambda j, i: (i, 0)),
                  pl.BlockSpec((None, k, tn), lambda j, i: (layer, 0, j)),
                  tile, pl.BlockSpec((2, 16, tn), lambda j, i: (0, 0, j))] + [_full(e.shape) for e in extra],
        out_specs=[tile] * len(outs), out_shape=outs,
        compiler_params=_params(("parallel", "parallel")),
    )(a, b, res, mods, *extra)


def _ffn_down_bwd(dy, wo, layer, au, name):
    m, d = dy.shape
    f = wo.shape[1]
    tm = _pick(m, ROW_TILES)

    def body(dy_ref, wo_ref, au_ref, dau_ref):
        dact = _dot(dy_ref[...], wo_ref[...], NT)
        dau_ref[:, 0:f] = (dact * au_ref[:, 0:f].astype(F32)).astype(BF16)
        dau_ref[:, f:2 * f] = (dact * au_ref[:, f:2 * f].astype(F32)).astype(BF16)

    wide = pl.BlockSpec((tm, 2 * f), lambda i: (i, 0))
    return pl.pallas_call(
        body, name=name, grid=(m // tm,),
        in_specs=[pl.BlockSpec((tm, d), lambda i: (i, 0)), pl.BlockSpec((None, f, d), lambda i: (layer, 0, 0)), wide],
        out_specs=wide, out_shape=jax.ShapeDtypeStruct((m, 2 * f), BF16),
        compiler_params=_params(("parallel",)),
    )(dy, wo, au)


def _mm(terms, dims, n, out_dtype, name, tm_pref=(512, 256, 128, 64, 32, 16, 8), tn_pref=(512, 256, 128), dep=None):
    m = terms[0][0].shape[0]
    tm = _pick(m, tm_pref)
    tn = _pick(n, tn_pref)
    nt = len(terms)
    deps = [] if dep is None else [dep]

    def body(*refs):
        out_ref = refs[-1]
        acc = None
        for t in range(nt):
            part = _dot(refs[2 * t][...].astype(BF16), refs[2 * t + 1][...].astype(BF16), dims)
            acc = part if acc is None else acc + part
        out_ref[...] = acc.astype(out_dtype)

    in_specs, args = [], []
    for a, b, layer, rb in terms:
        k = a.shape[1]
        in_specs.append(pl.BlockSpec((tm, k), lambda j, i: (i, 0)))
        if dims == NN:
            in_specs.append(pl.BlockSpec((None, k, tn), lambda j, i, layer=layer, rb=rb: (layer, rb, j)))
        else:
            nb = n // tn
            in_specs.append(pl.BlockSpec((None, tn, k), lambda j, i, layer=layer, rb=rb, nb=nb: (layer, rb * nb + j, 0)))
        args += [a, b]
    return pl.pallas_call(
        body, name=name, grid=(n // tn, m // tm), in_specs=in_specs + [pl.BlockSpec(memory_space=pl.ANY)] * len(deps),
        out_specs=pl.BlockSpec((tm, tn), lambda j, i: (i, j)),
        out_shape=jax.ShapeDtypeStruct((m, n), out_dtype),
        compiler_params=_params(("parallel", "parallel")),
    )(*args, *deps)


def _mm_tn(a, b, out_dtype, name, dep=None):
    t = a.shape[0]
    m, n = a.shape[1], b.shape[1]
    tm = _pick(m, (1408, 2432, 1024, 512, 256, 128))
    tn = _pick(n, (1024, 512, 256, 128))
    tk = _pick(t, TALL_TILES)
    deps = [] if dep is None else [dep]

    def body(a_ref, b_ref, *rest):
        out_ref, acc_ref = rest[-2:]
        kk = pl.program_id(2)

        @pl.when(kk == 0)
        def _():
            acc_ref[...] = jnp.zeros_like(acc_ref)

        acc_ref[...] += _dot(a_ref[...].astype(BF16), b_ref[...].astype(BF16), TN)

        @pl.when(kk == pl.num_programs(2) - 1)
        def _():
            out_ref[...] = acc_ref[...].astype(out_dtype)

    return pl.pallas_call(
        body, name=name, grid=(m // tm, n // tn, t // tk),
        in_specs=[pl.BlockSpec((tk, tm), lambda i, j, k: (k, i)), pl.BlockSpec((tk, tn), lambda i, j, k: (k, j))]
        + [pl.BlockSpec(memory_space=pl.ANY)] * len(deps),
        out_specs=pl.BlockSpec((tm, tn), lambda i, j, k: (i, j)),
        out_shape=jax.ShapeDtypeStruct((m, n), out_dtype),
        scratch_shapes=[pltpu.VMEM((tm, tn), F32)],
        compiler_params=_params(("parallel", "parallel", "arbitrary")),
    )(a, b, *deps)


def _stack_rows(a, b, name):
    ta, d = a.shape
    tm = _pick(int(np.gcd(ta, b.shape[0])), (256, 128, 64, 32, 16, 8))
    na, nb = ta // tm, b.shape[0] // tm

    def body(a_ref, b_ref, o_ref):
        o_ref[...] = jnp.where(pl.program_id(0) < na, a_ref[...], b_ref[...])

    return pl.pallas_call(
        body, name=name, grid=(na + nb,),
        in_specs=[pl.BlockSpec((tm, d), lambda i: (jnp.minimum(i, na - 1), 0)),
                  pl.BlockSpec((tm, d), lambda i: (jnp.maximum(i - na, 0), 0))],
        out_specs=pl.BlockSpec((tm, d), lambda i: (i, 0)),
        out_shape=jax.ShapeDtypeStruct((ta + b.shape[0], d), a.dtype),
        compiler_params=_params(("parallel",)),
    )(a, b)


def _assemble_dz(lat_parts, ctx_parts, both_parts, width, name):
    t = next(p.shape[0] for p in lat_parts if p is not None)
    l_ctx = next(p.shape[0] for p in ctx_parts if p is not None)
    tm = _pick(int(np.gcd(t, l_ctx)), (256, 128, 64, 32, 16, 8))
    nt, nl = t // tm, l_ctx // tm
    plan, args, in_specs, off = [], [], [], 0
    lat_spec = lambda w: pl.BlockSpec((tm, w), lambda i: (jnp.minimum(i, nt - 1), 0))
    ctx_spec = lambda w: pl.BlockSpec((tm, w), lambda i: (jnp.maximum(i - nt, 0), 0))
    all_spec = lambda w: pl.BlockSpec((tm, w), lambda i: (i, 0))
    for lat, ctx, both in zip(lat_parts, ctx_parts, both_parts):
        if both:
            w = both[0].shape[1]
            plan.append(("both", off, w, len(args), len(both)))
            args += both
            in_specs += [all_spec(w)] * len(both)
        else:
            w = (lat if lat is not None else ctx).shape[1]
            plan.append(("split", off, w, len(args), (lat is not None, ctx is not None)))
            for part, spec in ((lat, lat_spec), (ctx, ctx_spec)):
                if part is not None:
                    args.append(part)
                    in_specs.append(spec(w))
        off += w
    n_in = len(args)

    def body(*refs):
        out_ref = refs[n_in]
        is_ctx = pl.program_id(0) >= nt
        for kind, o, w, first, info in plan:
            if kind == "both":
                val = refs[first][...]
                for k in range(1, info):
                    val = val + refs[first + k][...]
            else:
                has_lat, has_ctx = info
                zero = jnp.zeros((tm, w), F32)
                lat = refs[first][...] if has_lat else zero
                ctx = refs[first + int(has_lat)][...] if has_ctx else zero
                val = jnp.where(is_ctx, ctx, lat)
            out_ref[:, o:o + w] = val.astype(BF16)
        if off < width:
            out_ref[:, off:width] = jnp.zeros((tm, width - off), BF16)

    return pl.pallas_call(
        body, name=name, grid=(nt + nl,), in_specs=in_specs,
        out_specs=pl.BlockSpec((tm, width), lambda i: (i, 0)),
        out_shape=jax.ShapeDtypeStruct((t + l_ctx, width), BF16),
        compiler_params=_params(("parallel",)),
    )(*args)


def _final_loss(x, g, target, then, name):
    t, d = x.shape
    tm = _pick(t, (256, 128, 64, 32, 16, 8))
    y, tmods, km, coef = then

    def body(x_ref, g_ref, t_ref, y_ref, tmods_ref, dx_ref, loss_ref, dg_ref, dy_ref, gate_acc_ref):
        @pl.when(pl.program_id(0) == 0)
        def _():
            loss_ref[...] = jnp.zeros_like(loss_ref)
            dg_ref[...] = jnp.zeros_like(dg_ref)
            gate_acc_ref[...] = jnp.zeros_like(gate_acc_ref)

        xx, gg = x_ref[...], g_ref[...]
        r = lax.rsqrt(jnp.mean(xx * xx, axis=-1, keepdims=True) + RMS_EPS)
        nz = xx * r
        err = nz * gg - t_ref[...]
        loss_ref[...] += jnp.sum(err * err, axis=0, keepdims=True) * (0.5 / d)
        dout = err * (1.0 / d)
        dg_ref[...] += jnp.sum(dout * nz, axis=0, keepdims=True)
        dn = dout * gg
        dx = r * (dn - nz * jnp.mean(dn * nz, axis=-1, keepdims=True))
        dx_ref[...] = dx
        dy, part = _gate_bwd_rows(dx, y_ref[...], tmods_ref[0, km:km + 1, :], coef)
        dy_ref[...] = dy
        gate_acc_ref[0:1, :] += part

    row = pl.BlockSpec((tm, d), lambda i: (i, 0))
    vec = pl.BlockSpec((1, d), lambda i: (0, 0))
    acc = pl.BlockSpec((None, 8, d), lambda i: (0, 0, 0))
    return pl.pallas_call(
        body, name=name, grid=(t // tm,), in_specs=[row, vec, row, row, _full(tmods.shape)],
        out_specs=[row, vec, vec, row, acc],
        out_shape=[jax.ShapeDtypeStruct((t, d), F32), jax.ShapeDtypeStruct((1, d), F32),
                   jax.ShapeDtypeStruct((1, d), F32), jax.ShapeDtypeStruct((t, d), BF16),
                   jax.ShapeDtypeStruct((2, 8, d), F32)],
        compiler_params=_params(("arbitrary",)),
    )(x, g, target, y, tmods)


def _adaln_fwd(craw, w_mod, b_cols, name):
    lyr, d, nc = w_mod.shape

    def body(c_ref, w_ref, b_ref, out_ref):
        out_ref[...] = _bdot(_silu(c_ref[...]), w_ref[...]) + b_ref[...]

    return pl.pallas_call(
        body, name=name, grid=(lyr,),
        in_specs=[_full(craw.shape), pl.BlockSpec((None, d, nc), lambda l: (l, 0, 0)),
                  pl.BlockSpec((None, 1, nc), lambda l: (l, 0, 0))],
        out_specs=pl.BlockSpec((None, 16, nc), lambda l: (l, 0, 0)),
        out_shape=jax.ShapeDtypeStruct((lyr, 16, nc), F32),
        compiler_params=_params(("parallel",)),
    )(craw, w_mod, b_cols)


def _adaln_bwd(craw, cs_t, dmm_cols, w_mod, name):
    lyr, d, nc = w_mod.shape

    def body(c_ref, cst_ref, dmm_ref, w_ref, gw_ref, dc_ref):
        dmm = dmm_ref[...]
        gw_ref[...] = _bdot(cst_ref[...], dmm)
        cc = c_ref[...]
        sg = jax.nn.sigmoid(cc)
        dc_ref[...] = _bdot(dmm, w_ref[...], NT) * (sg * (1.0 + cc * (1.0 - sg)))

    wspec = pl.BlockSpec((None, d, nc), lambda l: (l, 0, 0))
    return pl.pallas_call(
        body, name=name, grid=(lyr,),
        in_specs=[_full(craw.shape), _full(cs_t.shape), pl.BlockSpec((None, 16, nc), lambda l: (l, 0, 0)), wspec],
        out_specs=[wspec, pl.BlockSpec((None, 16, d), lambda l: (l, 0, 0))],
        out_shape=[jax.ShapeDtypeStruct((lyr, d, nc), F32), jax.ShapeDtypeStruct((lyr, 16, d), F32)],
        compiler_params=_params(("parallel",)),
    )(craw, cs_t, dmm_cols, w_mod)


def _rope_tables(t):
    rows = np.repeat(np.arange(t // GRID_W, dtype=np.float32), GRID_W)
    cols = np.tile(np.arange(GRID_W, dtype=np.float32), t // GRID_W)
    n = A_HEAD_DIM // 4
    freqs = (ROPE_BASE ** (-np.arange(n, dtype=np.float32) / n)).astype(np.float32)
    ang_r, ang_c = (rows[:, None] * freqs).astype(np.float32), (cols[:, None] * freqs).astype(np.float32)
    cr, sr, cc, sc = np.cos(ang_r), np.sin(ang_r), np.cos(ang_c), np.sin(ang_c)
    cos = np.concatenate([cr, cr, cc, cc] * 2, axis=-1).astype(np.float32)
    sin = np.concatenate([-sr, sr, -sc, sc] * 2, axis=-1).astype(np.float32)
    return jnp.asarray(cos), jnp.asarray(sin)


def _rope(xt, cos, sin, adjoint, name):
    t, w = xt.shape
    tb = _pick(t, (512, 256, 128))
    rep = w // cos.shape[1]

    def body(x_ref, c_ref, s_ref, o_ref):
        xx = x_ref[...]
        cc = jnp.concatenate([c_ref[...]] * rep, axis=1) if rep > 1 else c_ref[...]
        ss = jnp.concatenate([s_ref[...]] * rep, axis=1) if rep > 1 else s_ref[...]
        low = (lax.broadcasted_iota(jnp.int32, xx.shape, 1) % 32) < 16

        def partner(v):
            return jnp.where(low, pltpu.roll(v, w - 16, 1), pltpu.roll(v, 16, 1))

        if adjoint:
            o_ref[...] = xx * cc + partner(xx * ss)
        else:
            o_ref[...] = xx * cc + partner(xx) * ss

    blk = pl.BlockSpec((tb, w), lambda i: (i, 0))
    tab = pl.BlockSpec((tb, cos.shape[1]), lambda i: (i, 0))
    return pl.pallas_call(
        body, name=name, grid=(t // tb,), in_specs=[blk, tab, tab], out_specs=blk,
        out_shape=jax.ShapeDtypeStruct((t, w), F32), compiler_params=_params(("parallel",)),
    )(xt, cos, sin)


def _attn_probs(q, kb, kc, sink, n, t):
    scale = A_HEAD_DIM ** -0.5
    s1 = _bdot(q, kb, NT) * scale
    s2 = _bdot(q, kc, NT) * scale
    qpos = n * WINDOW + lax.broadcasted_iota(jnp.int32, s1.shape, 0) % WINDOW
    kpos = (n - 1) * WINDOW + lax.broadcasted_iota(jnp.int32, s1.shape, 1)
    valid = (kpos >= 0) & (kpos < t) & (jnp.abs(kpos - qpos) <= WINDOW)
    s1 = jnp.where(valid, s1, -jnp.inf)
    mx = jnp.maximum(jnp.maximum(jnp.max(s1, axis=-1, keepdims=True), jnp.max(s2, axis=-1, keepdims=True)), sink)
    p1, p2, ps = jnp.exp(s1 - mx), jnp.exp(s2 - mx), jnp.exp(sink - mx)
    inv = 1.0 / (jnp.sum(p1, axis=-1, keepdims=True) + jnp.sum(p2, axis=-1, keepdims=True) + ps)
    return p1 * inv, p2 * inv, ps * inv


def _sink_rows(sink_ref):
    return jnp.concatenate([jnp.broadcast_to(sink_ref[r], (WINDOW, 1)) for r in range(A_REP)], axis=0)


def _attn_fwd(q, kp, vp, kc, vc, sink, name):
    hq, t, dh = q.shape
    nb = t // WINDOW
    lc = kc.shape[1]
    rows = A_REP * WINDOW

    def body(q_ref, k_ref, v_ref, kc_ref, vc_ref, sink_ref, o_ref):
        n = pl.program_id(1)
        start = pl.multiple_of(n * WINDOW, WINDOW)
        kb, vb = k_ref[pl.ds(start, 3 * WINDOW), :], v_ref[pl.ds(start, 3 * WINDOW), :]
        p1, p2, _ = _attn_probs(q_ref[...].reshape(rows, dh), kb, kc_ref[...], _sink_rows(sink_ref), n, t)
        o_ref[...] = (_bdot(p1, vb) + _bdot(p2, vc_ref[...])).reshape(A_REP, WINDOW, dh)

    qblk = pl.BlockSpec((A_REP, WINDOW, dh), lambda g, n: (g, n, 0))
    kfull = pl.BlockSpec((None, t + 2 * WINDOW, dh), lambda g, n: (g, 0, 0))
    cfull = pl.BlockSpec((None, lc, dh), lambda g, n: (g, 0, 0))
    return pl.pallas_call(
        body, name=name, grid=(hq // A_REP, nb),
        in_specs=[qblk, kfull, kfull, cfull, cfull, pl.BlockSpec((A_REP, 1, 1), lambda g, n: (g, 0, 0))],
        out_specs=qblk, out_shape=jax.ShapeDtypeStruct((hq, t, dh), F32),
        compiler_params=_params(("parallel", "parallel")),
    )(q, kp, vp, kc, vc, sink)


def _attn_bwd(q, kp, vp, kc, vc, sink, o, do, name):
    hq, t, dh = q.shape
    nb = t // WINDOW
    lc = kc.shape[1]
    scale = A_HEAD_DIM ** -0.5
    rows = A_REP * WINDOW

    def body(q_ref, k_ref, v_ref, kc_ref, vc_ref, sink_ref, o_ref, do_ref,
             dq_ref, dk_ref, dv_ref, dkc_ref, dvc_ref, dsink_ref):
        n = pl.program_id(1)

        @pl.when(n == 0)
        def _():
            dk_ref[...] = jnp.zeros_like(dk_ref)
            dv_ref[...] = jnp.zeros_like(dv_ref)
            dkc_ref[...] = jnp.zeros_like(dkc_ref)
            dvc_ref[...] = jnp.zeros_like(dvc_ref)
            dsink_ref[...] = jnp.zeros_like(dsink_ref)

        start = pl.multiple_of(n * WINDOW, WINDOW)
        band = pl.ds(start, 3 * WINDOW)
        qq, kb, vb, kcc, vcc = q_ref[...].reshape(rows, dh), k_ref[band, :], v_ref[band, :], kc_ref[...], vc_ref[...]
        p1, p2, ps = _attn_probs(qq, kb, kcc, _sink_rows(sink_ref), n, t)
        dout = do_ref[...].reshape(rows, dh)
        delta = jnp.sum(dout * o_ref[...].reshape(rows, dh), axis=-1, keepdims=True)
        ds1 = p1 * (_bdot(dout, vb, NT) - delta)
        ds2 = p2 * (_bdot(dout, vcc, NT) - delta)
        dq_ref[...] = ((_bdot(ds1, kb) + _bdot(ds2, kcc)) * scale).reshape(A_REP, WINDOW, dh)
        dk_ref[band, :] += _bdot(ds1.T, qq) * scale
        dv_ref[band, :] += _bdot(p1.T, dout)
        dkc_ref[...] += _bdot(ds2.T, qq) * scale
        dvc_ref[...] += _bdot(p2.T, dout)
        dsink_ref[...] += jnp.sum((-ps * delta).reshape(A_REP, WINDOW, 1), axis=1, keepdims=True)

    qblk = pl.BlockSpec((A_REP, WINDOW, dh), lambda g, n: (g, n, 0))
    kfull = pl.BlockSpec((None, t + 2 * WINDOW, dh), lambda g, n: (g, 0, 0))
    cfull = pl.BlockSpec((None, lc, dh), lambda g, n: (g, 0, 0))
    return pl.pallas_call(
        body, name=name, grid=(hq // A_REP, nb),
        in_specs=[qblk, kfull, kfull, cfull, cfull, pl.BlockSpec((A_REP, 1, 1), lambda g, n: (g, 0, 0)), qblk, qblk],
        out_specs=[qblk, kfull, kfull, cfull, cfull, pl.BlockSpec((A_REP, 8, 128), lambda g, n: (g, 0, 0))],
        out_shape=[jax.ShapeDtypeStruct(q.shape, F32), jax.ShapeDtypeStruct(kp.shape, F32),
                   jax.ShapeDtypeStruct(kp.shape, F32), jax.ShapeDtypeStruct(kc.shape, F32),
                   jax.ShapeDtypeStruct(kc.shape, F32), jax.ShapeDtypeStruct((hq, 8, 128), F32)],
        compiler_params=_params(("parallel", "arbitrary")),
    )(q, kp, vp, kc, vc, sink, o, do)


def _gate_fwd(zg, w2, b2, name):
    m = zg.shape[0]
    n = w2.shape[1]
    tm = _pick(m, (512, 256, 128, 64, 32, 16, 8))

    def body(z_ref, w_ref, b_ref, o_ref):
        o_ref[...] = jax.nn.log_sigmoid(_bdot(z_ref[...], w_ref[...]) + b_ref[...]) / B_GATE_NORM

    return pl.pallas_call(
        body, name=name, grid=(m // tm,),
        in_specs=[pl.BlockSpec((tm, zg.shape[1]), lambda i: (i, 0)), _full(w2.shape), _full(b2.shape)],
        out_specs=pl.BlockSpec((tm, n), lambda i: (i, 0)), out_shape=jax.ShapeDtypeStruct((m, n), F32),
        compiler_params=_params(("parallel",)),
    )(zg, w2, b2)


def _gate_bwd(zg, w2, b2, dla, name):
    m, rk = zg.shape
    n = w2.shape[1]
    tm = _pick(m, (512, 256, 128, 64, 32, 16, 8))

    def body(z_ref, w_ref, b_ref, d_ref, dz_ref, dw_ref, db_ref):
        @pl.when(pl.program_id(0) == 0)
        def _():
            dw_ref[...] = jnp.zeros_like(dw_ref)
            db_ref[...] = jnp.zeros_like(db_ref)

        zz, ww = z_ref[...], w_ref[...]
        pre = _bdot(zz, ww) + b_ref[...]
        dpre = d_ref[...] * (1.0 / B_GATE_NORM) * jax.nn.sigmoid(-pre)
        dz_ref[...] = _bdot(dpre, ww, NT)
        dw_ref[...] += _bdot(zz.T, dpre)
        db_ref[...] += jnp.sum(dpre, axis=0, keepdims=True)

    return pl.pallas_call(
        body, name=name, grid=(m // tm,),
        in_specs=[pl.BlockSpec((tm, rk), lambda i: (i, 0)), _full(w2.shape), _full(b2.shape),
                  pl.BlockSpec((tm, n), lambda i: (i, 0))],
        out_specs=[pl.BlockSpec((tm, rk), lambda i: (i, 0)), _full(w2.shape), _full(b2.shape)],
        out_shape=[jax.ShapeDtypeStruct((m, rk), F32), jax.ShapeDtypeStruct(w2.shape, F32),
                   jax.ShapeDtypeStruct(b2.shape, F32)],
        compiler_params=_params(("arbitrary",)),
    )(zg, w2, b2, dla)


def _chunk_order(step, n_x_chunks, n_chunks, reverse):
    n_c = n_chunks - n_x_chunks
    if reverse:
        return jnp.where(step < n_c, n_chunks - 1 - step, n_chunks - 1 - step)
    return jnp.where(step < n_c, n_x_chunks + step, step - n_c)


def _tri(reverse, transpose=False):
    i = lax.broadcasted_iota(jnp.int32, (B_CHUNK, B_CHUNK), 0)
    j = lax.broadcasted_iota(jnp.int32, (B_CHUNK, B_CHUNK), 1)
    if transpose:
        i, j = j, i
    return (j >= i) if reverse else (j <= i)


def _gla_chunk(q, k, la, reverse):
    g = _dot_01(_tri(reverse), la)
    last = 0 if reverse else B_CHUNK - 1
    gl = g[last:last + 1, :]
    eg, eng, egl = jnp.exp(g), jnp.exp(-g), jnp.exp(gl - g)
    decay_col = jnp.exp(jnp.sum(la.T, axis=1, keepdims=True))
    return q * (B_DK ** -0.5) * eg, k * eng, k * egl, eg, eng, egl, decay_col


def _head_of(shape, axis, width):
    return lax.broadcasted_iota(jnp.int32, shape, axis) // width


def _gla_chunks_per_step(n_chunks, n_x_chunks):
    return _pick(int(np.gcd(n_chunks - n_x_chunks, n_x_chunks)), (4, 2, 1))


def _gla_fwd(q, k, v, la_f, la_b, n_x, name):
    tc, wk = q.shape
    wv = v.shape[1]
    hh = B_HEADS
    dk, dv = wk // hh, wv // hh
    nc, nxc = tc // B_CHUNK, n_x // B_CHUNK
    sub = _gla_chunks_per_step(nc, nxc)
    rows_per_step = sub * B_CHUNK
    orders = [functools.partial(_chunk_order, n_x_chunks=nxc // sub, n_chunks=nc // sub, reverse=rev)
              for rev in (False, True)]

    def body(*refs):
        ins, outs, s_refs = refs[:8], refs[8:12], refs[12:]

        @pl.when(pl.program_id(0) == 0)
        def _():
            for s_ref in s_refs:
                s_ref[...] = jnp.zeros_like(s_ref)

        lane_head = _head_of((B_CHUNK, wk), 1, dk)
        row_head = _head_of((wk, dv), 0, dk)
        for di, reverse in enumerate((False, True)):
            q_ref, k_ref, v_ref, la_ref = ins[4 * di:4 * di + 4]
            o_ref, s_save_ref = outs[2 * di:2 * di + 2]
            s_prev = s_refs[di][...]
            for c in (reversed(range(sub)) if reverse else range(sub)):
                rows = slice(c * B_CHUNK, (c + 1) * B_CHUNK)
                qt, kt, ke, _, _, _, decay_col = _gla_chunk(q_ref[rows, :], k_ref[rows, :], la_ref[rows, :], reverse)
                ke_t = ke.T
                update = jnp.zeros_like(s_prev)
                for h in range(hh):
                    vv = v_ref[rows, h * dv:(h + 1) * dv]
                    qm = jnp.where(lane_head == h, qt, 0.0)
                    att = jnp.where(_tri(reverse), _bdot(qm, kt, NT), 0.0)
                    o_ref[rows, h * dv:(h + 1) * dv] = _bdot(att, vv) + _bdot(qm, s_prev)
                    update = jnp.where(row_head == h, _bdot(ke_t, vv), update)
                s_save_ref[c] = s_prev
                s_prev = decay_col * s_prev + update
            s_refs[di][...] = s_prev

    def blk(w, order):
        return pl.BlockSpec((rows_per_step, w), lambda s: (order(s), 0))

    def sblk(order):
        return pl.BlockSpec((sub, wk, dv), lambda s: (order(s), 0, 0))

    in_specs, out_specs = [], []
    for order in orders:
        in_specs += [blk(wk, order), blk(wk, order), blk(wv, order), blk(wk, order)]
        out_specs += [blk(wv, order), sblk(order)]
    o_shape, s_shape = jax.ShapeDtypeStruct((tc, wv), F32), jax.ShapeDtypeStruct((nc, wk, dv), F32)
    return pl.pallas_call(
        body, name=name, grid=(nc // sub,), in_specs=in_specs, out_specs=out_specs,
        out_shape=[o_shape, s_shape, o_shape, s_shape],
        scratch_shapes=[pltpu.VMEM((wk, dv), F32)] * 2,
        compiler_params=_params(("arbitrary",)),
    )(q, k, v, la_f, q, k, v, la_b)


def _gla_bwd(q, k, v, la_f, la_b, s_f, s_b, do, n_x, name):
    tc, wk = q.shape
    wv = v.shape[1]
    hh = B_HEADS
    dk, dv = wk // hh, wv // hh
    nc, nxc = tc // B_CHUNK, n_x // B_CHUNK
    sub = _gla_chunks_per_step(nc, nxc)
    rows_per_step = sub * B_CHUNK
    nb, nxb = nc // sub, nxc // sub
    orders = [functools.partial(lambda s, rev: _chunk_order(nb - 1 - s, nxb, nb, rev), rev=rev) for rev in (False, True)]

    def body(*refs):
        ins, outs, ds_refs = refs[:12], refs[12:20], refs[20:]

        @pl.when(pl.program_id(0) == 0)
        def _():
            for ds_ref in ds_refs:
                ds_ref[...] = jnp.zeros_like(ds_ref)

        lane_head = _head_of((B_CHUNK, wk), 1, dk)
        row_head = _head_of((wk, dv), 0, dk)
        for di, reverse in enumerate((False, True)):
            q_ref, k_ref, v_ref, la_ref, s_save_ref, do_ref = ins[6 * di:6 * di + 6]
            dq_ref, dk_ref, dv_ref, dla_ref = outs[4 * di:4 * di + 4]
            mask = _tri(reverse)
            last = 0 if reverse else B_CHUNK - 1
            is_last = lax.broadcasted_iota(jnp.int32, (B_CHUNK, wk), 0) == last
            ds_new = ds_refs[di][...]
            for c in (range(sub) if reverse else reversed(range(sub))):
                rows = slice(c * B_CHUNK, (c + 1) * B_CHUNK)
                la = la_ref[rows, :]
                qt, kt, ke, eg, eng, egl, decay_col = _gla_chunk(q_ref[rows, :], k_ref[rows, :], la, reverse)
                qt_t = qt.T
                s_prev = s_save_ref[c]
                dqt, dkt, dke = jnp.zeros_like(qt), jnp.zeros_like(qt), jnp.zeros_like(qt)
                ds_add = jnp.zeros_like(ds_new)
                for h in range(hh):
                    cols = slice(h * dv, (h + 1) * dv)
                    vv, dout = v_ref[rows, cols], do_ref[rows, cols]
                    mine = lane_head == h
                    qm, km = jnp.where(mine, qt, 0.0), jnp.where(mine, ke, 0.0)
                    att = jnp.where(mask, _bdot(qm, kt, NT), 0.0)
                    datt = jnp.where(mask, _bdot(dout, vv, NT), 0.0)
                    dv_ref[rows, cols] = _bdot(att.T, dout) + _bdot(km, ds_new)
                    dqt = jnp.where(mine, _bdot(datt, kt) + _bdot(dout, s_prev, NT), dqt)
                    dkt = jnp.where(mine, _bdot(datt.T, qt), dkt)
                    dke = jnp.where(mine, _bdot(vv, ds_new, NT), dke)
                    ds_add = jnp.where(row_head == h, _bdot(qt_t, dout), ds_add)
                ddecay_row = jnp.sum((ds_new * s_prev).T, axis=0, keepdims=True)
                decay_row = jnp.exp(jnp.sum(la, axis=0, keepdims=True))
                dq_ref[rows, :] = dqt * (B_DK ** -0.5) * eg
                dk_ref[rows, :] = dkt * eng + dke * egl
                dgl = jnp.sum(dke * ke, axis=0, keepdims=True) + ddecay_row * decay_row
                dg = dqt * qt - dkt * kt - dke * ke + jnp.where(is_last, dgl, 0.0)
                dla_ref[rows, :] = _dot_01(_tri(reverse, transpose=True), dg)
                ds_new = decay_col * ds_new + ds_add
            ds_refs[di][...] = ds_new

    def blk(w, order):
        return pl.BlockSpec((rows_per_step, w), lambda s: (order(s), 0))

    in_specs, out_specs = [], []
    for order in orders:
        in_specs += [blk(wk, order), blk(wk, order), blk(wv, order), blk(wk, order),
                     pl.BlockSpec((sub, wk, dv), lambda s, order=order: (order(s), 0, 0)), blk(wv, order)]
        out_specs += [blk(wk, order), blk(wk, order), blk(wv, order), blk(wk, order)]
    k_shape, v_shape = jax.ShapeDtypeStruct((tc, wk), F32), jax.ShapeDtypeStruct((tc, wv), F32)
    return pl.pallas_call(
        body, name=name, grid=(nb,), in_specs=in_specs, out_specs=out_specs,
        out_shape=[k_shape, k_shape, v_shape, k_shape] * 2,
        scratch_shapes=[pltpu.VMEM((wk, dv), F32)] * 2,
        compiler_params=_params(("arbitrary",)),
    )(q, k, v, la_f, s_f, do, q, k, v, la_b, s_b, do)


def _gla_out_fwd(o_f, o_b, r, g, name):
    t = r.shape[0]
    dv = g.shape[1]
    hh = r.shape[1] // dv
    tb = _pick(t, (256, 128, 64))

    def body(of_ref, ob_ref, r_ref, g_ref, out_ref):
        for h in range(hh):
            cols = slice(h * dv, (h + 1) * dv)
            o = of_ref[:, cols] + ob_ref[:, cols]
            rs = lax.rsqrt(jnp.mean(o * o, axis=-1, keepdims=True) + RMS_EPS)
            out_ref[:, cols] = (o * rs) * g_ref[...] * _silu(r_ref[:, cols])

    rblk = pl.BlockSpec((tb, hh * dv), lambda i: (i, 0))
    return pl.pallas_call(
        body, name=name, grid=(t // tb,), in_specs=[rblk, rblk, rblk, _full(g.shape)], out_specs=rblk,
        out_shape=jax.ShapeDtypeStruct((t, hh * dv), F32), compiler_params=_params(("parallel",)),
    )(o_f, o_b, r, g)


def _gla_out_bwd(o_f, o_b, r, g, dout, name):
    tc = o_f.shape[0]
    t = r.shape[0]
    dv = g.shape[1]
    hh = r.shape[1] // dv
    tb = _pick(int(np.gcd(t, tc)), (256, 128, 64))
    nt = t // tb

    def body(of_ref, ob_ref, r_ref, g_ref, d_ref, do_ref, dr_ref, dg_ref):
        i = pl.program_id(0)

        @pl.when(i == 0)
        def _():
            dg_ref[...] = jnp.zeros_like(dg_ref)

        @pl.when(i >= nt)
        def _():
            do_ref[...] = jnp.zeros_like(do_ref)

        @pl.when(i < nt)
        def _():
            gg = g_ref[...]
            for h in range(hh):
                cols = slice(h * dv, (h + 1) * dv)
                o = of_ref[:, cols] + ob_ref[:, cols]
                rs = lax.rsqrt(jnp.mean(o * o, axis=-1, keepdims=True) + RMS_EPS)
                nz = o * rs
                rr, dd = r_ref[:, cols], d_ref[:, cols]
                sg = jax.nn.sigmoid(rr)
                dr_ref[:, cols] = dd * nz * gg * (sg * (1.0 + rr * (1.0 - sg)))
                dy = dd * (rr * sg)
                dg_ref[...] += jnp.sum(dy * nz, axis=0, keepdims=True)
                dn = dy * gg
                do_ref[:, cols] = rs * (dn - nz * jnp.mean(dn * nz, axis=-1, keepdims=True))

    oblk = pl.BlockSpec((tb, hh * dv), lambda i: (i, 0))
    rblk = pl.BlockSpec((tb, hh * dv), lambda i: (jnp.minimum(i, nt - 1), 0))
    return pl.pallas_call(
        body, name=name, grid=(tc // tb,), in_specs=[oblk, oblk, rblk, _full(g.shape), rblk],
        out_specs=[oblk, rblk, _full(g.shape)],
        out_shape=[jax.ShapeDtypeStruct(o_f.shape, F32), jax.ShapeDtypeStruct(r.shape, F32),
                   jax.ShapeDtypeStruct(g.shape, F32)],
        compiler_params=_params(("arbitrary",)),
    )(o_f, o_b, r, g, dout)


def _pool_window(i, tb, t):
    return pl.multiple_of(jnp.clip(i * tb - POOL_PAD, 0, t - (tb + 2 * POOL_PAD)), 8)


def _pool_band(half, i, tb, start, adjoint):
    pos = i * tb + lax.broadcasted_iota(jnp.int32, (tb, tb + 2 * POOL_PAD), 0)
    tok = start + lax.broadcasted_iota(jnp.int32, (tb, tb + 2 * POOL_PAD), 1)
    if adjoint:
        return (tok > pos - half) & (tok <= pos + half)
    return (tok >= pos - half) & (tok < pos + half)


def _pool_count(pos, half, t):
    return (jnp.minimum(pos + half, t) - jnp.maximum(pos - half, 0)).astype(F32)


def _pool_fwd(h, w_pool, pool_scale, res, mods, km, name):
    t, d = res.shape
    ng, gw = w_pool.shape[0], w_pool.shape[1]
    tb = _pick(t, (256, 128, 64))

    def body(h_ref, w_ref, ps_ref, res_ref, mods_ref, out_ref, pooled_ref, ypre_ref):
        gi, i = pl.program_id(0), pl.program_id(1)
        half = jnp.left_shift(1, gi)
        start = _pool_window(i, tb, t)
        win = h_ref[pl.ds(start, tb + 2 * POOL_PAD), :]
        total = _dot_01(_pool_band(half, i, tb, start, False), win)
        pos = i * tb + lax.broadcasted_iota(jnp.int32, (tb, 1), 0)
        pooled = total / _pool_count(pos, half, t) - h_ref[pl.ds(pl.multiple_of(i * tb, tb), tb), :]
        ypre = _bdot(pooled, w_ref[...])
        pooled_ref[...] = pooled.astype(BF16)
        ypre_ref[...] = ypre
        out_ref[...] = res_ref[...] + mods_ref[0, km:km + 1, :] * (ypre * ps_ref[...])

    tile = pl.BlockSpec((tb, gw), lambda gi, i: (i, gi))
    return pl.pallas_call(
        body, name=name, grid=(ng, t // tb),
        in_specs=[pl.BlockSpec((t, gw), lambda gi, i: (0, gi)),
                  pl.BlockSpec((None, gw, gw), lambda gi, i: (gi, 0, 0)),
                  pl.BlockSpec((1, gw), lambda gi, i: (0, gi)), tile,
                  pl.BlockSpec((2, 16, gw), lambda gi, i: (0, 0, gi))],
        out_specs=[tile, tile, tile],
        out_shape=[jax.ShapeDtypeStruct((t, d), F32), jax.ShapeDtypeStruct((t, d), BF16),
                   jax.ShapeDtypeStruct((t, d), F32)],
        compiler_params=_params(("parallel", "parallel")),
    )(h, w_pool, pool_scale, res, mods)


def _pool_bwd(dxp, w_pool, pool_scale, pooled, ypre, mods, km, name):
    t, d = pooled.shape
    ng, gw = w_pool.shape[0], w_pool.shape[1]
    tb = _pick(t, (256, 128, 64))

    def body(dxp_ref, w_ref, ps_ref, pooled_ref, ypre_ref, mods_ref, dh_ref, dw_ref, acc_ref):
        gi, i = pl.program_id(0), pl.program_id(1)

        @pl.when(i == 0)
        def _():
            dw_ref[...] = jnp.zeros_like(dw_ref)
            acc_ref[...] = jnp.zeros_like(acc_ref)

        half = jnp.left_shift(1, gi)
        mod, ps = mods_ref[0, km:km + 1, :], ps_ref[...]
        start = _pool_window(i, tb, t)
        dwin = dxp_ref[pl.ds(start, tb + 2 * POOL_PAD), :]
        dpooled = _bdot(dwin * (mod * ps), w_ref[...], NT)
        pos = start + lax.broadcasted_iota(jnp.int32, (tb + 2 * POOL_PAD, 1), 0)
        spread = _dot_01(_pool_band(half, i, tb, start, True), dpooled / _pool_count(pos, half, t))
        dxc, yp = dxp_ref[pl.ds(pl.multiple_of(i * tb, tb), tb), :], ypre_ref[...]
        dh_ref[...] = spread - _bdot(dxc * (mod * ps), w_ref[...], NT)
        dw_ref[...] += _bdot(pooled_ref[...].astype(F32).T, dxc * (mod * ps))
        acc_ref[0:1, :] += jnp.sum(dxc * yp * mod, axis=0, keepdims=True)
        acc_ref[1:2, :] += jnp.sum(dxc * yp * ps, axis=0, keepdims=True)

    tile = pl.BlockSpec((tb, gw), lambda gi, i: (i, gi))
    wblk = pl.BlockSpec((None, gw, gw), lambda gi, i: (gi, 0, 0))
    return pl.pallas_call(
        body, name=name, grid=(ng, t // tb),
        in_specs=[pl.BlockSpec((t, gw), lambda gi, i: (0, gi)), wblk,
                  pl.BlockSpec((1, gw), lambda gi, i: (0, gi)), tile, tile,
                  pl.BlockSpec((2, 16, gw), lambda gi, i: (0, 0, gi))],
        out_specs=[tile, wblk, pl.BlockSpec((8, gw), lambda gi, i: (0, gi))],
        out_shape=[jax.ShapeDtypeStruct((t, d), F32), jax.ShapeDtypeStruct(w_pool.shape, F32),
                   jax.ShapeDtypeStruct((8, d), F32)],
        compiler_params=_params(("arbitrary", "arbitrary")),
    )(dxp, w_pool, pool_scale, pooled, ypre, mods)


def _adamw(w, g, m, v, name):
    r, c = w.shape
    tr = _pick(r, (512, 352, 256, 128, 64, 32, 16, 8))
    c1 = 1.0 / (1.0 - ADAM_B1 ** ADAM_STEP)
    c2 = 1.0 / (1.0 - ADAM_B2 ** ADAM_STEP)

    def body(w_ref, g_ref, m_ref, v_ref, d_ref, nm_ref, nv_ref):
        gg = g_ref[...]
        nm = ADAM_B1 * m_ref[...] + (1.0 - ADAM_B1) * gg
        nv = ADAM_B2 * v_ref[...] + (1.0 - ADAM_B2) * (gg * gg)
        nm_ref[...] = nm
        nv_ref[...] = nv
        d_ref[...] = -ADAM_LR * ((nm * c1) / (jnp.sqrt(nv * c2) + ADAM_EPS) + ADAM_WD * w_ref[...])

    blk = pl.BlockSpec((tr, c), lambda i: (i, 0))
    shp = jax.ShapeDtypeStruct((r, c), F32)
    return pl.pallas_call(
        body, name=name, grid=(r // tr,), in_specs=[blk] * 4, out_specs=[blk] * 3, out_shape=[shp] * 3,
        compiler_params=_params(("parallel",)),
    )(w, g, m, v)


def _heads(z, n_heads):
    m = z.shape[0]
    return z.reshape(m, n_heads, -1).transpose(1, 0, 2)


def _unheads(zh):
    return zh.transpose(1, 0, 2).reshape(zh.shape[1], -1)


def _pad_rows(a, n):
    return jnp.pad(a, ((0, 0), (n, n), (0, 0))) if a.ndim == 3 else jnp.pad(a, ((n, n), (0, 0)))


def _local_step(x, ctx, target, mods, wts, fetch, emit):
    t, d = x.shape
    l_ctx = ctx.shape[0]
    tc = t + l_ctx
    norm_g = wts["norm_g"]
    ng = lambda l, k: norm_g[l, k][None, :]
    grads = {}
    dmods = [[[None] * N_MOD for _ in range(2)] for _ in range(2)]
    dnorm = [[None] * 3 for _ in range(2)]

    def ffn_fwd(z, h, l, kbase, wi, wo, n_x, tag, nxt):
        au, act = _ffn_up(h, wi, 0, f"ffn_up_{tag}")
        wo = wo(act) if callable(wo) else wo
        outs = _mm_resid(act, wo, 0, z, mods[l], kbase + 2, 0.5, n_x, f"ffn_down_{tag}", nxt=nxt)
        return outs[0], (z, h, au, act, outs[1], wi, wo), (outs[2] if nxt is not None else None)

    def ffn_bwd(dz_new, dy, saved, l, kbase, g, n_x, tag, stage, split=False, then=None):
        z, h, au, act, y, wi, wo = saved
        dau = _ffn_down_bwd(dy, wo, 0, au, f"ffn_down_bwd_{tag}")
        dwo = _mm_tn(act, dy, BF16, f"dwo_{tag}")
        if split:
            token = emit(stage, [dwo])
            dwi_t = _mm_tn(dau, h, BF16, f"dwi_{tag}", dep=token)
            token = emit(stage + 1, [dwi_t])
        else:
            dwi_t = _mm_tn(dau, h, BF16, f"dwi_{tag}")
            token = emit(stage, [dwi_t, dwo])
        dh = _mm([(dau, wi, 0, 0)], NN, d, F32, f"dh_{tag}", tm_pref=TALL_TILES, dep=token)
        return _modulate_bwd(z, dh, dz_new, mods[l], g, kbase + 1, n_x, f"mod_bwd_{tag}", latent_only=split, then=then)

    def record(l, kbase, k_norm, g, acc_mod, acc_gate, streams):
        total = None
        for s in range(streams):
            dmods[l][s][kbase] = acc_mod[s, 0]
            dmods[l][s][kbase + 1] = acc_mod[s, 1] * g[0]
            if acc_gate is not None:
                dmods[l][s][kbase + 2] = acc_gate[s, 0]
            part = acc_mod[s, 1] * (1.0 + mods[l][s, kbase + 1])
            total = part if total is None else total + part
        dnorm[l][k_norm] = total

    xc0 = _stack_rows(x, ctx, "stack_tokens")
    wi1_0 = fetch(0, None)["wi1_0"]
    h0 = _modulate(xc0, mods[0], ng(0, 0), 0, 1, t, BF16, "mod_l0f1")
    xc1, sv_f1, hc = ffn_fwd(xc0, h0, 0, 0, wi1_0, lambda act: fetch(1, act)["wo1_0"], t, "l0f1",
                             (mods[0], ng(0, 1), 3, 4, BF16))
    w_in_t = fetch(2, hc)["w_in_t"]
    n_proj = w_in_t.shape[1]
    zall = _mm([(hc, w_in_t, 0, 0)], NT, n_proj, F32, "proj", tm_pref=TALL_TILES,
               tn_pref=(n_proj,))
    offs = np.cumsum((0,) + PROJ_SIZES)
    part = lambda i, rows=slice(None): zall[rows, offs[i]:offs[i + 1]]
    lat, con = slice(0, t), slice(t, tc)
    cos, sin = _rope_tables(t)
    qa = _heads(_rope(part(0, lat), cos, sin, False, "rope_q"), A_HEADS)
    ka = _heads(_rope(part(1, lat), cos, sin, False, "rope_k"), A_KV_HEADS)
    va = _heads(part(2, lat), A_KV_HEADS)
    kca, vca = _heads(part(1, con), A_KV_HEADS), _heads(part(2, con), A_KV_HEADS)
    kap, vap = _pad_rows(ka, WINDOW), _pad_rows(va, WINDOW)
    sink = wts["sink"].reshape(A_HEADS, 1, 1)
    o_a = _attn_fwd(qa, kap, vap, kca, vca, sink, "attn_fwd")

    qb, kb, vb = part(3), part(4), part(5)
    rb = part(6, lat)
    zg = part(7)
    zg_f, zg_b = zg[:, :B_GATE_RANK], zg[:, B_GATE_RANK:]
    w2f, w2b, b2f, b2b = wts["w_a2_f"], wts["w_a2_b"], wts["b_a_f"], wts["b_a_b"]
    la_f = _gate_fwd(zg_f, w2f, b2f, "gate_f")
    la_b = _gate_fwd(zg_b, w2b, b2b, "gate_b")
    o_f, s_f, o_b, s_b = _gla_fwd(qb, kb, vb, la_f, la_b, t, "gla_fwd")
    gla_g = wts["gla_g"]
    go = _gla_out_fwd(o_f, o_b, rb, gla_g, "gla_out")
    cat = jnp.concatenate([_unheads(o_a), go], axis=-1).astype(BF16)
    big = fetch(3, cat)
    w_out, wi2_0, wo2_0 = big["w_out"], big["wi2_0"], big["wo2_0"]
    x2, y_mix0, h2 = _mm_resid(cat, w_out, 0, xc1, mods[0], 5, 1.0, t, "w_out", nxt=(mods[0], ng(0, 2), 6, 7, BF16))
    x3, sv_f2, h3 = ffn_fwd(x2, h2, 0, 6, wi2_0, wo2_0, t, "l0f2", (mods[1], ng(1, 0), 0, 1, BF16))

    big = fetch(4, x3)
    wi1_1, wo1_1, wi2_1, wo2_1 = big["wi1_1"], big["wo1_1"], big["wi2_1"], big["wo2_1"]
    x4, sv_g1, hp = ffn_fwd(x3, h3, 1, 0, wi1_1, wo1_1, t, "l1f1", (mods[1], ng(1, 1), 3, 4, F32))
    w_pool, pool_scale = wts["w_pool"], wts["pool_scale"]
    x5, pooled, ypre = _pool_fwd(hp, w_pool, pool_scale, x4, mods[1], 5, "pool_fwd")
    h5 = _modulate(x5, mods[1], ng(1, 2), 6, 7, t, BF16, "mod_l1f2")
    x6, sv_g2, _ = ffn_fwd(x5, h5, 1, 6, wi2_1, wo2_1, t, "l1f2", None)

    y_of = lambda saved: saved[4]
    dx6, loss_vec, dfinal_g, dy, acc_gate = _final_loss(x6, wts["final_g"], target, (y_of(sv_g2), mods[1], 8, 0.5),
                                                        "final_loss")
    grads["final_g"] = dfinal_g[0]

    dx5, acc_mod = ffn_bwd(dx6, dy, sv_g2, 1, 6, ng(1, 2), t, "l1f2", 0)
    record(1, 6, 2, ng(1, 2), acc_mod, acc_gate, 1)
    dhp, dw_pool, acc_pool = _pool_bwd(dx5, w_pool, pool_scale, pooled, ypre, mods[1], 5, "pool_bwd")
    grads["pool_scale"] = acc_pool[0]
    dmods[1][0][5] = acc_pool[1]
    dx4, acc_mod, dy, acc_gate = _modulate_bwd(x4, dhp, dx5, mods[1], ng(1, 1), 4, t, "mod_bwd_l1mix",
                                               then=(y_of(sv_g1), mods[1], 2, 0.5))
    record(1, 3, 1, ng(1, 1), acc_mod, None, 1)
    dx3, acc_mod, dy, acc_gate_next = ffn_bwd(dx4, dy, sv_g1, 1, 0, ng(1, 0), t, "l1f1", 1,
                                              then=(y_of(sv_f2), mods[0], 8, 0.5))
    record(1, 0, 0, ng(1, 0), acc_mod, acc_gate, 1)

    dx2, acc_mod, dymix, acc_gate_mix = ffn_bwd(dx3, dy, sv_f2, 0, 6, ng(0, 2), t, "l0f2", 2,
                                                then=(y_mix0, mods[0], 5, 1.0))
    record(0, 6, 2, ng(0, 2), acc_mod, acc_gate_next, 1)
    dmods[0][0][5] = acc_gate_mix[0, 0]
    dw_out = _mm_tn(cat, dymix, BF16, "dw_out")
    dcat = _mm([(dymix, w_out, 0, 0)], NT, cat.shape[1], F32, "dcat")
    do_a = _heads(dcat[:, :A_Q], A_HEADS)
    do_full, drb, dgla_g = _gla_out_bwd(o_f, o_b, rb, gla_g, dcat[:, A_Q:], "gla_out_bwd")
    grads["gla_g"] = dgla_g[0]
    dq_f, dk_f, dv_f, dla_f, dq_b, dk_b, dv_b, dla_b = _gla_bwd(qb, kb, vb, la_f, la_b, s_f, s_b, do_full, t, "gla_bwd")
    dzg_f, dw2f, db2f = _gate_bwd(zg_f, w2f, b2f, dla_f, "gate_bwd_f")
    dzg_b, dw2b, db2b = _gate_bwd(zg_b, w2b, b2b, dla_b, "gate_bwd_b")
    grads.update(w_a2_f=dw2f, w_a2_b=dw2b, b_a_f=db2f[0], b_a_b=db2b[0])
    dqa_r, dkap, dvap, dkca, dvca, dsink = _attn_bwd(qa, kap, vap, kca, vca, sink, o_a, do_a, "attn_bwd")
    grads["sink"] = dsink[:, 0, 0]
    dqa = _rope(_unheads(dqa_r), cos, sin, True, "rope_bwd_q")
    dka = _rope(_unheads(dkap[:, WINDOW:WINDOW + t]), cos, sin, True, "rope_bwd_k")
    dva = dvap[:, WINDOW:WINDOW + t]
    dzg = jnp.concatenate([dzg_f, dzg_b, jnp.zeros((tc, n_proj - PROJ_DIM), F32)], axis=-1)
    dzall = _assemble_dz(
        [dqa, dka, _unheads(dva), None, None, None, drb, None],
        [None, _unheads(dkca), _unheads(dvca), None, None, None, None, None],
        [None, None, None, [dq_f, dq_b], [dk_f, dk_b], [dv_f, dv_b], None, [dzg]], n_proj, "assemble_dz")
    dw_in_t = _mm_tn(dzall, hc, BF16, "dw_in")
    token = emit(3, [dw_in_t, dw_out, dw_pool])
    dhc = _mm([(dzall, w_in_t, 0, 0)], NN, d, F32, "dhc", tm_pref=TALL_TILES, dep=token)
    dxc1, acc_mod, dy, acc_gate = _modulate_bwd(xc1, dhc, dx2, mods[0], ng(0, 1), 4, t, "mod_bwd_l0mix",
                                                then=(y_of(sv_f1), mods[0], 2, 0.5))
    record(0, 3, 1, ng(0, 1), acc_mod, None, 2)
    dxc0, acc_mod = ffn_bwd(dxc1, dy, sv_f1, 0, 0, ng(0, 0), t, "l0f1", 4, split=True)
    record(0, 0, 0, ng(0, 0), acc_mod, acc_gate, 2)

    grads["norm_g"] = jnp.stack([jnp.stack(dnorm[0]), jnp.stack(dnorm[1])])
    zero = jnp.zeros((d,), F32)
    dmods_arr = jnp.stack([jnp.stack([jnp.stack([v if v is not None else zero for v in dmods[l][s]])
                                      for s in range(2)]) for l in range(2)])
    return loss_vec, dxc0, grads, dmods_arr


def _pack(parts):
    flat = jnp.concatenate([p.reshape(-1).astype(F32) for p in parts])
    pad = (-flat.shape[0]) % 128
    return jnp.pad(flat, (0, pad))[None, :]


def _unpack(rows, shapes):
    out, off = [], 0
    for s in shapes:
        n = int(np.prod(s))
        out.append(rows[:, off:off + n].reshape((rows.shape[0],) + tuple(s)))
        off += n
    return out


def _cols_to_full(g):
    g = jnp.moveaxis(g, 0, -2)
    return g.reshape(g.shape[:-2] + (-1,))


def kernel(x, c, ctx, c_ctx, w_mod, b_mod, norm_g, ffn1_wi, ffn1_wo, ffn2_wi, ffn2_wo, w_in, w_a2_f, b_a_f, w_a2_b, b_a_b, sink, gla_g, w_out, w_pool, pool_scale, final_g, loss_target, m_c_ctx, m_w_mod, m_b_mod, m_norm_g, m_ffn1_wi, m_ffn1_wo, m_ffn2_wi, m_ffn2_wo, m_w_in, m_w_a2_f, m_b_a_f, m_w_a2_b, m_b_a_b, m_sink, m_gla_g, m_w_out, m_w_pool, m_pool_scale, m_final_g, v_c_ctx, v_w_mod, v_b_mod, v_norm_g, v_ffn1_wi, v_ffn1_wo, v_ffn2_wi, v_ffn2_wo, v_w_in, v_w_a2_f, v_b_a_f, v_w_a2_b, v_b_a_b, v_sink, v_gla_g, v_w_out, v_w_pool, v_pool_scale, v_final_g):
    t, d = x.shape[1], x.shape[2]
    me = _dev_index()
    nc = w_mod.shape[2]
    ncol_in = w_in.shape[2]
    ncol_pad = -(-ncol_in // 16) * 16

    small_shapes = [(d,), norm_g.shape, pool_scale.shape, w_a2_f.shape, w_a2_b.shape, w_pool.shape]
    g1 = _gather_small(_pack([c, norm_g, pool_scale, w_a2_f, w_a2_b, w_pool]), "gather_params")
    c_all, norm_g_all, pool_scale_all, w2f_all, w2b_all, w_pool_all = _unpack(g1, small_shapes)
    wts = {
        "norm_g": _cols_to_full(norm_g_all),
        "pool_scale": _cols_to_full(pool_scale_all),
        "w_a2_f": _cols_to_full(w2f_all)[0],
        "w_a2_b": _cols_to_full(w2b_all)[0],
        "w_pool": jnp.moveaxis(w_pool_all[:, 0], 0, 1).reshape(w_pool.shape[1], -1, w_pool.shape[3]),
        "b_a_f": b_a_f, "b_a_b": b_a_b, "sink": sink[0], "gla_g": gla_g, "final_g": final_g[None, :],
    }

    craw = jnp.concatenate([c_all, c_ctx[None, :], jnp.zeros((16 - N_DEV - 1, d), F32)], axis=0)
    b_cols = lax.dynamic_slice_in_dim(b_mod, me * nc, nc, axis=1)[:, None, :]
    mm_cols = _adaln_fwd(craw, w_mod, b_cols, "adaln_fwd")
    g2 = _gather_small(mm_cols.reshape(1, -1), "gather_mods").reshape(N_DEV, 2, 16, nc)
    mm_full = jnp.moveaxis(g2, 0, 2).reshape(2, 16, N_MOD, d)
    mods = jnp.stack([lax.dynamic_index_in_dim(mm_full, me, axis=1, keepdims=False), mm_full[:, N_DEV]], axis=1)
    mods = jnp.pad(mods, ((0, 0), (0, 0), (0, 16 - N_MOD), (0, 0)))

    tr = lambda w: jnp.swapaxes(w, 1, 2).astype(BF16)
    wi1_sh, wi2_sh, wo1_sh, wo2_sh = tr(ffn1_wi), tr(ffn2_wi), ffn1_wo.astype(BF16), ffn2_wo.astype(BF16)
    w_in_sh = jnp.pad(tr(w_in), ((0, 0), (0, ncol_pad - ncol_in), (0, 0)))
    groups = [
        {"wi1_0": wi1_sh[0:1]},
        {"wo1_0": wo1_sh[0:1]},
        {"w_in": w_in_sh},
        {"w_out": w_out.astype(BF16), "wi2_0": wi2_sh[0:1], "wo2_0": wo2_sh[0:1]},
        {"wi1_1": wi1_sh[1:2], "wo1_1": wo1_sh[1:2], "wi2_1": wi2_sh[1:2], "wo2_1": wo2_sh[1:2]},
    ]

    reach = lambda gi: NEAR_PEERS if gi == 0 else N_DEV - 1
    gathers, token = [], mods
    for gi, grp in enumerate(groups):
        lands = [_place_shard(s, me, f"gather_place_{nm}") for nm, s in grp.items()]
        gathers.append(_exchange_start(list(grp.values()), lands, True, 1 + gi, token, f"gather_start_{gi}",
                                       n_peers=reach(gi)))
        token = gathers[-1][4]
    n_proj = -(-(N_DEV * ncol_in) // 128) * 128
    forward_id = 1 + len(groups) + 6

    def fetch(gi, after):
        _, lands = _exchange_wait(gathers[gi], True, token if after is None else after, f"gather_wait_{gi}",
                                  n_peers=reach(gi))
        if gi == 0:
            rows = [s.shape[1] for s in groups[gi].values()]
            passed = _forward_start(lands, rows, forward_id, "gather_forward")
            lands = _forward_wait(passed, rows, passed[3], "gather_forward_wait")
        out = dict(zip(groups[gi].keys(), lands))
        if "w_in" in out:
            w_in_t = out.pop("w_in").reshape(1, N_DEV, ncol_pad, d)[:, :, :ncol_in].reshape(1, N_DEV * ncol_in, d)
            out["w_in_t"] = jnp.pad(w_in_t, ((0, 0), (0, n_proj - N_DEV * ncol_in), (0, 0)))
        return out

    scatters = []

    def emit(stage, arrays):
        if stage == 3:
            dw_in_t, dw_out, dw_pool = arrays
            dw_in_full = dw_in_t[:N_DEV * ncol_in].reshape(N_DEV, ncol_in, d)
            dw_in_full = jnp.pad(dw_in_full, ((0, 0), (0, ncol_pad - ncol_in), (0, 0)))
            srcs = [dw_in_full.reshape(1, N_DEV * ncol_pad, d), dw_out[None], dw_pool.astype(BF16)]
        else:
            srcs = [a[None] for a in arrays]
        lands = [lax.empty((N_DEV, s.shape[0], s.shape[1] // N_DEV, s.shape[2]), s.dtype) for s in srcs]
        scatters.append(_exchange_start(srcs, lands, False, 1 + len(groups) + stage, None, f"scatter_start_{stage}"))
        return scatters[-1][4]

    loss_vec, grad_x, grads, dmods = _local_step(x[0], ctx[0], loss_target[0], mods, wts, fetch, emit)
    loss = lax.psum(jnp.sum(loss_vec), ("x", "y", "c"))

    def reduce_stage(stage, after):
        wholes, lands = _exchange_wait(scatters[stage], False, after, f"scatter_wait_{stage}")
        return [_sum_slots(ld, wh, me, f"sum_grad_{stage}_{i}") for i, (ld, wh) in enumerate(zip(lands, wholes))]

    (dwi2_1, dwo2_1), (dwi1_1, dwo1_1), (dwi2_0, dwo2_0), (dw_in_s, dw_out_s, dw_pool_s) = [
        reduce_stage(stage, grad_x) for stage in range(4)]
    back = lambda g: jnp.swapaxes(g, 1, 2)
    g_big = {
        "ffn2_wi": back(jnp.concatenate([dwi2_0, dwi2_1], axis=0)), "ffn2_wo": jnp.concatenate([dwo2_0, dwo2_1], axis=0),
        "w_in": back(dw_in_s[:, :ncol_in]), "w_out": dw_out_s, "w_pool": dw_pool_s[None],
    }

    order = ["c_ctx", "w_mod", "b_mod", "norm_g", "ffn1_wi", "ffn1_wo", "ffn2_wi", "ffn2_wo", "w_in", "w_a2_f", "b_a_f",
             "w_a2_b", "b_a_b", "sink", "gla_g", "w_out", "w_pool", "pool_scale", "final_g"]
    ws = dict(c_ctx=c_ctx, w_mod=w_mod, b_mod=b_mod, norm_g=norm_g, ffn1_wi=ffn1_wi, ffn1_wo=ffn1_wo, ffn2_wi=ffn2_wi,
              ffn2_wo=ffn2_wo, w_in=w_in, w_a2_f=w_a2_f, b_a_f=b_a_f, w_a2_b=w_a2_b, b_a_b=b_a_b, sink=sink, gla_g=gla_g,
              w_out=w_out, w_pool=w_pool, pool_scale=pool_scale, final_g=final_g)
    ms = dict(c_ctx=m_c_ctx, w_mod=m_w_mod, b_mod=m_b_mod, norm_g=m_norm_g, ffn1_wi=m_ffn1_wi, ffn1_wo=m_ffn1_wo,
              ffn2_wi=m_ffn2_wi, ffn2_wo=m_ffn2_wo, w_in=m_w_in, w_a2_f=m_w_a2_f, b_a_f=m_b_a_f, w_a2_b=m_w_a2_b,
              b_a_b=m_b_a_b, sink=m_sink, gla_g=m_gla_g, w_out=m_w_out, w_pool=m_w_pool, pool_scale=m_pool_scale,
              final_g=m_final_g)
    vs = dict(c_ctx=v_c_ctx, w_mod=v_w_mod, b_mod=v_b_mod, norm_g=v_norm_g, ffn1_wi=v_ffn1_wi, ffn1_wo=v_ffn1_wo,
              ffn2_wi=v_ffn2_wi, ffn2_wo=v_ffn2_wo, w_in=v_w_in, w_a2_f=v_w_a2_f, b_a_f=v_b_a_f, w_a2_b=v_w_a2_b,
              b_a_b=v_b_a_b, sink=v_sink, gla_g=v_gla_g, w_out=v_w_out, w_pool=v_w_pool, pool_scale=v_pool_scale,
              final_g=v_final_g)
    early, late = ["ffn2_wi", "ffn2_wo", "w_out", "w_in", "w_pool"], ["ffn1_wi", "ffn1_wo"]
    big = early + ["w_mod"] + late
    delta, new_m, new_v = {}, {}, {}
    g_all = dict(g_big)

    def adamw_big(nm):
        shp = ws[nm].shape
        two_d = lambda a: a.reshape(-1, shp[-1])
        dl, nm_, nv_ = _adamw(two_d(ws[nm]), two_d(g_all[nm]), two_d(ms[nm]), two_d(vs[nm]), f"adamw_{nm}")
        delta[nm], new_m[nm], new_v[nm] = dl.reshape(shp), nm_.reshape(shp), nv_.reshape(shp)

    for nm in early:
        adamw_big(nm)

    small_g = [dmods[:, :, :N_MOD].reshape(2, 2, N_MOD * d), grads["norm_g"], grads["pool_scale"], grads["final_g"],
               grads["b_a_f"], grads["b_a_b"], grads["sink"], grads["gla_g"], grads["w_a2_f"], grads["w_a2_b"]]
    small_g_shapes = [a.shape for a in small_g]
    g3 = _gather_small(_pack(small_g), "gather_small_grads", dep=delta["w_out"])
    total = _sum_rows8(g3, "sum_small_grads")
    dmm_all = _unpack(g3, small_g_shapes[:1])[0]
    (dmm_sum, dnorm_g, dpool_scale, dfinal_g, db_a_f, db_a_b, dsink, dgla_g, dw_a2_f, dw_a2_b) = [
        a[0] for a in _unpack(total, small_g_shapes)]
    dmm_rows = jnp.concatenate([dmm_all[:, :, 0].transpose(1, 0, 2), dmm_sum[:, 1][:, None, :],
                                jnp.zeros((2, 16 - N_DEV - 1, N_MOD * d), F32)], axis=1)
    grad_b_mod = dmm_sum[:, 0] + dmm_sum[:, 1]
    dmm_cols = lax.dynamic_slice_in_dim(dmm_rows, me * nc, nc, axis=2)
    cs_t = jnp.transpose(_silu(craw)).astype(BF16)
    grad_w_mod, dcraw = _adaln_bwd(craw, cs_t, dmm_cols, w_mod, "adaln_bwd")
    g4 = _gather_small((dcraw[0, N_DEV] + dcraw[1, N_DEV])[None, :], "gather_c_ctx_grad")
    grad_c_ctx = _sum_rows8(g4, "sum_c_ctx_grad")[0]

    col = lambda v, n: lax.dynamic_slice_in_dim(v, me * n, n, axis=v.ndim - 1)
    g_small = {
        "c_ctx": grad_c_ctx, "b_mod": grad_b_mod, "norm_g": col(dnorm_g, norm_g.shape[2]),
        "w_a2_f": col(dw_a2_f, w_a2_f.shape[2])[None], "b_a_f": db_a_f[None], "w_a2_b": col(dw_a2_b, w_a2_b.shape[2])[None],
        "b_a_b": db_a_b[None], "sink": dsink[None], "gla_g": dgla_g[None], "pool_scale": col(dpool_scale, pool_scale.shape[1])[None],
        "final_g": dfinal_g,
    }
    g_all.update(g_small, w_mod=grad_w_mod)
    adamw_big("w_mod")
    rest = [nm for nm in order if nm not in big]
    rest_shapes = [ws[nm].shape for nm in rest]
    packed = [_pack([d_[nm].reshape(ws[nm].shape) for nm in rest]).reshape(-1, 128) for d_ in (ws, g_all, ms, vs)]
    pad_rows = (-packed[0].shape[0]) % 512
    packed = [jnp.pad(p, ((0, pad_rows), (0, 0))) for p in packed]
    outs = _adamw(*packed, "adamw_small")
    for dst, arr in zip((delta, new_m, new_v), outs):
        for nm, val in zip(rest, _unpack(arr.reshape(1, -1), rest_shapes)):
            dst[nm] = val[0]

    (dwo1_0,), (dwi1_0,) = reduce_stage(4, outs[0]), reduce_stage(5, outs[0])
    g_all["ffn1_wi"] = back(jnp.concatenate([dwi1_0, dwi1_1], axis=0))
    g_all["ffn1_wo"] = jnp.concatenate([dwo1_0, dwo1_1], axis=0)
    for nm in late:
        adamw_big(nm)
    g_all = {nm: g_all[nm].reshape(ws[nm].shape) for nm in order}

    return (loss, grad_x[None], *[g_all[nm] for nm in order], *[delta[nm] for nm in order],
            *[new_m[nm] for nm in order], *[new_v[nm] for nm in order])
```

```python
import functools

import numpy as np
import jax
import jax.numpy as jnp
from jax import lax
from jax.experimental import pallas as pl
from jax.experimental.pallas import tpu as pltpu

F32 = jnp.float32
BF16 = jnp.bfloat16
MESH = pl.DeviceIdType.MESH

N_DEV = 8
RMS_EPS = 1e-6
N_MOD = 9
GRID_W = 64
A_HEADS, A_KV_HEADS, A_HEAD_DIM = 8, 2, 64
A_REP = A_HEADS // A_KV_HEADS
WINDOW = 128
ROPE_BASE = 10000.0
B_HEADS, B_DK, B_DV = 4, 64, 128
B_GATE_RANK = 16
B_GATE_NORM = 16.0
B_CHUNK = 64
POOL_WINDOWS = (2, 4, 8, 16)
POOL_PAD = 8
A_Q = A_HEADS * A_HEAD_DIM
A_KV = A_KV_HEADS * A_HEAD_DIM
B_QK = B_HEADS * B_DK
B_V = B_HEADS * B_DV
PROJ_SIZES = (A_Q, A_KV, A_KV, B_QK, B_QK, B_V, B_V, 2 * B_GATE_RANK)
PROJ_DIM = sum(PROJ_SIZES)
ADAM_LR, ADAM_B1, ADAM_B2, ADAM_EPS, ADAM_WD, ADAM_STEP = 0.001, 0.9, 0.999, 1e-08, 0.01, 10

VMEM_LIMIT = 56 * 1024 * 1024
ROW_TILES = (512, 544, 256, 128, 64, 32, 16, 8)
TALL_TILES = (1024, 1088) + ROW_TILES

NN = ((1,), (0,))
NT = ((1,), (1,))
TN = ((0,), (0,))


def _dot(a, b, dims=NN, prec=None):
    return lax.dot_general(a, b, (dims, ((), ())), precision=prec, preferred_element_type=F32)


def _bdot(a, b, dims=NN):
    return _dot(a.astype(BF16), b.astype(BF16), dims)


def _dot_01(sel, x):
    hi = x.astype(BF16)
    rest = x - hi.astype(F32)
    mid = rest.astype(BF16)
    lo = (rest - mid.astype(F32)).astype(BF16)
    sel = sel.astype(BF16)
    return _dot(sel, hi) + _dot(sel, mid) + _dot(sel, lo)


def _params(sem=None, **kw):
    return pltpu.CompilerParams(dimension_semantics=sem, vmem_limit_bytes=VMEM_LIMIT, **kw)


def _silu(a):
    return a * jax.nn.sigmoid(a)


def _pick(n, prefs):
    for p in prefs:
        if n % p == 0:
            return p
    return n


def _full(shape):
    nd = len(shape)
    return pl.BlockSpec(shape, lambda *_: (0,) * nd)


def _peers():
    x, y, c = lax.axis_index("x"), lax.axis_index("y"), lax.axis_index("c")
    return x, y, c


def _dev_index():
    x, y, c = _peers()
    return 4 * x + 2 * y + c


def _others(x, y, c):
    return [(x, y, 1 - c), (1 - x, y, c), (x, 1 - y, c), (1 - x, 1 - y, c),
            (1 - x, y, 1 - c), (x, 1 - y, 1 - c), (1 - x, 1 - y, 1 - c)]


def _index_of(dev):
    return 4 * dev[0] + 2 * dev[1] + dev[2]


def _exchange_refs(gather, shapes, srcs, lands, a, me, to):
    if gather:
        r = shapes[a][1]
        return srcs[a], lands[a].at[:, pl.ds(_index_of(me) * r, r), :]
    r = shapes[a][1] // N_DEV
    return srcs[a].at[:, pl.ds(_index_of(to) * r, r), :], lands[a].at[_index_of(me)]


HBM_SPEC = pl.BlockSpec(memory_space=pltpu.HBM)
SEM_SPEC = pl.BlockSpec(memory_space=pltpu.SEMAPHORE)
EFFECT = pltpu.SideEffectType.DATAFLOW_SIDE_EFFECTING


NEAR_PEERS = 4


def _exchange_start(srcs, lands, gather, collective_id, dep, name, n_peers=N_DEV - 1):
    n = len(srcs)
    shapes = [s.shape for s in srcs]
    deps = [] if dep is None else [dep]

    def body(*refs):
        src_refs, land_refs = refs[:n], refs[n:2 * n]
        send_sems, recv_sems = refs[2 * n + len(deps)], refs[2 * n + len(deps) + 1]
        token = refs[-1]
        x, y, c = _peers()
        others = _others(x, y, c)[:n_peers]
        barrier = pltpu.get_barrier_semaphore()
        for peer in others:
            pl.semaphore_signal(barrier, inc=1, device_id=peer, device_id_type=MESH)
        pl.semaphore_wait(barrier, len(others))
        for a in range(n):
            for k, to in enumerate(others):
                src, dst = _exchange_refs(gather, shapes, src_refs, land_refs, a, (x, y, c), to)
                pltpu.make_async_remote_copy(src_ref=src, dst_ref=dst, send_sem=send_sems.at[7 * a + k],
                                             recv_sem=recv_sems.at[7 * a + k], device_id=to, device_id_type=MESH).start()
        token[...] = jnp.zeros_like(token)

    outs = pl.pallas_call(
        body, name=name,
        out_shape=(pltpu.SemaphoreType.DMA((7 * n,)), pltpu.SemaphoreType.DMA((7 * n,)),
                   *[pltpu.HBM(s.shape, s.dtype) for s in srcs], *[pltpu.HBM(l.shape, l.dtype) for l in lands],
                   jax.ShapeDtypeStruct((8, 128), F32)),
        in_specs=[HBM_SPEC] * (2 * n) + [pl.BlockSpec(memory_space=pl.ANY)] * len(deps),
        out_specs=(SEM_SPEC, SEM_SPEC, *[HBM_SPEC] * (2 * n), pl.BlockSpec(memory_space=pltpu.VMEM)),
        input_output_aliases={i: 2 + i for i in range(2 * n)},
        compiler_params=pltpu.CompilerParams(has_side_effects=EFFECT, collective_id=collective_id),
    )(*[pltpu.with_memory_space_constraint(s, pltpu.HBM) for s in srcs],
      *[pltpu.with_memory_space_constraint(l, pltpu.HBM) for l in lands], *deps)
    return outs[0], outs[1], list(outs[2:2 + n]), list(outs[2 + n:2 + 2 * n]), outs[-1]


def _exchange_wait(started, gather, after, name, n_peers=N_DEV - 1):
    send_sems, recv_sems, srcs, lands, _ = started
    n = len(srcs)
    shapes = [s.shape for s in srcs]

    def body(*refs):
        src_refs, land_refs = refs[:n], refs[n:2 * n]
        send_sems, recv_sems = refs[2 * n], refs[2 * n + 1]
        x, y, c = _peers()
        for a in range(n):
            for k, peer in enumerate(_others(x, y, c)[:n_peers]):
                src, _ = _exchange_refs(gather, shapes, src_refs, land_refs, a, (x, y, c), peer)
                _, dst = _exchange_refs(gather, shapes, src_refs, land_refs, a, peer, (x, y, c))
                copy = pltpu.make_async_remote_copy(src_ref=src, dst_ref=dst, send_sem=send_sems.at[7 * a + k],
                                                    recv_sem=recv_sems.at[7 * a + k], device_id=peer, device_id_type=MESH)
                copy.wait_send()
                copy.wait_recv()

    outs = pl.pallas_call(
        body, name=name,
        out_shape=(*[pltpu.HBM(s.shape, s.dtype) for s in srcs], *[pltpu.HBM(l.shape, l.dtype) for l in lands]),
        in_specs=[HBM_SPEC] * (2 * n) + [SEM_SPEC, SEM_SPEC, pl.BlockSpec(memory_space=pl.ANY)],
        out_specs=tuple([HBM_SPEC] * (2 * n)),
        input_output_aliases={i: i for i in range(2 * n)},
        compiler_params=pltpu.CompilerParams(has_side_effects=EFFECT),
    )(*srcs, *lands, send_sems, recv_sems, after)
    return list(outs[:n]), list(outs[n:])


def _forward_refs(land_refs, rows, a, others, j, received):
    origin = others[j + 3] if received else others[j]
    return land_refs[a].at[:, pl.ds(_index_of(origin) * rows[a], rows[a]), :]


def _forward_start(lands, rows, collective_id, name):
    n = len(lands)

    def body(*refs):
        land_refs, send_sems, recv_sems, token = refs[:n], refs[n], refs[n + 1], refs[-1]
        x, y, c = _peers()
        others = _others(x, y, c)
        barrier = pltpu.get_barrier_semaphore()
        pl.semaphore_signal(barrier, inc=1, device_id=others[0], device_id_type=MESH)
        pl.semaphore_wait(barrier, 1)
        for a in range(n):
            for j in (1, 2, 3):
                blk = _forward_refs(land_refs, rows, a, others, j, False)
                pltpu.make_async_remote_copy(src_ref=blk, dst_ref=blk, send_sem=send_sems.at[3 * a + j - 1],
                                             recv_sem=recv_sems.at[3 * a + j - 1], device_id=others[0],
                                             device_id_type=MESH).start()
        token[...] = jnp.zeros_like(token)

    outs = pl.pallas_call(
        body, name=name,
        out_shape=(pltpu.SemaphoreType.DMA((3 * n,)), pltpu.SemaphoreType.DMA((3 * n,)),
                   *[pltpu.HBM(l.shape, l.dtype) for l in lands], jax.ShapeDtypeStruct((8, 128), F32)),
        in_specs=[HBM_SPEC] * n,
        out_specs=(SEM_SPEC, SEM_SPEC, *[HBM_SPEC] * n, pl.BlockSpec(memory_space=pltpu.VMEM)),
        input_output_aliases={i: 2 + i for i in range(n)},
        compiler_params=pltpu.CompilerParams(has_side_effects=EFFECT, collective_id=collective_id),
    )(*[pltpu.with_memory_space_constraint(l, pltpu.HBM) for l in lands])
    return outs[0], outs[1], list(outs[2:2 + n]), outs[-1]


def _forward_wait(started, rows, after, name):
    send_sems, recv_sems, lands, _ = started
    n = len(lands)

    def body(*refs):
        land_refs, send_sems, recv_sems = refs[:n], refs[n], refs[n + 1]
        x, y, c = _peers()
        others = _others(x, y, c)
        for a in range(n):
            for j in (1, 2, 3):
                copy = pltpu.make_async_remote_copy(
                    src_ref=_forward_refs(land_refs, rows, a, others, j, False),
                    dst_ref=_forward_refs(land_refs, rows, a, others, j, True), send_sem=send_sems.at[3 * a + j - 1],
                    recv_sem=recv_sems.at[3 * a + j - 1], device_id=others[0], device_id_type=MESH)
                copy.wait_send()
                copy.wait_recv()

    outs = pl.pallas_call(
        body, name=name, out_shape=tuple(pltpu.HBM(l.shape, l.dtype) for l in lands),
        in_specs=[HBM_SPEC] * n + [SEM_SPEC, SEM_SPEC, pl.BlockSpec(memory_space=pl.ANY)],
        out_specs=tuple([HBM_SPEC] * n), input_output_aliases={i: i for i in range(n)},
        compiler_params=pltpu.CompilerParams(has_side_effects=EFFECT),
    )(*lands, send_sems, recv_sems, after)
    return list(outs)


def _place_shard(shard, me, name):
    a_, r, c = shard.shape
    tr = _pick(r, (352, 304, 256, 128, 64, 32, 16, 8))
    nr = r // tr

    def body(me_ref, in_ref, out_ref):
        out_ref[...] = in_ref[...]

    return pl.pallas_call(
        body, name=name,
        grid_spec=pltpu.PrefetchScalarGridSpec(
            num_scalar_prefetch=1, grid=(a_, nr),
            in_specs=[pl.BlockSpec((None, tr, c), lambda i, j, me_ref: (i, j, 0))],
            out_specs=pl.BlockSpec((None, tr, c), lambda i, j, me_ref: (i, me_ref[0] * nr + j, 0))),
        out_shape=jax.ShapeDtypeStruct((a_, N_DEV * r, c), shard.dtype),
        compiler_params=_params(("parallel", "parallel")),
    )(me.reshape(1).astype(jnp.int32), shard)


def _sum_slots(land, whole, me, name):
    _, a_, r, c = land.shape
    tr = _pick(r, (352, 256, 128, 64, 32, 16, 8))
    nr = r // tr

    def body(me_ref, land_ref, own_ref, out_ref):
        acc = None
        for s in range(N_DEV):
            part = jnp.where(me_ref[0] == s, own_ref[...], land_ref[s]).astype(F32)
            acc = part if acc is None else acc + part
        out_ref[...] = acc

    return pl.pallas_call(
        body, name=name,
        grid_spec=pltpu.PrefetchScalarGridSpec(
            num_scalar_prefetch=1, grid=(a_, nr),
            in_specs=[pl.BlockSpec((N_DEV, None, tr, c), lambda i, j, me_ref: (0, i, j, 0)),
                      pl.BlockSpec((None, tr, c), lambda i, j, me_ref: (i, me_ref[0] * nr + j, 0))],
            out_specs=pl.BlockSpec((None, tr, c), lambda i, j, me_ref: (i, j, 0))),
        out_shape=jax.ShapeDtypeStruct((a_, r, c), F32),
        compiler_params=_params(("parallel", "parallel")),
    )(me.reshape(1).astype(jnp.int32), land, whole)


def _gather_small(vec, name, dep=None):
    p = vec.shape[1]
    pp = -(-p // 1024) * 1024
    blk = jnp.pad(vec, ((0, 0), (0, pp - p))).reshape(8, pp // 8)
    deps = [] if dep is None else [dep]

    def body(in_ref, *rest):
        out_ref, send_sems, recv_sems = rest[-3:]
        x, y, c = _peers()
        me = 4 * x + 2 * y + c
        others = [(x, y, 1 - c), (1 - x, y, c), (x, 1 - y, c), (1 - x, 1 - y, c),
                  (1 - x, y, 1 - c), (x, 1 - y, 1 - c), (1 - x, 1 - y, 1 - c)]

        def rows(idx):
            return out_ref.at[pl.ds(pl.multiple_of(idx * 8, 8), 8), :]

        out_ref[pl.ds(pl.multiple_of(me * 8, 8), 8), :] = in_ref[...]

        def copy(k, dev, slot):
            return pltpu.make_async_remote_copy(
                src_ref=in_ref, dst_ref=rows(slot), send_sem=send_sems.at[k], recv_sem=recv_sems.at[k],
                device_id=dev, device_id_type=MESH)

        sends = [copy(k, dev, me) for k, dev in enumerate(others)]
        for cp in sends:
            cp.start()
        for k, dev in enumerate(others):
            copy(k, dev, 4 * dev[0] + 2 * dev[1] + dev[2]).wait_recv()
        for cp in sends:
            cp.wait_send()

    vm = pl.BlockSpec(memory_space=pltpu.VMEM)
    out = pl.pallas_call(
        body, name=name, out_shape=jax.ShapeDtypeStruct((8 * N_DEV, pp // 8), F32),
        in_specs=[vm] + [pl.BlockSpec(memory_space=pl.ANY)] * len(deps), out_specs=vm,
        scratch_shapes=[pltpu.SemaphoreType.DMA((7,)), pltpu.SemaphoreType.DMA((7,))],
        compiler_params=pltpu.CompilerParams(has_side_effects=True, vmem_limit_bytes=VMEM_LIMIT),
    )(blk, *deps)
    return out.reshape(N_DEV, pp)[:, :p]


def _sum_rows8(g, name):
    p = g.shape[1]

    def body(in_ref, out_ref):
        acc = in_ref[0:1, :]
        for s in range(1, N_DEV):
            acc = acc + in_ref[s:s + 1, :]
        out_ref[...] = acc

    return pl.pallas_call(body, name=name, out_shape=jax.ShapeDtypeStruct((1, p), F32),
                          compiler_params=_params())(g)


def _sel_row(mods_ref, is_ctx, k):
    return jnp.where(is_ctx, mods_ref[1, k:k + 1, :], mods_ref[0, k:k + 1, :])


def _modulate(z, mods, g, ks, kc, n_x, out_dtype, name):
    m, d = z.shape
    tm = _pick(m, (256, 128, 64, 32, 16, 8))

    def body(z_ref, mods_ref, g_ref, h_ref):
        is_ctx = pl.program_id(0) * tm >= n_x
        zz = z_ref[...]
        r = lax.rsqrt(jnp.mean(zz * zz, axis=-1, keepdims=True) + RMS_EPS)
        shift, scale = _sel_row(mods_ref, is_ctx, ks), _sel_row(mods_ref, is_ctx, kc)
        h_ref[...] = ((zz * r) * g_ref[...] * (1.0 + scale) + shift).astype(out_dtype)

    return pl.pallas_call(
        body, name=name, grid=(m // tm,),
        in_specs=[pl.BlockSpec((tm, d), lambda i: (i, 0)), _full(mods.shape), _full(g.shape)],
        out_specs=pl.BlockSpec((tm, d), lambda i: (i, 0)),
        out_shape=jax.ShapeDtypeStruct((m, d), out_dtype),
        compiler_params=_params(("parallel",)),
    )(z, mods, g)


def _gate_bwd_rows(dx, y, gate, coef):
    return (coef * gate * dx).astype(BF16), jnp.sum(coef * y * dx, axis=0, keepdims=True)


def _modulate_bwd(z, dh, dres, mods, g, kc, n_x, name, latent_only=False, then=None):
    m, d = z.shape
    tm = _pick(m, (256, 128, 64, 32, 16, 8))
    first_ctx = n_x // tm
    res_blocks = dres.shape[0] // tm
    out_blocks = (n_x if latent_only else m) // tm
    extra = [] if then is None else [then[0], then[1]]

    def body(z_ref, dh_ref, dres_ref, mods_ref, g_ref, *rest):
        i = pl.program_id(0)
        is_ctx = i * tm >= n_x
        dx_ref, acc_ref = rest[len(extra)], rest[len(extra) + 1]

        @pl.when((i == 0) | (i == first_ctx))
        def _():
            acc_ref[...] = jnp.zeros_like(acc_ref)
            if then is not None:
                rest[-1][...] = jnp.zeros_like(rest[-1])

        zz, dhh = z_ref[...], dh_ref[...]
        r = lax.rsqrt(jnp.mean(zz * zz, axis=-1, keepdims=True) + RMS_EPS)
        nz = zz * r
        gain = g_ref[...] * (1.0 + _sel_row(mods_ref, is_ctx, kc))
        dn = dhh * gain
        dz = r * (dn - nz * jnp.mean(dn * nz, axis=-1, keepdims=True))
        dx = jnp.where(i < res_blocks, dres_ref[...], 0.0) + dz

        @pl.when(i < out_blocks)
        def _():
            dx_ref[...] = dx

        acc_ref[0:1, :] += jnp.sum(dhh, axis=0, keepdims=True)
        acc_ref[1:2, :] += jnp.sum(dhh * nz, axis=0, keepdims=True)
        if then is not None:
            y_ref, tmods_ref, dy_ref, gate_acc_ref = rest[0], rest[1], rest[-2], rest[-1]
            dy, part = _gate_bwd_rows(dx, y_ref[...], _sel_row(tmods_ref, is_ctx, then[2]), then[3])
            dy_ref[...] = dy
            gate_acc_ref[0:1, :] += part

    row = pl.BlockSpec((tm, d), lambda i: (i, 0))
    acc_spec = pl.BlockSpec((None, 8, d), lambda i: ((i * tm >= n_x).astype(jnp.int32), 0, 0))
    out_specs = [pl.BlockSpec((tm, d), lambda i: (jnp.minimum(i, out_blocks - 1), 0)), acc_spec]
    out_shape = [jax.ShapeDtypeStruct((out_blocks * tm, d), F32), jax.ShapeDtypeStruct((2, 8, d), F32)]
    in_specs = [row, row, pl.BlockSpec((tm, d), lambda i: (jnp.minimum(i, res_blocks - 1), 0)),
                _full(mods.shape), _full(g.shape)]
    if then is not None:
        in_specs += [row, _full(then[1].shape)]
        out_specs += [row, acc_spec]
        out_shape += [jax.ShapeDtypeStruct((m, d), BF16), jax.ShapeDtypeStruct((2, 8, d), F32)]
    return pl.pallas_call(
        body, name=name, grid=(m // tm,), in_specs=in_specs, out_specs=out_specs, out_shape=out_shape,
        compiler_params=_params(("arbitrary",)),
    )(z, dh, dres, mods, g, *extra)


def _ffn_up(h, wi_t, layer, name):
    m, d = h.shape
    f = wi_t.shape[1] // 2
    tm = _pick(m, ROW_TILES)

    def body(h_ref, w_ref, jac_ref, act_ref):
        hh = h_ref[...]
        a = _dot(hh, w_ref[0:f, :], NT)
        u = _dot(hh, w_ref[f:2 * f, :], NT)
        sg = jax.nn.sigmoid(a)
        s = a * sg
        jac_ref[:, 0:f] = (u * (sg * (1.0 + a * (1.0 - sg)))).astype(BF16)
        jac_ref[:, f:2 * f] = s.astype(BF16)
        act_ref[...] = (s * u).astype(BF16)

    return pl.pallas_call(
        body, name=name, grid=(m // tm,),
        in_specs=[pl.BlockSpec((tm, d), lambda i: (i, 0)),
                  pl.BlockSpec((None, 2 * f, d), lambda i: (layer, 0, 0))],
        out_specs=[pl.BlockSpec((tm, 2 * f), lambda i: (i, 0)), pl.BlockSpec((tm, f), lambda i: (i, 0))],
        out_shape=[jax.ShapeDtypeStruct((m, 2 * f), BF16), jax.ShapeDtypeStruct((m, f), BF16)],
        compiler_params=_params(("parallel",)),
    )(h, wi_t)


def _mm_resid(a, b, layer, res, mods, km, coef, n_x, name, nxt=None):
    m, k = a.shape
    n = b.shape[2]
    tm = _pick(m, (512, 256, 128, 64, 32, 16, 8))
    tn = n if nxt is not None else _pick(n, (1024, 512, 256, 128))
    extra = [] if nxt is None else [nxt[0], nxt[1]]

    def body(a_ref, b_ref, res_ref, mods_ref, *rest):
        is_ctx = pl.program_id(1) * tm >= n_x
        y = _dot(a_ref[...], b_ref[...])
        new = res_ref[...] + coef * _sel_row(mods_ref, is_ctx, km) * y
        if nxt is None:
            out_ref, y_ref = rest
        else:
            nmods_ref, g_ref, out_ref, y_ref, h_ref = rest
            r = lax.rsqrt(jnp.mean(new * new, axis=-1, keepdims=True) + RMS_EPS)
            shift, scale = _sel_row(nmods_ref, is_ctx, nxt[2]), _sel_row(nmods_ref, is_ctx, nxt[3])
            h_ref[...] = ((new * r) * g_ref[...] * (1.0 + scale) + shift).astype(nxt[4])
        y_ref[...] = y.astype(BF16)
        out_ref[...] = new

    tile = pl.BlockSpec((tm, tn), lambda j, i: (i, j))
    outs = [jax.ShapeDtypeStruct((m, n), F32), jax.ShapeDtypeStruct((m, n), BF16)]
    if nxt is not None:
        outs.append(jax.ShapeDtypeStruct((m, n), nxt[4]))
    return pl.pallas_call(
        body, name=name, grid=(n // tn, m // tm),
        in_specs=[pl.BlockSpec((tm, k), lambda j, i: (i, 0)),
                  pl.BlockSpec((None, k, tn), lambda j, i: (layer, 0, j)),
                  tile, pl.BlockSpec((2, 16, tn), lambda j, i: (0, 0, j))] + [_full(e.shape) for e in extra],
        out_specs=[tile] * len(outs), out_shape=outs,
        compiler_params=_params(("parallel", "parallel")),
    )(a, b, res, mods, *extra)


def _ffn_down_bwd(dy, wo, layer, au, name):
    m, d = dy.shape
    f = wo.shape[1]
    tm = _pick(m, ROW_TILES)

    def body(dy_ref, wo_ref, au_ref, dau_ref):
        dact = _dot(dy_ref[...], wo_ref[...], NT)
        dau_ref[:, 0:f] = (dact * au_ref[:, 0:f].astype(F32)).astype(BF16)
        dau_ref[:, f:2 * f] = (dact * au_ref[:, f:2 * f].astype(F32)).astype(BF16)

    wide = pl.BlockSpec((tm, 2 * f), lambda i: (i, 0))
    return pl.pallas_call(
        body, name=name, grid=(m // tm,),
        in_specs=[pl.BlockSpec((tm, d), lambda i: (i, 0)), pl.BlockSpec((None, f, d), lambda i: (layer, 0, 0)), wide],
        out_specs=wide, out_shape=jax.ShapeDtypeStruct((m, 2 * f), BF16),
        compiler_params=_params(("parallel",)),
    )(dy, wo, au)


def _mm(terms, dims, n, out_dtype, name, tm_pref=(512, 256, 128, 64, 32, 16, 8), tn_pref=(512, 256, 128), dep=None):
    m = terms[0][0].shape[0]
    tm = _pick(m, tm_pref)
    tn = _pick(n, tn_pref)
    nt = len(terms)
    deps = [] if dep is None else [dep]

    def body(*refs):
        out_ref = refs[-1]
        acc = None
        for t in range(nt):
            part = _dot(refs[2 * t][...].astype(BF16), refs[2 * t + 1][...].astype(BF16), dims)
            acc = part if acc is None else acc + part
        out_ref[...] = acc.astype(out_dtype)

    in_specs, args = [], []
    for a, b, layer, rb in terms:
        k = a.shape[1]
        in_specs.append(pl.BlockSpec((tm, k), lambda j, i: (i, 0)))
        if dims == NN:
            in_specs.append(pl.BlockSpec((None, k, tn), lambda j, i, layer=layer, rb=rb: (layer, rb, j)))
        else:
            nb = n // tn
            in_specs.append(pl.BlockSpec((None, tn, k), lambda j, i, layer=layer, rb=rb, nb=nb: (layer, rb * nb + j, 0)))
        args += [a, b]
    return pl.pallas_call(
        body, name=name, grid=(n // tn, m // tm), in_specs=in_specs + [pl.BlockSpec(memory_space=pl.ANY)] * len(deps),
        out_specs=pl.BlockSpec((tm, tn), lambda j, i: (i, j)),
        out_shape=jax.ShapeDtypeStruct((m, n), out_dtype),
        compiler_params=_params(("parallel", "parallel")),
    )(*args, *deps)


def _mm_tn(a, b, out_dtype, name, dep=None):
    t = a.shape[0]
    m, n = a.shape[1], b.shape[1]
    tm = _pick(m, (1408, 2432, 1024, 512, 256, 128))
    tn = _pick(n, (1024, 512, 256, 128))
    tk = _pick(t, TALL_TILES)
    deps = [] if dep is None else [dep]

    def body(a_ref, b_ref, *rest):
        out_ref, acc_ref = rest[-2:]
        kk = pl.program_id(2)

        @pl.when(kk == 0)
        def _():
            acc_ref[...] = jnp.zeros_like(acc_ref)

        acc_ref[...] += _dot(a_ref[...].astype(BF16), b_ref[...].astype(BF16), TN)

        @pl.when(kk == pl.num_programs(2) - 1)
        def _():
            out_ref[...] = acc_ref[...].astype(out_dtype)

    return pl.pallas_call(
        body, name=name, grid=(m // tm, n // tn, t // tk),
        in_specs=[pl.BlockSpec((tk, tm), lambda i, j, k: (k, i)), pl.BlockSpec((tk, tn), lambda i, j, k: (k, j))]
        + [pl.BlockSpec(memory_space=pl.ANY)] * len(deps),
        out_specs=pl.BlockSpec((tm, tn), lambda i, j, k: (i, j)),
        out_shape=jax.ShapeDtypeStruct((m, n), out_dtype),
        scratch_shapes=[pltpu.VMEM((tm, tn), F32)],
        compiler_params=_params(("parallel", "parallel", "arbitrary")),
    )(a, b, *deps)


def _stack_rows(a, b, name):
    ta, d = a.shape
    tm = _pick(int(np.gcd(ta, b.shape[0])), (256, 128, 64, 32, 16, 8))
    na, nb = ta // tm, b.shape[0] // tm

    def body(a_ref, b_ref, o_ref):
        o_ref[...] = jnp.where(pl.program_id(0) < na, a_ref[...], b_ref[...])

    return pl.pallas_call(
        body, name=name, grid=(na + nb,),
        in_specs=[pl.BlockSpec((tm, d), lambda i: (jnp.minimum(i, na - 1), 0)),
                  pl.BlockSpec((tm, d), lambda i: (jnp.maximum(i - na, 0), 0))],
        out_specs=pl.BlockSpec((tm, d), lambda i: (i, 0)),
        out_shape=jax.ShapeDtypeStruct((ta + b.shape[0], d), a.dtype),
        compiler_params=_params(("parallel",)),
    )(a, b)


def _assemble_dz(lat_parts, ctx_parts, both_parts, width, name):
    t = next(p.shape[0] for p in lat_parts if p is not None)
    l_ctx = next(p.shape[0] for p in ctx_parts if p is not None)
    tm = _pick(int(np.gcd(t, l_ctx)), (256, 128, 64, 32, 16, 8))
    nt, nl = t // tm, l_ctx // tm
    plan, args, in_specs, off = [], [], [], 0
    lat_spec = lambda w: pl.BlockSpec((tm, w), lambda i: (jnp.minimum(i, nt - 1), 0))
    ctx_spec = lambda w: pl.BlockSpec((tm, w), lambda i: (jnp.maximum(i - nt, 0), 0))
    all_spec = lambda w: pl.BlockSpec((tm, w), lambda i: (i, 0))
    for lat, ctx, both in zip(lat_parts, ctx_parts, both_parts):
        if both:
            w = both[0].shape[1]
            plan.append(("both", off, w, len(args), len(both)))
            args += both
            in_specs += [all_spec(w)] * len(both)
        else:
            w = (lat if lat is not None else ctx).shape[1]
            plan.append(("split", off, w, len(args), (lat is not None, ctx is not None)))
            for part, spec in ((lat, lat_spec), (ctx, ctx_spec)):
                if part is not None:
                    args.append(part)
                    in_specs.append(spec(w))
        off += w
    n_in = len(args)

    def body(*refs):
        out_ref = refs[n_in]
        is_ctx = pl.program_id(0) >= nt
        for kind, o, w, first, info in plan:
            if kind == "both":
                val = refs[first][...]
                for k in range(1, info):
                    val = val + refs[first + k][...]
            else:
                has_lat, has_ctx = info
                zero = jnp.zeros((tm, w), F32)
                lat = refs[first][...] if has_lat else zero
                ctx = refs[first + int(has_lat)][...] if has_ctx else zero
                val = jnp.where(is_ctx, ctx, lat)
            out_ref[:, o:o + w] = val.astype(BF16)
        if off < width:
            out_ref[:, off:width] = jnp.zeros((tm, width - off), BF16)

    return pl.pallas_call(
        body, name=name, grid=(nt + nl,), in_specs=in_specs,
        out_specs=pl.BlockSpec((tm, width), lambda i: (i, 0)),
        out_shape=jax.ShapeDtypeStruct((t + l_ctx, width), BF16),
        compiler_params=_params(("parallel",)),
    )(*args)


def _final_loss(x, g, target, then, name):
    t, d = x.shape
    tm = _pick(t, (256, 128, 64, 32, 16, 8))
    y, tmods, km, coef = then

    def body(x_ref, g_ref, t_ref, y_ref, tmods_ref, dx_ref, loss_ref, dg_ref, dy_ref, gate_acc_ref):
        @pl.when(pl.program_id(0) == 0)
        def _():
            loss_ref[...] = jnp.zeros_like(loss_ref)
            dg_ref[...] = jnp.zeros_like(dg_ref)
            gate_acc_ref[...] = jnp.zeros_like(gate_acc_ref)

        xx, gg = x_ref[...], g_ref[...]
        r = lax.rsqrt(jnp.mean(xx * xx, axis=-1, keepdims=True) + RMS_EPS)
        nz = xx * r
        err = nz * gg - t_ref[...]
        loss_ref[...] += jnp.sum(err * err, axis=0, keepdims=True) * (0.5 / d)
        dout = err * (1.0 / d)
        dg_ref[...] += jnp.sum(dout * nz, axis=0, keepdims=True)
        dn = dout * gg
        dx = r * (dn - nz * jnp.mean(dn * nz, axis=-1, keepdims=True))
        dx_ref[...] = dx
        dy, part = _gate_bwd_rows(dx, y_ref[...], tmods_ref[0, km:km + 1, :], coef)
        dy_ref[...] = dy
        gate_acc_ref[0:1, :] += part

    row = pl.BlockSpec((tm, d), lambda i: (i, 0))
    vec = pl.BlockSpec((1, d), lambda i: (0, 0))
    acc = pl.BlockSpec((None, 8, d), lambda i: (0, 0, 0))
    return pl.pallas_call(
        body, name=name, grid=(t // tm,), in_specs=[row, vec, row, row, _full(tmods.shape)],
        out_specs=[row, vec, vec, row, acc],
        out_shape=[jax.ShapeDtypeStruct((t, d), F32), jax.ShapeDtypeStruct((1, d), F32),
                   jax.ShapeDtypeStruct((1, d), F32), jax.ShapeDtypeStruct((t, d), BF16),
                   jax.ShapeDtypeStruct((2, 8, d), F32)],
        compiler_params=_params(("arbitrary",)),
    )(x, g, target, y, tmods)


def _adaln_fwd(craw, w_mod, b_cols, name):
    lyr, d, nc = w_mod.shape

    def body(c_ref, w_ref, b_ref, out_ref):
        out_ref[...] = _bdot(_silu(c_ref[...]), w_ref[...]) + b_ref[...]

    return pl.pallas_call(
        body, name=name, grid=(lyr,),
        in_specs=[_full(craw.shape), pl.BlockSpec((None, d, nc), lambda l: (l, 0, 0)),
                  pl.BlockSpec((None, 1, nc), lambda l: (l, 0, 0))],
        out_specs=pl.BlockSpec((None, 16, nc), lambda l: (l, 0, 0)),
        out_shape=jax.ShapeDtypeStruct((lyr, 16, nc), F32),
        compiler_params=_params(("parallel",)),
    )(craw, w_mod, b_cols)


def _adaln_bwd(craw, cs_t, dmm_cols, w_mod, name):
    lyr, d, nc = w_mod.shape

    def body(c_ref, cst_ref, dmm_ref, w_ref, gw_ref, dc_ref):
        dmm = dmm_ref[...]
        gw_ref[...] = _bdot(cst_ref[...], dmm)
        cc = c_ref[...]
        sg = jax.nn.sigmoid(cc)
        dc_ref[...] = _bdot(dmm, w_ref[...], NT) * (sg * (1.0 + cc * (1.0 - sg)))

    wspec = pl.BlockSpec((None, d, nc), lambda l: (l, 0, 0))
    return pl.pallas_call(
        body, name=name, grid=(lyr,),
        in_specs=[_full(craw.shape), _full(cs_t.shape), pl.BlockSpec((None, 16, nc), lambda l: (l, 0, 0)), wspec],
        out_specs=[wspec, pl.BlockSpec((None, 16, d), lambda l: (l, 0, 0))],
        out_shape=[jax.ShapeDtypeStruct((lyr, d, nc), F32), jax.ShapeDtypeStruct((lyr, 16, d), F32)],
        compiler_params=_params(("parallel",)),
    )(craw, cs_t, dmm_cols, w_mod)


def _rope_tables(t):
    rows = np.repeat(np.arange(t // GRID_W, dtype=np.float32), GRID_W)
    cols = np.tile(np.arange(GRID_W, dtype=np.float32), t // GRID_W)
    n = A_HEAD_DIM // 4
    freqs = (ROPE_BASE ** (-np.arange(n, dtype=np.float32) / n)).astype(np.float32)
    ang_r, ang_c = (rows[:, None] * freqs).astype(np.float32), (cols[:, None] * freqs).astype(np.float32)
    cr, sr, cc, sc = np.cos(ang_r), np.sin(ang_r), np.cos(ang_c), np.sin(ang_c)
    cos = np.concatenate([cr, cr, cc, cc] * 2, axis=-1).astype(np.float32)
    sin = np.concatenate([-sr, sr, -sc, sc] * 2, axis=-1).astype(np.float32)
    return jnp.asarray(cos), jnp.asarray(sin)


def _rope(xt, cos, sin, adjoint, name, view=None):
    t = cos.shape[0]
    w, col = (xt.shape[1], 0) if view is None else view
    tb = _pick(t, (512, 256, 128))
    rep = w // cos.shape[1]

    def body(x_ref, c_ref, s_ref, o_ref):
        xx = x_ref[...]
        cc = jnp.concatenate([c_ref[...]] * rep, axis=1) if rep > 1 else c_ref[...]
        ss = jnp.concatenate([s_ref[...]] * rep, axis=1) if rep > 1 else s_ref[...]
        low = (lax.broadcasted_iota(jnp.int32, xx.shape, 1) % 32) < 16

        def partner(v):
            return jnp.where(low, pltpu.roll(v, w - 16, 1), pltpu.roll(v, 16, 1))

        if adjoint:
            o_ref[...] = xx * cc + partner(xx * ss)
        else:
            o_ref[...] = xx * cc + partner(xx) * ss

    blk = pl.BlockSpec((tb, w), lambda i: (i, 0))
    tab = pl.BlockSpec((tb, cos.shape[1]), lambda i: (i, 0))
    return pl.pallas_call(
        body, name=name, grid=(t // tb,), in_specs=[pl.BlockSpec((tb, w), lambda i: (i, col)), tab, tab],
        out_specs=blk, out_shape=jax.ShapeDtypeStruct((t, w), F32), compiler_params=_params(("parallel",)),
    )(xt, cos, sin)


def _attn_bias():
    i = (np.arange(A_REP * WINDOW) % WINDOW)[:, None]
    j = np.arange(3 * WINDOW)[None, :]
    near = np.abs(j - WINDOW - i) <= WINDOW
    variants = [near, near & (j >= WINDOW), near & (j < 2 * WINDOW), near & (j >= WINDOW) & (j < 2 * WINDOW)]
    return jnp.asarray(np.where(np.stack(variants), 0.0, -np.inf).astype(np.float32))


def _attn_bias_spec(nb):
    rows = A_REP * WINDOW
    return pl.BlockSpec((None, rows, 3 * WINDOW),
                        lambda g, n: ((n == 0).astype(jnp.int32) + 2 * (n == nb - 1).astype(jnp.int32), 0, 0))


def _attn_probs(q, kb, kc, sink, bias):
    scale = A_HEAD_DIM ** -0.5
    s1 = _bdot(q, kb, NT) * scale + bias
    s2 = _bdot(q, kc, NT) * scale
    mx = jnp.maximum(jnp.maximum(jnp.max(s1, axis=-1, keepdims=True), jnp.max(s2, axis=-1, keepdims=True)), sink)
    p1, p2, ps = jnp.exp(s1 - mx), jnp.exp(s2 - mx), jnp.exp(sink - mx)
    inv = 1.0 / (jnp.sum(p1, axis=-1, keepdims=True) + jnp.sum(p2, axis=-1, keepdims=True) + ps)
    return p1 * inv, p2 * inv, ps * inv


def _sink_rows(sink_ref):
    return jnp.concatenate([jnp.broadcast_to(sink_ref[r], (WINDOW, 1)) for r in range(A_REP)], axis=0)


def _attn_fwd(q, kp, vp, kc, vc, sink, name):
    hq, t, dh = q.shape
    nb = t // WINDOW
    lc = kc.shape[1]
    rows = A_REP * WINDOW

    def body(q_ref, k_ref, v_ref, kc_ref, vc_ref, sink_ref, bias_ref, o_ref):
        n = pl.program_id(1)
        start = pl.multiple_of(n * WINDOW, WINDOW)
        kb, vb = k_ref[pl.ds(start, 3 * WINDOW), :], v_ref[pl.ds(start, 3 * WINDOW), :]
        p1, p2, _ = _attn_probs(q_ref[...].reshape(rows, dh), kb, kc_ref[...], _sink_rows(sink_ref), bias_ref[...])
        o_ref[...] = (_bdot(p1, vb) + _bdot(p2, vc_ref[...])).reshape(A_REP, WINDOW, dh)

    qblk = pl.BlockSpec((A_REP, WINDOW, dh), lambda g, n: (g, n, 0))
    kfull = pl.BlockSpec((None, t + 2 * WINDOW, dh), lambda g, n: (g, 0, 0))
    cfull = pl.BlockSpec((None, lc, dh), lambda g, n: (g, 0, 0))
    return pl.pallas_call(
        body, name=name, grid=(hq // A_REP, nb),
        in_specs=[qblk, kfull, kfull, cfull, cfull, pl.BlockSpec((A_REP, 1, 1), lambda g, n: (g, 0, 0)),
                  _attn_bias_spec(nb)],
        out_specs=qblk, out_shape=jax.ShapeDtypeStruct((hq, t, dh), F32),
        compiler_params=_params(("parallel", "parallel")),
    )(q, kp, vp, kc, vc, sink, _attn_bias())


def _attn_bwd(q, kp, vp, kc, vc, sink, o, do, name):
    hq, t, dh = q.shape
    nb = t // WINDOW
    lc = kc.shape[1]
    scale = A_HEAD_DIM ** -0.5
    rows = A_REP * WINDOW

    def body(q_ref, k_ref, v_ref, kc_ref, vc_ref, sink_ref, o_ref, do_ref, bias_ref,
             dq_ref, dk_ref, dv_ref, dkc_ref, dvc_ref, dsink_ref):
        n = pl.program_id(1)

        @pl.when(n == 0)
        def _():
            dk_ref[...] = jnp.zeros_like(dk_ref)
            dv_ref[...] = jnp.zeros_like(dv_ref)
            dkc_ref[...] = jnp.zeros_like(dkc_ref)
            dvc_ref[...] = jnp.zeros_like(dvc_ref)
            dsink_ref[...] = jnp.zeros_like(dsink_ref)

        start = pl.multiple_of(n * WINDOW, WINDOW)
        band = pl.ds(start, 3 * WINDOW)
        qq, kb, vb, kcc, vcc = q_ref[...].reshape(rows, dh), k_ref[band, :], v_ref[band, :], kc_ref[...], vc_ref[...]
        p1, p2, ps = _attn_probs(qq, kb, kcc, _sink_rows(sink_ref), bias_ref[...])
        dout = do_ref[...].reshape(rows, dh)
        delta = jnp.sum(dout * o_ref[...].reshape(rows, dh), axis=-1, keepdims=True)
        ds1 = p1 * (_bdot(dout, vb, NT) - delta)
        ds2 = p2 * (_bdot(dout, vcc, NT) - delta)
        dq_ref[...] = ((_bdot(ds1, kb) + _bdot(ds2, kcc)) * scale).reshape(A_REP, WINDOW, dh)
        dk_ref[band, :] += _bdot(ds1.T, qq) * scale
        dv_ref[band, :] += _bdot(p1.T, dout)
        dkc_ref[...] += _bdot(ds2.T, qq) * scale
        dvc_ref[...] += _bdot(p2.T, dout)
        dsink_ref[...] += jnp.sum((-ps * delta).reshape(A_REP, WINDOW, 1), axis=1, keepdims=True)

    qblk = pl.BlockSpec((A_REP, WINDOW, dh), lambda g, n: (g, n, 0))
    kfull = pl.BlockSpec((None, t + 2 * WINDOW, dh), lambda g, n: (g, 0, 0))
    cfull = pl.BlockSpec((None, lc, dh), lambda g, n: (g, 0, 0))
    return pl.pallas_call(
        body, name=name, grid=(hq // A_REP, nb),
        in_specs=[qblk, kfull, kfull, cfull, cfull, pl.BlockSpec((A_REP, 1, 1), lambda g, n: (g, 0, 0)), qblk, qblk,
                  _attn_bias_spec(nb)],
        out_specs=[qblk, kfull, kfull, cfull, cfull, pl.BlockSpec((A_REP, 8, 128), lambda g, n: (g, 0, 0))],
        out_shape=[jax.ShapeDtypeStruct(q.shape, F32), jax.ShapeDtypeStruct(kp.shape, F32),
                   jax.ShapeDtypeStruct(kp.shape, F32), jax.ShapeDtypeStruct(kc.shape, F32),
                   jax.ShapeDtypeStruct(kc.shape, F32), jax.ShapeDtypeStruct((hq, 8, 128), F32)],
        compiler_params=_params(("parallel", "arbitrary")),
    )(q, kp, vp, kc, vc, sink, o, do, _attn_bias())


def _gate_fwd(zg, w2, b2, name):
    m = zg.shape[0]
    n = w2.shape[1]
    tm = _pick(m, (512, 256, 128, 64, 32, 16, 8))

    def body(z_ref, w_ref, b_ref, o_ref):
        o_ref[...] = jax.nn.log_sigmoid(_bdot(z_ref[...], w_ref[...]) + b_ref[...]) / B_GATE_NORM

    return pl.pallas_call(
        body, name=name, grid=(m // tm,),
        in_specs=[pl.BlockSpec((tm, zg.shape[1]), lambda i: (i, 0)), _full(w2.shape), _full(b2.shape)],
        out_specs=pl.BlockSpec((tm, n), lambda i: (i, 0)), out_shape=jax.ShapeDtypeStruct((m, n), F32),
        compiler_params=_params(("parallel",)),
    )(zg, w2, b2)


def _gate_bwd(zg, w2, b2, dla, name):
    m, rk = zg.shape
    n = w2.shape[1]
    tm = _pick(m, (512, 256, 128, 64, 32, 16, 8))

    def body(z_ref, w_ref, b_ref, d_ref, dz_ref, dw_ref, db_ref):
        @pl.when(pl.program_id(0) == 0)
        def _():
            dw_ref[...] = jnp.zeros_like(dw_ref)
            db_ref[...] = jnp.zeros_like(db_ref)

        zz, ww = z_ref[...], w_ref[...]
        pre = _bdot(zz, ww) + b_ref[...]
        dpre = d_ref[...] * (1.0 / B_GATE_NORM) * jax.nn.sigmoid(-pre)
        dz_ref[...] = _bdot(dpre, ww, NT)
        dw_ref[...] += _bdot(zz.T, dpre)
        db_ref[...] += jnp.sum(dpre, axis=0, keepdims=True)

    return pl.pallas_call(
        body, name=name, grid=(m // tm,),
        in_specs=[pl.BlockSpec((tm, rk), lambda i: (i, 0)), _full(w2.shape), _full(b2.shape),
                  pl.BlockSpec((tm, n), lambda i: (i, 0))],
        out_specs=[pl.BlockSpec((tm, rk), lambda i: (i, 0)), _full(w2.shape), _full(b2.shape)],
        out_shape=[jax.ShapeDtypeStruct((m, rk), F32), jax.ShapeDtypeStruct(w2.shape, F32),
                   jax.ShapeDtypeStruct(b2.shape, F32)],
        compiler_params=_params(("arbitrary",)),
    )(zg, w2, b2, dla)


def _chunk_order(step, n_x_chunks, n_chunks, reverse):
    n_c = n_chunks - n_x_chunks
    if reverse:
        return jnp.where(step < n_c, n_chunks - 1 - step, n_chunks - 1 - step)
    return jnp.where(step < n_c, n_x_chunks + step, step - n_c)


def _tri(reverse, transpose=False):
    i = lax.broadcasted_iota(jnp.int32, (B_CHUNK, B_CHUNK), 0)
    j = lax.broadcasted_iota(jnp.int32, (B_CHUNK, B_CHUNK), 1)
    if transpose:
        i, j = j, i
    return (j >= i) if reverse else (j <= i)


def _gla_chunk(q, k, la, reverse):
    g = _dot_01(_tri(reverse), la)
    last = 0 if reverse else B_CHUNK - 1
    gl = g[last:last + 1, :]
    eg, eng, egl = jnp.exp(g), jnp.exp(-g), jnp.exp(gl - g)
    decay_col = jnp.exp(jnp.sum(la.T, axis=1, keepdims=True))
    return q * (B_DK ** -0.5) * eg, k * eng, k * egl, eg, eng, egl, decay_col


def _head_of(shape, axis, width):
    return lax.broadcasted_iota(jnp.int32, shape, axis) // width


def _gla_chunks_per_step(n_chunks, n_x_chunks):
    return _pick(int(np.gcd(n_chunks - n_x_chunks, n_x_chunks)), (4, 2, 1))


def _gla_fwd(q, k, v, la_f, la_b, n_x, name, qk_cols=(0, 0)):
    tc, wk = la_f.shape
    wv = v.shape[1]
    hh = B_HEADS
    dk, dv = wk // hh, wv // hh
    nc, nxc = tc // B_CHUNK, n_x // B_CHUNK
    sub = _gla_chunks_per_step(nc, nxc)
    rows_per_step = sub * B_CHUNK
    orders = [functools.partial(_chunk_order, n_x_chunks=nxc // sub, n_chunks=nc // sub, reverse=rev)
              for rev in (False, True)]

    def body(*refs):
        ins, outs, s_refs = refs[:8], refs[8:12], refs[12:]

        @pl.when(pl.program_id(0) == 0)
        def _():
            for s_ref in s_refs:
                s_ref[...] = jnp.zeros_like(s_ref)

        lane_head = _head_of((B_CHUNK, wk), 1, dk)
        row_head = _head_of((wk, dv), 0, dk)
        for di, reverse in enumerate((False, True)):
            q_ref, k_ref, v_ref, la_ref = ins[4 * di:4 * di + 4]
            o_ref, s_save_ref = outs[2 * di:2 * di + 2]
            s_prev = s_refs[di][...]
            for c in (reversed(range(sub)) if reverse else range(sub)):
                rows = slice(c * B_CHUNK, (c + 1) * B_CHUNK)
                qt, kt, ke, _, _, _, decay_col = _gla_chunk(q_ref[rows, :], k_ref[rows, :], la_ref[rows, :], reverse)
                ke_t = ke.T
                update = jnp.zeros_like(s_prev)
                for h in range(hh):
                    vv = v_ref[rows, h * dv:(h + 1) * dv]
                    qm = jnp.where(lane_head == h, qt, 0.0)
                    att = jnp.where(_tri(reverse), _bdot(qm, kt, NT), 0.0)
                    o_ref[rows, h * dv:(h + 1) * dv] = _bdot(att, vv) + _bdot(qm, s_prev)
                    update = jnp.where(row_head == h, _bdot(ke_t, vv), update)
                s_save_ref[c] = s_prev
                s_prev = decay_col * s_prev + update
            s_refs[di][...] = s_prev

    def blk(w, order, col=0):
        return pl.BlockSpec((rows_per_step, w), lambda s: (order(s), col))

    def sblk(order):
        return pl.BlockSpec((sub, wk, dv), lambda s: (order(s), 0, 0))

    in_specs, out_specs = [], []
    for order in orders:
        in_specs += [blk(wk, order, qk_cols[0]), blk(wk, order, qk_cols[1]), blk(wv, order), blk(wk, order)]
        out_specs += [blk(wv, order), sblk(order)]
    o_shape, s_shape = jax.ShapeDtypeStruct((tc, wv), F32), jax.ShapeDtypeStruct((nc, wk, dv), F32)
    return pl.pallas_call(
        body, name=name, grid=(nc // sub,), in_specs=in_specs, out_specs=out_specs,
        out_shape=[o_shape, s_shape, o_shape, s_shape],
        scratch_shapes=[pltpu.VMEM((wk, dv), F32)] * 2,
        compiler_params=_params(("arbitrary",)),
    )(q, k, v, la_f, q, k, v, la_b)


def _gla_bwd(q, k, v, la_f, la_b, s_f, s_b, do, n_x, name, qk_cols=(0, 0)):
    tc, wk = la_f.shape
    wv = v.shape[1]
    hh = B_HEADS
    dk, dv = wk // hh, wv // hh
    nc, nxc = tc // B_CHUNK, n_x // B_CHUNK
    sub = _gla_chunks_per_step(nc, nxc)
    rows_per_step = sub * B_CHUNK
    nb, nxb = nc // sub, nxc // sub
    orders = [functools.partial(lambda s, rev: _chunk_order(nb - 1 - s, nxb, nb, rev), rev=rev) for rev in (False, True)]

    def body(*refs):
        ins, outs, ds_refs = refs[:12], refs[12:20], refs[20:]

        @pl.when(pl.program_id(0) == 0)
        def _():
            for ds_ref in ds_refs:
                ds_ref[...] = jnp.zeros_like(ds_ref)

        lane_head = _head_of((B_CHUNK, wk), 1, dk)
        row_head = _head_of((wk, dv), 0, dk)
        for di, reverse in enumerate((False, True)):
            q_ref, k_ref, v_ref, la_ref, s_save_ref, do_ref = ins[6 * di:6 * di + 6]
            dq_ref, dk_ref, dv_ref, dla_ref = outs[4 * di:4 * di + 4]
            mask = _tri(reverse)
            last = 0 if reverse else B_CHUNK - 1
            is_last = lax.broadcasted_iota(jnp.int32, (B_CHUNK, wk), 0) == last
            ds_new = ds_refs[di][...]
            for c in (range(sub) if reverse else reversed(range(sub))):
                rows = slice(c * B_CHUNK, (c + 1) * B_CHUNK)
                la = la_ref[rows, :]
                qt, kt, ke, eg, eng, egl, decay_col = _gla_chunk(q_ref[rows, :], k_ref[rows, :], la, reverse)
                qt_t = qt.T
                s_prev = s_save_ref[c]
                dqt, dkt, dke = jnp.zeros_like(qt), jnp.zeros_like(qt), jnp.zeros_like(qt)
                ds_add = jnp.zeros_like(ds_new)
                for h in range(hh):
                    cols = slice(h * dv, (h + 1) * dv)
                    vv, dout = v_ref[rows, cols], do_ref[rows, cols]
                    mine = lane_head == h
                    qm, km = jnp.where(mine, qt, 0.0), jnp.where(mine, ke, 0.0)
                    att = jnp.where(mask, _bdot(qm, kt, NT), 0.0)
                    datt = jnp.where(mask, _bdot(dout, vv, NT), 0.0)
                    dv_ref[rows, cols] = _bdot(att.T, dout) + _bdot(km, ds_new)
                    dqt = jnp.where(mine, _bdot(datt, kt) + _bdot(dout, s_prev, NT), dqt)
                    dkt = jnp.where(mine, _bdot(datt.T, qt), dkt)
                    dke = jnp.where(mine, _bdot(vv, ds_new, NT), dke)
                    ds_add = jnp.where(row_head == h, _bdot(qt_t, dout), ds_add)
                ddecay_row = jnp.sum((ds_new * s_prev).T, axis=0, keepdims=True)
                decay_row = jnp.exp(jnp.sum(la, axis=0, keepdims=True))
                dq_ref[rows, :] = dqt * (B_DK ** -0.5) * eg
                dk_ref[rows, :] = dkt * eng + dke * egl
                dgl = jnp.sum(dke * ke, axis=0, keepdims=True) + ddecay_row * decay_row
                dg = dqt * qt - dkt * kt - dke * ke + jnp.where(is_last, dgl, 0.0)
                dla_ref[rows, :] = _dot_01(_tri(reverse, transpose=True), dg)
                ds_new = decay_col * ds_new + ds_add
            ds_refs[di][...] = ds_new

    def blk(w, order, col=0):
        return pl.BlockSpec((rows_per_step, w), lambda s: (order(s), col))

    in_specs, out_specs = [], []
    for order in orders:
        in_specs += [blk(wk, order, qk_cols[0]), blk(wk, order, qk_cols[1]), blk(wv, order), blk(wk, order),
                     pl.BlockSpec((sub, wk, dv), lambda s, order=order: (order(s), 0, 0)), blk(wv, order)]
        out_specs += [blk(wk, order), blk(wk, order), blk(wv, order), blk(wk, order)]
    k_shape, v_shape = jax.ShapeDtypeStruct((tc, wk), F32), jax.ShapeDtypeStruct((tc, wv), F32)
    return pl.pallas_call(
        body, name=name, grid=(nb,), in_specs=in_specs, out_specs=out_specs,
        out_shape=[k_shape, k_shape, v_shape, k_shape] * 2,
        scratch_shapes=[pltpu.VMEM((wk, dv), F32)] * 2,
        compiler_params=_params(("arbitrary",)),
    )(q, k, v, la_f, s_f, do, q, k, v, la_b, s_b, do)


def _gla_out_fwd(o_f, o_b, r, g, name):
    t = r.shape[0]
    dv = g.shape[1]
    hh = r.shape[1] // dv
    tb = _pick(t, (256, 128, 64))

    def body(of_ref, ob_ref, r_ref, g_ref, out_ref):
        for h in range(hh):
            cols = slice(h * dv, (h + 1) * dv)
            o = of_ref[:, cols] + ob_ref[:, cols]
            rs = lax.rsqrt(jnp.mean(o * o, axis=-1, keepdims=True) + RMS_EPS)
            out_ref[:, cols] = (o * rs) * g_ref[...] * _silu(r_ref[:, cols])

    rblk = pl.BlockSpec((tb, hh * dv), lambda i: (i, 0))
    return pl.pallas_call(
        body, name=name, grid=(t // tb,), in_specs=[rblk, rblk, rblk, _full(g.shape)], out_specs=rblk,
        out_shape=jax.ShapeDtypeStruct((t, hh * dv), F32), compiler_params=_params(("parallel",)),
    )(o_f, o_b, r, g)


def _gla_out_bwd(o_f, o_b, r, g, dout, name):
    tc = o_f.shape[0]
    t = r.shape[0]
    dv = g.shape[1]
    hh = r.shape[1] // dv
    tb = _pick(int(np.gcd(t, tc)), (256, 128, 64))
    nt = t // tb

    def body(of_ref, ob_ref, r_ref, g_ref, d_ref, do_ref, dr_ref, dg_ref):
        i = pl.program_id(0)

        @pl.when(i == 0)
        def _():
            dg_ref[...] = jnp.zeros_like(dg_ref)

        @pl.when(i >= nt)
        def _():
            do_ref[...] = jnp.zeros_like(do_ref)

        @pl.when(i < nt)
        def _():
            gg = g_ref[...]
            for h in range(hh):
                cols = slice(h * dv, (h + 1) * dv)
                o = of_ref[:, cols] + ob_ref[:, cols]
                rs = lax.rsqrt(jnp.mean(o * o, axis=-1, keepdims=True) + RMS_EPS)
                nz = o * rs
                rr, dd = r_ref[:, cols], d_ref[:, cols]
                sg = jax.nn.sigmoid(rr)
                dr_ref[:, cols] = dd * nz * gg * (sg * (1.0 + rr * (1.0 - sg)))
                dy = dd * (rr * sg)
                dg_ref[...] += jnp.sum(dy * nz, axis=0, keepdims=True)
                dn = dy * gg
                do_ref[:, cols] = rs * (dn - nz * jnp.mean(dn * nz, axis=-1, keepdims=True))

    oblk = pl.BlockSpec((tb, hh * dv), lambda i: (i, 0))
    rblk = pl.BlockSpec((tb, hh * dv), lambda i: (jnp.minimum(i, nt - 1), 0))
    return pl.pallas_call(
        body, name=name, grid=(tc // tb,), in_specs=[oblk, oblk, rblk, _full(g.shape), rblk],
        out_specs=[oblk, rblk, _full(g.shape)],
        out_shape=[jax.ShapeDtypeStruct(o_f.shape, F32), jax.ShapeDtypeStruct(r.shape, F32),
                   jax.ShapeDtypeStruct(g.shape, F32)],
        compiler_params=_params(("arbitrary",)),
    )(o_f, o_b, r, g, dout)


def _pool_window(i, tb, t):
    return pl.multiple_of(jnp.clip(i * tb - POOL_PAD, 0, t - (tb + 2 * POOL_PAD)), 8)


def _pool_band(half, i, tb, start, adjoint):
    pos = i * tb + lax.broadcasted_iota(jnp.int32, (tb, tb + 2 * POOL_PAD), 0)
    tok = start + lax.broadcasted_iota(jnp.int32, (tb, tb + 2 * POOL_PAD), 1)
    if adjoint:
        return (tok > pos - half) & (tok <= pos + half)
    return (tok >= pos - half) & (tok < pos + half)


def _pool_count(pos, half, t):
    return (jnp.minimum(pos + half, t) - jnp.maximum(pos - half, 0)).astype(F32)


def _pool_fwd(h, w_pool, pool_scale, res, mods, km, name):
    t, d = res.shape
    ng, gw = w_pool.shape[0], w_pool.shape[1]
    tb = _pick(t, (256, 128, 64))

    def body(h_ref, w_ref, ps_ref, res_ref, mods_ref, out_ref, pooled_ref, ypre_ref):
        gi, i = pl.program_id(0), pl.program_id(1)
        half = jnp.left_shift(1, gi)
        start = _pool_window(i, tb, t)
        win = h_ref[pl.ds(start, tb + 2 * POOL_PAD), :]
        total = _dot_01(_pool_band(half, i, tb, start, False), win)
        pos = i * tb + lax.broadcasted_iota(jnp.int32, (tb, 1), 0)
        pooled = total / _pool_count(pos, half, t) - h_ref[pl.ds(pl.multiple_of(i * tb, tb), tb), :]
        ypre = _bdot(pooled, w_ref[...])
        pooled_ref[...] = pooled.astype(BF16)
        ypre_ref[...] = ypre
        out_ref[...] = res_ref[...] + mods_ref[0, km:km + 1, :] * (ypre * ps_ref[...])

    tile = pl.BlockSpec((tb, gw), lambda gi, i: (i, gi))
    return pl.pallas_call(
        body, name=name, grid=(ng, t // tb),
        in_specs=[pl.BlockSpec((t, gw), lambda gi, i: (0, gi)),
                  pl.BlockSpec((None, gw, gw), lambda gi, i: (gi, 0, 0)),
                  pl.BlockSpec((1, gw), lambda gi, i: (0, gi)), tile,
                  pl.BlockSpec((2, 16, gw), lambda gi, i: (0, 0, gi))],
        out_specs=[tile, tile, tile],
        out_shape=[jax.ShapeDtypeStruct((t, d), F32), jax.ShapeDtypeStruct((t, d), BF16),
                   jax.ShapeDtypeStruct((t, d), F32)],
        compiler_params=_params(("parallel", "parallel")),
    )(h, w_pool, pool_scale, res, mods)


def _pool_bwd(dxp, w_pool, pool_scale, pooled, ypre, mods, km, name):
    t, d = pooled.shape
    ng, gw = w_pool.shape[0], w_pool.shape[1]
    tb = _pick(t, (256, 128, 64))

    def body(dxp_ref, w_ref, ps_ref, pooled_ref, ypre_ref, mods_ref, dh_ref, dw_ref, acc_ref):
        gi, i = pl.program_id(0), pl.program_id(1)

        @pl.when(i == 0)
        def _():
            dw_ref[...] = jnp.zeros_like(dw_ref)
            acc_ref[...] = jnp.zeros_like(acc_ref)

        half = jnp.left_shift(1, gi)
        mod, ps = mods_ref[0, km:km + 1, :], ps_ref[...]
        start = _pool_window(i, tb, t)
        dwin = dxp_ref[pl.ds(start, tb + 2 * POOL_PAD), :]
        dpooled = _bdot(dwin * (mod * ps), w_ref[...], NT)
        pos = start + lax.broadcasted_iota(jnp.int32, (tb + 2 * POOL_PAD, 1), 0)
        spread = _dot_01(_pool_band(half, i, tb, start, True), dpooled / _pool_count(pos, half, t))
        dxc, yp = dxp_ref[pl.ds(pl.multiple_of(i * tb, tb), tb), :], ypre_ref[...]
        dh_ref[...] = spread - _bdot(dxc * (mod * ps), w_ref[...], NT)
        dw_ref[...] += _bdot(pooled_ref[...].astype(F32).T, dxc * (mod * ps))
        acc_ref[0:1, :] += jnp.sum(dxc * yp * mod, axis=0, keepdims=True)
        acc_ref[1:2, :] += jnp.sum(dxc * yp * ps, axis=0, keepdims=True)

    tile = pl.BlockSpec((tb, gw), lambda gi, i: (i, gi))
    wblk = pl.BlockSpec((None, gw, gw), lambda gi, i: (gi, 0, 0))
    return pl.pallas_call(
        body, name=name, grid=(ng, t // tb),
        in_specs=[pl.BlockSpec((t, gw), lambda gi, i: (0, gi)), wblk,
                  pl.BlockSpec((1, gw), lambda gi, i: (0, gi)), tile, tile,
                  pl.BlockSpec((2, 16, gw), lambda gi, i: (0, 0, gi))],
        out_specs=[tile, wblk, pl.BlockSpec((8, gw), lambda gi, i: (0, gi))],
        out_shape=[jax.ShapeDtypeStruct((t, d), F32), jax.ShapeDtypeStruct(w_pool.shape, F32),
                   jax.ShapeDtypeStruct((8, d), F32)],
        compiler_params=_params(("arbitrary", "arbitrary")),
    )(dxp, w_pool, pool_scale, pooled, ypre, mods)


def _adamw(w, g, m, v, name):
    r, c = w.shape
    tr = _pick(r, (512, 352, 256, 128, 64, 32, 16, 8))
    c1 = 1.0 / (1.0 - ADAM_B1 ** ADAM_STEP)
    c2 = 1.0 / (1.0 - ADAM_B2 ** ADAM_STEP)

    def body(w_ref, g_ref, m_ref, v_ref, d_ref, nm_ref, nv_ref):
        gg = g_ref[...]
        nm = ADAM_B1 * m_ref[...] + (1.0 - ADAM_B1) * gg
        nv = ADAM_B2 * v_ref[...] + (1.0 - ADAM_B2) * (gg * gg)
        nm_ref[...] = nm
        nv_ref[...] = nv
        d_ref[...] = -ADAM_LR * ((nm * c1) / (jnp.sqrt(nv * c2) + ADAM_EPS) + ADAM_WD * w_ref[...])

    blk = pl.BlockSpec((tr, c), lambda i: (i, 0))
    shp = jax.ShapeDtypeStruct((r, c), F32)
    return pl.pallas_call(
        body, name=name, grid=(r // tr,), in_specs=[blk] * 4, out_specs=[blk] * 3, out_shape=[shp] * 3,
        compiler_params=_params(("parallel",)),
    )(w, g, m, v)


def _heads(z, n_heads):
    m = z.shape[0]
    return z.reshape(m, n_heads, -1).transpose(1, 0, 2)


def _unheads(zh):
    return zh.transpose(1, 0, 2).reshape(zh.shape[1], -1)


def _pad_rows(a, n):
    return jnp.pad(a, ((0, 0), (n, n), (0, 0))) if a.ndim == 3 else jnp.pad(a, ((n, n), (0, 0)))


def _local_step(x, ctx, target, mods, wts, fetch, emit):
    t, d = x.shape
    l_ctx = ctx.shape[0]
    tc = t + l_ctx
    norm_g = wts["norm_g"]
    ng = lambda l, k: norm_g[l, k][None, :]
    grads = {}
    dmods = [[[None] * N_MOD for _ in range(2)] for _ in range(2)]
    dnorm = [[None] * 3 for _ in range(2)]

    def ffn_fwd(z, h, l, kbase, wi, wo, n_x, tag, nxt):
        au, act = _ffn_up(h, wi, 0, f"ffn_up_{tag}")
        wo = wo(act) if callable(wo) else wo
        outs = _mm_resid(act, wo, 0, z, mods[l], kbase + 2, 0.5, n_x, f"ffn_down_{tag}", nxt=nxt)
        return outs[0], (z, h, au, act, outs[1], wi, wo), (outs[2] if nxt is not None else None)

    def ffn_bwd(dz_new, dy, saved, l, kbase, g, n_x, tag, stage, split=False, then=None):
        z, h, au, act, y, wi, wo = saved
        dau = _ffn_down_bwd(dy, wo, 0, au, f"ffn_down_bwd_{tag}")
        dwo = _mm_tn(act, dy, BF16, f"dwo_{tag}")
        if split:
            token = emit(stage, [dwo])
            dwi_t = _mm_tn(dau, h, BF16, f"dwi_{tag}", dep=token)
            token = emit(stage + 1, [dwi_t])
        else:
            dwi_t = _mm_tn(dau, h, BF16, f"dwi_{tag}")
            token = emit(stage, [dwi_t, dwo])
        dh = _mm([(dau, wi, 0, 0)], NN, d, F32, f"dh_{tag}", tm_pref=TALL_TILES, dep=token)
        return _modulate_bwd(z, dh, dz_new, mods[l], g, kbase + 1, n_x, f"mod_bwd_{tag}", latent_only=split, then=then)

    def record(l, kbase, k_norm, g, acc_mod, acc_gate, streams):
        total = None
        for s in range(streams):
            dmods[l][s][kbase] = acc_mod[s, 0]
            dmods[l][s][kbase + 1] = acc_mod[s, 1] * g[0]
            if acc_gate is not None:
                dmods[l][s][kbase + 2] = acc_gate[s, 0]
            part = acc_mod[s, 1] * (1.0 + mods[l][s, kbase + 1])
            total = part if total is None else total + part
        dnorm[l][k_norm] = total

    xc0 = _stack_rows(x, ctx, "stack_tokens")
    wi1_0 = fetch(0, None)["wi1_0"]
    h0 = _modulate(xc0, mods[0], ng(0, 0), 0, 1, t, BF16, "mod_l0f1")
    xc1, sv_f1, hc = ffn_fwd(xc0, h0, 0, 0, wi1_0, lambda act: fetch(1, act)["wo1_0"], t, "l0f1",
                             (mods[0], ng(0, 1), 3, 4, BF16))
    w_in_t = fetch(2, hc)["w_in_t"]
    n_proj = w_in_t.shape[1]
    zall = _mm([(hc, w_in_t, 0, 0)], NT, n_proj, F32, "proj", tm_pref=TALL_TILES,
               tn_pref=(n_proj,))
    offs = np.cumsum((0,) + PROJ_SIZES)
    part = lambda i, rows=slice(None): zall[rows, offs[i]:offs[i + 1]]
    lat, con = slice(0, t), slice(t, tc)
    cos, sin = _rope_tables(t)
    qa = _heads(_rope(zall, cos, sin, False, "rope_q", view=(A_Q, int(offs[0]) // A_Q)), A_HEADS)
    ka = _heads(_rope(zall, cos, sin, False, "rope_k", view=(A_KV, int(offs[1]) // A_KV)), A_KV_HEADS)
    va = _heads(part(2, lat), A_KV_HEADS)
    kca, vca = _heads(part(1, con), A_KV_HEADS), _heads(part(2, con), A_KV_HEADS)
    kap, vap = _pad_rows(ka, WINDOW), _pad_rows(va, WINDOW)
    sink = wts["sink"].reshape(A_HEADS, 1, 1)
    o_a = _attn_fwd(qa, kap, vap, kca, vca, sink, "attn_fwd")

    vb = part(5)
    qk_cols = (int(offs[3]) // B_QK, int(offs[4]) // B_QK)
    rb = part(6, lat)
    zg = part(7)
    zg_f, zg_b = zg[:, :B_GATE_RANK], zg[:, B_GATE_RANK:]
    w2f, w2b, b2f, b2b = wts["w_a2_f"], wts["w_a2_b"], wts["b_a_f"], wts["b_a_b"]
    la_f = _gate_fwd(zg_f, w2f, b2f, "gate_f")
    la_b = _gate_fwd(zg_b, w2b, b2b, "gate_b")
    o_f, s_f, o_b, s_b = _gla_fwd(zall, zall, vb, la_f, la_b, t, "gla_fwd", qk_cols=qk_cols)
    gla_g = wts["gla_g"]
    go = _gla_out_fwd(o_f, o_b, rb, gla_g, "gla_out")
    cat = jnp.concatenate([_unheads(o_a), go], axis=-1).astype(BF16)
    big = fetch(3, cat)
    w_out, wi2_0, wo2_0 = big["w_out"], big["wi2_0"], big["wo2_0"]
    x2, y_mix0, h2 = _mm_resid(cat, w_out, 0, xc1, mods[0], 5, 1.0, t, "w_out", nxt=(mods[0], ng(0, 2), 6, 7, BF16))
    x3, sv_f2, h3 = ffn_fwd(x2, h2, 0, 6, wi2_0, wo2_0, t, "l0f2", (mods[1], ng(1, 0), 0, 1, BF16))

    big = fetch(4, x3)
    wi1_1, wo1_1, wi2_1, wo2_1 = big["wi1_1"], big["wo1_1"], big["wi2_1"], big["wo2_1"]
    x4, sv_g1, hp = ffn_fwd(x3, h3, 1, 0, wi1_1, wo1_1, t, "l1f1", (mods[1], ng(1, 1), 3, 4, F32))
    w_pool, pool_scale = wts["w_pool"], wts["pool_scale"]
    x5, pooled, ypre = _pool_fwd(hp, w_pool, pool_scale, x4, mods[1], 5, "pool_fwd")
    h5 = _modulate(x5, mods[1], ng(1, 2), 6, 7, t, BF16, "mod_l1f2")
    x6, sv_g2, _ = ffn_fwd(x5, h5, 1, 6, wi2_1, wo2_1, t, "l1f2", None)

    y_of = lambda saved: saved[4]
    dx6, loss_vec, dfinal_g, dy, acc_gate = _final_loss(x6, wts["final_g"], target, (y_of(sv_g2), mods[1], 8, 0.5),
                                                        "final_loss")
    grads["final_g"] = dfinal_g[0]

    dx5, acc_mod = ffn_bwd(dx6, dy, sv_g2, 1, 6, ng(1, 2), t, "l1f2", 0)
    record(1, 6, 2, ng(1, 2), acc_mod, acc_gate, 1)
    dhp, dw_pool, acc_pool = _pool_bwd(dx5, w_pool, pool_scale, pooled, ypre, mods[1], 5, "pool_bwd")
    grads["pool_scale"] = acc_pool[0]
    dmods[1][0][5] = acc_pool[1]
    dx4, acc_mod, dy, acc_gate = _modulate_bwd(x4, dhp, dx5, mods[1], ng(1, 1), 4, t, "mod_bwd_l1mix",
                                               then=(y_of(sv_g1), mods[1], 2, 0.5))
    record(1, 3, 1, ng(1, 1), acc_mod, None, 1)
    dx3, acc_mod, dy, acc_gate_next = ffn_bwd(dx4, dy, sv_g1, 1, 0, ng(1, 0), t, "l1f1", 1,
                                              then=(y_of(sv_f2), mods[0], 8, 0.5))
    record(1, 0, 0, ng(1, 0), acc_mod, acc_gate, 1)

    dx2, acc_mod, dymix, acc_gate_mix = ffn_bwd(dx3, dy, sv_f2, 0, 6, ng(0, 2), t, "l0f2", 2,
                                                then=(y_mix0, mods[0], 5, 1.0))
    record(0, 6, 2, ng(0, 2), acc_mod, acc_gate_next, 1)
    dmods[0][0][5] = acc_gate_mix[0, 0]
    dw_out = _mm_tn(cat, dymix, BF16, "dw_out")
    dcat = _mm([(dymix, w_out, 0, 0)], NT, cat.shape[1], F32, "dcat")
    do_a = _heads(dcat[:, :A_Q], A_HEADS)
    do_full, drb, dgla_g = _gla_out_bwd(o_f, o_b, rb, gla_g, dcat[:, A_Q:], "gla_out_bwd")
    grads["gla_g"] = dgla_g[0]
    dq_f, dk_f, dv_f, dla_f, dq_b, dk_b, dv_b, dla_b = _gla_bwd(zall, zall, vb, la_f, la_b, s_f, s_b, do_full, t,
                                                                "gla_bwd", qk_cols=qk_cols)
    dzg_f, dw2f, db2f = _gate_bwd(zg_f, w2f, b2f, dla_f, "gate_bwd_f")
    dzg_b, dw2b, db2b = _gate_bwd(zg_b, w2b, b2b, dla_b, "gate_bwd_b")
    grads.update(w_a2_f=dw2f, w_a2_b=dw2b, b_a_f=db2f[0], b_a_b=db2b[0])
    dqa_r, dkap, dvap, dkca, dvca, dsink = _attn_bwd(qa, kap, vap, kca, vca, sink, o_a, do_a, "attn_bwd")
    grads["sink"] = dsink[:, 0, 0]
    dqa = _rope(_unheads(dqa_r), cos, sin, True, "rope_bwd_q")
    dka = _rope(_unheads(dkap[:, WINDOW:WINDOW + t]), cos, sin, True, "rope_bwd_k")
    dva = dvap[:, WINDOW:WINDOW + t]
    dzg = jnp.concatenate([dzg_f, dzg_b, jnp.zeros((tc, n_proj - PROJ_DIM), F32)], axis=-1)
    dzall = _assemble_dz(
        [dqa, dka, _unheads(dva), None, None, None, drb, None],
        [None, _unheads(dkca), _unheads(dvca), None, None, None, None, None],
        [None, None, None, [dq_f, dq_b], [dk_f, dk_b], [dv_f, dv_b], None, [dzg]], n_proj, "assemble_dz")
    dw_in_t = _mm_tn(dzall, hc, BF16, "dw_in")
    token = emit(3, [dw_in_t, dw_out, dw_pool])
    dhc = _mm([(dzall, w_in_t, 0, 0)], NN, d, F32, "dhc", tm_pref=TALL_TILES, dep=token)
    dxc1, acc_mod, dy, acc_gate = _modulate_bwd(xc1, dhc, dx2, mods[0], ng(0, 1), 4, t, "mod_bwd_l0mix",
                                                then=(y_of(sv_f1), mods[0], 2, 0.5))
    record(0, 3, 1, ng(0, 1), acc_mod, None, 2)
    dxc0, acc_mod = ffn_bwd(dxc1, dy, sv_f1, 0, 0, ng(0, 0), t, "l0f1", 4, split=True)
    record(0, 0, 0, ng(0, 0), acc_mod, acc_gate, 2)

    grads["norm_g"] = jnp.stack([jnp.stack(dnorm[0]), jnp.stack(dnorm[1])])
    zero = jnp.zeros((d,), F32)
    dmods_arr = jnp.stack([jnp.stack([jnp.stack([v if v is not None else zero for v in dmods[l][s]])
                                      for s in range(2)]) for l in range(2)])
    return loss_vec, dxc0, grads, dmods_arr


def _pack(parts):
    flat = jnp.concatenate([p.reshape(-1).astype(F32) for p in parts])
    pad = (-flat.shape[0]) % 128
    return jnp.pad(flat, (0, pad))[None, :]


def _unpack(rows, shapes):
    out, off = [], 0
    for s in shapes:
        n = int(np.prod(s))
        out.append(rows[:, off:off + n].reshape((rows.shape[0],) + tuple(s)))
        off += n
    return out


def _cols_to_full(g):
    g = jnp.moveaxis(g, 0, -2)
    return g.reshape(g.shape[:-2] + (-1,))


def kernel(x, c, ctx, c_ctx, w_mod, b_mod, norm_g, ffn1_wi, ffn1_wo, ffn2_wi, ffn2_wo, w_in, w_a2_f, b_a_f, w_a2_b, b_a_b, sink, gla_g, w_out, w_pool, pool_scale, final_g, loss_target, m_c_ctx, m_w_mod, m_b_mod, m_norm_g, m_ffn1_wi, m_ffn1_wo, m_ffn2_wi, m_ffn2_wo, m_w_in, m_w_a2_f, m_b_a_f, m_w_a2_b, m_b_a_b, m_sink, m_gla_g, m_w_out, m_w_pool, m_pool_scale, m_final_g, v_c_ctx, v_w_mod, v_b_mod, v_norm_g, v_ffn1_wi, v_ffn1_wo, v_ffn2_wi, v_ffn2_wo, v_w_in, v_w_a2_f, v_b_a_f, v_w_a2_b, v_b_a_b, v_sink, v_gla_g, v_w_out, v_w_pool, v_pool_scale, v_final_g):
    t, d = x.shape[1], x.shape[2]
    me = _dev_index()
    nc = w_mod.shape[2]
    ncol_in = w_in.shape[2]
    ncol_pad = -(-ncol_in // 16) * 16

    small_shapes = [(d,), norm_g.shape, pool_scale.shape, w_a2_f.shape, w_a2_b.shape, w_pool.shape]
    g1 = _gather_small(_pack([c, norm_g, pool_scale, w_a2_f, w_a2_b, w_pool]), "gather_params")
    c_all, norm_g_all, pool_scale_all, w2f_all, w2b_all, w_pool_all = _unpack(g1, small_shapes)
    wts = {
        "norm_g": _cols_to_full(norm_g_all),
        "pool_scale": _cols_to_full(pool_scale_all),
        "w_a2_f": _cols_to_full(w2f_all)[0],
        "w_a2_b": _cols_to_full(w2b_all)[0],
        "w_pool": jnp.moveaxis(w_pool_all[:, 0], 0, 1).reshape(w_pool.shape[1], -1, w_pool.shape[3]),
        "b_a_f": b_a_f, "b_a_b": b_a_b, "sink": sink[0], "gla_g": gla_g, "final_g": final_g[None, :],
    }

    craw = jnp.concatenate([c_all, c_ctx[None, :], jnp.zeros((16 - N_DEV - 1, d), F32)], axis=0)
    b_cols = lax.dynamic_slice_in_dim(b_mod, me * nc, nc, axis=1)[:, None, :]
    mm_cols = _adaln_fwd(craw, w_mod, b_cols, "adaln_fwd")
    g2 = _gather_small(mm_cols.reshape(1, -1), "gather_mods").reshape(N_DEV, 2, 16, nc)
    mm_full = jnp.moveaxis(g2, 0, 2).reshape(2, 16, N_MOD, d)
    mods = jnp.stack([lax.dynamic_index_in_dim(mm_full, me, axis=1, keepdims=False), mm_full[:, N_DEV]], axis=1)
    mods = jnp.pad(mods, ((0, 0), (0, 0), (0, 16 - N_MOD), (0, 0)))

    tr = lambda w: jnp.swapaxes(w, 1, 2).astype(BF16)
    wi1_sh, wi2_sh, wo1_sh, wo2_sh = tr(ffn1_wi), tr(ffn2_wi), ffn1_wo.astype(BF16), ffn2_wo.astype(BF16)
    w_in_sh = jnp.pad(tr(w_in), ((0, 0), (0, ncol_pad - ncol_in), (0, 0)))
    groups = [
        {"wi1_0": wi1_sh[0:1]},
        {"wo1_0": wo1_sh[0:1]},
        {"w_in": w_in_sh},
        {"w_out": w_out.astype(BF16), "wi2_0": wi2_sh[0:1], "wo2_0": wo2_sh[0:1]},
        {"wi1_1": wi1_sh[1:2], "wo1_1": wo1_sh[1:2], "wi2_1": wi2_sh[1:2], "wo2_1": wo2_sh[1:2]},
    ]

    reach = lambda gi: NEAR_PEERS if gi == 0 else N_DEV - 1
    gathers, token = [], mods
    for gi, grp in enumerate(groups):
        lands = [_place_shard(s, me, f"gather_place_{nm}") for nm, s in grp.items()]
        gathers.append(_exchange_start(list(grp.values()), lands, True, 1 + gi, token, f"gather_start_{gi}",
                                       n_peers=reach(gi)))
        token = gathers[-1][4]
    n_proj = -(-(N_DEV * ncol_in) // 128) * 128
    forward_id = 1 + len(groups) + 6

    def fetch(gi, after):
        _, lands = _exchange_wait(gathers[gi], True, token if after is None else after, f"gather_wait_{gi}",
                                  n_peers=reach(gi))
        if gi == 0:
            rows = [s.shape[1] for s in groups[gi].values()]
            passed = _forward_start(lands, rows, forward_id, "gather_forward")
            lands = _forward_wait(passed, rows, passed[3], "gather_forward_wait")
        out = dict(zip(groups[gi].keys(), lands))
        if "w_in" in out:
            w_in_t = out.pop("w_in").reshape(1, N_DEV, ncol_pad, d)[:, :, :ncol_in].reshape(1, N_DEV * ncol_in, d)
            out["w_in_t"] = jnp.pad(w_in_t, ((0, 0), (0, n_proj - N_DEV * ncol_in), (0, 0)))
        return out

    scatters = []

    def emit(stage, arrays):
        if stage == 3:
            dw_in_t, dw_out, dw_pool = arrays
            dw_in_full = dw_in_t[:N_DEV * ncol_in].reshape(N_DEV, ncol_in, d)
            dw_in_full = jnp.pad(dw_in_full, ((0, 0), (0, ncol_pad - ncol_in), (0, 0)))
            srcs = [dw_in_full.reshape(1, N_DEV * ncol_pad, d), dw_out[None], dw_pool.astype(BF16)]
        else:
            srcs = [a[None] for a in arrays]
        lands = [lax.empty((N_DEV, s.shape[0], s.shape[1] // N_DEV, s.shape[2]), s.dtype) for s in srcs]
        scatters.append(_exchange_start(srcs, lands, False, 1 + len(groups) + stage, None, f"scatter_start_{stage}"))
        return scatters[-1][4]

    loss_vec, grad_x, grads, dmods = _local_step(x[0], ctx[0], loss_target[0], mods, wts, fetch, emit)
    loss = lax.psum(jnp.sum(loss_vec), ("x", "y", "c"))

    def reduce_stage(stage, after):
        wholes, lands = _exchange_wait(scatters[stage], False, after, f"scatter_wait_{stage}")
        return [_sum_slots(ld, wh, me, f"sum_grad_{stage}_{i}") for i, (ld, wh) in enumerate(zip(lands, wholes))]

    (dwi2_1, dwo2_1), (dwi1_1, dwo1_1), (dwi2_0, dwo2_0), (dw_in_s, dw_out_s, dw_pool_s) = [
        reduce_stage(stage, grad_x) for stage in range(4)]
    back = lambda g: jnp.swapaxes(g, 1, 2)
    g_big = {
        "ffn2_wi": back(jnp.concatenate([dwi2_0, dwi2_1], axis=0)), "ffn2_wo": jnp.concatenate([dwo2_0, dwo2_1], axis=0),
        "w_in": back(dw_in_s[:, :ncol_in]), "w_out": dw_out_s, "w_pool": dw_pool_s[None],
    }

    order = ["c_ctx", "w_mod", "b_mod", "norm_g", "ffn1_wi", "ffn1_wo", "ffn2_wi", "ffn2_wo", "w_in", "w_a2_f", "b_a_f",
             "w_a2_b", "b_a_b", "sink", "gla_g", "w_out", "w_pool", "pool_scale", "final_g"]
    ws = dict(c_ctx=c_ctx, w_mod=w_mod, b_mod=b_mod, norm_g=norm_g, ffn1_wi=ffn1_wi, ffn1_wo=ffn1_wo, ffn2_wi=ffn2_wi,
              ffn2_wo=ffn2_wo, w_in=w_in, w_a2_f=w_a2_f, b_a_f=b_a_f, w_a2_b=w_a2_b, b_a_b=b_a_b, sink=sink, gla_g=gla_g,
              w_out=w_out, w_pool=w_pool, pool_scale=pool_scale, final_g=final_g)
    ms = dict(c_ctx=m_c_ctx, w_mod=m_w_mod, b_mod=m_b_mod, norm_g=m_norm_g, ffn1_wi=m_ffn1_wi, ffn1_wo=m_ffn1_wo,
              ffn2_wi=m_ffn2_wi, ffn2_wo=m_ffn2_wo, w_in=m_w_in, w_a2_f=m_w_a2_f, b_a_f=m_b_a_f, w_a2_b=m_w_a2_b,
              b_a_b=m_b_a_b, sink=m_sink, gla_g=m_gla_g, w_out=m_w_out, w_pool=m_w_pool, pool_scale=m_pool_scale,
              final_g=m_final_g)
    vs = dict(c_ctx=v_c_ctx, w_mod=v_w_mod, b_mod=v_b_mod, norm_g=v_norm_g, ffn1_wi=v_ffn1_wi, ffn1_wo=v_ffn1_wo,
              ffn2_wi=v_ffn2_wi, ffn2_wo=v_ffn2_wo, w_in=v_w_in, w_a2_f=v_w_a2_f, b_a_f=v_b_a_f, w_a2_b=v_w_a2_b,
              b_a_b=v_b_a_b, sink=v_sink, gla_g=v_gla_g, w_out=v_w_out, w_pool=v_w_pool, pool_scale=v_pool_scale,
              final_g=v_final_g)
    early, late = ["ffn2_wi", "ffn2_wo", "w_out", "w_in", "w_pool"], ["ffn1_wi", "ffn1_wo"]
    big = early + ["w_mod"] + late
    delta, new_m, new_v = {}, {}, {}
    g_all = dict(g_big)

    def adamw_big(nm):
        shp = ws[nm].shape
        two_d = lambda a: a.reshape(-1, shp[-1])
        dl, nm_, nv_ = _adamw(two_d(ws[nm]), two_d(g_all[nm]), two_d(ms[nm]), two_d(vs[nm]), f"adamw_{nm}")
        delta[nm], new_m[nm], new_v[nm] = dl.reshape(shp), nm_.reshape(shp), nv_.reshape(shp)

    for nm in early:
        adamw_big(nm)

    small_g = [dmods[:, :, :N_MOD].reshape(2, 2, N_MOD * d), grads["norm_g"], grads["pool_scale"], grads["final_g"],
               grads["b_a_f"], grads["b_a_b"], grads["sink"], grads["gla_g"], grads["w_a2_f"], grads["w_a2_b"]]
    small_g_shapes = [a.shape for a in small_g]
    g3 = _gather_small(_pack(small_g), "gather_small_grads", dep=delta["w_out"])
    total = _sum_rows8(g3, "sum_small_grads")
    dmm_all = _unpack(g3, small_g_shapes[:1])[0]
    (dmm_sum, dnorm_g, dpool_scale, dfinal_g, db_a_f, db_a_b, dsink, dgla_g, dw_a2_f, dw_a2_b) = [
        a[0] for a in _unpack(total, small_g_shapes)]
    dmm_rows = jnp.concatenate([dmm_all[:, :, 0].transpose(1, 0, 2), dmm_sum[:, 1][:, None, :],
                                jnp.zeros((2, 16 - N_DEV - 1, N_MOD * d), F32)], axis=1)
    grad_b_mod = dmm_sum[:, 0] + dmm_sum[:, 1]
    dmm_cols = lax.dynamic_slice_in_dim(dmm_rows, me * nc, nc, axis=2)
    cs_t = jnp.transpose(_silu(craw)).astype(BF16)
    grad_w_mod, dcraw = _adaln_bwd(craw, cs_t, dmm_cols, w_mod, "adaln_bwd")
    g4 = _gather_small((dcraw[0, N_DEV] + dcraw[1, N_DEV])[None, :], "gather_c_ctx_grad")
    grad_c_ctx = _sum_rows8(g4, "sum_c_ctx_grad")[0]

    col = lambda v, n: lax.dynamic_slice_in_dim(v, me * n, n, axis=v.ndim - 1)
    g_small = {
        "c_ctx": grad_c_ctx, "b_mod": grad_b_mod, "norm_g": col(dnorm_g, norm_g.shape[2]),
        "w_a2_f": col(dw_a2_f, w_a2_f.shape[2])[None], "b_a_f": db_a_f[None], "w_a2_b": col(dw_a2_b, w_a2_b.shape[2])[None],
        "b_a_b": db_a_b[None], "sink": dsink[None], "gla_g": dgla_g[None], "pool_scale": col(dpool_scale, pool_scale.shape[1])[None],
        "final_g": dfinal_g,
    }
    g_all.update(g_small, w_mod=grad_w_mod)
    adamw_big("w_mod")
    rest = [nm for nm in order if nm not in big]
    rest_shapes = [ws[nm].shape for nm in rest]
    packed = [_pack([d_[nm].reshape(ws[nm].shape) for nm in rest]).reshape(-1, 128) for d_ in (ws, g_all, ms, vs)]
    pad_rows = (-packed[0].shape[0]) % 512
    packed = [jnp.pad(p, ((0, pad_rows), (0, 0))) for p in packed]
    outs = _adamw(*packed, "adamw_small")
    for dst, arr in zip((delta, new_m, new_v), outs):
        for nm, val in zip(rest, _unpack(arr.reshape(1, -1), rest_shapes)):
            dst[nm] = val[0]

    (dwo1_0,), (dwi1_0,) = reduce_stage(4, outs[0]), reduce_stage(5, outs[0])
    g_all["ffn1_wi"] = back(jnp.concatenate([dwi1_0, dwi1_1], axis=0))
    g_all["ffn1_wo"] = jnp.concatenate([dwo1_0, dwo1_1], axis=0)
    for nm in late:
        adamw_big(nm)
    g_all = {nm: g_all[nm].reshape(ws[nm].shape) for nm in order}

    return (loss, grad_x[None], *[g_all[nm] for nm in order], *[delta[nm] for nm in order],
            *[new_m[nm] for nm in order], *[new_v[nm] for nm in order])
```

```python
import functools

import numpy as np
import jax
import jax.numpy as jnp
from jax import lax
from jax.experimental import pallas as pl
from jax.experimental.pallas import tpu as pltpu

F32 = jnp.float32
BF16 = jnp.bfloat16
MESH = pl.DeviceIdType.MESH

N_DEV = 8
RMS_EPS = 1e-6
N_MOD = 9
GRID_W = 64
A_HEADS, A_KV_HEADS, A_HEAD_DIM = 8, 2, 64
A_REP = A_HEADS // A_KV_HEADS
WINDOW = 128
ROPE_BASE = 10000.0
B_HEADS, B_DK, B_DV = 4, 64, 128
B_GATE_RANK = 16
B_GATE_NORM = 16.0
B_CHUNK = 64
POOL_WINDOWS = (2, 4, 8, 16)
POOL_PAD = 8
A_Q = A_HEADS * A_HEAD_DIM
A_KV = A_KV_HEADS * A_HEAD_DIM
B_QK = B_HEADS * B_DK
B_V = B_HEADS * B_DV
PROJ_SIZES = (A_Q, A_KV, A_KV, B_QK, B_QK, B_V, B_V, 2 * B_GATE_RANK)
PROJ_DIM = sum(PROJ_SIZES)
ADAM_LR, ADAM_B1, ADAM_B2, ADAM_EPS, ADAM_WD, ADAM_STEP = 0.001, 0.9, 0.999, 1e-08, 0.01, 10

VMEM_LIMIT = 56 * 1024 * 1024
ROW_TILES = (512, 544, 256, 128, 64, 32, 16, 8)
TALL_TILES = (1024, 1088) + ROW_TILES

NN = ((1,), (0,))
NT = ((1,), (1,))
TN = ((0,), (0,))


def _dot(a, b, dims=NN, prec=None):
    return lax.dot_general(a, b, (dims, ((), ())), precision=prec, preferred_element_type=F32)


def _bdot(a, b, dims=NN):
    return _dot(a.astype(BF16), b.astype(BF16), dims)


def _dot_01(sel, x):
    hi = x.astype(BF16)
    rest = x - hi.astype(F32)
    mid = rest.astype(BF16)
    lo = (rest - mid.astype(F32)).astype(BF16)
    sel = sel.astype(BF16)
    return _dot(sel, hi) + _dot(sel, mid) + _dot(sel, lo)


def _params(sem=None, **kw):
    return pltpu.CompilerParams(dimension_semantics=sem, vmem_limit_bytes=VMEM_LIMIT, **kw)


def _silu(a):
    return a * jax.nn.sigmoid(a)


def _pick(n, prefs):
    for p in prefs:
        if n % p == 0:
            return p
    return n


def _full(shape):
    nd = len(shape)
    return pl.BlockSpec(shape, lambda *_: (0,) * nd)


def _peers():
    x, y, c = lax.axis_index("x"), lax.axis_index("y"), lax.axis_index("c")
    return x, y, c


def _dev_index():
    x, y, c = _peers()
    return 4 * x + 2 * y + c


def _others(x, y, c):
    return [(x, y, 1 - c), (1 - x, y, c), (x, 1 - y, c), (1 - x, 1 - y, c),
            (1 - x, y, 1 - c), (x, 1 - y, 1 - c), (1 - x, 1 - y, 1 - c)]


def _index_of(dev):
    return 4 * dev[0] + 2 * dev[1] + dev[2]


def _exchange_refs(gather, shapes, srcs, lands, a, me, to):
    if gather:
        r = shapes[a][1]
        return srcs[a], lands[a].at[:, pl.ds(_index_of(me) * r, r), :]
    r = shapes[a][1] // N_DEV
    return srcs[a].at[:, pl.ds(_index_of(to) * r, r), :], lands[a].at[_index_of(me)]


HBM_SPEC = pl.BlockSpec(memory_space=pltpu.HBM)
SEM_SPEC = pl.BlockSpec(memory_space=pltpu.SEMAPHORE)
EFFECT = pltpu.SideEffectType.DATAFLOW_SIDE_EFFECTING


NEAR_PEERS = 4


def _exchange_start(srcs, lands, gather, collective_id, dep, name, n_peers=N_DEV - 1):
    n = len(srcs)
    shapes = [s.shape for s in srcs]
    deps = [] if dep is None else [dep]

    def body(*refs):
        src_refs, land_refs = refs[:n], refs[n:2 * n]
        send_sems, recv_sems = refs[2 * n + len(deps)], refs[2 * n + len(deps) + 1]
        token = refs[-1]
        x, y, c = _peers()
        others = _others(x, y, c)[:n_peers]
        barrier = pltpu.get_barrier_semaphore()
        for peer in others:
            pl.semaphore_signal(barrier, inc=1, device_id=peer, device_id_type=MESH)
        pl.semaphore_wait(barrier, len(others))
        for a in range(n):
            for k, to in enumerate(others):
                src, dst = _exchange_refs(gather, shapes, src_refs, land_refs, a, (x, y, c), to)
                pltpu.make_async_remote_copy(src_ref=src, dst_ref=dst, send_sem=send_sems.at[7 * a + k],
                                             recv_sem=recv_sems.at[7 * a + k], device_id=to, device_id_type=MESH).start()
        token[...] = jnp.zeros_like(token)

    outs = pl.pallas_call(
        body, name=name,
        out_shape=(pltpu.SemaphoreType.DMA((7 * n,)), pltpu.SemaphoreType.DMA((7 * n,)),
                   *[pltpu.HBM(s.shape, s.dtype) for s in srcs], *[pltpu.HBM(l.shape, l.dtype) for l in lands],
                   jax.ShapeDtypeStruct((8, 128), F32)),
        in_specs=[HBM_SPEC] * (2 * n) + [pl.BlockSpec(memory_space=pl.ANY)] * len(deps),
        out_specs=(SEM_SPEC, SEM_SPEC, *[HBM_SPEC] * (2 * n), pl.BlockSpec(memory_space=pltpu.VMEM)),
        input_output_aliases={i: 2 + i for i in range(2 * n)},
        compiler_params=pltpu.CompilerParams(has_side_effects=EFFECT, collective_id=collective_id),
    )(*[pltpu.with_memory_space_constraint(s, pltpu.HBM) for s in srcs],
      *[pltpu.with_memory_space_constraint(l, pltpu.HBM) for l in lands], *deps)
    return outs[0], outs[1], list(outs[2:2 + n]), list(outs[2 + n:2 + 2 * n]), outs[-1]


def _exchange_wait(started, gather, after, name, n_peers=N_DEV - 1):
    send_sems, recv_sems, srcs, lands, _ = started
    n = len(srcs)
    shapes = [s.shape for s in srcs]

    def body(*refs):
        src_refs, land_refs = refs[:n], refs[n:2 * n]
        send_sems, recv_sems = refs[2 * n], refs[2 * n + 1]
        x, y, c = _peers()
        for a in range(n):
            for k, peer in enumerate(_others(x, y, c)[:n_peers]):
                src, _ = _exchange_refs(gather, shapes, src_refs, land_refs, a, (x, y, c), peer)
                _, dst = _exchange_refs(gather, shapes, src_refs, land_refs, a, peer, (x, y, c))
                copy = pltpu.make_async_remote_copy(src_ref=src, dst_ref=dst, send_sem=send_sems.at[7 * a + k],
                                                    recv_sem=recv_sems.at[7 * a + k], device_id=peer, device_id_type=MESH)
                copy.wait_send()
                copy.wait_recv()

    outs = pl.pallas_call(
        body, name=name,
        out_shape=(*[pltpu.HBM(s.shape, s.dtype) for s in srcs], *[pltpu.HBM(l.shape, l.dtype) for l in lands]),
        in_specs=[HBM_SPEC] * (2 * n) + [SEM_SPEC, SEM_SPEC, pl.BlockSpec(memory_space=pl.ANY)],
        out_specs=tuple([HBM_SPEC] * (2 * n)),
        input_output_aliases={i: i for i in range(2 * n)},
        compiler_params=pltpu.CompilerParams(has_side_effects=EFFECT),
    )(*srcs, *lands, send_sems, recv_sems, after)
    return list(outs[:n]), list(outs[n:])


def _forward_refs(land_refs, rows, a, others, j, received):
    origin = others[j + 3] if received else others[j]
    return land_refs[a].at[:, pl.ds(_index_of(origin) * rows[a], rows[a]), :]


def _forward_start(lands, rows, collective_id, name):
    n = len(lands)

    def body(*refs):
        land_refs, send_sems, recv_sems, token = refs[:n], refs[n], refs[n + 1], refs[-1]
        x, y, c = _peers()
        others = _others(x, y, c)
        barrier = pltpu.get_barrier_semaphore()
        pl.semaphore_signal(barrier, inc=1, device_id=others[0], device_id_type=MESH)
        pl.semaphore_wait(barrier, 1)
        for a in range(n):
            for j in (1, 2, 3):
                blk = _forward_refs(land_refs, rows, a, others, j, False)
                pltpu.make_async_remote_copy(src_ref=blk, dst_ref=blk, send_sem=send_sems.at[3 * a + j - 1],
                                             recv_sem=recv_sems.at[3 * a + j - 1], device_id=others[0],
                                             device_id_type=MESH).start()
        token[...] = jnp.zeros_like(token)

    outs = pl.pallas_call(
        body, name=name,
        out_shape=(pltpu.SemaphoreType.DMA((3 * n,)), pltpu.SemaphoreType.DMA((3 * n,)),
                   *[pltpu.HBM(l.shape, l.dtype) for l in lands], jax.ShapeDtypeStruct((8, 128), F32)),
        in_specs=[HBM_SPEC] * n,
        out_specs=(SEM_SPEC, SEM_SPEC, *[HBM_SPEC] * n, pl.BlockSpec(memory_space=pltpu.VMEM)),
        input_output_aliases={i: 2 + i for i in range(n)},
        compiler_params=pltpu.CompilerParams(has_side_effects=EFFECT, collective_id=collective_id),
    )(*[pltpu.with_memory_space_constraint(l, pltpu.HBM) for l in lands])
    return outs[0], outs[1], list(outs[2:2 + n]), outs[-1]


def _forward_wait(started, rows, after, name):
    send_sems, recv_sems, lands, _ = started
    n = len(lands)

    def body(*refs):
        land_refs, send_sems, recv_sems = refs[:n], refs[n], refs[n + 1]
        x, y, c = _peers()
        others = _others(x, y, c)
        for a in range(n):
            for j in (1, 2, 3):
                copy = pltpu.make_async_remote_copy(
                    src_ref=_forward_refs(land_refs, rows, a, others, j, False),
                    dst_ref=_forward_refs(land_refs, rows, a, others, j, True), send_sem=send_sems.at[3 * a + j - 1],
                    recv_sem=recv_sems.at[3 * a + j - 1], device_id=others[0], device_id_type=MESH)
                copy.wait_send()
                copy.wait_recv()

    outs = pl.pallas_call(
        body, name=name, out_shape=tuple(pltpu.HBM(l.shape, l.dtype) for l in lands),
        in_specs=[HBM_SPEC] * n + [SEM_SPEC, SEM_SPEC, pl.BlockSpec(memory_space=pl.ANY)],
        out_specs=tuple([HBM_SPEC] * n), input_output_aliases={i: i for i in range(n)},
        compiler_params=pltpu.CompilerParams(has_side_effects=EFFECT),
    )(*lands, send_sems, recv_sems, after)
    return list(outs)


def _place_shard(shard, me, name):
    a_, r, c = shard.shape
    tr = _pick(r, (352, 304, 256, 128, 64, 32, 16, 8))
    nr = r // tr

    def body(me_ref, in_ref, out_ref):
        out_ref[...] = in_ref[...]

    return pl.pallas_call(
        body, name=name,
        grid_spec=pltpu.PrefetchScalarGridSpec(
            num_scalar_prefetch=1, grid=(a_, nr),
            in_specs=[pl.BlockSpec((None, tr, c), lambda i, j, me_ref: (i, j, 0))],
            out_specs=pl.BlockSpec((None, tr, c), lambda i, j, me_ref: (i, me_ref[0] * nr + j, 0))),
        out_shape=jax.ShapeDtypeStruct((a_, N_DEV * r, c), shard.dtype),
        compiler_params=_params(("parallel", "parallel")),
    )(me.reshape(1).astype(jnp.int32), shard)


def _sum_slots(land, whole, me, name):
    _, a_, r, c = land.shape
    tr = _pick(r, (352, 256, 128, 64, 32, 16, 8))
    nr = r // tr

    def body(me_ref, land_ref, own_ref, out_ref):
        acc = None
        for s in range(N_DEV):
            part = jnp.where(me_ref[0] == s, own_ref[...], land_ref[s]).astype(F32)
            acc = part if acc is None else acc + part
        out_ref[...] = acc

    return pl.pallas_call(
        body, name=name,
        grid_spec=pltpu.PrefetchScalarGridSpec(
            num_scalar_prefetch=1, grid=(a_, nr),
            in_specs=[pl.BlockSpec((N_DEV, None, tr, c), lambda i, j, me_ref: (0, i, j, 0)),
                      pl.BlockSpec((None, tr, c), lambda i, j, me_ref: (i, me_ref[0] * nr + j, 0))],
            out_specs=pl.BlockSpec((None, tr, c), lambda i, j, me_ref: (i, j, 0))),
        out_shape=jax.ShapeDtypeStruct((a_, r, c), F32),
        compiler_params=_params(("parallel", "parallel")),
    )(me.reshape(1).astype(jnp.int32), land, whole)


def _gather_small(vec, name, dep=None):
    p = vec.shape[1]
    pp = -(-p // 1024) * 1024
    blk = jnp.pad(vec, ((0, 0), (0, pp - p))).reshape(8, pp // 8)
    deps = [] if dep is None else [dep]

    def body(in_ref, *rest):
        out_ref, send_sems, recv_sems = rest[-3:]
        x, y, c = _peers()
        me = 4 * x + 2 * y + c
        others = [(x, y, 1 - c), (1 - x, y, c), (x, 1 - y, c), (1 - x, 1 - y, c),
                  (1 - x, y, 1 - c), (x, 1 - y, 1 - c), (1 - x, 1 - y, 1 - c)]

        def rows(idx):
            return out_ref.at[pl.ds(pl.multiple_of(idx * 8, 8), 8), :]

        out_ref[pl.ds(pl.multiple_of(me * 8, 8), 8), :] = in_ref[...]

        def copy(k, dev, slot):
            return pltpu.make_async_remote_copy(
                src_ref=in_ref, dst_ref=rows(slot), send_sem=send_sems.at[k], recv_sem=recv_sems.at[k],
                device_id=dev, device_id_type=MESH)

        sends = [copy(k, dev, me) for k, dev in enumerate(others)]
        for cp in sends:
            cp.start()
        for k, dev in enumerate(others):
            copy(k, dev, 4 * dev[0] + 2 * dev[1] + dev[2]).wait_recv()
        for cp in sends:
            cp.wait_send()

    vm = pl.BlockSpec(memory_space=pltpu.VMEM)
    out = pl.pallas_call(
        body, name=name, out_shape=jax.ShapeDtypeStruct((8 * N_DEV, pp // 8), F32),
        in_specs=[vm] + [pl.BlockSpec(memory_space=pl.ANY)] * len(deps), out_specs=vm,
        scratch_shapes=[pltpu.SemaphoreType.DMA((7,)), pltpu.SemaphoreType.DMA((7,))],
        compiler_params=pltpu.CompilerParams(has_side_effects=True, vmem_limit_bytes=VMEM_LIMIT),
    )(blk, *deps)
    return out.reshape(N_DEV, pp)[:, :p]


def _sum_rows8(g, name):
    p = g.shape[1]

    def body(in_ref, out_ref):
        acc = in_ref[0:1, :]
        for s in range(1, N_DEV):
            acc = acc + in_ref[s:s + 1, :]
        out_ref[...] = acc

    return pl.pallas_call(body, name=name, out_shape=jax.ShapeDtypeStruct((1, p), F32),
                          compiler_params=_params())(g)


def _sel_row(mods_ref, is_ctx, k):
    return jnp.where(is_ctx, mods_ref[1, k:k + 1, :], mods_ref[0, k:k + 1, :])


def _modulate(z, mods, g, ks, kc, n_x, out_dtype, name):
    m, d = z.shape
    tm = _pick(m, (256, 128, 64, 32, 16, 8))

    def body(z_ref, mods_ref, g_ref, h_ref):
        is_ctx = pl.program_id(0) * tm >= n_x
        zz = z_ref[...]
        r = lax.rsqrt(jnp.mean(zz * zz, axis=-1, keepdims=True) + RMS_EPS)
        shift, scale = _sel_row(mods_ref, is_ctx, ks), _sel_row(mods_ref, is_ctx, kc)
        h_ref[...] = ((zz * r) * g_ref[...] * (1.0 + scale) + shift).astype(out_dtype)

    return pl.pallas_call(
        body, name=name, grid=(m // tm,),
        in_specs=[pl.BlockSpec((tm, d), lambda i: (i, 0)), _full(mods.shape), _full(g.shape)],
        out_specs=pl.BlockSpec((tm, d), lambda i: (i, 0)),
        out_shape=jax.ShapeDtypeStruct((m, d), out_dtype),
        compiler_params=_params(("parallel",)),
    )(z, mods, g)


def _gate_bwd_rows(dx, y, gate, coef):
    return (coef * gate * dx).astype(BF16), jnp.sum(coef * y * dx, axis=0, keepdims=True)


def _modulate_bwd(z, dh, dres, mods, g, kc, n_x, name, latent_only=False, then=None):
    m, d = z.shape
    tm = _pick(m, (256, 128, 64, 32, 16, 8))
    first_ctx = n_x // tm
    res_blocks = dres.shape[0] // tm
    out_blocks = (n_x if latent_only else m) // tm
    extra = [] if then is None else [then[0], then[1]]

    def body(z_ref, dh_ref, dres_ref, mods_ref, g_ref, *rest):
        i = pl.program_id(0)
        is_ctx = i * tm >= n_x
        dx_ref, acc_ref = rest[len(extra)], rest[len(extra) + 1]

        @pl.when((i == 0) | (i == first_ctx))
        def _():
            acc_ref[...] = jnp.zeros_like(acc_ref)
            if then is not None:
                rest[-1][...] = jnp.zeros_like(rest[-1])

        zz, dhh = z_ref[...], dh_ref[...]
        r = lax.rsqrt(jnp.mean(zz * zz, axis=-1, keepdims=True) + RMS_EPS)
        nz = zz * r
        gain = g_ref[...] * (1.0 + _sel_row(mods_ref, is_ctx, kc))
        dn = dhh * gain
        dz = r * (dn - nz * jnp.mean(dn * nz, axis=-1, keepdims=True))
        dx = jnp.where(i < res_blocks, dres_ref[...], 0.0) + dz

        @pl.when(i < out_blocks)
        def _():
            dx_ref[...] = dx

        acc_ref[0:1, :] += jnp.sum(dhh, axis=0, keepdims=True)
        acc_ref[1:2, :] += jnp.sum(dhh * nz, axis=0, keepdims=True)
        if then is not None:
            y_ref, tmods_ref, dy_ref, gate_acc_ref = rest[0], rest[1], rest[-2], rest[-1]
            dy, part = _gate_bwd_rows(dx, y_ref[...], _sel_row(tmods_ref, is_ctx, then[2]), then[3])
            dy_ref[...] = dy
            gate_acc_ref[0:1, :] += part

    row = pl.BlockSpec((tm, d), lambda i: (i, 0))
    acc_spec = pl.BlockSpec((None, 8, d), lambda i: ((i * tm >= n_x).astype(jnp.int32), 0, 0))
    out_specs = [pl.BlockSpec((tm, d), lambda i: (jnp.minimum(i, out_blocks - 1), 0)), acc_spec]
    out_shape = [jax.ShapeDtypeStruct((out_blocks * tm, d), F32), jax.ShapeDtypeStruct((2, 8, d), F32)]
    in_specs = [row, row, pl.BlockSpec((tm, d), lambda i: (jnp.minimum(i, res_blocks - 1), 0)),
                _full(mods.shape), _full(g.shape)]
    if then is not None:
        in_specs += [row, _full(then[1].shape)]
        out_specs += [row, acc_spec]
        out_shape += [jax.ShapeDtypeStruct((m, d), BF16), jax.ShapeDtypeStruct((2, 8, d), F32)]
    return pl.pallas_call(
        body, name=name, grid=(m // tm,), in_specs=in_specs, out_specs=out_specs, out_shape=out_shape,
        compiler_params=_params(("arbitrary",)),
    )(z, dh, dres, mods, g, *extra)


def _ffn_up(h, wi_t, layer, name):
    m, d = h.shape
    f = wi_t.shape[1] // 2
    tm = _pick(m, ROW_TILES)

    def body(h_ref, w_ref, jac_ref, act_ref):
        hh = h_ref[...]
        a = _dot(hh, w_ref[0:f, :], NT)
        u = _dot(hh, w_ref[f:2 * f, :], NT)
        sg = jax.nn.sigmoid(a)
        s = a * sg
        jac_ref[:, 0:f] = (u * (sg * (1.0 + a * (1.0 - sg)))).astype(BF16)
        jac_ref[:, f:2 * f] = s.astype(BF16)
        act_ref[...] = (s * u).astype(BF16)

    return pl.pallas_call(
        body, name=name, grid=(m // tm,),
        in_specs=[pl.BlockSpec((tm, d), lambda i: (i, 0)),
                  pl.BlockSpec((None, 2 * f, d), lambda i: (layer, 0, 0))],
        out_specs=[pl.BlockSpec((tm, 2 * f), lambda i: (i, 0)), pl.BlockSpec((tm, f), lambda i: (i, 0))],
        out_shape=[jax.ShapeDtypeStruct((m, 2 * f), BF16), jax.ShapeDtypeStruct((m, f), BF16)],
        compiler_params=_params(("parallel",)),
    )(h, wi_t)


def _mm_resid(a, b, layer, res, mods, km, coef, n_x, name, nxt=None):
    m, k = a.shape
    n = b.shape[2]
    tm = _pick(m, (512, 256, 128, 64, 32, 16, 8))
    tn = n if nxt is not None else _pick(n, (1024, 512, 256, 128))
    extra = [] if nxt is None else [nxt[0], nxt[1]]

    def body(a_ref, b_ref, res_ref, mods_ref, *rest):
        is_ctx = pl.program_id(1) * tm >= n_x
        y = _dot(a_ref[...], b_ref[...])
        new = res_ref[...] + coef * _sel_row(mods_ref, is_ctx, km) * y
        if nxt is None:
            out_ref, y_ref = rest
        else:
            nmods_ref, g_ref, out_ref, y_ref, h_ref = rest
            r = lax.rsqrt(jnp.mean(new * new, axis=-1, keepdims=True) + RMS_EPS)
            shift, scale = _sel_row(nmods_ref, is_ctx, nxt[2]), _sel_row(nmods_ref, is_ctx, nxt[3])
            h_ref[...] = ((new * r) * g_ref[...] * (1.0 + scale) + shift).astype(nxt[4])
        y_ref[...] = y.astype(BF16)
        out_ref[...] = new

    tile = pl.BlockSpec((tm, tn), lambda j, i: (i, j))
    outs = [jax.ShapeDtypeStruct((m, n), F32), jax.ShapeDtypeStruct((m, n), BF16)]
    if nxt is not None:
        outs.append(jax.ShapeDtypeStruct((m, n), nxt[4]))
    return pl.pallas_call(
        body, name=name, grid=(n // tn, m // tm),
        in_specs=[pl.BlockSpec((tm, k), lambda j, i: (i, 0)),
                  pl.BlockSpec((None, k, tn), lambda j, i: (layer, 0, j)),
                  tile, pl.BlockSpec((2, 16, tn), lambda j, i: (0, 0, j))] + [_full(e.shape) for e in extra],
        out_specs=[tile] * len(outs), out_shape=outs,
        compiler_params=_params(("parallel", "parallel")),
    )(a, b, res, mods, *extra)


def _ffn_down_bwd(dy, wo, layer, au, name):
    m, d = dy.shape
    f = wo.shape[1]
    tm = _pick(m, ROW_TILES)

    def body(dy_ref, wo_ref, au_ref, dau_ref):
        dact = _dot(dy_ref[...], wo_ref[...], NT)
        dau_ref[:, 0:f] = (dact * au_ref[:, 0:f].astype(F32)).astype(BF16)
        dau_ref[:, f:2 * f] = (dact * au_ref[:, f:2 * f].astype(F32)).astype(BF16)

    wide = pl.BlockSpec((tm, 2 * f), lambda i: (i, 0))
    return pl.pallas_call(
        body, name=name, grid=(m // tm,),
        in_specs=[pl.BlockSpec((tm, d), lambda i: (i, 0)), pl.BlockSpec((None, f, d), lambda i: (layer, 0, 0)), wide],
        out_specs=wide, out_shape=jax.ShapeDtypeStruct((m, 2 * f), BF16),
        compiler_params=_params(("parallel",)),
    )(dy, wo, au)


def _mm(terms, dims, n, out_dtype, name, tm_pref=(512, 256, 128, 64, 32, 16, 8), tn_pref=(512, 256, 128), dep=None):
    m = terms[0][0].shape[0]
    tm = _pick(m, tm_pref)
    tn = _pick(n, tn_pref)
    nt = len(terms)
    deps = [] if dep is None else [dep]

    def body(*refs):
        out_ref = refs[-1]
        acc = None
        for t in range(nt):
            part = _dot(refs[2 * t][...].astype(BF16), refs[2 * t + 1][...].astype(BF16), dims)
            acc = part if acc is None else acc + part
        out_ref[...] = acc.astype(out_dtype)

    in_specs, args = [], []
    for a, b, layer, rb in terms:
        k = a.shape[1]
        in_specs.append(pl.BlockSpec((tm, k), lambda j, i: (i, 0)))
        if dims == NN:
            in_specs.append(pl.BlockSpec((None, k, tn), lambda j, i, layer=layer, rb=rb: (layer, rb, j)))
        else:
            nb = n // tn
            in_specs.append(pl.BlockSpec((None, tn, k), lambda j, i, layer=layer, rb=rb, nb=nb: (layer, rb * nb + j, 0)))
        args += [a, b]
    return pl.pallas_call(
        body, name=name, grid=(n // tn, m // tm), in_specs=in_specs + [pl.BlockSpec(memory_space=pl.ANY)] * len(deps),
        out_specs=pl.BlockSpec((tm, tn), lambda j, i: (i, j)),
        out_shape=jax.ShapeDtypeStruct((m, n), out_dtype),
        compiler_params=_params(("parallel", "parallel")),
    )(*args, *deps)


def _mm_tn(a, b, out_dtype, name, dep=None):
    t = a.shape[0]
    m, n = a.shape[1], b.shape[1]
    tm = _pick(m, (1408, 2432, 1024, 512, 256, 128))
    tn = _pick(n, (1024, 512, 256, 128))
    tk = _pick(t, TALL_TILES)
    deps = [] if dep is None else [dep]

    def body(a_ref, b_ref, *rest):
        out_ref, acc_ref = rest[-2:]
        kk = pl.program_id(2)

        @pl.when(kk == 0)
        def _():
            acc_ref[...] = jnp.zeros_like(acc_ref)

        acc_ref[...] += _dot(a_ref[...].astype(BF16), b_ref[...].astype(BF16), TN)

        @pl.when(kk == pl.num_programs(2) - 1)
        def _():
            out_ref[...] = acc_ref[...].astype(out_dtype)

    return pl.pallas_call(
        body, name=name, grid=(m // tm, n // tn, t // tk),
        in_specs=[pl.BlockSpec((tk, tm), lambda i, j, k: (k, i)), pl.BlockSpec((tk, tn), lambda i, j, k: (k, j))]
        + [pl.BlockSpec(memory_space=pl.ANY)] * len(deps),
        out_specs=pl.BlockSpec((tm, tn), lambda i, j, k: (i, j)),
        out_shape=jax.ShapeDtypeStruct((m, n), out_dtype),
        scratch_shapes=[pltpu.VMEM((tm, tn), F32)],
        compiler_params=_params(("parallel", "parallel", "arbitrary")),
    )(a, b, *deps)


def _stack_rows(a, b, name):
    ta, d = a.shape
    tm = _pick(int(np.gcd(ta, b.shape[0])), (256, 128, 64, 32, 16, 8))
    na, nb = ta // tm, b.shape[0] // tm

    def body(a_ref, b_ref, o_ref):
        o_ref[...] = jnp.where(pl.program_id(0) < na, a_ref[...], b_ref[...])

    return pl.pallas_call(
        body, name=name, grid=(na + nb,),
        in_specs=[pl.BlockSpec((tm, d), lambda i: (jnp.minimum(i, na - 1), 0)),
                  pl.BlockSpec((tm, d), lambda i: (jnp.maximum(i - na, 0), 0))],
        out_specs=pl.BlockSpec((tm, d), lambda i: (i, 0)),
        out_shape=jax.ShapeDtypeStruct((ta + b.shape[0], d), a.dtype),
        compiler_params=_params(("parallel",)),
    )(a, b)


def _assemble_dz(lat_parts, ctx_parts, both_parts, width, name):
    t = next(p.shape[0] for p in lat_parts if p is not None)
    l_ctx = next(p.shape[0] for p in ctx_parts if p is not None)
    tm = _pick(int(np.gcd(t, l_ctx)), (256, 128, 64, 32, 16, 8))
    nt, nl = t // tm, l_ctx // tm
    plan, args, in_specs, off = [], [], [], 0
    lat_spec = lambda w: pl.BlockSpec((tm, w), lambda i: (jnp.minimum(i, nt - 1), 0))
    ctx_spec = lambda w: pl.BlockSpec((tm, w), lambda i: (jnp.maximum(i - nt, 0), 0))
    all_spec = lambda w: pl.BlockSpec((tm, w), lambda i: (i, 0))
    for lat, ctx, both in zip(lat_parts, ctx_parts, both_parts):
        if both:
            w = both[0].shape[1]
            plan.append(("both", off, w, len(args), len(both)))
            args += both
            in_specs += [all_spec(w)] * len(both)
        else:
            w = (lat if lat is not None else ctx).shape[1]
            plan.append(("split", off, w, len(args), (lat is not None, ctx is not None)))
            for part, spec in ((lat, lat_spec), (ctx, ctx_spec)):
                if part is not None:
                    args.append(part)
                    in_specs.append(spec(w))
        off += w
    n_in = len(args)

    def body(*refs):
        out_ref = refs[n_in]
        is_ctx = pl.program_id(0) >= nt
        for kind, o, w, first, info in plan:
            if kind == "both":
                val = refs[first][...]
                for k in range(1, info):
                    val = val + refs[first + k][...]
            else:
                has_lat, has_ctx = info
                zero = jnp.zeros((tm, w), F32)
                lat = refs[first][...] if has_lat else zero
                ctx = refs[first + int(has_lat)][...] if has_ctx else zero
                val = jnp.where(is_ctx, ctx, lat)
            out_ref[:, o:o + w] = val.astype(BF16)
        if off < width:
            out_ref[:, off:width] = jnp.zeros((tm, width - off), BF16)

    return pl.pallas_call(
        body, name=name, grid=(nt + nl,), in_specs=in_specs,
        out_specs=pl.BlockSpec((tm, width), lambda i: (i, 0)),
        out_shape=jax.ShapeDtypeStruct((t + l_ctx, width), BF16),
        compiler_params=_params(("parallel",)),
    )(*args)


def _final_loss(x, g, target, then, name):
    t, d = x.shape
    tm = _pick(t, (256, 128, 64, 32, 16, 8))
    y, tmods, km, coef = then

    def body(x_ref, g_ref, t_ref, y_ref, tmods_ref, dx_ref, loss_ref, dg_ref, dy_ref, gate_acc_ref):
        @pl.when(pl.program_id(0) == 0)
        def _():
            loss_ref[...] = jnp.zeros_like(loss_ref)
            dg_ref[...] = jnp.zeros_like(dg_ref)
            gate_acc_ref[...] = jnp.zeros_like(gate_acc_ref)

        xx, gg = x_ref[...], g_ref[...]
        r = lax.rsqrt(jnp.mean(xx * xx, axis=-1, keepdims=True) + RMS_EPS)
        nz = xx * r
        err = nz * gg - t_ref[...]
        loss_ref[...] += jnp.sum(err * err, axis=0, keepdims=True) * (0.5 / d)
        dout = err * (1.0 / d)
        dg_ref[...] += jnp.sum(dout * nz, axis=0, keepdims=True)
        dn = dout * gg
        dx = r * (dn - nz * jnp.mean(dn * nz, axis=-1, keepdims=True))
        dx_ref[...] = dx
        dy, part = _gate_bwd_rows(dx, y_ref[...], tmods_ref[0, km:km + 1, :], coef)
        dy_ref[...] = dy
        gate_acc_ref[0:1, :] += part

    row = pl.BlockSpec((tm, d), lambda i: (i, 0))
    vec = pl.BlockSpec((1, d), lambda i: (0, 0))
    acc = pl.BlockSpec((None, 8, d), lambda i: (0, 0, 0))
    return pl.pallas_call(
        body, name=name, grid=(t // tm,), in_specs=[row, vec, row, row, _full(tmods.shape)],
        out_specs=[row, vec, vec, row, acc],
        out_shape=[jax.ShapeDtypeStruct((t, d), F32), jax.ShapeDtypeStruct((1, d), F32),
                   jax.ShapeDtypeStruct((1, d), F32), jax.ShapeDtypeStruct((t, d), BF16),
                   jax.ShapeDtypeStruct((2, 8, d), F32)],
        compiler_params=_params(("arbitrary",)),
    )(x, g, target, y, tmods)


def _adaln_fwd(craw, w_mod, b_cols, name):
    lyr, d, nc = w_mod.shape

    def body(c_ref, w_ref, b_ref, out_ref):
        out_ref[...] = _bdot(_silu(c_ref[...]), w_ref[...]) + b_ref[...]

    return pl.pallas_call(
        body, name=name, grid=(lyr,),
        in_specs=[_full(craw.shape), pl.BlockSpec((None, d, nc), lambda l: (l, 0, 0)),
                  pl.BlockSpec((None, 1, nc), lambda l: (l, 0, 0))],
        out_specs=pl.BlockSpec((None, 16, nc), lambda l: (l, 0, 0)),
        out_shape=jax.ShapeDtypeStruct((lyr, 16, nc), F32),
        compiler_params=_params(("parallel",)),
    )(craw, w_mod, b_cols)


def _adaln_bwd(craw, cs_t, dmm_cols, w_mod, name):
    lyr, d, nc = w_mod.shape

    def body(c_ref, cst_ref, dmm_ref, w_ref, gw_ref, dc_ref):
        dmm = dmm_ref[...]
        gw_ref[...] = _bdot(cst_ref[...], dmm)
        cc = c_ref[...]
        sg = jax.nn.sigmoid(cc)
        dc_ref[...] = _bdot(dmm, w_ref[...], NT) * (sg * (1.0 + cc * (1.0 - sg)))

    wspec = pl.BlockSpec((None, d, nc), lambda l: (l, 0, 0))
    return pl.pallas_call(
        body, name=name, grid=(lyr,),
        in_specs=[_full(craw.shape), _full(cs_t.shape), pl.BlockSpec((None, 16, nc), lambda l: (l, 0, 0)), wspec],
        out_specs=[wspec, pl.BlockSpec((None, 16, d), lambda l: (l, 0, 0))],
        out_shape=[jax.ShapeDtypeStruct((lyr, d, nc), F32), jax.ShapeDtypeStruct((lyr, 16, d), F32)],
        compiler_params=_params(("parallel",)),
    )(craw, cs_t, dmm_cols, w_mod)


def _rope_tables(t):
    rows = np.repeat(np.arange(t // GRID_W, dtype=np.float32), GRID_W)
    cols = np.tile(np.arange(GRID_W, dtype=np.float32), t // GRID_W)
    n = A_HEAD_DIM // 4
    freqs = (ROPE_BASE ** (-np.arange(n, dtype=np.float32) / n)).astype(np.float32)
    ang_r, ang_c = (rows[:, None] * freqs).astype(np.float32), (cols[:, None] * freqs).astype(np.float32)
    cr, sr, cc, sc = np.cos(ang_r), np.sin(ang_r), np.cos(ang_c), np.sin(ang_c)
    cos = np.concatenate([cr, cr, cc, cc] * 2, axis=-1).astype(np.float32)
    sin = np.concatenate([-sr, sr, -sc, sc] * 2, axis=-1).astype(np.float32)
    return jnp.asarray(cos), jnp.asarray(sin)


def _rope(xt, cos, sin, adjoint, name, view=None):
    t = cos.shape[0]
    w, col = (xt.shape[1], 0) if view is None else view
    tb = _pick(t, (512, 256, 128))
    rep = w // cos.shape[1]

    def body(x_ref, c_ref, s_ref, o_ref):
        xx = x_ref[...]
        cc = jnp.concatenate([c_ref[...]] * rep, axis=1) if rep > 1 else c_ref[...]
        ss = jnp.concatenate([s_ref[...]] * rep, axis=1) if rep > 1 else s_ref[...]
        low = (lax.broadcasted_iota(jnp.int32, xx.shape, 1) % 32) < 16

        def partner(v):
            return jnp.where(low, pltpu.roll(v, w - 16, 1), pltpu.roll(v, 16, 1))

        if adjoint:
            o_ref[...] = xx * cc + partner(xx * ss)
        else:
            o_ref[...] = xx * cc + partner(xx) * ss

    blk = pl.BlockSpec((tb, w), lambda i: (i, 0))
    tab = pl.BlockSpec((tb, cos.shape[1]), lambda i: (i, 0))
    return pl.pallas_call(
        body, name=name, grid=(t // tb,), in_specs=[pl.BlockSpec((tb, w), lambda i: (i, col)), tab, tab],
        out_specs=blk, out_shape=jax.ShapeDtypeStruct((t, w), F32), compiler_params=_params(("parallel",)),
    )(xt, cos, sin)


def _attn_bias():
    i = (np.arange(A_REP * WINDOW) % WINDOW)[:, None]
    j = np.arange(3 * WINDOW)[None, :]
    near = np.abs(j - WINDOW - i) <= WINDOW
    variants = [near, near & (j >= WINDOW), near & (j < 2 * WINDOW), near & (j >= WINDOW) & (j < 2 * WINDOW)]
    return jnp.asarray(np.where(np.stack(variants), 0.0, -np.inf).astype(np.float32))


def _attn_bias_spec(nb):
    rows = A_REP * WINDOW
    return pl.BlockSpec((None, rows, 3 * WINDOW),
                        lambda g, n: ((n == 0).astype(jnp.int32) + 2 * (n == nb - 1).astype(jnp.int32), 0, 0))


def _attn_probs(q, kb, kc, sink, bias):
    scale = A_HEAD_DIM ** -0.5
    s1 = _bdot(q, kb, NT) * scale + bias
    s2 = _bdot(q, kc, NT) * scale
    mx = jnp.maximum(jnp.maximum(jnp.max(s1, axis=-1, keepdims=True), jnp.max(s2, axis=-1, keepdims=True)), sink)
    p1, p2, ps = jnp.exp(s1 - mx), jnp.exp(s2 - mx), jnp.exp(sink - mx)
    inv = 1.0 / (jnp.sum(p1, axis=-1, keepdims=True) + jnp.sum(p2, axis=-1, keepdims=True) + ps)
    return p1 * inv, p2 * inv, ps * inv


def _sink_rows(sink_ref):
    return jnp.concatenate([jnp.broadcast_to(sink_ref[r], (WINDOW, 1)) for r in range(A_REP)], axis=0)


def _attn_fwd(q, kp, vp, kc, vc, sink, name):
    hq, t, dh = q.shape
    nb = t // WINDOW
    lc = kc.shape[1]
    rows = A_REP * WINDOW

    def body(q_ref, k_ref, v_ref, kc_ref, vc_ref, sink_ref, bias_ref, o_ref):
        n = pl.program_id(1)
        start = pl.multiple_of(n * WINDOW, WINDOW)
        kb, vb = k_ref[pl.ds(start, 3 * WINDOW), :], v_ref[pl.ds(start, 3 * WINDOW), :]
        p1, p2, _ = _attn_probs(q_ref[...].reshape(rows, dh), kb, kc_ref[...], _sink_rows(sink_ref), bias_ref[...])
        o_ref[...] = (_bdot(p1, vb) + _bdot(p2, vc_ref[...])).reshape(A_REP, WINDOW, dh)

    qblk = pl.BlockSpec((A_REP, WINDOW, dh), lambda g, n: (g, n, 0))
    kfull = pl.BlockSpec((None, t + 2 * WINDOW, dh), lambda g, n: (g, 0, 0))
    cfull = pl.BlockSpec((None, lc, dh), lambda g, n: (g, 0, 0))
    return pl.pallas_call(
        body, name=name, grid=(hq // A_REP, nb),
        in_specs=[qblk, kfull, kfull, cfull, cfull, pl.BlockSpec((A_REP, 1, 1), lambda g, n: (g, 0, 0)),
                  _attn_bias_spec(nb)],
        out_specs=qblk, out_shape=jax.ShapeDtypeStruct((hq, t, dh), F32),
        compiler_params=_params(("parallel", "parallel")),
    )(q, kp, vp, kc, vc, sink, _attn_bias())


def _attn_bwd(q, kp, vp, kc, vc, sink, o, do, name):
    hq, t, dh = q.shape
    nb = t // WINDOW
    lc = kc.shape[1]
    scale = A_HEAD_DIM ** -0.5
    rows = A_REP * WINDOW

    def body(q_ref, k_ref, v_ref, kc_ref, vc_ref, sink_ref, o_ref, do_ref, bias_ref,
             dq_ref, dk_ref, dv_ref, dkc_ref, dvc_ref, dsink_ref):
        n = pl.program_id(1)

        @pl.when(n == 0)
        def _():
            dk_ref[...] = jnp.zeros_like(dk_ref)
            dv_ref[...] = jnp.zeros_like(dv_ref)
            dkc_ref[...] = jnp.zeros_like(dkc_ref)
            dvc_ref[...] = jnp.zeros_like(dvc_ref)
            dsink_ref[...] = jnp.zeros_like(dsink_ref)

        start = pl.multiple_of(n * WINDOW, WINDOW)
        band = pl.ds(start, 3 * WINDOW)
        qq, kb, vb, kcc, vcc = q_ref[...].reshape(rows, dh), k_ref[band, :], v_ref[band, :], kc_ref[...], vc_ref[...]
        p1, p2, ps = _attn_probs(qq, kb, kcc, _sink_rows(sink_ref), bias_ref[...])
        dout = do_ref[...].reshape(rows, dh)
        delta = jnp.sum(dout * o_ref[...].reshape(rows, dh), axis=-1, keepdims=True)
        ds1 = p1 * (_bdot(dout, vb, NT) - delta)
        ds2 = p2 * (_bdot(dout, vcc, NT) - delta)
        dq_ref[...] = ((_bdot(ds1, kb) + _bdot(ds2, kcc)) * scale).reshape(A_REP, WINDOW, dh)
        dk_ref[band, :] += _bdot(ds1.T, qq) * scale
        dv_ref[band, :] += _bdot(p1.T, dout)
        dkc_ref[...] += _bdot(ds2.T, qq) * scale
        dvc_ref[...] += _bdot(p2.T, dout)
        dsink_ref[...] += jnp.sum((-ps * delta).reshape(A_REP, WINDOW, 1), axis=1, keepdims=True)

    qblk = pl.BlockSpec((A_REP, WINDOW, dh), lambda g, n: (g, n, 0))
    kfull = pl.BlockSpec((None, t + 2 * WINDOW, dh), lambda g, n: (g, 0, 0))
    cfull = pl.BlockSpec((None, lc, dh), lambda g, n: (g, 0, 0))
    return pl.pallas_call(
        body, name=name, grid=(hq // A_REP, nb),
        in_specs=[qblk, kfull, kfull, cfull, cfull, pl.BlockSpec((A_REP, 1, 1), lambda g, n: (g, 0, 0)), qblk, qblk,
                  _attn_bias_spec(nb)],
        out_specs=[qblk, kfull, kfull, cfull, cfull, pl.BlockSpec((A_REP, 8, 128), lambda g, n: (g, 0, 0))],
        out_shape=[jax.ShapeDtypeStruct(q.shape, F32), jax.ShapeDtypeStruct(kp.shape, F32),
                   jax.ShapeDtypeStruct(kp.shape, F32), jax.ShapeDtypeStruct(kc.shape, F32),
                   jax.ShapeDtypeStruct(kc.shape, F32), jax.ShapeDtypeStruct((hq, 8, 128), F32)],
        compiler_params=_params(("parallel", "arbitrary")),
    )(q, kp, vp, kc, vc, sink, o, do, _attn_bias())


def _gate_fwd(zg, w2, b2, name):
    m = zg.shape[0]
    n = w2.shape[1]
    tm = _pick(m, TALL_TILES)

    def body(z_ref, w_ref, b_ref, o_ref):
        o_ref[...] = jax.nn.log_sigmoid(_bdot(z_ref[...], w_ref[...]) + b_ref[...]) / B_GATE_NORM

    return pl.pallas_call(
        body, name=name, grid=(m // tm,),
        in_specs=[pl.BlockSpec((tm, zg.shape[1]), lambda i: (i, 0)), _full(w2.shape), _full(b2.shape)],
        out_specs=pl.BlockSpec((tm, n), lambda i: (i, 0)), out_shape=jax.ShapeDtypeStruct((m, n), F32),
        compiler_params=_params(("parallel",)),
    )(zg, w2, b2)


def _gate_bwd(zg, w2, b2, dla, name):
    m, rk = zg.shape
    n = w2.shape[1]
    tm = _pick(m, TALL_TILES)

    def body(z_ref, w_ref, b_ref, d_ref, dz_ref, dw_ref, db_ref):
        @pl.when(pl.program_id(0) == 0)
        def _():
            dw_ref[...] = jnp.zeros_like(dw_ref)
            db_ref[...] = jnp.zeros_like(db_ref)

        zz, ww = z_ref[...], w_ref[...]
        pre = _bdot(zz, ww) + b_ref[...]
        dpre = d_ref[...] * (1.0 / B_GATE_NORM) * jax.nn.sigmoid(-pre)
        dz_ref[...] = _bdot(dpre, ww, NT)
        dw_ref[...] += _bdot(zz.T, dpre)
        db_ref[...] += jnp.sum(dpre, axis=0, keepdims=True)

    return pl.pallas_call(
        body, name=name, grid=(m // tm,),
        in_specs=[pl.BlockSpec((tm, rk), lambda i: (i, 0)), _full(w2.shape), _full(b2.shape),
                  pl.BlockSpec((tm, n), lambda i: (i, 0))],
        out_specs=[pl.BlockSpec((tm, rk), lambda i: (i, 0)), _full(w2.shape), _full(b2.shape)],
        out_shape=[jax.ShapeDtypeStruct((m, rk), F32), jax.ShapeDtypeStruct(w2.shape, F32),
                   jax.ShapeDtypeStruct(b2.shape, F32)],
        compiler_params=_params(("arbitrary",)),
    )(zg, w2, b2, dla)


def _chunk_order(step, n_x_chunks, n_chunks, reverse):
    n_c = n_chunks - n_x_chunks
    if reverse:
        return jnp.where(step < n_c, n_chunks - 1 - step, n_chunks - 1 - step)
    return jnp.where(step < n_c, n_x_chunks + step, step - n_c)


def _tri(reverse, transpose=False):
    i = lax.broadcasted_iota(jnp.int32, (B_CHUNK, B_CHUNK), 0)
    j = lax.broadcasted_iota(jnp.int32, (B_CHUNK, B_CHUNK), 1)
    if transpose:
        i, j = j, i
    return (j >= i) if reverse else (j <= i)


def _gla_chunk(q, k, la, reverse):
    g = _dot_01(_tri(reverse), la)
    last = 0 if reverse else B_CHUNK - 1
    gl = g[last:last + 1, :]
    eg, eng, egl = jnp.exp(g), jnp.exp(-g), jnp.exp(gl - g)
    decay_col = jnp.exp(jnp.sum(la.T, axis=1, keepdims=True))
    return q * (B_DK ** -0.5) * eg, k * eng, k * egl, eg, eng, egl, decay_col


def _head_of(shape, axis, width):
    return lax.broadcasted_iota(jnp.int32, shape, axis) // width


def _gla_chunks_per_step(n_chunks, n_x_chunks):
    return _pick(int(np.gcd(n_chunks - n_x_chunks, n_x_chunks)), (4, 2, 1))


def _gla_fwd(q, k, v, la_f, la_b, n_x, name, qk_cols=(0, 0)):
    tc, wk = la_f.shape
    wv = v.shape[1]
    hh = B_HEADS
    dk, dv = wk // hh, wv // hh
    nc, nxc = tc // B_CHUNK, n_x // B_CHUNK
    sub = _gla_chunks_per_step(nc, nxc)
    rows_per_step = sub * B_CHUNK
    orders = [functools.partial(_chunk_order, n_x_chunks=nxc // sub, n_chunks=nc // sub, reverse=rev)
              for rev in (False, True)]

    def body(*refs):
        ins, outs, s_refs = refs[:8], refs[8:12], refs[12:]

        @pl.when(pl.program_id(0) == 0)
        def _():
            for s_ref in s_refs:
                s_ref[...] = jnp.zeros_like(s_ref)

        lane_head = _head_of((B_CHUNK, wk), 1, dk)
        row_head = _head_of((wk, dv), 0, dk)
        for di, reverse in enumerate((False, True)):
            q_ref, k_ref, v_ref, la_ref = ins[4 * di:4 * di + 4]
            o_ref, s_save_ref = outs[2 * di:2 * di + 2]
            s_prev = s_refs[di][...]
            for c in (reversed(range(sub)) if reverse else range(sub)):
                rows = slice(c * B_CHUNK, (c + 1) * B_CHUNK)
                qt, kt, ke, _, _, _, decay_col = _gla_chunk(q_ref[rows, :], k_ref[rows, :], la_ref[rows, :], reverse)
                ke_t = ke.T
                update = jnp.zeros_like(s_prev)
                for h in range(hh):
                    vv = v_ref[rows, h * dv:(h + 1) * dv]
                    qm = jnp.where(lane_head == h, qt, 0.0)
                    att = jnp.where(_tri(reverse), _bdot(qm, kt, NT), 0.0)
                    o_ref[rows, h * dv:(h + 1) * dv] = _bdot(att, vv) + _bdot(qm, s_prev)
                    update = jnp.where(row_head == h, _bdot(ke_t, vv), update)
                s_save_ref[c] = s_prev
                s_prev = decay_col * s_prev + update
            s_refs[di][...] = s_prev

    def blk(w, order, col=0):
        return pl.BlockSpec((rows_per_step, w), lambda s: (order(s), col))

    def sblk(order):
        return pl.BlockSpec((sub, wk, dv), lambda s: (order(s), 0, 0))

    in_specs, out_specs = [], []
    for order in orders:
        in_specs += [blk(wk, order, qk_cols[0]), blk(wk, order, qk_cols[1]), blk(wv, order), blk(wk, order)]
        out_specs += [blk(wv, order), sblk(order)]
    o_shape, s_shape = jax.ShapeDtypeStruct((tc, wv), F32), jax.ShapeDtypeStruct((nc, wk, dv), F32)
    return pl.pallas_call(
        body, name=name, grid=(nc // sub,), in_specs=in_specs, out_specs=out_specs,
        out_shape=[o_shape, s_shape, o_shape, s_shape],
        scratch_shapes=[pltpu.VMEM((wk, dv), F32)] * 2,
        compiler_params=_params(("arbitrary",)),
    )(q, k, v, la_f, q, k, v, la_b)


def _gla_bwd(q, k, v, la_f, la_b, s_f, s_b, do, n_x, name, qk_cols=(0, 0)):
    tc, wk = la_f.shape
    wv = v.shape[1]
    hh = B_HEADS
    dk, dv = wk // hh, wv // hh
    nc, nxc = tc // B_CHUNK, n_x // B_CHUNK
    sub = _gla_chunks_per_step(nc, nxc)
    rows_per_step = sub * B_CHUNK
    nb, nxb = nc // sub, nxc // sub
    orders = [functools.partial(lambda s, rev: _chunk_order(nb - 1 - s, nxb, nb, rev), rev=rev) for rev in (False, True)]

    def body(*refs):
        ins, outs, ds_refs = refs[:12], refs[12:20], refs[20:]

        @pl.when(pl.program_id(0) == 0)
        def _():
            for ds_ref in ds_refs:
                ds_ref[...] = jnp.zeros_like(ds_ref)

        lane_head = _head_of((B_CHUNK, wk), 1, dk)
        row_head = _head_of((wk, dv), 0, dk)
        for di, reverse in enumerate((False, True)):
            q_ref, k_ref, v_ref, la_ref, s_save_ref, do_ref = ins[6 * di:6 * di + 6]
            dq_ref, dk_ref, dv_ref, dla_ref = outs[4 * di:4 * di + 4]
            mask = _tri(reverse)
            last = 0 if reverse else B_CHUNK - 1
            is_last = lax.broadcasted_iota(jnp.int32, (B_CHUNK, wk), 0) == last
            ds_new = ds_refs[di][...]
            for c in (range(sub) if reverse else reversed(range(sub))):
                rows = slice(c * B_CHUNK, (c + 1) * B_CHUNK)
                la = la_ref[rows, :]
                qt, kt, ke, eg, eng, egl, decay_col = _gla_chunk(q_ref[rows, :], k_ref[rows, :], la, reverse)
                qt_t = qt.T
                s_prev = s_save_ref[c]
                dqt, dkt, dke = jnp.zeros_like(qt), jnp.zeros_like(qt), jnp.zeros_like(qt)
                ds_add = jnp.zeros_like(ds_new)
                for h in range(hh):
                    cols = slice(h * dv, (h + 1) * dv)
                    vv, dout = v_ref[rows, cols], do_ref[rows, cols]
                    mine = lane_head == h
                    qm, km = jnp.where(mine, qt, 0.0), jnp.where(mine, ke, 0.0)
                    att = jnp.where(mask, _bdot(qm, kt, NT), 0.0)
                    datt = jnp.where(mask, _bdot(dout, vv, NT), 0.0)
                    dv_ref[rows, cols] = _bdot(att.T, dout) + _bdot(km, ds_new)
                    dqt = jnp.where(mine, _bdot(datt, kt) + _bdot(dout, s_prev, NT), dqt)
                    dkt = jnp.where(mine, _bdot(datt.T, qt), dkt)
                    dke = jnp.where(mine, _bdot(vv, ds_new, NT), dke)
                    ds_add = jnp.where(row_head == h, _bdot(qt_t, dout), ds_add)
                ddecay_row = jnp.sum((ds_new * s_prev).T, axis=0, keepdims=True)
                decay_row = jnp.exp(jnp.sum(la, axis=0, keepdims=True))
                dq_ref[rows, :] = dqt * (B_DK ** -0.5) * eg
                dk_ref[rows, :] = dkt * eng + dke * egl
                dgl = jnp.sum(dke * ke, axis=0, keepdims=True) + ddecay_row * decay_row
                dg = dqt * qt - dkt * kt - dke * ke + jnp.where(is_last, dgl, 0.0)
                dla_ref[rows, :] = _dot_01(_tri(reverse, transpose=True), dg)
                ds_new = decay_col * ds_new + ds_add
            ds_refs[di][...] = ds_new

    def blk(w, order, col=0):
        return pl.BlockSpec((rows_per_step, w), lambda s: (order(s), col))

    in_specs, out_specs = [], []
    for order in orders:
        in_specs += [blk(wk, order, qk_cols[0]), blk(wk, order, qk_cols[1]), blk(wv, order), blk(wk, order),
                     pl.BlockSpec((sub, wk, dv), lambda s, order=order: (order(s), 0, 0)), blk(wv, order)]
        out_specs += [blk(wk, order), blk(wk, order), blk(wv, order), blk(wk, order)]
    k_shape, v_shape = jax.ShapeDtypeStruct((tc, wk), F32), jax.ShapeDtypeStruct((tc, wv), F32)
    return pl.pallas_call(
        body, name=name, grid=(nb,), in_specs=in_specs, out_specs=out_specs,
        out_shape=[k_shape, k_shape, v_shape, k_shape] * 2,
        scratch_shapes=[pltpu.VMEM((wk, dv), F32)] * 2,
        compiler_params=_params(("arbitrary",)),
    )(q, k, v, la_f, s_f, do, q, k, v, la_b, s_b, do)


def _gla_out_fwd(o_f, o_b, r, g, name):
    t = r.shape[0]
    dv = g.shape[1]
    hh = r.shape[1] // dv
    tb = _pick(t, (256, 128, 64))

    def body(of_ref, ob_ref, r_ref, g_ref, out_ref):
        for h in range(hh):
            cols = slice(h * dv, (h + 1) * dv)
            o = of_ref[:, cols] + ob_ref[:, cols]
            rs = lax.rsqrt(jnp.mean(o * o, axis=-1, keepdims=True) + RMS_EPS)
            out_ref[:, cols] = (o * rs) * g_ref[...] * _silu(r_ref[:, cols])

    rblk = pl.BlockSpec((tb, hh * dv), lambda i: (i, 0))
    return pl.pallas_call(
        body, name=name, grid=(t // tb,), in_specs=[rblk, rblk, rblk, _full(g.shape)], out_specs=rblk,
        out_shape=jax.ShapeDtypeStruct((t, hh * dv), F32), compiler_params=_params(("parallel",)),
    )(o_f, o_b, r, g)


def _gla_out_bwd(o_f, o_b, r, g, dout, name):
    tc = o_f.shape[0]
    t = r.shape[0]
    dv = g.shape[1]
    hh = r.shape[1] // dv
    tb = _pick(int(np.gcd(t, tc)), (256, 128, 64))
    nt = t // tb

    def body(of_ref, ob_ref, r_ref, g_ref, d_ref, do_ref, dr_ref, dg_ref):
        i = pl.program_id(0)

        @pl.when(i == 0)
        def _():
            dg_ref[...] = jnp.zeros_like(dg_ref)

        @pl.when(i >= nt)
        def _():
            do_ref[...] = jnp.zeros_like(do_ref)

        @pl.when(i < nt)
        def _():
            gg = g_ref[...]
            for h in range(hh):
                cols = slice(h * dv, (h + 1) * dv)
                o = of_ref[:, cols] + ob_ref[:, cols]
                rs = lax.rsqrt(jnp.mean(o * o, axis=-1, keepdims=True) + RMS_EPS)
                nz = o * rs
                rr, dd = r_ref[:, cols], d_ref[:, cols]
                sg = jax.nn.sigmoid(rr)
                dr_ref[:, cols] = dd * nz * gg * (sg * (1.0 + rr * (1.0 - sg)))
                dy = dd * (rr * sg)
                dg_ref[...] += jnp.sum(dy * nz, axis=0, keepdims=True)
                dn = dy * gg
                do_ref[:, cols] = rs * (dn - nz * jnp.mean(dn * nz, axis=-1, keepdims=True))

    oblk = pl.BlockSpec((tb, hh * dv), lambda i: (i, 0))
    rblk = pl.BlockSpec((tb, hh * dv), lambda i: (jnp.minimum(i, nt - 1), 0))
    return pl.pallas_call(
        body, name=name, grid=(tc // tb,), in_specs=[oblk, oblk, rblk, _full(g.shape), rblk],
        out_specs=[oblk, rblk, _full(g.shape)],
        out_shape=[jax.ShapeDtypeStruct(o_f.shape, F32), jax.ShapeDtypeStruct(r.shape, F32),
                   jax.ShapeDtypeStruct(g.shape, F32)],
        compiler_params=_params(("arbitrary",)),
    )(o_f, o_b, r, g, dout)


def _pool_window(i, tb, t):
    return pl.multiple_of(jnp.clip(i * tb - POOL_PAD, 0, t - (tb + 2 * POOL_PAD)), 8)


def _pool_band(half, i, tb, start, adjoint):
    pos = i * tb + lax.broadcasted_iota(jnp.int32, (tb, tb + 2 * POOL_PAD), 0)
    tok = start + lax.broadcasted_iota(jnp.int32, (tb, tb + 2 * POOL_PAD), 1)
    if adjoint:
        return (tok > pos - half) & (tok <= pos + half)
    return (tok >= pos - half) & (tok < pos + half)


def _pool_count(pos, half, t):
    return (jnp.minimum(pos + half, t) - jnp.maximum(pos - half, 0)).astype(F32)


def _pool_fwd(h, w_pool, pool_scale, res, mods, km, name):
    t, d = res.shape
    ng, gw = w_pool.shape[0], w_pool.shape[1]
    tb = _pick(t, (256, 128, 64))

    def body(h_ref, w_ref, ps_ref, res_ref, mods_ref, out_ref, pooled_ref, ypre_ref):
        gi, i = pl.program_id(0), pl.program_id(1)
        half = jnp.left_shift(1, gi)
        start = _pool_window(i, tb, t)
        win = h_ref[pl.ds(start, tb + 2 * POOL_PAD), :]
        total = _dot_01(_pool_band(half, i, tb, start, False), win)
        pos = i * tb + lax.broadcasted_iota(jnp.int32, (tb, 1), 0)
        pooled = total / _pool_count(pos, half, t) - h_ref[pl.ds(pl.multiple_of(i * tb, tb), tb), :]
        ypre = _bdot(pooled, w_ref[...])
        pooled_ref[...] = pooled.astype(BF16)
        ypre_ref[...] = ypre
        out_ref[...] = res_ref[...] + mods_ref[0, km:km + 1, :] * (ypre * ps_ref[...])

    tile = pl.BlockSpec((tb, gw), lambda gi, i: (i, gi))
    return pl.pallas_call(
        body, name=name, grid=(ng, t // tb),
        in_specs=[pl.BlockSpec((t, gw), lambda gi, i: (0, gi)),
                  pl.BlockSpec((None, gw, gw), lambda gi, i: (gi, 0, 0)),
                  pl.BlockSpec((1, gw), lambda gi, i: (0, gi)), tile,
                  pl.BlockSpec((2, 16, gw), lambda gi, i: (0, 0, gi))],
        out_specs=[tile, tile, tile],
        out_shape=[jax.ShapeDtypeStruct((t, d), F32), jax.ShapeDtypeStruct((t, d), BF16),
                   jax.ShapeDtypeStruct((t, d), F32)],
        compiler_params=_params(("parallel", "parallel")),
    )(h, w_pool, pool_scale, res, mods)


def _pool_bwd(dxp, w_pool, pool_scale, pooled, ypre, mods, km, name):
    t, d = pooled.shape
    ng, gw = w_pool.shape[0], w_pool.shape[1]
    tb = _pick(t, (256, 128, 64))

    def body(dxp_ref, w_ref, ps_ref, pooled_ref, ypre_ref, mods_ref, dh_ref, dw_ref, acc_ref):
        gi, i = pl.program_id(0), pl.program_id(1)

        @pl.when(i == 0)
        def _():
            dw_ref[...] = jnp.zeros_like(dw_ref)
            acc_ref[...] = jnp.zeros_like(acc_ref)

        half = jnp.left_shift(1, gi)
        mod, ps = mods_ref[0, km:km + 1, :], ps_ref[...]
        start = _pool_window(i, tb, t)
        dwin = dxp_ref[pl.ds(start, tb + 2 * POOL_PAD), :]
        dpooled = _bdot(dwin * (mod * ps), w_ref[...], NT)
        pos = start + lax.broadcasted_iota(jnp.int32, (tb + 2 * POOL_PAD, 1), 0)
        spread = _dot_01(_pool_band(half, i, tb, start, True), dpooled / _pool_count(pos, half, t))
        dxc, yp = dxp_ref[pl.ds(pl.multiple_of(i * tb, tb), tb), :], ypre_ref[...]
        dh_ref[...] = spread - _bdot(dxc * (mod * ps), w_ref[...], NT)
        dw_ref[...] += _bdot(pooled_ref[...].astype(F32).T, dxc * (mod * ps))
        acc_ref[0:1, :] += jnp.sum(dxc * yp * mod, axis=0, keepdims=True)
        acc_ref[1:2, :] += jnp.sum(dxc * yp * ps, axis=0, keepdims=True)

    tile = pl.BlockSpec((tb, gw), lambda gi, i: (i, gi))
    wblk = pl.BlockSpec((None, gw, gw), lambda gi, i: (gi, 0, 0))
    return pl.pallas_call(
        body, name=name, grid=(ng, t // tb),
        in_specs=[pl.BlockSpec((t, gw), lambda gi, i: (0, gi)), wblk,
                  pl.BlockSpec((1, gw), lambda gi, i: (0, gi)), tile, tile,
                  pl.BlockSpec((2, 16, gw), lambda gi, i: (0, 0, gi))],
        out_specs=[tile, wblk, pl.BlockSpec((8, gw), lambda gi, i: (0, gi))],
        out_shape=[jax.ShapeDtypeStruct((t, d), F32), jax.ShapeDtypeStruct(w_pool.shape, F32),
                   jax.ShapeDtypeStruct((8, d), F32)],
        compiler_params=_params(("arbitrary", "arbitrary")),
    )(dxp, w_pool, pool_scale, pooled, ypre, mods)


def _adamw(w, g, m, v, name):
    r, c = w.shape
    tr = _pick(r, (512, 352, 256, 128, 64, 32, 16, 8))
    c1 = 1.0 / (1.0 - ADAM_B1 ** ADAM_STEP)
    c2 = 1.0 / (1.0 - ADAM_B2 ** ADAM_STEP)

    def body(w_ref, g_ref, m_ref, v_ref, d_ref, nm_ref, nv_ref):
        gg = g_ref[...]
        nm = ADAM_B1 * m_ref[...] + (1.0 - ADAM_B1) * gg
        nv = ADAM_B2 * v_ref[...] + (1.0 - ADAM_B2) * (gg * gg)
        nm_ref[...] = nm
        nv_ref[...] = nv
        d_ref[...] = -ADAM_LR * ((nm * c1) / (jnp.sqrt(nv * c2) + ADAM_EPS) + ADAM_WD * w_ref[...])

    blk = pl.BlockSpec((tr, c), lambda i: (i, 0))
    shp = jax.ShapeDtypeStruct((r, c), F32)
    return pl.pallas_call(
        body, name=name, grid=(r // tr,), in_specs=[blk] * 4, out_specs=[blk] * 3, out_shape=[shp] * 3,
        compiler_params=_params(("parallel",)),
    )(w, g, m, v)


def _heads(z, n_heads):
    m = z.shape[0]
    return z.reshape(m, n_heads, -1).transpose(1, 0, 2)


def _unheads(zh):
    return zh.transpose(1, 0, 2).reshape(zh.shape[1], -1)


def _pad_rows(a, n):
    return jnp.pad(a, ((0, 0), (n, n), (0, 0))) if a.ndim == 3 else jnp.pad(a, ((n, n), (0, 0)))


def _local_step(x, ctx, target, mods, wts, fetch, emit):
    t, d = x.shape
    l_ctx = ctx.shape[0]
    tc = t + l_ctx
    norm_g = wts["norm_g"]
    ng = lambda l, k: norm_g[l, k][None, :]
    grads = {}
    dmods = [[[None] * N_MOD for _ in range(2)] for _ in range(2)]
    dnorm = [[None] * 3 for _ in range(2)]

    def ffn_fwd(z, h, l, kbase, wi, wo, n_x, tag, nxt):
        au, act = _ffn_up(h, wi, 0, f"ffn_up_{tag}")
        wo = wo(act) if callable(wo) else wo
        outs = _mm_resid(act, wo, 0, z, mods[l], kbase + 2, 0.5, n_x, f"ffn_down_{tag}", nxt=nxt)
        return outs[0], (z, h, au, act, outs[1], wi, wo), (outs[2] if nxt is not None else None)

    def ffn_bwd(dz_new, dy, saved, l, kbase, g, n_x, tag, stage, split=False, then=None):
        z, h, au, act, y, wi, wo = saved
        dau = _ffn_down_bwd(dy, wo, 0, au, f"ffn_down_bwd_{tag}")
        dwo = _mm_tn(act, dy, BF16, f"dwo_{tag}")
        if split:
            token = emit(stage, [dwo])
            dwi_t = _mm_tn(dau, h, BF16, f"dwi_{tag}", dep=token)
            token = emit(stage + 1, [dwi_t])
        else:
            dwi_t = _mm_tn(dau, h, BF16, f"dwi_{tag}")
            token = emit(stage, [dwi_t, dwo])
        dh = _mm([(dau, wi, 0, 0)], NN, d, F32, f"dh_{tag}", tm_pref=TALL_TILES, dep=token)
        return _modulate_bwd(z, dh, dz_new, mods[l], g, kbase + 1, n_x, f"mod_bwd_{tag}", latent_only=split, then=then)

    def record(l, kbase, k_norm, g, acc_mod, acc_gate, streams):
        total = None
        for s in range(streams):
            dmods[l][s][kbase] = acc_mod[s, 0]
            dmods[l][s][kbase + 1] = acc_mod[s, 1] * g[0]
            if acc_gate is not None:
                dmods[l][s][kbase + 2] = acc_gate[s, 0]
            part = acc_mod[s, 1] * (1.0 + mods[l][s, kbase + 1])
            total = part if total is None else total + part
        dnorm[l][k_norm] = total

    xc0 = _stack_rows(x, ctx, "stack_tokens")
    wi1_0 = fetch(0, None)["wi1_0"]
    h0 = _modulate(xc0, mods[0], ng(0, 0), 0, 1, t, BF16, "mod_l0f1")
    xc1, sv_f1, hc = ffn_fwd(xc0, h0, 0, 0, wi1_0, lambda act: fetch(1, act)["wo1_0"], t, "l0f1",
                             (mods[0], ng(0, 1), 3, 4, BF16))
    w_in_t = fetch(2, hc)["w_in_t"]
    n_proj = w_in_t.shape[1]
    zall = _mm([(hc, w_in_t, 0, 0)], NT, n_proj, F32, "proj", tm_pref=TALL_TILES,
               tn_pref=(n_proj,))
    offs = np.cumsum((0,) + PROJ_SIZES)
    part = lambda i, rows=slice(None): zall[rows, offs[i]:offs[i + 1]]
    lat, con = slice(0, t), slice(t, tc)
    cos, sin = _rope_tables(t)
    qa = _heads(_rope(zall, cos, sin, False, "rope_q", view=(A_Q, int(offs[0]) // A_Q)), A_HEADS)
    ka = _heads(_rope(zall, cos, sin, False, "rope_k", view=(A_KV, int(offs[1]) // A_KV)), A_KV_HEADS)
    va = _heads(part(2, lat), A_KV_HEADS)
    kca, vca = _heads(part(1, con), A_KV_HEADS), _heads(part(2, con), A_KV_HEADS)
    kap, vap = _pad_rows(ka, WINDOW), _pad_rows(va, WINDOW)
    sink = wts["sink"].reshape(A_HEADS, 1, 1)
    o_a = _attn_fwd(qa, kap, vap, kca, vca, sink, "attn_fwd")

    vb = part(5)
    qk_cols = (int(offs[3]) // B_QK, int(offs[4]) // B_QK)
    rb = part(6, lat)
    zg = part(7)
    zg_f, zg_b = zg[:, :B_GATE_RANK], zg[:, B_GATE_RANK:]
    w2f, w2b, b2f, b2b = wts["w_a2_f"], wts["w_a2_b"], wts["b_a_f"], wts["b_a_b"]
    la_f = _gate_fwd(zg_f, w2f, b2f, "gate_f")
    la_b = _gate_fwd(zg_b, w2b, b2b, "gate_b")
    o_f, s_f, o_b, s_b = _gla_fwd(zall, zall, vb, la_f, la_b, t, "gla_fwd", qk_cols=qk_cols)
    gla_g = wts["gla_g"]
    go = _gla_out_fwd(o_f, o_b, rb, gla_g, "gla_out")
    cat = jnp.concatenate([_unheads(o_a), go], axis=-1).astype(BF16)
    big = fetch(3, cat)
    w_out, wi2_0, wo2_0 = big["w_out"], big["wi2_0"], big["wo2_0"]
    x2, y_mix0, h2 = _mm_resid(cat, w_out, 0, xc1, mods[0], 5, 1.0, t, "w_out", nxt=(mods[0], ng(0, 2), 6, 7, BF16))
    x3, sv_f2, h3 = ffn_fwd(x2, h2, 0, 6, wi2_0, wo2_0, t, "l0f2", (mods[1], ng(1, 0), 0, 1, BF16))

    big = fetch(4, x3)
    wi1_1, wo1_1, wi2_1, wo2_1 = big["wi1_1"], big["wo1_1"], big["wi2_1"], big["wo2_1"]
    x4, sv_g1, hp = ffn_fwd(x3, h3, 1, 0, wi1_1, wo1_1, t, "l1f1", (mods[1], ng(1, 1), 3, 4, F32))
    w_pool, pool_scale = wts["w_pool"], wts["pool_scale"]
    x5, pooled, ypre = _pool_fwd(hp, w_pool, pool_scale, x4, mods[1], 5, "pool_fwd")
    h5 = _modulate(x5, mods[1], ng(1, 2), 6, 7, t, BF16, "mod_l1f2")
    x6, sv_g2, _ = ffn_fwd(x5, h5, 1, 6, wi2_1, wo2_1, t, "l1f2", None)

    y_of = lambda saved: saved[4]
    dx6, loss_vec, dfinal_g, dy, acc_gate = _final_loss(x6, wts["final_g"], target, (y_of(sv_g2), mods[1], 8, 0.5),
                                                        "final_loss")
    grads["final_g"] = dfinal_g[0]

    dx5, acc_mod = ffn_bwd(dx6, dy, sv_g2, 1, 6, ng(1, 2), t, "l1f2", 0)
    record(1, 6, 2, ng(1, 2), acc_mod, acc_gate, 1)
    dhp, dw_pool, acc_pool = _pool_bwd(dx5, w_pool, pool_scale, pooled, ypre, mods[1], 5, "pool_bwd")
    grads["pool_scale"] = acc_pool[0]
    dmods[1][0][5] = acc_pool[1]
    dx4, acc_mod, dy, acc_gate = _modulate_bwd(x4, dhp, dx5, mods[1], ng(1, 1), 4, t, "mod_bwd_l1mix",
                                               then=(y_of(sv_g1), mods[1], 2, 0.5))
    record(1, 3, 1, ng(1, 1), acc_mod, None, 1)
    dx3, acc_mod, dy, acc_gate_next = ffn_bwd(dx4, dy, sv_g1, 1, 0, ng(1, 0), t, "l1f1", 1,
                                              then=(y_of(sv_f2), mods[0], 8, 0.5))
    record(1, 0, 0, ng(1, 0), acc_mod, acc_gate, 1)

    dx2, acc_mod, dymix, acc_gate_mix = ffn_bwd(dx3, dy, sv_f2, 0, 6, ng(0, 2), t, "l0f2", 2,
                                                then=(y_mix0, mods[0], 5, 1.0))
    record(0, 6, 2, ng(0, 2), acc_mod, acc_gate_next, 1)
    dmods[0][0][5] = acc_gate_mix[0, 0]
    dw_out = _mm_tn(cat, dymix, BF16, "dw_out")
    dcat = _mm([(dymix, w_out, 0, 0)], NT, cat.shape[1], F32, "dcat")
    do_a = _heads(dcat[:, :A_Q], A_HEADS)
    do_full, drb, dgla_g = _gla_out_bwd(o_f, o_b, rb, gla_g, dcat[:, A_Q:], "gla_out_bwd")
    grads["gla_g"] = dgla_g[0]
    dq_f, dk_f, dv_f, dla_f, dq_b, dk_b, dv_b, dla_b = _gla_bwd(zall, zall, vb, la_f, la_b, s_f, s_b, do_full, t,
                                                                "gla_bwd", qk_cols=qk_cols)
    dzg_f, dw2f, db2f = _gate_bwd(zg_f, w2f, b2f, dla_f, "gate_bwd_f")
    dzg_b, dw2b, db2b = _gate_bwd(zg_b, w2b, b2b, dla_b, "gate_bwd_b")
    grads.update(w_a2_f=dw2f, w_a2_b=dw2b, b_a_f=db2f[0], b_a_b=db2b[0])
    dqa_r, dkap, dvap, dkca, dvca, dsink = _attn_bwd(qa, kap, vap, kca, vca, sink, o_a, do_a, "attn_bwd")
    grads["sink"] = dsink[:, 0, 0]
    dqa = _rope(_unheads(dqa_r), cos, sin, True, "rope_bwd_q")
    dka = _rope(_unheads(dkap[:, WINDOW:WINDOW + t]), cos, sin, True, "rope_bwd_k")
    dva = dvap[:, WINDOW:WINDOW + t]
    dzg = jnp.concatenate([dzg_f, dzg_b, jnp.zeros((tc, n_proj - PROJ_DIM), F32)], axis=-1)
    dzall = _assemble_dz(
        [dqa, dka, _unheads(dva), None, None, None, drb, None],
        [None, _unheads(dkca), _unheads(dvca), None, None, None, None, None],
        [None, None, None, [dq_f, dq_b], [dk_f, dk_b], [dv_f, dv_b], None, [dzg]], n_proj, "assemble_dz")
    dw_in_t = _mm_tn(dzall, hc, BF16, "dw_in")
    token = emit(3, [dw_in_t, dw_out, dw_pool])
    dhc = _mm([(dzall, w_in_t, 0, 0)], NN, d, F32, "dhc", tm_pref=TALL_TILES, dep=token)
    dxc1, acc_mod, dy, acc_gate = _modulate_bwd(xc1, dhc, dx2, mods[0], ng(0, 1), 4, t, "mod_bwd_l0mix",
                                                then=(y_of(sv_f1), mods[0], 2, 0.5))
    record(0, 3, 1, ng(0, 1), acc_mod, None, 2)
    dxc0, acc_mod = ffn_bwd(dxc1, dy, sv_f1, 0, 0, ng(0, 0), t, "l0f1", 4, split=True)
    record(0, 0, 0, ng(0, 0), acc_mod, acc_gate, 2)

    grads["norm_g"] = jnp.stack([jnp.stack(dnorm[0]), jnp.stack(dnorm[1])])
    zero = jnp.zeros((d,), F32)
    dmods_arr = jnp.stack([jnp.stack([jnp.stack([v if v is not None else zero for v in dmods[l][s]])
                                      for s in range(2)]) for l in range(2)])
    return loss_vec, dxc0, grads, dmods_arr


def _pack(parts):
    flat = jnp.concatenate([p.reshape(-1).astype(F32) for p in parts])
    pad = (-flat.shape[0]) % 128
    return jnp.pad(flat, (0, pad))[None, :]


def _unpack(rows, shapes):
    out, off = [], 0
    for s in shapes:
        n = int(np.prod(s))
        out.append(rows[:, off:off + n].reshape((rows.shape[0],) + tuple(s)))
        off += n
    return out


def _cols_to_full(g):
    g = jnp.moveaxis(g, 0, -2)
    return g.reshape(g.shape[:-2] + (-1,))


def kernel(x, c, ctx, c_ctx, w_mod, b_mod, norm_g, ffn1_wi, ffn1_wo, ffn2_wi, ffn2_wo, w_in, w_a2_f, b_a_f, w_a2_b, b_a_b, sink, gla_g, w_out, w_pool, pool_scale, final_g, loss_target, m_c_ctx, m_w_mod, m_b_mod, m_norm_g, m_ffn1_wi, m_ffn1_wo, m_ffn2_wi, m_ffn2_wo, m_w_in, m_w_a2_f, m_b_a_f, m_w_a2_b, m_b_a_b, m_sink, m_gla_g, m_w_out, m_w_pool, m_pool_scale, m_final_g, v_c_ctx, v_w_mod, v_b_mod, v_norm_g, v_ffn1_wi, v_ffn1_wo, v_ffn2_wi, v_ffn2_wo, v_w_in, v_w_a2_f, v_b_a_f, v_w_a2_b, v_b_a_b, v_sink, v_gla_g, v_w_out, v_w_pool, v_pool_scale, v_final_g):
    t, d = x.shape[1], x.shape[2]
    me = _dev_index()
    nc = w_mod.shape[2]
    ncol_in = w_in.shape[2]
    ncol_pad = -(-ncol_in // 16) * 16

    small_shapes = [(d,), norm_g.shape, pool_scale.shape, w_a2_f.shape, w_a2_b.shape, w_pool.shape]
    g1 = _gather_small(_pack([c, norm_g, pool_scale, w_a2_f, w_a2_b, w_pool]), "gather_params")
    c_all, norm_g_all, pool_scale_all, w2f_all, w2b_all, w_pool_all = _unpack(g1, small_shapes)
    wts = {
        "norm_g": _cols_to_full(norm_g_all),
        "pool_scale": _cols_to_full(pool_scale_all),
        "w_a2_f": _cols_to_full(w2f_all)[0],
        "w_a2_b": _cols_to_full(w2b_all)[0],
        "w_pool": jnp.moveaxis(w_pool_all[:, 0], 0, 1).reshape(w_pool.shape[1], -1, w_pool.shape[3]),
        "b_a_f": b_a_f, "b_a_b": b_a_b, "sink": sink[0], "gla_g": gla_g, "final_g": final_g[None, :],
    }

    craw = jnp.concatenate([c_all, c_ctx[None, :], jnp.zeros((16 - N_DEV - 1, d), F32)], axis=0)
    b_cols = lax.dynamic_slice_in_dim(b_mod, me * nc, nc, axis=1)[:, None, :]
    mm_cols = _adaln_fwd(craw, w_mod, b_cols, "adaln_fwd")
    g2 = _gather_small(mm_cols.reshape(1, -1), "gather_mods").reshape(N_DEV, 2, 16, nc)
    mm_full = jnp.moveaxis(g2, 0, 2).reshape(2, 16, N_MOD, d)
    mods = jnp.stack([lax.dynamic_index_in_dim(mm_full, me, axis=1, keepdims=False), mm_full[:, N_DEV]], axis=1)
    mods = jnp.pad(mods, ((0, 0), (0, 0), (0, 16 - N_MOD), (0, 0)))

    tr = lambda w: jnp.swapaxes(w, 1, 2).astype(BF16)
    wi1_sh, wi2_sh, wo1_sh, wo2_sh = tr(ffn1_wi), tr(ffn2_wi), ffn1_wo.astype(BF16), ffn2_wo.astype(BF16)
    w_in_sh = jnp.pad(tr(w_in), ((0, 0), (0, ncol_pad - ncol_in), (0, 0)))
    groups = [
        {"wi1_0": wi1_sh[0:1]},
        {"wo1_0": wo1_sh[0:1]},
        {"w_in": w_in_sh},
        {"w_out": w_out.astype(BF16), "wi2_0": wi2_sh[0:1], "wo2_0": wo2_sh[0:1]},
        {"wi1_1": wi1_sh[1:2], "wo1_1": wo1_sh[1:2], "wi2_1": wi2_sh[1:2], "wo2_1": wo2_sh[1:2]},
    ]

    reach = lambda gi: NEAR_PEERS if gi == 0 else N_DEV - 1
    gathers, token = [], mods
    for gi, grp in enumerate(groups):
        lands = [_place_shard(s, me, f"gather_place_{nm}") for nm, s in grp.items()]
        gathers.append(_exchange_start(list(grp.values()), lands, True, 1 + gi, token, f"gather_start_{gi}",
                                       n_peers=reach(gi)))
        token = gathers[-1][4]
    n_proj = -(-(N_DEV * ncol_in) // 128) * 128
    forward_id = 1 + len(groups) + 6

    def fetch(gi, after):
        _, lands = _exchange_wait(gathers[gi], True, token if after is None else after, f"gather_wait_{gi}",
                                  n_peers=reach(gi))
        if gi == 0:
            rows = [s.shape[1] for s in groups[gi].values()]
            passed = _forward_start(lands, rows, forward_id, "gather_forward")
            lands = _forward_wait(passed, rows, passed[3], "gather_forward_wait")
        out = dict(zip(groups[gi].keys(), lands))
        if "w_in" in out:
            w_in_t = out.pop("w_in").reshape(1, N_DEV, ncol_pad, d)[:, :, :ncol_in].reshape(1, N_DEV * ncol_in, d)
            out["w_in_t"] = jnp.pad(w_in_t, ((0, 0), (0, n_proj - N_DEV * ncol_in), (0, 0)))
        return out

    scatters = []

    def emit(stage, arrays):
        if stage == 3:
            dw_in_t, dw_out, dw_pool = arrays
            dw_in_full = dw_in_t[:N_DEV * ncol_in].reshape(N_DEV, ncol_in, d)
            dw_in_full = jnp.pad(dw_in_full, ((0, 0), (0, ncol_pad - ncol_in), (0, 0)))
            srcs = [dw_in_full.reshape(1, N_DEV * ncol_pad, d), dw_out[None], dw_pool.astype(BF16)]
        else:
            srcs = [a[None] for a in arrays]
        lands = [lax.empty((N_DEV, s.shape[0], s.shape[1] // N_DEV, s.shape[2]), s.dtype) for s in srcs]
        scatters.append(_exchange_start(srcs, lands, False, 1 + len(groups) + stage, None, f"scatter_start_{stage}"))
        return scatters[-1][4]

    loss_vec, grad_x, grads, dmods = _local_step(x[0], ctx[0], loss_target[0], mods, wts, fetch, emit)

    def reduce_stage(stage, after):
        wholes, lands = _exchange_wait(scatters[stage], False, after, f"scatter_wait_{stage}")
        return [_sum_slots(ld, wh, me, f"sum_grad_{stage}_{i}") for i, (ld, wh) in enumerate(zip(lands, wholes))]

    (dwi2_1, dwo2_1), (dwi1_1, dwo1_1), (dwi2_0, dwo2_0), (dw_in_s, dw_out_s, dw_pool_s) = [
        reduce_stage(stage, grad_x) for stage in range(4)]
    back = lambda g: jnp.swapaxes(g, 1, 2)
    g_big = {
        "ffn2_wi": back(jnp.concatenate([dwi2_0, dwi2_1], axis=0)), "ffn2_wo": jnp.concatenate([dwo2_0, dwo2_1], axis=0),
        "w_in": back(dw_in_s[:, :ncol_in]), "w_out": dw_out_s, "w_pool": dw_pool_s[None],
    }

    order = ["c_ctx", "w_mod", "b_mod", "norm_g", "ffn1_wi", "ffn1_wo", "ffn2_wi", "ffn2_wo", "w_in", "w_a2_f", "b_a_f",
             "w_a2_b", "b_a_b", "sink", "gla_g", "w_out", "w_pool", "pool_scale", "final_g"]
    ws = dict(c_ctx=c_ctx, w_mod=w_mod, b_mod=b_mod, norm_g=norm_g, ffn1_wi=ffn1_wi, ffn1_wo=ffn1_wo, ffn2_wi=ffn2_wi,
              ffn2_wo=ffn2_wo, w_in=w_in, w_a2_f=w_a2_f, b_a_f=b_a_f, w_a2_b=w_a2_b, b_a_b=b_a_b, sink=sink, gla_g=gla_g,
              w_out=w_out, w_pool=w_pool, pool_scale=pool_scale, final_g=final_g)
    ms = dict(c_ctx=m_c_ctx, w_mod=m_w_mod, b_mod=m_b_mod, norm_g=m_norm_g, ffn1_wi=m_ffn1_wi, ffn1_wo=m_ffn1_wo,
              ffn2_wi=m_ffn2_wi, ffn2_wo=m_ffn2_wo, w_in=m_w_in, w_a2_f=m_w_a2_f, b_a_f=m_b_a_f, w_a2_b=m_w_a2_b,
              b_a_b=m_b_a_b, sink=m_sink, gla_g=m_gla_g, w_out=m_w_out, w_pool=m_w_pool, pool_scale=m_pool_scale,
              final_g=m_final_g)
    vs = dict(c_ctx=v_c_ctx, w_mod=v_w_mod, b_mod=v_b_mod, norm_g=v_norm_g, ffn1_wi=v_ffn1_wi, ffn1_wo=v_ffn1_wo,
              ffn2_wi=v_ffn2_wi, ffn2_wo=v_ffn2_wo, w_in=v_w_in, w_a2_f=v_w_a2_f, b_a_f=v_b_a_f, w_a2_b=v_w_a2_b,
              b_a_b=v_b_a_b, sink=v_sink, gla_g=v_gla_g, w_out=v_w_out, w_pool=v_w_pool, pool_scale=v_pool_scale,
              final_g=v_final_g)
    early, late = ["ffn2_wi", "ffn2_wo", "w_out", "w_in", "w_pool"], ["ffn1_wi", "ffn1_wo"]
    big = early + ["w_mod"] + late
    delta, new_m, new_v = {}, {}, {}
    g_all = dict(g_big)

    def adamw_big(nm):
        shp = ws[nm].shape
        two_d = lambda a: a.reshape(-1, shp[-1])
        dl, nm_, nv_ = _adamw(two_d(ws[nm]), two_d(g_all[nm]), two_d(ms[nm]), two_d(vs[nm]), f"adamw_{nm}")
        delta[nm], new_m[nm], new_v[nm] = dl.reshape(shp), nm_.reshape(shp), nv_.reshape(shp)

    for nm in early:
        adamw_big(nm)

    small_g = [dmods[:, :, :N_MOD].reshape(2, 2, N_MOD * d), grads["norm_g"], grads["pool_scale"], grads["final_g"],
               grads["b_a_f"], grads["b_a_b"], grads["sink"], grads["gla_g"], grads["w_a2_f"], grads["w_a2_b"],
               loss_vec]
    small_g_shapes = [a.shape for a in small_g]
    g3 = _gather_small(_pack(small_g), "gather_small_grads", dep=delta["w_out"])
    total = _sum_rows8(g3, "sum_small_grads")
    dmm_all = _unpack(g3, small_g_shapes[:1])[0]
    (dmm_sum, dnorm_g, dpool_scale, dfinal_g, db_a_f, db_a_b, dsink, dgla_g, dw_a2_f, dw_a2_b, loss_all) = [
        a[0] for a in _unpack(total, small_g_shapes)]
    loss = jnp.sum(loss_all)
    dmm_rows = jnp.concatenate([dmm_all[:, :, 0].transpose(1, 0, 2), dmm_sum[:, 1][:, None, :],
                                jnp.zeros((2, 16 - N_DEV - 1, N_MOD * d), F32)], axis=1)
    grad_b_mod = dmm_sum[:, 0] + dmm_sum[:, 1]
    dmm_cols = lax.dynamic_slice_in_dim(dmm_rows, me * nc, nc, axis=2)
    cs_t = jnp.transpose(_silu(craw)).astype(BF16)
    grad_w_mod, dcraw = _adaln_bwd(craw, cs_t, dmm_cols, w_mod, "adaln_bwd")
    g4 = _gather_small((dcraw[0, N_DEV] + dcraw[1, N_DEV])[None, :], "gather_c_ctx_grad")
    grad_c_ctx = _sum_rows8(g4, "sum_c_ctx_grad")[0]

    col = lambda v, n: lax.dynamic_slice_in_dim(v, me * n, n, axis=v.ndim - 1)
    g_small = {
        "c_ctx": grad_c_ctx, "b_mod": grad_b_mod, "norm_g": col(dnorm_g, norm_g.shape[2]),
        "w_a2_f": col(dw_a2_f, w_a2_f.shape[2])[None], "b_a_f": db_a_f[None], "w_a2_b": col(dw_a2_b, w_a2_b.shape[2])[None],
        "b_a_b": db_a_b[None], "sink": dsink[None], "gla_g": dgla_g[None], "pool_scale": col(dpool_scale, pool_scale.shape[1])[None],
        "final_g": dfinal_g,
    }
    g_all.update(g_small, w_mod=grad_w_mod)
    adamw_big("w_mod")
    rest = [nm for nm in order if nm not in big]
    rest_shapes = [ws[nm].shape for nm in rest]
    packed = [_pack([d_[nm].reshape(ws[nm].shape) for nm in rest]).reshape(-1, 128) for d_ in (ws, g_all, ms, vs)]
    pad_rows = (-packed[0].shape[0]) % 512
    packed = [jnp.pad(p, ((0, pad_rows), (0, 0))) for p in packed]
    outs = _adamw(*packed, "adamw_small")
    for dst, arr in zip((delta, new_m, new_v), outs):
        for nm, val in zip(rest, _unpack(arr.reshape(1, -1), rest_shapes)):
            dst[nm] = val[0]

    (dwo1_0,), (dwi1_0,) = reduce_stage(4, outs[0]), reduce_stage(5, outs[0])
    g_all["ffn1_wi"] = back(jnp.concatenate([dwi1_0, dwi1_1], axis=0))
    g_all["ffn1_wo"] = jnp.concatenate([dwo1_0, dwo1_1], axis=0)
    for nm in late:
        adamw_big(nm)
    g_all = {nm: g_all[nm].reshape(ws[nm].shape) for nm in order}

    return (loss, grad_x[None], *[g_all[nm] for nm in order], *[delta[nm] for nm in order],
            *[new_m[nm] for nm in order], *[new_v[nm] for nm in order])
```

```python
import functools

import numpy as np
import jax
import jax.numpy as jnp
from jax import lax
from jax.experimental import pallas as pl
from jax.experimental.pallas import tpu as pltpu

F32 = jnp.float32
BF16 = jnp.bfloat16
MESH = pl.DeviceIdType.MESH

N_DEV = 8
RMS_EPS = 1e-6
N_MOD = 9
GRID_W = 64
A_HEADS, A_KV_HEADS, A_HEAD_DIM = 8, 2, 64
A_REP = A_HEADS // A_KV_HEADS
WINDOW = 128
ROPE_BASE = 10000.0
B_HEADS, B_DK, B_DV = 4, 64, 128
B_GATE_RANK = 16
B_GATE_NORM = 16.0
B_CHUNK = 64
POOL_WINDOWS = (2, 4, 8, 16)
POOL_PAD = 8
A_Q = A_HEADS * A_HEAD_DIM
A_KV = A_KV_HEADS * A_HEAD_DIM
B_QK = B_HEADS * B_DK
B_V = B_HEADS * B_DV
PROJ_SIZES = (A_Q, A_KV, A_KV, B_QK, B_QK, B_V, B_V, 2 * B_GATE_RANK)
PROJ_DIM = sum(PROJ_SIZES)
ADAM_LR, ADAM_B1, ADAM_B2, ADAM_EPS, ADAM_WD, ADAM_STEP = 0.001, 0.9, 0.999, 1e-08, 0.01, 10

VMEM_LIMIT = 56 * 1024 * 1024
ROW_TILES = (512, 544, 256, 128, 64, 32, 16, 8)
TALL_TILES = (1024, 1088) + ROW_TILES

NN = ((1,), (0,))
NT = ((1,), (1,))
TN = ((0,), (0,))


def _dot(a, b, dims=NN, prec=None):
    return lax.dot_general(a, b, (dims, ((), ())), precision=prec, preferred_element_type=F32)


def _bdot(a, b, dims=NN):
    return _dot(a.astype(BF16), b.astype(BF16), dims)


def _dot_01(sel, x):
    hi = x.astype(BF16)
    rest = x - hi.astype(F32)
    mid = rest.astype(BF16)
    lo = (rest - mid.astype(F32)).astype(BF16)
    sel = sel.astype(BF16)
    return _dot(sel, hi) + _dot(sel, mid) + _dot(sel, lo)


def _params(sem=None, **kw):
    return pltpu.CompilerParams(dimension_semantics=sem, vmem_limit_bytes=VMEM_LIMIT, **kw)


def _silu(a):
    return a * jax.nn.sigmoid(a)


def _pick(n, prefs):
    for p in prefs:
        if n % p == 0:
            return p
    return n


def _full(shape):
    nd = len(shape)
    return pl.BlockSpec(shape, lambda *_: (0,) * nd)


def _peers():
    x, y, c = lax.axis_index("x"), lax.axis_index("y"), lax.axis_index("c")
    return x, y, c


def _dev_index():
    x, y, c = _peers()
    return 4 * x + 2 * y + c


def _others(x, y, c):
    return [(x, y, 1 - c), (1 - x, y, c), (x, 1 - y, c), (1 - x, 1 - y, c),
            (1 - x, y, 1 - c), (x, 1 - y, 1 - c), (1 - x, 1 - y, 1 - c)]


def _index_of(dev):
    return 4 * dev[0] + 2 * dev[1] + dev[2]


def _exchange_refs(gather, shapes, srcs, lands, a, me, to):
    if gather:
        r = shapes[a][1]
        return srcs[a], lands[a].at[:, pl.ds(_index_of(me) * r, r), :]
    r = shapes[a][1] // N_DEV
    return srcs[a].at[:, pl.ds(_index_of(to) * r, r), :], lands[a].at[_index_of(me)]


HBM_SPEC = pl.BlockSpec(memory_space=pltpu.HBM)
SEM_SPEC = pl.BlockSpec(memory_space=pltpu.SEMAPHORE)
EFFECT = pltpu.SideEffectType.DATAFLOW_SIDE_EFFECTING


NEAR_PEERS = 4


def _exchange_start(srcs, lands, gather, collective_id, dep, name, n_peers=N_DEV - 1):
    n = len(srcs)
    shapes = [s.shape for s in srcs]
    deps = [] if dep is None else [dep]

    def body(*refs):
        src_refs, land_refs = refs[:n], refs[n:2 * n]
        send_sems, recv_sems = refs[2 * n + len(deps)], refs[2 * n + len(deps) + 1]
        token = refs[-1]
        x, y, c = _peers()
        others = _others(x, y, c)[:n_peers]
        barrier = pltpu.get_barrier_semaphore()
        for peer in others:
            pl.semaphore_signal(barrier, inc=1, device_id=peer, device_id_type=MESH)
        pl.semaphore_wait(barrier, len(others))
        for a in range(n):
            for k, to in enumerate(others):
                src, dst = _exchange_refs(gather, shapes, src_refs, land_refs, a, (x, y, c), to)
                pltpu.make_async_remote_copy(src_ref=src, dst_ref=dst, send_sem=send_sems.at[7 * a + k],
                                             recv_sem=recv_sems.at[7 * a + k], device_id=to, device_id_type=MESH).start()
        token[...] = jnp.zeros_like(token)

    outs = pl.pallas_call(
        body, name=name,
        out_shape=(pltpu.SemaphoreType.DMA((7 * n,)), pltpu.SemaphoreType.DMA((7 * n,)),
                   *[pltpu.HBM(s.shape, s.dtype) for s in srcs], *[pltpu.HBM(l.shape, l.dtype) for l in lands],
                   jax.ShapeDtypeStruct((8, 128), F32)),
        in_specs=[HBM_SPEC] * (2 * n) + [pl.BlockSpec(memory_space=pl.ANY)] * len(deps),
        out_specs=(SEM_SPEC, SEM_SPEC, *[HBM_SPEC] * (2 * n), pl.BlockSpec(memory_space=pltpu.VMEM)),
        input_output_aliases={i: 2 + i for i in range(2 * n)},
        compiler_params=pltpu.CompilerParams(has_side_effects=EFFECT, collective_id=collective_id),
    )(*[pltpu.with_memory_space_constraint(s, pltpu.HBM) for s in srcs],
      *[pltpu.with_memory_space_constraint(l, pltpu.HBM) for l in lands], *deps)
    return outs[0], outs[1], list(outs[2:2 + n]), list(outs[2 + n:2 + 2 * n]), outs[-1]


def _exchange_wait(started, gather, after, name, n_peers=N_DEV - 1):
    send_sems, recv_sems, srcs, lands, _ = started
    n = len(srcs)
    shapes = [s.shape for s in srcs]

    def body(*refs):
        src_refs, land_refs = refs[:n], refs[n:2 * n]
        send_sems, recv_sems = refs[2 * n], refs[2 * n + 1]
        x, y, c = _peers()
        for a in range(n):
            for k, peer in enumerate(_others(x, y, c)[:n_peers]):
                src, _ = _exchange_refs(gather, shapes, src_refs, land_refs, a, (x, y, c), peer)
                _, dst = _exchange_refs(gather, shapes, src_refs, land_refs, a, peer, (x, y, c))
                copy = pltpu.make_async_remote_copy(src_ref=src, dst_ref=dst, send_sem=send_sems.at[7 * a + k],
                                                    recv_sem=recv_sems.at[7 * a + k], device_id=peer, device_id_type=MESH)
                copy.wait_send()
                copy.wait_recv()

    outs = pl.pallas_call(
        body, name=name,
        out_shape=(*[pltpu.HBM(s.shape, s.dtype) for s in srcs], *[pltpu.HBM(l.shape, l.dtype) for l in lands]),
        in_specs=[HBM_SPEC] * (2 * n) + [SEM_SPEC, SEM_SPEC, pl.BlockSpec(memory_space=pl.ANY)],
        out_specs=tuple([HBM_SPEC] * (2 * n)),
        input_output_aliases={i: i for i in range(2 * n)},
        compiler_params=pltpu.CompilerParams(has_side_effects=EFFECT),
    )(*srcs, *lands, send_sems, recv_sems, after)
    return list(outs[:n]), list(outs[n:])


def _forward_refs(land_refs, rows, a, others, j, received):
    origin = others[j + 3] if received else others[j]
    return land_refs[a].at[:, pl.ds(_index_of(origin) * rows[a], rows[a]), :]


def _forward_start(lands, rows, collective_id, name):
    n = len(lands)

    def body(*refs):
        land_refs, send_sems, recv_sems, token = refs[:n], refs[n], refs[n + 1], refs[-1]
        x, y, c = _peers()
        others = _others(x, y, c)
        barrier = pltpu.get_barrier_semaphore()
        pl.semaphore_signal(barrier, inc=1, device_id=others[0], device_id_type=MESH)
        pl.semaphore_wait(barrier, 1)
        for a in range(n):
            for j in (1, 2, 3):
                blk = _forward_refs(land_refs, rows, a, others, j, False)
                pltpu.make_async_remote_copy(src_ref=blk, dst_ref=blk, send_sem=send_sems.at[3 * a + j - 1],
                                             recv_sem=recv_sems.at[3 * a + j - 1], device_id=others[0],
                                             device_id_type=MESH).start()
        token[...] = jnp.zeros_like(token)

    outs = pl.pallas_call(
        body, name=name,
        out_shape=(pltpu.SemaphoreType.DMA((3 * n,)), pltpu.SemaphoreType.DMA((3 * n,)),
                   *[pltpu.HBM(l.shape, l.dtype) for l in lands], jax.ShapeDtypeStruct((8, 128), F32)),
        in_specs=[HBM_SPEC] * n,
        out_specs=(SEM_SPEC, SEM_SPEC, *[HBM_SPEC] * n, pl.BlockSpec(memory_space=pltpu.VMEM)),
        input_output_aliases={i: 2 + i for i in range(n)},
        compiler_params=pltpu.CompilerParams(has_side_effects=EFFECT, collective_id=collective_id),
    )(*[pltpu.with_memory_space_constraint(l, pltpu.HBM) for l in lands])
    return outs[0], outs[1], list(outs[2:2 + n]), outs[-1]


def _forward_wait(started, rows, after, name):
    send_sems, recv_sems, lands, _ = started
    n = len(lands)

    def body(*refs):
        land_refs, send_sems, recv_sems = refs[:n], refs[n], refs[n + 1]
        x, y, c = _peers()
        others = _others(x, y, c)
        for a in range(n):
            for j in (1, 2, 3):
                copy = pltpu.make_async_remote_copy(
                    src_ref=_forward_refs(land_refs, rows, a, others, j, False),
                    dst_ref=_forward_refs(land_refs, rows, a, others, j, True), send_sem=send_sems.at[3 * a + j - 1],
                    recv_sem=recv_sems.at[3 * a + j - 1], device_id=others[0], device_id_type=MESH)
                copy.wait_send()
                copy.wait_recv()

    outs = pl.pallas_call(
        body, name=name, out_shape=tuple(pltpu.HBM(l.shape, l.dtype) for l in lands),
        in_specs=[HBM_SPEC] * n + [SEM_SPEC, SEM_SPEC, pl.BlockSpec(memory_space=pl.ANY)],
        out_specs=tuple([HBM_SPEC] * n), input_output_aliases={i: i for i in range(n)},
        compiler_params=pltpu.CompilerParams(has_side_effects=EFFECT),
    )(*lands, send_sems, recv_sems, after)
    return list(outs)


def _place_shard(shard, me, name):
    a_, r, c = shard.shape
    tr = _pick(r, (352, 304, 256, 128, 64, 32, 16, 8))
    nr = r // tr

    def body(me_ref, in_ref, out_ref):
        out_ref[...] = in_ref[...]

    return pl.pallas_call(
        body, name=name,
        grid_spec=pltpu.PrefetchScalarGridSpec(
            num_scalar_prefetch=1, grid=(a_, nr),
            in_specs=[pl.BlockSpec((None, tr, c), lambda i, j, me_ref: (i, j, 0))],
            out_specs=pl.BlockSpec((None, tr, c), lambda i, j, me_ref: (i, me_ref[0] * nr + j, 0))),
        out_shape=jax.ShapeDtypeStruct((a_, N_DEV * r, c), shard.dtype),
        compiler_params=_params(("parallel", "parallel")),
    )(me.reshape(1).astype(jnp.int32), shard)


def _sum_slots(land, whole, me, name):
    _, a_, r, c = land.shape
    tr = _pick(r, (352, 256, 128, 64, 32, 16, 8))
    nr = r // tr

    def body(me_ref, land_ref, own_ref, out_ref):
        acc = None
        for s in range(N_DEV):
            part = jnp.where(me_ref[0] == s, own_ref[...], land_ref[s]).astype(F32)
            acc = part if acc is None else acc + part
        out_ref[...] = acc

    return pl.pallas_call(
        body, name=name,
        grid_spec=pltpu.PrefetchScalarGridSpec(
            num_scalar_prefetch=1, grid=(a_, nr),
            in_specs=[pl.BlockSpec((N_DEV, None, tr, c), lambda i, j, me_ref: (0, i, j, 0)),
                      pl.BlockSpec((None, tr, c), lambda i, j, me_ref: (i, me_ref[0] * nr + j, 0))],
            out_specs=pl.BlockSpec((None, tr, c), lambda i, j, me_ref: (i, j, 0))),
        out_shape=jax.ShapeDtypeStruct((a_, r, c), F32),
        compiler_params=_params(("parallel", "parallel")),
    )(me.reshape(1).astype(jnp.int32), land, whole)


def _gather_small(vec, name, dep=None):
    p = vec.shape[1]
    pp = -(-p // 1024) * 1024
    blk = jnp.pad(vec, ((0, 0), (0, pp - p))).reshape(8, pp // 8)
    deps = [] if dep is None else [dep]

    def body(in_ref, *rest):
        out_ref, send_sems, recv_sems = rest[-3:]
        x, y, c = _peers()
        me = 4 * x + 2 * y + c
        others = [(x, y, 1 - c), (1 - x, y, c), (x, 1 - y, c), (1 - x, 1 - y, c),
                  (1 - x, y, 1 - c), (x, 1 - y, 1 - c), (1 - x, 1 - y, 1 - c)]

        def rows(idx):
            return out_ref.at[pl.ds(pl.multiple_of(idx * 8, 8), 8), :]

        out_ref[pl.ds(pl.multiple_of(me * 8, 8), 8), :] = in_ref[...]

        def copy(k, dev, slot):
            return pltpu.make_async_remote_copy(
                src_ref=in_ref, dst_ref=rows(slot), send_sem=send_sems.at[k], recv_sem=recv_sems.at[k],
                device_id=dev, device_id_type=MESH)

        sends = [copy(k, dev, me) for k, dev in enumerate(others)]
        for cp in sends:
            cp.start()
        for k, dev in enumerate(others):
            copy(k, dev, 4 * dev[0] + 2 * dev[1] + dev[2]).wait_recv()
        for cp in sends:
            cp.wait_send()

    vm = pl.BlockSpec(memory_space=pltpu.VMEM)
    out = pl.pallas_call(
        body, name=name, out_shape=jax.ShapeDtypeStruct((8 * N_DEV, pp // 8), F32),
        in_specs=[vm] + [pl.BlockSpec(memory_space=pl.ANY)] * len(deps), out_specs=vm,
        scratch_shapes=[pltpu.SemaphoreType.DMA((7,)), pltpu.SemaphoreType.DMA((7,))],
        compiler_params=pltpu.CompilerParams(has_side_effects=True, vmem_limit_bytes=VMEM_LIMIT),
    )(blk, *deps)
    return out.reshape(N_DEV, pp)[:, :p]


def _sum_rows8(g, name):
    p = g.shape[1]

    def body(in_ref, out_ref):
        acc = in_ref[0:1, :]
        for s in range(1, N_DEV):
            acc = acc + in_ref[s:s + 1, :]
        out_ref[...] = acc

    return pl.pallas_call(body, name=name, out_shape=jax.ShapeDtypeStruct((1, p), F32),
                          compiler_params=_params())(g)


def _sel_row(mods_ref, is_ctx, k):
    return jnp.where(is_ctx, mods_ref[1, k:k + 1, :], mods_ref[0, k:k + 1, :])


def _stream_tile(m, n_x):
    span = n_x if m == n_x else int(np.gcd(n_x, m - n_x))
    return _pick(span, (512, 256, 128, 64, 32, 16, 8))


def _modulate(z, mods, g, ks, kc, n_x, out_dtype, name):
    m, d = z.shape
    tm = _stream_tile(m, n_x)

    def body(z_ref, mods_ref, g_ref, h_ref):
        is_ctx = pl.program_id(0) * tm >= n_x
        zz = z_ref[...]
        r = lax.rsqrt(jnp.mean(zz * zz, axis=-1, keepdims=True) + RMS_EPS)
        shift, scale = _sel_row(mods_ref, is_ctx, ks), _sel_row(mods_ref, is_ctx, kc)
        h_ref[...] = ((zz * r) * g_ref[...] * (1.0 + scale) + shift).astype(out_dtype)

    return pl.pallas_call(
        body, name=name, grid=(m // tm,),
        in_specs=[pl.BlockSpec((tm, d), lambda i: (i, 0)), _full(mods.shape), _full(g.shape)],
        out_specs=pl.BlockSpec((tm, d), lambda i: (i, 0)),
        out_shape=jax.ShapeDtypeStruct((m, d), out_dtype),
        compiler_params=_params(("parallel",)),
    )(z, mods, g)


def _gate_bwd_rows(dx, y, gate, coef):
    return (coef * gate * dx).astype(BF16), jnp.sum(coef * y * dx, axis=0, keepdims=True)


def _modulate_bwd(z, dh, dres, mods, g, kc, n_x, name, latent_only=False, then=None):
    m, d = z.shape
    tm = _stream_tile(m, n_x)
    first_ctx = n_x // tm
    res_blocks = dres.shape[0] // tm
    out_blocks = (n_x if latent_only else m) // tm
    extra = [] if then is None else [then[0], then[1]]

    def body(z_ref, dh_ref, dres_ref, mods_ref, g_ref, *rest):
        i = pl.program_id(0)
        is_ctx = i * tm >= n_x
        dx_ref, acc_ref = rest[len(extra)], rest[len(extra) + 1]

        @pl.when((i == 0) | (i == first_ctx))
        def _():
            acc_ref[...] = jnp.zeros_like(acc_ref)
            if then is not None:
                rest[-1][...] = jnp.zeros_like(rest[-1])

        zz, dhh = z_ref[...], dh_ref[...]
        r = lax.rsqrt(jnp.mean(zz * zz, axis=-1, keepdims=True) + RMS_EPS)
        nz = zz * r
        gain = g_ref[...] * (1.0 + _sel_row(mods_ref, is_ctx, kc))
        dn = dhh * gain
        dz = r * (dn - nz * jnp.mean(dn * nz, axis=-1, keepdims=True))
        dx = jnp.where(i < res_blocks, dres_ref[...], 0.0) + dz

        @pl.when(i < out_blocks)
        def _():
            dx_ref[...] = dx

        acc_ref[0:1, :] += jnp.sum(dhh, axis=0, keepdims=True)
        acc_ref[1:2, :] += jnp.sum(dhh * nz, axis=0, keepdims=True)
        if then is not None:
            y_ref, tmods_ref, dy_ref, gate_acc_ref = rest[0], rest[1], rest[-2], rest[-1]
            dy, part = _gate_bwd_rows(dx, y_ref[...], _sel_row(tmods_ref, is_ctx, then[2]), then[3])
            dy_ref[...] = dy
            gate_acc_ref[0:1, :] += part

    row = pl.BlockSpec((tm, d), lambda i: (i, 0))
    acc_spec = pl.BlockSpec((None, 8, d), lambda i: ((i * tm >= n_x).astype(jnp.int32), 0, 0))
    out_specs = [pl.BlockSpec((tm, d), lambda i: (jnp.minimum(i, out_blocks - 1), 0)), acc_spec]
    out_shape = [jax.ShapeDtypeStruct((out_blocks * tm, d), F32), jax.ShapeDtypeStruct((2, 8, d), F32)]
    in_specs = [row, row, pl.BlockSpec((tm, d), lambda i: (jnp.minimum(i, res_blocks - 1), 0)),
                _full(mods.shape), _full(g.shape)]
    if then is not None:
        in_specs += [row, _full(then[1].shape)]
        out_specs += [row, acc_spec]
        out_shape += [jax.ShapeDtypeStruct((m, d), BF16), jax.ShapeDtypeStruct((2, 8, d), F32)]
    return pl.pallas_call(
        body, name=name, grid=(m // tm,), in_specs=in_specs, out_specs=out_specs, out_shape=out_shape,
        compiler_params=_params(("arbitrary",)),
    )(z, dh, dres, mods, g, *extra)


def _ffn_up(h, wi_t, layer, name):
    m, d = h.shape
    f = wi_t.shape[1] // 2
    tm = _pick(m, ROW_TILES)

    def body(h_ref, w_ref, jac_ref, act_ref):
        hh = h_ref[...]
        a = _dot(hh, w_ref[0:f, :], NT)
        u = _dot(hh, w_ref[f:2 * f, :], NT)
        sg = jax.nn.sigmoid(a)
        s = a * sg
        jac_ref[:, 0:f] = (u * (sg * (1.0 + a * (1.0 - sg)))).astype(BF16)
        jac_ref[:, f:2 * f] = s.astype(BF16)
        act_ref[...] = (s * u).astype(BF16)

    return pl.pallas_call(
        body, name=name, grid=(m // tm,),
        in_specs=[pl.BlockSpec((tm, d), lambda i: (i, 0)),
                  pl.BlockSpec((None, 2 * f, d), lambda i: (layer, 0, 0))],
        out_specs=[pl.BlockSpec((tm, 2 * f), lambda i: (i, 0)), pl.BlockSpec((tm, f), lambda i: (i, 0))],
        out_shape=[jax.ShapeDtypeStruct((m, 2 * f), BF16), jax.ShapeDtypeStruct((m, f), BF16)],
        compiler_params=_params(("parallel",)),
    )(h, wi_t)


def _mm_resid(a, b, layer, res, mods, km, coef, n_x, name, nxt=None):
    m, k = a.shape
    n = b.shape[2]
    tm = _pick(m, (512, 256, 128, 64, 32, 16, 8))
    tn = n if nxt is not None else _pick(n, (1024, 512, 256, 128))
    extra = [] if nxt is None else [nxt[0], nxt[1]]

    def body(a_ref, b_ref, res_ref, mods_ref, *rest):
        is_ctx = pl.program_id(1) * tm >= n_x
        y = _dot(a_ref[...], b_ref[...])
        new = res_ref[...] + coef * _sel_row(mods_ref, is_ctx, km) * y
        if nxt is None:
            out_ref, y_ref = rest
        else:
            nmods_ref, g_ref, out_ref, y_ref, h_ref = rest
            r = lax.rsqrt(jnp.mean(new * new, axis=-1, keepdims=True) + RMS_EPS)
            shift, scale = _sel_row(nmods_ref, is_ctx, nxt[2]), _sel_row(nmods_ref, is_ctx, nxt[3])
            h_ref[...] = ((new * r) * g_ref[...] * (1.0 + scale) + shift).astype(nxt[4])
        y_ref[...] = y.astype(BF16)
        out_ref[...] = new

    tile = pl.BlockSpec((tm, tn), lambda j, i: (i, j))
    outs = [jax.ShapeDtypeStruct((m, n), F32), jax.ShapeDtypeStruct((m, n), BF16)]
    if nxt is not None:
        outs.append(jax.ShapeDtypeStruct((m, n), nxt[4]))
    return pl.pallas_call(
        body, name=name, grid=(n // tn, m // tm),
        in_specs=[pl.BlockSpec((tm, k), lambda j, i: (i, 0)),
                  pl.BlockSpec((None, k, tn), lambda j, i: (layer, 0, j)),
                  tile, pl.BlockSpec((2, 16, tn), lambda j, i: (0, 0, j))] + [_full(e.shape) for e in extra],
        out_specs=[tile] * len(outs), out_shape=outs,
        compiler_params=_params(("parallel", "parallel")),
    )(a, b, res, mods, *extra)


def _ffn_down_bwd(dy, wo, layer, au, name):
    m, d = dy.shape
    f = wo.shape[1]
    tm = _pick(m, ROW_TILES)

    def body(dy_ref, wo_ref, au_ref, dau_ref):
        dact = _dot(dy_ref[...], wo_ref[...], NT)
        dau_ref[:, 0:f] = (dact * au_ref[:, 0:f].astype(F32)).astype(BF16)
        dau_ref[:, f:2 * f] = (dact * au_ref[:, f:2 * f].astype(F32)).astype(BF16)

    wide = pl.BlockSpec((tm, 2 * f), lambda i: (i, 0))
    return pl.pallas_call(
        body, name=name, grid=(m // tm,),
        in_specs=[pl.BlockSpec((tm, d), lambda i: (i, 0)), pl.BlockSpec((None, f, d), lambda i: (layer, 0, 0)), wide],
        out_specs=wide, out_shape=jax.ShapeDtypeStruct((m, 2 * f), BF16),
        compiler_params=_params(("parallel",)),
    )(dy, wo, au)


def _mm(terms, dims, n, out_dtype, name, tm_pref=(512, 256, 128, 64, 32, 16, 8), tn_pref=(512, 256, 128), dep=None):
    m = terms[0][0].shape[0]
    tm = _pick(m, tm_pref)
    tn = _pick(n, tn_pref)
    nt = len(terms)
    deps = [] if dep is None else [dep]

    def body(*refs):
        out_ref = refs[-1]
        acc = None
        for t in range(nt):
            part = _dot(refs[2 * t][...].astype(BF16), refs[2 * t + 1][...].astype(BF16), dims)
            acc = part if acc is None else acc + part
        out_ref[...] = acc.astype(out_dtype)

    in_specs, args = [], []
    for a, b, layer, rb in terms:
        k = a.shape[1]
        in_specs.append(pl.BlockSpec((tm, k), lambda j, i: (i, 0)))
        if dims == NN:
            in_specs.append(pl.BlockSpec((None, k, tn), lambda j, i, layer=layer, rb=rb: (layer, rb, j)))
        else:
            nb = n // tn
            in_specs.append(pl.BlockSpec((None, tn, k), lambda j, i, layer=layer, rb=rb, nb=nb: (layer, rb * nb + j, 0)))
        args += [a, b]
    return pl.pallas_call(
        body, name=name, grid=(n // tn, m // tm), in_specs=in_specs + [pl.BlockSpec(memory_space=pl.ANY)] * len(deps),
        out_specs=pl.BlockSpec((tm, tn), lambda j, i: (i, j)),
        out_shape=jax.ShapeDtypeStruct((m, n), out_dtype),
        compiler_params=_params(("parallel", "parallel")),
    )(*args, *deps)


def _mm_tn(a, b, out_dtype, name, dep=None):
    t = a.shape[0]
    m, n = a.shape[1], b.shape[1]
    tm = _pick(m, (1408, 2432, 1024, 512, 256, 128))
    tn = _pick(n, (1024, 512, 256, 128))
    tk = _pick(t, TALL_TILES)
    deps = [] if dep is None else [dep]

    def body(a_ref, b_ref, *rest):
        out_ref, acc_ref = rest[-2:]
        kk = pl.program_id(2)

        @pl.when(kk == 0)
        def _():
            acc_ref[...] = jnp.zeros_like(acc_ref)

        acc_ref[...] += _dot(a_ref[...].astype(BF16), b_ref[...].astype(BF16), TN)

        @pl.when(kk == pl.num_programs(2) - 1)
        def _():
            out_ref[...] = acc_ref[...].astype(out_dtype)

    return pl.pallas_call(
        body, name=name, grid=(m // tm, n // tn, t // tk),
        in_specs=[pl.BlockSpec((tk, tm), lambda i, j, k: (k, i)), pl.BlockSpec((tk, tn), lambda i, j, k: (k, j))]
        + [pl.BlockSpec(memory_space=pl.ANY)] * len(deps),
        out_specs=pl.BlockSpec((tm, tn), lambda i, j, k: (i, j)),
        out_shape=jax.ShapeDtypeStruct((m, n), out_dtype),
        scratch_shapes=[pltpu.VMEM((tm, tn), F32)],
        compiler_params=_params(("parallel", "parallel", "arbitrary")),
    )(a, b, *deps)


def _stack_rows(a, b, name):
    ta, d = a.shape
    tm = _pick(int(np.gcd(ta, b.shape[0])), (256, 128, 64, 32, 16, 8))
    na, nb = ta // tm, b.shape[0] // tm

    def body(a_ref, b_ref, o_ref):
        o_ref[...] = jnp.where(pl.program_id(0) < na, a_ref[...], b_ref[...])

    return pl.pallas_call(
        body, name=name, grid=(na + nb,),
        in_specs=[pl.BlockSpec((tm, d), lambda i: (jnp.minimum(i, na - 1), 0)),
                  pl.BlockSpec((tm, d), lambda i: (jnp.maximum(i - na, 0), 0))],
        out_specs=pl.BlockSpec((tm, d), lambda i: (i, 0)),
        out_shape=jax.ShapeDtypeStruct((ta + b.shape[0], d), a.dtype),
        compiler_params=_params(("parallel",)),
    )(a, b)


def _assemble_dz(lat_parts, ctx_parts, both_parts, width, name):
    t = next(p.shape[0] for p in lat_parts if p is not None)
    l_ctx = next(p.shape[0] for p in ctx_parts if p is not None)
    tm = _pick(int(np.gcd(t, l_ctx)), (256, 128, 64, 32, 16, 8))
    nt, nl = t // tm, l_ctx // tm
    plan, args, in_specs, off = [], [], [], 0
    lat_spec = lambda w: pl.BlockSpec((tm, w), lambda i: (jnp.minimum(i, nt - 1), 0))
    ctx_spec = lambda w: pl.BlockSpec((tm, w), lambda i: (jnp.maximum(i - nt, 0), 0))
    all_spec = lambda w: pl.BlockSpec((tm, w), lambda i: (i, 0))
    for lat, ctx, both in zip(lat_parts, ctx_parts, both_parts):
        if both:
            w = both[0].shape[1]
            plan.append(("both", off, w, len(args), len(both)))
            args += both
            in_specs += [all_spec(w)] * len(both)
        else:
            w = (lat if lat is not None else ctx).shape[1]
            plan.append(("split", off, w, len(args), (lat is not None, ctx is not None)))
            for part, spec in ((lat, lat_spec), (ctx, ctx_spec)):
                if part is not None:
                    args.append(part)
                    in_specs.append(spec(w))
        off += w
    n_in = len(args)

    def body(*refs):
        out_ref = refs[n_in]
        is_ctx = pl.program_id(0) >= nt
        for kind, o, w, first, info in plan:
            if kind == "both":
                val = refs[first][...]
                for k in range(1, info):
                    val = val + refs[first + k][...]
            else:
                has_lat, has_ctx = info
                zero = jnp.zeros((tm, w), F32)
                lat = refs[first][...] if has_lat else zero
                ctx = refs[first + int(has_lat)][...] if has_ctx else zero
                val = jnp.where(is_ctx, ctx, lat)
            out_ref[:, o:o + w] = val.astype(BF16)
        if off < width:
            out_ref[:, off:width] = jnp.zeros((tm, width - off), BF16)

    return pl.pallas_call(
        body, name=name, grid=(nt + nl,), in_specs=in_specs,
        out_specs=pl.BlockSpec((tm, width), lambda i: (i, 0)),
        out_shape=jax.ShapeDtypeStruct((t + l_ctx, width), BF16),
        compiler_params=_params(("parallel",)),
    )(*args)


def _final_loss(x, g, target, then, name):
    t, d = x.shape
    tm = _stream_tile(t, t)
    y, tmods, km, coef = then

    def body(x_ref, g_ref, t_ref, y_ref, tmods_ref, dx_ref, loss_ref, dg_ref, dy_ref, gate_acc_ref):
        @pl.when(pl.program_id(0) == 0)
        def _():
            loss_ref[...] = jnp.zeros_like(loss_ref)
            dg_ref[...] = jnp.zeros_like(dg_ref)
            gate_acc_ref[...] = jnp.zeros_like(gate_acc_ref)

        xx, gg = x_ref[...], g_ref[...]
        r = lax.rsqrt(jnp.mean(xx * xx, axis=-1, keepdims=True) + RMS_EPS)
        nz = xx * r
        err = nz * gg - t_ref[...]
        loss_ref[...] += jnp.sum(err * err, axis=0, keepdims=True) * (0.5 / d)
        dout = err * (1.0 / d)
        dg_ref[...] += jnp.sum(dout * nz, axis=0, keepdims=True)
        dn = dout * gg
        dx = r * (dn - nz * jnp.mean(dn * nz, axis=-1, keepdims=True))
        dx_ref[...] = dx
        dy, part = _gate_bwd_rows(dx, y_ref[...], tmods_ref[0, km:km + 1, :], coef)
        dy_ref[...] = dy
        gate_acc_ref[0:1, :] += part

    row = pl.BlockSpec((tm, d), lambda i: (i, 0))
    vec = pl.BlockSpec((1, d), lambda i: (0, 0))
    acc = pl.BlockSpec((None, 8, d), lambda i: (0, 0, 0))
    return pl.pallas_call(
        body, name=name, grid=(t // tm,), in_specs=[row, vec, row, row, _full(tmods.shape)],
        out_specs=[row, vec, vec, row, acc],
        out_shape=[jax.ShapeDtypeStruct((t, d), F32), jax.ShapeDtypeStruct((1, d), F32),
                   jax.ShapeDtypeStruct((1, d), F32), jax.ShapeDtypeStruct((t, d), BF16),
                   jax.ShapeDtypeStruct((2, 8, d), F32)],
        compiler_params=_params(("arbitrary",)),
    )(x, g, target, y, tmods)


def _adaln_fwd(craw, w_mod, b_cols, name):
    lyr, d, nc = w_mod.shape

    def body(c_ref, w_ref, b_ref, out_ref):
        out_ref[...] = _bdot(_silu(c_ref[...]), w_ref[...]) + b_ref[...]

    return pl.pallas_call(
        body, name=name, grid=(lyr,),
        in_specs=[_full(craw.shape), pl.BlockSpec((None, d, nc), lambda l: (l, 0, 0)),
                  pl.BlockSpec((None, 1, nc), lambda l: (l, 0, 0))],
        out_specs=pl.BlockSpec((None, 16, nc), lambda l: (l, 0, 0)),
        out_shape=jax.ShapeDtypeStruct((lyr, 16, nc), F32),
        compiler_params=_params(("parallel",)),
    )(craw, w_mod, b_cols)


def _adaln_bwd(craw, cs_t, dmm_cols, w_mod, name):
    lyr, d, nc = w_mod.shape

    def body(c_ref, cst_ref, dmm_ref, w_ref, gw_ref, dc_ref):
        dmm = dmm_ref[...]
        gw_ref[...] = _bdot(cst_ref[...], dmm)
        cc = c_ref[...]
        sg = jax.nn.sigmoid(cc)
        dc_ref[...] = _bdot(dmm, w_ref[...], NT) * (sg * (1.0 + cc * (1.0 - sg)))

    wspec = pl.BlockSpec((None, d, nc), lambda l: (l, 0, 0))
    return pl.pallas_call(
        body, name=name, grid=(lyr,),
        in_specs=[_full(craw.shape), _full(cs_t.shape), pl.BlockSpec((None, 16, nc), lambda l: (l, 0, 0)), wspec],
        out_specs=[wspec, pl.BlockSpec((None, 16, d), lambda l: (l, 0, 0))],
        out_shape=[jax.ShapeDtypeStruct((lyr, d, nc), F32), jax.ShapeDtypeStruct((lyr, 16, d), F32)],
        compiler_params=_params(("parallel",)),
    )(craw, cs_t, dmm_cols, w_mod)


def _rope_tables(t):
    rows = np.repeat(np.arange(t // GRID_W, dtype=np.float32), GRID_W)
    cols = np.tile(np.arange(GRID_W, dtype=np.float32), t // GRID_W)
    n = A_HEAD_DIM // 4
    freqs = (ROPE_BASE ** (-np.arange(n, dtype=np.float32) / n)).astype(np.float32)
    ang_r, ang_c = (rows[:, None] * freqs).astype(np.float32), (cols[:, None] * freqs).astype(np.float32)
    cr, sr, cc, sc = np.cos(ang_r), np.sin(ang_r), np.cos(ang_c), np.sin(ang_c)
    cos = np.concatenate([cr, cr, cc, cc] * 2, axis=-1).astype(np.float32)
    sin = np.concatenate([-sr, sr, -sc, sc] * 2, axis=-1).astype(np.float32)
    return jnp.asarray(cos), jnp.asarray(sin)


def _rope(xt, cos, sin, adjoint, name, view=None):
    t = cos.shape[0]
    w, col = (xt.shape[1], 0) if view is None else view
    tb = _pick(t, (512, 256, 128))
    rep = w // cos.shape[1]

    def body(x_ref, c_ref, s_ref, o_ref):
        xx = x_ref[...]
        cc = jnp.concatenate([c_ref[...]] * rep, axis=1) if rep > 1 else c_ref[...]
        ss = jnp.concatenate([s_ref[...]] * rep, axis=1) if rep > 1 else s_ref[...]
        low = (lax.broadcasted_iota(jnp.int32, xx.shape, 1) % 32) < 16

        def partner(v):
            return jnp.where(low, pltpu.roll(v, w - 16, 1), pltpu.roll(v, 16, 1))

        if adjoint:
            o_ref[...] = xx * cc + partner(xx * ss)
        else:
            o_ref[...] = xx * cc + partner(xx) * ss

    blk = pl.BlockSpec((tb, w), lambda i: (i, 0))
    tab = pl.BlockSpec((tb, cos.shape[1]), lambda i: (i, 0))
    return pl.pallas_call(
        body, name=name, grid=(t // tb,), in_specs=[pl.BlockSpec((tb, w), lambda i: (i, col)), tab, tab],
        out_specs=blk, out_shape=jax.ShapeDtypeStruct((t, w), F32), compiler_params=_params(("parallel",)),
    )(xt, cos, sin)


def _attn_bias():
    i = (np.arange(A_REP * WINDOW) % WINDOW)[:, None]
    j = np.arange(3 * WINDOW)[None, :]
    near = np.abs(j - WINDOW - i) <= WINDOW
    variants = [near, near & (j >= WINDOW), near & (j < 2 * WINDOW), near & (j >= WINDOW) & (j < 2 * WINDOW)]
    return jnp.asarray(np.where(np.stack(variants), 0.0, -np.inf).astype(np.float32))


def _attn_bias_spec(nb):
    rows = A_REP * WINDOW
    return pl.BlockSpec((None, rows, 3 * WINDOW),
                        lambda g, n: ((n == 0).astype(jnp.int32) + 2 * (n == nb - 1).astype(jnp.int32), 0, 0))


def _attn_probs(q, kb, kc, sink, bias):
    scale = A_HEAD_DIM ** -0.5
    s1 = _bdot(q, kb, NT) * scale + bias
    s2 = _bdot(q, kc, NT) * scale
    mx = jnp.maximum(jnp.maximum(jnp.max(s1, axis=-1, keepdims=True), jnp.max(s2, axis=-1, keepdims=True)), sink)
    p1, p2, ps = jnp.exp(s1 - mx), jnp.exp(s2 - mx), jnp.exp(sink - mx)
    inv = 1.0 / (jnp.sum(p1, axis=-1, keepdims=True) + jnp.sum(p2, axis=-1, keepdims=True) + ps)
    return p1 * inv, p2 * inv, ps * inv


def _sink_rows(sink_ref):
    return jnp.concatenate([jnp.broadcast_to(sink_ref[r], (WINDOW, 1)) for r in range(A_REP)], axis=0)


def _attn_fwd(q, kp, vp, kc, vc, sink, name):
    hq, t, dh = q.shape
    nb = t // WINDOW
    lc = kc.shape[1]
    rows = A_REP * WINDOW

    def body(q_ref, k_ref, v_ref, kc_ref, vc_ref, sink_ref, bias_ref, o_ref):
        n = pl.program_id(1)
        start = pl.multiple_of(n * WINDOW, WINDOW)
        kb, vb = k_ref[pl.ds(start, 3 * WINDOW), :], v_ref[pl.ds(start, 3 * WINDOW), :]
        p1, p2, _ = _attn_probs(q_ref[...].reshape(rows, dh), kb, kc_ref[...], _sink_rows(sink_ref), bias_ref[...])
        o_ref[...] = (_bdot(p1, vb) + _bdot(p2, vc_ref[...])).reshape(A_REP, WINDOW, dh)

    qblk = pl.BlockSpec((A_REP, WINDOW, dh), lambda g, n: (g, n, 0))
    kfull = pl.BlockSpec((None, t + 2 * WINDOW, dh), lambda g, n: (g, 0, 0))
    cfull = pl.BlockSpec((None, lc, dh), lambda g, n: (g, 0, 0))
    return pl.pallas_call(
        body, name=name, grid=(hq // A_REP, nb),
        in_specs=[qblk, kfull, kfull, cfull, cfull, pl.BlockSpec((A_REP, 1, 1), lambda g, n: (g, 0, 0)),
                  _attn_bias_spec(nb)],
        out_specs=qblk, out_shape=jax.ShapeDtypeStruct((hq, t, dh), F32),
        compiler_params=_params(("parallel", "parallel")),
    )(q, kp, vp, kc, vc, sink, _attn_bias())


def _attn_bwd(q, kp, vp, kc, vc, sink, o, do, name):
    hq, t, dh = q.shape
    nb = t // WINDOW
    lc = kc.shape[1]
    scale = A_HEAD_DIM ** -0.5
    rows = A_REP * WINDOW

    def body(q_ref, k_ref, v_ref, kc_ref, vc_ref, sink_ref, o_ref, do_ref, bias_ref,
             dq_ref, dk_ref, dv_ref, dkc_ref, dvc_ref, dsink_ref):
        n = pl.program_id(1)

        @pl.when(n == 0)
        def _():
            dk_ref[...] = jnp.zeros_like(dk_ref)
            dv_ref[...] = jnp.zeros_like(dv_ref)
            dkc_ref[...] = jnp.zeros_like(dkc_ref)
            dvc_ref[...] = jnp.zeros_like(dvc_ref)
            dsink_ref[...] = jnp.zeros_like(dsink_ref)

        start = pl.multiple_of(n * WINDOW, WINDOW)
        band = pl.ds(start, 3 * WINDOW)
        qq, kb, vb, kcc, vcc = q_ref[...].reshape(rows, dh), k_ref[band, :], v_ref[band, :], kc_ref[...], vc_ref[...]
        p1, p2, ps = _attn_probs(qq, kb, kcc, _sink_rows(sink_ref), bias_ref[...])
        dout = do_ref[...].reshape(rows, dh)
        delta = jnp.sum(dout * o_ref[...].reshape(rows, dh), axis=-1, keepdims=True)
        ds1 = p1 * (_bdot(dout, vb, NT) - delta)
        ds2 = p2 * (_bdot(dout, vcc, NT) - delta)
        dq_ref[...] = ((_bdot(ds1, kb) + _bdot(ds2, kcc)) * scale).reshape(A_REP, WINDOW, dh)
        dk_ref[band, :] += _bdot(ds1.T, qq) * scale
        dv_ref[band, :] += _bdot(p1.T, dout)
        dkc_ref[...] += _bdot(ds2.T, qq) * scale
        dvc_ref[...] += _bdot(p2.T, dout)
        dsink_ref[...] += jnp.sum((-ps * delta).reshape(A_REP, WINDOW, 1), axis=1, keepdims=True)

    qblk = pl.BlockSpec((A_REP, WINDOW, dh), lambda g, n: (g, n, 0))
    kfull = pl.BlockSpec((None, t + 2 * WINDOW, dh), lambda g, n: (g, 0, 0))
    cfull = pl.BlockSpec((None, lc, dh), lambda g, n: (g, 0, 0))
    return pl.pallas_call(
        body, name=name, grid=(hq // A_REP, nb),
        in_specs=[qblk, kfull, kfull, cfull, cfull, pl.BlockSpec((A_REP, 1, 1), lambda g, n: (g, 0, 0)), qblk, qblk,
                  _attn_bias_spec(nb)],
        out_specs=[qblk, kfull, kfull, cfull, cfull, pl.BlockSpec((A_REP, 8, 128), lambda g, n: (g, 0, 0))],
        out_shape=[jax.ShapeDtypeStruct(q.shape, F32), jax.ShapeDtypeStruct(kp.shape, F32),
                   jax.ShapeDtypeStruct(kp.shape, F32), jax.ShapeDtypeStruct(kc.shape, F32),
                   jax.ShapeDtypeStruct(kc.shape, F32), jax.ShapeDtypeStruct((hq, 8, 128), F32)],
        compiler_params=_params(("parallel", "arbitrary")),
    )(q, kp, vp, kc, vc, sink, o, do, _attn_bias())


def _gate_fwd(zg, w2, b2, name):
    m = zg.shape[0]
    n = w2.shape[1]
    tm = _pick(m, TALL_TILES)

    def body(z_ref, w_ref, b_ref, o_ref):
        o_ref[...] = jax.nn.log_sigmoid(_bdot(z_ref[...], w_ref[...]) + b_ref[...]) / B_GATE_NORM

    return pl.pallas_call(
        body, name=name, grid=(m // tm,),
        in_specs=[pl.BlockSpec((tm, zg.shape[1]), lambda i: (i, 0)), _full(w2.shape), _full(b2.shape)],
        out_specs=pl.BlockSpec((tm, n), lambda i: (i, 0)), out_shape=jax.ShapeDtypeStruct((m, n), F32),
        compiler_params=_params(("parallel",)),
    )(zg, w2, b2)


def _gate_bwd(zg, w2, b2, dla, name):
    m, rk = zg.shape
    n = w2.shape[1]
    tm = _pick(m, TALL_TILES)

    def body(z_ref, w_ref, b_ref, d_ref, dz_ref, dw_ref, db_ref):
        @pl.when(pl.program_id(0) == 0)
        def _():
            dw_ref[...] = jnp.zeros_like(dw_ref)
            db_ref[...] = jnp.zeros_like(db_ref)

        zz, ww = z_ref[...], w_ref[...]
        pre = _bdot(zz, ww) + b_ref[...]
        dpre = d_ref[...] * (1.0 / B_GATE_NORM) * jax.nn.sigmoid(-pre)
        dz_ref[...] = _bdot(dpre, ww, NT)
        dw_ref[...] += _bdot(zz.T, dpre)
        db_ref[...] += jnp.sum(dpre, axis=0, keepdims=True)

    return pl.pallas_call(
        body, name=name, grid=(m // tm,),
        in_specs=[pl.BlockSpec((tm, rk), lambda i: (i, 0)), _full(w2.shape), _full(b2.shape),
                  pl.BlockSpec((tm, n), lambda i: (i, 0))],
        out_specs=[pl.BlockSpec((tm, rk), lambda i: (i, 0)), _full(w2.shape), _full(b2.shape)],
        out_shape=[jax.ShapeDtypeStruct((m, rk), F32), jax.ShapeDtypeStruct(w2.shape, F32),
                   jax.ShapeDtypeStruct(b2.shape, F32)],
        compiler_params=_params(("arbitrary",)),
    )(zg, w2, b2, dla)


def _chunk_order(step, n_x_chunks, n_chunks, reverse):
    n_c = n_chunks - n_x_chunks
    if reverse:
        return jnp.where(step < n_c, n_chunks - 1 - step, n_chunks - 1 - step)
    return jnp.where(step < n_c, n_x_chunks + step, step - n_c)


def _tri(reverse, transpose=False):
    i = lax.broadcasted_iota(jnp.int32, (B_CHUNK, B_CHUNK), 0)
    j = lax.broadcasted_iota(jnp.int32, (B_CHUNK, B_CHUNK), 1)
    if transpose:
        i, j = j, i
    return (j >= i) if reverse else (j <= i)


def _gla_chunk(q, k, la, reverse):
    g = _dot_01(_tri(reverse), la)
    last = 0 if reverse else B_CHUNK - 1
    gl = g[last:last + 1, :]
    eg, eng, egl = jnp.exp(g), jnp.exp(-g), jnp.exp(gl - g)
    decay_col = jnp.exp(jnp.sum(la.T, axis=1, keepdims=True))
    return q * (B_DK ** -0.5) * eg, k * eng, k * egl, eg, eng, egl, decay_col


def _head_of(shape, axis, width):
    return lax.broadcasted_iota(jnp.int32, shape, axis) // width


def _gla_chunks_per_step(n_chunks, n_x_chunks):
    return _pick(int(np.gcd(n_chunks - n_x_chunks, n_x_chunks)), (4, 2, 1))


def _gla_fwd(q, k, v, la_f, la_b, n_x, name, qk_cols=(0, 0)):
    tc, wk = la_f.shape
    wv = v.shape[1]
    hh = B_HEADS
    dk, dv = wk // hh, wv // hh
    nc, nxc = tc // B_CHUNK, n_x // B_CHUNK
    sub = _gla_chunks_per_step(nc, nxc)
    rows_per_step = sub * B_CHUNK
    orders = [functools.partial(_chunk_order, n_x_chunks=nxc // sub, n_chunks=nc // sub, reverse=rev)
              for rev in (False, True)]

    def body(*refs):
        ins, outs, s_refs = refs[:8], refs[8:12], refs[12:]

        @pl.when(pl.program_id(0) == 0)
        def _():
            for s_ref in s_refs:
                s_ref[...] = jnp.zeros_like(s_ref)

        lane_head = _head_of((B_CHUNK, wk), 1, dk)
        row_head = _head_of((wk, dv), 0, dk)
        for di, reverse in enumerate((False, True)):
            q_ref, k_ref, v_ref, la_ref = ins[4 * di:4 * di + 4]
            o_ref, s_save_ref = outs[2 * di:2 * di + 2]
            s_prev = s_refs[di][...]
            for c in (reversed(range(sub)) if reverse else range(sub)):
                rows = slice(c * B_CHUNK, (c + 1) * B_CHUNK)
                qt, kt, ke, _, _, _, decay_col = _gla_chunk(q_ref[rows, :], k_ref[rows, :], la_ref[rows, :], reverse)
                ke_t = ke.T
                update = jnp.zeros_like(s_prev)
                for h in range(hh):
                    vv = v_ref[rows, h * dv:(h + 1) * dv]
                    qm = jnp.where(lane_head == h, qt, 0.0)
                    att = jnp.where(_tri(reverse), _bdot(qm, kt, NT), 0.0)
                    o_ref[rows, h * dv:(h + 1) * dv] = _bdot(att, vv) + _bdot(qm, s_prev)
                    update = jnp.where(row_head == h, _bdot(ke_t, vv), update)
                s_save_ref[c] = s_prev
                s_prev = decay_col * s_prev + update
            s_refs[di][...] = s_prev

    def blk(w, order, col=0):
        return pl.BlockSpec((rows_per_step, w), lambda s: (order(s), col))

    def sblk(order):
        return pl.BlockSpec((sub, wk, dv), lambda s: (order(s), 0, 0))

    in_specs, out_specs = [], []
    for order in orders:
        in_specs += [blk(wk, order, qk_cols[0]), blk(wk, order, qk_cols[1]), blk(wv, order), blk(wk, order)]
        out_specs += [blk(wv, order), sblk(order)]
    o_shape, s_shape = jax.ShapeDtypeStruct((tc, wv), F32), jax.ShapeDtypeStruct((nc, wk, dv), F32)
    return pl.pallas_call(
        body, name=name, grid=(nc // sub,), in_specs=in_specs, out_specs=out_specs,
        out_shape=[o_shape, s_shape, o_shape, s_shape],
        scratch_shapes=[pltpu.VMEM((wk, dv), F32)] * 2,
        compiler_params=_params(("arbitrary",)),
    )(q, k, v, la_f, q, k, v, la_b)


def _gla_bwd(q, k, v, la_f, la_b, s_f, s_b, do, n_x, name, qk_cols=(0, 0)):
    tc, wk = la_f.shape
    wv = v.shape[1]
    hh = B_HEADS
    dk, dv = wk // hh, wv // hh
    nc, nxc = tc // B_CHUNK, n_x // B_CHUNK
    sub = _gla_chunks_per_step(nc, nxc)
    rows_per_step = sub * B_CHUNK
    nb, nxb = nc // sub, nxc // sub
    orders = [functools.partial(lambda s, rev: _chunk_order(nb - 1 - s, nxb, nb, rev), rev=rev) for rev in (False, True)]

    def body(*refs):
        ins, outs, ds_refs = refs[:12], refs[12:20], refs[20:]

        @pl.when(pl.program_id(0) == 0)
        def _():
            for ds_ref in ds_refs:
                ds_ref[...] = jnp.zeros_like(ds_ref)

        lane_head = _head_of((B_CHUNK, wk), 1, dk)
        row_head = _head_of((wk, dv), 0, dk)
        for di, reverse in enumerate((False, True)):
            q_ref, k_ref, v_ref, la_ref, s_save_ref, do_ref = ins[6 * di:6 * di + 6]
            dq_ref, dk_ref, dv_ref, dla_ref = outs[4 * di:4 * di + 4]
            mask = _tri(reverse)
            last = 0 if reverse else B_CHUNK - 1
            is_last = lax.broadcasted_iota(jnp.int32, (B_CHUNK, wk), 0) == last
            ds_new = ds_refs[di][...]
            for c in (range(sub) if reverse else reversed(range(sub))):
                rows = slice(c * B_CHUNK, (c + 1) * B_CHUNK)
                la = la_ref[rows, :]
                qt, kt, ke, eg, eng, egl, decay_col = _gla_chunk(q_ref[rows, :], k_ref[rows, :], la, reverse)
                qt_t = qt.T
                s_prev = s_save_ref[c]
                dqt, dkt, dke = jnp.zeros_like(qt), jnp.zeros_like(qt), jnp.zeros_like(qt)
                ds_add = jnp.zeros_like(ds_new)
                for h in range(hh):
                    cols = slice(h * dv, (h + 1) * dv)
                    vv, dout = v_ref[rows, cols], do_ref[rows, cols]
                    mine = lane_head == h
                    qm, km = jnp.where(mine, qt, 0.0), jnp.where(mine, ke, 0.0)
                    att = jnp.where(mask, _bdot(qm, kt, NT), 0.0)
                    datt = jnp.where(mask, _bdot(dout, vv, NT), 0.0)
                    dv_ref[rows, cols] = _bdot(att.T, dout) + _bdot(km, ds_new)
                    dqt = jnp.where(mine, _bdot(datt, kt) + _bdot(dout, s_prev, NT), dqt)
                    dkt = jnp.where(mine, _bdot(datt.T, qt), dkt)
                    dke = jnp.where(mine, _bdot(vv, ds_new, NT), dke)
                    ds_add = jnp.where(row_head == h, _bdot(qt_t, dout), ds_add)
                ddecay_row = jnp.sum((ds_new * s_prev).T, axis=0, keepdims=True)
                decay_row = jnp.exp(jnp.sum(la, axis=0, keepdims=True))
                dq_ref[rows, :] = dqt * (B_DK ** -0.5) * eg
                dk_ref[rows, :] = dkt * eng + dke * egl
                dgl = jnp.sum(dke * ke, axis=0, keepdims=True) + ddecay_row * decay_row
                dg = dqt * qt - dkt * kt - dke * ke + jnp.where(is_last, dgl, 0.0)
                dla_ref[rows, :] = _dot_01(_tri(reverse, transpose=True), dg)
                ds_new = decay_col * ds_new + ds_add
            ds_refs[di][...] = ds_new

    def blk(w, order, col=0):
        return pl.BlockSpec((rows_per_step, w), lambda s: (order(s), col))

    in_specs, out_specs = [], []
    for order in orders:
        in_specs += [blk(wk, order, qk_cols[0]), blk(wk, order, qk_cols[1]), blk(wv, order), blk(wk, order),
                     pl.BlockSpec((sub, wk, dv), lambda s, order=order: (order(s), 0, 0)), blk(wv, order)]
        out_specs += [blk(wk, order), blk(wk, order), blk(wv, order), blk(wk, order)]
    k_shape, v_shape = jax.ShapeDtypeStruct((tc, wk), F32), jax.ShapeDtypeStruct((tc, wv), F32)
    return pl.pallas_call(
        body, name=name, grid=(nb,), in_specs=in_specs, out_specs=out_specs,
        out_shape=[k_shape, k_shape, v_shape, k_shape] * 2,
        scratch_shapes=[pltpu.VMEM((wk, dv), F32)] * 2,
        compiler_params=_params(("arbitrary",)),
    )(q, k, v, la_f, s_f, do, q, k, v, la_b, s_b, do)


def _gla_out_fwd(o_f, o_b, r, g, name):
    t = r.shape[0]
    dv = g.shape[1]
    hh = r.shape[1] // dv
    tb = _pick(t, (512, 256, 128, 64))

    def body(of_ref, ob_ref, r_ref, g_ref, out_ref):
        for h in range(hh):
            cols = slice(h * dv, (h + 1) * dv)
            o = of_ref[:, cols] + ob_ref[:, cols]
            rs = lax.rsqrt(jnp.mean(o * o, axis=-1, keepdims=True) + RMS_EPS)
            out_ref[:, cols] = (o * rs) * g_ref[...] * _silu(r_ref[:, cols])

    rblk = pl.BlockSpec((tb, hh * dv), lambda i: (i, 0))
    return pl.pallas_call(
        body, name=name, grid=(t // tb,), in_specs=[rblk, rblk, rblk, _full(g.shape)], out_specs=rblk,
        out_shape=jax.ShapeDtypeStruct((t, hh * dv), F32), compiler_params=_params(("parallel",)),
    )(o_f, o_b, r, g)


def _gla_out_bwd(o_f, o_b, r, g, dout, name):
    tc = o_f.shape[0]
    t = r.shape[0]
    dv = g.shape[1]
    hh = r.shape[1] // dv
    tb = _pick(int(np.gcd(t, tc)), (256, 128, 64))
    nt = t // tb

    def body(of_ref, ob_ref, r_ref, g_ref, d_ref, do_ref, dr_ref, dg_ref):
        i = pl.program_id(0)

        @pl.when(i == 0)
        def _():
            dg_ref[...] = jnp.zeros_like(dg_ref)

        @pl.when(i >= nt)
        def _():
            do_ref[...] = jnp.zeros_like(do_ref)

        @pl.when(i < nt)
        def _():
            gg = g_ref[...]
            for h in range(hh):
                cols = slice(h * dv, (h + 1) * dv)
                o = of_ref[:, cols] + ob_ref[:, cols]
                rs = lax.rsqrt(jnp.mean(o * o, axis=-1, keepdims=True) + RMS_EPS)
                nz = o * rs
                rr, dd = r_ref[:, cols], d_ref[:, cols]
                sg = jax.nn.sigmoid(rr)
                dr_ref[:, cols] = dd * nz * gg * (sg * (1.0 + rr * (1.0 - sg)))
                dy = dd * (rr * sg)
                dg_ref[...] += jnp.sum(dy * nz, axis=0, keepdims=True)
                dn = dy * gg
                do_ref[:, cols] = rs * (dn - nz * jnp.mean(dn * nz, axis=-1, keepdims=True))

    oblk = pl.BlockSpec((tb, hh * dv), lambda i: (i, 0))
    rblk = pl.BlockSpec((tb, hh * dv), lambda i: (jnp.minimum(i, nt - 1), 0))
    return pl.pallas_call(
        body, name=name, grid=(tc // tb,), in_specs=[oblk, oblk, rblk, _full(g.shape), rblk],
        out_specs=[oblk, rblk, _full(g.shape)],
        out_shape=[jax.ShapeDtypeStruct(o_f.shape, F32), jax.ShapeDtypeStruct(r.shape, F32),
                   jax.ShapeDtypeStruct(g.shape, F32)],
        compiler_params=_params(("arbitrary",)),
    )(o_f, o_b, r, g, dout)


def _pool_tile(t):
    return _pick(t, tuple(p for p in (512, 256, 128, 64) if p + 2 * POOL_PAD <= t))


def _pool_window(i, tb, t):
    return pl.multiple_of(jnp.clip(i * tb - POOL_PAD, 0, t - (tb + 2 * POOL_PAD)), 8)


def _pool_band(half, i, tb, start, adjoint):
    pos = i * tb + lax.broadcasted_iota(jnp.int32, (tb, tb + 2 * POOL_PAD), 0)
    tok = start + lax.broadcasted_iota(jnp.int32, (tb, tb + 2 * POOL_PAD), 1)
    if adjoint:
        return (tok > pos - half) & (tok <= pos + half)
    return (tok >= pos - half) & (tok < pos + half)


def _pool_count(pos, half, t):
    return (jnp.minimum(pos + half, t) - jnp.maximum(pos - half, 0)).astype(F32)


def _pool_fwd(h, w_pool, pool_scale, res, mods, km, name):
    t, d = res.shape
    ng, gw = w_pool.shape[0], w_pool.shape[1]
    tb = _pool_tile(t)

    def body(h_ref, w_ref, ps_ref, res_ref, mods_ref, out_ref, pooled_ref, ypre_ref):
        gi, i = pl.program_id(0), pl.program_id(1)
        half = jnp.left_shift(1, gi)
        start = _pool_window(i, tb, t)
        win = h_ref[pl.ds(start, tb + 2 * POOL_PAD), :]
        total = _dot_01(_pool_band(half, i, tb, start, False), win)
        pos = i * tb + lax.broadcasted_iota(jnp.int32, (tb, 1), 0)
        pooled = total / _pool_count(pos, half, t) - h_ref[pl.ds(pl.multiple_of(i * tb, tb), tb), :]
        ypre = _bdot(pooled, w_ref[...])
        pooled_ref[...] = pooled.astype(BF16)
        ypre_ref[...] = ypre
        out_ref[...] = res_ref[...] + mods_ref[0, km:km + 1, :] * (ypre * ps_ref[...])

    tile = pl.BlockSpec((tb, gw), lambda gi, i: (i, gi))
    return pl.pallas_call(
        body, name=name, grid=(ng, t // tb),
        in_specs=[pl.BlockSpec((t, gw), lambda gi, i: (0, gi)),
                  pl.BlockSpec((None, gw, gw), lambda gi, i: (gi, 0, 0)),
                  pl.BlockSpec((1, gw), lambda gi, i: (0, gi)), tile,
                  pl.BlockSpec((2, 16, gw), lambda gi, i: (0, 0, gi))],
        out_specs=[tile, tile, tile],
        out_shape=[jax.ShapeDtypeStruct((t, d), F32), jax.ShapeDtypeStruct((t, d), BF16),
                   jax.ShapeDtypeStruct((t, d), F32)],
        compiler_params=_params(("parallel", "parallel")),
    )(h, w_pool, pool_scale, res, mods)


def _pool_bwd(dxp, w_pool, pool_scale, pooled, ypre, mods, km, name):
    t, d = pooled.shape
    ng, gw = w_pool.shape[0], w_pool.shape[1]
    tb = _pool_tile(t)

    def body(dxp_ref, w_ref, ps_ref, pooled_ref, ypre_ref, mods_ref, dh_ref, dw_ref, acc_ref):
        gi, i = pl.program_id(0), pl.program_id(1)

        @pl.when(i == 0)
        def _():
            dw_ref[...] = jnp.zeros_like(dw_ref)
            acc_ref[...] = jnp.zeros_like(acc_ref)

        half = jnp.left_shift(1, gi)
        mod, ps = mods_ref[0, km:km + 1, :], ps_ref[...]
        start = _pool_window(i, tb, t)
        dwin = dxp_ref[pl.ds(start, tb + 2 * POOL_PAD), :]
        dpooled = _bdot(dwin * (mod * ps), w_ref[...], NT)
        pos = start + lax.broadcasted_iota(jnp.int32, (tb + 2 * POOL_PAD, 1), 0)
        spread = _dot_01(_pool_band(half, i, tb, start, True), dpooled / _pool_count(pos, half, t))
        dxc, yp = dxp_ref[pl.ds(pl.multiple_of(i * tb, tb), tb), :], ypre_ref[...]
        dh_ref[...] = spread - _bdot(dxc * (mod * ps), w_ref[...], NT)
        dw_ref[...] += _bdot(pooled_ref[...].astype(F32).T, dxc * (mod * ps))
        acc_ref[0:1, :] += jnp.sum(dxc * yp * mod, axis=0, keepdims=True)
        acc_ref[1:2, :] += jnp.sum(dxc * yp * ps, axis=0, keepdims=True)

    tile = pl.BlockSpec((tb, gw), lambda gi, i: (i, gi))
    wblk = pl.BlockSpec((None, gw, gw), lambda gi, i: (gi, 0, 0))
    return pl.pallas_call(
        body, name=name, grid=(ng, t // tb),
        in_specs=[pl.BlockSpec((t, gw), lambda gi, i: (0, gi)), wblk,
                  pl.BlockSpec((1, gw), lambda gi, i: (0, gi)), tile, tile,
                  pl.BlockSpec((2, 16, gw), lambda gi, i: (0, 0, gi))],
        out_specs=[tile, wblk, pl.BlockSpec((8, gw), lambda gi, i: (0, gi))],
        out_shape=[jax.ShapeDtypeStruct((t, d), F32), jax.ShapeDtypeStruct(w_pool.shape, F32),
                   jax.ShapeDtypeStruct((8, d), F32)],
        compiler_params=_params(("arbitrary", "arbitrary")),
    )(dxp, w_pool, pool_scale, pooled, ypre, mods)


def _adamw(w, g, m, v, name):
    r, c = w.shape
    tr = _pick(r, (512, 352, 256, 128, 64, 32, 16, 8))
    c1 = 1.0 / (1.0 - ADAM_B1 ** ADAM_STEP)
    c2 = 1.0 / (1.0 - ADAM_B2 ** ADAM_STEP)

    def body(w_ref, g_ref, m_ref, v_ref, d_ref, nm_ref, nv_ref):
        gg = g_ref[...]
        nm = ADAM_B1 * m_ref[...] + (1.0 - ADAM_B1) * gg
        nv = ADAM_B2 * v_ref[...] + (1.0 - ADAM_B2) * (gg * gg)
        nm_ref[...] = nm
        nv_ref[...] = nv
        d_ref[...] = -ADAM_LR * ((nm * c1) / (jnp.sqrt(nv * c2) + ADAM_EPS) + ADAM_WD * w_ref[...])

    blk = pl.BlockSpec((tr, c), lambda i: (i, 0))
    shp = jax.ShapeDtypeStruct((r, c), F32)
    return pl.pallas_call(
        body, name=name, grid=(r // tr,), in_specs=[blk] * 4, out_specs=[blk] * 3, out_shape=[shp] * 3,
        compiler_params=_params(("parallel",)),
    )(w, g, m, v)


def _heads(z, n_heads):
    m = z.shape[0]
    return z.reshape(m, n_heads, -1).transpose(1, 0, 2)


def _unheads(zh):
    return zh.transpose(1, 0, 2).reshape(zh.shape[1], -1)


def _pad_rows(a, n):
    return jnp.pad(a, ((0, 0), (n, n), (0, 0))) if a.ndim == 3 else jnp.pad(a, ((n, n), (0, 0)))


def _local_step(x, ctx, target, mods, wts, fetch, emit):
    t, d = x.shape
    l_ctx = ctx.shape[0]
    tc = t + l_ctx
    norm_g = wts["norm_g"]
    ng = lambda l, k: norm_g[l, k][None, :]
    grads = {}
    dmods = [[[None] * N_MOD for _ in range(2)] for _ in range(2)]
    dnorm = [[None] * 3 for _ in range(2)]

    def ffn_fwd(z, h, l, kbase, wi, wo, n_x, tag, nxt):
        au, act = _ffn_up(h, wi, 0, f"ffn_up_{tag}")
        wo = wo(act) if callable(wo) else wo
        outs = _mm_resid(act, wo, 0, z, mods[l], kbase + 2, 0.5, n_x, f"ffn_down_{tag}", nxt=nxt)
        return outs[0], (z, h, au, act, outs[1], wi, wo), (outs[2] if nxt is not None else None)

    def ffn_bwd(dz_new, dy, saved, l, kbase, g, n_x, tag, stage, split=False, then=None):
        z, h, au, act, y, wi, wo = saved
        dau = _ffn_down_bwd(dy, wo, 0, au, f"ffn_down_bwd_{tag}")
        dwo = _mm_tn(act, dy, BF16, f"dwo_{tag}")
        if split:
            token = emit(stage, [dwo])
            dwi_t = _mm_tn(dau, h, BF16, f"dwi_{tag}", dep=token)
            token = emit(stage + 1, [dwi_t])
        else:
            dwi_t = _mm_tn(dau, h, BF16, f"dwi_{tag}")
            token = emit(stage, [dwi_t, dwo])
        dh = _mm([(dau, wi, 0, 0)], NN, d, F32, f"dh_{tag}", tm_pref=TALL_TILES, dep=token)
        return _modulate_bwd(z, dh, dz_new, mods[l], g, kbase + 1, n_x, f"mod_bwd_{tag}", latent_only=split, then=then)

    def record(l, kbase, k_norm, g, acc_mod, acc_gate, streams):
        total = None
        for s in range(streams):
            dmods[l][s][kbase] = acc_mod[s, 0]
            dmods[l][s][kbase + 1] = acc_mod[s, 1] * g[0]
            if acc_gate is not None:
                dmods[l][s][kbase + 2] = acc_gate[s, 0]
            part = acc_mod[s, 1] * (1.0 + mods[l][s, kbase + 1])
            total = part if total is None else total + part
        dnorm[l][k_norm] = total

    xc0 = _stack_rows(x, ctx, "stack_tokens")
    wi1_0 = fetch(0, None)["wi1_0"]
    h0 = _modulate(xc0, mods[0], ng(0, 0), 0, 1, t, BF16, "mod_l0f1")
    xc1, sv_f1, hc = ffn_fwd(xc0, h0, 0, 0, wi1_0, lambda act: fetch(1, act)["wo1_0"], t, "l0f1",
                             (mods[0], ng(0, 1), 3, 4, BF16))
    w_in_t = fetch(2, hc)["w_in_t"]
    n_proj = w_in_t.shape[1]
    zall = _mm([(hc, w_in_t, 0, 0)], NT, n_proj, F32, "proj", tm_pref=TALL_TILES,
               tn_pref=(n_proj,))
    offs = np.cumsum((0,) + PROJ_SIZES)
    part = lambda i, rows=slice(None): zall[rows, offs[i]:offs[i + 1]]
    lat, con = slice(0, t), slice(t, tc)
    cos, sin = _rope_tables(t)
    qa = _heads(_rope(zall, cos, sin, False, "rope_q", view=(A_Q, int(offs[0]) // A_Q)), A_HEADS)
    ka = _heads(_rope(zall, cos, sin, False, "rope_k", view=(A_KV, int(offs[1]) // A_KV)), A_KV_HEADS)
    va = _heads(part(2, lat), A_KV_HEADS)
    kca, vca = _heads(part(1, con), A_KV_HEADS), _heads(part(2, con), A_KV_HEADS)
    kap, vap = _pad_rows(ka, WINDOW), _pad_rows(va, WINDOW)
    sink = wts["sink"].reshape(A_HEADS, 1, 1)
    o_a = _attn_fwd(qa, kap, vap, kca, vca, sink, "attn_fwd")

    vb = part(5)
    qk_cols = (int(offs[3]) // B_QK, int(offs[4]) // B_QK)
    rb = part(6, lat)
    zg = part(7)
    zg_f, zg_b = zg[:, :B_GATE_RANK], zg[:, B_GATE_RANK:]
    w2f, w2b, b2f, b2b = wts["w_a2_f"], wts["w_a2_b"], wts["b_a_f"], wts["b_a_b"]
    la_f = _gate_fwd(zg_f, w2f, b2f, "gate_f")
    la_b = _gate_fwd(zg_b, w2b, b2b, "gate_b")
    o_f, s_f, o_b, s_b = _gla_fwd(zall, zall, vb, la_f, la_b, t, "gla_fwd", qk_cols=qk_cols)
    gla_g = wts["gla_g"]
    go = _gla_out_fwd(o_f, o_b, rb, gla_g, "gla_out")
    cat = jnp.concatenate([_unheads(o_a), go], axis=-1).astype(BF16)
    big = fetch(3, cat)
    w_out, wi2_0, wo2_0 = big["w_out"], big["wi2_0"], big["wo2_0"]
    x2, y_mix0, h2 = _mm_resid(cat, w_out, 0, xc1, mods[0], 5, 1.0, t, "w_out", nxt=(mods[0], ng(0, 2), 6, 7, BF16))
    x3, sv_f2, h3 = ffn_fwd(x2, h2, 0, 6, wi2_0, wo2_0, t, "l0f2", (mods[1], ng(1, 0), 0, 1, BF16))

    big = fetch(4, x3)
    wi1_1, wo1_1, wi2_1, wo2_1 = big["wi1_1"], big["wo1_1"], big["wi2_1"], big["wo2_1"]
    x4, sv_g1, hp = ffn_fwd(x3, h3, 1, 0, wi1_1, wo1_1, t, "l1f1", (mods[1], ng(1, 1), 3, 4, F32))
    w_pool, pool_scale = wts["w_pool"], wts["pool_scale"]
    x5, pooled, ypre = _pool_fwd(hp, w_pool, pool_scale, x4, mods[1], 5, "pool_fwd")
    h5 = _modulate(x5, mods[1], ng(1, 2), 6, 7, t, BF16, "mod_l1f2")
    x6, sv_g2, _ = ffn_fwd(x5, h5, 1, 6, wi2_1, wo2_1, t, "l1f2", None)

    y_of = lambda saved: saved[4]
    dx6, loss_vec, dfinal_g, dy, acc_gate = _final_loss(x6, wts["final_g"], target, (y_of(sv_g2), mods[1], 8, 0.5),
                                                        "final_loss")
    grads["final_g"] = dfinal_g[0]

    dx5, acc_mod = ffn_bwd(dx6, dy, sv_g2, 1, 6, ng(1, 2), t, "l1f2", 0)
    record(1, 6, 2, ng(1, 2), acc_mod, acc_gate, 1)
    dhp, dw_pool, acc_pool = _pool_bwd(dx5, w_pool, pool_scale, pooled, ypre, mods[1], 5, "pool_bwd")
    grads["pool_scale"] = acc_pool[0]
    dmods[1][0][5] = acc_pool[1]
    dx4, acc_mod, dy, acc_gate = _modulate_bwd(x4, dhp, dx5, mods[1], ng(1, 1), 4, t, "mod_bwd_l1mix",
                                               then=(y_of(sv_g1), mods[1], 2, 0.5))
    record(1, 3, 1, ng(1, 1), acc_mod, None, 1)
    dx3, acc_mod, dy, acc_gate_next = ffn_bwd(dx4, dy, sv_g1, 1, 0, ng(1, 0), t, "l1f1", 1,
                                              then=(y_of(sv_f2), mods[0], 8, 0.5))
    record(1, 0, 0, ng(1, 0), acc_mod, acc_gate, 1)

    dx2, acc_mod, dymix, acc_gate_mix = ffn_bwd(dx3, dy, sv_f2, 0, 6, ng(0, 2), t, "l0f2", 2,
                                                then=(y_mix0, mods[0], 5, 1.0))
    record(0, 6, 2, ng(0, 2), acc_mod, acc_gate_next, 1)
    dmods[0][0][5] = acc_gate_mix[0, 0]
    dw_out = _mm_tn(cat, dymix, BF16, "dw_out")
    dcat = _mm([(dymix, w_out, 0, 0)], NT, cat.shape[1], F32, "dcat")
    do_a = _heads(dcat[:, :A_Q], A_HEADS)
    do_full, drb, dgla_g = _gla_out_bwd(o_f, o_b, rb, gla_g, dcat[:, A_Q:], "gla_out_bwd")
    grads["gla_g"] = dgla_g[0]
    dq_f, dk_f, dv_f, dla_f, dq_b, dk_b, dv_b, dla_b = _gla_bwd(zall, zall, vb, la_f, la_b, s_f, s_b, do_full, t,
                                                                "gla_bwd", qk_cols=qk_cols)
    dzg_f, dw2f, db2f = _gate_bwd(zg_f, w2f, b2f, dla_f, "gate_bwd_f")
    dzg_b, dw2b, db2b = _gate_bwd(zg_b, w2b, b2b, dla_b, "gate_bwd_b")
    grads.update(w_a2_f=dw2f, w_a2_b=dw2b, b_a_f=db2f[0], b_a_b=db2b[0])
    dqa_r, dkap, dvap, dkca, dvca, dsink = _attn_bwd(qa, kap, vap, kca, vca, sink, o_a, do_a, "attn_bwd")
    grads["sink"] = dsink[:, 0, 0]
    dqa = _rope(_unheads(dqa_r), cos, sin, True, "rope_bwd_q")
    dka = _rope(_unheads(dkap[:, WINDOW:WINDOW + t]), cos, sin, True, "rope_bwd_k")
    dva = dvap[:, WINDOW:WINDOW + t]
    dzg = jnp.concatenate([dzg_f, dzg_b, jnp.zeros((tc, n_proj - PROJ_DIM), F32)], axis=-1)
    dzall = _assemble_dz(
        [dqa, dka, _unheads(dva), None, None, None, drb, None],
        [None, _unheads(dkca), _unheads(dvca), None, None, None, None, None],
        [None, None, None, [dq_f, dq_b], [dk_f, dk_b], [dv_f, dv_b], None, [dzg]], n_proj, "assemble_dz")
    dw_in_t = _mm_tn(dzall, hc, BF16, "dw_in")
    token = emit(3, [dw_in_t, dw_out, dw_pool])
    dhc = _mm([(dzall, w_in_t, 0, 0)], NN, d, F32, "dhc", tm_pref=TALL_TILES, dep=token)
    dxc1, acc_mod, dy, acc_gate = _modulate_bwd(xc1, dhc, dx2, mods[0], ng(0, 1), 4, t, "mod_bwd_l0mix",
                                                then=(y_of(sv_f1), mods[0], 2, 0.5))
    record(0, 3, 1, ng(0, 1), acc_mod, None, 2)
    dxc0, acc_mod = ffn_bwd(dxc1, dy, sv_f1, 0, 0, ng(0, 0), t, "l0f1", 4, split=True)
    record(0, 0, 0, ng(0, 0), acc_mod, acc_gate, 2)

    grads["norm_g"] = jnp.stack([jnp.stack(dnorm[0]), jnp.stack(dnorm[1])])
    zero = jnp.zeros((d,), F32)
    dmods_arr = jnp.stack([jnp.stack([jnp.stack([v if v is not None else zero for v in dmods[l][s]])
                                      for s in range(2)]) for l in range(2)])
    return loss_vec, dxc0, grads, dmods_arr


def _pack(parts):
    flat = jnp.concatenate([p.reshape(-1).astype(F32) for p in parts])
    pad = (-flat.shape[0]) % 128
    return jnp.pad(flat, (0, pad))[None, :]


def _unpack(rows, shapes):
    out, off = [], 0
    for s in shapes:
        n = int(np.prod(s))
        out.append(rows[:, off:off + n].reshape((rows.shape[0],) + tuple(s)))
        off += n
    return out


def _cols_to_full(g):
    g = jnp.moveaxis(g, 0, -2)
    return g.reshape(g.shape[:-2] + (-1,))


def kernel(x, c, ctx, c_ctx, w_mod, b_mod, norm_g, ffn1_wi, ffn1_wo, ffn2_wi, ffn2_wo, w_in, w_a2_f, b_a_f, w_a2_b, b_a_b, sink, gla_g, w_out, w_pool, pool_scale, final_g, loss_target, m_c_ctx, m_w_mod, m_b_mod, m_norm_g, m_ffn1_wi, m_ffn1_wo, m_ffn2_wi, m_ffn2_wo, m_w_in, m_w_a2_f, m_b_a_f, m_w_a2_b, m_b_a_b, m_sink, m_gla_g, m_w_out, m_w_pool, m_pool_scale, m_final_g, v_c_ctx, v_w_mod, v_b_mod, v_norm_g, v_ffn1_wi, v_ffn1_wo, v_ffn2_wi, v_ffn2_wo, v_w_in, v_w_a2_f, v_b_a_f, v_w_a2_b, v_b_a_b, v_sink, v_gla_g, v_w_out, v_w_pool, v_pool_scale, v_final_g):
    t, d = x.shape[1], x.shape[2]
    me = _dev_index()
    nc = w_mod.shape[2]
    ncol_in = w_in.shape[2]
    ncol_pad = -(-ncol_in // 16) * 16

    small_shapes = [(d,), norm_g.shape, pool_scale.shape, w_a2_f.shape, w_a2_b.shape, w_pool.shape]
    g1 = _gather_small(_pack([c, norm_g, pool_scale, w_a2_f, w_a2_b, w_pool]), "gather_params")
    c_all, norm_g_all, pool_scale_all, w2f_all, w2b_all, w_pool_all = _unpack(g1, small_shapes)
    wts = {
        "norm_g": _cols_to_full(norm_g_all),
        "pool_scale": _cols_to_full(pool_scale_all),
        "w_a2_f": _cols_to_full(w2f_all)[0],
        "w_a2_b": _cols_to_full(w2b_all)[0],
        "w_pool": jnp.moveaxis(w_pool_all[:, 0], 0, 1).reshape(w_pool.shape[1], -1, w_pool.shape[3]),
        "b_a_f": b_a_f, "b_a_b": b_a_b, "sink": sink[0], "gla_g": gla_g, "final_g": final_g[None, :],
    }

    craw = jnp.concatenate([c_all, c_ctx[None, :], jnp.zeros((16 - N_DEV - 1, d), F32)], axis=0)
    b_cols = lax.dynamic_slice_in_dim(b_mod, me * nc, nc, axis=1)[:, None, :]
    mm_cols = _adaln_fwd(craw, w_mod, b_cols, "adaln_fwd")
    g2 = _gather_small(mm_cols.reshape(1, -1), "gather_mods").reshape(N_DEV, 2, 16, nc)
    mm_full = jnp.moveaxis(g2, 0, 2).reshape(2, 16, N_MOD, d)
    mods = jnp.stack([lax.dynamic_index_in_dim(mm_full, me, axis=1, keepdims=False), mm_full[:, N_DEV]], axis=1)
    mods = jnp.pad(mods, ((0, 0), (0, 0), (0, 16 - N_MOD), (0, 0)))

    tr = lambda w: jnp.swapaxes(w, 1, 2).astype(BF16)
    wi1_sh, wi2_sh, wo1_sh, wo2_sh = tr(ffn1_wi), tr(ffn2_wi), ffn1_wo.astype(BF16), ffn2_wo.astype(BF16)
    w_in_sh = jnp.pad(tr(w_in), ((0, 0), (0, ncol_pad - ncol_in), (0, 0)))
    groups = [
        {"wi1_0": wi1_sh[0:1]},
        {"wo1_0": wo1_sh[0:1]},
        {"w_in": w_in_sh},
        {"w_out": w_out.astype(BF16), "wi2_0": wi2_sh[0:1], "wo2_0": wo2_sh[0:1]},
        {"wi1_1": wi1_sh[1:2], "wo1_1": wo1_sh[1:2], "wi2_1": wi2_sh[1:2], "wo2_1": wo2_sh[1:2]},
    ]

    reach = lambda gi: NEAR_PEERS if gi == 0 else N_DEV - 1
    gathers, token = [], mods
    for gi, grp in enumerate(groups):
        lands = [_place_shard(s, me, f"gather_place_{nm}") for nm, s in grp.items()]
        gathers.append(_exchange_start(list(grp.values()), lands, True, 1 + gi, token, f"gather_start_{gi}",
                                       n_peers=reach(gi)))
        token = gathers[-1][4]
    n_proj = -(-(N_DEV * ncol_in) // 128) * 128
    forward_id = 1 + len(groups) + 6

    def fetch(gi, after):
        _, lands = _exchange_wait(gathers[gi], True, token if after is None else after, f"gather_wait_{gi}",
                                  n_peers=reach(gi))
        if gi == 0:
            rows = [s.shape[1] for s in groups[gi].values()]
            passed = _forward_start(lands, rows, forward_id, "gather_forward")
            lands = _forward_wait(passed, rows, passed[3], "gather_forward_wait")
        out = dict(zip(groups[gi].keys(), lands))
        if "w_in" in out:
            w_in_t = out.pop("w_in").reshape(1, N_DEV, ncol_pad, d)[:, :, :ncol_in].reshape(1, N_DEV * ncol_in, d)
            out["w_in_t"] = jnp.pad(w_in_t, ((0, 0), (0, n_proj - N_DEV * ncol_in), (0, 0)))
        return out

    scatters = []

    def emit(stage, arrays):
        if stage == 3:
            dw_in_t, dw_out, dw_pool = arrays
            dw_in_full = dw_in_t[:N_DEV * ncol_in].reshape(N_DEV, ncol_in, d)
            dw_in_full = jnp.pad(dw_in_full, ((0, 0), (0, ncol_pad - ncol_in), (0, 0)))
            srcs = [dw_in_full.reshape(1, N_DEV * ncol_pad, d), dw_out[None], dw_pool.astype(BF16)]
        else:
            srcs = [a[None] for a in arrays]
        lands = [lax.empty((N_DEV, s.shape[0], s.shape[1] // N_DEV, s.shape[2]), s.dtype) for s in srcs]
        scatters.append(_exchange_start(srcs, lands, False, 1 + len(groups) + stage, None, f"scatter_start_{stage}"))
        return scatters[-1][4]

    loss_vec, grad_x, grads, dmods = _local_step(x[0], ctx[0], loss_target[0], mods, wts, fetch, emit)

    def reduce_stage(stage, after):
        wholes, lands = _exchange_wait(scatters[stage], False, after, f"scatter_wait_{stage}")
        return [_sum_slots(ld, wh, me, f"sum_grad_{stage}_{i}") for i, (ld, wh) in enumerate(zip(lands, wholes))]

    (dwi2_1, dwo2_1), (dwi1_1, dwo1_1), (dwi2_0, dwo2_0), (dw_in_s, dw_out_s, dw_pool_s) = [
        reduce_stage(stage, grad_x) for stage in range(4)]
    back = lambda g: jnp.swapaxes(g, 1, 2)
    g_big = {
        "ffn2_wi": back(jnp.concatenate([dwi2_0, dwi2_1], axis=0)), "ffn2_wo": jnp.concatenate([dwo2_0, dwo2_1], axis=0),
        "w_in": back(dw_in_s[:, :ncol_in]), "w_out": dw_out_s, "w_pool": dw_pool_s[None],
    }

    order = ["c_ctx", "w_mod", "b_mod", "norm_g", "ffn1_wi", "ffn1_wo", "ffn2_wi", "ffn2_wo", "w_in", "w_a2_f", "b_a_f",
             "w_a2_b", "b_a_b", "sink", "gla_g", "w_out", "w_pool", "pool_scale", "final_g"]
    ws = dict(c_ctx=c_ctx, w_mod=w_mod, b_mod=b_mod, norm_g=norm_g, ffn1_wi=ffn1_wi, ffn1_wo=ffn1_wo, ffn2_wi=ffn2_wi,
              ffn2_wo=ffn2_wo, w_in=w_in, w_a2_f=w_a2_f, b_a_f=b_a_f, w_a2_b=w_a2_b, b_a_b=b_a_b, sink=sink, gla_g=gla_g,
              w_out=w_out, w_pool=w_pool, pool_scale=pool_scale, final_g=final_g)
    ms = dict(c_ctx=m_c_ctx, w_mod=m_w_mod, b_mod=m_b_mod, norm_g=m_norm_g, ffn1_wi=m_ffn1_wi, ffn1_wo=m_ffn1_wo,
              ffn2_wi=m_ffn2_wi, ffn2_wo=m_ffn2_wo, w_in=m_w_in, w_a2_f=m_w_a2_f, b_a_f=m_b_a_f, w_a2_b=m_w_a2_b,
              b_a_b=m_b_a_b, sink=m_sink, gla_g=m_gla_g, w_out=m_w_out, w_pool=m_w_pool, pool_scale=m_pool_scale,
              final_g=m_final_g)
    vs = dict(c_ctx=v_c_ctx, w_mod=v_w_mod, b_mod=v_b_mod, norm_g=v_norm_g, ffn1_wi=v_ffn1_wi, ffn1_wo=v_ffn1_wo,
              ffn2_wi=v_ffn2_wi, ffn2_wo=v_ffn2_wo, w_in=v_w_in, w_a2_f=v_w_a2_f, b_a_f=v_b_a_f, w_a2_b=v_w_a2_b,
              b_a_b=v_b_a_b, sink=v_sink, gla_g=v_gla_g, w_out=v_w_out, w_pool=v_w_pool, pool_scale=v_pool_scale,
              final_g=v_final_g)
    early, late = ["ffn2_wi", "ffn2_wo", "w_out", "w_in", "w_pool"], ["ffn1_wi", "ffn1_wo"]
    big = early + ["w_mod"] + late
    delta, new_m, new_v = {}, {}, {}
    g_all = dict(g_big)

    def adamw_big(nm):
        shp = ws[nm].shape
        two_d = lambda a: a.reshape(-1, shp[-1])
        dl, nm_, nv_ = _adamw(two_d(ws[nm]), two_d(g_all[nm]), two_d(ms[nm]), two_d(vs[nm]), f"adamw_{nm}")
        delta[nm], new_m[nm], new_v[nm] = dl.reshape(shp), nm_.reshape(shp), nv_.reshape(shp)

    for nm in early:
        adamw_big(nm)

    small_g = [dmods[:, :, :N_MOD].reshape(2, 2, N_MOD * d), grads["norm_g"], grads["pool_scale"], grads["final_g"],
               grads["b_a_f"], grads["b_a_b"], grads["sink"], grads["gla_g"], grads["w_a2_f"], grads["w_a2_b"],
               loss_vec]
    small_g_shapes = [a.shape for a in small_g]
    g3 = _gather_small(_pack(small_g), "gather_small_grads", dep=delta["w_out"])
    total = _sum_rows8(g3, "sum_small_grads")
    dmm_all = _unpack(g3, small_g_shapes[:1])[0]
    (dmm_sum, dnorm_g, dpool_scale, dfinal_g, db_a_f, db_a_b, dsink, dgla_g, dw_a2_f, dw_a2_b, loss_all) = [
        a[0] for a in _unpack(total, small_g_shapes)]
    loss = jnp.sum(loss_all)
    dmm_rows = jnp.concatenate([dmm_all[:, :, 0].transpose(1, 0, 2), dmm_sum[:, 1][:, None, :],
                                jnp.zeros((2, 16 - N_DEV - 1, N_MOD * d), F32)], axis=1)
    grad_b_mod = dmm_sum[:, 0] + dmm_sum[:, 1]
    dmm_cols = lax.dynamic_slice_in_dim(dmm_rows, me * nc, nc, axis=2)
    cs_t = jnp.transpose(_silu(craw)).astype(BF16)
    grad_w_mod, dcraw = _adaln_bwd(craw, cs_t, dmm_cols, w_mod, "adaln_bwd")
    g4 = _gather_small((dcraw[0, N_DEV] + dcraw[1, N_DEV])[None, :], "gather_c_ctx_grad")
    grad_c_ctx = _sum_rows8(g4, "sum_c_ctx_grad")[0]

    col = lambda v, n: lax.dynamic_slice_in_dim(v, me * n, n, axis=v.ndim - 1)
    g_small = {
        "c_ctx": grad_c_ctx, "b_mod": grad_b_mod, "norm_g": col(dnorm_g, norm_g.shape[2]),
        "w_a2_f": col(dw_a2_f, w_a2_f.shape[2])[None], "b_a_f": db_a_f[None], "w_a2_b": col(dw_a2_b, w_a2_b.shape[2])[None],
        "b_a_b": db_a_b[None], "sink": dsink[None], "gla_g": dgla_g[None], "pool_scale": col(dpool_scale, pool_scale.shape[1])[None],
        "final_g": dfinal_g,
    }
    g_all.update(g_small, w_mod=grad_w_mod)
    adamw_big("w_mod")
    rest = [nm for nm in order if nm not in big]
    rest_shapes = [ws[nm].shape for nm in rest]
    packed = [_pack([d_[nm].reshape(ws[nm].shape) for nm in rest]).reshape(-1, 128) for d_ in (ws, g_all, ms, vs)]
    pad_rows = (-packed[0].shape[0]) % 512
    packed = [jnp.pad(p, ((0, pad_rows), (0, 0))) for p in packed]
    outs = _adamw(*packed, "adamw_small")
    for dst, arr in zip((delta, new_m, new_v), outs):
        for nm, val in zip(rest, _unpack(arr.reshape(1, -1), rest_shapes)):
            dst[nm] = val[0]

    (dwo1_0,), (dwi1_0,) = reduce_stage(4, outs[0]), reduce_stage(5, outs[0])
    g_all["ffn1_wi"] = back(jnp.concatenate([dwi1_0, dwi1_1], axis=0))
    g_all["ffn1_wo"] = jnp.concatenate([dwo1_0, dwo1_1], axis=0)
    for nm in late:
        adamw_big(nm)
    g_all = {nm: g_all[nm].reshape(ws[nm].shape) for nm in order}

    return (loss, grad_x[None], *[g_all[nm] for nm in order], *[delta[nm] for nm in order],
            *[new_m[nm] for nm in order], *[new_v[nm] for nm in order])
```

```python
import functools

import numpy as np
import jax
import jax.numpy as jnp
from jax import lax
from jax.experimental import pallas as pl
from jax.experimental.pallas import tpu as pltpu

F32 = jnp.float32
BF16 = jnp.bfloat16
MESH = pl.DeviceIdType.MESH

N_DEV = 8
RMS_EPS = 1e-6
N_MOD = 9
GRID_W = 64
A_HEADS, A_KV_HEADS, A_HEAD_DIM = 8, 2, 64
A_REP = A_HEADS // A_KV_HEADS
WINDOW = 128
ROPE_BASE = 10000.0
B_HEADS, B_DK, B_DV = 4, 64, 128
B_GATE_RANK = 16
B_GATE_NORM = 16.0
B_CHUNK = 64
POOL_WINDOWS = (2, 4, 8, 16)
POOL_PAD = 8
A_Q = A_HEADS * A_HEAD_DIM
A_KV = A_KV_HEADS * A_HEAD_DIM
B_QK = B_HEADS * B_DK
B_V = B_HEADS * B_DV
PROJ_SIZES = (A_Q, A_KV, A_KV, B_QK, B_QK, B_V, B_V, 2 * B_GATE_RANK)
PROJ_DIM = sum(PROJ_SIZES)
ADAM_LR, ADAM_B1, ADAM_B2, ADAM_EPS, ADAM_WD, ADAM_STEP = 0.001, 0.9, 0.999, 1e-08, 0.01, 10

VMEM_LIMIT = 56 * 1024 * 1024
ROW_TILES = (512, 544, 256, 128, 64, 32, 16, 8)
TALL_TILES = (1024, 1088) + ROW_TILES

NN = ((1,), (0,))
NT = ((1,), (1,))
TN = ((0,), (0,))


def _dot(a, b, dims=NN, prec=None):
    return lax.dot_general(a, b, (dims, ((), ())), precision=prec, preferred_element_type=F32)


def _bdot(a, b, dims=NN):
    return _dot(a.astype(BF16), b.astype(BF16), dims)


def _dot_01(sel, x):
    hi = x.astype(BF16)
    rest = x - hi.astype(F32)
    mid = rest.astype(BF16)
    lo = (rest - mid.astype(F32)).astype(BF16)
    sel = sel.astype(BF16)
    return _dot(sel, hi) + _dot(sel, mid) + _dot(sel, lo)


def _params(sem=None, **kw):
    return pltpu.CompilerParams(dimension_semantics=sem, vmem_limit_bytes=VMEM_LIMIT, **kw)


def _silu(a):
    return a * jax.nn.sigmoid(a)


def _pick(n, prefs):
    for p in prefs:
        if n % p == 0:
            return p
    return n


def _full(shape):
    nd = len(shape)
    return pl.BlockSpec(shape, lambda *_: (0,) * nd)


def _peers():
    x, y, c = lax.axis_index("x"), lax.axis_index("y"), lax.axis_index("c")
    return x, y, c


def _dev_index():
    x, y, c = _peers()
    return 4 * x + 2 * y + c


def _others(x, y, c):
    return [(x, y, 1 - c), (1 - x, y, c), (x, 1 - y, c), (1 - x, 1 - y, c),
            (1 - x, y, 1 - c), (x, 1 - y, 1 - c), (1 - x, 1 - y, 1 - c)]


def _index_of(dev):
    return 4 * dev[0] + 2 * dev[1] + dev[2]


def _exchange_refs(gather, shapes, srcs, lands, a, me, to):
    if gather:
        r = shapes[a][1]
        return srcs[a], lands[a].at[:, pl.ds(_index_of(me) * r, r), :]
    r = shapes[a][1] // N_DEV
    return srcs[a].at[:, pl.ds(_index_of(to) * r, r), :], lands[a].at[_index_of(me)]


HBM_SPEC = pl.BlockSpec(memory_space=pltpu.HBM)
SEM_SPEC = pl.BlockSpec(memory_space=pltpu.SEMAPHORE)
EFFECT = pltpu.SideEffectType.DATAFLOW_SIDE_EFFECTING


NEAR_PEERS = 4


def _exchange_start(srcs, lands, gather, collective_id, dep, name, n_peers=N_DEV - 1):
    n = len(srcs)
    shapes = [s.shape for s in srcs]
    deps = [] if dep is None else [dep]

    def body(*refs):
        src_refs, land_refs = refs[:n], refs[n:2 * n]
        send_sems, recv_sems = refs[2 * n + len(deps)], refs[2 * n + len(deps) + 1]
        token = refs[-1]
        x, y, c = _peers()
        others = _others(x, y, c)[:n_peers]
        barrier = pltpu.get_barrier_semaphore()
        for peer in others:
            pl.semaphore_signal(barrier, inc=1, device_id=peer, device_id_type=MESH)
        pl.semaphore_wait(barrier, len(others))
        for a in range(n):
            for k, to in enumerate(others):
                src, dst = _exchange_refs(gather, shapes, src_refs, land_refs, a, (x, y, c), to)
                pltpu.make_async_remote_copy(src_ref=src, dst_ref=dst, send_sem=send_sems.at[7 * a + k],
                                             recv_sem=recv_sems.at[7 * a + k], device_id=to, device_id_type=MESH).start()
        token[...] = jnp.zeros_like(token)

    outs = pl.pallas_call(
        body, name=name,
        out_shape=(pltpu.SemaphoreType.DMA((7 * n,)), pltpu.SemaphoreType.DMA((7 * n,)),
                   *[pltpu.HBM(s.shape, s.dtype) for s in srcs], *[pltpu.HBM(l.shape, l.dtype) for l in lands],
                   jax.ShapeDtypeStruct((8, 128), F32)),
        in_specs=[HBM_SPEC] * (2 * n) + [pl.BlockSpec(memory_space=pl.ANY)] * len(deps),
        out_specs=(SEM_SPEC, SEM_SPEC, *[HBM_SPEC] * (2 * n), pl.BlockSpec(memory_space=pltpu.VMEM)),
        input_output_aliases={i: 2 + i for i in range(2 * n)},
        compiler_params=pltpu.CompilerParams(has_side_effects=EFFECT, collective_id=collective_id),
    )(*[pltpu.with_memory_space_constraint(s, pltpu.HBM) for s in srcs],
      *[pltpu.with_memory_space_constraint(l, pltpu.HBM) for l in lands], *deps)
    return outs[0], outs[1], list(outs[2:2 + n]), list(outs[2 + n:2 + 2 * n]), outs[-1]


def _exchange_wait(started, gather, after, name, n_peers=N_DEV - 1):
    send_sems, recv_sems, srcs, lands, _ = started
    n = len(srcs)
    shapes = [s.shape for s in srcs]

    def body(*refs):
        src_refs, land_refs = refs[:n], refs[n:2 * n]
        send_sems, recv_sems = refs[2 * n], refs[2 * n + 1]
        x, y, c = _peers()
        for a in range(n):
            for k, peer in enumerate(_others(x, y, c)[:n_peers]):
                src, _ = _exchange_refs(gather, shapes, src_refs, land_refs, a, (x, y, c), peer)
                _, dst = _exchange_refs(gather, shapes, src_refs, land_refs, a, peer, (x, y, c))
                copy = pltpu.make_async_remote_copy(src_ref=src, dst_ref=dst, send_sem=send_sems.at[7 * a + k],
                                                    recv_sem=recv_sems.at[7 * a + k], device_id=peer, device_id_type=MESH)
                copy.wait_send()
                copy.wait_recv()

    outs = pl.pallas_call(
        body, name=name,
        out_shape=(*[pltpu.HBM(s.shape, s.dtype) for s in srcs], *[pltpu.HBM(l.shape, l.dtype) for l in lands]),
        in_specs=[HBM_SPEC] * (2 * n) + [SEM_SPEC, SEM_SPEC, pl.BlockSpec(memory_space=pl.ANY)],
        out_specs=tuple([HBM_SPEC] * (2 * n)),
        input_output_aliases={i: i for i in range(2 * n)},
        compiler_params=pltpu.CompilerParams(has_side_effects=EFFECT),
    )(*srcs, *lands, send_sems, recv_sems, after)
    return list(outs[:n]), list(outs[n:])


def _forward_refs(land_refs, rows, a, others, j, received):
    origin = others[j + 3] if received else others[j]
    return land_refs[a].at[:, pl.ds(_index_of(origin) * rows[a], rows[a]), :]


def _forward_start(lands, rows, collective_id, name):
    n = len(lands)

    def body(*refs):
        land_refs, send_sems, recv_sems, token = refs[:n], refs[n], refs[n + 1], refs[-1]
        x, y, c = _peers()
        others = _others(x, y, c)
        barrier = pltpu.get_barrier_semaphore()
        pl.semaphore_signal(barrier, inc=1, device_id=others[0], device_id_type=MESH)
        pl.semaphore_wait(barrier, 1)
        for a in range(n):
            for j in (1, 2, 3):
                blk = _forward_refs(land_refs, rows, a, others, j, False)
                pltpu.make_async_remote_copy(src_ref=blk, dst_ref=blk, send_sem=send_sems.at[3 * a + j - 1],
                                             recv_sem=recv_sems.at[3 * a + j - 1], device_id=others[0],
                                             device_id_type=MESH).start()
        token[...] = jnp.zeros_like(token)

    outs = pl.pallas_call(
        body, name=name,
        out_shape=(pltpu.SemaphoreType.DMA((3 * n,)), pltpu.SemaphoreType.DMA((3 * n,)),
                   *[pltpu.HBM(l.shape, l.dtype) for l in lands], jax.ShapeDtypeStruct((8, 128), F32)),
        in_specs=[HBM_SPEC] * n,
        out_specs=(SEM_SPEC, SEM_SPEC, *[HBM_SPEC] * n, pl.BlockSpec(memory_space=pltpu.VMEM)),
        input_output_aliases={i: 2 + i for i in range(n)},
        compiler_params=pltpu.CompilerParams(has_side_effects=EFFECT, collective_id=collective_id),
    )(*[pltpu.with_memory_space_constraint(l, pltpu.HBM) for l in lands])
    return outs[0], outs[1], list(outs[2:2 + n]), outs[-1]


def _forward_wait(started, rows, after, name):
    send_sems, recv_sems, lands, _ = started
    n = len(lands)

    def body(*refs):
        land_refs, send_sems, recv_sems = refs[:n], refs[n], refs[n + 1]
        x, y, c = _peers()
        others = _others(x, y, c)
        for a in range(n):
            for j in (1, 2, 3):
                copy = pltpu.make_async_remote_copy(
                    src_ref=_forward_refs(land_refs, rows, a, others, j, False),
                    dst_ref=_forward_refs(land_refs, rows, a, others, j, True), send_sem=send_sems.at[3 * a + j - 1],
                    recv_sem=recv_sems.at[3 * a + j - 1], device_id=others[0], device_id_type=MESH)
                copy.wait_send()
                copy.wait_recv()

    outs = pl.pallas_call(
        body, name=name, out_shape=tuple(pltpu.HBM(l.shape, l.dtype) for l in lands),
        in_specs=[HBM_SPEC] * n + [SEM_SPEC, SEM_SPEC, pl.BlockSpec(memory_space=pl.ANY)],
        out_specs=tuple([HBM_SPEC] * n), input_output_aliases={i: i for i in range(n)},
        compiler_params=pltpu.CompilerParams(has_side_effects=EFFECT),
    )(*lands, send_sems, recv_sems, after)
    return list(outs)


def _place_shard(shard, me, name):
    a_, r, c = shard.shape
    tr = _pick(r, (352, 304, 256, 128, 64, 32, 16, 8))
    nr = r // tr

    def body(me_ref, in_ref, out_ref):
        out_ref[...] = in_ref[...]

    return pl.pallas_call(
        body, name=name,
        grid_spec=pltpu.PrefetchScalarGridSpec(
            num_scalar_prefetch=1, grid=(a_, nr),
            in_specs=[pl.BlockSpec((None, tr, c), lambda i, j, me_ref: (i, j, 0))],
            out_specs=pl.BlockSpec((None, tr, c), lambda i, j, me_ref: (i, me_ref[0] * nr + j, 0))),
        out_shape=jax.ShapeDtypeStruct((a_, N_DEV * r, c), shard.dtype),
        compiler_params=_params(("parallel", "parallel")),
    )(me.reshape(1).astype(jnp.int32), shard)


def _sum_slots(land, whole, me, name):
    _, a_, r, c = land.shape
    tr = _pick(r, (352, 256, 128, 64, 32, 16, 8))
    nr = r // tr

    def body(me_ref, land_ref, own_ref, out_ref):
        acc = None
        for s in range(N_DEV):
            part = jnp.where(me_ref[0] == s, own_ref[...], land_ref[s]).astype(F32)
            acc = part if acc is None else acc + part
        out_ref[...] = acc

    return pl.pallas_call(
        body, name=name,
        grid_spec=pltpu.PrefetchScalarGridSpec(
            num_scalar_prefetch=1, grid=(a_, nr),
            in_specs=[pl.BlockSpec((N_DEV, None, tr, c), lambda i, j, me_ref: (0, i, j, 0)),
                      pl.BlockSpec((None, tr, c), lambda i, j, me_ref: (i, me_ref[0] * nr + j, 0))],
            out_specs=pl.BlockSpec((None, tr, c), lambda i, j, me_ref: (i, j, 0))),
        out_shape=jax.ShapeDtypeStruct((a_, r, c), F32),
        compiler_params=_params(("parallel", "parallel")),
    )(me.reshape(1).astype(jnp.int32), land, whole)


def _gather_small(vec, name, dep=None):
    p = vec.shape[1]
    pp = -(-p // 1024) * 1024
    blk = jnp.pad(vec, ((0, 0), (0, pp - p))).reshape(8, pp // 8)
    deps = [] if dep is None else [dep]

    def body(in_ref, *rest):
        out_ref, send_sems, recv_sems = rest[-3:]
        x, y, c = _peers()
        me = 4 * x + 2 * y + c
        others = [(x, y, 1 - c), (1 - x, y, c), (x, 1 - y, c), (1 - x, 1 - y, c),
                  (1 - x, y, 1 - c), (x, 1 - y, 1 - c), (1 - x, 1 - y, 1 - c)]

        def rows(idx):
            return out_ref.at[pl.ds(pl.multiple_of(idx * 8, 8), 8), :]

        out_ref[pl.ds(pl.multiple_of(me * 8, 8), 8), :] = in_ref[...]

        def copy(k, dev, slot):
            return pltpu.make_async_remote_copy(
                src_ref=in_ref, dst_ref=rows(slot), send_sem=send_sems.at[k], recv_sem=recv_sems.at[k],
                device_id=dev, device_id_type=MESH)

        sends = [copy(k, dev, me) for k, dev in enumerate(others)]
        for cp in sends:
            cp.start()
        for k, dev in enumerate(others):
            copy(k, dev, 4 * dev[0] + 2 * dev[1] + dev[2]).wait_recv()
        for cp in sends:
            cp.wait_send()

    vm = pl.BlockSpec(memory_space=pltpu.VMEM)
    out = pl.pallas_call(
        body, name=name, out_shape=jax.ShapeDtypeStruct((8 * N_DEV, pp // 8), F32),
        in_specs=[vm] + [pl.BlockSpec(memory_space=pl.ANY)] * len(deps), out_specs=vm,
        scratch_shapes=[pltpu.SemaphoreType.DMA((7,)), pltpu.SemaphoreType.DMA((7,))],
        compiler_params=pltpu.CompilerParams(has_side_effects=True, vmem_limit_bytes=VMEM_LIMIT),
    )(blk, *deps)
    return out.reshape(N_DEV, pp)[:, :p]


def _sum_rows8(g, name):
    p = g.shape[1]

    def body(in_ref, out_ref):
        acc = in_ref[0:1, :]
        for s in range(1, N_DEV):
            acc = acc + in_ref[s:s + 1, :]
        out_ref[...] = acc

    return pl.pallas_call(body, name=name, out_shape=jax.ShapeDtypeStruct((1, p), F32),
                          compiler_params=_params())(g)


def _sel_row(mods_ref, is_ctx, k):
    return jnp.where(is_ctx, mods_ref[1, k:k + 1, :], mods_ref[0, k:k + 1, :])


def _stream_tile(m, n_x):
    span = n_x if m == n_x else int(np.gcd(n_x, m - n_x))
    return _pick(span, (512, 256, 128, 64, 32, 16, 8))


def _modulate(z, mods, g, ks, kc, n_x, out_dtype, name):
    m, d = z.shape
    tm = _stream_tile(m, n_x)

    def body(z_ref, mods_ref, g_ref, h_ref):
        is_ctx = pl.program_id(0) * tm >= n_x
        zz = z_ref[...]
        r = lax.rsqrt(jnp.mean(zz * zz, axis=-1, keepdims=True) + RMS_EPS)
        shift, scale = _sel_row(mods_ref, is_ctx, ks), _sel_row(mods_ref, is_ctx, kc)
        h_ref[...] = ((zz * r) * g_ref[...] * (1.0 + scale) + shift).astype(out_dtype)

    return pl.pallas_call(
        body, name=name, grid=(m // tm,),
        in_specs=[pl.BlockSpec((tm, d), lambda i: (i, 0)), _full(mods.shape), _full(g.shape)],
        out_specs=pl.BlockSpec((tm, d), lambda i: (i, 0)),
        out_shape=jax.ShapeDtypeStruct((m, d), out_dtype),
        compiler_params=_params(("parallel",)),
    )(z, mods, g)


def _gate_bwd_rows(dx, y, gate, coef):
    return (coef * gate * dx).astype(BF16), jnp.sum(coef * y * dx, axis=0, keepdims=True)


def _modulate_bwd(z, dh, dres, mods, g, kc, n_x, name, latent_only=False, then=None):
    m, d = z.shape
    tm = _stream_tile(m, n_x)
    first_ctx = n_x // tm
    res_blocks = dres.shape[0] // tm
    out_blocks = (n_x if latent_only else m) // tm
    extra = [] if then is None else [then[0], then[1]]

    def body(z_ref, dh_ref, dres_ref, mods_ref, g_ref, *rest):
        i = pl.program_id(0)
        is_ctx = i * tm >= n_x
        dx_ref, acc_ref = rest[len(extra)], rest[len(extra) + 1]

        @pl.when((i == 0) | (i == first_ctx))
        def _():
            acc_ref[...] = jnp.zeros_like(acc_ref)
            if then is not None:
                rest[-1][...] = jnp.zeros_like(rest[-1])

        zz, dhh = z_ref[...], dh_ref[...]
        r = lax.rsqrt(jnp.mean(zz * zz, axis=-1, keepdims=True) + RMS_EPS)
        nz = zz * r
        gain = g_ref[...] * (1.0 + _sel_row(mods_ref, is_ctx, kc))
        dn = dhh * gain
        dz = r * (dn - nz * jnp.mean(dn * nz, axis=-1, keepdims=True))
        dx = jnp.where(i < res_blocks, dres_ref[...], 0.0) + dz

        @pl.when(i < out_blocks)
        def _():
            dx_ref[...] = dx

        acc_ref[0:1, :] += jnp.sum(dhh, axis=0, keepdims=True)
        acc_ref[1:2, :] += jnp.sum(dhh * nz, axis=0, keepdims=True)
        if then is not None:
            y_ref, tmods_ref, dy_ref, gate_acc_ref = rest[0], rest[1], rest[-2], rest[-1]
            dy, part = _gate_bwd_rows(dx, y_ref[...], _sel_row(tmods_ref, is_ctx, then[2]), then[3])
            dy_ref[...] = dy
            gate_acc_ref[0:1, :] += part

    row = pl.BlockSpec((tm, d), lambda i: (i, 0))
    acc_spec = pl.BlockSpec((None, 8, d), lambda i: ((i * tm >= n_x).astype(jnp.int32), 0, 0))
    out_specs = [pl.BlockSpec((tm, d), lambda i: (jnp.minimum(i, out_blocks - 1), 0)), acc_spec]
    out_shape = [jax.ShapeDtypeStruct((out_blocks * tm, d), F32), jax.ShapeDtypeStruct((2, 8, d), F32)]
    in_specs = [row, row, pl.BlockSpec((tm, d), lambda i: (jnp.minimum(i, res_blocks - 1), 0)),
                _full(mods.shape), _full(g.shape)]
    if then is not None:
        in_specs += [row, _full(then[1].shape)]
        out_specs += [row, acc_spec]
        out_shape += [jax.ShapeDtypeStruct((m, d), BF16), jax.ShapeDtypeStruct((2, 8, d), F32)]
    return pl.pallas_call(
        body, name=name, grid=(m // tm,), in_specs=in_specs, out_specs=out_specs, out_shape=out_shape,
        compiler_params=_params(("arbitrary",)),
    )(z, dh, dres, mods, g, *extra)


def _ffn_up(h, wi_t, layer, name):
    m, d = h.shape
    f = wi_t.shape[1] // 2
    tm = _pick(m, ROW_TILES)

    def body(h_ref, w_ref, jac_ref, act_ref):
        hh = h_ref[...]
        a = _dot(hh, w_ref[0:f, :], NT)
        u = _dot(hh, w_ref[f:2 * f, :], NT)
        sg = jax.nn.sigmoid(a)
        s = a * sg
        jac_ref[:, 0:f] = (u * (sg * (1.0 + a * (1.0 - sg)))).astype(BF16)
        jac_ref[:, f:2 * f] = s.astype(BF16)
        act_ref[...] = (s * u).astype(BF16)

    return pl.pallas_call(
        body, name=name, grid=(m // tm,),
        in_specs=[pl.BlockSpec((tm, d), lambda i: (i, 0)),
                  pl.BlockSpec((None, 2 * f, d), lambda i: (layer, 0, 0))],
        out_specs=[pl.BlockSpec((tm, 2 * f), lambda i: (i, 0)), pl.BlockSpec((tm, f), lambda i: (i, 0))],
        out_shape=[jax.ShapeDtypeStruct((m, 2 * f), BF16), jax.ShapeDtypeStruct((m, f), BF16)],
        compiler_params=_params(("parallel",)),
    )(h, wi_t)


def _mm_resid(a, b, layer, res, mods, km, coef, n_x, name, nxt=None):
    m, k = a.shape
    n = b.shape[2]
    tm = _pick(m, (512, 256, 128, 64, 32, 16, 8))
    tn = n if nxt is not None else _pick(n, (1024, 512, 256, 128))
    extra = [] if nxt is None else [nxt[0], nxt[1]]

    def body(a_ref, b_ref, res_ref, mods_ref, *rest):
        is_ctx = pl.program_id(1) * tm >= n_x
        y = _dot(a_ref[...], b_ref[...])
        new = res_ref[...] + coef * _sel_row(mods_ref, is_ctx, km) * y
        if nxt is None:
            out_ref, y_ref = rest
        else:
            nmods_ref, g_ref, out_ref, y_ref, h_ref = rest
            r = lax.rsqrt(jnp.mean(new * new, axis=-1, keepdims=True) + RMS_EPS)
            shift, scale = _sel_row(nmods_ref, is_ctx, nxt[2]), _sel_row(nmods_ref, is_ctx, nxt[3])
            h_ref[...] = ((new * r) * g_ref[...] * (1.0 + scale) + shift).astype(nxt[4])
        y_ref[...] = y.astype(BF16)
        out_ref[...] = new

    tile = pl.BlockSpec((tm, tn), lambda j, i: (i, j))
    outs = [jax.ShapeDtypeStruct((m, n), F32), jax.ShapeDtypeStruct((m, n), BF16)]
    if nxt is not None:
        outs.append(jax.ShapeDtypeStruct((m, n), nxt[4]))
    return pl.pallas_call(
        body, name=name, grid=(n // tn, m // tm),
        in_specs=[pl.BlockSpec((tm, k), lambda j, i: (i, 0)),
                  pl.BlockSpec((None, k, tn), lambda j, i: (layer, 0, j)),
                  tile, pl.BlockSpec((2, 16, tn), lambda j, i: (0, 0, j))] + [_full(e.shape) for e in extra],
        out_specs=[tile] * len(outs), out_shape=outs,
        compiler_params=_params(("parallel", "parallel")),
    )(a, b, res, mods, *extra)


def _ffn_down_bwd(dy, wo, layer, au, name):
    m, d = dy.shape
    f = wo.shape[1]
    tm = _pick(m, ROW_TILES)

    def body(dy_ref, wo_ref, au_ref, dau_ref):
        dact = _dot(dy_ref[...], wo_ref[...], NT)
        dau_ref[:, 0:f] = (dact * au_ref[:, 0:f].astype(F32)).astype(BF16)
        dau_ref[:, f:2 * f] = (dact * au_ref[:, f:2 * f].astype(F32)).astype(BF16)

    wide = pl.BlockSpec((tm, 2 * f), lambda i: (i, 0))
    return pl.pallas_call(
        body, name=name, grid=(m // tm,),
        in_specs=[pl.BlockSpec((tm, d), lambda i: (i, 0)), pl.BlockSpec((None, f, d), lambda i: (layer, 0, 0)), wide],
        out_specs=wide, out_shape=jax.ShapeDtypeStruct((m, 2 * f), BF16),
        compiler_params=_params(("parallel",)),
    )(dy, wo, au)


def _mm(terms, dims, n, out_dtype, name, tm_pref=(512, 256, 128, 64, 32, 16, 8), tn_pref=(512, 256, 128), dep=None):
    m = terms[0][0].shape[0]
    tm = _pick(m, tm_pref)
    tn = _pick(n, tn_pref)
    nt = len(terms)
    deps = [] if dep is None else [dep]

    def body(*refs):
        out_ref = refs[-1]
        acc = None
        for t in range(nt):
            part = _dot(refs[2 * t][...].astype(BF16), refs[2 * t + 1][...].astype(BF16), dims)
            acc = part if acc is None else acc + part
        out_ref[...] = acc.astype(out_dtype)

    in_specs, args = [], []
    for a, b, layer, rb in terms:
        k = a.shape[1]
        in_specs.append(pl.BlockSpec((tm, k), lambda j, i: (i, 0)))
        if dims == NN:
            in_specs.append(pl.BlockSpec((None, k, tn), lambda j, i, layer=layer, rb=rb: (layer, rb, j)))
        else:
            nb = n // tn
            in_specs.append(pl.BlockSpec((None, tn, k), lambda j, i, layer=layer, rb=rb, nb=nb: (layer, rb * nb + j, 0)))
        args += [a, b]
    return pl.pallas_call(
        body, name=name, grid=(n // tn, m // tm), in_specs=in_specs + [pl.BlockSpec(memory_space=pl.ANY)] * len(deps),
        out_specs=pl.BlockSpec((tm, tn), lambda j, i: (i, j)),
        out_shape=jax.ShapeDtypeStruct((m, n), out_dtype),
        compiler_params=_params(("parallel", "parallel")),
    )(*args, *deps)


def _mm_tn(a, b, out_dtype, name, dep=None):
    t = a.shape[0]
    m, n = a.shape[1], b.shape[1]
    tm = _pick(m, (1408, 2432, 1024, 512, 256, 128))
    tn = _pick(n, (1024, 512, 256, 128))
    tk = _pick(t, TALL_TILES)
    deps = [] if dep is None else [dep]

    def body(a_ref, b_ref, *rest):
        out_ref, acc_ref = rest[-2:]
        kk = pl.program_id(2)

        @pl.when(kk == 0)
        def _():
            acc_ref[...] = jnp.zeros_like(acc_ref)

        acc_ref[...] += _dot(a_ref[...].astype(BF16), b_ref[...].astype(BF16), TN)

        @pl.when(kk == pl.num_programs(2) - 1)
        def _():
            out_ref[...] = acc_ref[...].astype(out_dtype)

    return pl.pallas_call(
        body, name=name, grid=(m // tm, n // tn, t // tk),
        in_specs=[pl.BlockSpec((tk, tm), lambda i, j, k: (k, i)), pl.BlockSpec((tk, tn), lambda i, j, k: (k, j))]
        + [pl.BlockSpec(memory_space=pl.ANY)] * len(deps),
        out_specs=pl.BlockSpec((tm, tn), lambda i, j, k: (i, j)),
        out_shape=jax.ShapeDtypeStruct((m, n), out_dtype),
        scratch_shapes=[pltpu.VMEM((tm, tn), F32)],
        compiler_params=_params(("parallel", "parallel", "arbitrary")),
    )(a, b, *deps)


def _stack_rows(a, b, name):
    ta, d = a.shape
    tm = _pick(int(np.gcd(ta, b.shape[0])), (256, 128, 64, 32, 16, 8))
    na, nb = ta // tm, b.shape[0] // tm

    def body(a_ref, b_ref, o_ref):
        o_ref[...] = jnp.where(pl.program_id(0) < na, a_ref[...], b_ref[...])

    return pl.pallas_call(
        body, name=name, grid=(na + nb,),
        in_specs=[pl.BlockSpec((tm, d), lambda i: (jnp.minimum(i, na - 1), 0)),
                  pl.BlockSpec((tm, d), lambda i: (jnp.maximum(i - na, 0), 0))],
        out_specs=pl.BlockSpec((tm, d), lambda i: (i, 0)),
        out_shape=jax.ShapeDtypeStruct((ta + b.shape[0], d), a.dtype),
        compiler_params=_params(("parallel",)),
    )(a, b)


def _assemble_dz(lat_parts, ctx_parts, both_parts, width, name):
    t = next(p.shape[0] for p in lat_parts if p is not None)
    l_ctx = next(p.shape[0] for p in ctx_parts if p is not None)
    tm = _pick(int(np.gcd(t, l_ctx)), (256, 128, 64, 32, 16, 8))
    nt, nl = t // tm, l_ctx // tm
    plan, args, in_specs, off = [], [], [], 0
    lat_spec = lambda w: pl.BlockSpec((tm, w), lambda i: (jnp.minimum(i, nt - 1), 0))
    ctx_spec = lambda w: pl.BlockSpec((tm, w), lambda i: (jnp.maximum(i - nt, 0), 0))
    all_spec = lambda w: pl.BlockSpec((tm, w), lambda i: (i, 0))
    for lat, ctx, both in zip(lat_parts, ctx_parts, both_parts):
        if both:
            w = both[0].shape[1]
            plan.append(("both", off, w, len(args), len(both)))
            args += both
            in_specs += [all_spec(w)] * len(both)
        else:
            w = (lat if lat is not None else ctx).shape[1]
            plan.append(("split", off, w, len(args), (lat is not None, ctx is not None)))
            for part, spec in ((lat, lat_spec), (ctx, ctx_spec)):
                if part is not None:
                    args.append(part)
                    in_specs.append(spec(w))
        off += w
    n_in = len(args)

    def body(*refs):
        out_ref = refs[n_in]
        is_ctx = pl.program_id(0) >= nt
        for kind, o, w, first, info in plan:
            if kind == "both":
                val = refs[first][...]
                for k in range(1, info):
                    val = val + refs[first + k][...]
            else:
                has_lat, has_ctx = info
                zero = jnp.zeros((tm, w), F32)
                lat = refs[first][...] if has_lat else zero
                ctx = refs[first + int(has_lat)][...] if has_ctx else zero
                val = jnp.where(is_ctx, ctx, lat)
            out_ref[:, o:o + w] = val.astype(BF16)
        if off < width:
            out_ref[:, off:width] = jnp.zeros((tm, width - off), BF16)

    return pl.pallas_call(
        body, name=name, grid=(nt + nl,), in_specs=in_specs,
        out_specs=pl.BlockSpec((tm, width), lambda i: (i, 0)),
        out_shape=jax.ShapeDtypeStruct((t + l_ctx, width), BF16),
        compiler_params=_params(("parallel",)),
    )(*args)


def _final_loss(x, g, target, then, name):
    t, d = x.shape
    tm = _stream_tile(t, t)
    y, tmods, km, coef = then

    def body(x_ref, g_ref, t_ref, y_ref, tmods_ref, dx_ref, loss_ref, dg_ref, dy_ref, gate_acc_ref):
        @pl.when(pl.program_id(0) == 0)
        def _():
            loss_ref[...] = jnp.zeros_like(loss_ref)
            dg_ref[...] = jnp.zeros_like(dg_ref)
            gate_acc_ref[...] = jnp.zeros_like(gate_acc_ref)

        xx, gg = x_ref[...], g_ref[...]
        r = lax.rsqrt(jnp.mean(xx * xx, axis=-1, keepdims=True) + RMS_EPS)
        nz = xx * r
        err = nz * gg - t_ref[...]
        loss_ref[...] += jnp.sum(err * err, axis=0, keepdims=True) * (0.5 / d)
        dout = err * (1.0 / d)
        dg_ref[...] += jnp.sum(dout * nz, axis=0, keepdims=True)
        dn = dout * gg
        dx = r * (dn - nz * jnp.mean(dn * nz, axis=-1, keepdims=True))
        dx_ref[...] = dx
        dy, part = _gate_bwd_rows(dx, y_ref[...], tmods_ref[0, km:km + 1, :], coef)
        dy_ref[...] = dy
        gate_acc_ref[0:1, :] += part

    row = pl.BlockSpec((tm, d), lambda i: (i, 0))
    vec = pl.BlockSpec((1, d), lambda i: (0, 0))
    acc = pl.BlockSpec((None, 8, d), lambda i: (0, 0, 0))
    return pl.pallas_call(
        body, name=name, grid=(t // tm,), in_specs=[row, vec, row, row, _full(tmods.shape)],
        out_specs=[row, vec, vec, row, acc],
        out_shape=[jax.ShapeDtypeStruct((t, d), F32), jax.ShapeDtypeStruct((1, d), F32),
                   jax.ShapeDtypeStruct((1, d), F32), jax.ShapeDtypeStruct((t, d), BF16),
                   jax.ShapeDtypeStruct((2, 8, d), F32)],
        compiler_params=_params(("arbitrary",)),
    )(x, g, target, y, tmods)


def _adaln_fwd(craw, w_mod, b_cols, name):
    lyr, d, nc = w_mod.shape

    def body(c_ref, w_ref, b_ref, out_ref):
        out_ref[...] = _bdot(_silu(c_ref[...]), w_ref[...]) + b_ref[...]

    return pl.pallas_call(
        body, name=name, grid=(lyr,),
        in_specs=[_full(craw.shape), pl.BlockSpec((None, d, nc), lambda l: (l, 0, 0)),
                  pl.BlockSpec((None, 1, nc), lambda l: (l, 0, 0))],
        out_specs=pl.BlockSpec((None, 16, nc), lambda l: (l, 0, 0)),
        out_shape=jax.ShapeDtypeStruct((lyr, 16, nc), F32),
        compiler_params=_params(("parallel",)),
    )(craw, w_mod, b_cols)


def _adaln_bwd(craw, cs_t, dmm_cols, w_mod, name):
    lyr, d, nc = w_mod.shape

    def body(c_ref, cst_ref, dmm_ref, w_ref, gw_ref, dc_ref):
        dmm = dmm_ref[...]
        gw_ref[...] = _bdot(cst_ref[...], dmm)
        cc = c_ref[...]
        sg = jax.nn.sigmoid(cc)
        dc_ref[...] = _bdot(dmm, w_ref[...], NT) * (sg * (1.0 + cc * (1.0 - sg)))

    wspec = pl.BlockSpec((None, d, nc), lambda l: (l, 0, 0))
    return pl.pallas_call(
        body, name=name, grid=(lyr,),
        in_specs=[_full(craw.shape), _full(cs_t.shape), pl.BlockSpec((None, 16, nc), lambda l: (l, 0, 0)), wspec],
        out_specs=[wspec, pl.BlockSpec((None, 16, d), lambda l: (l, 0, 0))],
        out_shape=[jax.ShapeDtypeStruct((lyr, d, nc), F32), jax.ShapeDtypeStruct((lyr, 16, d), F32)],
        compiler_params=_params(("parallel",)),
    )(craw, cs_t, dmm_cols, w_mod)


def _rope_tables(t):
    rows = np.repeat(np.arange(t // GRID_W, dtype=np.float32), GRID_W)
    cols = np.tile(np.arange(GRID_W, dtype=np.float32), t // GRID_W)
    n = A_HEAD_DIM // 4
    freqs = (ROPE_BASE ** (-np.arange(n, dtype=np.float32) / n)).astype(np.float32)
    ang_r, ang_c = (rows[:, None] * freqs).astype(np.float32), (cols[:, None] * freqs).astype(np.float32)
    cr, sr, cc, sc = np.cos(ang_r), np.sin(ang_r), np.cos(ang_c), np.sin(ang_c)
    cos = np.concatenate([cr, cr, cc, cc] * 2, axis=-1).astype(np.float32)
    sin = np.concatenate([-sr, sr, -sc, sc] * 2, axis=-1).astype(np.float32)
    return jnp.asarray(cos), jnp.asarray(sin)


def _rope(xt, cos, sin, adjoint, name, view=None):
    t = cos.shape[0]
    w, col = (xt.shape[1], 0) if view is None else view
    tb = _pick(t, (1024, 512, 256, 128))
    rep = w // cos.shape[1]

    def body(x_ref, c_ref, s_ref, o_ref):
        xx = x_ref[...]
        cc = jnp.concatenate([c_ref[...]] * rep, axis=1) if rep > 1 else c_ref[...]
        ss = jnp.concatenate([s_ref[...]] * rep, axis=1) if rep > 1 else s_ref[...]
        low = (lax.broadcasted_iota(jnp.int32, xx.shape, 1) % 32) < 16

        def partner(v):
            return jnp.where(low, pltpu.roll(v, w - 16, 1), pltpu.roll(v, 16, 1))

        if adjoint:
            o_ref[...] = xx * cc + partner(xx * ss)
        else:
            o_ref[...] = xx * cc + partner(xx) * ss

    blk = pl.BlockSpec((tb, w), lambda i: (i, 0))
    tab = pl.BlockSpec((tb, cos.shape[1]), lambda i: (i, 0))
    return pl.pallas_call(
        body, name=name, grid=(t // tb,), in_specs=[pl.BlockSpec((tb, w), lambda i: (i, col)), tab, tab],
        out_specs=blk, out_shape=jax.ShapeDtypeStruct((t, w), F32), compiler_params=_params(("parallel",)),
    )(xt, cos, sin)


def _attn_bias():
    i = (np.arange(A_REP * WINDOW) % WINDOW)[:, None]
    j = np.arange(3 * WINDOW)[None, :]
    near = np.abs(j - WINDOW - i) <= WINDOW
    variants = [near, near & (j >= WINDOW), near & (j < 2 * WINDOW), near & (j >= WINDOW) & (j < 2 * WINDOW)]
    return jnp.asarray(np.where(np.stack(variants), 0.0, -np.inf).astype(np.float32))


def _attn_bias_spec(nb):
    rows = A_REP * WINDOW
    return pl.BlockSpec((None, rows, 3 * WINDOW),
                        lambda g, n: ((n == 0).astype(jnp.int32) + 2 * (n == nb - 1).astype(jnp.int32), 0, 0))


def _attn_probs(q, kb, kc, sink, bias):
    scale = A_HEAD_DIM ** -0.5
    s1 = _bdot(q, kb, NT) * scale + bias
    s2 = _bdot(q, kc, NT) * scale
    mx = jnp.maximum(jnp.maximum(jnp.max(s1, axis=-1, keepdims=True), jnp.max(s2, axis=-1, keepdims=True)), sink)
    p1, p2, ps = jnp.exp(s1 - mx), jnp.exp(s2 - mx), jnp.exp(sink - mx)
    inv = 1.0 / (jnp.sum(p1, axis=-1, keepdims=True) + jnp.sum(p2, axis=-1, keepdims=True) + ps)
    return p1 * inv, p2 * inv, ps * inv


def _sink_rows(sink_ref):
    return jnp.concatenate([jnp.broadcast_to(sink_ref[r], (WINDOW, 1)) for r in range(A_REP)], axis=0)


def _attn_fwd(q, kp, vp, kc, vc, sink, name):
    hq, t, dh = q.shape
    nb = t // WINDOW
    lc = kc.shape[1]
    rows = A_REP * WINDOW

    def body(q_ref, k_ref, v_ref, kc_ref, vc_ref, sink_ref, bias_ref, o_ref):
        n = pl.program_id(1)
        start = pl.multiple_of(n * WINDOW, WINDOW)
        kb, vb = k_ref[pl.ds(start, 3 * WINDOW), :], v_ref[pl.ds(start, 3 * WINDOW), :]
        p1, p2, _ = _attn_probs(q_ref[...].reshape(rows, dh), kb, kc_ref[...], _sink_rows(sink_ref), bias_ref[...])
        o_ref[...] = (_bdot(p1, vb) + _bdot(p2, vc_ref[...])).reshape(A_REP, WINDOW, dh)

    qblk = pl.BlockSpec((A_REP, WINDOW, dh), lambda g, n: (g, n, 0))
    kfull = pl.BlockSpec((None, t + 2 * WINDOW, dh), lambda g, n: (g, 0, 0))
    cfull = pl.BlockSpec((None, lc, dh), lambda g, n: (g, 0, 0))
    return pl.pallas_call(
        body, name=name, grid=(hq // A_REP, nb),
        in_specs=[qblk, kfull, kfull, cfull, cfull, pl.BlockSpec((A_REP, 1, 1), lambda g, n: (g, 0, 0)),
                  _attn_bias_spec(nb)],
        out_specs=qblk, out_shape=jax.ShapeDtypeStruct((hq, t, dh), F32),
        compiler_params=_params(("parallel", "parallel")),
    )(q, kp, vp, kc, vc, sink, _attn_bias())


def _attn_bwd(q, kp, vp, kc, vc, sink, o, do, name):
    hq, t, dh = q.shape
    nb = t // WINDOW
    lc = kc.shape[1]
    scale = A_HEAD_DIM ** -0.5
    rows = A_REP * WINDOW

    def body(q_ref, k_ref, v_ref, kc_ref, vc_ref, sink_ref, o_ref, do_ref, bias_ref,
             dq_ref, dk_ref, dv_ref, dkc_ref, dvc_ref, dsink_ref):
        n = pl.program_id(1)

        @pl.when(n == 0)
        def _():
            dk_ref[...] = jnp.zeros_like(dk_ref)
            dv_ref[...] = jnp.zeros_like(dv_ref)
            dkc_ref[...] = jnp.zeros_like(dkc_ref)
            dvc_ref[...] = jnp.zeros_like(dvc_ref)
            dsink_ref[...] = jnp.zeros_like(dsink_ref)

        start = pl.multiple_of(n * WINDOW, WINDOW)
        band = pl.ds(start, 3 * WINDOW)
        qq, kb, vb, kcc, vcc = q_ref[...].reshape(rows, dh), k_ref[band, :], v_ref[band, :], kc_ref[...], vc_ref[...]
        p1, p2, ps = _attn_probs(qq, kb, kcc, _sink_rows(sink_ref), bias_ref[...])
        dout = do_ref[...].reshape(rows, dh)
        delta = jnp.sum(dout * o_ref[...].reshape(rows, dh), axis=-1, keepdims=True)
        ds1 = p1 * (_bdot(dout, vb, NT) - delta)
        ds2 = p2 * (_bdot(dout, vcc, NT) - delta)
        dq_ref[...] = ((_bdot(ds1, kb) + _bdot(ds2, kcc)) * scale).reshape(A_REP, WINDOW, dh)
        dk_ref[band, :] += _bdot(ds1.T, qq) * scale
        dv_ref[band, :] += _bdot(p1.T, dout)
        dkc_ref[...] += _bdot(ds2.T, qq) * scale
        dvc_ref[...] += _bdot(p2.T, dout)
        dsink_ref[...] += jnp.sum((-ps * delta).reshape(A_REP, WINDOW, 1), axis=1, keepdims=True)

    qblk = pl.BlockSpec((A_REP, WINDOW, dh), lambda g, n: (g, n, 0))
    kfull = pl.BlockSpec((None, t + 2 * WINDOW, dh), lambda g, n: (g, 0, 0))
    cfull = pl.BlockSpec((None, lc, dh), lambda g, n: (g, 0, 0))
    return pl.pallas_call(
        body, name=name, grid=(hq // A_REP, nb),
        in_specs=[qblk, kfull, kfull, cfull, cfull, pl.BlockSpec((A_REP, 1, 1), lambda g, n: (g, 0, 0)), qblk, qblk,
                  _attn_bias_spec(nb)],
        out_specs=[qblk, kfull, kfull, cfull, cfull, pl.BlockSpec((A_REP, 8, 128), lambda g, n: (g, 0, 0))],
        out_shape=[jax.ShapeDtypeStruct(q.shape, F32), jax.ShapeDtypeStruct(kp.shape, F32),
                   jax.ShapeDtypeStruct(kp.shape, F32), jax.ShapeDtypeStruct(kc.shape, F32),
                   jax.ShapeDtypeStruct(kc.shape, F32), jax.ShapeDtypeStruct((hq, 8, 128), F32)],
        compiler_params=_params(("parallel", "arbitrary")),
    )(q, kp, vp, kc, vc, sink, o, do, _attn_bias())


def _gate_fwd(zg, w2, b2, name):
    m = zg.shape[0]
    n = w2.shape[1]
    tm = _pick(m, TALL_TILES)

    def body(z_ref, w_ref, b_ref, o_ref):
        o_ref[...] = jax.nn.log_sigmoid(_bdot(z_ref[...], w_ref[...]) + b_ref[...]) / B_GATE_NORM

    return pl.pallas_call(
        body, name=name, grid=(m // tm,),
        in_specs=[pl.BlockSpec((tm, zg.shape[1]), lambda i: (i, 0)), _full(w2.shape), _full(b2.shape)],
        out_specs=pl.BlockSpec((tm, n), lambda i: (i, 0)), out_shape=jax.ShapeDtypeStruct((m, n), F32),
        compiler_params=_params(("parallel",)),
    )(zg, w2, b2)


def _gate_bwd(zg, w2, b2, dla, name):
    m, rk = zg.shape
    n = w2.shape[1]
    tm = _pick(m, TALL_TILES)

    def body(z_ref, w_ref, b_ref, d_ref, dz_ref, dw_ref, db_ref):
        @pl.when(pl.program_id(0) == 0)
        def _():
            dw_ref[...] = jnp.zeros_like(dw_ref)
            db_ref[...] = jnp.zeros_like(db_ref)

        zz, ww = z_ref[...], w_ref[...]
        pre = _bdot(zz, ww) + b_ref[...]
        dpre = d_ref[...] * (1.0 / B_GATE_NORM) * jax.nn.sigmoid(-pre)
        dz_ref[...] = _bdot(dpre, ww, NT)
        dw_ref[...] += _bdot(zz.T, dpre)
        db_ref[...] += jnp.sum(dpre, axis=0, keepdims=True)

    return pl.pallas_call(
        body, name=name, grid=(m // tm,),
        in_specs=[pl.BlockSpec((tm, rk), lambda i: (i, 0)), _full(w2.shape), _full(b2.shape),
                  pl.BlockSpec((tm, n), lambda i: (i, 0))],
        out_specs=[pl.BlockSpec((tm, rk), lambda i: (i, 0)), _full(w2.shape), _full(b2.shape)],
        out_shape=[jax.ShapeDtypeStruct((m, rk), F32), jax.ShapeDtypeStruct(w2.shape, F32),
                   jax.ShapeDtypeStruct(b2.shape, F32)],
        compiler_params=_params(("arbitrary",)),
    )(zg, w2, b2, dla)


def _chunk_order(step, n_x_chunks, n_chunks, reverse):
    n_c = n_chunks - n_x_chunks
    if reverse:
        return jnp.where(step < n_c, n_chunks - 1 - step, n_chunks - 1 - step)
    return jnp.where(step < n_c, n_x_chunks + step, step - n_c)


def _tri(reverse, transpose=False):
    i = lax.broadcasted_iota(jnp.int32, (B_CHUNK, B_CHUNK), 0)
    j = lax.broadcasted_iota(jnp.int32, (B_CHUNK, B_CHUNK), 1)
    if transpose:
        i, j = j, i
    return (j >= i) if reverse else (j <= i)


def _gla_chunk(q, k, la, reverse):
    g = _dot_01(_tri(reverse), la)
    last = 0 if reverse else B_CHUNK - 1
    gl = g[last:last + 1, :]
    eg, eng, egl = jnp.exp(g), jnp.exp(-g), jnp.exp(gl - g)
    decay_col = jnp.exp(jnp.sum(la.T, axis=1, keepdims=True))
    return q * (B_DK ** -0.5) * eg, k * eng, k * egl, eg, eng, egl, decay_col


def _head_of(shape, axis, width):
    return lax.broadcasted_iota(jnp.int32, shape, axis) // width


def _gla_chunks_per_step(n_chunks, n_x_chunks):
    return _pick(int(np.gcd(n_chunks - n_x_chunks, n_x_chunks)), (4, 2, 1))


def _gla_fwd(q, k, v, la_f, la_b, n_x, name, qk_cols=(0, 0)):
    tc, wk = la_f.shape
    wv = v.shape[1]
    hh = B_HEADS
    dk, dv = wk // hh, wv // hh
    nc, nxc = tc // B_CHUNK, n_x // B_CHUNK
    sub = _gla_chunks_per_step(nc, nxc)
    rows_per_step = sub * B_CHUNK
    orders = [functools.partial(_chunk_order, n_x_chunks=nxc // sub, n_chunks=nc // sub, reverse=rev)
              for rev in (False, True)]

    def body(*refs):
        ins, outs, s_refs = refs[:8], refs[8:12], refs[12:]

        @pl.when(pl.program_id(0) == 0)
        def _():
            for s_ref in s_refs:
                s_ref[...] = jnp.zeros_like(s_ref)

        lane_head = _head_of((B_CHUNK, wk), 1, dk)
        row_head = _head_of((wk, dv), 0, dk)
        for di, reverse in enumerate((False, True)):
            q_ref, k_ref, v_ref, la_ref = ins[4 * di:4 * di + 4]
            o_ref, s_save_ref = outs[2 * di:2 * di + 2]
            s_prev = s_refs[di][...]
            for c in (reversed(range(sub)) if reverse else range(sub)):
                rows = slice(c * B_CHUNK, (c + 1) * B_CHUNK)
                qt, kt, ke, _, _, _, decay_col = _gla_chunk(q_ref[rows, :], k_ref[rows, :], la_ref[rows, :], reverse)
                ke_t = ke.T
                update = jnp.zeros_like(s_prev)
                for h in range(hh):
                    vv = v_ref[rows, h * dv:(h + 1) * dv]
                    qm = jnp.where(lane_head == h, qt, 0.0)
                    att = jnp.where(_tri(reverse), _bdot(qm, kt, NT), 0.0)
                    o_ref[rows, h * dv:(h + 1) * dv] = _bdot(att, vv) + _bdot(qm, s_prev)
                    update = jnp.where(row_head == h, _bdot(ke_t, vv), update)
                s_save_ref[c] = s_prev
                s_prev = decay_col * s_prev + update
            s_refs[di][...] = s_prev

    def blk(w, order, col=0):
        return pl.BlockSpec((rows_per_step, w), lambda s: (order(s), col))

    def sblk(order):
        return pl.BlockSpec((sub, wk, dv), lambda s: (order(s), 0, 0))

    in_specs, out_specs = [], []
    for order in orders:
        in_specs += [blk(wk, order, qk_cols[0]), blk(wk, order, qk_cols[1]), blk(wv, order), blk(wk, order)]
        out_specs += [blk(wv, order), sblk(order)]
    o_shape, s_shape = jax.ShapeDtypeStruct((tc, wv), F32), jax.ShapeDtypeStruct((nc, wk, dv), F32)
    return pl.pallas_call(
        body, name=name, grid=(nc // sub,), in_specs=in_specs, out_specs=out_specs,
        out_shape=[o_shape, s_shape, o_shape, s_shape],
        scratch_shapes=[pltpu.VMEM((wk, dv), F32)] * 2,
        compiler_params=_params(("arbitrary",)),
    )(q, k, v, la_f, q, k, v, la_b)


def _gla_bwd(q, k, v, la_f, la_b, s_f, s_b, do, n_x, name, qk_cols=(0, 0)):
    tc, wk = la_f.shape
    wv = v.shape[1]
    hh = B_HEADS
    dk, dv = wk // hh, wv // hh
    nc, nxc = tc // B_CHUNK, n_x // B_CHUNK
    sub = _gla_chunks_per_step(nc, nxc)
    rows_per_step = sub * B_CHUNK
    nb, nxb = nc // sub, nxc // sub
    orders = [functools.partial(lambda s, rev: _chunk_order(nb - 1 - s, nxb, nb, rev), rev=rev) for rev in (False, True)]

    def body(*refs):
        ins, outs, ds_refs = refs[:12], refs[12:20], refs[20:]

        @pl.when(pl.program_id(0) == 0)
        def _():
            for ds_ref in ds_refs:
                ds_ref[...] = jnp.zeros_like(ds_ref)

        lane_head = _head_of((B_CHUNK, wk), 1, dk)
        row_head = _head_of((wk, dv), 0, dk)
        for di, reverse in enumerate((False, True)):
            q_ref, k_ref, v_ref, la_ref, s_save_ref, do_ref = ins[6 * di:6 * di + 6]
            dq_ref, dk_ref, dv_ref, dla_ref = outs[4 * di:4 * di + 4]
            mask = _tri(reverse)
            last = 0 if reverse else B_CHUNK - 1
            is_last = lax.broadcasted_iota(jnp.int32, (B_CHUNK, wk), 0) == last
            ds_new = ds_refs[di][...]
            for c in (range(sub) if reverse else reversed(range(sub))):
                rows = slice(c * B_CHUNK, (c + 1) * B_CHUNK)
                la = la_ref[rows, :]
                qt, kt, ke, eg, eng, egl, decay_col = _gla_chunk(q_ref[rows, :], k_ref[rows, :], la, reverse)
                qt_t = qt.T
                s_prev = s_save_ref[c]
                dqt, dkt, dke = jnp.zeros_like(qt), jnp.zeros_like(qt), jnp.zeros_like(qt)
                ds_add = jnp.zeros_like(ds_new)
                for h in range(hh):
                    cols = slice(h * dv, (h + 1) * dv)
                    vv, dout = v_ref[rows, cols], do_ref[rows, cols]
                    mine = lane_head == h
                    qm, km = jnp.where(mine, qt, 0.0), jnp.where(mine, ke, 0.0)
                    att = jnp.where(mask, _bdot(qm, kt, NT), 0.0)
                    datt = jnp.where(mask, _bdot(dout, vv, NT), 0.0)
                    dv_ref[rows, cols] = _bdot(att.T, dout) + _bdot(km, ds_new)
                    dqt = jnp.where(mine, _bdot(datt, kt) + _bdot(dout, s_prev, NT), dqt)
                    dkt = jnp.where(mine, _bdot(datt.T, qt), dkt)
                    dke = jnp.where(mine, _bdot(vv, ds_new, NT), dke)
                    ds_add = jnp.where(row_head == h, _bdot(qt_t, dout), ds_add)
                ddecay_row = jnp.sum((ds_new * s_prev).T, axis=0, keepdims=True)
                decay_row = jnp.exp(jnp.sum(la, axis=0, keepdims=True))
                dq_ref[rows, :] = dqt * (B_DK ** -0.5) * eg
                dk_ref[rows, :] = dkt * eng + dke * egl
                dgl = jnp.sum(dke * ke, axis=0, keepdims=True) + ddecay_row * decay_row
                dg = dqt * qt - dkt * kt - dke * ke + jnp.where(is_last, dgl, 0.0)
                dla_ref[rows, :] = _dot_01(_tri(reverse, transpose=True), dg)
                ds_new = decay_col * ds_new + ds_add
            ds_refs[di][...] = ds_new

    def blk(w, order, col=0):
        return pl.BlockSpec((rows_per_step, w), lambda s: (order(s), col))

    in_specs, out_specs = [], []
    for order in orders:
        in_specs += [blk(wk, order, qk_cols[0]), blk(wk, order, qk_cols[1]), blk(wv, order), blk(wk, order),
                     pl.BlockSpec((sub, wk, dv), lambda s, order=order: (order(s), 0, 0)), blk(wv, order)]
        out_specs += [blk(wk, order), blk(wk, order), blk(wv, order), blk(wk, order)]
    k_shape, v_shape = jax.ShapeDtypeStruct((tc, wk), F32), jax.ShapeDtypeStruct((tc, wv), F32)
    return pl.pallas_call(
        body, name=name, grid=(nb,), in_specs=in_specs, out_specs=out_specs,
        out_shape=[k_shape, k_shape, v_shape, k_shape] * 2,
        scratch_shapes=[pltpu.VMEM((wk, dv), F32)] * 2,
        compiler_params=_params(("arbitrary",)),
    )(q, k, v, la_f, s_f, do, q, k, v, la_b, s_b, do)


def _gla_out_fwd(o_f, o_b, r, g, name):
    t = r.shape[0]
    dv = g.shape[1]
    hh = r.shape[1] // dv
    tb = _pick(t, (512, 256, 128, 64))

    def body(of_ref, ob_ref, r_ref, g_ref, out_ref):
        for h in range(hh):
            cols = slice(h * dv, (h + 1) * dv)
            o = of_ref[:, cols] + ob_ref[:, cols]
            rs = lax.rsqrt(jnp.mean(o * o, axis=-1, keepdims=True) + RMS_EPS)
            out_ref[:, cols] = (o * rs) * g_ref[...] * _silu(r_ref[:, cols])

    rblk = pl.BlockSpec((tb, hh * dv), lambda i: (i, 0))
    return pl.pallas_call(
        body, name=name, grid=(t // tb,), in_specs=[rblk, rblk, rblk, _full(g.shape)], out_specs=rblk,
        out_shape=jax.ShapeDtypeStruct((t, hh * dv), F32), compiler_params=_params(("parallel",)),
    )(o_f, o_b, r, g)


def _gla_out_bwd(o_f, o_b, r, g, dout, name):
    tc = o_f.shape[0]
    t = r.shape[0]
    dv = g.shape[1]
    hh = r.shape[1] // dv
    tb = _pick(int(np.gcd(t, tc)), (256, 128, 64))
    nt = t // tb

    def body(of_ref, ob_ref, r_ref, g_ref, d_ref, do_ref, dr_ref, dg_ref):
        i = pl.program_id(0)

        @pl.when(i == 0)
        def _():
            dg_ref[...] = jnp.zeros_like(dg_ref)

        @pl.when(i >= nt)
        def _():
            do_ref[...] = jnp.zeros_like(do_ref)

        @pl.when(i < nt)
        def _():
            gg = g_ref[...]
            for h in range(hh):
                cols = slice(h * dv, (h + 1) * dv)
                o = of_ref[:, cols] + ob_ref[:, cols]
                rs = lax.rsqrt(jnp.mean(o * o, axis=-1, keepdims=True) + RMS_EPS)
                nz = o * rs
                rr, dd = r_ref[:, cols], d_ref[:, cols]
                sg = jax.nn.sigmoid(rr)
                dr_ref[:, cols] = dd * nz * gg * (sg * (1.0 + rr * (1.0 - sg)))
                dy = dd * (rr * sg)
                dg_ref[...] += jnp.sum(dy * nz, axis=0, keepdims=True)
                dn = dy * gg
                do_ref[:, cols] = rs * (dn - nz * jnp.mean(dn * nz, axis=-1, keepdims=True))

    oblk = pl.BlockSpec((tb, hh * dv), lambda i: (i, 0))
    rblk = pl.BlockSpec((tb, hh * dv), lambda i: (jnp.minimum(i, nt - 1), 0))
    return pl.pallas_call(
        body, name=name, grid=(tc // tb,), in_specs=[oblk, oblk, rblk, _full(g.shape), rblk],
        out_specs=[oblk, rblk, _full(g.shape)],
        out_shape=[jax.ShapeDtypeStruct(o_f.shape, F32), jax.ShapeDtypeStruct(r.shape, F32),
                   jax.ShapeDtypeStruct(g.shape, F32)],
        compiler_params=_params(("arbitrary",)),
    )(o_f, o_b, r, g, dout)


def _pool_tile(t):
    return _pick(t, tuple(p for p in (512, 256, 128, 64) if p + 2 * POOL_PAD <= t))


def _pool_window(i, tb, t):
    return pl.multiple_of(jnp.clip(i * tb - POOL_PAD, 0, t - (tb + 2 * POOL_PAD)), 8)


def _pool_band(half, i, tb, start, adjoint):
    pos = i * tb + lax.broadcasted_iota(jnp.int32, (tb, tb + 2 * POOL_PAD), 0)
    tok = start + lax.broadcasted_iota(jnp.int32, (tb, tb + 2 * POOL_PAD), 1)
    if adjoint:
        return (tok > pos - half) & (tok <= pos + half)
    return (tok >= pos - half) & (tok < pos + half)


def _pool_count(pos, half, t):
    return (jnp.minimum(pos + half, t) - jnp.maximum(pos - half, 0)).astype(F32)


def _pool_fwd(h, w_pool, pool_scale, res, mods, km, name):
    t, d = res.shape
    ng, gw = w_pool.shape[0], w_pool.shape[1]
    tb = _pool_tile(t)

    def body(h_ref, w_ref, ps_ref, res_ref, mods_ref, out_ref, pooled_ref, ypre_ref):
        gi, i = pl.program_id(0), pl.program_id(1)
        half = jnp.left_shift(1, gi)
        start = _pool_window(i, tb, t)
        win = h_ref[pl.ds(start, tb + 2 * POOL_PAD), :]
        total = _dot_01(_pool_band(half, i, tb, start, False), win)
        pos = i * tb + lax.broadcasted_iota(jnp.int32, (tb, 1), 0)
        pooled = total / _pool_count(pos, half, t) - h_ref[pl.ds(pl.multiple_of(i * tb, tb), tb), :]
        ypre = _bdot(pooled, w_ref[...])
        pooled_ref[...] = pooled.astype(BF16)
        ypre_ref[...] = ypre
        out_ref[...] = res_ref[...] + mods_ref[0, km:km + 1, :] * (ypre * ps_ref[...])

    tile = pl.BlockSpec((tb, gw), lambda gi, i: (i, gi))
    return pl.pallas_call(
        body, name=name, grid=(ng, t // tb),
        in_specs=[pl.BlockSpec((t, gw), lambda gi, i: (0, gi)),
                  pl.BlockSpec((None, gw, gw), lambda gi, i: (gi, 0, 0)),
                  pl.BlockSpec((1, gw), lambda gi, i: (0, gi)), tile,
                  pl.BlockSpec((2, 16, gw), lambda gi, i: (0, 0, gi))],
        out_specs=[tile, tile, tile],
        out_shape=[jax.ShapeDtypeStruct((t, d), F32), jax.ShapeDtypeStruct((t, d), BF16),
                   jax.ShapeDtypeStruct((t, d), F32)],
        compiler_params=_params(("parallel", "parallel")),
    )(h, w_pool, pool_scale, res, mods)


def _pool_bwd(dxp, w_pool, pool_scale, pooled, ypre, mods, km, name):
    t, d = pooled.shape
    ng, gw = w_pool.shape[0], w_pool.shape[1]
    tb = _pool_tile(t)

    def body(dxp_ref, w_ref, ps_ref, pooled_ref, ypre_ref, mods_ref, dh_ref, dw_ref, acc_ref):
        gi, i = pl.program_id(0), pl.program_id(1)

        @pl.when(i == 0)
        def _():
            dw_ref[...] = jnp.zeros_like(dw_ref)
            acc_ref[...] = jnp.zeros_like(acc_ref)

        half = jnp.left_shift(1, gi)
        mod, ps = mods_ref[0, km:km + 1, :], ps_ref[...]
        start = _pool_window(i, tb, t)
        dwin = dxp_ref[pl.ds(start, tb + 2 * POOL_PAD), :]
        dpooled = _bdot(dwin * (mod * ps), w_ref[...], NT)
        pos = start + lax.broadcasted_iota(jnp.int32, (tb + 2 * POOL_PAD, 1), 0)
        spread = _dot_01(_pool_band(half, i, tb, start, True), dpooled / _pool_count(pos, half, t))
        dxc, yp = dxp_ref[pl.ds(pl.multiple_of(i * tb, tb), tb), :], ypre_ref[...]
        dh_ref[...] = spread - _bdot(dxc * (mod * ps), w_ref[...], NT)
        dw_ref[...] += _bdot(pooled_ref[...].astype(F32).T, dxc * (mod * ps))
        acc_ref[0:1, :] += jnp.sum(dxc * yp * mod, axis=0, keepdims=True)
        acc_ref[1:2, :] += jnp.sum(dxc * yp * ps, axis=0, keepdims=True)

    tile = pl.BlockSpec((tb, gw), lambda gi, i: (i, gi))
    wblk = pl.BlockSpec((None, gw, gw), lambda gi, i: (gi, 0, 0))
    return pl.pallas_call(
        body, name=name, grid=(ng, t // tb),
        in_specs=[pl.BlockSpec((t, gw), lambda gi, i: (0, gi)), wblk,
                  pl.BlockSpec((1, gw), lambda gi, i: (0, gi)), tile, tile,
                  pl.BlockSpec((2, 16, gw), lambda gi, i: (0, 0, gi))],
        out_specs=[tile, wblk, pl.BlockSpec((8, gw), lambda gi, i: (0, gi))],
        out_shape=[jax.ShapeDtypeStruct((t, d), F32), jax.ShapeDtypeStruct(w_pool.shape, F32),
                   jax.ShapeDtypeStruct((8, d), F32)],
        compiler_params=_params(("arbitrary", "arbitrary")),
    )(dxp, w_pool, pool_scale, pooled, ypre, mods)


def _adamw(w, g, m, v, name):
    r, c = w.shape
    tr = _pick(r, (512, 352, 256, 128, 64, 32, 16, 8))
    c1 = 1.0 / (1.0 - ADAM_B1 ** ADAM_STEP)
    c2 = 1.0 / (1.0 - ADAM_B2 ** ADAM_STEP)

    def body(w_ref, g_ref, m_ref, v_ref, d_ref, nm_ref, nv_ref):
        gg = g_ref[...]
        nm = ADAM_B1 * m_ref[...] + (1.0 - ADAM_B1) * gg
        nv = ADAM_B2 * v_ref[...] + (1.0 - ADAM_B2) * (gg * gg)
        nm_ref[...] = nm
        nv_ref[...] = nv
        d_ref[...] = -ADAM_LR * ((nm * c1) / (jnp.sqrt(nv * c2) + ADAM_EPS) + ADAM_WD * w_ref[...])

    blk = pl.BlockSpec((tr, c), lambda i: (i, 0))
    shp = jax.ShapeDtypeStruct((r, c), F32)
    return pl.pallas_call(
        body, name=name, grid=(r // tr,), in_specs=[blk] * 4, out_specs=[blk] * 3, out_shape=[shp] * 3,
        compiler_params=_params(("parallel",)),
    )(w, g, m, v)


def _heads(z, n_heads):
    m = z.shape[0]
    return z.reshape(m, n_heads, -1).transpose(1, 0, 2)


def _unheads(zh):
    return zh.transpose(1, 0, 2).reshape(zh.shape[1], -1)


def _pad_rows(a, n):
    return jnp.pad(a, ((0, 0), (n, n), (0, 0))) if a.ndim == 3 else jnp.pad(a, ((n, n), (0, 0)))


def _local_step(x, ctx, target, mods, wts, fetch, emit):
    t, d = x.shape
    l_ctx = ctx.shape[0]
    tc = t + l_ctx
    norm_g = wts["norm_g"]
    ng = lambda l, k: norm_g[l, k][None, :]
    grads = {}
    dmods = [[[None] * N_MOD for _ in range(2)] for _ in range(2)]
    dnorm = [[None] * 3 for _ in range(2)]

    def ffn_fwd(z, h, l, kbase, wi, wo, n_x, tag, nxt):
        au, act = _ffn_up(h, wi, 0, f"ffn_up_{tag}")
        wo = wo(act) if callable(wo) else wo
        outs = _mm_resid(act, wo, 0, z, mods[l], kbase + 2, 0.5, n_x, f"ffn_down_{tag}", nxt=nxt)
        return outs[0], (z, h, au, act, outs[1], wi, wo), (outs[2] if nxt is not None else None)

    def ffn_bwd(dz_new, dy, saved, l, kbase, g, n_x, tag, stage, split=False, then=None):
        z, h, au, act, y, wi, wo = saved
        dau = _ffn_down_bwd(dy, wo, 0, au, f"ffn_down_bwd_{tag}")
        dwo = _mm_tn(act, dy, BF16, f"dwo_{tag}")
        if split:
            token = emit(stage, [dwo])
            dwi_t = _mm_tn(dau, h, BF16, f"dwi_{tag}", dep=token)
            token = emit(stage + 1, [dwi_t])
        else:
            dwi_t = _mm_tn(dau, h, BF16, f"dwi_{tag}")
            token = emit(stage, [dwi_t, dwo])
        dh = _mm([(dau, wi, 0, 0)], NN, d, F32, f"dh_{tag}", tm_pref=TALL_TILES, dep=token)
        return _modulate_bwd(z, dh, dz_new, mods[l], g, kbase + 1, n_x, f"mod_bwd_{tag}", latent_only=split, then=then)

    def record(l, kbase, k_norm, g, acc_mod, acc_gate, streams):
        total = None
        for s in range(streams):
            dmods[l][s][kbase] = acc_mod[s, 0]
            dmods[l][s][kbase + 1] = acc_mod[s, 1] * g[0]
            if acc_gate is not None:
                dmods[l][s][kbase + 2] = acc_gate[s, 0]
            part = acc_mod[s, 1] * (1.0 + mods[l][s, kbase + 1])
            total = part if total is None else total + part
        dnorm[l][k_norm] = total

    xc0 = _stack_rows(x, ctx, "stack_tokens")
    wi1_0 = fetch(0, None)["wi1_0"]
    h0 = _modulate(xc0, mods[0], ng(0, 0), 0, 1, t, BF16, "mod_l0f1")
    xc1, sv_f1, hc = ffn_fwd(xc0, h0, 0, 0, wi1_0, lambda act: fetch(1, act)["wo1_0"], t, "l0f1",
                             (mods[0], ng(0, 1), 3, 4, BF16))
    w_in_t = fetch(2, hc)["w_in_t"]
    n_proj = w_in_t.shape[1]
    zall = _mm([(hc, w_in_t, 0, 0)], NT, n_proj, F32, "proj", tm_pref=TALL_TILES,
               tn_pref=(n_proj,))
    offs = np.cumsum((0,) + PROJ_SIZES)
    part = lambda i, rows=slice(None): zall[rows, offs[i]:offs[i + 1]]
    lat, con = slice(0, t), slice(t, tc)
    cos, sin = _rope_tables(t)
    qa = _heads(_rope(zall, cos, sin, False, "rope_q", view=(A_Q, int(offs[0]) // A_Q)), A_HEADS)
    ka = _heads(_rope(zall, cos, sin, False, "rope_k", view=(A_KV, int(offs[1]) // A_KV)), A_KV_HEADS)
    va = _heads(part(2, lat), A_KV_HEADS)
    kca, vca = _heads(part(1, con), A_KV_HEADS), _heads(part(2, con), A_KV_HEADS)
    kap, vap = _pad_rows(ka, WINDOW), _pad_rows(va, WINDOW)
    sink = wts["sink"].reshape(A_HEADS, 1, 1)
    o_a = _attn_fwd(qa, kap, vap, kca, vca, sink, "attn_fwd")

    vb = part(5)
    qk_cols = (int(offs[3]) // B_QK, int(offs[4]) // B_QK)
    rb = part(6, lat)
    zg = part(7)
    zg_f, zg_b = zg[:, :B_GATE_RANK], zg[:, B_GATE_RANK:]
    w2f, w2b, b2f, b2b = wts["w_a2_f"], wts["w_a2_b"], wts["b_a_f"], wts["b_a_b"]
    la_f = _gate_fwd(zg_f, w2f, b2f, "gate_f")
    la_b = _gate_fwd(zg_b, w2b, b2b, "gate_b")
    o_f, s_f, o_b, s_b = _gla_fwd(zall, zall, vb, la_f, la_b, t, "gla_fwd", qk_cols=qk_cols)
    gla_g = wts["gla_g"]
    go = _gla_out_fwd(o_f, o_b, rb, gla_g, "gla_out")
    cat = jnp.concatenate([_unheads(o_a), go], axis=-1).astype(BF16)
    big = fetch(3, cat)
    w_out, wi2_0, wo2_0 = big["w_out"], big["wi2_0"], big["wo2_0"]
    x2, y_mix0, h2 = _mm_resid(cat, w_out, 0, xc1, mods[0], 5, 1.0, t, "w_out", nxt=(mods[0], ng(0, 2), 6, 7, BF16))
    x3, sv_f2, h3 = ffn_fwd(x2, h2, 0, 6, wi2_0, wo2_0, t, "l0f2", (mods[1], ng(1, 0), 0, 1, BF16))

    big = fetch(4, x3)
    wi1_1, wo1_1, wi2_1, wo2_1 = big["wi1_1"], big["wo1_1"], big["wi2_1"], big["wo2_1"]
    x4, sv_g1, hp = ffn_fwd(x3, h3, 1, 0, wi1_1, wo1_1, t, "l1f1", (mods[1], ng(1, 1), 3, 4, F32))
    w_pool, pool_scale = big["w_pool"], wts["pool_scale"]
    x5, pooled, ypre = _pool_fwd(hp, w_pool, pool_scale, x4, mods[1], 5, "pool_fwd")
    h5 = _modulate(x5, mods[1], ng(1, 2), 6, 7, t, BF16, "mod_l1f2")
    x6, sv_g2, _ = ffn_fwd(x5, h5, 1, 6, wi2_1, wo2_1, t, "l1f2", None)

    y_of = lambda saved: saved[4]
    dx6, loss_vec, dfinal_g, dy, acc_gate = _final_loss(x6, wts["final_g"], target, (y_of(sv_g2), mods[1], 8, 0.5),
                                                        "final_loss")
    grads["final_g"] = dfinal_g[0]

    dx5, acc_mod = ffn_bwd(dx6, dy, sv_g2, 1, 6, ng(1, 2), t, "l1f2", 0)
    record(1, 6, 2, ng(1, 2), acc_mod, acc_gate, 1)
    dhp, dw_pool, acc_pool = _pool_bwd(dx5, w_pool, pool_scale, pooled, ypre, mods[1], 5, "pool_bwd")
    grads["pool_scale"] = acc_pool[0]
    dmods[1][0][5] = acc_pool[1]
    dx4, acc_mod, dy, acc_gate = _modulate_bwd(x4, dhp, dx5, mods[1], ng(1, 1), 4, t, "mod_bwd_l1mix",
                                               then=(y_of(sv_g1), mods[1], 2, 0.5))
    record(1, 3, 1, ng(1, 1), acc_mod, None, 1)
    dx3, acc_mod, dy, acc_gate_next = ffn_bwd(dx4, dy, sv_g1, 1, 0, ng(1, 0), t, "l1f1", 1,
                                              then=(y_of(sv_f2), mods[0], 8, 0.5))
    record(1, 0, 0, ng(1, 0), acc_mod, acc_gate, 1)

    dx2, acc_mod, dymix, acc_gate_mix = ffn_bwd(dx3, dy, sv_f2, 0, 6, ng(0, 2), t, "l0f2", 2,
                                                then=(y_mix0, mods[0], 5, 1.0))
    record(0, 6, 2, ng(0, 2), acc_mod, acc_gate_next, 1)
    dmods[0][0][5] = acc_gate_mix[0, 0]
    dw_out = _mm_tn(cat, dymix, BF16, "dw_out")
    dcat = _mm([(dymix, w_out, 0, 0)], NT, cat.shape[1], F32, "dcat")
    do_a = _heads(dcat[:, :A_Q], A_HEADS)
    do_full, drb, dgla_g = _gla_out_bwd(o_f, o_b, rb, gla_g, dcat[:, A_Q:], "gla_out_bwd")
    grads["gla_g"] = dgla_g[0]
    dq_f, dk_f, dv_f, dla_f, dq_b, dk_b, dv_b, dla_b = _gla_bwd(zall, zall, vb, la_f, la_b, s_f, s_b, do_full, t,
                                                                "gla_bwd", qk_cols=qk_cols)
    dzg_f, dw2f, db2f = _gate_bwd(zg_f, w2f, b2f, dla_f, "gate_bwd_f")
    dzg_b, dw2b, db2b = _gate_bwd(zg_b, w2b, b2b, dla_b, "gate_bwd_b")
    grads.update(w_a2_f=dw2f, w_a2_b=dw2b, b_a_f=db2f[0], b_a_b=db2b[0])
    dqa_r, dkap, dvap, dkca, dvca, dsink = _attn_bwd(qa, kap, vap, kca, vca, sink, o_a, do_a, "attn_bwd")
    grads["sink"] = dsink[:, 0, 0]
    dqa = _rope(_unheads(dqa_r), cos, sin, True, "rope_bwd_q")
    dka = _rope(_unheads(dkap[:, WINDOW:WINDOW + t]), cos, sin, True, "rope_bwd_k")
    dva = dvap[:, WINDOW:WINDOW + t]
    dzg = jnp.concatenate([dzg_f, dzg_b, jnp.zeros((tc, n_proj - PROJ_DIM), F32)], axis=-1)
    dzall = _assemble_dz(
        [dqa, dka, _unheads(dva), None, None, None, drb, None],
        [None, _unheads(dkca), _unheads(dvca), None, None, None, None, None],
        [None, None, None, [dq_f, dq_b], [dk_f, dk_b], [dv_f, dv_b], None, [dzg]], n_proj, "assemble_dz")
    dw_in_t = _mm_tn(dzall, hc, BF16, "dw_in")
    token = emit(3, [dw_in_t, dw_out, dw_pool])
    dhc = _mm([(dzall, w_in_t, 0, 0)], NN, d, F32, "dhc", tm_pref=TALL_TILES, dep=token)
    dxc1, acc_mod, dy, acc_gate = _modulate_bwd(xc1, dhc, dx2, mods[0], ng(0, 1), 4, t, "mod_bwd_l0mix",
                                                then=(y_of(sv_f1), mods[0], 2, 0.5))
    record(0, 3, 1, ng(0, 1), acc_mod, None, 2)
    dxc0, acc_mod = ffn_bwd(dxc1, dy, sv_f1, 0, 0, ng(0, 0), t, "l0f1", 4, split=True)
    record(0, 0, 0, ng(0, 0), acc_mod, acc_gate, 2)

    grads["norm_g"] = jnp.stack([jnp.stack(dnorm[0]), jnp.stack(dnorm[1])])
    zero = jnp.zeros((d,), F32)
    dmods_arr = jnp.stack([jnp.stack([jnp.stack([v if v is not None else zero for v in dmods[l][s]])
                                      for s in range(2)]) for l in range(2)])
    return loss_vec, dxc0, grads, dmods_arr


def _pack(parts):
    flat = jnp.concatenate([p.reshape(-1).astype(F32) for p in parts])
    pad = (-flat.shape[0]) % 128
    return jnp.pad(flat, (0, pad))[None, :]


def _unpack(rows, shapes):
    out, off = [], 0
    for s in shapes:
        n = int(np.prod(s))
        out.append(rows[:, off:off + n].reshape((rows.shape[0],) + tuple(s)))
        off += n
    return out


def _cols_to_full(g):
    g = jnp.moveaxis(g, 0, -2)
    return g.reshape(g.shape[:-2] + (-1,))


def kernel(x, c, ctx, c_ctx, w_mod, b_mod, norm_g, ffn1_wi, ffn1_wo, ffn2_wi, ffn2_wo, w_in, w_a2_f, b_a_f, w_a2_b, b_a_b, sink, gla_g, w_out, w_pool, pool_scale, final_g, loss_target, m_c_ctx, m_w_mod, m_b_mod, m_norm_g, m_ffn1_wi, m_ffn1_wo, m_ffn2_wi, m_ffn2_wo, m_w_in, m_w_a2_f, m_b_a_f, m_w_a2_b, m_b_a_b, m_sink, m_gla_g, m_w_out, m_w_pool, m_pool_scale, m_final_g, v_c_ctx, v_w_mod, v_b_mod, v_norm_g, v_ffn1_wi, v_ffn1_wo, v_ffn2_wi, v_ffn2_wo, v_w_in, v_w_a2_f, v_b_a_f, v_w_a2_b, v_b_a_b, v_sink, v_gla_g, v_w_out, v_w_pool, v_pool_scale, v_final_g):
    t, d = x.shape[1], x.shape[2]
    me = _dev_index()
    nc = w_mod.shape[2]
    ncol_in = w_in.shape[2]
    ncol_pad = -(-ncol_in // 16) * 16

    small_shapes = [(d,), norm_g.shape, pool_scale.shape, w_a2_f.shape, w_a2_b.shape]
    g1 = _gather_small(_pack([c, norm_g, pool_scale, w_a2_f, w_a2_b]), "gather_params")
    c_all, norm_g_all, pool_scale_all, w2f_all, w2b_all = _unpack(g1, small_shapes)
    wts = {
        "norm_g": _cols_to_full(norm_g_all),
        "pool_scale": _cols_to_full(pool_scale_all),
        "w_a2_f": _cols_to_full(w2f_all)[0],
        "w_a2_b": _cols_to_full(w2b_all)[0],
        "b_a_f": b_a_f, "b_a_b": b_a_b, "sink": sink[0], "gla_g": gla_g, "final_g": final_g[None, :],
    }

    craw = jnp.concatenate([c_all, c_ctx[None, :], jnp.zeros((16 - N_DEV - 1, d), F32)], axis=0)
    b_cols = lax.dynamic_slice_in_dim(b_mod, me * nc, nc, axis=1)[:, None, :]
    mm_cols = _adaln_fwd(craw, w_mod, b_cols, "adaln_fwd")
    g2 = _gather_small(mm_cols.reshape(1, -1), "gather_mods").reshape(N_DEV, 2, 16, nc)
    mm_full = jnp.moveaxis(g2, 0, 2).reshape(2, 16, N_MOD, d)
    mods = jnp.stack([lax.dynamic_index_in_dim(mm_full, me, axis=1, keepdims=False), mm_full[:, N_DEV]], axis=1)
    mods = jnp.pad(mods, ((0, 0), (0, 0), (0, 16 - N_MOD), (0, 0)))

    tr = lambda w: jnp.swapaxes(w, 1, 2).astype(BF16)
    wi1_sh, wi2_sh, wo1_sh, wo2_sh = tr(ffn1_wi), tr(ffn2_wi), ffn1_wo.astype(BF16), ffn2_wo.astype(BF16)
    w_in_sh = jnp.pad(tr(w_in), ((0, 0), (0, ncol_pad - ncol_in), (0, 0)))
    groups = [
        {"wi1_0": wi1_sh[0:1]},
        {"wo1_0": wo1_sh[0:1]},
        {"w_in": w_in_sh},
        {"w_out": w_out.astype(BF16), "wi2_0": wi2_sh[0:1], "wo2_0": wo2_sh[0:1]},
        {"wi1_1": wi1_sh[1:2], "wo1_1": wo1_sh[1:2], "wi2_1": wi2_sh[1:2], "wo2_1": wo2_sh[1:2],
         "w_pool": w_pool[0].astype(BF16)},
    ]

    reach = lambda gi: NEAR_PEERS if gi == 0 else N_DEV - 1
    gathers, token = [], mods
    for gi, grp in enumerate(groups):
        lands = [_place_shard(s, me, f"gather_place_{nm}") for nm, s in grp.items()]
        gathers.append(_exchange_start(list(grp.values()), lands, True, 1 + gi, token, f"gather_start_{gi}",
                                       n_peers=reach(gi)))
        token = gathers[-1][4]
    n_proj = -(-(N_DEV * ncol_in) // 128) * 128
    forward_id = 1 + len(groups) + 6

    def fetch(gi, after):
        _, lands = _exchange_wait(gathers[gi], True, token if after is None else after, f"gather_wait_{gi}",
                                  n_peers=reach(gi))
        if gi == 0:
            rows = [s.shape[1] for s in groups[gi].values()]
            passed = _forward_start(lands, rows, forward_id, "gather_forward")
            lands = _forward_wait(passed, rows, passed[3], "gather_forward_wait")
        out = dict(zip(groups[gi].keys(), lands))
        if "w_in" in out:
            w_in_t = out.pop("w_in").reshape(1, N_DEV, ncol_pad, d)[:, :, :ncol_in].reshape(1, N_DEV * ncol_in, d)
            out["w_in_t"] = jnp.pad(w_in_t, ((0, 0), (0, n_proj - N_DEV * ncol_in), (0, 0)))
        return out

    scatters = []

    def emit(stage, arrays):
        if stage == 3:
            dw_in_t, dw_out, dw_pool = arrays
            dw_in_full = dw_in_t[:N_DEV * ncol_in].reshape(N_DEV, ncol_in, d)
            dw_in_full = jnp.pad(dw_in_full, ((0, 0), (0, ncol_pad - ncol_in), (0, 0)))
            srcs = [dw_in_full.reshape(1, N_DEV * ncol_pad, d), dw_out[None], dw_pool.astype(BF16)]
        else:
            srcs = [a[None] for a in arrays]
        lands = [lax.empty((N_DEV, s.shape[0], s.shape[1] // N_DEV, s.shape[2]), s.dtype) for s in srcs]
        scatters.append(_exchange_start(srcs, lands, False, 1 + len(groups) + stage, None, f"scatter_start_{stage}"))
        return scatters[-1][4]

    loss_vec, grad_x, grads, dmods = _local_step(x[0], ctx[0], loss_target[0], mods, wts, fetch, emit)

    def reduce_stage(stage, after):
        wholes, lands = _exchange_wait(scatters[stage], False, after, f"scatter_wait_{stage}")
        return [_sum_slots(ld, wh, me, f"sum_grad_{stage}_{i}") for i, (ld, wh) in enumerate(zip(lands, wholes))]

    (dwi2_1, dwo2_1), (dwi1_1, dwo1_1), (dwi2_0, dwo2_0), (dw_in_s, dw_out_s, dw_pool_s) = [
        reduce_stage(stage, grad_x) for stage in range(4)]
    back = lambda g: jnp.swapaxes(g, 1, 2)
    g_big = {
        "ffn2_wi": back(jnp.concatenate([dwi2_0, dwi2_1], axis=0)), "ffn2_wo": jnp.concatenate([dwo2_0, dwo2_1], axis=0),
        "w_in": back(dw_in_s[:, :ncol_in]), "w_out": dw_out_s, "w_pool": dw_pool_s[None],
    }

    order = ["c_ctx", "w_mod", "b_mod", "norm_g", "ffn1_wi", "ffn1_wo", "ffn2_wi", "ffn2_wo", "w_in", "w_a2_f", "b_a_f",
             "w_a2_b", "b_a_b", "sink", "gla_g", "w_out", "w_pool", "pool_scale", "final_g"]
    ws = dict(c_ctx=c_ctx, w_mod=w_mod, b_mod=b_mod, norm_g=norm_g, ffn1_wi=ffn1_wi, ffn1_wo=ffn1_wo, ffn2_wi=ffn2_wi,
              ffn2_wo=ffn2_wo, w_in=w_in, w_a2_f=w_a2_f, b_a_f=b_a_f, w_a2_b=w_a2_b, b_a_b=b_a_b, sink=sink, gla_g=gla_g,
              w_out=w_out, w_pool=w_pool, pool_scale=pool_scale, final_g=final_g)
    ms = dict(c_ctx=m_c_ctx, w_mod=m_w_mod, b_mod=m_b_mod, norm_g=m_norm_g, ffn1_wi=m_ffn1_wi, ffn1_wo=m_ffn1_wo,
              ffn2_wi=m_ffn2_wi, ffn2_wo=m_ffn2_wo, w_in=m_w_in, w_a2_f=m_w_a2_f, b_a_f=m_b_a_f, w_a2_b=m_w_a2_b,
              b_a_b=m_b_a_b, sink=m_sink, gla_g=m_gla_g, w_out=m_w_out, w_pool=m_w_pool, pool_scale=m_pool_scale,
              final_g=m_final_g)
    vs = dict(c_ctx=v_c_ctx, w_mod=v_w_mod, b_mod=v_b_mod, norm_g=v_norm_g, ffn1_wi=v_ffn1_wi, ffn1_wo=v_ffn1_wo,
              ffn2_wi=v_ffn2_wi, ffn2_wo=v_ffn2_wo, w_in=v_w_in, w_a2_f=v_w_a2_f, b_a_f=v_b_a_f, w_a2_b=v_w_a2_b,
              b_a_b=v_b_a_b, sink=v_sink, gla_g=v_gla_g, w_out=v_w_out, w_pool=v_w_pool, pool_scale=v_pool_scale,
              final_g=v_final_g)
    early, late = ["ffn2_wi", "ffn2_wo", "w_out", "w_in", "w_pool"], ["ffn1_wi", "ffn1_wo"]
    big = early + ["w_mod"] + late
    delta, new_m, new_v = {}, {}, {}
    g_all = dict(g_big)

    def adamw_big(nm):
        shp = ws[nm].shape
        two_d = lambda a: a.reshape(-1, shp[-1])
        dl, nm_, nv_ = _adamw(two_d(ws[nm]), two_d(g_all[nm]), two_d(ms[nm]), two_d(vs[nm]), f"adamw_{nm}")
        delta[nm], new_m[nm], new_v[nm] = dl.reshape(shp), nm_.reshape(shp), nv_.reshape(shp)

    for nm in early:
        adamw_big(nm)

    small_g = [dmods[:, :, :N_MOD].reshape(2, 2, N_MOD * d), grads["norm_g"], grads["pool_scale"], grads["final_g"],
               grads["b_a_f"], grads["b_a_b"], grads["sink"], grads["gla_g"], grads["w_a2_f"], grads["w_a2_b"],
               loss_vec]
    small_g_shapes = [a.shape for a in small_g]
    g3 = _gather_small(_pack(small_g), "gather_small_grads", dep=delta["w_out"])
    total = _sum_rows8(g3, "sum_small_grads")
    dmm_all = _unpack(g3, small_g_shapes[:1])[0]
    (dmm_sum, dnorm_g, dpool_scale, dfinal_g, db_a_f, db_a_b, dsink, dgla_g, dw_a2_f, dw_a2_b, loss_all) = [
        a[0] for a in _unpack(total, small_g_shapes)]
    loss = jnp.sum(loss_all)
    dmm_rows = jnp.concatenate([dmm_all[:, :, 0].transpose(1, 0, 2), dmm_sum[:, 1][:, None, :],
                                jnp.zeros((2, 16 - N_DEV - 1, N_MOD * d), F32)], axis=1)
    grad_b_mod = dmm_sum[:, 0] + dmm_sum[:, 1]
    dmm_cols = lax.dynamic_slice_in_dim(dmm_rows, me * nc, nc, axis=2)
    cs_t = jnp.transpose(_silu(craw)).astype(BF16)
    grad_w_mod, dcraw = _adaln_bwd(craw, cs_t, dmm_cols, w_mod, "adaln_bwd")
    g4 = _gather_small((dcraw[0, N_DEV] + dcraw[1, N_DEV])[None, :], "gather_c_ctx_grad")
    grad_c_ctx = _sum_rows8(g4, "sum_c_ctx_grad")[0]

    col = lambda v, n: lax.dynamic_slice_in_dim(v, me * n, n, axis=v.ndim - 1)
    g_small = {
        "c_ctx": grad_c_ctx, "b_mod": grad_b_mod, "norm_g": col(dnorm_g, norm_g.shape[2]),
        "w_a2_f": col(dw_a2_f, w_a2_f.shape[2])[None], "b_a_f": db_a_f[None], "w_a2_b": col(dw_a2_b, w_a2_b.shape[2])[None],
        "b_a_b": db_a_b[None], "sink": dsink[None], "gla_g": dgla_g[None], "pool_scale": col(dpool_scale, pool_scale.shape[1])[None],
        "final_g": dfinal_g,
    }
    g_all.update(g_small, w_mod=grad_w_mod)
    adamw_big("w_mod")
    rest = [nm for nm in order if nm not in big]
    rest_shapes = [ws[nm].shape for nm in rest]
    packed = [_pack([d_[nm].reshape(ws[nm].shape) for nm in rest]).reshape(-1, 128) for d_ in (ws, g_all, ms, vs)]
    pad_rows = (-packed[0].shape[0]) % 512
    packed = [jnp.pad(p, ((0, pad_rows), (0, 0))) for p in packed]
    outs = _adamw(*packed, "adamw_small")
    for dst, arr in zip((delta, new_m, new_v), outs):
        for nm, val in zip(rest, _unpack(arr.reshape(1, -1), rest_shapes)):
            dst[nm] = val[0]

    (dwo1_0,), (dwi1_0,) = reduce_stage(4, outs[0]), reduce_stage(5, outs[0])
    g_all["ffn1_wi"] = back(jnp.concatenate([dwi1_0, dwi1_1], axis=0))
    g_all["ffn1_wo"] = jnp.concatenate([dwo1_0, dwo1_1], axis=0)
    for nm in late:
        adamw_big(nm)
    g_all = {nm: g_all[nm].reshape(ws[nm].shape) for nm in order}

    return (loss, grad_x[None], *[g_all[nm] for nm in order], *[delta[nm] for nm in order],
            *[new_m[nm] for nm in order], *[new_v[nm] for nm in order])
```

```python
import functools

import numpy as np
import jax
import jax.numpy as jnp
from jax import lax
from jax.experimental import pallas as pl
from jax.experimental.pallas import tpu as pltpu

F32 = jnp.float32
BF16 = jnp.bfloat16
MESH = pl.DeviceIdType.MESH

N_DEV = 8
RMS_EPS = 1e-6
N_MOD = 9
GRID_W = 64
A_HEADS, A_KV_HEADS, A_HEAD_DIM = 8, 2, 64
A_REP = A_HEADS // A_KV_HEADS
WINDOW = 128
ROPE_BASE = 10000.0
B_HEADS, B_DK, B_DV = 4, 64, 128
B_GATE_RANK = 16
B_GATE_NORM = 16.0
B_CHUNK = 64
POOL_WINDOWS = (2, 4, 8, 16)
POOL_PAD = 8
A_Q = A_HEADS * A_HEAD_DIM
A_KV = A_KV_HEADS * A_HEAD_DIM
B_QK = B_HEADS * B_DK
B_V = B_HEADS * B_DV
PROJ_SIZES = (A_Q, A_KV, A_KV, B_QK, B_QK, B_V, B_V, 2 * B_GATE_RANK)
PROJ_DIM = sum(PROJ_SIZES)
ADAM_LR, ADAM_B1, ADAM_B2, ADAM_EPS, ADAM_WD, ADAM_STEP = 0.001, 0.9, 0.999, 1e-08, 0.01, 10

VMEM_LIMIT = 56 * 1024 * 1024
ROW_TILES = (512, 544, 256, 128, 64, 32, 16, 8)
TALL_TILES = (1024, 1088) + ROW_TILES

NN = ((1,), (0,))
NT = ((1,), (1,))
TN = ((0,), (0,))


def _dot(a, b, dims=NN, prec=None):
    return lax.dot_general(a, b, (dims, ((), ())), precision=prec, preferred_element_type=F32)


def _bdot(a, b, dims=NN):
    return _dot(a.astype(BF16), b.astype(BF16), dims)


def _dot_01(sel, x):
    hi = x.astype(BF16)
    rest = x - hi.astype(F32)
    mid = rest.astype(BF16)
    lo = (rest - mid.astype(F32)).astype(BF16)
    sel = sel.astype(BF16)
    return _dot(sel, hi) + _dot(sel, mid) + _dot(sel, lo)


def _params(sem=None, **kw):
    return pltpu.CompilerParams(dimension_semantics=sem, vmem_limit_bytes=VMEM_LIMIT, **kw)


def _silu(a):
    return a * jax.nn.sigmoid(a)


def _pick(n, prefs):
    for p in prefs:
        if n % p == 0:
            return p
    return n


def _full(shape):
    nd = len(shape)
    return pl.BlockSpec(shape, lambda *_: (0,) * nd)


def _peers():
    x, y, c = lax.axis_index("x"), lax.axis_index("y"), lax.axis_index("c")
    return x, y, c


def _dev_index():
    x, y, c = _peers()
    return 4 * x + 2 * y + c


def _others(x, y, c):
    return [(x, y, 1 - c), (1 - x, y, c), (x, 1 - y, c), (1 - x, 1 - y, c),
            (1 - x, y, 1 - c), (x, 1 - y, 1 - c), (1 - x, 1 - y, 1 - c)]


def _index_of(dev):
    return 4 * dev[0] + 2 * dev[1] + dev[2]


def _exchange_refs(gather, shapes, srcs, lands, a, me, to):
    if gather:
        r = shapes[a][1]
        return srcs[a], lands[a].at[:, pl.ds(_index_of(me) * r, r), :]
    r = shapes[a][1] // N_DEV
    return srcs[a].at[:, pl.ds(_index_of(to) * r, r), :], lands[a].at[_index_of(me)]


HBM_SPEC = pl.BlockSpec(memory_space=pltpu.HBM)
SEM_SPEC = pl.BlockSpec(memory_space=pltpu.SEMAPHORE)
EFFECT = pltpu.SideEffectType.DATAFLOW_SIDE_EFFECTING


NEAR_PEERS = 4


def _exchange_start(srcs, lands, gather, collective_id, dep, name, n_peers=N_DEV - 1):
    n = len(srcs)
    shapes = [s.shape for s in srcs]
    deps = [] if dep is None else [dep]

    def body(*refs):
        src_refs, land_refs = refs[:n], refs[n:2 * n]
        send_sems, recv_sems = refs[2 * n + len(deps)], refs[2 * n + len(deps) + 1]
        token = refs[-1]
        x, y, c = _peers()
        others = _others(x, y, c)[:n_peers]
        barrier = pltpu.get_barrier_semaphore()
        for peer in others:
            pl.semaphore_signal(barrier, inc=1, device_id=peer, device_id_type=MESH)
        pl.semaphore_wait(barrier, len(others))
        for a in range(n):
            for k, to in enumerate(others):
                src, dst = _exchange_refs(gather, shapes, src_refs, land_refs, a, (x, y, c), to)
                pltpu.make_async_remote_copy(src_ref=src, dst_ref=dst, send_sem=send_sems.at[7 * a + k],
                                             recv_sem=recv_sems.at[7 * a + k], device_id=to, device_id_type=MESH).start()
        token[...] = jnp.zeros_like(token)

    outs = pl.pallas_call(
        body, name=name,
        out_shape=(pltpu.SemaphoreType.DMA((7 * n,)), pltpu.SemaphoreType.DMA((7 * n,)),
                   *[pltpu.HBM(s.shape, s.dtype) for s in srcs], *[pltpu.HBM(l.shape, l.dtype) for l in lands],
                   jax.ShapeDtypeStruct((8, 128), F32)),
        in_specs=[HBM_SPEC] * (2 * n) + [pl.BlockSpec(memory_space=pl.ANY)] * len(deps),
        out_specs=(SEM_SPEC, SEM_SPEC, *[HBM_SPEC] * (2 * n), pl.BlockSpec(memory_space=pltpu.VMEM)),
        input_output_aliases={i: 2 + i for i in range(2 * n)},
        compiler_params=pltpu.CompilerParams(has_side_effects=EFFECT, collective_id=collective_id),
    )(*[pltpu.with_memory_space_constraint(s, pltpu.HBM) for s in srcs],
      *[pltpu.with_memory_space_constraint(l, pltpu.HBM) for l in lands], *deps)
    return outs[0], outs[1], list(outs[2:2 + n]), list(outs[2 + n:2 + 2 * n]), outs[-1]


def _exchange_wait(started, gather, after, name, n_peers=N_DEV - 1):
    send_sems, recv_sems, srcs, lands, _ = started
    n = len(srcs)
    shapes = [s.shape for s in srcs]

    def body(*refs):
        src_refs, land_refs = refs[:n], refs[n:2 * n]
        send_sems, recv_sems = refs[2 * n], refs[2 * n + 1]
        x, y, c = _peers()
        for a in range(n):
            for k, peer in enumerate(_others(x, y, c)[:n_peers]):
                src, _ = _exchange_refs(gather, shapes, src_refs, land_refs, a, (x, y, c), peer)
                _, dst = _exchange_refs(gather, shapes, src_refs, land_refs, a, peer, (x, y, c))
                copy = pltpu.make_async_remote_copy(src_ref=src, dst_ref=dst, send_sem=send_sems.at[7 * a + k],
                                                    recv_sem=recv_sems.at[7 * a + k], device_id=peer, device_id_type=MESH)
                copy.wait_send()
                copy.wait_recv()

    outs = pl.pallas_call(
        body, name=name,
        out_shape=(*[pltpu.HBM(s.shape, s.dtype) for s in srcs], *[pltpu.HBM(l.shape, l.dtype) for l in lands]),
        in_specs=[HBM_SPEC] * (2 * n) + [SEM_SPEC, SEM_SPEC, pl.BlockSpec(memory_space=pl.ANY)],
        out_specs=tuple([HBM_SPEC] * (2 * n)),
        input_output_aliases={i: i for i in range(2 * n)},
        compiler_params=pltpu.CompilerParams(has_side_effects=EFFECT),
    )(*srcs, *lands, send_sems, recv_sems, after)
    return list(outs[:n]), list(outs[n:])


def _forward_refs(land_refs, rows, a, others, j, received):
    origin = others[j + 3] if received else others[j]
    return land_refs[a].at[:, pl.ds(_index_of(origin) * rows[a], rows[a]), :]


def _forward_start(lands, rows, collective_id, name):
    n = len(lands)

    def body(*refs):
        land_refs, send_sems, recv_sems, token = refs[:n], refs[n], refs[n + 1], refs[-1]
        x, y, c = _peers()
        others = _others(x, y, c)
        barrier = pltpu.get_barrier_semaphore()
        pl.semaphore_signal(barrier, inc=1, device_id=others[0], device_id_type=MESH)
        pl.semaphore_wait(barrier, 1)
        for a in range(n):
            for j in (1, 2, 3):
                blk = _forward_refs(land_refs, rows, a, others, j, False)
                pltpu.make_async_remote_copy(src_ref=blk, dst_ref=blk, send_sem=send_sems.at[3 * a + j - 1],
                                             recv_sem=recv_sems.at[3 * a + j - 1], device_id=others[0],
                                             device_id_type=MESH).start()
        token[...] = jnp.zeros_like(token)

    outs = pl.pallas_call(
        body, name=name,
        out_shape=(pltpu.SemaphoreType.DMA((3 * n,)), pltpu.SemaphoreType.DMA((3 * n,)),
                   *[pltpu.HBM(l.shape, l.dtype) for l in lands], jax.ShapeDtypeStruct((8, 128), F32)),
        in_specs=[HBM_SPEC] * n,
        out_specs=(SEM_SPEC, SEM_SPEC, *[HBM_SPEC] * n, pl.BlockSpec(memory_space=pltpu.VMEM)),
        input_output_aliases={i: 2 + i for i in range(n)},
        compiler_params=pltpu.CompilerParams(has_side_effects=EFFECT, collective_id=collective_id),
    )(*[pltpu.with_memory_space_constraint(l, pltpu.HBM) for l in lands])
    return outs[0], outs[1], list(outs[2:2 + n]), outs[-1]


def _forward_wait(started, rows, after, name):
    send_sems, recv_sems, lands, _ = started
    n = len(lands)

    def body(*refs):
        land_refs, send_sems, recv_sems = refs[:n], refs[n], refs[n + 1]
        x, y, c = _peers()
        others = _others(x, y, c)
        for a in range(n):
            for j in (1, 2, 3):
                copy = pltpu.make_async_remote_copy(
                    src_ref=_forward_refs(land_refs, rows, a, others, j, False),
                    dst_ref=_forward_refs(land_refs, rows, a, others, j, True), send_sem=send_sems.at[3 * a + j - 1],
                    recv_sem=recv_sems.at[3 * a + j - 1], device_id=others[0], device_id_type=MESH)
                copy.wait_send()
                copy.wait_recv()

    outs = pl.pallas_call(
        body, name=name, out_shape=tuple(pltpu.HBM(l.shape, l.dtype) for l in lands),
        in_specs=[HBM_SPEC] * n + [SEM_SPEC, SEM_SPEC, pl.BlockSpec(memory_space=pl.ANY)],
        out_specs=tuple([HBM_SPEC] * n), input_output_aliases={i: i for i in range(n)},
        compiler_params=pltpu.CompilerParams(has_side_effects=EFFECT),
    )(*lands, send_sems, recv_sems, after)
    return list(outs)


def _place_shard(shard, me, name):
    a_, r, c = shard.shape
    tr = _pick(r, (352, 304, 256, 128, 64, 32, 16, 8))
    nr = r // tr

    def body(me_ref, in_ref, out_ref):
        out_ref[...] = in_ref[...]

    return pl.pallas_call(
        body, name=name,
        grid_spec=pltpu.PrefetchScalarGridSpec(
            num_scalar_prefetch=1, grid=(a_, nr),
            in_specs=[pl.BlockSpec((None, tr, c), lambda i, j, me_ref: (i, j, 0))],
            out_specs=pl.BlockSpec((None, tr, c), lambda i, j, me_ref: (i, me_ref[0] * nr + j, 0))),
        out_shape=jax.ShapeDtypeStruct((a_, N_DEV * r, c), shard.dtype),
        compiler_params=_params(("parallel", "parallel")),
    )(me.reshape(1).astype(jnp.int32), shard)


def _sum_slots(land, whole, me, name):
    _, a_, r, c = land.shape
    tr = _pick(r, (352, 256, 128, 64, 32, 16, 8))
    nr = r // tr

    def body(me_ref, land_ref, own_ref, out_ref):
        acc = None
        for s in range(N_DEV):
            part = jnp.where(me_ref[0] == s, own_ref[...], land_ref[s]).astype(F32)
            acc = part if acc is None else acc + part
        out_ref[...] = acc

    return pl.pallas_call(
        body, name=name,
        grid_spec=pltpu.PrefetchScalarGridSpec(
            num_scalar_prefetch=1, grid=(a_, nr),
            in_specs=[pl.BlockSpec((N_DEV, None, tr, c), lambda i, j, me_ref: (0, i, j, 0)),
                      pl.BlockSpec((None, tr, c), lambda i, j, me_ref: (i, me_ref[0] * nr + j, 0))],
            out_specs=pl.BlockSpec((None, tr, c), lambda i, j, me_ref: (i, j, 0))),
        out_shape=jax.ShapeDtypeStruct((a_, r, c), F32),
        compiler_params=_params(("parallel", "parallel")),
    )(me.reshape(1).astype(jnp.int32), land, whole)


def _gather_small(vec, name, dep=None):
    p = vec.shape[1]
    pp = -(-p // 1024) * 1024
    blk = jnp.pad(vec, ((0, 0), (0, pp - p))).reshape(8, pp // 8)
    deps = [] if dep is None else [dep]

    def body(in_ref, *rest):
        out_ref, send_sems, recv_sems = rest[-3:]
        x, y, c = _peers()
        me = 4 * x + 2 * y + c
        others = [(x, y, 1 - c), (1 - x, y, c), (x, 1 - y, c), (1 - x, 1 - y, c),
                  (1 - x, y, 1 - c), (x, 1 - y, 1 - c), (1 - x, 1 - y, 1 - c)]

        def rows(idx):
            return out_ref.at[pl.ds(pl.multiple_of(idx * 8, 8), 8), :]

        out_ref[pl.ds(pl.multiple_of(me * 8, 8), 8), :] = in_ref[...]

        def copy(k, dev, slot):
            return pltpu.make_async_remote_copy(
                src_ref=in_ref, dst_ref=rows(slot), send_sem=send_sems.at[k], recv_sem=recv_sems.at[k],
                device_id=dev, device_id_type=MESH)

        sends = [copy(k, dev, me) for k, dev in enumerate(others)]
        for cp in sends:
            cp.start()
        for k, dev in enumerate(others):
            copy(k, dev, 4 * dev[0] + 2 * dev[1] + dev[2]).wait_recv()
        for cp in sends:
            cp.wait_send()

    vm = pl.BlockSpec(memory_space=pltpu.VMEM)
    out = pl.pallas_call(
        body, name=name, out_shape=jax.ShapeDtypeStruct((8 * N_DEV, pp // 8), F32),
        in_specs=[vm] + [pl.BlockSpec(memory_space=pl.ANY)] * len(deps), out_specs=vm,
        scratch_shapes=[pltpu.SemaphoreType.DMA((7,)), pltpu.SemaphoreType.DMA((7,))],
        compiler_params=pltpu.CompilerParams(has_side_effects=True, vmem_limit_bytes=VMEM_LIMIT),
    )(blk, *deps)
    return out.reshape(N_DEV, pp)[:, :p]


def _sum_rows8(g, name):
    p = g.shape[1]

    def body(in_ref, out_ref):
        acc = in_ref[0:1, :]
        for s in range(1, N_DEV):
            acc = acc + in_ref[s:s + 1, :]
        out_ref[...] = acc

    return pl.pallas_call(body, name=name, out_shape=jax.ShapeDtypeStruct((1, p), F32),
                          compiler_params=_params())(g)


def _sel_row(mods_ref, is_ctx, k):
    return jnp.where(is_ctx, mods_ref[1, k:k + 1, :], mods_ref[0, k:k + 1, :])


def _stream_tile(m, n_x):
    span = n_x if m == n_x else int(np.gcd(n_x, m - n_x))
    return _pick(span, (512, 256, 128, 64, 32, 16, 8))


def _modulate(z, mods, g, ks, kc, n_x, out_dtype, name):
    m, d = z.shape
    tm = _stream_tile(m, n_x)

    def body(z_ref, mods_ref, g_ref, h_ref):
        is_ctx = pl.program_id(0) * tm >= n_x
        zz = z_ref[...]
        r = lax.rsqrt(jnp.mean(zz * zz, axis=-1, keepdims=True) + RMS_EPS)
        shift, scale = _sel_row(mods_ref, is_ctx, ks), _sel_row(mods_ref, is_ctx, kc)
        h_ref[...] = ((zz * r) * g_ref[...] * (1.0 + scale) + shift).astype(out_dtype)

    return pl.pallas_call(
        body, name=name, grid=(m // tm,),
        in_specs=[pl.BlockSpec((tm, d), lambda i: (i, 0)), _full(mods.shape), _full(g.shape)],
        out_specs=pl.BlockSpec((tm, d), lambda i: (i, 0)),
        out_shape=jax.ShapeDtypeStruct((m, d), out_dtype),
        compiler_params=_params(("parallel",)),
    )(z, mods, g)


def _gate_bwd_rows(dx, y, gate, coef):
    return (coef * gate * dx).astype(BF16), jnp.sum(coef * y * dx, axis=0, keepdims=True)


def _modulate_bwd(z, dh, dres, mods, g, kc, n_x, name, latent_only=False, then=None):
    m, d = z.shape
    tm = _stream_tile(m, n_x)
    first_ctx = n_x // tm
    res_blocks = dres.shape[0] // tm
    out_blocks = (n_x if latent_only else m) // tm
    extra = [] if then is None else [then[0], then[1]]

    def body(z_ref, dh_ref, dres_ref, mods_ref, g_ref, *rest):
        i = pl.program_id(0)
        is_ctx = i * tm >= n_x
        dx_ref, acc_ref = rest[len(extra)], rest[len(extra) + 1]

        @pl.when((i == 0) | (i == first_ctx))
        def _():
            acc_ref[...] = jnp.zeros_like(acc_ref)
            if then is not None:
                rest[-1][...] = jnp.zeros_like(rest[-1])

        zz, dhh = z_ref[...], dh_ref[...]
        r = lax.rsqrt(jnp.mean(zz * zz, axis=-1, keepdims=True) + RMS_EPS)
        nz = zz * r
        gain = g_ref[...] * (1.0 + _sel_row(mods_ref, is_ctx, kc))
        dn = dhh * gain
        dz = r * (dn - nz * jnp.mean(dn * nz, axis=-1, keepdims=True))
        dx = jnp.where(i < res_blocks, dres_ref[...], 0.0) + dz

        @pl.when(i < out_blocks)
        def _():
            dx_ref[...] = dx

        acc_ref[0:1, :] += jnp.sum(dhh, axis=0, keepdims=True)
        acc_ref[1:2, :] += jnp.sum(dhh * nz, axis=0, keepdims=True)
        if then is not None:
            y_ref, tmods_ref, dy_ref, gate_acc_ref = rest[0], rest[1], rest[-2], rest[-1]
            dy, part = _gate_bwd_rows(dx, y_ref[...], _sel_row(tmods_ref, is_ctx, then[2]), then[3])
            dy_ref[...] = dy
            gate_acc_ref[0:1, :] += part

    row = pl.BlockSpec((tm, d), lambda i: (i, 0))
    acc_spec = pl.BlockSpec((None, 8, d), lambda i: ((i * tm >= n_x).astype(jnp.int32), 0, 0))
    out_specs = [pl.BlockSpec((tm, d), lambda i: (jnp.minimum(i, out_blocks - 1), 0)), acc_spec]
    out_shape = [jax.ShapeDtypeStruct((out_blocks * tm, d), F32), jax.ShapeDtypeStruct((2, 8, d), F32)]
    in_specs = [row, row, pl.BlockSpec((tm, d), lambda i: (jnp.minimum(i, res_blocks - 1), 0)),
                _full(mods.shape), _full(g.shape)]
    if then is not None:
        in_specs += [row, _full(then[1].shape)]
        out_specs += [row, acc_spec]
        out_shape += [jax.ShapeDtypeStruct((m, d), BF16), jax.ShapeDtypeStruct((2, 8, d), F32)]
    return pl.pallas_call(
        body, name=name, grid=(m // tm,), in_specs=in_specs, out_specs=out_specs, out_shape=out_shape,
        compiler_params=_params(("arbitrary",)),
    )(z, dh, dres, mods, g, *extra)


def _ffn_up(h, wi_t, layer, name):
    m, d = h.shape
    f = wi_t.shape[1] // 2
    tm = _pick(m, ROW_TILES)

    def body(h_ref, w_ref, jac_ref, act_ref):
        hh = h_ref[...]
        a = _dot(hh, w_ref[0:f, :], NT)
        u = _dot(hh, w_ref[f:2 * f, :], NT)
        sg = jax.nn.sigmoid(a)
        s = a * sg
        jac_ref[:, 0:f] = (u * (sg * (1.0 + a * (1.0 - sg)))).astype(BF16)
        jac_ref[:, f:2 * f] = s.astype(BF16)
        act_ref[...] = (s * u).astype(BF16)

    return pl.pallas_call(
        body, name=name, grid=(m // tm,),
        in_specs=[pl.BlockSpec((tm, d), lambda i: (i, 0)),
                  pl.BlockSpec((None, 2 * f, d), lambda i: (layer, 0, 0))],
        out_specs=[pl.BlockSpec((tm, 2 * f), lambda i: (i, 0)), pl.BlockSpec((tm, f), lambda i: (i, 0))],
        out_shape=[jax.ShapeDtypeStruct((m, 2 * f), BF16), jax.ShapeDtypeStruct((m, f), BF16)],
        compiler_params=_params(("parallel",)),
    )(h, wi_t)


def _mm_resid(a, b, layer, res, mods, km, coef, n_x, name, nxt=None):
    m, k = a.shape
    n = b.shape[2]
    tm = _pick(m, (512, 256, 128, 64, 32, 16, 8))
    tn = n if nxt is not None else _pick(n, (1024, 512, 256, 128))
    extra = [] if nxt is None else [nxt[0], nxt[1]]

    def body(a_ref, b_ref, res_ref, mods_ref, *rest):
        is_ctx = pl.program_id(1) * tm >= n_x
        y = _dot(a_ref[...], b_ref[...])
        new = res_ref[...] + coef * _sel_row(mods_ref, is_ctx, km) * y
        if nxt is None:
            out_ref, y_ref = rest
        else:
            nmods_ref, g_ref, out_ref, y_ref, h_ref = rest
            r = lax.rsqrt(jnp.mean(new * new, axis=-1, keepdims=True) + RMS_EPS)
            shift, scale = _sel_row(nmods_ref, is_ctx, nxt[2]), _sel_row(nmods_ref, is_ctx, nxt[3])
            h_ref[...] = ((new * r) * g_ref[...] * (1.0 + scale) + shift).astype(nxt[4])
        y_ref[...] = y.astype(BF16)
        out_ref[...] = new

    tile = pl.BlockSpec((tm, tn), lambda j, i: (i, j))
    outs = [jax.ShapeDtypeStruct((m, n), F32), jax.ShapeDtypeStruct((m, n), BF16)]
    if nxt is not None:
        outs.append(jax.ShapeDtypeStruct((m, n), nxt[4]))
    return pl.pallas_call(
        body, name=name, grid=(n // tn, m // tm),
        in_specs=[pl.BlockSpec((tm, k), lambda j, i: (i, 0)),
                  pl.BlockSpec((None, k, tn), lambda j, i: (layer, 0, j)),
                  tile, pl.BlockSpec((2, 16, tn), lambda j, i: (0, 0, j))] + [_full(e.shape) for e in extra],
        out_specs=[tile] * len(outs), out_shape=outs,
        compiler_params=_params(("parallel", "parallel")),
    )(a, b, res, mods, *extra)


def _ffn_down_bwd(dy, wo, layer, au, name):
    m, d = dy.shape
    f = wo.shape[1]
    tm = _pick(m, ROW_TILES)

    def body(dy_ref, wo_ref, au_ref, dau_ref):
        dact = _dot(dy_ref[...], wo_ref[...], NT)
        dau_ref[:, 0:f] = (dact * au_ref[:, 0:f].astype(F32)).astype(BF16)
        dau_ref[:, f:2 * f] = (dact * au_ref[:, f:2 * f].astype(F32)).astype(BF16)

    wide = pl.BlockSpec((tm, 2 * f), lambda i: (i, 0))
    return pl.pallas_call(
        body, name=name, grid=(m // tm,),
        in_specs=[pl.BlockSpec((tm, d), lambda i: (i, 0)), pl.BlockSpec((None, f, d), lambda i: (layer, 0, 0)), wide],
        out_specs=wide, out_shape=jax.ShapeDtypeStruct((m, 2 * f), BF16),
        compiler_params=_params(("parallel",)),
    )(dy, wo, au)


def _mm(terms, dims, n, out_dtype, name, tm_pref=(512, 256, 128, 64, 32, 16, 8), tn_pref=(512, 256, 128), dep=None):
    m = terms[0][0].shape[0]
    tm = _pick(m, tm_pref)
    tn = _pick(n, tn_pref)
    nt = len(terms)
    deps = [] if dep is None else [dep]

    def body(*refs):
        out_ref = refs[-1]
        acc = None
        for t in range(nt):
            part = _dot(refs[2 * t][...].astype(BF16), refs[2 * t + 1][...].astype(BF16), dims)
            acc = part if acc is None else acc + part
        out_ref[...] = acc.astype(out_dtype)

    in_specs, args = [], []
    for a, b, layer, rb in terms:
        k = a.shape[1]
        in_specs.append(pl.BlockSpec((tm, k), lambda j, i: (i, 0)))
        if dims == NN:
            in_specs.append(pl.BlockSpec((None, k, tn), lambda j, i, layer=layer, rb=rb: (layer, rb, j)))
        else:
            nb = n // tn
            in_specs.append(pl.BlockSpec((None, tn, k), lambda j, i, layer=layer, rb=rb, nb=nb: (layer, rb * nb + j, 0)))
        args += [a, b]
    return pl.pallas_call(
        body, name=name, grid=(n // tn, m // tm), in_specs=in_specs + [pl.BlockSpec(memory_space=pl.ANY)] * len(deps),
        out_specs=pl.BlockSpec((tm, tn), lambda j, i: (i, j)),
        out_shape=jax.ShapeDtypeStruct((m, n), out_dtype),
        compiler_params=_params(("parallel", "parallel")),
    )(*args, *deps)


def _mm_tn(a, b, out_dtype, name, dep=None):
    t = a.shape[0]
    m, n = a.shape[1], b.shape[1]
    tm = _pick(m, (1408, 2432, 1024, 512, 256, 128))
    tn = _pick(n, (1024, 512, 256, 128))
    tk = _pick(t, TALL_TILES)
    deps = [] if dep is None else [dep]

    def body(a_ref, b_ref, *rest):
        out_ref, acc_ref = rest[-2:]
        kk = pl.program_id(2)

        @pl.when(kk == 0)
        def _():
            acc_ref[...] = jnp.zeros_like(acc_ref)

        acc_ref[...] += _dot(a_ref[...].astype(BF16), b_ref[...].astype(BF16), TN)

        @pl.when(kk == pl.num_programs(2) - 1)
        def _():
            out_ref[...] = acc_ref[...].astype(out_dtype)

    return pl.pallas_call(
        body, name=name, grid=(m // tm, n // tn, t // tk),
        in_specs=[pl.BlockSpec((tk, tm), lambda i, j, k: (k, i)), pl.BlockSpec((tk, tn), lambda i, j, k: (k, j))]
        + [pl.BlockSpec(memory_space=pl.ANY)] * len(deps),
        out_specs=pl.BlockSpec((tm, tn), lambda i, j, k: (i, j)),
        out_shape=jax.ShapeDtypeStruct((m, n), out_dtype),
        scratch_shapes=[pltpu.VMEM((tm, tn), F32)],
        compiler_params=_params(("parallel", "parallel", "arbitrary")),
    )(a, b, *deps)


def _stack_rows(a, b, name):
    ta, d = a.shape
    tm = _pick(int(np.gcd(ta, b.shape[0])), (256, 128, 64, 32, 16, 8))
    na, nb = ta // tm, b.shape[0] // tm

    def body(a_ref, b_ref, o_ref):
        o_ref[...] = jnp.where(pl.program_id(0) < na, a_ref[...], b_ref[...])

    return pl.pallas_call(
        body, name=name, grid=(na + nb,),
        in_specs=[pl.BlockSpec((tm, d), lambda i: (jnp.minimum(i, na - 1), 0)),
                  pl.BlockSpec((tm, d), lambda i: (jnp.maximum(i - na, 0), 0))],
        out_specs=pl.BlockSpec((tm, d), lambda i: (i, 0)),
        out_shape=jax.ShapeDtypeStruct((ta + b.shape[0], d), a.dtype),
        compiler_params=_params(("parallel",)),
    )(a, b)


def _assemble_dz(lat_parts, ctx_parts, both_parts, width, name):
    t = next(p.shape[0] for p in lat_parts if p is not None)
    l_ctx = next(p.shape[0] for p in ctx_parts if p is not None)
    tm = _pick(int(np.gcd(t, l_ctx)), (256, 128, 64, 32, 16, 8))
    nt, nl = t // tm, l_ctx // tm
    plan, args, in_specs, off = [], [], [], 0
    lat_spec = lambda w: pl.BlockSpec((tm, w), lambda i: (jnp.minimum(i, nt - 1), 0))
    ctx_spec = lambda w: pl.BlockSpec((tm, w), lambda i: (jnp.maximum(i - nt, 0), 0))
    all_spec = lambda w: pl.BlockSpec((tm, w), lambda i: (i, 0))
    for lat, ctx, both in zip(lat_parts, ctx_parts, both_parts):
        if both:
            w = both[0].shape[1]
            plan.append(("both", off, w, len(args), len(both)))
            args += both
            in_specs += [all_spec(w)] * len(both)
        else:
            w = (lat if lat is not None else ctx).shape[1]
            plan.append(("split", off, w, len(args), (lat is not None, ctx is not None)))
            for part, spec in ((lat, lat_spec), (ctx, ctx_spec)):
                if part is not None:
                    args.append(part)
                    in_specs.append(spec(w))
        off += w
    n_in = len(args)

    def body(*refs):
        out_ref = refs[n_in]
        is_ctx = pl.program_id(0) >= nt
        for kind, o, w, first, info in plan:
            if kind == "both":
                val = refs[first][...]
                for k in range(1, info):
                    val = val + refs[first + k][...]
            else:
                has_lat, has_ctx = info
                zero = jnp.zeros((tm, w), F32)
                lat = refs[first][...] if has_lat else zero
                ctx = refs[first + int(has_lat)][...] if has_ctx else zero
                val = jnp.where(is_ctx, ctx, lat)
            out_ref[:, o:o + w] = val.astype(BF16)
        if off < width:
            out_ref[:, off:width] = jnp.zeros((tm, width - off), BF16)

    return pl.pallas_call(
        body, name=name, grid=(nt + nl,), in_specs=in_specs,
        out_specs=pl.BlockSpec((tm, width), lambda i: (i, 0)),
        out_shape=jax.ShapeDtypeStruct((t + l_ctx, width), BF16),
        compiler_params=_params(("parallel",)),
    )(*args)


def _final_loss(x, g, target, then, name):
    t, d = x.shape
    tm = _stream_tile(t, t)
    y, tmods, km, coef = then

    def body(x_ref, g_ref, t_ref, y_ref, tmods_ref, dx_ref, loss_ref, dg_ref, dy_ref, gate_acc_ref):
        @pl.when(pl.program_id(0) == 0)
        def _():
            loss_ref[...] = jnp.zeros_like(loss_ref)
            dg_ref[...] = jnp.zeros_like(dg_ref)
            gate_acc_ref[...] = jnp.zeros_like(gate_acc_ref)

        xx, gg = x_ref[...], g_ref[...]
        r = lax.rsqrt(jnp.mean(xx * xx, axis=-1, keepdims=True) + RMS_EPS)
        nz = xx * r
        err = nz * gg - t_ref[...]
        loss_ref[...] += jnp.sum(err * err, axis=0, keepdims=True) * (0.5 / d)
        dout = err * (1.0 / d)
        dg_ref[...] += jnp.sum(dout * nz, axis=0, keepdims=True)
        dn = dout * gg
        dx = r * (dn - nz * jnp.mean(dn * nz, axis=-1, keepdims=True))
        dx_ref[...] = dx
        dy, part = _gate_bwd_rows(dx, y_ref[...], tmods_ref[0, km:km + 1, :], coef)
        dy_ref[...] = dy
        gate_acc_ref[0:1, :] += part

    row = pl.BlockSpec((tm, d), lambda i: (i, 0))
    vec = pl.BlockSpec((1, d), lambda i: (0, 0))
    acc = pl.BlockSpec((None, 8, d), lambda i: (0, 0, 0))
    return pl.pallas_call(
        body, name=name, grid=(t // tm,), in_specs=[row, vec, row, row, _full(tmods.shape)],
        out_specs=[row, vec, vec, row, acc],
        out_shape=[jax.ShapeDtypeStruct((t, d), F32), jax.ShapeDtypeStruct((1, d), F32),
                   jax.ShapeDtypeStruct((1, d), F32), jax.ShapeDtypeStruct((t, d), BF16),
                   jax.ShapeDtypeStruct((2, 8, d), F32)],
        compiler_params=_params(("arbitrary",)),
    )(x, g, target, y, tmods)


def _adaln_fwd(craw, w_mod, b_cols, name):
    lyr, d, nc = w_mod.shape

    def body(c_ref, w_ref, b_ref, out_ref):
        out_ref[...] = _bdot(_silu(c_ref[...]), w_ref[...]) + b_ref[...]

    return pl.pallas_call(
        body, name=name, grid=(lyr,),
        in_specs=[_full(craw.shape), pl.BlockSpec((None, d, nc), lambda l: (l, 0, 0)),
                  pl.BlockSpec((None, 1, nc), lambda l: (l, 0, 0))],
        out_specs=pl.BlockSpec((None, 16, nc), lambda l: (l, 0, 0)),
        out_shape=jax.ShapeDtypeStruct((lyr, 16, nc), F32),
        compiler_params=_params(("parallel",)),
    )(craw, w_mod, b_cols)


def _adaln_bwd(craw, cs_t, dmm_cols, w_mod, name):
    lyr, d, nc = w_mod.shape

    def body(c_ref, cst_ref, dmm_ref, w_ref, gw_ref, dc_ref):
        dmm = dmm_ref[...]
        gw_ref[...] = _bdot(cst_ref[...], dmm)
        cc = c_ref[...]
        sg = jax.nn.sigmoid(cc)
        dc_ref[...] = _bdot(dmm, w_ref[...], NT) * (sg * (1.0 + cc * (1.0 - sg)))

    wspec = pl.BlockSpec((None, d, nc), lambda l: (l, 0, 0))
    return pl.pallas_call(
        body, name=name, grid=(lyr,),
        in_specs=[_full(craw.shape), _full(cs_t.shape), pl.BlockSpec((None, 16, nc), lambda l: (l, 0, 0)), wspec],
        out_specs=[wspec, pl.BlockSpec((None, 16, d), lambda l: (l, 0, 0))],
        out_shape=[jax.ShapeDtypeStruct((lyr, d, nc), F32), jax.ShapeDtypeStruct((lyr, 16, d), F32)],
        compiler_params=_params(("parallel",)),
    )(craw, cs_t, dmm_cols, w_mod)


def _rope_tables(t):
    rows = np.repeat(np.arange(t // GRID_W, dtype=np.float32), GRID_W)
    cols = np.tile(np.arange(GRID_W, dtype=np.float32), t // GRID_W)
    n = A_HEAD_DIM // 4
    freqs = (ROPE_BASE ** (-np.arange(n, dtype=np.float32) / n)).astype(np.float32)
    ang_r, ang_c = (rows[:, None] * freqs).astype(np.float32), (cols[:, None] * freqs).astype(np.float32)
    cr, sr, cc, sc = np.cos(ang_r), np.sin(ang_r), np.cos(ang_c), np.sin(ang_c)
    cos = np.concatenate([cr, cr, cc, cc] * 2, axis=-1).astype(np.float32)
    sin = np.concatenate([-sr, sr, -sc, sc] * 2, axis=-1).astype(np.float32)
    return jnp.asarray(cos), jnp.asarray(sin)


def _rope(xt, cos, sin, adjoint, name, view=None):
    t = cos.shape[0]
    w, col = (xt.shape[1], 0) if view is None else view
    tb = _pick(t, (1024, 512, 256, 128))
    rep = w // cos.shape[1]

    def body(x_ref, c_ref, s_ref, o_ref):
        xx = x_ref[...]
        cc = jnp.concatenate([c_ref[...]] * rep, axis=1) if rep > 1 else c_ref[...]
        ss = jnp.concatenate([s_ref[...]] * rep, axis=1) if rep > 1 else s_ref[...]
        low = (lax.broadcasted_iota(jnp.int32, xx.shape, 1) % 32) < 16

        def partner(v):
            return jnp.where(low, pltpu.roll(v, w - 16, 1), pltpu.roll(v, 16, 1))

        if adjoint:
            o_ref[...] = xx * cc + partner(xx * ss)
        else:
            o_ref[...] = xx * cc + partner(xx) * ss

    blk = pl.BlockSpec((tb, w), lambda i: (i, 0))
    tab = pl.BlockSpec((tb, cos.shape[1]), lambda i: (i, 0))
    return pl.pallas_call(
        body, name=name, grid=(t // tb,), in_specs=[pl.BlockSpec((tb, w), lambda i: (i, col)), tab, tab],
        out_specs=blk, out_shape=jax.ShapeDtypeStruct((t, w), F32), compiler_params=_params(("parallel",)),
    )(xt, cos, sin)


def _attn_bias():
    i = (np.arange(A_REP * WINDOW) % WINDOW)[:, None]
    j = np.arange(3 * WINDOW)[None, :]
    near = np.abs(j - WINDOW - i) <= WINDOW
    variants = [near, near & (j >= WINDOW), near & (j < 2 * WINDOW), near & (j >= WINDOW) & (j < 2 * WINDOW)]
    return jnp.asarray(np.where(np.stack(variants), 0.0, -np.inf).astype(np.float32))


def _attn_bias_spec(nb):
    rows = A_REP * WINDOW
    return pl.BlockSpec((None, rows, 3 * WINDOW),
                        lambda g, n: ((n == 0).astype(jnp.int32) + 2 * (n == nb - 1).astype(jnp.int32), 0, 0))


def _attn_probs(q, kb, kc, sink, bias):
    scale = A_HEAD_DIM ** -0.5
    s1 = _bdot(q, kb, NT) * scale + bias
    s2 = _bdot(q, kc, NT) * scale
    mx = jnp.maximum(jnp.maximum(jnp.max(s1, axis=-1, keepdims=True), jnp.max(s2, axis=-1, keepdims=True)), sink)
    p1, p2, ps = jnp.exp(s1 - mx), jnp.exp(s2 - mx), jnp.exp(sink - mx)
    inv = 1.0 / (jnp.sum(p1, axis=-1, keepdims=True) + jnp.sum(p2, axis=-1, keepdims=True) + ps)
    return p1 * inv, p2 * inv, ps * inv


def _sink_rows(sink_ref):
    return jnp.concatenate([jnp.broadcast_to(sink_ref[r], (WINDOW, 1)) for r in range(A_REP)], axis=0)


def _attn_fwd(q, kp, vp, kc, vc, sink, name):
    hq, t, dh = q.shape
    nb = t // WINDOW
    lc = kc.shape[1]
    rows = A_REP * WINDOW

    def body(q_ref, k_ref, v_ref, kc_ref, vc_ref, sink_ref, bias_ref, o_ref):
        n = pl.program_id(1)
        start = pl.multiple_of(n * WINDOW, WINDOW)
        kb, vb = k_ref[pl.ds(start, 3 * WINDOW), :], v_ref[pl.ds(start, 3 * WINDOW), :]
        p1, p2, _ = _attn_probs(q_ref[...].reshape(rows, dh), kb, kc_ref[...], _sink_rows(sink_ref), bias_ref[...])
        o_ref[...] = (_bdot(p1, vb) + _bdot(p2, vc_ref[...])).reshape(A_REP, WINDOW, dh)

    qblk = pl.BlockSpec((A_REP, WINDOW, dh), lambda g, n: (g, n, 0))
    kfull = pl.BlockSpec((None, t + 2 * WINDOW, dh), lambda g, n: (g, 0, 0))
    cfull = pl.BlockSpec((None, lc, dh), lambda g, n: (g, 0, 0))
    return pl.pallas_call(
        body, name=name, grid=(hq // A_REP, nb),
        in_specs=[qblk, kfull, kfull, cfull, cfull, pl.BlockSpec((A_REP, 1, 1), lambda g, n: (g, 0, 0)),
                  _attn_bias_spec(nb)],
        out_specs=qblk, out_shape=jax.ShapeDtypeStruct((hq, t, dh), F32),
        compiler_params=_params(("parallel", "parallel")),
    )(q, kp, vp, kc, vc, sink, _attn_bias())


def _attn_bwd(q, kp, vp, kc, vc, sink, o, do, name):
    hq, t, dh = q.shape
    nb = t // WINDOW
    lc = kc.shape[1]
    scale = A_HEAD_DIM ** -0.5
    rows = A_REP * WINDOW

    def body(q_ref, k_ref, v_ref, kc_ref, vc_ref, sink_ref, o_ref, do_ref, bias_ref,
             dq_ref, dk_ref, dv_ref, dkc_ref, dvc_ref, dsink_ref):
        n = pl.program_id(1)

        @pl.when(n == 0)
        def _():
            dk_ref[...] = jnp.zeros_like(dk_ref)
            dv_ref[...] = jnp.zeros_like(dv_ref)
            dkc_ref[...] = jnp.zeros_like(dkc_ref)
            dvc_ref[...] = jnp.zeros_like(dvc_ref)
            dsink_ref[...] = jnp.zeros_like(dsink_ref)

        start = pl.multiple_of(n * WINDOW, WINDOW)
        band = pl.ds(start, 3 * WINDOW)
        qq, kb, vb, kcc, vcc = q_ref[...].reshape(rows, dh), k_ref[band, :], v_ref[band, :], kc_ref[...], vc_ref[...]
        p1, p2, ps = _attn_probs(qq, kb, kcc, _sink_rows(sink_ref), bias_ref[...])
        dout = do_ref[...].reshape(rows, dh)
        delta = jnp.sum(dout * o_ref[...].reshape(rows, dh), axis=-1, keepdims=True)
        ds1 = p1 * (_bdot(dout, vb, NT) - delta)
        ds2 = p2 * (_bdot(dout, vcc, NT) - delta)
        dq_ref[...] = ((_bdot(ds1, kb) + _bdot(ds2, kcc)) * scale).reshape(A_REP, WINDOW, dh)
        dk_ref[band, :] += _bdot(ds1.T, qq) * scale
        dv_ref[band, :] += _bdot(p1.T, dout)
        dkc_ref[...] += _bdot(ds2.T, qq) * scale
        dvc_ref[...] += _bdot(p2.T, dout)
        dsink_ref[...] += jnp.sum((-ps * delta).reshape(A_REP, WINDOW, 1), axis=1, keepdims=True)

    qblk = pl.BlockSpec((A_REP, WINDOW, dh), lambda g, n: (g, n, 0))
    kfull = pl.BlockSpec((None, t + 2 * WINDOW, dh), lambda g, n: (g, 0, 0))
    cfull = pl.BlockSpec((None, lc, dh), lambda g, n: (g, 0, 0))
    return pl.pallas_call(
        body, name=name, grid=(hq // A_REP, nb),
        in_specs=[qblk, kfull, kfull, cfull, cfull, pl.BlockSpec((A_REP, 1, 1), lambda g, n: (g, 0, 0)), qblk, qblk,
                  _attn_bias_spec(nb)],
        out_specs=[qblk, kfull, kfull, cfull, cfull, pl.BlockSpec((A_REP, 8, 128), lambda g, n: (g, 0, 0))],
        out_shape=[jax.ShapeDtypeStruct(q.shape, F32), jax.ShapeDtypeStruct(kp.shape, F32),
                   jax.ShapeDtypeStruct(kp.shape, F32), jax.ShapeDtypeStruct(kc.shape, F32),
                   jax.ShapeDtypeStruct(kc.shape, F32), jax.ShapeDtypeStruct((hq, 8, 128), F32)],
        compiler_params=_params(("parallel", "arbitrary")),
    )(q, kp, vp, kc, vc, sink, o, do, _attn_bias())


def _gate_fwd(zg, w2, b2, name):
    m = zg.shape[0]
    n = w2.shape[1]
    tm = _pick(m, TALL_TILES)

    def body(z_ref, w_ref, b_ref, o_ref):
        o_ref[...] = jax.nn.log_sigmoid(_bdot(z_ref[...], w_ref[...]) + b_ref[...]) / B_GATE_NORM

    return pl.pallas_call(
        body, name=name, grid=(m // tm,),
        in_specs=[pl.BlockSpec((tm, zg.shape[1]), lambda i: (i, 0)), _full(w2.shape), _full(b2.shape)],
        out_specs=pl.BlockSpec((tm, n), lambda i: (i, 0)), out_shape=jax.ShapeDtypeStruct((m, n), F32),
        compiler_params=_params(("parallel",)),
    )(zg, w2, b2)


def _gate_bwd(zg, w2, b2, dla, name):
    m, rk = zg.shape
    n = w2.shape[1]
    tm = _pick(m, TALL_TILES)

    def body(z_ref, w_ref, b_ref, d_ref, dz_ref, dw_ref, db_ref):
        @pl.when(pl.program_id(0) == 0)
        def _():
            dw_ref[...] = jnp.zeros_like(dw_ref)
            db_ref[...] = jnp.zeros_like(db_ref)

        zz, ww = z_ref[...], w_ref[...]
        pre = _bdot(zz, ww) + b_ref[...]
        dpre = d_ref[...] * (1.0 / B_GATE_NORM) * jax.nn.sigmoid(-pre)
        dz_ref[...] = _bdot(dpre, ww, NT)
        dw_ref[...] += _bdot(zz.T, dpre)
        db_ref[...] += jnp.sum(dpre, axis=0, keepdims=True)

    return pl.pallas_call(
        body, name=name, grid=(m // tm,),
        in_specs=[pl.BlockSpec((tm, rk), lambda i: (i, 0)), _full(w2.shape), _full(b2.shape),
                  pl.BlockSpec((tm, n), lambda i: (i, 0))],
        out_specs=[pl.BlockSpec((tm, rk), lambda i: (i, 0)), _full(w2.shape), _full(b2.shape)],
        out_shape=[jax.ShapeDtypeStruct((m, rk), F32), jax.ShapeDtypeStruct(w2.shape, F32),
                   jax.ShapeDtypeStruct(b2.shape, F32)],
        compiler_params=_params(("arbitrary",)),
    )(zg, w2, b2, dla)


def _chunk_order(step, n_x_chunks, n_chunks, reverse):
    n_c = n_chunks - n_x_chunks
    if reverse:
        return jnp.where(step < n_c, n_chunks - 1 - step, n_chunks - 1 - step)
    return jnp.where(step < n_c, n_x_chunks + step, step - n_c)


def _tri(reverse, transpose=False):
    i = lax.broadcasted_iota(jnp.int32, (B_CHUNK, B_CHUNK), 0)
    j = lax.broadcasted_iota(jnp.int32, (B_CHUNK, B_CHUNK), 1)
    if transpose:
        i, j = j, i
    return (j >= i) if reverse else (j <= i)


def _gla_chunk(q, k, la, reverse):
    g = _dot_01(_tri(reverse), la)
    last = 0 if reverse else B_CHUNK - 1
    gl = g[last:last + 1, :]
    eg, eng, egl = jnp.exp(g), jnp.exp(-g), jnp.exp(gl - g)
    decay_col = jnp.exp(jnp.sum(la.T, axis=1, keepdims=True))
    return q * (B_DK ** -0.5) * eg, k * eng, k * egl, eg, eng, egl, decay_col


def _head_of(shape, axis, width):
    return lax.broadcasted_iota(jnp.int32, shape, axis) // width


def _gla_chunks_per_step(n_chunks, n_x_chunks):
    return _pick(int(np.gcd(n_chunks - n_x_chunks, n_x_chunks)), (4, 2, 1))


def _gla_fwd(q, k, v, la_f, la_b, n_x, name, qk_cols=(0, 0)):
    tc, wk = la_f.shape
    wv = v.shape[1]
    hh = B_HEADS
    dk, dv = wk // hh, wv // hh
    nc, nxc = tc // B_CHUNK, n_x // B_CHUNK
    sub = _gla_chunks_per_step(nc, nxc)
    rows_per_step = sub * B_CHUNK
    orders = [functools.partial(_chunk_order, n_x_chunks=nxc // sub, n_chunks=nc // sub, reverse=rev)
              for rev in (False, True)]

    def body(*refs):
        ins, outs, s_refs = refs[:8], refs[8:12], refs[12:]

        @pl.when(pl.program_id(0) == 0)
        def _():
            for s_ref in s_refs:
                s_ref[...] = jnp.zeros_like(s_ref)

        lane_head = _head_of((B_CHUNK, wk), 1, dk)
        row_head = _head_of((wk, dv), 0, dk)
        for di, reverse in enumerate((False, True)):
            q_ref, k_ref, v_ref, la_ref = ins[4 * di:4 * di + 4]
            o_ref, s_save_ref = outs[2 * di:2 * di + 2]
            s_prev = s_refs[di][...]
            for c in (reversed(range(sub)) if reverse else range(sub)):
                rows = slice(c * B_CHUNK, (c + 1) * B_CHUNK)
                qt, kt, ke, _, _, _, decay_col = _gla_chunk(q_ref[rows, :], k_ref[rows, :], la_ref[rows, :], reverse)
                ke_t = ke.T
                update = jnp.zeros_like(s_prev)
                for h in range(hh):
                    vv = v_ref[rows, h * dv:(h + 1) * dv]
                    qm = jnp.where(lane_head == h, qt, 0.0)
                    att = jnp.where(_tri(reverse), _bdot(qm, kt, NT), 0.0)
                    o_ref[rows, h * dv:(h + 1) * dv] = _bdot(att, vv) + _bdot(qm, s_prev)
                    update = jnp.where(row_head == h, _bdot(ke_t, vv), update)
                s_save_ref[c] = s_prev
                s_prev = decay_col * s_prev + update
            s_refs[di][...] = s_prev

    def blk(w, order, col=0):
        return pl.BlockSpec((rows_per_step, w), lambda s: (order(s), col))

    def sblk(order):
        return pl.BlockSpec((sub, wk, dv), lambda s: (order(s), 0, 0))

    in_specs, out_specs = [], []
    for order in orders:
        in_specs += [blk(wk, order, qk_cols[0]), blk(wk, order, qk_cols[1]), blk(wv, order), blk(wk, order)]
        out_specs += [blk(wv, order), sblk(order)]
    o_shape, s_shape = jax.ShapeDtypeStruct((tc, wv), F32), jax.ShapeDtypeStruct((nc, wk, dv), F32)
    return pl.pallas_call(
        body, name=name, grid=(nc // sub,), in_specs=in_specs, out_specs=out_specs,
        out_shape=[o_shape, s_shape, o_shape, s_shape],
        scratch_shapes=[pltpu.VMEM((wk, dv), F32)] * 2,
        compiler_params=_params(("arbitrary",)),
    )(q, k, v, la_f, q, k, v, la_b)


def _gla_bwd(q, k, v, la_f, la_b, s_f, s_b, do, n_x, name, qk_cols=(0, 0)):
    tc, wk = la_f.shape
    wv = v.shape[1]
    hh = B_HEADS
    dk, dv = wk // hh, wv // hh
    nc, nxc = tc // B_CHUNK, n_x // B_CHUNK
    sub = _gla_chunks_per_step(nc, nxc)
    rows_per_step = sub * B_CHUNK
    nb, nxb = nc // sub, nxc // sub
    orders = [functools.partial(lambda s, rev: _chunk_order(nb - 1 - s, nxb, nb, rev), rev=rev) for rev in (False, True)]

    def body(*refs):
        ins, outs, ds_refs = refs[:12], refs[12:20], refs[20:]

        @pl.when(pl.program_id(0) == 0)
        def _():
            for ds_ref in ds_refs:
                ds_ref[...] = jnp.zeros_like(ds_ref)

        lane_head = _head_of((B_CHUNK, wk), 1, dk)
        row_head = _head_of((wk, dv), 0, dk)
        for di, reverse in enumerate((False, True)):
            q_ref, k_ref, v_ref, la_ref, s_save_ref, do_ref = ins[6 * di:6 * di + 6]
            dq_ref, dk_ref, dv_ref, dla_ref = outs[4 * di:4 * di + 4]
            mask = _tri(reverse)
            last = 0 if reverse else B_CHUNK - 1
            is_last = lax.broadcasted_iota(jnp.int32, (B_CHUNK, wk), 0) == last
            ds_new = ds_refs[di][...]
            for c in (range(sub) if reverse else reversed(range(sub))):
                rows = slice(c * B_CHUNK, (c + 1) * B_CHUNK)
                la = la_ref[rows, :]
                qt, kt, ke, eg, eng, egl, decay_col = _gla_chunk(q_ref[rows, :], k_ref[rows, :], la, reverse)
                qt_t = qt.T
                s_prev = s_save_ref[c]
                dqt, dkt, dke = jnp.zeros_like(qt), jnp.zeros_like(qt), jnp.zeros_like(qt)
                ds_add = jnp.zeros_like(ds_new)
                for h in range(hh):
                    cols = slice(h * dv, (h + 1) * dv)
                    vv, dout = v_ref[rows, cols], do_ref[rows, cols]
                    mine = lane_head == h
                    qm, km = jnp.where(mine, qt, 0.0), jnp.where(mine, ke, 0.0)
                    att = jnp.where(mask, _bdot(qm, kt, NT), 0.0)
                    datt = jnp.where(mask, _bdot(dout, vv, NT), 0.0)
                    dv_ref[rows, cols] = _bdot(att.T, dout) + _bdot(km, ds_new)
                    dqt = jnp.where(mine, _bdot(datt, kt) + _bdot(dout, s_prev, NT), dqt)
                    dkt = jnp.where(mine, _bdot(datt.T, qt), dkt)
                    dke = jnp.where(mine, _bdot(vv, ds_new, NT), dke)
                    ds_add = jnp.where(row_head == h, _bdot(qt_t, dout), ds_add)
                ddecay_row = jnp.sum((ds_new * s_prev).T, axis=0, keepdims=True)
                decay_row = jnp.exp(jnp.sum(la, axis=0, keepdims=True))
                dq_ref[rows, :] = dqt * (B_DK ** -0.5) * eg
                dk_ref[rows, :] = dkt * eng + dke * egl
                dgl = jnp.sum(dke * ke, axis=0, keepdims=True) + ddecay_row * decay_row
                dg = dqt * qt - dkt * kt - dke * ke + jnp.where(is_last, dgl, 0.0)
                dla_ref[rows, :] = _dot_01(_tri(reverse, transpose=True), dg)
                ds_new = decay_col * ds_new + ds_add
            ds_refs[di][...] = ds_new

    def blk(w, order, col=0):
        return pl.BlockSpec((rows_per_step, w), lambda s: (order(s), col))

    in_specs, out_specs = [], []
    for order in orders:
        in_specs += [blk(wk, order, qk_cols[0]), blk(wk, order, qk_cols[1]), blk(wv, order), blk(wk, order),
                     pl.BlockSpec((sub, wk, dv), lambda s, order=order: (order(s), 0, 0)), blk(wv, order)]
        out_specs += [blk(wk, order), blk(wk, order), blk(wv, order), blk(wk, order)]
    k_shape, v_shape = jax.ShapeDtypeStruct((tc, wk), F32), jax.ShapeDtypeStruct((tc, wv), F32)
    return pl.pallas_call(
        body, name=name, grid=(nb,), in_specs=in_specs, out_specs=out_specs,
        out_shape=[k_shape, k_shape, v_shape, k_shape] * 2,
        scratch_shapes=[pltpu.VMEM((wk, dv), F32)] * 2,
        compiler_params=_params(("arbitrary",)),
    )(q, k, v, la_f, s_f, do, q, k, v, la_b, s_b, do)


def _gla_out_fwd(o_f, o_b, r, g, name):
    t = r.shape[0]
    dv = g.shape[1]
    hh = r.shape[1] // dv
    tb = _pick(t, (512, 256, 128, 64))

    def body(of_ref, ob_ref, r_ref, g_ref, out_ref):
        for h in range(hh):
            cols = slice(h * dv, (h + 1) * dv)
            o = of_ref[:, cols] + ob_ref[:, cols]
            rs = lax.rsqrt(jnp.mean(o * o, axis=-1, keepdims=True) + RMS_EPS)
            out_ref[:, cols] = (o * rs) * g_ref[...] * _silu(r_ref[:, cols])

    rblk = pl.BlockSpec((tb, hh * dv), lambda i: (i, 0))
    return pl.pallas_call(
        body, name=name, grid=(t // tb,), in_specs=[rblk, rblk, rblk, _full(g.shape)], out_specs=rblk,
        out_shape=jax.ShapeDtypeStruct((t, hh * dv), F32), compiler_params=_params(("parallel",)),
    )(o_f, o_b, r, g)


def _gla_out_bwd(o_f, o_b, r, g, dout, name):
    tc = o_f.shape[0]
    t = r.shape[0]
    dv = g.shape[1]
    hh = r.shape[1] // dv
    tb = _pick(int(np.gcd(t, tc)), (256, 128, 64))
    nt = t // tb

    def body(of_ref, ob_ref, r_ref, g_ref, d_ref, do_ref, dr_ref, dg_ref):
        i = pl.program_id(0)

        @pl.when(i == 0)
        def _():
            dg_ref[...] = jnp.zeros_like(dg_ref)

        @pl.when(i >= nt)
        def _():
            do_ref[...] = jnp.zeros_like(do_ref)

        @pl.when(i < nt)
        def _():
            gg = g_ref[...]
            for h in range(hh):
                cols = slice(h * dv, (h + 1) * dv)
                o = of_ref[:, cols] + ob_ref[:, cols]
                rs = lax.rsqrt(jnp.mean(o * o, axis=-1, keepdims=True) + RMS_EPS)
                nz = o * rs
                rr, dd = r_ref[:, cols], d_ref[:, cols]
                sg = jax.nn.sigmoid(rr)
                dr_ref[:, cols] = dd * nz * gg * (sg * (1.0 + rr * (1.0 - sg)))
                dy = dd * (rr * sg)
                dg_ref[...] += jnp.sum(dy * nz, axis=0, keepdims=True)
                dn = dy * gg
                do_ref[:, cols] = rs * (dn - nz * jnp.mean(dn * nz, axis=-1, keepdims=True))

    oblk = pl.BlockSpec((tb, hh * dv), lambda i: (i, 0))
    rblk = pl.BlockSpec((tb, hh * dv), lambda i: (jnp.minimum(i, nt - 1), 0))
    return pl.pallas_call(
        body, name=name, grid=(tc // tb,), in_specs=[oblk, oblk, rblk, _full(g.shape), rblk],
        out_specs=[oblk, rblk, _full(g.shape)],
        out_shape=[jax.ShapeDtypeStruct(o_f.shape, F32), jax.ShapeDtypeStruct(r.shape, F32),
                   jax.ShapeDtypeStruct(g.shape, F32)],
        compiler_params=_params(("arbitrary",)),
    )(o_f, o_b, r, g, dout)


def _pool_tile(t):
    return _pick(t, tuple(p for p in (512, 256, 128, 64) if p + 2 * POOL_PAD <= t))


def _pool_window(i, tb, t):
    return pl.multiple_of(jnp.clip(i * tb - POOL_PAD, 0, t - (tb + 2 * POOL_PAD)), 8)


def _pool_band(half, i, tb, start, adjoint):
    pos = i * tb + lax.broadcasted_iota(jnp.int32, (tb, tb + 2 * POOL_PAD), 0)
    tok = start + lax.broadcasted_iota(jnp.int32, (tb, tb + 2 * POOL_PAD), 1)
    if adjoint:
        return (tok > pos - half) & (tok <= pos + half)
    return (tok >= pos - half) & (tok < pos + half)


def _pool_count(pos, half, t):
    return (jnp.minimum(pos + half, t) - jnp.maximum(pos - half, 0)).astype(F32)


def _pool_fwd(h, w_pool, pool_scale, res, mods, km, name):
    t, d = res.shape
    ng, gw = w_pool.shape[0], w_pool.shape[1]
    tb = _pool_tile(t)

    def body(h_ref, w_ref, ps_ref, res_ref, mods_ref, out_ref, pooled_ref, ypre_ref):
        gi, i = pl.program_id(0), pl.program_id(1)
        half = jnp.left_shift(1, gi)
        start = _pool_window(i, tb, t)
        win = h_ref[pl.ds(start, tb + 2 * POOL_PAD), :]
        total = _dot_01(_pool_band(half, i, tb, start, False), win)
        pos = i * tb + lax.broadcasted_iota(jnp.int32, (tb, 1), 0)
        pooled = total / _pool_count(pos, half, t) - h_ref[pl.ds(pl.multiple_of(i * tb, tb), tb), :]
        ypre = _bdot(pooled, w_ref[...])
        pooled_ref[...] = pooled.astype(BF16)
        ypre_ref[...] = ypre
        out_ref[...] = res_ref[...] + mods_ref[0, km:km + 1, :] * (ypre * ps_ref[...])

    tile = pl.BlockSpec((tb, gw), lambda gi, i: (i, gi))
    return pl.pallas_call(
        body, name=name, grid=(ng, t // tb),
        in_specs=[pl.BlockSpec((t, gw), lambda gi, i: (0, gi)),
                  pl.BlockSpec((None, gw, gw), lambda gi, i: (gi, 0, 0)),
                  pl.BlockSpec((1, gw), lambda gi, i: (0, gi)), tile,
                  pl.BlockSpec((2, 16, gw), lambda gi, i: (0, 0, gi))],
        out_specs=[tile, tile, tile],
        out_shape=[jax.ShapeDtypeStruct((t, d), F32), jax.ShapeDtypeStruct((t, d), BF16),
                   jax.ShapeDtypeStruct((t, d), F32)],
        compiler_params=_params(("parallel", "parallel")),
    )(h, w_pool, pool_scale, res, mods)


def _pool_bwd(dxp, w_pool, pool_scale, pooled, ypre, mods, km, name):
    t, d = pooled.shape
    ng, gw = w_pool.shape[0], w_pool.shape[1]
    tb = _pool_tile(t)

    def body(dxp_ref, w_ref, ps_ref, pooled_ref, ypre_ref, mods_ref, dh_ref, dw_ref, acc_ref):
        gi, i = pl.program_id(0), pl.program_id(1)

        @pl.when(i == 0)
        def _():
            dw_ref[...] = jnp.zeros_like(dw_ref)
            acc_ref[...] = jnp.zeros_like(acc_ref)

        half = jnp.left_shift(1, gi)
        mod, ps = mods_ref[0, km:km + 1, :], ps_ref[...]
        start = _pool_window(i, tb, t)
        dwin = dxp_ref[pl.ds(start, tb + 2 * POOL_PAD), :]
        dpooled = _bdot(dwin * (mod * ps), w_ref[...], NT)
        pos = start + lax.broadcasted_iota(jnp.int32, (tb + 2 * POOL_PAD, 1), 0)
        spread = _dot_01(_pool_band(half, i, tb, start, True), dpooled / _pool_count(pos, half, t))
        dxc, yp = dxp_ref[pl.ds(pl.multiple_of(i * tb, tb), tb), :], ypre_ref[...]
        dh_ref[...] = spread - _bdot(dxc * (mod * ps), w_ref[...], NT)
        dw_ref[...] += _bdot(pooled_ref[...].astype(F32).T, dxc * (mod * ps))
        acc_ref[0:1, :] += jnp.sum(dxc * yp * mod, axis=0, keepdims=True)
        acc_ref[1:2, :] += jnp.sum(dxc * yp * ps, axis=0, keepdims=True)

    tile = pl.BlockSpec((tb, gw), lambda gi, i: (i, gi))
    wblk = pl.BlockSpec((None, gw, gw), lambda gi, i: (gi, 0, 0))
    return pl.pallas_call(
        body, name=name, grid=(ng, t // tb),
        in_specs=[pl.BlockSpec((t, gw), lambda gi, i: (0, gi)), wblk,
                  pl.BlockSpec((1, gw), lambda gi, i: (0, gi)), tile, tile,
                  pl.BlockSpec((2, 16, gw), lambda gi, i: (0, 0, gi))],
        out_specs=[tile, wblk, pl.BlockSpec((8, gw), lambda gi, i: (0, gi))],
        out_shape=[jax.ShapeDtypeStruct((t, d), F32), jax.ShapeDtypeStruct(w_pool.shape, F32),
                   jax.ShapeDtypeStruct((8, d), F32)],
        compiler_params=_params(("arbitrary", "arbitrary")),
    )(dxp, w_pool, pool_scale, pooled, ypre, mods)


def _adamw(w, g, m, v, name):
    r, c = w.shape
    tr = _pick(r, (512, 352, 256, 128, 64, 32, 16, 8))
    c1 = 1.0 / (1.0 - ADAM_B1 ** ADAM_STEP)
    c2 = 1.0 / (1.0 - ADAM_B2 ** ADAM_STEP)

    def body(w_ref, g_ref, m_ref, v_ref, d_ref, nm_ref, nv_ref):
        gg = g_ref[...]
        nm = ADAM_B1 * m_ref[...] + (1.0 - ADAM_B1) * gg
        nv = ADAM_B2 * v_ref[...] + (1.0 - ADAM_B2) * (gg * gg)
        nm_ref[...] = nm
        nv_ref[...] = nv
        d_ref[...] = -ADAM_LR * ((nm * c1) / (jnp.sqrt(nv * c2) + ADAM_EPS) + ADAM_WD * w_ref[...])

    blk = pl.BlockSpec((tr, c), lambda i: (i, 0))
    shp = jax.ShapeDtypeStruct((r, c), F32)
    return pl.pallas_call(
        body, name=name, grid=(r // tr,), in_specs=[blk] * 4, out_specs=[blk] * 3, out_shape=[shp] * 3,
        compiler_params=_params(("parallel",)),
    )(w, g, m, v)


def _heads(z, n_heads):
    m = z.shape[0]
    return z.reshape(m, n_heads, -1).transpose(1, 0, 2)


def _unheads(zh):
    return zh.transpose(1, 0, 2).reshape(zh.shape[1], -1)


def _pad_rows(a, n):
    return jnp.pad(a, ((0, 0), (n, n), (0, 0))) if a.ndim == 3 else jnp.pad(a, ((n, n), (0, 0)))


def _local_step(x, ctx, target, mods, wts, fetch, emit):
    t, d = x.shape
    l_ctx = ctx.shape[0]
    tc = t + l_ctx
    norm_g = wts["norm_g"]
    ng = lambda l, k: norm_g[l, k][None, :]
    grads = {}
    dmods = [[[None] * N_MOD for _ in range(2)] for _ in range(2)]
    dnorm = [[None] * 3 for _ in range(2)]

    def ffn_fwd(z, h, l, kbase, wi, wo, n_x, tag, nxt):
        au, act = _ffn_up(h, wi, 0, f"ffn_up_{tag}")
        wo = wo(act) if callable(wo) else wo
        outs = _mm_resid(act, wo, 0, z, mods[l], kbase + 2, 0.5, n_x, f"ffn_down_{tag}", nxt=nxt)
        return outs[0], (z, h, au, act, outs[1], wi, wo), (outs[2] if nxt is not None else None)

    def ffn_bwd(dz_new, dy, saved, l, kbase, g, n_x, tag, stage, split=False, then=None):
        z, h, au, act, y, wi, wo = saved
        dau = _ffn_down_bwd(dy, wo, 0, au, f"ffn_down_bwd_{tag}")
        dwo = _mm_tn(act, dy, BF16, f"dwo_{tag}")
        if split:
            token = emit(stage, [dwo])
            dwi_t = _mm_tn(dau, h, BF16, f"dwi_{tag}", dep=token)
            token = emit(stage + 1, [dwi_t])
        else:
            dwi_t = _mm_tn(dau, h, BF16, f"dwi_{tag}")
            token = emit(stage, [dwi_t, dwo])
        dh = _mm([(dau, wi, 0, 0)], NN, d, F32, f"dh_{tag}", tm_pref=TALL_TILES, dep=token)
        return _modulate_bwd(z, dh, dz_new, mods[l], g, kbase + 1, n_x, f"mod_bwd_{tag}", latent_only=split, then=then)

    def record(l, kbase, k_norm, g, acc_mod, acc_gate, streams):
        total = None
        for s in range(streams):
            dmods[l][s][kbase] = acc_mod[s, 0]
            dmods[l][s][kbase + 1] = acc_mod[s, 1] * g[0]
            if acc_gate is not None:
                dmods[l][s][kbase + 2] = acc_gate[s, 0]
            part = acc_mod[s, 1] * (1.0 + mods[l][s, kbase + 1])
            total = part if total is None else total + part
        dnorm[l][k_norm] = total

    xc0 = _stack_rows(x, ctx, "stack_tokens")
    wi1_0 = fetch(0, None)["wi1_0"]
    h0 = _modulate(xc0, mods[0], ng(0, 0), 0, 1, t, BF16, "mod_l0f1")
    xc1, sv_f1, hc = ffn_fwd(xc0, h0, 0, 0, wi1_0, lambda act: fetch(1, act)["wo1_0"], t, "l0f1",
                             (mods[0], ng(0, 1), 3, 4, BF16))
    w_in_t = fetch(2, hc)["w_in_t"]
    n_proj = w_in_t.shape[1]
    zall = _mm([(hc, w_in_t, 0, 0)], NT, n_proj, F32, "proj", tm_pref=TALL_TILES,
               tn_pref=(n_proj,))
    offs = np.cumsum((0,) + PROJ_SIZES)
    part = lambda i, rows=slice(None): zall[rows, offs[i]:offs[i + 1]]
    lat, con = slice(0, t), slice(t, tc)
    cos, sin = _rope_tables(t)
    qa = _heads(_rope(zall, cos, sin, False, "rope_q", view=(A_Q, int(offs[0]) // A_Q)), A_HEADS)
    ka = _heads(_rope(zall, cos, sin, False, "rope_k", view=(A_KV, int(offs[1]) // A_KV)), A_KV_HEADS)
    va = _heads(part(2, lat), A_KV_HEADS)
    kca, vca = _heads(part(1, con), A_KV_HEADS), _heads(part(2, con), A_KV_HEADS)
    kap, vap = _pad_rows(ka, WINDOW), _pad_rows(va, WINDOW)
    sink = wts["sink"].reshape(A_HEADS, 1, 1)
    o_a = _attn_fwd(qa, kap, vap, kca, vca, sink, "attn_fwd")

    vb = part(5)
    qk_cols = (int(offs[3]) // B_QK, int(offs[4]) // B_QK)
    rb = part(6, lat)
    zg = part(7)
    zg_f, zg_b = zg[:, :B_GATE_RANK], zg[:, B_GATE_RANK:]
    w2f, w2b, b2f, b2b = wts["w_a2_f"], wts["w_a2_b"], wts["b_a_f"], wts["b_a_b"]
    la_f = _gate_fwd(zg_f, w2f, b2f, "gate_f")
    la_b = _gate_fwd(zg_b, w2b, b2b, "gate_b")
    o_f, s_f, o_b, s_b = _gla_fwd(zall, zall, vb, la_f, la_b, t, "gla_fwd", qk_cols=qk_cols)
    gla_g = wts["gla_g"]
    go = _gla_out_fwd(o_f, o_b, rb, gla_g, "gla_out")
    cat = jnp.concatenate([_unheads(o_a), go], axis=-1).astype(BF16)
    big = fetch(3, cat)
    w_out, wi2_0, wo2_0 = big["w_out"], big["wi2_0"], big["wo2_0"]
    x2, y_mix0, h2 = _mm_resid(cat, w_out, 0, xc1, mods[0], 5, 1.0, t, "w_out", nxt=(mods[0], ng(0, 2), 6, 7, BF16))
    x3, sv_f2, h3 = ffn_fwd(x2, h2, 0, 6, wi2_0, wo2_0, t, "l0f2", (mods[1], ng(1, 0), 0, 1, BF16))

    big = fetch(4, x3)
    wi1_1, wo1_1, wi2_1, wo2_1 = big["wi1_1"], big["wo1_1"], big["wi2_1"], big["wo2_1"]
    x4, sv_g1, hp = ffn_fwd(x3, h3, 1, 0, wi1_1, wo1_1, t, "l1f1", (mods[1], ng(1, 1), 3, 4, F32))
    w_pool, pool_scale = big["w_pool"], wts["pool_scale"]
    x5, pooled, ypre = _pool_fwd(hp, w_pool, pool_scale, x4, mods[1], 5, "pool_fwd")
    h5 = _modulate(x5, mods[1], ng(1, 2), 6, 7, t, BF16, "mod_l1f2")
    x6, sv_g2, _ = ffn_fwd(x5, h5, 1, 6, wi2_1, wo2_1, t, "l1f2", None)

    y_of = lambda saved: saved[4]
    dx6, loss_vec, dfinal_g, dy, acc_gate = _final_loss(x6, wts["final_g"], target, (y_of(sv_g2), mods[1], 8, 0.5),
                                                        "final_loss")
    grads["final_g"] = dfinal_g[0]

    dx5, acc_mod = ffn_bwd(dx6, dy, sv_g2, 1, 6, ng(1, 2), t, "l1f2", 0)
    record(1, 6, 2, ng(1, 2), acc_mod, acc_gate, 1)
    dhp, dw_pool, acc_pool = _pool_bwd(dx5, w_pool, pool_scale, pooled, ypre, mods[1], 5, "pool_bwd")
    grads["pool_scale"] = acc_pool[0]
    dmods[1][0][5] = acc_pool[1]
    dx4, acc_mod, dy, acc_gate = _modulate_bwd(x4, dhp, dx5, mods[1], ng(1, 1), 4, t, "mod_bwd_l1mix",
                                               then=(y_of(sv_g1), mods[1], 2, 0.5))
    record(1, 3, 1, ng(1, 1), acc_mod, None, 1)
    dx3, acc_mod, dy, acc_gate_next = ffn_bwd(dx4, dy, sv_g1, 1, 0, ng(1, 0), t, "l1f1", 1,
                                              then=(y_of(sv_f2), mods[0], 8, 0.5))
    record(1, 0, 0, ng(1, 0), acc_mod, acc_gate, 1)

    dx2, acc_mod, dymix, acc_gate_mix = ffn_bwd(dx3, dy, sv_f2, 0, 6, ng(0, 2), t, "l0f2", 2,
                                                then=(y_mix0, mods[0], 5, 1.0))
    record(0, 6, 2, ng(0, 2), acc_mod, acc_gate_next, 1)
    dmods[0][0][5] = acc_gate_mix[0, 0]
    dw_out = _mm_tn(cat, dymix, BF16, "dw_out")
    dcat = _mm([(dymix, w_out, 0, 0)], NT, cat.shape[1], F32, "dcat")
    do_a = _heads(dcat[:, :A_Q], A_HEADS)
    do_full, drb, dgla_g = _gla_out_bwd(o_f, o_b, rb, gla_g, dcat[:, A_Q:], "gla_out_bwd")
    grads["gla_g"] = dgla_g[0]
    dq_f, dk_f, dv_f, dla_f, dq_b, dk_b, dv_b, dla_b = _gla_bwd(zall, zall, vb, la_f, la_b, s_f, s_b, do_full, t,
                                                                "gla_bwd", qk_cols=qk_cols)
    dzg_f, dw2f, db2f = _gate_bwd(zg_f, w2f, b2f, dla_f, "gate_bwd_f")
    dzg_b, dw2b, db2b = _gate_bwd(zg_b, w2b, b2b, dla_b, "gate_bwd_b")
    grads.update(w_a2_f=dw2f, w_a2_b=dw2b, b_a_f=db2f[0], b_a_b=db2b[0])
    dqa_r, dkap, dvap, dkca, dvca, dsink = _attn_bwd(qa, kap, vap, kca, vca, sink, o_a, do_a, "attn_bwd")
    grads["sink"] = dsink[:, 0, 0]
    dqa = _rope(_unheads(dqa_r), cos, sin, True, "rope_bwd_q")
    dka = _rope(_unheads(dkap[:, WINDOW:WINDOW + t]), cos, sin, True, "rope_bwd_k")
    dva = dvap[:, WINDOW:WINDOW + t]
    dzg = jnp.concatenate([dzg_f, dzg_b, jnp.zeros((tc, n_proj - PROJ_DIM), F32)], axis=-1)
    dzall = _assemble_dz(
        [dqa, dka, _unheads(dva), None, None, None, drb, None],
        [None, _unheads(dkca), _unheads(dvca), None, None, None, None, None],
        [None, None, None, [dq_f, dq_b], [dk_f, dk_b], [dv_f, dv_b], None, [dzg]], n_proj, "assemble_dz")
    dw_in_t = _mm_tn(dzall, hc, BF16, "dw_in")
    token = emit(3, [dw_in_t, dw_out, dw_pool])
    dhc = _mm([(dzall, w_in_t, 0, 0)], NN, d, F32, "dhc", tm_pref=TALL_TILES, dep=token)
    dxc1, acc_mod, dy, acc_gate = _modulate_bwd(xc1, dhc, dx2, mods[0], ng(0, 1), 4, t, "mod_bwd_l0mix",
                                                then=(y_of(sv_f1), mods[0], 2, 0.5))
    record(0, 3, 1, ng(0, 1), acc_mod, None, 2)
    dxc0, acc_mod = ffn_bwd(dxc1, dy, sv_f1, 0, 0, ng(0, 0), t, "l0f1", 4, split=True)
    record(0, 0, 0, ng(0, 0), acc_mod, acc_gate, 2)

    grads["norm_g"] = jnp.stack([jnp.stack(dnorm[0]), jnp.stack(dnorm[1])])
    zero = jnp.zeros((d,), F32)
    dmods_arr = jnp.stack([jnp.stack([jnp.stack([v if v is not None else zero for v in dmods[l][s]])
                                      for s in range(2)]) for l in range(2)])
    return loss_vec, dxc0, grads, dmods_arr


def _pack(parts):
    flat = jnp.concatenate([p.reshape(-1).astype(F32) for p in parts])
    pad = (-flat.shape[0]) % 128
    return jnp.pad(flat, (0, pad))[None, :]


def _unpack(rows, shapes):
    out, off = [], 0
    for s in shapes:
        n = int(np.prod(s))
        out.append(rows[:, off:off + n].reshape((rows.shape[0],) + tuple(s)))
        off += n
    return out


def _cols_to_full(g):
    g = jnp.moveaxis(g, 0, -2)
    return g.reshape(g.shape[:-2] + (-1,))


def kernel(x, c, ctx, c_ctx, w_mod, b_mod, norm_g, ffn1_wi, ffn1_wo, ffn2_wi, ffn2_wo, w_in, w_a2_f, b_a_f, w_a2_b, b_a_b, sink, gla_g, w_out, w_pool, pool_scale, final_g, loss_target, m_c_ctx, m_w_mod, m_b_mod, m_norm_g, m_ffn1_wi, m_ffn1_wo, m_ffn2_wi, m_ffn2_wo, m_w_in, m_w_a2_f, m_b_a_f, m_w_a2_b, m_b_a_b, m_sink, m_gla_g, m_w_out, m_w_pool, m_pool_scale, m_final_g, v_c_ctx, v_w_mod, v_b_mod, v_norm_g, v_ffn1_wi, v_ffn1_wo, v_ffn2_wi, v_ffn2_wo, v_w_in, v_w_a2_f, v_b_a_f, v_w_a2_b, v_b_a_b, v_sink, v_gla_g, v_w_out, v_w_pool, v_pool_scale, v_final_g):
    t, d = x.shape[1], x.shape[2]
    me = _dev_index()
    nc = w_mod.shape[2]
    ncol_in = w_in.shape[2]
    ncol_pad = -(-ncol_in // 16) * 16

    small_shapes = [(d,), norm_g.shape, pool_scale.shape, w_a2_f.shape, w_a2_b.shape]
    g1 = _gather_small(_pack([c, norm_g, pool_scale, w_a2_f, w_a2_b]), "gather_params")
    c_all, norm_g_all, pool_scale_all, w2f_all, w2b_all = _unpack(g1, small_shapes)
    wts = {
        "norm_g": _cols_to_full(norm_g_all),
        "pool_scale": _cols_to_full(pool_scale_all),
        "w_a2_f": _cols_to_full(w2f_all)[0],
        "w_a2_b": _cols_to_full(w2b_all)[0],
        "b_a_f": b_a_f, "b_a_b": b_a_b, "sink": sink[0], "gla_g": gla_g, "final_g": final_g[None, :],
    }

    craw = jnp.concatenate([c_all, c_ctx[None, :], jnp.zeros((16 - N_DEV - 1, d), F32)], axis=0)
    b_cols = lax.dynamic_slice_in_dim(b_mod, me * nc, nc, axis=1)[:, None, :]
    mm_cols = _adaln_fwd(craw, w_mod, b_cols, "adaln_fwd")
    n_cond = N_DEV + 1
    g2 = _gather_small(mm_cols[:, :n_cond].reshape(1, -1), "gather_mods").reshape(N_DEV, 2, n_cond, nc)
    mm_full = jnp.moveaxis(g2, 0, 2).reshape(2, n_cond, N_MOD, d)
    mods = jnp.stack([lax.dynamic_index_in_dim(mm_full, me, axis=1, keepdims=False), mm_full[:, N_DEV]], axis=1)
    mods = jnp.pad(mods, ((0, 0), (0, 0), (0, 16 - N_MOD), (0, 0)))

    tr = lambda w: jnp.swapaxes(w, 1, 2).astype(BF16)
    wi1_sh, wi2_sh, wo1_sh, wo2_sh = tr(ffn1_wi), tr(ffn2_wi), ffn1_wo.astype(BF16), ffn2_wo.astype(BF16)
    w_in_sh = jnp.pad(tr(w_in), ((0, 0), (0, ncol_pad - ncol_in), (0, 0)))
    groups = [
        {"wi1_0": wi1_sh[0:1]},
        {"wo1_0": wo1_sh[0:1]},
        {"w_in": w_in_sh},
        {"w_out": w_out.astype(BF16), "wi2_0": wi2_sh[0:1], "wo2_0": wo2_sh[0:1]},
        {"wi1_1": wi1_sh[1:2], "wo1_1": wo1_sh[1:2], "wi2_1": wi2_sh[1:2], "wo2_1": wo2_sh[1:2],
         "w_pool": w_pool[0].astype(BF16)},
    ]

    reach = lambda gi: NEAR_PEERS if gi == 0 else N_DEV - 1
    gathers, token = [], mods
    for gi, grp in enumerate(groups):
        lands = [_place_shard(s, me, f"gather_place_{nm}") for nm, s in grp.items()]
        gathers.append(_exchange_start(list(grp.values()), lands, True, 1 + gi, token, f"gather_start_{gi}",
                                       n_peers=reach(gi)))
        token = gathers[-1][4]
    n_proj = -(-(N_DEV * ncol_in) // 128) * 128
    forward_id = 1 + len(groups) + 6

    def fetch(gi, after):
        _, lands = _exchange_wait(gathers[gi], True, token if after is None else after, f"gather_wait_{gi}",
                                  n_peers=reach(gi))
        if gi == 0:
            rows = [s.shape[1] for s in groups[gi].values()]
            passed = _forward_start(lands, rows, forward_id, "gather_forward")
            lands = _forward_wait(passed, rows, passed[3], "gather_forward_wait")
        out = dict(zip(groups[gi].keys(), lands))
        if "w_in" in out:
            w_in_t = out.pop("w_in").reshape(1, N_DEV, ncol_pad, d)[:, :, :ncol_in].reshape(1, N_DEV * ncol_in, d)
            out["w_in_t"] = jnp.pad(w_in_t, ((0, 0), (0, n_proj - N_DEV * ncol_in), (0, 0)))
        return out

    scatters = []

    def emit(stage, arrays):
        if stage == 3:
            dw_in_t, dw_out, dw_pool = arrays
            dw_in_full = dw_in_t[:N_DEV * ncol_in].reshape(N_DEV, ncol_in, d)
            dw_in_full = jnp.pad(dw_in_full, ((0, 0), (0, ncol_pad - ncol_in), (0, 0)))
            srcs = [dw_in_full.reshape(1, N_DEV * ncol_pad, d), dw_out[None], dw_pool.astype(BF16)]
        else:
            srcs = [a[None] for a in arrays]
        lands = [lax.empty((N_DEV, s.shape[0], s.shape[1] // N_DEV, s.shape[2]), s.dtype) for s in srcs]
        scatters.append(_exchange_start(srcs, lands, False, 1 + len(groups) + stage, None, f"scatter_start_{stage}"))
        return scatters[-1][4]

    loss_vec, grad_x, grads, dmods = _local_step(x[0], ctx[0], loss_target[0], mods, wts, fetch, emit)

    def reduce_stage(stage, after):
        wholes, lands = _exchange_wait(scatters[stage], False, after, f"scatter_wait_{stage}")
        return [_sum_slots(ld, wh, me, f"sum_grad_{stage}_{i}") for i, (ld, wh) in enumerate(zip(lands, wholes))]

    (dwi2_1, dwo2_1), (dwi1_1, dwo1_1), (dwi2_0, dwo2_0), (dw_in_s, dw_out_s, dw_pool_s) = [
        reduce_stage(stage, grad_x) for stage in range(4)]
    back = lambda g: jnp.swapaxes(g, 1, 2)
    g_big = {
        "ffn2_wi": back(jnp.concatenate([dwi2_0, dwi2_1], axis=0)), "ffn2_wo": jnp.concatenate([dwo2_0, dwo2_1], axis=0),
        "w_in": back(dw_in_s[:, :ncol_in]), "w_out": dw_out_s, "w_pool": dw_pool_s[None],
    }

    order = ["c_ctx", "w_mod", "b_mod", "norm_g", "ffn1_wi", "ffn1_wo", "ffn2_wi", "ffn2_wo", "w_in", "w_a2_f", "b_a_f",
             "w_a2_b", "b_a_b", "sink", "gla_g", "w_out", "w_pool", "pool_scale", "final_g"]
    ws = dict(c_ctx=c_ctx, w_mod=w_mod, b_mod=b_mod, norm_g=norm_g, ffn1_wi=ffn1_wi, ffn1_wo=ffn1_wo, ffn2_wi=ffn2_wi,
              ffn2_wo=ffn2_wo, w_in=w_in, w_a2_f=w_a2_f, b_a_f=b_a_f, w_a2_b=w_a2_b, b_a_b=b_a_b, sink=sink, gla_g=gla_g,
              w_out=w_out, w_pool=w_pool, pool_scale=pool_scale, final_g=final_g)
    ms = dict(c_ctx=m_c_ctx, w_mod=m_w_mod, b_mod=m_b_mod, norm_g=m_norm_g, ffn1_wi=m_ffn1_wi, ffn1_wo=m_ffn1_wo,
              ffn2_wi=m_ffn2_wi, ffn2_wo=m_ffn2_wo, w_in=m_w_in, w_a2_f=m_w_a2_f, b_a_f=m_b_a_f, w_a2_b=m_w_a2_b,
              b_a_b=m_b_a_b, sink=m_sink, gla_g=m_gla_g, w_out=m_w_out, w_pool=m_w_pool, pool_scale=m_pool_scale,
              final_g=m_final_g)
    vs = dict(c_ctx=v_c_ctx, w_mod=v_w_mod, b_mod=v_b_mod, norm_g=v_norm_g, ffn1_wi=v_ffn1_wi, ffn1_wo=v_ffn1_wo,
              ffn2_wi=v_ffn2_wi, ffn2_wo=v_ffn2_wo, w_in=v_w_in, w_a2_f=v_w_a2_f, b_a_f=v_b_a_f, w_a2_b=v_w_a2_b,
              b_a_b=v_b_a_b, sink=v_sink, gla_g=v_gla_g, w_out=v_w_out, w_pool=v_w_pool, pool_scale=v_pool_scale,
              final_g=v_final_g)
    early, late = ["ffn2_wi", "ffn2_wo", "w_out", "w_in", "w_pool"], ["ffn1_wi", "ffn1_wo"]
    big = early + ["w_mod"] + late
    delta, new_m, new_v = {}, {}, {}
    g_all = dict(g_big)

    def adamw_big(nm):
        shp = ws[nm].shape
        two_d = lambda a: a.reshape(-1, shp[-1])
        dl, nm_, nv_ = _adamw(two_d(ws[nm]), two_d(g_all[nm]), two_d(ms[nm]), two_d(vs[nm]), f"adamw_{nm}")
        delta[nm], new_m[nm], new_v[nm] = dl.reshape(shp), nm_.reshape(shp), nv_.reshape(shp)

    for nm in early:
        adamw_big(nm)

    small_g = [dmods[:, :, :N_MOD].reshape(2, 2, N_MOD * d), grads["norm_g"], grads["pool_scale"], grads["final_g"],
               grads["b_a_f"], grads["b_a_b"], grads["sink"], grads["gla_g"], grads["w_a2_f"], grads["w_a2_b"],
               loss_vec]
    small_g_shapes = [a.shape for a in small_g]
    g3 = _gather_small(_pack(small_g), "gather_small_grads", dep=delta["w_out"])
    total = _sum_rows8(g3, "sum_small_grads")
    dmm_all = _unpack(g3, small_g_shapes[:1])[0]
    (dmm_sum, dnorm_g, dpool_scale, dfinal_g, db_a_f, db_a_b, dsink, dgla_g, dw_a2_f, dw_a2_b, loss_all) = [
        a[0] for a in _unpack(total, small_g_shapes)]
    loss = jnp.sum(loss_all)
    dmm_rows = jnp.concatenate([dmm_all[:, :, 0].transpose(1, 0, 2), dmm_sum[:, 1][:, None, :],
                                jnp.zeros((2, 16 - N_DEV - 1, N_MOD * d), F32)], axis=1)
    grad_b_mod = dmm_sum[:, 0] + dmm_sum[:, 1]
    dmm_cols = lax.dynamic_slice_in_dim(dmm_rows, me * nc, nc, axis=2)
    cs_t = jnp.transpose(_silu(craw)).astype(BF16)
    grad_w_mod, dcraw = _adaln_bwd(craw, cs_t, dmm_cols, w_mod, "adaln_bwd")
    g4 = _gather_small((dcraw[0, N_DEV] + dcraw[1, N_DEV])[None, :], "gather_c_ctx_grad")
    grad_c_ctx = _sum_rows8(g4, "sum_c_ctx_grad")[0]

    col = lambda v, n: lax.dynamic_slice_in_dim(v, me * n, n, axis=v.ndim - 1)
    g_small = {
        "c_ctx": grad_c_ctx, "b_mod": grad_b_mod, "norm_g": col(dnorm_g, norm_g.shape[2]),
        "w_a2_f": col(dw_a2_f, w_a2_f.shape[2])[None], "b_a_f": db_a_f[None], "w_a2_b": col(dw_a2_b, w_a2_b.shape[2])[None],
        "b_a_b": db_a_b[None], "sink": dsink[None], "gla_g": dgla_g[None], "pool_scale": col(dpool_scale, pool_scale.shape[1])[None],
        "final_g": dfinal_g,
    }
    g_all.update(g_small, w_mod=grad_w_mod)
    adamw_big("w_mod")
    rest = [nm for nm in order if nm not in big]
    rest_shapes = [ws[nm].shape for nm in rest]
    packed = [_pack([d_[nm].reshape(ws[nm].shape) for nm in rest]).reshape(-1, 128) for d_ in (ws, g_all, ms, vs)]
    pad_rows = (-packed[0].shape[0]) % 512
    packed = [jnp.pad(p, ((0, pad_rows), (0, 0))) for p in packed]
    outs = _adamw(*packed, "adamw_small")
    for dst, arr in zip((delta, new_m, new_v), outs):
        for nm, val in zip(rest, _unpack(arr.reshape(1, -1), rest_shapes)):
            dst[nm] = val[0]

    (dwo1_0,), (dwi1_0,) = reduce_stage(4, outs[0]), reduce_stage(5, outs[0])
    g_all["ffn1_wi"] = back(jnp.concatenate([dwi1_0, dwi1_1], axis=0))
    g_all["ffn1_wo"] = jnp.concatenate([dwo1_0, dwo1_1], axis=0)
    for nm in late:
        adamw_big(nm)
    g_all = {nm: g_all[nm].reshape(ws[nm].shape) for nm in order}

    return (loss, grad_x[None], *[g_all[nm] for nm in order], *[delta[nm] for nm in order],
            *[new_m[nm] for nm in order], *[new_v[nm] for nm in order])
```

```python
import functools

import numpy as np
import jax
import jax.numpy as jnp
from jax import lax
from jax.experimental import pallas as pl
from jax.experimental.pallas import tpu as pltpu

F32 = jnp.float32
BF16 = jnp.bfloat16
MESH = pl.DeviceIdType.MESH

N_DEV = 8
RMS_EPS = 1e-6
N_MOD = 9
GRID_W = 64
A_HEADS, A_KV_HEADS, A_HEAD_DIM = 8, 2, 64
A_REP = A_HEADS // A_KV_HEADS
WINDOW = 128
ROPE_BASE = 10000.0
B_HEADS, B_DK, B_DV = 4, 64, 128
B_GATE_RANK = 16
B_GATE_NORM = 16.0
B_CHUNK = 64
POOL_WINDOWS = (2, 4, 8, 16)
POOL_PAD = 8
A_Q = A_HEADS * A_HEAD_DIM
A_KV = A_KV_HEADS * A_HEAD_DIM
B_QK = B_HEADS * B_DK
B_V = B_HEADS * B_DV
PROJ_SIZES = (A_Q, A_KV, A_KV, B_QK, B_QK, B_V, B_V, 2 * B_GATE_RANK)
PROJ_DIM = sum(PROJ_SIZES)
ADAM_LR, ADAM_B1, ADAM_B2, ADAM_EPS, ADAM_WD, ADAM_STEP = 0.001, 0.9, 0.999, 1e-08, 0.01, 10

VMEM_LIMIT = 56 * 1024 * 1024
ROW_TILES = (512, 544, 256, 128, 64, 32, 16, 8)
TALL_TILES = (1024, 1088) + ROW_TILES

NN = ((1,), (0,))
NT = ((1,), (1,))
TN = ((0,), (0,))


def _dot(a, b, dims=NN, prec=None):
    return lax.dot_general(a, b, (dims, ((), ())), precision=prec, preferred_element_type=F32)


def _bdot(a, b, dims=NN):
    return _dot(a.astype(BF16), b.astype(BF16), dims)


def _dot_01(sel, x):
    hi = x.astype(BF16)
    rest = x - hi.astype(F32)
    mid = rest.astype(BF16)
    lo = (rest - mid.astype(F32)).astype(BF16)
    sel = sel.astype(BF16)
    return _dot(sel, hi) + _dot(sel, mid) + _dot(sel, lo)


def _params(sem=None, **kw):
    return pltpu.CompilerParams(dimension_semantics=sem, vmem_limit_bytes=VMEM_LIMIT, **kw)


def _silu(a):
    return a * jax.nn.sigmoid(a)


def _pick(n, prefs):
    for p in prefs:
        if n % p == 0:
            return p
    return n


def _full(shape):
    nd = len(shape)
    return pl.BlockSpec(shape, lambda *_: (0,) * nd)


def _peers():
    x, y, c = lax.axis_index("x"), lax.axis_index("y"), lax.axis_index("c")
    return x, y, c


def _dev_index():
    x, y, c = _peers()
    return 4 * x + 2 * y + c


def _others(x, y, c):
    return [(x, y, 1 - c), (1 - x, y, c), (x, 1 - y, c), (1 - x, 1 - y, c),
            (1 - x, y, 1 - c), (x, 1 - y, 1 - c), (1 - x, 1 - y, 1 - c)]


def _index_of(dev):
    return 4 * dev[0] + 2 * dev[1] + dev[2]


def _exchange_refs(gather, shapes, srcs, lands, a, me, to):
    if gather:
        r = shapes[a][1]
        return srcs[a], lands[a].at[:, pl.ds(_index_of(me) * r, r), :]
    r = shapes[a][1] // N_DEV
    return srcs[a].at[:, pl.ds(_index_of(to) * r, r), :], lands[a].at[_index_of(me)]


HBM_SPEC = pl.BlockSpec(memory_space=pltpu.HBM)
SEM_SPEC = pl.BlockSpec(memory_space=pltpu.SEMAPHORE)
EFFECT = pltpu.SideEffectType.DATAFLOW_SIDE_EFFECTING


NEAR_PEERS = 4


def _exchange_start(srcs, lands, gather, collective_id, dep, name, n_peers=N_DEV - 1):
    n = len(srcs)
    shapes = [s.shape for s in srcs]
    deps = [] if dep is None else [dep]

    def body(*refs):
        src_refs, land_refs = refs[:n], refs[n:2 * n]
        send_sems, recv_sems = refs[2 * n + len(deps)], refs[2 * n + len(deps) + 1]
        token = refs[-1]
        x, y, c = _peers()
        others = _others(x, y, c)[:n_peers]
        barrier = pltpu.get_barrier_semaphore()
        for peer in others:
            pl.semaphore_signal(barrier, inc=1, device_id=peer, device_id_type=MESH)
        pl.semaphore_wait(barrier, len(others))
        for a in range(n):
            for k, to in enumerate(others):
                src, dst = _exchange_refs(gather, shapes, src_refs, land_refs, a, (x, y, c), to)
                pltpu.make_async_remote_copy(src_ref=src, dst_ref=dst, send_sem=send_sems.at[7 * a + k],
                                             recv_sem=recv_sems.at[7 * a + k], device_id=to, device_id_type=MESH).start()
        token[...] = jnp.zeros_like(token)

    outs = pl.pallas_call(
        body, name=name,
        out_shape=(pltpu.SemaphoreType.DMA((7 * n,)), pltpu.SemaphoreType.DMA((7 * n,)),
                   *[pltpu.HBM(s.shape, s.dtype) for s in srcs], *[pltpu.HBM(l.shape, l.dtype) for l in lands],
                   jax.ShapeDtypeStruct((8, 128), F32)),
        in_specs=[HBM_SPEC] * (2 * n) + [pl.BlockSpec(memory_space=pl.ANY)] * len(deps),
        out_specs=(SEM_SPEC, SEM_SPEC, *[HBM_SPEC] * (2 * n), pl.BlockSpec(memory_space=pltpu.VMEM)),
        input_output_aliases={i: 2 + i for i in range(2 * n)},
        compiler_params=pltpu.CompilerParams(has_side_effects=EFFECT, collective_id=collective_id),
    )(*[pltpu.with_memory_space_constraint(s, pltpu.HBM) for s in srcs],
      *[pltpu.with_memory_space_constraint(l, pltpu.HBM) for l in lands], *deps)
    return outs[0], outs[1], list(outs[2:2 + n]), list(outs[2 + n:2 + 2 * n]), outs[-1]


def _exchange_wait(started, gather, after, name, n_peers=N_DEV - 1):
    send_sems, recv_sems, srcs, lands, _ = started
    n = len(srcs)
    shapes = [s.shape for s in srcs]

    def body(*refs):
        src_refs, land_refs = refs[:n], refs[n:2 * n]
        send_sems, recv_sems = refs[2 * n], refs[2 * n + 1]
        x, y, c = _peers()
        for a in range(n):
            for k, peer in enumerate(_others(x, y, c)[:n_peers]):
                src, _ = _exchange_refs(gather, shapes, src_refs, land_refs, a, (x, y, c), peer)
                _, dst = _exchange_refs(gather, shapes, src_refs, land_refs, a, peer, (x, y, c))
                copy = pltpu.make_async_remote_copy(src_ref=src, dst_ref=dst, send_sem=send_sems.at[7 * a + k],
                                                    recv_sem=recv_sems.at[7 * a + k], device_id=peer, device_id_type=MESH)
                copy.wait_send()
                copy.wait_recv()

    outs = pl.pallas_call(
        body, name=name,
        out_shape=(*[pltpu.HBM(s.shape, s.dtype) for s in srcs], *[pltpu.HBM(l.shape, l.dtype) for l in lands]),
        in_specs=[HBM_SPEC] * (2 * n) + [SEM_SPEC, SEM_SPEC, pl.BlockSpec(memory_space=pl.ANY)],
        out_specs=tuple([HBM_SPEC] * (2 * n)),
        input_output_aliases={i: i for i in range(2 * n)},
        compiler_params=pltpu.CompilerParams(has_side_effects=EFFECT),
    )(*srcs, *lands, send_sems, recv_sems, after)
    return list(outs[:n]), list(outs[n:])


def _forward_refs(land_refs, rows, a, others, j, received):
    origin = others[j + 3] if received else others[j]
    return land_refs[a].at[:, pl.ds(_index_of(origin) * rows[a], rows[a]), :]


def _forward_start(lands, rows, collective_id, name):
    n = len(lands)

    def body(*refs):
        land_refs, send_sems, recv_sems, token = refs[:n], refs[n], refs[n + 1], refs[-1]
        x, y, c = _peers()
        others = _others(x, y, c)
        barrier = pltpu.get_barrier_semaphore()
        pl.semaphore_signal(barrier, inc=1, device_id=others[0], device_id_type=MESH)
        pl.semaphore_wait(barrier, 1)
        for a in range(n):
            for j in (1, 2, 3):
                blk = _forward_refs(land_refs, rows, a, others, j, False)
                pltpu.make_async_remote_copy(src_ref=blk, dst_ref=blk, send_sem=send_sems.at[3 * a + j - 1],
                                             recv_sem=recv_sems.at[3 * a + j - 1], device_id=others[0],
                                             device_id_type=MESH).start()
        token[...] = jnp.zeros_like(token)

    outs = pl.pallas_call(
        body, name=name,
        out_shape=(pltpu.SemaphoreType.DMA((3 * n,)), pltpu.SemaphoreType.DMA((3 * n,)),
                   *[pltpu.HBM(l.shape, l.dtype) for l in lands], jax.ShapeDtypeStruct((8, 128), F32)),
        in_specs=[HBM_SPEC] * n,
        out_specs=(SEM_SPEC, SEM_SPEC, *[HBM_SPEC] * n, pl.BlockSpec(memory_space=pltpu.VMEM)),
        input_output_aliases={i: 2 + i for i in range(n)},
        compiler_params=pltpu.CompilerParams(has_side_effects=EFFECT, collective_id=collective_id),
    )(*[pltpu.with_memory_space_constraint(l, pltpu.HBM) for l in lands])
    return outs[0], outs[1], list(outs[2:2 + n]), outs[-1]


def _forward_wait(started, rows, after, name):
    send_sems, recv_sems, lands, _ = started
    n = len(lands)

    def body(*refs):
        land_refs, send_sems, recv_sems = refs[:n], refs[n], refs[n + 1]
        x, y, c = _peers()
        others = _others(x, y, c)
        for a in range(n):
            for j in (1, 2, 3):
                copy = pltpu.make_async_remote_copy(
                    src_ref=_forward_refs(land_refs, rows, a, others, j, False),
                    dst_ref=_forward_refs(land_refs, rows, a, others, j, True), send_sem=send_sems.at[3 * a + j - 1],
                    recv_sem=recv_sems.at[3 * a + j - 1], device_id=others[0], device_id_type=MESH)
                copy.wait_send()
                copy.wait_recv()

    outs = pl.pallas_call(
        body, name=name, out_shape=tuple(pltpu.HBM(l.shape, l.dtype) for l in lands),
        in_specs=[HBM_SPEC] * n + [SEM_SPEC, SEM_SPEC, pl.BlockSpec(memory_space=pl.ANY)],
        out_specs=tuple([HBM_SPEC] * n), input_output_aliases={i: i for i in range(n)},
        compiler_params=pltpu.CompilerParams(has_side_effects=EFFECT),
    )(*lands, send_sems, recv_sems, after)
    return list(outs)


def _place_shard(shard, me, name):
    a_, r, c = shard.shape
    tr = _pick(r, (352, 304, 256, 128, 64, 32, 16, 8))
    nr = r // tr

    def body(me_ref, in_ref, out_ref):
        out_ref[...] = in_ref[...]

    return pl.pallas_call(
        body, name=name,
        grid_spec=pltpu.PrefetchScalarGridSpec(
            num_scalar_prefetch=1, grid=(a_, nr),
            in_specs=[pl.BlockSpec((None, tr, c), lambda i, j, me_ref: (i, j, 0))],
            out_specs=pl.BlockSpec((None, tr, c), lambda i, j, me_ref: (i, me_ref[0] * nr + j, 0))),
        out_shape=jax.ShapeDtypeStruct((a_, N_DEV * r, c), shard.dtype),
        compiler_params=_params(("parallel", "parallel")),
    )(me.reshape(1).astype(jnp.int32), shard)


def _sum_slots(land, whole, me, name):
    _, a_, r, c = land.shape
    tr = _pick(r, (352, 256, 128, 64, 32, 16, 8))
    nr = r // tr

    def body(me_ref, land_ref, own_ref, out_ref):
        acc = None
        for s in range(N_DEV):
            part = jnp.where(me_ref[0] == s, own_ref[...], land_ref[s]).astype(F32)
            acc = part if acc is None else acc + part
        out_ref[...] = acc

    return pl.pallas_call(
        body, name=name,
        grid_spec=pltpu.PrefetchScalarGridSpec(
            num_scalar_prefetch=1, grid=(a_, nr),
            in_specs=[pl.BlockSpec((N_DEV, None, tr, c), lambda i, j, me_ref: (0, i, j, 0)),
                      pl.BlockSpec((None, tr, c), lambda i, j, me_ref: (i, me_ref[0] * nr + j, 0))],
            out_specs=pl.BlockSpec((None, tr, c), lambda i, j, me_ref: (i, j, 0))),
        out_shape=jax.ShapeDtypeStruct((a_, r, c), F32),
        compiler_params=_params(("parallel", "parallel")),
    )(me.reshape(1).astype(jnp.int32), land, whole)


def _gather_small(vec, name, dep=None):
    p = vec.shape[1]
    pp = -(-p // 1024) * 1024
    blk = jnp.pad(vec, ((0, 0), (0, pp - p))).reshape(8, pp // 8)
    deps = [] if dep is None else [dep]

    def body(in_ref, *rest):
        out_ref, send_sems, recv_sems = rest[-3:]
        x, y, c = _peers()
        me = 4 * x + 2 * y + c
        others = [(x, y, 1 - c), (1 - x, y, c), (x, 1 - y, c), (1 - x, 1 - y, c),
                  (1 - x, y, 1 - c), (x, 1 - y, 1 - c), (1 - x, 1 - y, 1 - c)]

        def rows(idx):
            return out_ref.at[pl.ds(pl.multiple_of(idx * 8, 8), 8), :]

        out_ref[pl.ds(pl.multiple_of(me * 8, 8), 8), :] = in_ref[...]

        def copy(k, dev, slot):
            return pltpu.make_async_remote_copy(
                src_ref=in_ref, dst_ref=rows(slot), send_sem=send_sems.at[k], recv_sem=recv_sems.at[k],
                device_id=dev, device_id_type=MESH)

        sends = [copy(k, dev, me) for k, dev in enumerate(others)]
        for cp in sends:
            cp.start()
        for k, dev in enumerate(others):
            copy(k, dev, 4 * dev[0] + 2 * dev[1] + dev[2]).wait_recv()
        for cp in sends:
            cp.wait_send()

    vm = pl.BlockSpec(memory_space=pltpu.VMEM)
    out = pl.pallas_call(
        body, name=name, out_shape=jax.ShapeDtypeStruct((8 * N_DEV, pp // 8), F32),
        in_specs=[vm] + [pl.BlockSpec(memory_space=pl.ANY)] * len(deps), out_specs=vm,
        scratch_shapes=[pltpu.SemaphoreType.DMA((7,)), pltpu.SemaphoreType.DMA((7,))],
        compiler_params=pltpu.CompilerParams(has_side_effects=True, vmem_limit_bytes=VMEM_LIMIT),
    )(blk, *deps)
    return out.reshape(N_DEV, pp)[:, :p]


def _sum_rows8(g, name):
    p = g.shape[1]

    def body(in_ref, out_ref):
        acc = in_ref[0:1, :]
        for s in range(1, N_DEV):
            acc = acc + in_ref[s:s + 1, :]
        out_ref[...] = acc

    return pl.pallas_call(body, name=name, out_shape=jax.ShapeDtypeStruct((1, p), F32),
                          compiler_params=_params())(g)


def _sel_row(mods_ref, is_ctx, k):
    return jnp.where(is_ctx, mods_ref[1, k:k + 1, :], mods_ref[0, k:k + 1, :])


def _stream_tile(m, n_x):
    span = n_x if m == n_x else int(np.gcd(n_x, m - n_x))
    return _pick(span, (512, 256, 128, 64, 32, 16, 8))


def _modulate(z, mods, g, ks, kc, n_x, out_dtype, name):
    m, d = z.shape
    tm = _stream_tile(m, n_x)

    def body(z_ref, mods_ref, g_ref, h_ref):
        is_ctx = pl.program_id(0) * tm >= n_x
        zz = z_ref[...]
        r = lax.rsqrt(jnp.mean(zz * zz, axis=-1, keepdims=True) + RMS_EPS)
        shift, scale = _sel_row(mods_ref, is_ctx, ks), _sel_row(mods_ref, is_ctx, kc)
        h_ref[...] = ((zz * r) * g_ref[...] * (1.0 + scale) + shift).astype(out_dtype)

    return pl.pallas_call(
        body, name=name, grid=(m // tm,),
        in_specs=[pl.BlockSpec((tm, d), lambda i: (i, 0)), _full(mods.shape), _full(g.shape)],
        out_specs=pl.BlockSpec((tm, d), lambda i: (i, 0)),
        out_shape=jax.ShapeDtypeStruct((m, d), out_dtype),
        compiler_params=_params(("parallel",)),
    )(z, mods, g)


def _gate_bwd_rows(dx, y, gate, coef):
    return (coef * gate * dx).astype(BF16), jnp.sum(coef * y * dx, axis=0, keepdims=True)


def _modulate_bwd(z, dh, dres, mods, g, kc, n_x, name, latent_only=False, then=None):
    m, d = z.shape
    tm = _stream_tile(m, n_x)
    first_ctx = n_x // tm
    res_blocks = dres.shape[0] // tm
    out_blocks = (n_x if latent_only else m) // tm
    extra = [] if then is None else [then[0], then[1]]

    def body(z_ref, dh_ref, dres_ref, mods_ref, g_ref, *rest):
        i = pl.program_id(0)
        is_ctx = i * tm >= n_x
        dx_ref, acc_ref = rest[len(extra)], rest[len(extra) + 1]

        @pl.when((i == 0) | (i == first_ctx))
        def _():
            acc_ref[...] = jnp.zeros_like(acc_ref)
            if then is not None:
                rest[-1][...] = jnp.zeros_like(rest[-1])

        zz, dhh = z_ref[...], dh_ref[...]
        r = lax.rsqrt(jnp.mean(zz * zz, axis=-1, keepdims=True) + RMS_EPS)
        nz = zz * r
        gain = g_ref[...] * (1.0 + _sel_row(mods_ref, is_ctx, kc))
        dn = dhh * gain
        dz = r * (dn - nz * jnp.mean(dn * nz, axis=-1, keepdims=True))
        dx = jnp.where(i < res_blocks, dres_ref[...], 0.0) + dz

        @pl.when(i < out_blocks)
        def _():
            dx_ref[...] = dx

        acc_ref[0:1, :] += jnp.sum(dhh, axis=0, keepdims=True)
        acc_ref[1:2, :] += jnp.sum(dhh * nz, axis=0, keepdims=True)
        if then is not None:
            y_ref, tmods_ref, dy_ref, gate_acc_ref = rest[0], rest[1], rest[-2], rest[-1]
            dy, part = _gate_bwd_rows(dx, y_ref[...], _sel_row(tmods_ref, is_ctx, then[2]), then[3])
            dy_ref[...] = dy
            gate_acc_ref[0:1, :] += part

    row = pl.BlockSpec((tm, d), lambda i: (i, 0))
    acc_spec = pl.BlockSpec((None, 8, d), lambda i: ((i * tm >= n_x).astype(jnp.int32), 0, 0))
    out_specs = [pl.BlockSpec((tm, d), lambda i: (jnp.minimum(i, out_blocks - 1), 0)), acc_spec]
    out_shape = [jax.ShapeDtypeStruct((out_blocks * tm, d), F32), jax.ShapeDtypeStruct((2, 8, d), F32)]
    in_specs = [row, row, pl.BlockSpec((tm, d), lambda i: (jnp.minimum(i, res_blocks - 1), 0)),
                _full(mods.shape), _full(g.shape)]
    if then is not None:
        in_specs += [row, _full(then[1].shape)]
        out_specs += [row, acc_spec]
        out_shape += [jax.ShapeDtypeStruct((m, d), BF16), jax.ShapeDtypeStruct((2, 8, d), F32)]
    return pl.pallas_call(
        body, name=name, grid=(m // tm,), in_specs=in_specs, out_specs=out_specs, out_shape=out_shape,
        compiler_params=_params(("arbitrary",)),
    )(z, dh, dres, mods, g, *extra)


def _ffn_up(h, wi_t, layer, name):
    m, d = h.shape
    f = wi_t.shape[1] // 2
    tm = _pick(m, ROW_TILES)

    def body(h_ref, w_ref, jac_ref, act_ref):
        hh = h_ref[...]
        a = _dot(hh, w_ref[0:f, :], NT)
        u = _dot(hh, w_ref[f:2 * f, :], NT)
        sg = jax.nn.sigmoid(a)
        s = a * sg
        jac_ref[:, 0:f] = (u * (sg * (1.0 + a * (1.0 - sg)))).astype(BF16)
        jac_ref[:, f:2 * f] = s.astype(BF16)
        act_ref[...] = (s * u).astype(BF16)

    return pl.pallas_call(
        body, name=name, grid=(m // tm,),
        in_specs=[pl.BlockSpec((tm, d), lambda i: (i, 0)),
                  pl.BlockSpec((None, 2 * f, d), lambda i: (layer, 0, 0))],
        out_specs=[pl.BlockSpec((tm, 2 * f), lambda i: (i, 0)), pl.BlockSpec((tm, f), lambda i: (i, 0))],
        out_shape=[jax.ShapeDtypeStruct((m, 2 * f), BF16), jax.ShapeDtypeStruct((m, f), BF16)],
        compiler_params=_params(("parallel",)),
    )(h, wi_t)


def _mm_resid(a, b, layer, res, mods, km, coef, n_x, name, nxt=None):
    m, k = a.shape
    n = b.shape[2]
    tm = _pick(m, (512, 256, 128, 64, 32, 16, 8))
    tn = n if nxt is not None else _pick(n, (1024, 512, 256, 128))
    extra = [] if nxt is None else [nxt[0], nxt[1]]

    def body(a_ref, b_ref, res_ref, mods_ref, *rest):
        is_ctx = pl.program_id(1) * tm >= n_x
        y = _dot(a_ref[...], b_ref[...])
        new = res_ref[...] + coef * _sel_row(mods_ref, is_ctx, km) * y
        if nxt is None:
            out_ref, y_ref = rest
        else:
            nmods_ref, g_ref, out_ref, y_ref, h_ref = rest
            r = lax.rsqrt(jnp.mean(new * new, axis=-1, keepdims=True) + RMS_EPS)
            shift, scale = _sel_row(nmods_ref, is_ctx, nxt[2]), _sel_row(nmods_ref, is_ctx, nxt[3])
            h_ref[...] = ((new * r) * g_ref[...] * (1.0 + scale) + shift).astype(nxt[4])
        y_ref[...] = y.astype(BF16)
        out_ref[...] = new

    tile = pl.BlockSpec((tm, tn), lambda j, i: (i, j))
    outs = [jax.ShapeDtypeStruct((m, n), F32), jax.ShapeDtypeStruct((m, n), BF16)]
    if nxt is not None:
        outs.append(jax.ShapeDtypeStruct((m, n), nxt[4]))
    return pl.pallas_call(
        body, name=name, grid=(n // tn, m // tm),
        in_specs=[pl.BlockSpec((tm, k), lambda j, i: (i, 0)),
                  pl.BlockSpec((None, k, tn), lambda j, i: (layer, 0, j)),
                  tile, pl.BlockSpec((2, 16, tn), lambda j, i: (0, 0, j))] + [_full(e.shape) for e in extra],
        out_specs=[tile] * len(outs), out_shape=outs,
        compiler_params=_params(("parallel", "parallel")),
    )(a, b, res, mods, *extra)


def _ffn_down_bwd(dy, wo, layer, au, name):
    m, d = dy.shape
    f = wo.shape[1]
    tm = _pick(m, ROW_TILES)

    def body(dy_ref, wo_ref, au_ref, dau_ref):
        dact = _dot(dy_ref[...], wo_ref[...], NT)
        dau_ref[:, 0:f] = (dact * au_ref[:, 0:f].astype(F32)).astype(BF16)
        dau_ref[:, f:2 * f] = (dact * au_ref[:, f:2 * f].astype(F32)).astype(BF16)

    wide = pl.BlockSpec((tm, 2 * f), lambda i: (i, 0))
    return pl.pallas_call(
        body, name=name, grid=(m // tm,),
        in_specs=[pl.BlockSpec((tm, d), lambda i: (i, 0)), pl.BlockSpec((None, f, d), lambda i: (layer, 0, 0)), wide],
        out_specs=wide, out_shape=jax.ShapeDtypeStruct((m, 2 * f), BF16),
        compiler_params=_params(("parallel",)),
    )(dy, wo, au)


def _mm(terms, dims, n, out_dtype, name, tm_pref=(512, 256, 128, 64, 32, 16, 8), tn_pref=(512, 256, 128), dep=None):
    m = terms[0][0].shape[0]
    tm = _pick(m, tm_pref)
    tn = _pick(n, tn_pref)
    nt = len(terms)
    deps = [] if dep is None else [dep]

    def body(*refs):
        out_ref = refs[-1]
        acc = None
        for t in range(nt):
            part = _dot(refs[2 * t][...].astype(BF16), refs[2 * t + 1][...].astype(BF16), dims)
            acc = part if acc is None else acc + part
        out_ref[...] = acc.astype(out_dtype)

    in_specs, args = [], []
    for a, b, layer, rb in terms:
        k = a.shape[1]
        in_specs.append(pl.BlockSpec((tm, k), lambda j, i: (i, 0)))
        if dims == NN:
            in_specs.append(pl.BlockSpec((None, k, tn), lambda j, i, layer=layer, rb=rb: (layer, rb, j)))
        else:
            nb = n // tn
            in_specs.append(pl.BlockSpec((None, tn, k), lambda j, i, layer=layer, rb=rb, nb=nb: (layer, rb * nb + j, 0)))
        args += [a, b]
    return pl.pallas_call(
        body, name=name, grid=(n // tn, m // tm), in_specs=in_specs + [pl.BlockSpec(memory_space=pl.ANY)] * len(deps),
        out_specs=pl.BlockSpec((tm, tn), lambda j, i: (i, j)),
        out_shape=jax.ShapeDtypeStruct((m, n), out_dtype),
        compiler_params=_params(("parallel", "parallel")),
    )(*args, *deps)


def _mm_tn(a, b, out_dtype, name, dep=None):
    t = a.shape[0]
    m, n = a.shape[1], b.shape[1]
    tm = _pick(m, (1408, 2432, 1024, 512, 256, 128))
    tn = _pick(n, (1024, 512, 256, 128))
    tk = _pick(t, TALL_TILES)
    deps = [] if dep is None else [dep]

    def body(a_ref, b_ref, *rest):
        out_ref, acc_ref = rest[-2:]
        kk = pl.program_id(2)

        @pl.when(kk == 0)
        def _():
            acc_ref[...] = jnp.zeros_like(acc_ref)

        acc_ref[...] += _dot(a_ref[...].astype(BF16), b_ref[...].astype(BF16), TN)

        @pl.when(kk == pl.num_programs(2) - 1)
        def _():
            out_ref[...] = acc_ref[...].astype(out_dtype)

    return pl.pallas_call(
        body, name=name, grid=(m // tm, n // tn, t // tk),
        in_specs=[pl.BlockSpec((tk, tm), lambda i, j, k: (k, i)), pl.BlockSpec((tk, tn), lambda i, j, k: (k, j))]
        + [pl.BlockSpec(memory_space=pl.ANY)] * len(deps),
        out_specs=pl.BlockSpec((tm, tn), lambda i, j, k: (i, j)),
        out_shape=jax.ShapeDtypeStruct((m, n), out_dtype),
        scratch_shapes=[pltpu.VMEM((tm, tn), F32)],
        compiler_params=_params(("parallel", "parallel", "arbitrary")),
    )(a, b, *deps)


def _stack_rows(a, b, name):
    ta, d = a.shape
    tm = _pick(int(np.gcd(ta, b.shape[0])), (256, 128, 64, 32, 16, 8))
    na, nb = ta // tm, b.shape[0] // tm

    def body(a_ref, b_ref, o_ref):
        o_ref[...] = jnp.where(pl.program_id(0) < na, a_ref[...], b_ref[...])

    return pl.pallas_call(
        body, name=name, grid=(na + nb,),
        in_specs=[pl.BlockSpec((tm, d), lambda i: (jnp.minimum(i, na - 1), 0)),
                  pl.BlockSpec((tm, d), lambda i: (jnp.maximum(i - na, 0), 0))],
        out_specs=pl.BlockSpec((tm, d), lambda i: (i, 0)),
        out_shape=jax.ShapeDtypeStruct((ta + b.shape[0], d), a.dtype),
        compiler_params=_params(("parallel",)),
    )(a, b)


def _assemble_dz(lat_parts, ctx_parts, both_parts, width, name):
    t = next(p.shape[0] for p in lat_parts if p is not None)
    l_ctx = next(p.shape[0] for p in ctx_parts if p is not None)
    tm = _pick(int(np.gcd(t, l_ctx)), (256, 128, 64, 32, 16, 8))
    nt, nl = t // tm, l_ctx // tm
    plan, args, in_specs, off = [], [], [], 0
    lat_spec = lambda w: pl.BlockSpec((tm, w), lambda i: (jnp.minimum(i, nt - 1), 0))
    ctx_spec = lambda w: pl.BlockSpec((tm, w), lambda i: (jnp.maximum(i - nt, 0), 0))
    all_spec = lambda w: pl.BlockSpec((tm, w), lambda i: (i, 0))
    for lat, ctx, both in zip(lat_parts, ctx_parts, both_parts):
        if both:
            w = both[0].shape[1]
            plan.append(("both", off, w, len(args), len(both)))
            args += both
            in_specs += [all_spec(w)] * len(both)
        else:
            w = (lat if lat is not None else ctx).shape[1]
            plan.append(("split", off, w, len(args), (lat is not None, ctx is not None)))
            for part, spec in ((lat, lat_spec), (ctx, ctx_spec)):
                if part is not None:
                    args.append(part)
                    in_specs.append(spec(w))
        off += w
    n_in = len(args)

    def body(*refs):
        out_ref = refs[n_in]
        is_ctx = pl.program_id(0) >= nt
        for kind, o, w, first, info in plan:
            if kind == "both":
                val = refs[first][...]
                for k in range(1, info):
                    val = val + refs[first + k][...]
            else:
                has_lat, has_ctx = info
                zero = jnp.zeros((tm, w), F32)
                lat = refs[first][...] if has_lat else zero
                ctx = refs[first + int(has_lat)][...] if has_ctx else zero
                val = jnp.where(is_ctx, ctx, lat)
            out_ref[:, o:o + w] = val.astype(BF16)
        if off < width:
            out_ref[:, off:width] = jnp.zeros((tm, width - off), BF16)

    return pl.pallas_call(
        body, name=name, grid=(nt + nl,), in_specs=in_specs,
        out_specs=pl.BlockSpec((tm, width), lambda i: (i, 0)),
        out_shape=jax.ShapeDtypeStruct((t + l_ctx, width), BF16),
        compiler_params=_params(("parallel",)),
    )(*args)


def _final_loss(x, g, target, then, name):
    t, d = x.shape
    tm = _stream_tile(t, t)
    y, tmods, km, coef = then

    def body(x_ref, g_ref, t_ref, y_ref, tmods_ref, dx_ref, loss_ref, dg_ref, dy_ref, gate_acc_ref):
        @pl.when(pl.program_id(0) == 0)
        def _():
            loss_ref[...] = jnp.zeros_like(loss_ref)
            dg_ref[...] = jnp.zeros_like(dg_ref)
            gate_acc_ref[...] = jnp.zeros_like(gate_acc_ref)

        xx, gg = x_ref[...], g_ref[...]
        r = lax.rsqrt(jnp.mean(xx * xx, axis=-1, keepdims=True) + RMS_EPS)
        nz = xx * r
        err = nz * gg - t_ref[...]
        loss_ref[...] += jnp.sum(err * err, axis=0, keepdims=True) * (0.5 / d)
        dout = err * (1.0 / d)
        dg_ref[...] += jnp.sum(dout * nz, axis=0, keepdims=True)
        dn = dout * gg
        dx = r * (dn - nz * jnp.mean(dn * nz, axis=-1, keepdims=True))
        dx_ref[...] = dx
        dy, part = _gate_bwd_rows(dx, y_ref[...], tmods_ref[0, km:km + 1, :], coef)
        dy_ref[...] = dy
        gate_acc_ref[0:1, :] += part

    row = pl.BlockSpec((tm, d), lambda i: (i, 0))
    vec = pl.BlockSpec((1, d), lambda i: (0, 0))
    acc = pl.BlockSpec((None, 8, d), lambda i: (0, 0, 0))
    return pl.pallas_call(
        body, name=name, grid=(t // tm,), in_specs=[row, vec, row, row, _full(tmods.shape)],
        out_specs=[row, vec, vec, row, acc],
        out_shape=[jax.ShapeDtypeStruct((t, d), F32), jax.ShapeDtypeStruct((1, d), F32),
                   jax.ShapeDtypeStruct((1, d), F32), jax.ShapeDtypeStruct((t, d), BF16),
                   jax.ShapeDtypeStruct((2, 8, d), F32)],
        compiler_params=_params(("arbitrary",)),
    )(x, g, target, y, tmods)


def _adaln_fwd(craw, w_mod, b_cols, name):
    lyr, d, nc = w_mod.shape

    def body(c_ref, w_ref, b_ref, out_ref):
        out_ref[...] = _bdot(_silu(c_ref[...]), w_ref[...]) + b_ref[...]

    return pl.pallas_call(
        body, name=name, grid=(lyr,),
        in_specs=[_full(craw.shape), pl.BlockSpec((None, d, nc), lambda l: (l, 0, 0)),
                  pl.BlockSpec((None, 1, nc), lambda l: (l, 0, 0))],
        out_specs=pl.BlockSpec((None, 16, nc), lambda l: (l, 0, 0)),
        out_shape=jax.ShapeDtypeStruct((lyr, 16, nc), F32),
        compiler_params=_params(("parallel",)),
    )(craw, w_mod, b_cols)


def _adaln_bwd(craw, cs_t, dmm_cols, w_mod, name):
    lyr, d, nc = w_mod.shape

    def body(c_ref, cst_ref, dmm_ref, w_ref, gw_ref, dc_ref):
        dmm = dmm_ref[...]
        gw_ref[...] = _bdot(cst_ref[...], dmm)
        cc = c_ref[...]
        sg = jax.nn.sigmoid(cc)
        dc_ref[...] = _bdot(dmm, w_ref[...], NT) * (sg * (1.0 + cc * (1.0 - sg)))

    wspec = pl.BlockSpec((None, d, nc), lambda l: (l, 0, 0))
    return pl.pallas_call(
        body, name=name, grid=(lyr,),
        in_specs=[_full(craw.shape), _full(cs_t.shape), pl.BlockSpec((None, 16, nc), lambda l: (l, 0, 0)), wspec],
        out_specs=[wspec, pl.BlockSpec((None, 16, d), lambda l: (l, 0, 0))],
        out_shape=[jax.ShapeDtypeStruct((lyr, d, nc), F32), jax.ShapeDtypeStruct((lyr, 16, d), F32)],
        compiler_params=_params(("parallel",)),
    )(craw, cs_t, dmm_cols, w_mod)


def _rope_tables(t):
    rows = np.repeat(np.arange(t // GRID_W, dtype=np.float32), GRID_W)
    cols = np.tile(np.arange(GRID_W, dtype=np.float32), t // GRID_W)
    n = A_HEAD_DIM // 4
    freqs = (ROPE_BASE ** (-np.arange(n, dtype=np.float32) / n)).astype(np.float32)
    ang_r, ang_c = (rows[:, None] * freqs).astype(np.float32), (cols[:, None] * freqs).astype(np.float32)
    cr, sr, cc, sc = np.cos(ang_r), np.sin(ang_r), np.cos(ang_c), np.sin(ang_c)
    cos = np.concatenate([cr, cr, cc, cc] * 2, axis=-1).astype(np.float32)
    sin = np.concatenate([-sr, sr, -sc, sc] * 2, axis=-1).astype(np.float32)
    return jnp.asarray(cos), jnp.asarray(sin)


def _rope(xt, cos, sin, adjoint, name, view=None):
    t = cos.shape[0]
    w, col = (xt.shape[1], 0) if view is None else view
    tb = _pick(t, (1024, 512, 256, 128))
    rep = w // cos.shape[1]

    def body(x_ref, c_ref, s_ref, o_ref):
        xx = x_ref[...]
        cc = jnp.concatenate([c_ref[...]] * rep, axis=1) if rep > 1 else c_ref[...]
        ss = jnp.concatenate([s_ref[...]] * rep, axis=1) if rep > 1 else s_ref[...]
        low = (lax.broadcasted_iota(jnp.int32, xx.shape, 1) % 32) < 16

        def partner(v):
            return jnp.where(low, pltpu.roll(v, w - 16, 1), pltpu.roll(v, 16, 1))

        if adjoint:
            o_ref[...] = xx * cc + partner(xx * ss)
        else:
            o_ref[...] = xx * cc + partner(xx) * ss

    blk = pl.BlockSpec((tb, w), lambda i: (i, 0))
    tab = pl.BlockSpec((tb, cos.shape[1]), lambda i: (i, 0))
    return pl.pallas_call(
        body, name=name, grid=(t // tb,), in_specs=[pl.BlockSpec((tb, w), lambda i: (i, col)), tab, tab],
        out_specs=blk, out_shape=jax.ShapeDtypeStruct((t, w), F32), compiler_params=_params(("parallel",)),
    )(xt, cos, sin)


def _attn_bias():
    i = (np.arange(A_REP * WINDOW) % WINDOW)[:, None]
    j = np.arange(3 * WINDOW)[None, :]
    near = np.abs(j - WINDOW - i) <= WINDOW
    variants = [near, near & (j >= WINDOW), near & (j < 2 * WINDOW), near & (j >= WINDOW) & (j < 2 * WINDOW)]
    return jnp.asarray(np.where(np.stack(variants), 0.0, -np.inf).astype(np.float32))


def _attn_bias_spec(nb):
    rows = A_REP * WINDOW
    return pl.BlockSpec((None, rows, 3 * WINDOW),
                        lambda g, n: ((n == 0).astype(jnp.int32) + 2 * (n == nb - 1).astype(jnp.int32), 0, 0))


def _attn_probs(q, kb, kc, sink, bias):
    scale = A_HEAD_DIM ** -0.5
    s1 = _bdot(q, kb, NT) * scale + bias
    s2 = _bdot(q, kc, NT) * scale
    mx = jnp.maximum(jnp.maximum(jnp.max(s1, axis=-1, keepdims=True), jnp.max(s2, axis=-1, keepdims=True)), sink)
    p1, p2, ps = jnp.exp(s1 - mx), jnp.exp(s2 - mx), jnp.exp(sink - mx)
    inv = 1.0 / (jnp.sum(p1, axis=-1, keepdims=True) + jnp.sum(p2, axis=-1, keepdims=True) + ps)
    return p1 * inv, p2 * inv, ps * inv


def _sink_rows(sink_ref):
    return jnp.concatenate([jnp.broadcast_to(sink_ref[r], (WINDOW, 1)) for r in range(A_REP)], axis=0)


def _attn_fwd(q, kp, vp, kc, vc, sink, name):
    hq, t, dh = q.shape
    nb = t // WINDOW
    lc = kc.shape[1]
    rows = A_REP * WINDOW

    def body(q_ref, k_ref, v_ref, kc_ref, vc_ref, sink_ref, bias_ref, o_ref):
        n = pl.program_id(1)
        start = pl.multiple_of(n * WINDOW, WINDOW)
        kb, vb = k_ref[pl.ds(start, 3 * WINDOW), :], v_ref[pl.ds(start, 3 * WINDOW), :]
        p1, p2, _ = _attn_probs(q_ref[...].reshape(rows, dh), kb, kc_ref[...], _sink_rows(sink_ref), bias_ref[...])
        o_ref[...] = (_bdot(p1, vb) + _bdot(p2, vc_ref[...])).reshape(A_REP, WINDOW, dh)

    qblk = pl.BlockSpec((A_REP, WINDOW, dh), lambda g, n: (g, n, 0))
    kfull = pl.BlockSpec((None, t + 2 * WINDOW, dh), lambda g, n: (g, 0, 0))
    cfull = pl.BlockSpec((None, lc, dh), lambda g, n: (g, 0, 0))
    return pl.pallas_call(
        body, name=name, grid=(hq // A_REP, nb),
        in_specs=[qblk, kfull, kfull, cfull, cfull, pl.BlockSpec((A_REP, 1, 1), lambda g, n: (g, 0, 0)),
                  _attn_bias_spec(nb)],
        out_specs=qblk, out_shape=jax.ShapeDtypeStruct((hq, t, dh), F32),
        compiler_params=_params(("parallel", "parallel")),
    )(q, kp, vp, kc, vc, sink, _attn_bias())


def _attn_bwd(q, kp, vp, kc, vc, sink, o, do, name):
    hq, t, dh = q.shape
    nb = t // WINDOW
    lc = kc.shape[1]
    scale = A_HEAD_DIM ** -0.5
    rows = A_REP * WINDOW

    def body(q_ref, k_ref, v_ref, kc_ref, vc_ref, sink_ref, o_ref, do_ref, bias_ref,
             dq_ref, dk_ref, dv_ref, dkc_ref, dvc_ref, dsink_ref):
        n = pl.program_id(1)

        @pl.when(n == 0)
        def _():
            dk_ref[...] = jnp.zeros_like(dk_ref)
            dv_ref[...] = jnp.zeros_like(dv_ref)
            dkc_ref[...] = jnp.zeros_like(dkc_ref)
            dvc_ref[...] = jnp.zeros_like(dvc_ref)
            dsink_ref[...] = jnp.zeros_like(dsink_ref)

        start = pl.multiple_of(n * WINDOW, WINDOW)
        band = pl.ds(start, 3 * WINDOW)
        qq, kb, vb, kcc, vcc = q_ref[...].reshape(rows, dh), k_ref[band, :], v_ref[band, :], kc_ref[...], vc_ref[...]
        p1, p2, ps = _attn_probs(qq, kb, kcc, _sink_rows(sink_ref), bias_ref[...])
        dout = do_ref[...].reshape(rows, dh)
        delta = jnp.sum(dout * o_ref[...].reshape(rows, dh), axis=-1, keepdims=True)
        ds1 = p1 * (_bdot(dout, vb, NT) - delta)
        ds2 = p2 * (_bdot(dout, vcc, NT) - delta)
        dq_ref[...] = ((_bdot(ds1, kb) + _bdot(ds2, kcc)) * scale).reshape(A_REP, WINDOW, dh)
        dk_ref[band, :] += _bdot(ds1.T, qq) * scale
        dv_ref[band, :] += _bdot(p1.T, dout)
        dkc_ref[...] += _bdot(ds2.T, qq) * scale
        dvc_ref[...] += _bdot(p2.T, dout)
        dsink_ref[...] += jnp.sum((-ps * delta).reshape(A_REP, WINDOW, 1), axis=1, keepdims=True)

    qblk = pl.BlockSpec((A_REP, WINDOW, dh), lambda g, n: (g, n, 0))
    kfull = pl.BlockSpec((None, t + 2 * WINDOW, dh), lambda g, n: (g, 0, 0))
    cfull = pl.BlockSpec((None, lc, dh), lambda g, n: (g, 0, 0))
    return pl.pallas_call(
        body, name=name, grid=(hq // A_REP, nb),
        in_specs=[qblk, kfull, kfull, cfull, cfull, pl.BlockSpec((A_REP, 1, 1), lambda g, n: (g, 0, 0)), qblk, qblk,
                  _attn_bias_spec(nb)],
        out_specs=[qblk, kfull, kfull, cfull, cfull, pl.BlockSpec((A_REP, 8, 128), lambda g, n: (g, 0, 0))],
        out_shape=[jax.ShapeDtypeStruct(q.shape, F32), jax.ShapeDtypeStruct(kp.shape, F32),
                   jax.ShapeDtypeStruct(kp.shape, F32), jax.ShapeDtypeStruct(kc.shape, F32),
                   jax.ShapeDtypeStruct(kc.shape, F32), jax.ShapeDtypeStruct((hq, 8, 128), F32)],
        compiler_params=_params(("parallel", "arbitrary")),
    )(q, kp, vp, kc, vc, sink, o, do, _attn_bias())


def _gate_fwd(zg, w2, b2, name):
    m = zg.shape[0]
    n = w2.shape[1]
    tm = _pick(m, TALL_TILES)

    def body(z_ref, w_ref, b_ref, o_ref):
        o_ref[...] = jax.nn.log_sigmoid(_bdot(z_ref[...], w_ref[...]) + b_ref[...]) / B_GATE_NORM

    return pl.pallas_call(
        body, name=name, grid=(m // tm,),
        in_specs=[pl.BlockSpec((tm, zg.shape[1]), lambda i: (i, 0)), _full(w2.shape), _full(b2.shape)],
        out_specs=pl.BlockSpec((tm, n), lambda i: (i, 0)), out_shape=jax.ShapeDtypeStruct((m, n), F32),
        compiler_params=_params(("parallel",)),
    )(zg, w2, b2)


def _gate_bwd(zg, w2, b2, dla, name):
    m, rk = zg.shape
    n = w2.shape[1]
    tm = _pick(m, TALL_TILES)

    def body(z_ref, w_ref, b_ref, d_ref, dz_ref, dw_ref, db_ref):
        @pl.when(pl.program_id(0) == 0)
        def _():
            dw_ref[...] = jnp.zeros_like(dw_ref)
            db_ref[...] = jnp.zeros_like(db_ref)

        zz, ww = z_ref[...], w_ref[...]
        pre = _bdot(zz, ww) + b_ref[...]
        dpre = d_ref[...] * (1.0 / B_GATE_NORM) * jax.nn.sigmoid(-pre)
        dz_ref[...] = _bdot(dpre, ww, NT)
        dw_ref[...] += _bdot(zz.T, dpre)
        db_ref[...] += jnp.sum(dpre, axis=0, keepdims=True)

    return pl.pallas_call(
        body, name=name, grid=(m // tm,),
        in_specs=[pl.BlockSpec((tm, rk), lambda i: (i, 0)), _full(w2.shape), _full(b2.shape),
                  pl.BlockSpec((tm, n), lambda i: (i, 0))],
        out_specs=[pl.BlockSpec((tm, rk), lambda i: (i, 0)), _full(w2.shape), _full(b2.shape)],
        out_shape=[jax.ShapeDtypeStruct((m, rk), F32), jax.ShapeDtypeStruct(w2.shape, F32),
                   jax.ShapeDtypeStruct(b2.shape, F32)],
        compiler_params=_params(("arbitrary",)),
    )(zg, w2, b2, dla)


def _chunk_order(step, n_x_chunks, n_chunks, reverse):
    n_c = n_chunks - n_x_chunks
    if reverse:
        return jnp.where(step < n_c, n_chunks - 1 - step, n_chunks - 1 - step)
    return jnp.where(step < n_c, n_x_chunks + step, step - n_c)


def _tri(reverse, transpose=False):
    i = lax.broadcasted_iota(jnp.int32, (B_CHUNK, B_CHUNK), 0)
    j = lax.broadcasted_iota(jnp.int32, (B_CHUNK, B_CHUNK), 1)
    if transpose:
        i, j = j, i
    return (j >= i) if reverse else (j <= i)


def _gla_chunk(q, k, la, reverse):
    g = _dot_01(_tri(reverse), la)
    last = 0 if reverse else B_CHUNK - 1
    gl = g[last:last + 1, :]
    eg, eng, egl = jnp.exp(g), jnp.exp(-g), jnp.exp(gl - g)
    decay_col = jnp.exp(jnp.sum(la.T, axis=1, keepdims=True))
    return q * (B_DK ** -0.5) * eg, k * eng, k * egl, eg, eng, egl, decay_col


def _head_of(shape, axis, width):
    return lax.broadcasted_iota(jnp.int32, shape, axis) // width


def _gla_chunks_per_step(n_chunks, n_x_chunks):
    return _pick(int(np.gcd(n_chunks - n_x_chunks, n_x_chunks)), (4, 2, 1))


def _gla_fwd(q, k, v, la_f, la_b, n_x, name, qk_cols=(0, 0)):
    tc, wk = la_f.shape
    wv = v.shape[1]
    hh = B_HEADS
    dk, dv = wk // hh, wv // hh
    nc, nxc = tc // B_CHUNK, n_x // B_CHUNK
    sub = _gla_chunks_per_step(nc, nxc)
    rows_per_step = sub * B_CHUNK
    orders = [functools.partial(_chunk_order, n_x_chunks=nxc // sub, n_chunks=nc // sub, reverse=rev)
              for rev in (False, True)]

    def body(*refs):
        ins, outs, s_refs = refs[:8], refs[8:12], refs[12:]

        @pl.when(pl.program_id(0) == 0)
        def _():
            for s_ref in s_refs:
                s_ref[...] = jnp.zeros_like(s_ref)

        lane_head = _head_of((B_CHUNK, wk), 1, dk)
        row_head = _head_of((wk, dv), 0, dk)
        for di, reverse in enumerate((False, True)):
            q_ref, k_ref, v_ref, la_ref = ins[4 * di:4 * di + 4]
            o_ref, s_save_ref = outs[2 * di:2 * di + 2]
            s_prev = s_refs[di][...]
            for c in (reversed(range(sub)) if reverse else range(sub)):
                rows = slice(c * B_CHUNK, (c + 1) * B_CHUNK)
                qt, kt, ke, _, _, _, decay_col = _gla_chunk(q_ref[rows, :], k_ref[rows, :], la_ref[rows, :], reverse)
                ke_t = ke.T
                update = jnp.zeros_like(s_prev)
                for h in range(hh):
                    vv = v_ref[rows, h * dv:(h + 1) * dv]
                    qm = jnp.where(lane_head == h, qt, 0.0)
                    att = jnp.where(_tri(reverse), _bdot(qm, kt, NT), 0.0)
                    o_ref[rows, h * dv:(h + 1) * dv] = _bdot(att, vv) + _bdot(qm, s_prev)
                    update = jnp.where(row_head == h, _bdot(ke_t, vv), update)
                s_save_ref[c] = s_prev
                s_prev = decay_col * s_prev + update
            s_refs[di][...] = s_prev

    def blk(w, order, col=0):
        return pl.BlockSpec((rows_per_step, w), lambda s: (order(s), col))

    def sblk(order):
        return pl.BlockSpec((sub, wk, dv), lambda s: (order(s), 0, 0))

    in_specs, out_specs = [], []
    for order in orders:
        in_specs += [blk(wk, order, qk_cols[0]), blk(wk, order, qk_cols[1]), blk(wv, order), blk(wk, order)]
        out_specs += [blk(wv, order), sblk(order)]
    o_shape, s_shape = jax.ShapeDtypeStruct((tc, wv), F32), jax.ShapeDtypeStruct((nc, wk, dv), F32)
    return pl.pallas_call(
        body, name=name, grid=(nc // sub,), in_specs=in_specs, out_specs=out_specs,
        out_shape=[o_shape, s_shape, o_shape, s_shape],
        scratch_shapes=[pltpu.VMEM((wk, dv), F32)] * 2,
        compiler_params=_params(("arbitrary",)),
    )(q, k, v, la_f, q, k, v, la_b)


def _gla_bwd(q, k, v, la_f, la_b, s_f, s_b, do, n_x, name, qk_cols=(0, 0)):
    tc, wk = la_f.shape
    wv = v.shape[1]
    hh = B_HEADS
    dk, dv = wk // hh, wv // hh
    nc, nxc = tc // B_CHUNK, n_x // B_CHUNK
    sub = _gla_chunks_per_step(nc, nxc)
    rows_per_step = sub * B_CHUNK
    nb, nxb = nc // sub, nxc // sub
    orders = [functools.partial(lambda s, rev: _chunk_order(nb - 1 - s, nxb, nb, rev), rev=rev) for rev in (False, True)]

    def body(*refs):
        ins, outs, ds_refs = refs[:12], refs[12:20], refs[20:]

        @pl.when(pl.program_id(0) == 0)
        def _():
            for ds_ref in ds_refs:
                ds_ref[...] = jnp.zeros_like(ds_ref)

        lane_head = _head_of((B_CHUNK, wk), 1, dk)
        row_head = _head_of((wk, dv), 0, dk)
        for di, reverse in enumerate((False, True)):
            q_ref, k_ref, v_ref, la_ref, s_save_ref, do_ref = ins[6 * di:6 * di + 6]
            dq_ref, dk_ref, dv_ref, dla_ref = outs[4 * di:4 * di + 4]
            mask = _tri(reverse)
            last = 0 if reverse else B_CHUNK - 1
            is_last = lax.broadcasted_iota(jnp.int32, (B_CHUNK, wk), 0) == last
            ds_new = ds_refs[di][...]
            for c in (range(sub) if reverse else reversed(range(sub))):
                rows = slice(c * B_CHUNK, (c + 1) * B_CHUNK)
                la = la_ref[rows, :]
                qt, kt, ke, eg, eng, egl, decay_col = _gla_chunk(q_ref[rows, :], k_ref[rows, :], la, reverse)
                qt_t = qt.T
                s_prev = s_save_ref[c]
                dqt, dkt, dke = jnp.zeros_like(qt), jnp.zeros_like(qt), jnp.zeros_like(qt)
                ds_add = jnp.zeros_like(ds_new)
                for h in range(hh):
                    cols = slice(h * dv, (h + 1) * dv)
                    vv, dout = v_ref[rows, cols], do_ref[rows, cols]
                    mine = lane_head == h
                    qm, km = jnp.where(mine, qt, 0.0), jnp.where(mine, ke, 0.0)
                    att = jnp.where(mask, _bdot(qm, kt, NT), 0.0)
                    datt = jnp.where(mask, _bdot(dout, vv, NT), 0.0)
                    dv_ref[rows, cols] = _bdot(att.T, dout) + _bdot(km, ds_new)
                    dqt = jnp.where(mine, _bdot(datt, kt) + _bdot(dout, s_prev, NT), dqt)
                    dkt = jnp.where(mine, _bdot(datt.T, qt), dkt)
                    dke = jnp.where(mine, _bdot(vv, ds_new, NT), dke)
                    ds_add = jnp.where(row_head == h, _bdot(qt_t, dout), ds_add)
                ddecay_row = jnp.sum((ds_new * s_prev).T, axis=0, keepdims=True)
                decay_row = jnp.exp(jnp.sum(la, axis=0, keepdims=True))
                dq_ref[rows, :] = dqt * (B_DK ** -0.5) * eg
                dk_ref[rows, :] = dkt * eng + dke * egl
                dgl = jnp.sum(dke * ke, axis=0, keepdims=True) + ddecay_row * decay_row
                dg = dqt * qt - dkt * kt - dke * ke + jnp.where(is_last, dgl, 0.0)
                dla_ref[rows, :] = _dot_01(_tri(reverse, transpose=True), dg)
                ds_new = decay_col * ds_new + ds_add
            ds_refs[di][...] = ds_new

    def blk(w, order, col=0):
        return pl.BlockSpec((rows_per_step, w), lambda s: (order(s), col))

    in_specs, out_specs = [], []
    for order in orders:
        in_specs += [blk(wk, order, qk_cols[0]), blk(wk, order, qk_cols[1]), blk(wv, order), blk(wk, order),
                     pl.BlockSpec((sub, wk, dv), lambda s, order=order: (order(s), 0, 0)), blk(wv, order)]
        out_specs += [blk(wk, order), blk(wk, order), blk(wv, order), blk(wk, order)]
    k_shape, v_shape = jax.ShapeDtypeStruct((tc, wk), F32), jax.ShapeDtypeStruct((tc, wv), F32)
    return pl.pallas_call(
        body, name=name, grid=(nb,), in_specs=in_specs, out_specs=out_specs,
        out_shape=[k_shape, k_shape, v_shape, k_shape] * 2,
        scratch_shapes=[pltpu.VMEM((wk, dv), F32)] * 2,
        compiler_params=_params(("arbitrary",)),
    )(q, k, v, la_f, s_f, do, q, k, v, la_b, s_b, do)


def _gla_out_fwd(o_f, o_b, r, g, name):
    t = r.shape[0]
    dv = g.shape[1]
    hh = r.shape[1] // dv
    tb = _pick(t, (512, 256, 128, 64))

    def body(of_ref, ob_ref, r_ref, g_ref, out_ref):
        for h in range(hh):
            cols = slice(h * dv, (h + 1) * dv)
            o = of_ref[:, cols] + ob_ref[:, cols]
            rs = lax.rsqrt(jnp.mean(o * o, axis=-1, keepdims=True) + RMS_EPS)
            out_ref[:, cols] = (o * rs) * g_ref[...] * _silu(r_ref[:, cols])

    rblk = pl.BlockSpec((tb, hh * dv), lambda i: (i, 0))
    return pl.pallas_call(
        body, name=name, grid=(t // tb,), in_specs=[rblk, rblk, rblk, _full(g.shape)], out_specs=rblk,
        out_shape=jax.ShapeDtypeStruct((t, hh * dv), F32), compiler_params=_params(("parallel",)),
    )(o_f, o_b, r, g)


def _gla_out_bwd(o_f, o_b, r, g, dout, name):
    tc = o_f.shape[0]
    t = r.shape[0]
    dv = g.shape[1]
    hh = r.shape[1] // dv
    tb = _pick(int(np.gcd(t, tc)), (256, 128, 64))
    nt = t // tb

    def body(of_ref, ob_ref, r_ref, g_ref, d_ref, do_ref, dr_ref, dg_ref):
        i = pl.program_id(0)

        @pl.when(i == 0)
        def _():
            dg_ref[...] = jnp.zeros_like(dg_ref)

        @pl.when(i >= nt)
        def _():
            do_ref[...] = jnp.zeros_like(do_ref)

        @pl.when(i < nt)
        def _():
            gg = g_ref[...]
            for h in range(hh):
                cols = slice(h * dv, (h + 1) * dv)
                o = of_ref[:, cols] + ob_ref[:, cols]
                rs = lax.rsqrt(jnp.mean(o * o, axis=-1, keepdims=True) + RMS_EPS)
                nz = o * rs
                rr, dd = r_ref[:, cols], d_ref[:, cols]
                sg = jax.nn.sigmoid(rr)
                dr_ref[:, cols] = dd * nz * gg * (sg * (1.0 + rr * (1.0 - sg)))
                dy = dd * (rr * sg)
                dg_ref[...] += jnp.sum(dy * nz, axis=0, keepdims=True)
                dn = dy * gg
                do_ref[:, cols] = rs * (dn - nz * jnp.mean(dn * nz, axis=-1, keepdims=True))

    oblk = pl.BlockSpec((tb, hh * dv), lambda i: (i, 0))
    rblk = pl.BlockSpec((tb, hh * dv), lambda i: (jnp.minimum(i, nt - 1), 0))
    return pl.pallas_call(
        body, name=name, grid=(tc // tb,), in_specs=[oblk, oblk, rblk, _full(g.shape), rblk],
        out_specs=[oblk, rblk, _full(g.shape)],
        out_shape=[jax.ShapeDtypeStruct(o_f.shape, F32), jax.ShapeDtypeStruct(r.shape, F32),
                   jax.ShapeDtypeStruct(g.shape, F32)],
        compiler_params=_params(("arbitrary",)),
    )(o_f, o_b, r, g, dout)


def _pool_tile(t):
    return _pick(t, tuple(p for p in (512, 256, 128, 64) if p + 2 * POOL_PAD <= t))


def _pool_window(i, tb, t):
    return pl.multiple_of(jnp.clip(i * tb - POOL_PAD, 0, t - (tb + 2 * POOL_PAD)), 8)


def _pool_band(half, i, tb, start, adjoint):
    pos = i * tb + lax.broadcasted_iota(jnp.int32, (tb, tb + 2 * POOL_PAD), 0)
    tok = start + lax.broadcasted_iota(jnp.int32, (tb, tb + 2 * POOL_PAD), 1)
    if adjoint:
        return (tok > pos - half) & (tok <= pos + half)
    return (tok >= pos - half) & (tok < pos + half)


def _pool_count(pos, half, t):
    return (jnp.minimum(pos + half, t) - jnp.maximum(pos - half, 0)).astype(F32)


def _pool_fwd(h, w_pool, pool_scale, res, mods, km, name):
    t, d = res.shape
    ng, gw = w_pool.shape[0], w_pool.shape[1]
    tb = _pool_tile(t)

    def body(h_ref, w_ref, ps_ref, res_ref, mods_ref, out_ref, pooled_ref, ypre_ref):
        gi, i = pl.program_id(0), pl.program_id(1)
        half = jnp.left_shift(1, gi)
        start = _pool_window(i, tb, t)
        win = h_ref[pl.ds(start, tb + 2 * POOL_PAD), :]
        total = _dot_01(_pool_band(half, i, tb, start, False), win)
        pos = i * tb + lax.broadcasted_iota(jnp.int32, (tb, 1), 0)
        pooled = total / _pool_count(pos, half, t) - h_ref[pl.ds(pl.multiple_of(i * tb, tb), tb), :]
        ypre = _bdot(pooled, w_ref[...])
        pooled_ref[...] = pooled.astype(BF16)
        ypre_ref[...] = ypre
        out_ref[...] = res_ref[...] + mods_ref[0, km:km + 1, :] * (ypre * ps_ref[...])

    tile = pl.BlockSpec((tb, gw), lambda gi, i: (i, gi))
    return pl.pallas_call(
        body, name=name, grid=(ng, t // tb),
        in_specs=[pl.BlockSpec((t, gw), lambda gi, i: (0, gi)),
                  pl.BlockSpec((None, gw, gw), lambda gi, i: (gi, 0, 0)),
                  pl.BlockSpec((1, gw), lambda gi, i: (0, gi)), tile,
                  pl.BlockSpec((2, 16, gw), lambda gi, i: (0, 0, gi))],
        out_specs=[tile, tile, tile],
        out_shape=[jax.ShapeDtypeStruct((t, d), F32), jax.ShapeDtypeStruct((t, d), BF16),
                   jax.ShapeDtypeStruct((t, d), F32)],
        compiler_params=_params(("parallel", "parallel")),
    )(h, w_pool, pool_scale, res, mods)


def _pool_bwd(dxp, w_pool, pool_scale, pooled, ypre, mods, km, name):
    t, d = pooled.shape
    ng, gw = w_pool.shape[0], w_pool.shape[1]
    tb = _pool_tile(t)

    def body(dxp_ref, w_ref, ps_ref, pooled_ref, ypre_ref, mods_ref, dh_ref, dw_ref, acc_ref):
        gi, i = pl.program_id(0), pl.program_id(1)

        @pl.when(i == 0)
        def _():
            dw_ref[...] = jnp.zeros_like(dw_ref)
            acc_ref[...] = jnp.zeros_like(acc_ref)

        half = jnp.left_shift(1, gi)
        mod, ps = mods_ref[0, km:km + 1, :], ps_ref[...]
        start = _pool_window(i, tb, t)
        dwin = dxp_ref[pl.ds(start, tb + 2 * POOL_PAD), :]
        dpooled = _bdot(dwin * (mod * ps), w_ref[...], NT)
        pos = start + lax.broadcasted_iota(jnp.int32, (tb + 2 * POOL_PAD, 1), 0)
        spread = _dot_01(_pool_band(half, i, tb, start, True), dpooled / _pool_count(pos, half, t))
        dxc, yp = dxp_ref[pl.ds(pl.multiple_of(i * tb, tb), tb), :], ypre_ref[...]
        dh_ref[...] = spread - _bdot(dxc * (mod * ps), w_ref[...], NT)
        dw_ref[...] += _bdot(pooled_ref[...].astype(F32).T, dxc * (mod * ps))
        acc_ref[0:1, :] += jnp.sum(dxc * yp * mod, axis=0, keepdims=True)
        acc_ref[1:2, :] += jnp.sum(dxc * yp * ps, axis=0, keepdims=True)

    tile = pl.BlockSpec((tb, gw), lambda gi, i: (i, gi))
    wblk = pl.BlockSpec((None, gw, gw), lambda gi, i: (gi, 0, 0))
    return pl.pallas_call(
        body, name=name, grid=(ng, t // tb),
        in_specs=[pl.BlockSpec((t, gw), lambda gi, i: (0, gi)), wblk,
                  pl.BlockSpec((1, gw), lambda gi, i: (0, gi)), tile, tile,
                  pl.BlockSpec((2, 16, gw), lambda gi, i: (0, 0, gi))],
        out_specs=[tile, wblk, pl.BlockSpec((8, gw), lambda gi, i: (0, gi))],
        out_shape=[jax.ShapeDtypeStruct((t, d), F32), jax.ShapeDtypeStruct(w_pool.shape, F32),
                   jax.ShapeDtypeStruct((8, d), F32)],
        compiler_params=_params(("arbitrary", "arbitrary")),
    )(dxp, w_pool, pool_scale, pooled, ypre, mods)


def _adamw(w, g, m, v, name):
    r, c = w.shape
    tr = _pick(r, (512, 352, 256, 128, 64, 32, 16, 8))
    c1 = 1.0 / (1.0 - ADAM_B1 ** ADAM_STEP)
    c2 = 1.0 / (1.0 - ADAM_B2 ** ADAM_STEP)

    def body(w_ref, g_ref, m_ref, v_ref, d_ref, nm_ref, nv_ref):
        gg = g_ref[...]
        nm = ADAM_B1 * m_ref[...] + (1.0 - ADAM_B1) * gg
        nv = ADAM_B2 * v_ref[...] + (1.0 - ADAM_B2) * (gg * gg)
        nm_ref[...] = nm
        nv_ref[...] = nv
        d_ref[...] = -ADAM_LR * ((nm * c1) / (jnp.sqrt(nv * c2) + ADAM_EPS) + ADAM_WD * w_ref[...])

    blk = pl.BlockSpec((tr, c), lambda i: (i, 0))
    shp = jax.ShapeDtypeStruct((r, c), F32)
    return pl.pallas_call(
        body, name=name, grid=(r // tr,), in_specs=[blk] * 4, out_specs=[blk] * 3, out_shape=[shp] * 3,
        compiler_params=_params(("parallel",)),
    )(w, g, m, v)


def _heads(z, n_heads):
    m = z.shape[0]
    return z.reshape(m, n_heads, -1).transpose(1, 0, 2)


def _unheads(zh):
    return zh.transpose(1, 0, 2).reshape(zh.shape[1], -1)


def _pad_rows(a, n):
    return jnp.pad(a, ((0, 0), (n, n), (0, 0))) if a.ndim == 3 else jnp.pad(a, ((n, n), (0, 0)))


def _local_step(x, ctx, target, mods, wts, fetch, emit):
    t, d = x.shape
    l_ctx = ctx.shape[0]
    tc = t + l_ctx
    norm_g = wts["norm_g"]
    ng = lambda l, k: norm_g[l, k][None, :]
    grads = {}
    dmods = [[[None] * N_MOD for _ in range(2)] for _ in range(2)]
    dnorm = [[None] * 3 for _ in range(2)]

    def ffn_fwd(z, h, l, kbase, wi, wo, n_x, tag, nxt):
        au, act = _ffn_up(h, wi, 0, f"ffn_up_{tag}")
        wo = wo(act) if callable(wo) else wo
        outs = _mm_resid(act, wo, 0, z, mods[l], kbase + 2, 0.5, n_x, f"ffn_down_{tag}", nxt=nxt)
        return outs[0], (z, h, au, act, outs[1], wi, wo), (outs[2] if nxt is not None else None)

    def ffn_bwd(dz_new, dy, saved, l, kbase, g, n_x, tag, stage, split=False, then=None):
        z, h, au, act, y, wi, wo = saved
        dau = _ffn_down_bwd(dy, wo, 0, au, f"ffn_down_bwd_{tag}")
        dwo = _mm_tn(act, dy, BF16, f"dwo_{tag}")
        if split:
            token = emit(stage, [dwo])
            dwi_t = _mm_tn(dau, h, BF16, f"dwi_{tag}", dep=token)
            token = emit(stage + 1, [dwi_t])
        else:
            dwi_t = _mm_tn(dau, h, BF16, f"dwi_{tag}")
            token = emit(stage, [dwi_t, dwo])
        dh = _mm([(dau, wi, 0, 0)], NN, d, F32, f"dh_{tag}", tm_pref=TALL_TILES, dep=token)
        return _modulate_bwd(z, dh, dz_new, mods[l], g, kbase + 1, n_x, f"mod_bwd_{tag}", latent_only=split, then=then)

    def record(l, kbase, k_norm, g, acc_mod, acc_gate, streams):
        total = None
        for s in range(streams):
            dmods[l][s][kbase] = acc_mod[s, 0]
            dmods[l][s][kbase + 1] = acc_mod[s, 1] * g[0]
            if acc_gate is not None:
                dmods[l][s][kbase + 2] = acc_gate[s, 0]
            part = acc_mod[s, 1] * (1.0 + mods[l][s, kbase + 1])
            total = part if total is None else total + part
        dnorm[l][k_norm] = total

    xc0 = _stack_rows(x, ctx, "stack_tokens")
    wi1_0 = fetch(0, None)["wi1_0"]
    h0 = _modulate(xc0, mods[0], ng(0, 0), 0, 1, t, BF16, "mod_l0f1")
    xc1, sv_f1, hc = ffn_fwd(xc0, h0, 0, 0, wi1_0, lambda act: fetch(1, act)["wo1_0"], t, "l0f1",
                             (mods[0], ng(0, 1), 3, 4, BF16))
    w_in_t = fetch(2, hc)["w_in_t"]
    n_proj = w_in_t.shape[1]
    zall = _mm([(hc, w_in_t, 0, 0)], NT, n_proj, F32, "proj", tm_pref=TALL_TILES,
               tn_pref=(n_proj,))
    offs = np.cumsum((0,) + PROJ_SIZES)
    part = lambda i, rows=slice(None): zall[rows, offs[i]:offs[i + 1]]
    lat, con = slice(0, t), slice(t, tc)
    cos, sin = _rope_tables(t)
    qa = _heads(_rope(zall, cos, sin, False, "rope_q", view=(A_Q, int(offs[0]) // A_Q)), A_HEADS)
    ka = _heads(_rope(zall, cos, sin, False, "rope_k", view=(A_KV, int(offs[1]) // A_KV)), A_KV_HEADS)
    va = _heads(part(2, lat), A_KV_HEADS)
    kca, vca = _heads(part(1, con), A_KV_HEADS), _heads(part(2, con), A_KV_HEADS)
    kap, vap = _pad_rows(ka, WINDOW), _pad_rows(va, WINDOW)
    sink = wts["sink"].reshape(A_HEADS, 1, 1)
    o_a = _attn_fwd(qa, kap, vap, kca, vca, sink, "attn_fwd")

    vb = part(5)
    qk_cols = (int(offs[3]) // B_QK, int(offs[4]) // B_QK)
    rb = part(6, lat)
    zg = part(7)
    zg_f, zg_b = zg[:, :B_GATE_RANK], zg[:, B_GATE_RANK:]
    w2f, w2b, b2f, b2b = wts["w_a2_f"], wts["w_a2_b"], wts["b_a_f"], wts["b_a_b"]
    la_f = _gate_fwd(zg_f, w2f, b2f, "gate_f")
    la_b = _gate_fwd(zg_b, w2b, b2b, "gate_b")
    o_f, s_f, o_b, s_b = _gla_fwd(zall, zall, vb, la_f, la_b, t, "gla_fwd", qk_cols=qk_cols)
    gla_g = wts["gla_g"]
    go = _gla_out_fwd(o_f, o_b, rb, gla_g, "gla_out")
    cat = jnp.concatenate([_unheads(o_a), go], axis=-1).astype(BF16)
    big = fetch(3, cat)
    w_out, wi2_0, wo2_0 = big["w_out"], big["wi2_0"], big["wo2_0"]
    x2, y_mix0, h2 = _mm_resid(cat, w_out, 0, xc1, mods[0], 5, 1.0, t, "w_out", nxt=(mods[0], ng(0, 2), 6, 7, BF16))
    x3, sv_f2, h3 = ffn_fwd(x2, h2, 0, 6, wi2_0, wo2_0, t, "l0f2", (mods[1], ng(1, 0), 0, 1, BF16))

    big = fetch(4, x3)
    wi1_1, wo1_1, wi2_1, wo2_1 = big["wi1_1"], big["wo1_1"], big["wi2_1"], big["wo2_1"]
    x4, sv_g1, hp = ffn_fwd(x3, h3, 1, 0, wi1_1, wo1_1, t, "l1f1", (mods[1], ng(1, 1), 3, 4, F32))
    w_pool, pool_scale = big["w_pool"], wts["pool_scale"]
    x5, pooled, ypre = _pool_fwd(hp, w_pool, pool_scale, x4, mods[1], 5, "pool_fwd")
    h5 = _modulate(x5, mods[1], ng(1, 2), 6, 7, t, BF16, "mod_l1f2")
    x6, sv_g2, _ = ffn_fwd(x5, h5, 1, 6, wi2_1, wo2_1, t, "l1f2", None)

    y_of = lambda saved: saved[4]
    dx6, loss_vec, dfinal_g, dy, acc_gate = _final_loss(x6, wts["final_g"], target, (y_of(sv_g2), mods[1], 8, 0.5),
                                                        "final_loss")
    grads["final_g"] = dfinal_g[0]

    dx5, acc_mod = ffn_bwd(dx6, dy, sv_g2, 1, 6, ng(1, 2), t, "l1f2", 0)
    record(1, 6, 2, ng(1, 2), acc_mod, acc_gate, 1)
    dhp, dw_pool, acc_pool = _pool_bwd(dx5, w_pool, pool_scale, pooled, ypre, mods[1], 5, "pool_bwd")
    grads["pool_scale"] = acc_pool[0]
    dmods[1][0][5] = acc_pool[1]
    dx4, acc_mod, dy, acc_gate = _modulate_bwd(x4, dhp, dx5, mods[1], ng(1, 1), 4, t, "mod_bwd_l1mix",
                                               then=(y_of(sv_g1), mods[1], 2, 0.5))
    record(1, 3, 1, ng(1, 1), acc_mod, None, 1)
    dx3, acc_mod, dy, acc_gate_next = ffn_bwd(dx4, dy, sv_g1, 1, 0, ng(1, 0), t, "l1f1", 1,
                                              then=(y_of(sv_f2), mods[0], 8, 0.5))
    record(1, 0, 0, ng(1, 0), acc_mod, acc_gate, 1)

    dx2, acc_mod, dymix, acc_gate_mix = ffn_bwd(dx3, dy, sv_f2, 0, 6, ng(0, 2), t, "l0f2", 2,
                                                then=(y_mix0, mods[0], 5, 1.0))
    record(0, 6, 2, ng(0, 2), acc_mod, acc_gate_next, 1)
    dmods[0][0][5] = acc_gate_mix[0, 0]
    dw_out = _mm_tn(cat, dymix, BF16, "dw_out")
    dcat = _mm([(dymix, w_out, 0, 0)], NT, cat.shape[1], F32, "dcat")
    do_a = _heads(dcat[:, :A_Q], A_HEADS)
    do_full, drb, dgla_g = _gla_out_bwd(o_f, o_b, rb, gla_g, dcat[:, A_Q:], "gla_out_bwd")
    grads["gla_g"] = dgla_g[0]
    dq_f, dk_f, dv_f, dla_f, dq_b, dk_b, dv_b, dla_b = _gla_bwd(zall, zall, vb, la_f, la_b, s_f, s_b, do_full, t,
                                                                "gla_bwd", qk_cols=qk_cols)
    dzg_f, dw2f, db2f = _gate_bwd(zg_f, w2f, b2f, dla_f, "gate_bwd_f")
    dzg_b, dw2b, db2b = _gate_bwd(zg_b, w2b, b2b, dla_b, "gate_bwd_b")
    grads.update(w_a2_f=dw2f, w_a2_b=dw2b, b_a_f=db2f[0], b_a_b=db2b[0])
    dqa_r, dkap, dvap, dkca, dvca, dsink = _attn_bwd(qa, kap, vap, kca, vca, sink, o_a, do_a, "attn_bwd")
    grads["sink"] = dsink[:, 0, 0]
    dqa = _rope(_unheads(dqa_r), cos, sin, True, "rope_bwd_q")
    dka = _rope(_unheads(dkap[:, WINDOW:WINDOW + t]), cos, sin, True, "rope_bwd_k")
    dva = dvap[:, WINDOW:WINDOW + t]
    dzg = jnp.concatenate([dzg_f, dzg_b, jnp.zeros((tc, n_proj - PROJ_DIM), F32)], axis=-1)
    dzall = _assemble_dz(
        [dqa, dka, _unheads(dva), None, None, None, drb, None],
        [None, _unheads(dkca), _unheads(dvca), None, None, None, None, None],
        [None, None, None, [dq_f, dq_b], [dk_f, dk_b], [dv_f, dv_b], None, [dzg]], n_proj, "assemble_dz")
    dw_in_t = _mm_tn(dzall, hc, BF16, "dw_in")
    token = emit(3, [dw_in_t, dw_out, dw_pool])
    dhc = _mm([(dzall, w_in_t, 0, 0)], NN, d, F32, "dhc", tm_pref=TALL_TILES, dep=token)
    dxc1, acc_mod, dy, acc_gate = _modulate_bwd(xc1, dhc, dx2, mods[0], ng(0, 1), 4, t, "mod_bwd_l0mix",
                                                then=(y_of(sv_f1), mods[0], 2, 0.5))
    record(0, 3, 1, ng(0, 1), acc_mod, None, 2)
    dxc0, acc_mod = ffn_bwd(dxc1, dy, sv_f1, 0, 0, ng(0, 0), t, "l0f1", 4, split=True)
    record(0, 0, 0, ng(0, 0), acc_mod, acc_gate, 2)

    grads["norm_g"] = jnp.stack([jnp.stack(dnorm[0]), jnp.stack(dnorm[1])])
    zero = jnp.zeros((d,), F32)
    dmods_arr = jnp.stack([jnp.stack([jnp.stack([v if v is not None else zero for v in dmods[l][s]])
                                      for s in range(2)]) for l in range(2)])
    return loss_vec, dxc0, grads, dmods_arr


def _pack(parts):
    flat = jnp.concatenate([p.reshape(-1).astype(F32) for p in parts])
    pad = (-flat.shape[0]) % 128
    return jnp.pad(flat, (0, pad))[None, :]


def _unpack(rows, shapes):
    out, off = [], 0
    for s in shapes:
        n = int(np.prod(s))
        out.append(rows[:, off:off + n].reshape((rows.shape[0],) + tuple(s)))
        off += n
    return out


def _cols_to_full(g):
    g = jnp.moveaxis(g, 0, -2)
    return g.reshape(g.shape[:-2] + (-1,))


def kernel(x, c, ctx, c_ctx, w_mod, b_mod, norm_g, ffn1_wi, ffn1_wo, ffn2_wi, ffn2_wo, w_in, w_a2_f, b_a_f, w_a2_b, b_a_b, sink, gla_g, w_out, w_pool, pool_scale, final_g, loss_target, m_c_ctx, m_w_mod, m_b_mod, m_norm_g, m_ffn1_wi, m_ffn1_wo, m_ffn2_wi, m_ffn2_wo, m_w_in, m_w_a2_f, m_b_a_f, m_w_a2_b, m_b_a_b, m_sink, m_gla_g, m_w_out, m_w_pool, m_pool_scale, m_final_g, v_c_ctx, v_w_mod, v_b_mod, v_norm_g, v_ffn1_wi, v_ffn1_wo, v_ffn2_wi, v_ffn2_wo, v_w_in, v_w_a2_f, v_b_a_f, v_w_a2_b, v_b_a_b, v_sink, v_gla_g, v_w_out, v_w_pool, v_pool_scale, v_final_g):
    t, d = x.shape[1], x.shape[2]
    me = _dev_index()
    nc = w_mod.shape[2]
    ncol_in = w_in.shape[2]
    ncol_pad = -(-ncol_in // 16) * 16

    small_shapes = [(d,), norm_g.shape, pool_scale.shape, w_a2_f.shape, w_a2_b.shape]
    g1 = _gather_small(_pack([c, norm_g, pool_scale, w_a2_f, w_a2_b]), "gather_params")
    c_all, norm_g_all, pool_scale_all, w2f_all, w2b_all = _unpack(g1, small_shapes)
    wts = {
        "norm_g": _cols_to_full(norm_g_all),
        "pool_scale": _cols_to_full(pool_scale_all),
        "w_a2_f": _cols_to_full(w2f_all)[0],
        "w_a2_b": _cols_to_full(w2b_all)[0],
        "b_a_f": b_a_f, "b_a_b": b_a_b, "sink": sink[0], "gla_g": gla_g, "final_g": final_g[None, :],
    }

    craw = jnp.concatenate([c_all, c_ctx[None, :], jnp.zeros((16 - N_DEV - 1, d), F32)], axis=0)
    b_cols = lax.dynamic_slice_in_dim(b_mod, me * nc, nc, axis=1)[:, None, :]
    mm_cols = _adaln_fwd(craw, w_mod, b_cols, "adaln_fwd")
    n_cond = N_DEV + 1
    g2 = _gather_small(mm_cols[:, :n_cond].reshape(1, -1), "gather_mods").reshape(N_DEV, 2, n_cond, nc)
    mm_full = jnp.moveaxis(g2, 0, 2).reshape(2, n_cond, N_MOD, d)
    mods = jnp.stack([lax.dynamic_index_in_dim(mm_full, me, axis=1, keepdims=False), mm_full[:, N_DEV]], axis=1)
    mods = jnp.pad(mods, ((0, 0), (0, 0), (0, 16 - N_MOD), (0, 0)))

    tr = lambda w: jnp.swapaxes(w, 1, 2).astype(BF16)
    wi1_sh, wi2_sh, wo1_sh, wo2_sh = tr(ffn1_wi), tr(ffn2_wi), ffn1_wo.astype(BF16), ffn2_wo.astype(BF16)
    w_in_sh = jnp.pad(tr(w_in), ((0, 0), (0, ncol_pad - ncol_in), (0, 0)))
    groups = [
        {"wi1_0": wi1_sh[0:1]},
        {"wo1_0": wo1_sh[0:1]},
        {"w_in": w_in_sh},
        {"w_out": w_out.astype(BF16), "wi2_0": wi2_sh[0:1], "wo2_0": wo2_sh[0:1]},
        {"wi1_1": wi1_sh[1:2], "wo1_1": wo1_sh[1:2], "wi2_1": wi2_sh[1:2], "wo2_1": wo2_sh[1:2],
         "w_pool": w_pool[0].astype(BF16)},
    ]

    reach = lambda gi: NEAR_PEERS if gi == 0 else N_DEV - 1
    gathers, token = [], mods
    for gi, grp in enumerate(groups):
        lands = [_place_shard(s, me, f"gather_place_{nm}") for nm, s in grp.items()]
        gathers.append(_exchange_start(list(grp.values()), lands, True, 1 + gi, token, f"gather_start_{gi}",
                                       n_peers=reach(gi)))
        token = gathers[-1][4]
    n_proj = -(-(N_DEV * ncol_in) // 128) * 128
    forward_id = 1 + len(groups) + 6

    def fetch(gi, after):
        _, lands = _exchange_wait(gathers[gi], True, token if after is None else after, f"gather_wait_{gi}",
                                  n_peers=reach(gi))
        if gi == 0:
            rows = [s.shape[1] for s in groups[gi].values()]
            passed = _forward_start(lands, rows, forward_id, "gather_forward")
            lands = _forward_wait(passed, rows, passed[3], "gather_forward_wait")
        out = dict(zip(groups[gi].keys(), lands))
        if "w_in" in out:
            w_in_t = out.pop("w_in").reshape(1, N_DEV, ncol_pad, d)[:, :, :ncol_in].reshape(1, N_DEV * ncol_in, d)
            out["w_in_t"] = jnp.pad(w_in_t, ((0, 0), (0, n_proj - N_DEV * ncol_in), (0, 0)))
        return out

    scatters = []

    def emit(stage, arrays):
        if stage == 3:
            dw_in_t, dw_out, dw_pool = arrays
            dw_in_full = dw_in_t[:N_DEV * ncol_in].reshape(N_DEV, ncol_in, d)
            dw_in_full = jnp.pad(dw_in_full, ((0, 0), (0, ncol_pad - ncol_in), (0, 0)))
            srcs = [dw_in_full.reshape(1, N_DEV * ncol_pad, d), dw_out[None], dw_pool.astype(BF16)]
        else:
            srcs = [a[None] for a in arrays]
        lands = [lax.empty((N_DEV, s.shape[0], s.shape[1] // N_DEV, s.shape[2]), s.dtype) for s in srcs]
        scatters.append(_exchange_start(srcs, lands, False, 1 + len(groups) + stage, None, f"scatter_start_{stage}"))
        return scatters[-1][4]

    loss_vec, grad_x, grads, dmods = _local_step(x[0], ctx[0], loss_target[0], mods, wts, fetch, emit)

    def reduce_stage(stage, after):
        wholes, lands = _exchange_wait(scatters[stage], False, after, f"scatter_wait_{stage}")
        return [_sum_slots(ld, wh, me, f"sum_grad_{stage}_{i}") for i, (ld, wh) in enumerate(zip(lands, wholes))]

    (dwi2_1, dwo2_1), (dwi1_1, dwo1_1), (dwi2_0, dwo2_0), (dw_in_s, dw_out_s, dw_pool_s) = [
        reduce_stage(stage, grad_x) for stage in range(4)]
    back = lambda g: jnp.swapaxes(g, 1, 2)
    g_big = {
        "ffn2_wi": back(jnp.concatenate([dwi2_0, dwi2_1], axis=0)), "ffn2_wo": jnp.concatenate([dwo2_0, dwo2_1], axis=0),
        "w_in": back(dw_in_s[:, :ncol_in]), "w_out": dw_out_s, "w_pool": dw_pool_s[None],
    }

    order = ["c_ctx", "w_mod", "b_mod", "norm_g", "ffn1_wi", "ffn1_wo", "ffn2_wi", "ffn2_wo", "w_in", "w_a2_f", "b_a_f",
             "w_a2_b", "b_a_b", "sink", "gla_g", "w_out", "w_pool", "pool_scale", "final_g"]
    ws = dict(c_ctx=c_ctx, w_mod=w_mod, b_mod=b_mod, norm_g=norm_g, ffn1_wi=ffn1_wi, ffn1_wo=ffn1_wo, ffn2_wi=ffn2_wi,
              ffn2_wo=ffn2_wo, w_in=w_in, w_a2_f=w_a2_f, b_a_f=b_a_f, w_a2_b=w_a2_b, b_a_b=b_a_b, sink=sink, gla_g=gla_g,
              w_out=w_out, w_pool=w_pool, pool_scale=pool_scale, final_g=final_g)
    ms = dict(c_ctx=m_c_ctx, w_mod=m_w_mod, b_mod=m_b_mod, norm_g=m_norm_g, ffn1_wi=m_ffn1_wi, ffn1_wo=m_ffn1_wo,
              ffn2_wi=m_ffn2_wi, ffn2_wo=m_ffn2_wo, w_in=m_w_in, w_a2_f=m_w_a2_f, b_a_f=m_b_a_f, w_a2_b=m_w_a2_b,
              b_a_b=m_b_a_b, sink=m_sink, gla_g=m_gla_g, w_out=m_w_out, w_pool=m_w_pool, pool_scale=m_pool_scale,
              final_g=m_final_g)
    vs = dict(c_ctx=v_c_ctx, w_mod=v_w_mod, b_mod=v_b_mod, norm_g=v_norm_g, ffn1_wi=v_ffn1_wi, ffn1_wo=v_ffn1_wo,
              ffn2_wi=v_ffn2_wi, ffn2_wo=v_ffn2_wo, w_in=v_w_in, w_a2_f=v_w_a2_f, b_a_f=v_b_a_f, w_a2_b=v_w_a2_b,
              b_a_b=v_b_a_b, sink=v_sink, gla_g=v_gla_g, w_out=v_w_out, w_pool=v_w_pool, pool_scale=v_pool_scale,
              final_g=v_final_g)
    early, late = ["ffn2_wi", "ffn2_wo", "w_out", "w_in", "w_pool"], ["ffn1_wi", "ffn1_wo"]
    big = early + ["w_mod"] + late
    delta, new_m, new_v = {}, {}, {}
    g_all = dict(g_big)

    def adamw_big(nm):
        shp = ws[nm].shape
        two_d = lambda a: a.reshape(-1, shp[-1])
        dl, nm_, nv_ = _adamw(two_d(ws[nm]), two_d(g_all[nm]), two_d(ms[nm]), two_d(vs[nm]), f"adamw_{nm}")
        delta[nm], new_m[nm], new_v[nm] = dl.reshape(shp), nm_.reshape(shp), nv_.reshape(shp)

    for nm in early:
        adamw_big(nm)

    dm = dmods[:, :, :N_MOD].reshape(2, 2, N_MOD * d)
    small_g = [dm[:, 0], dm[0, 1], grads["norm_g"], grads["pool_scale"], grads["final_g"],
               grads["b_a_f"], grads["b_a_b"], grads["sink"], grads["gla_g"], grads["w_a2_f"], grads["w_a2_b"],
               loss_vec]
    small_g_shapes = [a.shape for a in small_g]
    g3 = _gather_small(_pack(small_g), "gather_small_grads", dep=delta["w_out"])
    total = _sum_rows8(g3, "sum_small_grads")
    dmm_all = _unpack(g3, small_g_shapes[:1])[0]
    (dmm_sum, dmc_sum, dnorm_g, dpool_scale, dfinal_g, db_a_f, db_a_b, dsink, dgla_g, dw_a2_f, dw_a2_b, loss_all) = [
        a[0] for a in _unpack(total, small_g_shapes)]
    loss = jnp.sum(loss_all)
    dmc_rows = jnp.stack([dmc_sum, jnp.zeros_like(dmc_sum)])
    dmm_rows = jnp.concatenate([dmm_all.transpose(1, 0, 2), dmc_rows[:, None, :],
                                jnp.zeros((2, 16 - N_DEV - 1, N_MOD * d), F32)], axis=1)
    grad_b_mod = dmm_sum + dmc_rows
    dmm_cols = lax.dynamic_slice_in_dim(dmm_rows, me * nc, nc, axis=2)
    cs_t = jnp.transpose(_silu(craw)).astype(BF16)
    grad_w_mod, dcraw = _adaln_bwd(craw, cs_t, dmm_cols, w_mod, "adaln_bwd")
    g4 = _gather_small((dcraw[0, N_DEV] + dcraw[1, N_DEV])[None, :], "gather_c_ctx_grad")
    grad_c_ctx = _sum_rows8(g4, "sum_c_ctx_grad")[0]

    col = lambda v, n: lax.dynamic_slice_in_dim(v, me * n, n, axis=v.ndim - 1)
    g_small = {
        "c_ctx": grad_c_ctx, "b_mod": grad_b_mod, "norm_g": col(dnorm_g, norm_g.shape[2]),
        "w_a2_f": col(dw_a2_f, w_a2_f.shape[2])[None], "b_a_f": db_a_f[None], "w_a2_b": col(dw_a2_b, w_a2_b.shape[2])[None],
        "b_a_b": db_a_b[None], "sink": dsink[None], "gla_g": dgla_g[None], "pool_scale": col(dpool_scale, pool_scale.shape[1])[None],
        "final_g": dfinal_g,
    }
    g_all.update(g_small, w_mod=grad_w_mod)
    adamw_big("w_mod")
    rest = [nm for nm in order if nm not in big]
    rest_shapes = [ws[nm].shape for nm in rest]
    packed = [_pack([d_[nm].reshape(ws[nm].shape) for nm in rest]).reshape(-1, 128) for d_ in (ws, g_all, ms, vs)]
    pad_rows = (-packed[0].shape[0]) % 512
    packed = [jnp.pad(p, ((0, pad_rows), (0, 0))) for p in packed]
    outs = _adamw(*packed, "adamw_small")
    for dst, arr in zip((delta, new_m, new_v), outs):
        for nm, val in zip(rest, _unpack(arr.reshape(1, -1), rest_shapes)):
            dst[nm] = val[0]

    (dwo1_0,), (dwi1_0,) = reduce_stage(4, outs[0]), reduce_stage(5, outs[0])
    g_all["ffn1_wi"] = back(jnp.concatenate([dwi1_0, dwi1_1], axis=0))
    g_all["ffn1_wo"] = jnp.concatenate([dwo1_0, dwo1_1], axis=0)
    for nm in late:
        adamw_big(nm)
    g_all = {nm: g_all[nm].reshape(ws[nm].shape) for nm in order}

    return (loss, grad_x[None], *[g_all[nm] for nm in order], *[delta[nm] for nm in order],
            *[new_m[nm] for nm in order], *[new_v[nm] for nm in order])
```
